```python
import math
import jax
import jax.numpy as jnp
from jax import lax

D_MODEL = 1024
BATCH = 8
SEQ = 16384
DEPTH = 4

CHUNK = 64
NORM_EPS = 1e-6
CONV_K = 4
D_FF = 2816
N_MOD = 9

GDN_HEADS = 4
GDN_DK = 128
GDN_DV = 128
GDN_WIDTH = GDN_HEADS * GDN_DV
GDN_CONV_CH = 2 * GDN_HEADS * GDN_DK + GDN_WIDTH

MLA_HEADS = 4
MLA_Q_RANK = 384
MLA_KV_RANK = 256
MLA_NOPE = 128
MLA_ROPE = 64
MLA_DV = 128
MLA_WIDTH = MLA_HEADS * MLA_DV
ROPE_THETA = 10000.0
Q_BLOCK = 128

MIX_WIDTH = GDN_WIDTH + MLA_WIDTH
EVEN_SPLITS = (GDN_CONV_CH, GDN_WIDTH, GDN_HEADS, GDN_HEADS, MLA_Q_RANK, MLA_KV_RANK, MLA_ROPE)
EVEN_IN = GDN_CONV_CH + GDN_WIDTH + 2 * GDN_HEADS + MLA_Q_RANK + MLA_KV_RANK + MLA_ROPE

SSD_D_INNER = 2 * D_MODEL
SSD_HEADDIM = 64
SSD_HEADS = SSD_D_INNER // SSD_HEADDIM
SSD_GROUPS = 4
SSD_STATE = 128
SSD_CONV_CH = SSD_D_INNER + 2 * SSD_GROUPS * SSD_STATE
ODD_SPLITS = (SSD_D_INNER, SSD_CONV_CH, SSD_HEADS)
ODD_IN = SSD_D_INNER + SSD_CONV_CH + SSD_HEADS

N_EVEN = (DEPTH + 1) // 2
N_ODD = DEPTH // 2

kernel_name = "hybrid_gdn_mla_mamba2_macaron_adaln"


def split_cols(t, sizes):
    out, start = [], 0
    for n in sizes:
        out.append(t[..., start:start + n])
        start += n
    return out


def rmsnorm(x, g):
    xf = x.astype(jnp.float32)
    y = xf * lax.rsqrt(jnp.mean(xf * xf, axis=-1, keepdims=True) + NORM_EPS)
    return (y * g.astype(jnp.float32)).astype(x.dtype)


def l2norm(x):
    xf = x.astype(jnp.float32)
    return xf * lax.rsqrt(jnp.sum(xf * xf, axis=-1, keepdims=True) + NORM_EPS)


def adaln_norm(x, g, shift, scale):
    return rmsnorm(x, g) * (1.0 + scale[:, None, :]) + shift[:, None, :]


def swiglu(h, w1, w3, w2):
    return (jax.nn.silu(h @ w1) * (h @ w3)) @ w2


def causal_conv(x, w, b=None):
    k_w = w.shape[0]
    s = x.shape[1]
    xp = jnp.pad(x, ((0, 0), (k_w - 1, 0), (0, 0)))
    y = sum(w[j] * xp[:, j:j + s] for j in range(k_w))
    return y if b is None else y + b


def rope_tables(positions, dim):
    half = dim // 2
    inv_freq = ROPE_THETA ** (-jnp.arange(half, dtype=jnp.float32) / half)
    ang = positions.astype(jnp.float32)[..., None] * inv_freq
    return jnp.cos(ang), jnp.sin(ang)


def apply_rope(x, cos, sin):
    half = x.shape[-1] // 2
    x1, x2 = x[..., :half], x[..., half:]
    return jnp.concatenate([x1 * cos - x2 * sin, x2 * cos + x1 * sin], axis=-1).astype(x.dtype)


def gated_delta_rule(q, k, v, beta_logit, a, a_log, dt_bias):
    b_, s, h, _ = q.shape
    nc = s // CHUNK
    f32 = jnp.float32
    q = l2norm(q) * (GDN_DK ** -0.5)
    k = l2norm(k)
    v = v.astype(f32)
    beta = jax.nn.sigmoid(beta_logit.astype(f32))
    g = -jnp.exp(a_log.astype(f32)) * jax.nn.softplus(a.astype(f32) + dt_bias.astype(f32))

    def to_chunks(t):
        return jnp.swapaxes(t.reshape(b_, nc, CHUNK, h, *t.shape[3:]), 2, 3)

    qc, kc, vc, bc = to_chunks(q), to_chunks(k), to_chunks(v), to_chunks(beta)
    gc = jnp.cumsum(to_chunks(g), axis=-1)
    lower = jnp.tril(jnp.ones((CHUNK, CHUNK), dtype=bool))
    strict = jnp.tril(jnp.ones((CHUNK, CHUNK), dtype=bool), -1)
    decay = jnp.exp(jnp.where(lower, gc[..., :, None] - gc[..., None, :], -jnp.inf))
    kb = kc * bc[..., None]
    a_strict = jnp.where(strict, jnp.einsum("bnhid,bnhjd->bnhij", kb, kc) * decay, 0.0)
    t_mat = a_strict + jnp.eye(CHUNK, dtype=f32)
    rhs = jnp.concatenate([vc * bc[..., None], kb * jnp.exp(gc)[..., None]], axis=-1)
    sol = lax.linalg.triangular_solve(t_mat, rhs, left_side=True, lower=True, unit_diagonal=True)
    u, w = sol[..., :GDN_DV], sol[..., GDN_DV:]
    attn = jnp.einsum("bnhid,bnhjd->bnhij", qc, kc) * decay
    g_last = gc[..., -1]
    k_end = kc * jnp.exp(g_last[..., None] - gc)[..., None]
    q_start = qc * jnp.exp(gc)[..., None]

    def step(state, inp):
        u_i, w_i, q_i, k_i, a_i, gl_i = inp
        v_new = u_i - jnp.einsum("bhck,bhkv->bhcv", w_i, state)
        o_i = jnp.einsum("bhck,bhkv->bhcv", q_i, state) + jnp.einsum("bhij,bhjv->bhiv", a_i, v_new)
        state = state * jnp.exp(gl_i)[..., None, None] + jnp.einsum("bhck,bhcv->bhkv", k_i, v_new)
        return state, o_i

    s0 = jnp.zeros((b_, h, GDN_DK, GDN_DV), f32)
    xs = (jnp.moveaxis(u, 1, 0), jnp.moveaxis(w, 1, 0), jnp.moveaxis(q_start, 1, 0),
          jnp.moveaxis(k_end, 1, 0), jnp.moveaxis(attn, 1, 0), jnp.moveaxis(g_last, 1, 0))
    _, o = lax.scan(step, s0, xs)
    return jnp.transpose(o, (1, 0, 3, 2, 4)).reshape(b_, s, h, GDN_DV)


def latent_attention(c_q, c_kv, k_rope, positions, q_norm_g, w_uq, kv_norm_g, w_ukv):
    b_, s, _ = c_q.shape
    q = jnp.einsum("bsr,rhd->bshd", rmsnorm(c_q, q_norm_g), w_uq)
    q_nope, q_pe = q[..., :MLA_NOPE], q[..., MLA_NOPE:]
    kv = jnp.einsum("bsr,rhd->bshd", rmsnorm(c_kv, kv_norm_g), w_ukv)
    k_nope, v = kv[..., :MLA_NOPE], kv[..., MLA_NOPE:]
    cos, sin = rope_tables(positions, MLA_ROPE)
    q_pe = apply_rope(q_pe, cos[:, :, None, :], sin[:, :, None, :])
    k_pe = apply_rope(k_rope, cos, sin)
    scale = (MLA_NOPE + MLA_ROPE) ** -0.5
    key_chunk = jnp.arange(s) // CHUNK

    def query_block(i):
        start = i * Q_BLOCK
        qn = lax.dynamic_slice_in_dim(q_nope, start, Q_BLOCK, axis=1)
        qp = lax.dynamic_slice_in_dim(q_pe, start, Q_BLOCK, axis=1)
        sc = jnp.einsum("bqhd,bkhd->bhqk", qn, k_nope) + jnp.einsum("bqhd,bkd->bhqk", qp, k_pe)
        sc = sc.astype(jnp.float32) * scale
        q_chunk = (start + jnp.arange(Q_BLOCK)) // CHUNK
        mask = key_chunk[None, :] <= q_chunk[:, None]
        p = jax.nn.softmax(jnp.where(mask, sc, -jnp.inf), axis=-1).astype(v.dtype)
        return jnp.einsum("bhqk,bkhd->bqhd", p, v)

    out = lax.map(query_block, jnp.arange(s // Q_BLOCK))
    return jnp.swapaxes(out, 0, 1).reshape(b_, s, MLA_WIDTH)


def ssd_chunked(x, dt_raw, bm, cm, a_log, dt_bias, d_skip):
    b_, s, _ = x.shape
    nc = s // CHUNK
    r = SSD_HEADS // SSD_GROUPS
    f32 = jnp.float32
    xh = x.astype(f32).reshape(b_, s, SSD_HEADS, SSD_HEADDIM)
    dt = jax.nn.softplus(dt_raw.astype(f32) + dt_bias.astype(f32))
    da = dt * (-jnp.exp(a_log.astype(f32)))
    xc = (xh * dt[..., None]).reshape(b_, nc, CHUNK, SSD_GROUPS, r, SSD_HEADDIM)
    bc = bm.astype(f32).reshape(b_, nc, CHUNK, SSD_GROUPS, SSD_STATE)
    cc = cm.astype(f32).reshape(b_, nc, CHUNK, SSD_GROUPS, SSD_STATE)
    acs = jnp.cumsum(da.reshape(b_, nc, CHUNK, SSD_GROUPS, r), axis=2)
    acs = jnp.transpose(acs, (0, 1, 3, 4, 2))
    lower = jnp.tril(jnp.ones((CHUNK, CHUNK), dtype=bool))
    lmat = jnp.exp(jnp.where(lower, acs[..., :, None] - acs[..., None, :], -jnp.inf))
    cb = jnp.einsum("bclgn,bcsgn->bcgls", cc, bc)
    y_diag = jnp.einsum("bcgls,bcgrls,bcsgrp->bclgrp", cb, lmat, xc)
    d_start = jnp.exp(acs)
    d_end = jnp.exp(acs[..., -1:] - acs)
    c_decay = jnp.exp(acs[..., -1])

    def step(hs, inp):
        c_i, b_i, x_i, ds_i, de_i, cd_i = inp
        y_off = jnp.einsum("blgn,bgrpn,bgrl->blgrp", c_i, hs, ds_i)
        hs = hs * cd_i[..., None, None] + jnp.einsum("blgn,bgrl,blgrp->bgrpn", b_i, de_i, x_i)
        return hs, y_off

    h0 = jnp.zeros((b_, SSD_GROUPS, r, SSD_HEADDIM, SSD_STATE), f32)
    xs = (jnp.moveaxis(cc, 1, 0), jnp.moveaxis(bc, 1, 0), jnp.moveaxis(xc, 1, 0),
          jnp.moveaxis(d_start, 1, 0), jnp.moveaxis(d_end, 1, 0), jnp.moveaxis(c_decay, 1, 0))
    _, y_off = lax.scan(step, h0, xs)
    y = (y_diag + jnp.moveaxis(y_off, 0, 1)).reshape(b_, s, SSD_HEADS, SSD_HEADDIM)
    y = y + d_skip.astype(f32)[:, None] * xh
    return y.reshape(b_, s, SSD_D_INNER)


def even_mixer(h, positions, w_in, conv_w, a_log, dt_bias, gdn_g,
               q_norm_g, w_uq, kv_norm_g, w_ukv, w_out):
    b_, s, _ = h.shape
    qkv, z, beta_logit, a, c_q, c_kv, k_rope = split_cols(h @ w_in, EVEN_SPLITS)
    qkv = jax.nn.silu(causal_conv(qkv, conv_w))
    q, k, v = split_cols(qkv, (GDN_HEADS * GDN_DK, GDN_HEADS * GDN_DK, GDN_WIDTH))
    o_a = gated_delta_rule(q.reshape(b_, s, GDN_HEADS, GDN_DK), k.reshape(b_, s, GDN_HEADS, GDN_DK),
                           v.reshape(b_, s, GDN_HEADS, GDN_DV), beta_logit, a, a_log, dt_bias)
    o_a = rmsnorm(o_a, gdn_g) * jax.nn.silu(z.astype(jnp.float32).reshape(b_, s, GDN_HEADS, GDN_DV))
    o_b = latent_attention(c_q, c_kv, k_rope, positions, q_norm_g, w_uq, kv_norm_g, w_ukv)
    o = jnp.concatenate([o_a.reshape(b_, s, GDN_WIDTH).astype(h.dtype), o_b.astype(h.dtype)], axis=-1)
    return o @ w_out


def odd_mixer(h, w_in, conv_w, conv_b, a_log, dt_bias, d_skip, norm_g, w_out):
    b_, s, _ = h.shape
    z, xbc, dt_raw = split_cols(h @ w_in, ODD_SPLITS)
    xbc = jax.nn.silu(causal_conv(xbc, conv_w, conv_b))
    xs, bm, cm = split_cols(xbc, (SSD_D_INNER, SSD_GROUPS * SSD_STATE, SSD_GROUPS * SSD_STATE))
    y = ssd_chunked(xs, dt_raw, bm, cm, a_log, dt_bias, d_skip)
    y = y * jax.nn.silu(z.astype(jnp.float32))
    yg = y.reshape(b_, s, SSD_GROUPS, SSD_D_INNER // SSD_GROUPS)
    yg = yg * lax.rsqrt(jnp.mean(yg * yg, axis=-1, keepdims=True) + NORM_EPS)
    y = (yg.reshape(b_, s, SSD_D_INNER) * norm_g.astype(jnp.float32)).astype(h.dtype)
    return y @ w_out


def _fwd_setup_inputs(seed: int = 0) -> dict:
    key = jax.random.key(seed)
    ks = jax.random.split(key, 27)
    f32 = jnp.float32

    def nrm(k, shape, scale):
        return jax.random.normal(k, shape, f32) * scale

    def gain(k, shape):
        return 1.0 + 0.05 * jax.random.normal(k, shape, f32)

    def a_log_init(k, shape):
        return jnp.log(jax.random.uniform(k, shape, f32, 1.0, 16.0))

    def dt_bias_init(k, shape):
        dt = jnp.exp(jax.random.uniform(k, shape, f32, math.log(1e-3), math.log(1e-1)))
        return dt + jnp.log(-jnp.expm1(-dt))

    positions = jnp.broadcast_to(jnp.arange(SEQ, dtype=jnp.int32), (BATCH, SEQ))
    return {
        "x": nrm(ks[0], (BATCH, SEQ, D_MODEL), 1.0),
        "c": nrm(ks[1], (BATCH, D_MODEL), 1.0),
        "positions": positions,
        "ada_w": nrm(ks[2], (DEPTH, D_MODEL, N_MOD * D_MODEL), 0.5 * D_MODEL ** -0.5),
        "ada_b": nrm(ks[3], (DEPTH, N_MOD * D_MODEL), 0.02),
        "norm_g": gain(ks[4], (DEPTH, 3, D_MODEL)),
        "ffn_w1": nrm(ks[5], (DEPTH, 2, D_MODEL, D_FF), D_MODEL ** -0.5),
        "ffn_w3": nrm(ks[6], (DEPTH, 2, D_MODEL, D_FF), D_MODEL ** -0.5),
        "ffn_w2": nrm(ks[7], (DEPTH, 2, D_FF, D_MODEL), D_FF ** -0.5),
        "ev_w_in": nrm(ks[8], (N_EVEN, D_MODEL, EVEN_IN), D_MODEL ** -0.5),
        "gdn_conv_w": nrm(ks[9], (N_EVEN, CONV_K, GDN_CONV_CH), CONV_K ** -0.5),
        "gdn_A_log": a_log_init(ks[10], (N_EVEN, GDN_HEADS)),
        "gdn_dt_bias": dt_bias_init(ks[11], (N_EVEN, GDN_HEADS)),
        "gdn_norm_g": gain(ks[12], (N_EVEN, GDN_DV)),
        "mla_q_norm_g": gain(ks[13], (N_EVEN, MLA_Q_RANK)),
        "mla_w_uq": nrm(ks[14], (N_EVEN, MLA_Q_RANK, MLA_HEADS, MLA_NOPE + MLA_ROPE), MLA_Q_RANK ** -0.5),
        "mla_kv_norm_g": gain(ks[15], (N_EVEN, MLA_KV_RANK)),
        "mla_w_ukv": nrm(ks[16], (N_EVEN, MLA_KV_RANK, MLA_HEADS, MLA_NOPE + MLA_DV), MLA_KV_RANK ** -0.5),
        "ev_w_out": nrm(ks[17], (N_EVEN, MIX_WIDTH, D_MODEL), MIX_WIDTH ** -0.5),
        "ssd_w_in": nrm(ks[18], (N_ODD, D_MODEL, ODD_IN), D_MODEL ** -0.5),
        "ssd_conv_w": nrm(ks[19], (N_ODD, CONV_K, SSD_CONV_CH), CONV_K ** -0.5),
        "ssd_conv_b": nrm(ks[20], (N_ODD, SSD_CONV_CH), 0.02),
        "ssd_A_log": a_log_init(ks[21], (N_ODD, SSD_HEADS)),
        "ssd_dt_bias": dt_bias_init(ks[22], (N_ODD, SSD_HEADS)),
        "ssd_D": gain(ks[23], (N_ODD, SSD_HEADS)),
        "ssd_norm_g": gain(ks[24], (N_ODD, SSD_D_INNER)),
        "ssd_w_out": nrm(ks[25], (N_ODD, SSD_D_INNER, D_MODEL), SSD_D_INNER ** -0.5),
        "final_g": gain(ks[26], (D_MODEL,)),
    }


def _fwd_reference(x, c, positions, ada_w, ada_b, norm_g, ffn_w1, ffn_w3, ffn_w2,
              ev_w_in, gdn_conv_w, gdn_A_log, gdn_dt_bias, gdn_norm_g,
              mla_q_norm_g, mla_w_uq, mla_kv_norm_g, mla_w_ukv, ev_w_out,
              ssd_w_in, ssd_conv_w, ssd_conv_b, ssd_A_log, ssd_dt_bias, ssd_D, ssd_norm_g, ssd_w_out,
              final_g):
    b_ = x.shape[0]
    c_act = jax.nn.silu(c)
    for l in range(DEPTH):
        mod = (c_act @ ada_w[l] + ada_b[l]).reshape(b_, 3, 3, D_MODEL)
        shift, scale, gate = mod[:, :, 0], mod[:, :, 1], mod[:, :, 2]
        h = adaln_norm(x, norm_g[l, 0], shift[:, 0], scale[:, 0])
        x = x + 0.5 * gate[:, 0, None] * swiglu(h, ffn_w1[l, 0], ffn_w3[l, 0], ffn_w2[l, 0])
        h = adaln_norm(x, norm_g[l, 1], shift[:, 1], scale[:, 1])
        if l % 2 == 0:
            e = l // 2
            y = even_mixer(h, positions, ev_w_in[e], gdn_conv_w[e], gdn_A_log[e], gdn_dt_bias[e],
                           gdn_norm_g[e], mla_q_norm_g[e], mla_w_uq[e], mla_kv_norm_g[e],
                           mla_w_ukv[e], ev_w_out[e])
        else:
            o = l // 2
            y = odd_mixer(h, ssd_w_in[o], ssd_conv_w[o], ssd_conv_b[o], ssd_A_log[o],
                          ssd_dt_bias[o], ssd_D[o], ssd_norm_g[o], ssd_w_out[o])
        x = x + gate[:, 1, None] * y
        h = adaln_norm(x, norm_g[l, 2], shift[:, 2], scale[:, 2])
        x = x + 0.5 * gate[:, 2, None] * swiglu(h, ffn_w1[l, 1], ffn_w3[l, 1], ffn_w2[l, 1])
    return rmsnorm(x, final_g)


import jax as _jax
import jax.numpy as _jnp

TWIN_FORMAT = 'train_step'
FWD_PARAMS = ['x', 'c', 'positions', 'ada_w', 'ada_b', 'norm_g', 'ffn_w1', 'ffn_w3', 'ffn_w2', 'ev_w_in', 'gdn_conv_w', 'gdn_A_log', 'gdn_dt_bias', 'gdn_norm_g', 'mla_q_norm_g', 'mla_w_uq', 'mla_kv_norm_g', 'mla_w_ukv', 'ev_w_out', 'ssd_w_in', 'ssd_conv_w', 'ssd_conv_b', 'ssd_A_log', 'ssd_dt_bias', 'ssd_D', 'ssd_norm_g', 'ssd_w_out', 'final_g']
TWIN_WEIGHTS = ['ada_w', 'ada_b', 'norm_g', 'ffn_w1', 'ffn_w3', 'ffn_w2', 'ev_w_in', 'gdn_conv_w', 'gdn_A_log', 'gdn_dt_bias', 'gdn_norm_g', 'mla_q_norm_g', 'mla_w_uq', 'mla_kv_norm_g', 'mla_w_ukv', 'ev_w_out', 'ssd_w_in', 'ssd_conv_w', 'ssd_conv_b', 'ssd_A_log', 'ssd_dt_bias', 'ssd_D', 'ssd_norm_g', 'ssd_w_out', 'final_g']
TWIN_DIFF_INPUT = 'x'
TWIN_INPUTS = ['x', 'c', 'positions', 'ada_w', 'ada_b', 'norm_g', 'ffn_w1', 'ffn_w3', 'ffn_w2', 'ev_w_in', 'gdn_conv_w', 'gdn_A_log', 'gdn_dt_bias', 'gdn_norm_g', 'mla_q_norm_g', 'mla_w_uq', 'mla_kv_norm_g', 'mla_w_ukv', 'ev_w_out', 'ssd_w_in', 'ssd_conv_w', 'ssd_conv_b', 'ssd_A_log', 'ssd_dt_bias', 'ssd_D', 'ssd_norm_g', 'ssd_w_out', 'final_g', 'loss_target', 'm_ada_w', 'm_ada_b', 'm_norm_g', 'm_ffn_w1', 'm_ffn_w3', 'm_ffn_w2', 'm_ev_w_in', 'm_gdn_conv_w', 'm_gdn_A_log', 'm_gdn_dt_bias', 'm_gdn_norm_g', 'm_mla_q_norm_g', 'm_mla_w_uq', 'm_mla_kv_norm_g', 'm_mla_w_ukv', 'm_ev_w_out', 'm_ssd_w_in', 'm_ssd_conv_w', 'm_ssd_conv_b', 'm_ssd_A_log', 'm_ssd_dt_bias', 'm_ssd_D', 'm_ssd_norm_g', 'm_ssd_w_out', 'm_final_g', 'v_ada_w', 'v_ada_b', 'v_norm_g', 'v_ffn_w1', 'v_ffn_w3', 'v_ffn_w2', 'v_ev_w_in', 'v_gdn_conv_w', 'v_gdn_A_log', 'v_gdn_dt_bias', 'v_gdn_norm_g', 'v_mla_q_norm_g', 'v_mla_w_uq', 'v_mla_kv_norm_g', 'v_mla_w_ukv', 'v_ev_w_out', 'v_ssd_w_in', 'v_ssd_conv_w', 'v_ssd_conv_b', 'v_ssd_A_log', 'v_ssd_dt_bias', 'v_ssd_D', 'v_ssd_norm_g', 'v_ssd_w_out', 'v_final_g']
TWIN_OUTPUTS = ['loss', 'grad_x', 'grad_ada_w', 'grad_ada_b', 'grad_norm_g', 'grad_ffn_w1', 'grad_ffn_w3', 'grad_ffn_w2', 'grad_ev_w_in', 'grad_gdn_conv_w', 'grad_gdn_A_log', 'grad_gdn_dt_bias', 'grad_gdn_norm_g', 'grad_mla_q_norm_g', 'grad_mla_w_uq', 'grad_mla_kv_norm_g', 'grad_mla_w_ukv', 'grad_ev_w_out', 'grad_ssd_w_in', 'grad_ssd_conv_w', 'grad_ssd_conv_b', 'grad_ssd_A_log', 'grad_ssd_dt_bias', 'grad_ssd_D', 'grad_ssd_norm_g', 'grad_ssd_w_out', 'grad_final_g', 'delta_ada_w', 'delta_ada_b', 'delta_norm_g', 'delta_ffn_w1', 'delta_ffn_w3', 'delta_ffn_w2', 'delta_ev_w_in', 'delta_gdn_conv_w', 'delta_gdn_A_log', 'delta_gdn_dt_bias', 'delta_gdn_norm_g', 'delta_mla_q_norm_g', 'delta_mla_w_uq', 'delta_mla_kv_norm_g', 'delta_mla_w_ukv', 'delta_ev_w_out', 'delta_ssd_w_in', 'delta_ssd_conv_w', 'delta_ssd_conv_b', 'delta_ssd_A_log', 'delta_ssd_dt_bias', 'delta_ssd_D', 'delta_ssd_norm_g', 'delta_ssd_w_out', 'delta_final_g', 'new_m_ada_w', 'new_m_ada_b', 'new_m_norm_g', 'new_m_ffn_w1', 'new_m_ffn_w3', 'new_m_ffn_w2', 'new_m_ev_w_in', 'new_m_gdn_conv_w', 'new_m_gdn_A_log', 'new_m_gdn_dt_bias', 'new_m_gdn_norm_g', 'new_m_mla_q_norm_g', 'new_m_mla_w_uq', 'new_m_mla_kv_norm_g', 'new_m_mla_w_ukv', 'new_m_ev_w_out', 'new_m_ssd_w_in', 'new_m_ssd_conv_w', 'new_m_ssd_conv_b', 'new_m_ssd_A_log', 'new_m_ssd_dt_bias', 'new_m_ssd_D', 'new_m_ssd_norm_g', 'new_m_ssd_w_out', 'new_m_final_g', 'new_v_ada_w', 'new_v_ada_b', 'new_v_norm_g', 'new_v_ffn_w1', 'new_v_ffn_w3', 'new_v_ffn_w2', 'new_v_ev_w_in', 'new_v_gdn_conv_w', 'new_v_gdn_A_log', 'new_v_gdn_dt_bias', 'new_v_gdn_norm_g', 'new_v_mla_q_norm_g', 'new_v_mla_w_uq', 'new_v_mla_kv_norm_g', 'new_v_mla_w_ukv', 'new_v_ev_w_out', 'new_v_ssd_w_in', 'new_v_ssd_conv_w', 'new_v_ssd_conv_b', 'new_v_ssd_A_log', 'new_v_ssd_dt_bias', 'new_v_ssd_D', 'new_v_ssd_norm_g', 'new_v_ssd_w_out', 'new_v_final_g']
TWIN_LEAF_KINDS = {'loss': 'loss', 'grad_x': 'grad_x', 'grad_ada_w': 'grad_w', 'grad_ada_b': 'grad_w', 'grad_norm_g': 'grad_w', 'grad_ffn_w1': 'grad_w', 'grad_ffn_w3': 'grad_w', 'grad_ffn_w2': 'grad_w', 'grad_ev_w_in': 'grad_w', 'grad_gdn_conv_w': 'grad_w', 'grad_gdn_A_log': 'grad_w', 'grad_gdn_dt_bias': 'grad_w', 'grad_gdn_norm_g': 'grad_w', 'grad_mla_q_norm_g': 'grad_w', 'grad_mla_w_uq': 'grad_w', 'grad_mla_kv_norm_g': 'grad_w', 'grad_mla_w_ukv': 'grad_w', 'grad_ev_w_out': 'grad_w', 'grad_ssd_w_in': 'grad_w', 'grad_ssd_conv_w': 'grad_w', 'grad_ssd_conv_b': 'grad_w', 'grad_ssd_A_log': 'grad_w', 'grad_ssd_dt_bias': 'grad_w', 'grad_ssd_D': 'grad_w', 'grad_ssd_norm_g': 'grad_w', 'grad_ssd_w_out': 'grad_w', 'grad_final_g': 'grad_w', 'delta_ada_w': 'delta_w', 'delta_ada_b': 'delta_w', 'delta_norm_g': 'delta_w', 'delta_ffn_w1': 'delta_w', 'delta_ffn_w3': 'delta_w', 'delta_ffn_w2': 'delta_w', 'delta_ev_w_in': 'delta_w', 'delta_gdn_conv_w': 'delta_w', 'delta_gdn_A_log': 'delta_w', 'delta_gdn_dt_bias': 'delta_w', 'delta_gdn_norm_g': 'delta_w', 'delta_mla_q_norm_g': 'delta_w', 'delta_mla_w_uq': 'delta_w', 'delta_mla_kv_norm_g': 'delta_w', 'delta_mla_w_ukv': 'delta_w', 'delta_ev_w_out': 'delta_w', 'delta_ssd_w_in': 'delta_w', 'delta_ssd_conv_w': 'delta_w', 'delta_ssd_conv_b': 'delta_w', 'delta_ssd_A_log': 'delta_w', 'delta_ssd_dt_bias': 'delta_w', 'delta_ssd_D': 'delta_w', 'delta_ssd_norm_g': 'delta_w', 'delta_ssd_w_out': 'delta_w', 'delta_final_g': 'delta_w', 'new_m_ada_w': 'new_m', 'new_m_ada_b': 'new_m', 'new_m_norm_g': 'new_m', 'new_m_ffn_w1': 'new_m', 'new_m_ffn_w3': 'new_m', 'new_m_ffn_w2': 'new_m', 'new_m_ev_w_in': 'new_m', 'new_m_gdn_conv_w': 'new_m', 'new_m_gdn_A_log': 'new_m', 'new_m_gdn_dt_bias': 'new_m', 'new_m_gdn_norm_g': 'new_m', 'new_m_mla_q_norm_g': 'new_m', 'new_m_mla_w_uq': 'new_m', 'new_m_mla_kv_norm_g': 'new_m', 'new_m_mla_w_ukv': 'new_m', 'new_m_ev_w_out': 'new_m', 'new_m_ssd_w_in': 'new_m', 'new_m_ssd_conv_w': 'new_m', 'new_m_ssd_conv_b': 'new_m', 'new_m_ssd_A_log': 'new_m', 'new_m_ssd_dt_bias': 'new_m', 'new_m_ssd_D': 'new_m', 'new_m_ssd_norm_g': 'new_m', 'new_m_ssd_w_out': 'new_m', 'new_m_final_g': 'new_m', 'new_v_ada_w': 'new_v', 'new_v_ada_b': 'new_v', 'new_v_norm_g': 'new_v', 'new_v_ffn_w1': 'new_v', 'new_v_ffn_w3': 'new_v', 'new_v_ffn_w2': 'new_v', 'new_v_ev_w_in': 'new_v', 'new_v_gdn_conv_w': 'new_v', 'new_v_gdn_A_log': 'new_v', 'new_v_gdn_dt_bias': 'new_v', 'new_v_gdn_norm_g': 'new_v', 'new_v_mla_q_norm_g': 'new_v', 'new_v_mla_w_uq': 'new_v', 'new_v_mla_kv_norm_g': 'new_v', 'new_v_mla_w_ukv': 'new_v', 'new_v_ev_w_out': 'new_v', 'new_v_ssd_w_in': 'new_v', 'new_v_ssd_conv_w': 'new_v', 'new_v_ssd_conv_b': 'new_v', 'new_v_ssd_A_log': 'new_v', 'new_v_ssd_dt_bias': 'new_v', 'new_v_ssd_D': 'new_v', 'new_v_ssd_norm_g': 'new_v', 'new_v_ssd_w_out': 'new_v', 'new_v_final_g': 'new_v'}


def _forward(args):
    return _fwd_reference(*[args[k] for k in FWD_PARAMS])


def _output_shape():
    def fwd():
        inp = _fwd_setup_inputs(0)
        return _fwd_reference(*[inp[k] for k in FWD_PARAMS])
    out = _jax.eval_shape(fwd)
    return out.shape, out.dtype

N_MICROBATCH = 1
ADAM_LR = 0.001
ADAM_B1 = 0.9
ADAM_B2 = 0.999
ADAM_EPS = 1e-08
ADAM_WD = 0.01
ADAM_STEP = 10
PER_EXAMPLE_BATCH_AXIS = {'x': 0, 'c': 0, 'positions': 0, 'loss_target': 0}
SHARED_INPUTS = []
_WEIGHT_DTYPES = {'ada_w': _jnp.float32, 'ada_b': _jnp.float32, 'norm_g': _jnp.float32, 'ffn_w1': _jnp.float32, 'ffn_w3': _jnp.float32, 'ffn_w2': _jnp.float32, 'ev_w_in': _jnp.float32, 'gdn_conv_w': _jnp.float32, 'gdn_A_log': _jnp.float32, 'gdn_dt_bias': _jnp.float32, 'gdn_norm_g': _jnp.float32, 'mla_q_norm_g': _jnp.float32, 'mla_w_uq': _jnp.float32, 'mla_kv_norm_g': _jnp.float32, 'mla_w_ukv': _jnp.float32, 'ev_w_out': _jnp.float32, 'ssd_w_in': _jnp.float32, 'ssd_conv_w': _jnp.float32, 'ssd_conv_b': _jnp.float32, 'ssd_A_log': _jnp.float32, 'ssd_dt_bias': _jnp.float32, 'ssd_D': _jnp.float32, 'ssd_norm_g': _jnp.float32, 'ssd_w_out': _jnp.float32, 'final_g': _jnp.float32}
MOMENT_SCALE = {'ada_w': 1.080612e-01, 'ada_b': 2.070229e-01, 'norm_g': 8.181691e-02, 'ffn_w1': 2.425780e-02, 'ffn_w3': 2.360796e-02, 'ffn_w2': 3.913303e-02, 'ev_w_in': 5.561097e-02, 'gdn_conv_w': 5.397059e-02, 'gdn_A_log': 3.188207e-01, 'gdn_dt_bias': 3.131654e-01, 'gdn_norm_g': 1.704544e-01, 'mla_q_norm_g': 1.868212e-02, 'mla_w_uq': 1.291231e-02, 'mla_kv_norm_g': 6.062572e-02, 'mla_w_ukv': 2.816805e-02, 'ev_w_out': 5.950032e-02, 'ssd_w_in': 6.720539e-02, 'ssd_conv_w': 6.396777e-02, 'ssd_conv_b': 8.561297e-02, 'ssd_A_log': 2.752920e-01, 'ssd_dt_bias': 1.585359e-01, 'ssd_D': 3.766246e-01, 'ssd_norm_g': 8.178763e-02, 'ssd_w_out': 1.064660e-01, 'final_g': 1.282388e+02}


def _to_microbatches(a, axis):
    t = _jnp.moveaxis(a, axis, 0)
    t = t.reshape((N_MICROBATCH, t.shape[0] // N_MICROBATCH) + t.shape[1:])
    return _jnp.moveaxis(t, 1, axis + 1)


def setup_inputs(seed: int = 0) -> dict:
    inp = _fwd_setup_inputs(seed)
    key = _jax.random.fold_in(_jax.random.key(seed), 7919)
    shape, _ = _output_shape()
    out = dict(inp)
    out["loss_target"] = _jax.random.normal(_jax.random.fold_in(key, 0), shape, _jnp.float32)
    for i, name in enumerate(TWIN_WEIGHTS):
        w = inp[name].astype(_jnp.float32)
        if MOMENT_SCALE is None:
            s = _jnp.sqrt(_jnp.mean(_jnp.square(w)) + 1e-30)
        else:
            s = MOMENT_SCALE[name]
        km, kv = _jax.random.split(_jax.random.fold_in(key, i + 1))
        out[name] = w
        out["m_" + name] = s * _jax.random.normal(km, w.shape, _jnp.float32)
        out["v_" + name] = (s * s) * _jax.random.uniform(kv, w.shape, _jnp.float32, 0.5, 1.5)
    if N_MICROBATCH > 1:
        for name, axis in PER_EXAMPLE_BATCH_AXIS.items():
            out[name] = _to_microbatches(out[name], axis)
    return {'x': out['x'], 'c': out['c'], 'positions': out['positions'], 'ada_w': out['ada_w'], 'ada_b': out['ada_b'], 'norm_g': out['norm_g'], 'ffn_w1': out['ffn_w1'], 'ffn_w3': out['ffn_w3'], 'ffn_w2': out['ffn_w2'], 'ev_w_in': out['ev_w_in'], 'gdn_conv_w': out['gdn_conv_w'], 'gdn_A_log': out['gdn_A_log'], 'gdn_dt_bias': out['gdn_dt_bias'], 'gdn_norm_g': out['gdn_norm_g'], 'mla_q_norm_g': out['mla_q_norm_g'], 'mla_w_uq': out['mla_w_uq'], 'mla_kv_norm_g': out['mla_kv_norm_g'], 'mla_w_ukv': out['mla_w_ukv'], 'ev_w_out': out['ev_w_out'], 'ssd_w_in': out['ssd_w_in'], 'ssd_conv_w': out['ssd_conv_w'], 'ssd_conv_b': out['ssd_conv_b'], 'ssd_A_log': out['ssd_A_log'], 'ssd_dt_bias': out['ssd_dt_bias'], 'ssd_D': out['ssd_D'], 'ssd_norm_g': out['ssd_norm_g'], 'ssd_w_out': out['ssd_w_out'], 'final_g': out['final_g'], 'loss_target': out['loss_target'], 'm_ada_w': out['m_ada_w'], 'm_ada_b': out['m_ada_b'], 'm_norm_g': out['m_norm_g'], 'm_ffn_w1': out['m_ffn_w1'], 'm_ffn_w3': out['m_ffn_w3'], 'm_ffn_w2': out['m_ffn_w2'], 'm_ev_w_in': out['m_ev_w_in'], 'm_gdn_conv_w': out['m_gdn_conv_w'], 'm_gdn_A_log': out['m_gdn_A_log'], 'm_gdn_dt_bias': out['m_gdn_dt_bias'], 'm_gdn_norm_g': out['m_gdn_norm_g'], 'm_mla_q_norm_g': out['m_mla_q_norm_g'], 'm_mla_w_uq': out['m_mla_w_uq'], 'm_mla_kv_norm_g': out['m_mla_kv_norm_g'], 'm_mla_w_ukv': out['m_mla_w_ukv'], 'm_ev_w_out': out['m_ev_w_out'], 'm_ssd_w_in': out['m_ssd_w_in'], 'm_ssd_conv_w': out['m_ssd_conv_w'], 'm_ssd_conv_b': out['m_ssd_conv_b'], 'm_ssd_A_log': out['m_ssd_A_log'], 'm_ssd_dt_bias': out['m_ssd_dt_bias'], 'm_ssd_D': out['m_ssd_D'], 'm_ssd_norm_g': out['m_ssd_norm_g'], 'm_ssd_w_out': out['m_ssd_w_out'], 'm_final_g': out['m_final_g'], 'v_ada_w': out['v_ada_w'], 'v_ada_b': out['v_ada_b'], 'v_norm_g': out['v_norm_g'], 'v_ffn_w1': out['v_ffn_w1'], 'v_ffn_w3': out['v_ffn_w3'], 'v_ffn_w2': out['v_ffn_w2'], 'v_ev_w_in': out['v_ev_w_in'], 'v_gdn_conv_w': out['v_gdn_conv_w'], 'v_gdn_A_log': out['v_gdn_A_log'], 'v_gdn_dt_bias': out['v_gdn_dt_bias'], 'v_gdn_norm_g': out['v_gdn_norm_g'], 'v_mla_q_norm_g': out['v_mla_q_norm_g'], 'v_mla_w_uq': out['v_mla_w_uq'], 'v_mla_kv_norm_g': out['v_mla_kv_norm_g'], 'v_mla_w_ukv': out['v_mla_w_ukv'], 'v_ev_w_out': out['v_ev_w_out'], 'v_ssd_w_in': out['v_ssd_w_in'], 'v_ssd_conv_w': out['v_ssd_conv_w'], 'v_ssd_conv_b': out['v_ssd_conv_b'], 'v_ssd_A_log': out['v_ssd_A_log'], 'v_ssd_dt_bias': out['v_ssd_dt_bias'], 'v_ssd_D': out['v_ssd_D'], 'v_ssd_norm_g': out['v_ssd_norm_g'], 'v_ssd_w_out': out['v_ssd_w_out'], 'v_final_g': out['v_final_g']}


def _loss(weights, diff, rest, loss_target):
    with _jax.named_scope("forward"):
        args = {**rest, TWIN_DIFF_INPUT: diff, **{k: w.astype(_WEIGHT_DTYPES[k]) for k, w in weights.items()}}
        y = _forward(args)
    with _jax.named_scope("loss_head"):
        err = _jnp.square(y.astype(_jnp.float32) - loss_target)
        return 0.5 * _jnp.sum(_jnp.mean(err, axis=-1)) if err.ndim else 0.5 * err


def _adamw(w, g, m, v):
    m = ADAM_B1 * m + (1.0 - ADAM_B1) * g
    v = ADAM_B2 * v + (1.0 - ADAM_B2) * _jnp.square(g)
    m_hat = m / (1.0 - ADAM_B1 ** ADAM_STEP)
    v_hat = v / (1.0 - ADAM_B2 ** ADAM_STEP)
    delta = -ADAM_LR * (m_hat / (_jnp.sqrt(v_hat) + ADAM_EPS) + ADAM_WD * w)
    return delta, m, v


def reference(x, c, positions, ada_w, ada_b, norm_g, ffn_w1, ffn_w3, ffn_w2, ev_w_in, gdn_conv_w, gdn_A_log, gdn_dt_bias, gdn_norm_g, mla_q_norm_g, mla_w_uq, mla_kv_norm_g, mla_w_ukv, ev_w_out, ssd_w_in, ssd_conv_w, ssd_conv_b, ssd_A_log, ssd_dt_bias, ssd_D, ssd_norm_g, ssd_w_out, final_g, loss_target, m_ada_w, m_ada_b, m_norm_g, m_ffn_w1, m_ffn_w3, m_ffn_w2, m_ev_w_in, m_gdn_conv_w, m_gdn_A_log, m_gdn_dt_bias, m_gdn_norm_g, m_mla_q_norm_g, m_mla_w_uq, m_mla_kv_norm_g, m_mla_w_ukv, m_ev_w_out, m_ssd_w_in, m_ssd_conv_w, m_ssd_conv_b, m_ssd_A_log, m_ssd_dt_bias, m_ssd_D, m_ssd_norm_g, m_ssd_w_out, m_final_g, v_ada_w, v_ada_b, v_norm_g, v_ffn_w1, v_ffn_w3, v_ffn_w2, v_ev_w_in, v_gdn_conv_w, v_gdn_A_log, v_gdn_dt_bias, v_gdn_norm_g, v_mla_q_norm_g, v_mla_w_uq, v_mla_kv_norm_g, v_mla_w_ukv, v_ev_w_out, v_ssd_w_in, v_ssd_conv_w, v_ssd_conv_b, v_ssd_A_log, v_ssd_dt_bias, v_ssd_D, v_ssd_norm_g, v_ssd_w_out, v_final_g):
    given = dict(x=x, c=c, positions=positions, ada_w=ada_w, ada_b=ada_b, norm_g=norm_g, ffn_w1=ffn_w1, ffn_w3=ffn_w3, ffn_w2=ffn_w2, ev_w_in=ev_w_in, gdn_conv_w=gdn_conv_w, gdn_A_log=gdn_A_log, gdn_dt_bias=gdn_dt_bias, gdn_norm_g=gdn_norm_g, mla_q_norm_g=mla_q_norm_g, mla_w_uq=mla_w_uq, mla_kv_norm_g=mla_kv_norm_g, mla_w_ukv=mla_w_ukv, ev_w_out=ev_w_out, ssd_w_in=ssd_w_in, ssd_conv_w=ssd_conv_w, ssd_conv_b=ssd_conv_b, ssd_A_log=ssd_A_log, ssd_dt_bias=ssd_dt_bias, ssd_D=ssd_D, ssd_norm_g=ssd_norm_g, ssd_w_out=ssd_w_out, final_g=final_g, loss_target=loss_target, m_ada_w=m_ada_w, m_ada_b=m_ada_b, m_norm_g=m_norm_g, m_ffn_w1=m_ffn_w1, m_ffn_w3=m_ffn_w3, m_ffn_w2=m_ffn_w2, m_ev_w_in=m_ev_w_in, m_gdn_conv_w=m_gdn_conv_w, m_gdn_A_log=m_gdn_A_log, m_gdn_dt_bias=m_gdn_dt_bias, m_gdn_norm_g=m_gdn_norm_g, m_mla_q_norm_g=m_mla_q_norm_g, m_mla_w_uq=m_mla_w_uq, m_mla_kv_norm_g=m_mla_kv_norm_g, m_mla_w_ukv=m_mla_w_ukv, m_ev_w_out=m_ev_w_out, m_ssd_w_in=m_ssd_w_in, m_ssd_conv_w=m_ssd_conv_w, m_ssd_conv_b=m_ssd_conv_b, m_ssd_A_log=m_ssd_A_log, m_ssd_dt_bias=m_ssd_dt_bias, m_ssd_D=m_ssd_D, m_ssd_norm_g=m_ssd_norm_g, m_ssd_w_out=m_ssd_w_out, m_final_g=m_final_g, v_ada_w=v_ada_w, v_ada_b=v_ada_b, v_norm_g=v_norm_g, v_ffn_w1=v_ffn_w1, v_ffn_w3=v_ffn_w3, v_ffn_w2=v_ffn_w2, v_ev_w_in=v_ev_w_in, v_gdn_conv_w=v_gdn_conv_w, v_gdn_A_log=v_gdn_A_log, v_gdn_dt_bias=v_gdn_dt_bias, v_gdn_norm_g=v_gdn_norm_g, v_mla_q_norm_g=v_mla_q_norm_g, v_mla_w_uq=v_mla_w_uq, v_mla_kv_norm_g=v_mla_kv_norm_g, v_mla_w_ukv=v_mla_w_ukv, v_ev_w_out=v_ev_w_out, v_ssd_w_in=v_ssd_w_in, v_ssd_conv_w=v_ssd_conv_w, v_ssd_conv_b=v_ssd_conv_b, v_ssd_A_log=v_ssd_A_log, v_ssd_dt_bias=v_ssd_dt_bias, v_ssd_D=v_ssd_D, v_ssd_norm_g=v_ssd_norm_g, v_ssd_w_out=v_ssd_w_out, v_final_g=v_final_g)
    weights = {n: given[n] for n in TWIN_WEIGHTS}
    shared = {n: given[n] for n in SHARED_INPUTS}
    per_example = {n: given[n] for n in ['x', 'c', 'positions']}
    grad_fn = _jax.value_and_grad(_loss, argnums=(0, 1))

    def one_microbatch(ex, loss_target):
        ex = dict(ex)
        diff = ex.pop(TWIN_DIFF_INPUT)
        return grad_fn(weights, diff, {**shared, **ex}, loss_target)

    if N_MICROBATCH == 1:
        loss, (grad_w, grad_x) = one_microbatch(per_example, given["loss_target"])
    else:
        def body(carry, xs):
            loss_sum, grad_sum = carry
            l_k, (gw_k, gx_k) = one_microbatch(xs[0], xs[1])
            with _jax.named_scope("update"):
                return (loss_sum + l_k, _jax.tree.map(_jnp.add, grad_sum, gw_k)), gx_k

        init = (_jnp.zeros((), _jnp.float32), _jax.tree.map(_jnp.zeros_like, weights))
        (loss, grad_w), grad_x = _jax.lax.scan(body, init, (per_example, given["loss_target"]))
    with _jax.named_scope("update"):
        delta_w, new_m, new_v = {}, {}, {}
        for n in TWIN_WEIGHTS:
            delta_w[n], new_m[n], new_v[n] = _adamw(weights[n], grad_w[n], given["m_" + n], given["v_" + n])
    return (loss, grad_x, *[grad_w[n] for n in TWIN_WEIGHTS], *[delta_w[n] for n in TWIN_WEIGHTS],
            *[new_m[n] for n in TWIN_WEIGHTS], *[new_v[n] for n in TWIN_WEIGHTS])
```

```python
import functools
import math

import numpy as np
import jax
import jax.numpy as jnp
from jax import lax
from jax.experimental import pallas as pl
from jax.experimental.pallas import tpu as pltpu

F32 = jnp.float32
BF16 = jnp.bfloat16
HI = lax.Precision.HIGHEST

D_MODEL = 1024
DEPTH = 4
CHUNK = 64
NORM_EPS = 1e-6
CONV_K = 4
D_FF = 2816
GDN_HEADS = 4
GDN_DK = 128
MLA_HEADS = 4
MLA_NOPE = 128
MLA_ROPE = 64
ROPE_THETA = 10000.0
SSD_HEADS = 32
SSD_HEADDIM = 64
SSD_GROUPS = 4
SSD_STATE = 128
SSD_D_INNER = 2048
N_DEV = 8

ADAM_LR = 0.001
ADAM_B1 = 0.9
ADAM_B2 = 0.999
ADAM_EPS = 1e-08
ADAM_WD = 0.01
ADAM_STEP = 10

V7X_VMEM_LIMIT = 56 * 1024 * 1024
ROW_TILE = 512
SEQ_TILE = 128
ATT_TILE = 512
FF_HALF = D_FF // 2
LANE = 128

EV_QKV, EV_CQ, EV_MISC, EV_Z, EV_CKV, EV_W = 0, 1536, 1920, 2048, 2560, 2816
OD_Z, OD_XBC, OD_DT, OD_W = 0, 2048, 5120, 5376


def _cparams(sem=None):
    return pltpu.CompilerParams(dimension_semantics=sem, vmem_limit_bytes=V7X_VMEM_LIMIT)


def _pick(n, cands):
    for c in cands:
        if n % c == 0:
            return c
    return n


_DN = {"nn": (((1,), (0,)), ((), ())), "nt": (((1,), (1,)), ((), ())), "tn": (((0,), (0,)), ((), ()))}


def _dot(a, b, mode, hi=False):
    if hi:
        return lax.dot_general(a.astype(F32), b.astype(F32), _DN[mode], precision=HI, preferred_element_type=F32)
    return lax.dot_general(a.astype(BF16), b.astype(BF16), _DN[mode], preferred_element_type=F32)


@functools.partial(jax.custom_vjp, nondiff_argnums=(2, 3))
def _mm(a, b, mode, hi):
    return _dot(a, b, mode, hi)


def _mm_fwd(a, b, mode, hi):
    return _dot(a, b, mode, hi), (a, b)


def _mm_bwd(mode, hi, res, g):
    a, b = res
    if mode == "nn":
        return _dot(g, b, "nt", hi), _dot(a, g, "tn", hi)
    if mode == "nt":
        return _dot(g, b, "nn", hi), _dot(g, a, "tn", hi)
    return _dot(b, g, "nt", hi), _dot(a, g, "nn", hi)


_mm.defvjp(_mm_fwd, _mm_bwd)


def _iota2(shape, dim):
    return lax.broadcasted_iota(jnp.int32, shape, dim)


def _row_spec(a, tm):
    if isinstance(a, tuple):
        arr, c0, w = a
        assert c0 % w == 0
        cb = c0 // w
        return arr, pl.BlockSpec((tm, w), lambda i, cb=cb: (i, cb))
    return a, pl.BlockSpec((tm, a.shape[1]), lambda i: (i, 0))


def _full_spec(b):
    return pl.BlockSpec(b.shape, lambda i: (0,) * b.ndim)


def _rows(fn, tiled, bcast, outs, *, name, tm=ROW_TILE):
    arrs, specs = zip(*[_row_spec(a, 0) for a in tiled])
    t_len = arrs[0].shape[0]
    tm = min(tm, t_len)
    arrs, specs = zip(*[_row_spec(a, tm) for a in tiled])
    nt, nb = len(tiled), len(bcast)

    def body(*refs):
        ins = [r[...].astype(F32) for r in refs[:nt]] + [r[...] for r in refs[nt:nt + nb]]
        res = fn(*ins)
        for r, v in zip(refs[nt + nb:], res):
            r[...] = v.astype(r.dtype)

    return pl.pallas_call(
        body, grid=(t_len // tm,), name=name,
        in_specs=list(specs) + [_full_spec(b) for b in bcast],
        out_specs=[pl.BlockSpec((tm, c), lambda i: (i, 0)) for c, _ in outs],
        out_shape=[jax.ShapeDtypeStruct((t_len, c), dt) for c, dt in outs],
        compiler_params=_cparams(("parallel",)),
    )(*arrs, *bcast)


def _rows_vjp(fn, tiled, consts, bcast, bconsts, douts, grads, *, name, adds=None, tm=ROW_TILE // 2):
    adds = adds or {}
    t_arrs, t_specs = zip(*[_row_spec(a, 0) for a in tiled])
    t_len = t_arrs[0].shape[0]
    tm = min(tm, t_len)
    rows_in = list(tiled) + list(consts) + list(douts) + [adds[k] for k in sorted(adds)]
    arrs, specs = zip(*[_row_spec(a, tm) for a in rows_in])
    nt, nc, nb, nbc, nd, na = len(tiled), len(consts), len(bcast), len(bconsts), len(douts), len(adds)
    add_pos = {k: j for j, k in enumerate(sorted(adds))}
    want = [j for j, g in enumerate(grads) if g is not None]

    def body(*refs):
        p = 0
        t = [r[...].astype(F32) for r in refs[p:p + nt]]; p += nt
        c = [r[...].astype(F32) for r in refs[p:p + nc]]; p += nc
        d = [r[...].astype(F32) for r in refs[p:p + nd]]; p += nd
        a = [r[...].astype(F32) for r in refs[p:p + na]]; p += na
        b = [r[...] for r in refs[p:p + nb]]; p += nb
        bc = [r[...] for r in refs[p:p + nbc]]; p += nbc
        g_refs = refs[p:p + len(want)]; p += len(want)
        gb_refs = refs[p:p + nb]

        def f(*args):
            return fn(*args[:nt], *c, *args[nt:], *bc)

        _, vjp = jax.vjp(f, *t, *b)
        g = vjp(tuple(d))
        for r, j in zip(g_refs, want):
            val = g[j]
            if j in add_pos:
                val = val + a[add_pos[j]]
            r[...] = val.astype(r.dtype)

        @pl.when(pl.program_id(0) == 0)
        def _():
            for r in gb_refs:
                r[...] = jnp.zeros_like(r)

        for r, val in zip(gb_refs, g[nt:]):
            r[...] += val

    def width(a):
        return a[2] if isinstance(a, tuple) else a.shape[1]

    res = pl.pallas_call(
        body, grid=(t_len // tm,), name=name,
        in_specs=list(specs) + [_full_spec(b) for b in list(bcast) + list(bconsts)],
        out_specs=[pl.BlockSpec((tm, width(tiled[j])), lambda i: (i, 0)) for j in want] + [_full_spec(b) for b in bcast],
        out_shape=[jax.ShapeDtypeStruct((t_len, width(tiled[j])), grads[j]) for j in want]
        + [jax.ShapeDtypeStruct(b.shape, F32) for b in bcast],
        compiler_params=_cparams(("arbitrary",)),
    )(*arrs, *bcast, *bconsts)
    tg = [None] * nt
    for r, j in zip(res[:len(want)], want):
        tg[j] = r
    return tg, list(res[len(want):])


def _matmul(a, b, mode, out_dtype, *, name):
    if mode == "tn":
        k_len, m_len = a.shape
        n_len = b.shape[1]
        tm = _pick(m_len, (512, 384, 256, 128))
        tn = _pick(n_len, (1408, 1024, 768, 512, 384, 256, 128))
        tk = _pick(k_len, (512, 256, 128))
    else:
        m_len, k_len = a.shape
        n_len = b.shape[1] if mode == "nn" else b.shape[0]
        tm = _pick(m_len, (512, 256, 128))
        tn = _pick(n_len, (768, 512, 384, 256, 128))
        tk = k_len if k_len <= 2048 else _pick(k_len, (2048, 1792, 1536, 1408, 1024, 768, 512, 256, 128))
    nk = k_len // tk
    if mode == "nn":
        a_spec = pl.BlockSpec((tm, tk), lambda i, j, k: (i, k))
        b_spec = pl.BlockSpec((tk, tn), lambda i, j, k: (k, j))
    elif mode == "nt":
        a_spec = pl.BlockSpec((tm, tk), lambda i, j, k: (i, k))
        b_spec = pl.BlockSpec((tn, tk), lambda i, j, k: (j, k))
    else:
        a_spec = pl.BlockSpec((tk, tm), lambda i, j, k: (k, i))
        b_spec = pl.BlockSpec((tk, tn), lambda i, j, k: (k, j))

    def body(a_ref, b_ref, o_ref, acc_ref):
        @pl.when(pl.program_id(2) == 0)
        def _():
            acc_ref[...] = jnp.zeros_like(acc_ref)

        acc_ref[...] += _dot(a_ref[...], b_ref[...], mode)

        @pl.when(pl.program_id(2) == nk - 1)
        def _():
            o_ref[...] = acc_ref[...].astype(o_ref.dtype)

    return pl.pallas_call(
        body, grid=(m_len // tm, n_len // tn, nk), name=name,
        in_specs=[a_spec, b_spec],
        out_specs=pl.BlockSpec((tm, tn), lambda i, j, k: (i, j)),
        out_shape=jax.ShapeDtypeStruct((m_len, n_len), out_dtype),
        scratch_shapes=[pltpu.VMEM((tm, tn), F32)],
        compiler_params=_cparams(("parallel", "parallel", "arbitrary")),
    )(a, b)


def _ffn_act(h, w13, *, name):
    t_len, d = h.shape
    tm = min(ROW_TILE, t_len)

    def body(h_ref, w_ref, s_ref):
        ab = _dot(h_ref[...], w_ref[...], "nn")
        a, b = ab[:, :FF_HALF], ab[:, FF_HALF:]
        s_ref[...] = (a * jax.nn.sigmoid(a) * b).astype(s_ref.dtype)

    return pl.pallas_call(
        body, grid=(2, t_len // tm), name=name,
        in_specs=[pl.BlockSpec((tm, d), lambda f, i: (i, 0)), pl.BlockSpec((d, 2 * FF_HALF), lambda f, i: (0, f))],
        out_specs=pl.BlockSpec((tm, FF_HALF), lambda f, i: (i, f)),
        out_shape=jax.ShapeDtypeStruct((t_len, D_FF), BF16),
        compiler_params=_cparams(("parallel", "parallel")),
    )(h, w13)


def _ffn_act_bwd(h, dy, w13, w2, *, name):
    t_len, d = h.shape
    tm = min(ROW_TILE, t_len)

    def body(h_ref, dy_ref, w_ref, w2_ref, o_ref):
        ab = _dot(h_ref[...], w_ref[...], "nn")
        a, b = ab[:, :FF_HALF], ab[:, FF_HALF:]
        ds = _dot(dy_ref[...], w2_ref[...], "nt")
        sig = jax.nn.sigmoid(a)
        silu = a * sig
        da = ds * b * (sig * (1.0 + a * (1.0 - sig)))
        db = ds * silu
        o_ref[...] = jnp.concatenate([da, db], axis=-1).astype(o_ref.dtype)

    return pl.pallas_call(
        body, grid=(2, t_len // tm), name=name,
        in_specs=[pl.BlockSpec((tm, d), lambda f, i: (i, 0)), pl.BlockSpec((tm, d), lambda f, i: (i, 0)),
                  pl.BlockSpec((d, 2 * FF_HALF), lambda f, i: (0, f)), pl.BlockSpec((FF_HALF, d), lambda f, i: (f, 0))],
        out_specs=pl.BlockSpec((tm, 2 * FF_HALF), lambda f, i: (i, f)),
        out_shape=jax.ShapeDtypeStruct((t_len, 2 * D_FF), BF16),
        compiler_params=_cparams(("parallel", "parallel")),
    )(h, dy, w13, w2)


CONV_CB = 512
HALO = 8


def _conv_fwd(p, c0, n_ch, w, b, *, name):
    t_len = p.shape[0]
    tm = min(ROW_TILE, t_len)
    hb = tm // HALO
    cb0 = c0 // CONV_CB

    def body(x_ref, halo_ref, w_ref, b_ref, act_ref, pre_ref):
        first = pl.program_id(1) == 0
        halo = jnp.where(first, 0.0, halo_ref[...])
        xx = jnp.concatenate([halo, x_ref[...]], axis=0)
        wv = w_ref[...]
        acc = b_ref[...] + wv[0:1] * xx[HALO - 3:HALO - 3 + tm]
        for j in range(1, CONV_K):
            acc = acc + wv[j:j + 1] * xx[HALO - 3 + j:HALO - 3 + j + tm]
        pre_ref[...] = acc
        act_ref[...] = acc * jax.nn.sigmoid(acc)

    return pl.pallas_call(
        body, grid=(n_ch // CONV_CB, t_len // tm), name=name,
        in_specs=[pl.BlockSpec((tm, CONV_CB), lambda j, i: (i, cb0 + j)),
                  pl.BlockSpec((HALO, CONV_CB), lambda j, i: (jnp.maximum(i * hb - 1, 0), cb0 + j)),
                  pl.BlockSpec((CONV_K, CONV_CB), lambda j, i: (0, j)),
                  pl.BlockSpec((1, CONV_CB), lambda j, i: (0, j))],
        out_specs=[pl.BlockSpec((tm, CONV_CB), lambda j, i: (i, j))] * 2,
        out_shape=[jax.ShapeDtypeStruct((t_len, n_ch), F32)] * 2,
        compiler_params=_cparams(("parallel", "arbitrary")),
    )(p, p, w, b)


def _conv_bwd(dact, pre, pre_c0, p, p_c0, w, *, name):
    t_len, n_ch = dact.shape
    tm = min(ROW_TILE, t_len)
    hb = tm // HALO
    nt = t_len // tm
    last_hb = t_len // HALO - 1
    cb0 = p_c0 // CONV_CB
    cbp = pre_c0 // CONV_CB

    def dsilu(z):
        sig = jax.nn.sigmoid(z)
        return sig * (1.0 + z * (1.0 - sig))

    def body(d_ref, dn_ref, pre_ref, pren_ref, x_ref, xh_ref, w_ref, dx_ref, dw_ref, db_ref):
        i = pl.program_id(1)
        dpre = d_ref[...] * dsilu(pre_ref[...])
        dnext = jnp.where(i == nt - 1, 0.0, dn_ref[...] * dsilu(pren_ref[...]))
        ext = jnp.concatenate([dpre, dnext], axis=0)
        xx = jnp.concatenate([jnp.where(i == 0, 0.0, xh_ref[...]), x_ref[...]], axis=0)
        wv = w_ref[...]
        dx = wv[0:1] * ext[3:3 + tm]
        for j in range(1, CONV_K):
            dx = dx + wv[j:j + 1] * ext[3 - j:3 - j + tm]
        dx_ref[...] = dx
        dws = [jnp.sum(dpre * xx[HALO - 3 + j:HALO - 3 + j + tm], axis=0, keepdims=True) for j in range(CONV_K)]

        @pl.when(i == 0)
        def _():
            dw_ref[...] = jnp.zeros_like(dw_ref)
            db_ref[...] = jnp.zeros_like(db_ref)

        dw_ref[...] += jnp.concatenate(dws, axis=0)
        db_ref[...] += jnp.sum(dpre, axis=0, keepdims=True)

    tile = lambda off: pl.BlockSpec((tm, CONV_CB), lambda j, i: (i, off + j))
    nxt = lambda off: pl.BlockSpec((HALO, CONV_CB), lambda j, i: (jnp.minimum((i + 1) * hb, last_hb), off + j))
    return pl.pallas_call(
        body, grid=(n_ch // CONV_CB, nt), name=name,
        in_specs=[tile(0), nxt(0), tile(cbp), nxt(cbp), tile(cb0),
                  pl.BlockSpec((HALO, CONV_CB), lambda j, i: (jnp.maximum(i * hb - 1, 0), cb0 + j)),
                  pl.BlockSpec((CONV_K, CONV_CB), lambda j, i: (0, cbp + j))],
        out_specs=[tile(0), pl.BlockSpec((CONV_K, CONV_CB), lambda j, i: (0, j)), pl.BlockSpec((1, CONV_CB), lambda j, i: (0, j))],
        out_shape=[jax.ShapeDtypeStruct((t_len, n_ch), F32), jax.ShapeDtypeStruct((CONV_K, n_ch), F32),
                   jax.ShapeDtypeStruct((1, n_ch), F32)],
        compiler_params=_cparams(("parallel", "arbitrary")),
    )(dact, dact, pre, pre, p, p, w)


@jax.custom_vjp
def _inv_unit_lower(a_mat):
    c = a_mat.shape[0]
    a_t = a_mat.T
    rows, cols = _iota2((c, c), 0), _iota2((1, c), 1)
    x = jnp.zeros((c, c), F32)
    for i in range(c):
        r = (cols == i).astype(F32)
        if i:
            r = r - jnp.sum(a_t[:, i:i + 1] * x, axis=0, keepdims=True)
        x = jnp.where(rows == i, r, x)
    return x


def _inv_fwd(a_mat):
    x = _inv_unit_lower(a_mat)
    return x, x


def _inv_bwd(x, g):
    return (-_dot(x, _dot(g, x, "nt", True), "tn", True),)


_inv_unit_lower.defvjp(_inv_fwd, _inv_bwd)


def _l2norm(x):
    return x * lax.rsqrt(jnp.sum(x * x, axis=-1, keepdims=True) + NORM_EPS)


def _rms(x):
    return x * lax.rsqrt(jnp.mean(x * x, axis=-1, keepdims=True) + NORM_EPS)


def _tri_masks(c):
    rows, cols = _iota2((c, c), 0), _iota2((c, c), 1)
    return rows >= cols, rows > cols, (rows >= cols).astype(F32), (rows <= cols).astype(F32)


def _gdn_tile(q, k, v, misc, s0, alog, dtb, head):
    c = CHUNK
    lane = _iota2((1, LANE), 1)
    alog_s = jnp.sum(jnp.where(lane == head, alog, 0.0), axis=-1, keepdims=True)
    dtb_s = jnp.sum(jnp.where(lane == head, dtb, 0.0), axis=-1, keepdims=True)
    lane_t = _iota2(misc.shape, 1)
    bl_all = jnp.sum(jnp.where(lane_t == 64 + head, misc, 0.0), axis=-1, keepdims=True)
    a_all = jnp.sum(jnp.where(lane_t == 68 + head, misc, 0.0), axis=-1, keepdims=True)
    lower, strict, ltri, utri = _tri_masks(c)
    s = s0
    outs = []
    for ci in range(q.shape[0] // c):
        sl = slice(ci * c, (ci + 1) * c)
        qn = _l2norm(q[sl]) * (GDN_DK ** -0.5)
        kn = _l2norm(k[sl])
        beta = jax.nn.sigmoid(bl_all[sl])
        g = -jnp.exp(alog_s) * jax.nn.softplus(a_all[sl] + dtb_s)
        gb = jnp.broadcast_to(g, (c, c))
        gc_col = _mm(ltri, gb, "nn", True)
        gc_row = _mm(gb, utri, "tn", True)
        decay = jnp.where(lower, jnp.exp(jnp.where(lower, gc_col - gc_row, 0.0)), 0.0)
        gc = gc_col[:, 0:1]
        g_last = gc_col[c - 1:c, 0:1]
        kb = kn * beta
        a_mat = jnp.where(strict, _mm(kb, kn, "nt", False) * decay, 0.0)
        t_inv = _inv_unit_lower(a_mat)
        u = _mm(t_inv, v[sl] * beta, "nn", True)
        w = _mm(t_inv, kb * jnp.exp(gc), "nn", True)
        attn = _mm(qn, kn, "nt", False) * decay
        k_end = kn * jnp.exp(g_last - gc)
        q_start = qn * jnp.exp(gc)
        v_new = u - _mm(w, s, "nn", False)
        outs.append(_mm(q_start, s, "nn", False) + _mm(attn, v_new, "nn", False))
        s = s * jnp.exp(g_last) + _mm(k_end, v_new, "tn", False)
    return jnp.concatenate(outs, axis=0), s


def _gdn_specs(tt, rev_n=None):
    t = (lambda i: i) if rev_n is None else (lambda i: rev_n - 1 - i)
    col = lambda off: pl.BlockSpec((tt, LANE), lambda h, i: (t(i), off + h))
    vec = pl.BlockSpec((1, LANE), lambda h, i: (0, 0))
    misc = pl.BlockSpec((tt, LANE), lambda h, i: (t(i), EV_MISC // LANE))
    return [col(0), col(GDN_HEADS), col(2 * GDN_HEADS), misc, vec, vec], t


def _gdn_fwd(act, p, alog, dtb, *, name):
    t_len = act.shape[0]
    tt = min(SEQ_TILE, t_len)
    ntile = t_len // tt
    in_specs, _ = _gdn_specs(tt)

    def body(q_ref, k_ref, v_ref, m_ref, al_ref, dt_ref, o_ref, s_ref, state):
        @pl.when(pl.program_id(1) == 0)
        def _():
            state[...] = jnp.zeros_like(state)

        s_ref[0, 0] = state[...]
        o, s_new = _gdn_tile(q_ref[...], k_ref[...], v_ref[...], m_ref[...], state[...], al_ref[...], dt_ref[...],
                             pl.program_id(0))
        o_ref[...] = o
        state[...] = s_new

    return pl.pallas_call(
        body, grid=(GDN_HEADS, ntile), name=name, in_specs=in_specs,
        out_specs=[pl.BlockSpec((tt, LANE), lambda h, i: (i, h)),
                   pl.BlockSpec((1, 1, GDN_DK, LANE), lambda h, i: (h, i, 0, 0))],
        out_shape=[jax.ShapeDtypeStruct((t_len, GDN_HEADS * LANE), F32),
                   jax.ShapeDtypeStruct((GDN_HEADS, ntile, GDN_DK, LANE), F32)],
        scratch_shapes=[pltpu.VMEM((GDN_DK, LANE), F32)],
        compiler_params=_cparams(("arbitrary", "arbitrary")),
    )(act, act, act, p, alog, dtb)


def _gdn_bwd(act, p, alog, dtb, states, do, *, name):
    t_len = act.shape[0]
    tt = min(SEQ_TILE, t_len)
    ntile = t_len // tt
    in_specs, t = _gdn_specs(tt, ntile)

    def body(q_ref, k_ref, v_ref, m_ref, al_ref, dt_ref, s0_ref, do_ref,
             dq_ref, dk_ref, dv_ref, dm_ref, dal_ref, ddt_ref, dstate):
        head = pl.program_id(0)

        @pl.when(pl.program_id(1) == 0)
        def _():
            dstate[...] = jnp.zeros_like(dstate)

        @pl.when((pl.program_id(1) == 0) & (head == 0))
        def _():
            dal_ref[...] = jnp.zeros_like(dal_ref)
            ddt_ref[...] = jnp.zeros_like(ddt_ref)

        def f(q, k, v, m, s0, al, dt):
            return _gdn_tile(q, k, v, m, s0, al, dt, head)

        _, vjp = jax.vjp(f, q_ref[...], k_ref[...], v_ref[...], m_ref[...], s0_ref[0, 0], al_ref[...], dt_ref[...])
        dq, dk, dv, dm, ds0, dal, ddt = vjp((do_ref[...], dstate[...]))
        dq_ref[...] = dq
        dk_ref[...] = dk
        dv_ref[...] = dv
        dm_ref[0] = dm
        dstate[...] = ds0
        dal_ref[...] += dal
        ddt_ref[...] += ddt

    row = pl.BlockSpec((tt, LANE), lambda h, i: (t(i), h))
    vec = pl.BlockSpec((1, LANE), lambda h, i: (0, 0))
    return pl.pallas_call(
        body, grid=(GDN_HEADS, ntile), name=name,
        in_specs=in_specs + [pl.BlockSpec((1, 1, GDN_DK, LANE), lambda h, i: (h, t(i), 0, 0)), row],
        out_specs=[row, row, row, pl.BlockSpec((1, tt, LANE), lambda h, i: (h, t(i), 0)), vec, vec],
        out_shape=[jax.ShapeDtypeStruct((t_len, GDN_HEADS * LANE), F32)] * 3
        + [jax.ShapeDtypeStruct((GDN_HEADS, t_len, LANE), F32)] + [jax.ShapeDtypeStruct((1, LANE), F32)] * 2,
        scratch_shapes=[pltpu.VMEM((GDN_DK, LANE), F32)],
        compiler_params=_cparams(("arbitrary", "arbitrary")),
    )(act, act, act, p, alog, dtb, states, do)


def _head_expand():
    return jnp.asarray(np.repeat(np.eye(LANE, SSD_HEADS, dtype=np.float32), SSD_HEADDIM, axis=1))


def _ssd_tile(xs, bm, cm, dtr, hs0, alog, dtb, dsk, expand):
    c = CHUNK
    gw = SSD_D_INNER // SSD_GROUPS
    hpg = SSD_HEADS // SSD_GROUPS
    lower, _, ltri, utri = _tri_masks(c)
    half = _iota2((c, LANE), 1) // SSD_HEADDIM
    dt = jax.nn.softplus(dtr + dtb)
    da = dt * (-jnp.exp(alog))
    xdt = xs * _mm(dt, expand, "nn", True)
    d_x = _mm(jnp.broadcast_to(dsk, (8, LANE)), expand, "nn", True)[0:1]
    hs = [hs0[g] for g in range(SSD_GROUPS)]
    ys = []
    for ci in range(xs.shape[0] // c):
        sl = slice(ci * c, (ci + 1) * c)
        acs = _mm(ltri, da[sl], "nn", True)
        acs_t = _mm(da[sl], utri, "tn", True)
        acs_last = acs[c - 1:c, :]
        e_start = _mm(jnp.exp(acs), expand, "nn", True)
        e_end = _mm(jnp.exp(acs_last - acs), expand, "nn", True)
        e_dec = _mm(jnp.broadcast_to(jnp.exp(acs_last), (8, LANE)), expand, "nn", True)[0:1]
        xdt_c = xdt[sl]
        y_tiles = [None] * (SSD_D_INNER // LANE)
        y_off = []
        for g in range(SSD_GROUPS):
            b_g = bm[sl, g * SSD_STATE:(g + 1) * SSD_STATE]
            c_g = cm[sl, g * SSD_STATE:(g + 1) * SSD_STATE]
            gs = slice(g * gw, (g + 1) * gw)
            cb = _mm(c_g, b_g, "nt", False)
            y_off.append(_mm(c_g, hs[g], "nn", False) * e_start[:, gs])
            for r in range(hpg):
                h = g * hpg + r
                j = h // 2
                lm = jnp.where(lower, jnp.exp(jnp.where(lower, acs[:, h:h + 1] - acs_t[h:h + 1, :], 0.0)), 0.0)
                xm = jnp.where(half == (h % 2), xdt_c[:, j * LANE:(j + 1) * LANE], 0.0)
                part = _mm(cb * lm, xm, "nn", False)
                y_tiles[j] = part if y_tiles[j] is None else y_tiles[j] + part
            hs[g] = hs[g] * e_dec[:, gs] + _mm(b_g, xdt_c[:, gs] * e_end[:, gs], "tn", False)
        ys.append(jnp.concatenate(y_tiles, axis=-1) + jnp.concatenate(y_off, axis=-1) + d_x * xs[sl])
    return jnp.concatenate(ys, axis=0), jnp.stack(hs, axis=0)


def _ssd_specs(tt, rev_n=None):
    t = (lambda i: i) if rev_n is None else (lambda i: rev_n - 1 - i)
    vec = pl.BlockSpec((1, LANE), lambda i: (0, 0))
    specs = [pl.BlockSpec((tt, SSD_D_INNER), lambda i: (t(i), 0)),
             pl.BlockSpec((tt, 512), lambda i: (t(i), SSD_D_INNER // 512)),
             pl.BlockSpec((tt, 512), lambda i: (t(i), SSD_D_INNER // 512 + 1)),
             pl.BlockSpec((tt, LANE), lambda i: (t(i), OD_DT // LANE)), vec, vec, vec,
             pl.BlockSpec((LANE, SSD_D_INNER), lambda i: (0, 0))]
    return specs, t


def _ssd_fwd(act, p, alog, dtb, dsk, *, name):
    t_len = act.shape[0]
    tt = min(SEQ_TILE, t_len)
    ntile = t_len // tt
    in_specs, _ = _ssd_specs(tt)

    def body(x_ref, b_ref, c_ref, dt_ref, al_ref, db_ref, dk_ref, e_ref, y_ref, s_ref, state):
        @pl.when(pl.program_id(0) == 0)
        def _():
            state[...] = jnp.zeros_like(state)

        s_ref[0] = state[...]
        y, hs = _ssd_tile(x_ref[...], b_ref[...], c_ref[...], dt_ref[...], state[...], al_ref[...], db_ref[...],
                          dk_ref[...], e_ref[...])
        y_ref[...] = y
        state[...] = hs

    return pl.pallas_call(
        body, grid=(ntile,), name=name, in_specs=in_specs,
        out_specs=[pl.BlockSpec((tt, SSD_D_INNER), lambda i: (i, 0)),
                   pl.BlockSpec((1, SSD_GROUPS, SSD_STATE, 512), lambda i: (i, 0, 0, 0))],
        out_shape=[jax.ShapeDtypeStruct((t_len, SSD_D_INNER), F32),
                   jax.ShapeDtypeStruct((ntile, SSD_GROUPS, SSD_STATE, 512), F32)],
        scratch_shapes=[pltpu.VMEM((SSD_GROUPS, SSD_STATE, 512), F32)],
        compiler_params=_cparams(("arbitrary",)),
    )(act, act, act, p, alog, dtb, dsk, _head_expand())


def _ssd_bwd(act, p, alog, dtb, dsk, states, dy, *, name):
    t_len = act.shape[0]
    tt = min(SEQ_TILE, t_len)
    ntile = t_len // tt
    in_specs, t = _ssd_specs(tt, ntile)

    def body(x_ref, b_ref, c_ref, dt_ref, al_ref, db_ref, dk_ref, e_ref, s0_ref, dy_ref,
             dx_ref, dbm_ref, dcm_ref, ddt_ref, dal_ref, ddb_ref, ddk_ref, dstate):
        @pl.when(pl.program_id(0) == 0)
        def _():
            dstate[...] = jnp.zeros_like(dstate)
            dal_ref[...] = jnp.zeros_like(dal_ref)
            ddb_ref[...] = jnp.zeros_like(ddb_ref)
            ddk_ref[...] = jnp.zeros_like(ddk_ref)

        expand = e_ref[...]

        def f(xs, bm, cm, dtr, hs0, al, db, dk):
            return _ssd_tile(xs, bm, cm, dtr, hs0, al, db, dk, expand)

        _, vjp = jax.vjp(f, x_ref[...], b_ref[...], c_ref[...], dt_ref[...], s0_ref[0], al_ref[...], db_ref[...],
                         dk_ref[...])
        dx, dbm, dcm, ddt, dhs, dal, ddb, ddk = vjp((dy_ref[...], dstate[...]))
        dx_ref[...] = dx
        dbm_ref[...] = dbm
        dcm_ref[...] = dcm
        ddt_ref[...] = ddt
        dstate[...] = dhs
        dal_ref[...] += dal
        ddb_ref[...] += ddb
        ddk_ref[...] += ddk

    vec = pl.BlockSpec((1, LANE), lambda i: (0, 0))
    rows = lambda w: pl.BlockSpec((tt, w), lambda i: (t(i), 0))
    return pl.pallas_call(
        body, grid=(ntile,), name=name,
        in_specs=in_specs + [pl.BlockSpec((1, SSD_GROUPS, SSD_STATE, 512), lambda i: (t(i), 0, 0, 0)), rows(SSD_D_INNER)],
        out_specs=[rows(SSD_D_INNER), rows(512), rows(512), rows(LANE), vec, vec, vec],
        out_shape=[jax.ShapeDtypeStruct((t_len, SSD_D_INNER), F32), jax.ShapeDtypeStruct((t_len, 512), F32),
                   jax.ShapeDtypeStruct((t_len, 512), F32), jax.ShapeDtypeStruct((t_len, LANE), F32)]
        + [jax.ShapeDtypeStruct((1, LANE), F32)] * 3,
        scratch_shapes=[pltpu.VMEM((SSD_GROUPS, SSD_STATE, 512), F32)],
        compiler_params=_cparams(("arbitrary",)),
    )(act, act, act, p, alog, dtb, dsk, _head_expand(), states, dy)


ATT_SCALE = (MLA_NOPE + MLA_ROPE) ** -0.5
QK_W = 2 * LANE


def _chunk_mask(tq):
    return (_iota2((tq, tq), 1) // CHUNK) <= (_iota2((tq, tq), 0) // CHUNK)


def _attn_fwd(qc, kc, vv, *, name):
    t_len = qc.shape[0]
    tq = min(ATT_TILE, t_len)
    nq = t_len // tq

    def body(q_ref, k_ref, v_ref, o_ref, lse_ref, m_s, l_s, acc_s):
        qi, ki = pl.program_id(1), pl.program_id(2)

        @pl.when(ki == 0)
        def _():
            m_s[...] = jnp.full_like(m_s, -jnp.inf)
            l_s[...] = jnp.zeros_like(l_s)
            acc_s[...] = jnp.zeros_like(acc_s)

        def step(masked):
            s = _dot(q_ref[...], k_ref[...], "nt") * ATT_SCALE
            if masked:
                s = jnp.where(_chunk_mask(tq), s, -jnp.inf)
            m_new = jnp.maximum(m_s[...], jnp.max(s, axis=-1, keepdims=True))
            alpha = jnp.exp(m_s[...] - m_new)
            p = jnp.exp(s - m_new)
            l_s[...] = alpha * l_s[...] + jnp.sum(p, axis=-1, keepdims=True)
            acc_s[...] = alpha * acc_s[...] + _dot(p, v_ref[...], "nn")
            m_s[...] = m_new

        @pl.when(ki < qi)
        def _():
            step(False)

        @pl.when(ki == qi)
        def _():
            step(True)
            o_ref[...] = acc_s[...] / l_s[...]
            lse_ref[...] = jnp.broadcast_to(m_s[...] + jnp.log(l_s[...]), lse_ref.shape)

    kv_idx = lambda h, i, k: (jnp.minimum(k, i), h)
    return pl.pallas_call(
        body, grid=(MLA_HEADS, nq, nq), name=name,
        in_specs=[pl.BlockSpec((tq, QK_W), lambda h, i, k: (i, h)), pl.BlockSpec((tq, QK_W), kv_idx),
                  pl.BlockSpec((tq, LANE), kv_idx)],
        out_specs=[pl.BlockSpec((tq, LANE), lambda h, i, k: (i, h))] * 2,
        out_shape=[jax.ShapeDtypeStruct((t_len, MLA_HEADS * LANE), F32)] * 2,
        scratch_shapes=[pltpu.VMEM((tq, 1), F32), pltpu.VMEM((tq, 1), F32), pltpu.VMEM((tq, LANE), F32)],
        compiler_params=_cparams(("parallel", "parallel", "arbitrary")),
    )(qc, kc, vv)


def _attn_probs(q, k, v, do, o, lse, masked, tq):
    s = _dot(q, k, "nt") * ATT_SCALE
    if masked:
        s = jnp.where(_chunk_mask(tq), s, -jnp.inf)
    p = jnp.exp(s - lse[:, 0:1])
    delta = jnp.sum(do * o, axis=-1, keepdims=True)
    ds = p * (_dot(do, v, "nt") - delta) * ATT_SCALE
    return p, ds


def _attn_bwd_q(qc, kc, vv, o, lse, do, *, name):
    t_len = qc.shape[0]
    tq = min(ATT_TILE, t_len)
    nq = t_len // tq

    def body(q_ref, k_ref, v_ref, o_ref, lse_ref, do_ref, dq_ref, acc_s):
        qi, ki = pl.program_id(1), pl.program_id(2)

        @pl.when(ki == 0)
        def _():
            acc_s[...] = jnp.zeros_like(acc_s)

        def step(masked):
            _, ds = _attn_probs(q_ref[...], k_ref[...], v_ref[...], do_ref[...], o_ref[...], lse_ref[...], masked, tq)
            acc_s[...] += _dot(ds, k_ref[...], "nn")

        @pl.when(ki < qi)
        def _():
            step(False)

        @pl.when(ki == qi)
        def _():
            step(True)
            dq_ref[...] = acc_s[...]

    kv_idx = lambda h, i, k: (jnp.minimum(k, i), h)
    q_idx = lambda h, i, k: (i, h)
    return pl.pallas_call(
        body, grid=(MLA_HEADS, nq, nq), name=name,
        in_specs=[pl.BlockSpec((tq, QK_W), q_idx), pl.BlockSpec((tq, QK_W), kv_idx), pl.BlockSpec((tq, LANE), kv_idx),
                  pl.BlockSpec((tq, LANE), q_idx), pl.BlockSpec((tq, LANE), q_idx), pl.BlockSpec((tq, LANE), q_idx)],
        out_specs=pl.BlockSpec((tq, QK_W), q_idx),
        out_shape=jax.ShapeDtypeStruct((t_len, MLA_HEADS * QK_W), F32),
        scratch_shapes=[pltpu.VMEM((tq, QK_W), F32)],
        compiler_params=_cparams(("parallel", "parallel", "arbitrary")),
    )(qc, kc, vv, o, lse, do)


def _attn_bwd_kv(qc, kc, vv, o, lse, do, *, name):
    t_len = qc.shape[0]
    tq = min(ATT_TILE, t_len)
    nq = t_len // tq

    def body(q_ref, k_ref, v_ref, o_ref, lse_ref, do_ref, dk_ref, dv_ref, dk_s, dv_s):
        ki, qi = pl.program_id(1), pl.program_id(2)

        @pl.when(qi == 0)
        def _():
            dk_s[...] = jnp.zeros_like(dk_s)
            dv_s[...] = jnp.zeros_like(dv_s)

        def step(masked):
            p, ds = _attn_probs(q_ref[...], k_ref[...], v_ref[...], do_ref[...], o_ref[...], lse_ref[...], masked, tq)
            dv_s[...] += _dot(p, do_ref[...], "tn")
            dk_s[...] += _dot(ds, q_ref[...], "tn")

        @pl.when(qi > ki)
        def _():
            step(False)

        @pl.when(qi == ki)
        def _():
            step(True)

        @pl.when(qi == nq - 1)
        def _():
            dk_ref[...] = dk_s[...]
            dv_ref[...] = dv_s[...]

    q_idx = lambda h, k, i: (jnp.maximum(i, k), h)
    k_idx = lambda h, k, i: (k, h)
    return pl.pallas_call(
        body, grid=(MLA_HEADS, nq, nq), name=name,
        in_specs=[pl.BlockSpec((tq, QK_W), q_idx), pl.BlockSpec((tq, QK_W), k_idx), pl.BlockSpec((tq, LANE), k_idx),
                  pl.BlockSpec((tq, LANE), q_idx), pl.BlockSpec((tq, LANE), q_idx), pl.BlockSpec((tq, LANE), q_idx)],
        out_specs=[pl.BlockSpec((tq, QK_W), k_idx), pl.BlockSpec((tq, LANE), k_idx)],
        out_shape=[jax.ShapeDtypeStruct((t_len, MLA_HEADS * QK_W), F32), jax.ShapeDtypeStruct((t_len, MLA_HEADS * LANE), F32)],
        scratch_shapes=[pltpu.VMEM((tq, QK_W), F32), pltpu.VMEM((tq, LANE), F32)],
        compiler_params=_cparams(("parallel", "parallel", "arbitrary")),
    )(qc, kc, vv, o, lse, do)


def _adaln_fn(x, g, shift, scale):
    return ((_rms(x) * g) * (1.0 + scale) + shift,)


def _resid_fn(coef, y, x, gate):
    return (x + coef * gate * y,)


def _rms2_fn(cq, ckv, gq, gkv):
    return _rms(cq) * gq, _rms(ckv) * gkv


@jax.custom_vjp
def _swap_halves(x):
    return jnp.concatenate([x[:, 32:64], x[:, 0:32], x[:, 64:128]], axis=-1)


_swap_halves.defvjp(lambda x: (_swap_halves(x), None), lambda _, g: (_swap_halves(g),))


def _rope_fn(q, kv, misc, pos, invf, sgn):
    ang = pos * invf
    cos, sin = jnp.cos(ang), jnp.sin(ang) * sgn

    def rope(x):
        return x * cos + _swap_halves(x) * sin

    k_pe = rope(jnp.where(_iota2(misc.shape, 1) < MLA_ROPE, misc, 0.0))
    qs, ks = [], []
    for h in range(MLA_HEADS):
        qs += [q[:, h * LANE:(h + 1) * LANE], rope(q[:, (MLA_HEADS + h) * LANE:(MLA_HEADS + h + 1) * LANE])]
        ks += [kv[:, h * LANE:(h + 1) * LANE], k_pe]
    return jnp.concatenate(qs, axis=-1), jnp.concatenate(ks, axis=-1), kv[:, MLA_HEADS * LANE:]


def _ev_out_fn(oa, z, ob, g):
    parts = []
    for h in range(GDN_HEADS):
        hs = slice(h * LANE, (h + 1) * LANE)
        zz = z[:, hs]
        parts.append(_rms(oa[:, hs]) * g * (zz * jax.nn.sigmoid(zz)))
    return (jnp.concatenate(parts + [ob], axis=-1),)


def _od_out_fn(y, z, g):
    yz = y * (z * jax.nn.sigmoid(z))
    gw = SSD_D_INNER // SSD_GROUPS
    return (jnp.concatenate([_rms(yz[:, i * gw:(i + 1) * gw]) for i in range(SSD_GROUPS)], axis=-1) * g,)


def _loss_bwd(x, tgt, g, *, name):
    t_len, d = x.shape
    tm = min(ROW_TILE // 2, t_len)

    def body(x_ref, t_ref, g_ref, loss_ref, dx_ref, dg_ref):
        tgt_v = t_ref[...]

        def f(xv, gv):
            err = _rms(xv) * gv - tgt_v
            return 0.5 * jnp.sum(jnp.mean(err * err, axis=-1, keepdims=True), axis=0, keepdims=True)

        val, vjp = jax.vjp(f, x_ref[...], g_ref[...])
        dx, dg = vjp(jnp.ones((1, 1), F32))
        dx_ref[...] = dx

        @pl.when(pl.program_id(0) == 0)
        def _():
            loss_ref[...] = jnp.zeros_like(loss_ref)
            dg_ref[...] = jnp.zeros_like(dg_ref)

        loss_ref[...] += jnp.broadcast_to(val, loss_ref.shape)
        dg_ref[...] += dg

    row = pl.BlockSpec((tm, d), lambda i: (i, 0))
    return pl.pallas_call(
        body, grid=(t_len // tm,), name=name,
        in_specs=[row, row, pl.BlockSpec((1, d), lambda i: (0, 0))],
        out_specs=[pl.BlockSpec((1, LANE), lambda i: (0, 0)), row, pl.BlockSpec((1, d), lambda i: (0, 0))],
        out_shape=[jax.ShapeDtypeStruct((1, LANE), F32), jax.ShapeDtypeStruct((t_len, d), F32),
                   jax.ShapeDtypeStruct((1, d), F32)],
        compiler_params=_cparams(("arbitrary",)),
    )(x, tgt, g)


def _mesh_pos():
    return lax.axis_index("x"), lax.axis_index("y"), lax.axis_index("c")


def _exchange(x, scatter, *, name):
    def body(in_ref, out_ref, send_sems, recv_sems, local_sem):
        mx, my, mc = _mesh_pos()
        me = 4 * mx + 2 * my + mc

        def src(j):
            return in_ref.at[j] if scatter else in_ref

        local = pltpu.make_async_copy(src(me), out_ref.at[me], local_sem)
        local.start()
        started = []
        for d in range(1, N_DEV):
            px = 1 - mx if d & 4 else mx
            py = 1 - my if d & 2 else my
            pc = 1 - mc if d & 1 else mc
            peer = 4 * px + 2 * py + pc
            send = pltpu.make_async_remote_copy(
                src_ref=src(peer), dst_ref=out_ref.at[me], send_sem=send_sems.at[d - 1], recv_sem=recv_sems.at[d - 1],
                device_id=(px, py, pc), device_id_type=pl.DeviceIdType.MESH)
            send.start()
            recv = pltpu.make_async_remote_copy(
                src_ref=src(peer), dst_ref=out_ref.at[peer], send_sem=send_sems.at[d - 1], recv_sem=recv_sems.at[d - 1],
                device_id=(px, py, pc), device_id_type=pl.DeviceIdType.MESH)
            started.append((send, recv))
        for send, recv in started:
            send.wait_send()
            recv.wait_recv()
        local.wait()

    block = x.shape[1:] if scatter else x.shape
    return pl.pallas_call(
        body, name=name,
        in_specs=[pl.BlockSpec(memory_space=pl.ANY)],
        out_specs=pl.BlockSpec(memory_space=pl.ANY),
        out_shape=jax.ShapeDtypeStruct((N_DEV,) + tuple(block), x.dtype),
        scratch_shapes=[pltpu.SemaphoreType.DMA((N_DEV - 1,)), pltpu.SemaphoreType.DMA((N_DEV - 1,)),
                        pltpu.SemaphoreType.DMA(())],
        compiler_params=pltpu.CompilerParams(has_side_effects=True),
    )(x)


PACK_W = 1024


def _adamw(w, gparts, m, v, *, name):
    n_rows = w.shape[0]
    n_parts = gparts.shape[0]
    tm = _pick(n_rows, (256, 128, 64, 32, 16, 8))

    def body(w_ref, g_ref, m_ref, v_ref, go_ref, d_ref, mo_ref, vo_ref):
        g = g_ref[0]
        for j in range(1, n_parts):
            g = g + g_ref[j]
        m_new = ADAM_B1 * m_ref[...] + (1.0 - ADAM_B1) * g
        v_new = ADAM_B2 * v_ref[...] + (1.0 - ADAM_B2) * jnp.square(g)
        m_hat = m_new / (1.0 - ADAM_B1 ** ADAM_STEP)
        v_hat = v_new / (1.0 - ADAM_B2 ** ADAM_STEP)
        go_ref[...] = g
        d_ref[...] = -ADAM_LR * (m_hat / (jnp.sqrt(v_hat) + ADAM_EPS) + ADAM_WD * w_ref[...])
        mo_ref[...] = m_new
        vo_ref[...] = v_new

    row = pl.BlockSpec((tm, PACK_W), lambda i: (i, 0))
    return pl.pallas_call(
        body, grid=(n_rows // tm,), name=name,
        in_specs=[row, pl.BlockSpec((n_parts, tm, PACK_W), lambda i: (0, i, 0)), row, row],
        out_specs=[row] * 4,
        out_shape=[jax.ShapeDtypeStruct((n_rows, PACK_W), F32)] * 4,
        compiler_params=_cparams(("parallel",)),
    )(w, gparts, m, v)


def _pack(parts, dtype, lead=()):
    flat = [p.astype(dtype).reshape(lead + (-1,)) for p in parts]
    n = sum(f.shape[-1] for f in flat)
    n_pad = -n % (16 * PACK_W)
    if n_pad:
        flat.append(jnp.zeros(lead + (n_pad,), dtype))
    return jnp.concatenate(flat, axis=-1).reshape(lead + (-1, PACK_W))


def _unpack(packed, shapes, lead=()):
    flat = packed.reshape(lead + (-1,))
    out, off = [], 0
    for s in shapes:
        n = int(np.prod(s))
        out.append(flat[..., off:off + n].reshape(lead + tuple(s)))
        off += n
    return out


def _mod_shard(c_all, ada_w, ada_b_shard, *, name):
    n_layer, d, n_col = ada_w.shape

    def body(c_ref, w_ref, b_ref, o_ref):
        cv = c_ref[...]
        o_ref[0] = _dot(cv * jax.nn.sigmoid(cv), w_ref[0], "nn") + b_ref[0]

    return pl.pallas_call(
        body, grid=(n_layer,), name=name,
        in_specs=[pl.BlockSpec((N_DEV, d), lambda l: (0, 0)), pl.BlockSpec((1, d, n_col), lambda l: (l, 0, 0)),
                  pl.BlockSpec((1, 1, n_col), lambda l: (l, 0, 0))],
        out_specs=pl.BlockSpec((1, N_DEV, n_col), lambda l: (l, 0, 0)),
        out_shape=jax.ShapeDtypeStruct((n_layer, N_DEV, n_col), F32),
        compiler_params=_cparams(("parallel",)),
    )(c_all, ada_w, ada_b_shard)


def _ada_w_grad(c_all, dmod_shard, *, name):
    n_layer, _, n_col = dmod_shard.shape
    d = c_all.shape[1]

    def body(c_ref, g_ref, o_ref):
        cv = c_ref[...]
        o_ref[0] = _dot(cv * jax.nn.sigmoid(cv), g_ref[0], "tn", True)

    return pl.pallas_call(
        body, grid=(n_layer,), name=name,
        in_specs=[pl.BlockSpec((N_DEV, d), lambda l: (0, 0)), pl.BlockSpec((1, N_DEV, n_col), lambda l: (l, 0, 0))],
        out_specs=pl.BlockSpec((1, d, n_col), lambda l: (l, 0, 0)),
        out_shape=jax.ShapeDtypeStruct((n_layer, d, n_col), F32),
        compiler_params=_cparams(("parallel",)),
    )(c_all, dmod_shard)


def _ffn_w13(w1, w3):
    return jnp.concatenate([w1[:, :FF_HALF], w3[:, :FF_HALF], w1[:, FF_HALF:], w3[:, FF_HALF:]], axis=1)


def _ffn_w13_back(d):
    h = FF_HALF
    return (jnp.concatenate([d[:, :h], d[:, 2 * h:3 * h]], axis=1), jnp.concatenate([d[:, h:2 * h], d[:, 3 * h:]], axis=1))


def _ev_w_in(w):
    pad = jnp.zeros((w.shape[0], LANE - MLA_ROPE - 2 * GDN_HEADS), w.dtype)
    return jnp.concatenate([w[:, 0:1536], w[:, 2056:2440], w[:, 2696:2760], w[:, 2048:2056], pad, w[:, 1536:2048],
                            w[:, 2440:2696]], axis=1)


def _ev_w_in_back(d):
    return jnp.concatenate([d[:, 0:1536], d[:, EV_Z:EV_Z + 512], d[:, EV_MISC + 64:EV_MISC + 72], d[:, EV_CQ:EV_CQ + 384],
                            d[:, EV_CKV:EV_CKV + 256], d[:, EV_MISC:EV_MISC + 64]], axis=1)


def _uq(w):
    r = w.shape[0]
    rope = jnp.pad(w[:, :, MLA_NOPE:], ((0, 0), (0, 0), (0, LANE - MLA_ROPE)))
    return jnp.concatenate([w[:, :, :MLA_NOPE].reshape(r, -1), rope.reshape(r, -1)], axis=1)


def _uq_back(d):
    r = d.shape[0]
    half = MLA_HEADS * LANE
    return jnp.concatenate([d[:, :half].reshape(r, MLA_HEADS, LANE),
                            d[:, half:].reshape(r, MLA_HEADS, LANE)[:, :, :MLA_ROPE]], axis=-1)


def _ukv(w):
    r = w.shape[0]
    return jnp.concatenate([w[:, :, :MLA_NOPE].reshape(r, -1), w[:, :, MLA_NOPE:].reshape(r, -1)], axis=1)


def _ukv_back(d):
    r = d.shape[0]
    half = MLA_HEADS * LANE
    return jnp.concatenate([d[:, :half].reshape(r, MLA_HEADS, LANE), d[:, half:].reshape(r, MLA_HEADS, LANE)], axis=-1)


def _lane_vec(v):
    return jnp.pad(v.astype(F32), (0, LANE - v.shape[0])).reshape(1, LANE)


def _row(v):
    return v.astype(F32).reshape(1, -1)


def _adaln(x, ln):
    return _rows(_adaln_fn, [x], list(ln), [(D_MODEL, BF16)], name="adaln")[0]


def _adaln_bwd(x, ln, dh, dxn):
    (dx,), dln = _rows_vjp(_adaln_fn, [x], [], list(ln), [], [dh], [F32], adds={0: dxn}, name="adaln_bwd")
    return dx, dln


def _resid(coef, y, x, gate):
    return _rows(functools.partial(_resid_fn, coef), [y, x], [gate], [(D_MODEL, F32)], name="resid")[0]


def _resid_bwd(coef, y, x, gate, dxn):
    (dy,), (dgate,) = _rows_vjp(functools.partial(_resid_fn, coef), [y], [x], [gate], [], [dxn], [BF16],
                                name="resid_bwd")
    return dy, dgate


def _ffn_fwd(x, ln, gate, w13, w2):
    h = _adaln(x, ln)
    s = _ffn_act(h, w13, name="ffn_act")
    y = _matmul(s, w2, "nn", F32, name="ffn_down")
    return _resid(0.5, y, x, gate), (x, h, s, y)


def _ffn_bwd(saved, dxn, ln, gate, w13, w2):
    x, h, s, y = saved
    dy, dgate = _resid_bwd(0.5, y, x, gate, dxn)
    dab = _ffn_act_bwd(h, dy, w13, w2, name="ffn_act_bwd")
    dh = _matmul(dab, w13, "nt", F32, name="ffn_dh")
    dw13 = _matmul(h, dab, "tn", F32, name="ffn_dw13")
    dw2 = _matmul(s, dy, "tn", F32, name="ffn_dw2")
    dx, dln = _adaln_bwd(x, ln, dh, dxn)
    return dx, dw13, dw2, dln, dgate


def _rope_consts():
    half = MLA_ROPE // 2
    inv = (ROPE_THETA ** (-jnp.arange(half, dtype=F32) / half)).astype(F32)
    zeros = jnp.zeros((LANE - MLA_ROPE,), F32)
    invf = jnp.concatenate([inv, inv, zeros]).reshape(1, LANE)
    sgn = jnp.concatenate([-jnp.ones((half,), F32), jnp.ones((half,), F32), zeros]).reshape(1, LANE)
    return invf, sgn


def _even_fwd(x, pos, ln, gate, wt):
    h = _adaln(x, ln)
    p = _matmul(h, wt["w_in"], "nn", F32, name="ev_in")
    act, pre = _conv_fwd(p, EV_QKV, 1536, wt["conv_w"], jnp.zeros((1, 1536), F32), name="ev_conv")
    o_a, states = _gdn_fwd(act, p, wt["alog"], wt["dtb"], name="gdn_fwd")
    cqn, ckvn = _rows(_rms2_fn, [(p, EV_CQ, 384), (p, EV_CKV, 256)], [wt["gq"], wt["gkv"]],
                      [(384, BF16), (256, BF16)], name="mla_rms")
    q = _matmul(cqn, wt["w_uq"], "nn", F32, name="mla_uq")
    kv = _matmul(ckvn, wt["w_ukv"], "nn", F32, name="mla_ukv")
    invf, sgn = _rope_consts()
    qc, kc, vv = _rows(_rope_fn, [q, kv, (p, EV_MISC, LANE), pos], [invf, sgn],
                       [(1024, BF16), (1024, BF16), (512, BF16)], name="mla_rope", tm=ROW_TILE // 2)
    o_b, lse = _attn_fwd(qc, kc, vv, name="attn_fwd")
    (o,) = _rows(_ev_out_fn, [o_a, (p, EV_Z, 512), o_b], [wt["gdn_g"]], [(1024, BF16)], name="ev_out")
    y = _matmul(o, wt["w_out"], "nn", F32, name="ev_wout")
    return _resid(1.0, y, x, gate), (x, h, p, act, pre, states, cqn, ckvn, q, kv, qc, kc, vv, o_a, o_b, lse, o, y)


def _cat_fn(*parts):
    return (jnp.concatenate(parts, axis=-1),)


def _ev_dp_fn(dx0, dx1, dx2, dcq, dm_r, dm0, dm1, dm2, dm3, dz, dckv):
    return (jnp.concatenate([dx0, dx1, dx2, dcq, dm_r + ((dm0 + dm1) + (dm2 + dm3)), dz, dckv], axis=-1),)


def _even_bwd(saved, dxn, pos, ln, gate, wt):
    x, h, p, act, pre, states, cqn, ckvn, q, kv, qc, kc, vv, o_a, o_b, lse, o, y = saved
    g = {}
    dy, g["gate"] = _resid_bwd(1.0, y, x, gate, dxn)
    do = _matmul(dy, wt["w_out"], "nt", F32, name="ev_dwout_x")
    g["w_out"] = _matmul(o, dy, "tn", F32, name="ev_dwout_w")
    (d_oa, dz, d_ob), (g["gdn_g"],) = _rows_vjp(_ev_out_fn, [o_a, (p, EV_Z, 512), o_b], [], [wt["gdn_g"]], [], [do],
                                                [F32, F32, F32], name="ev_out_bwd")
    dqc = _attn_bwd_q(qc, kc, vv, o_b, lse, d_ob, name="attn_bwd_q")
    dkc, dvv = _attn_bwd_kv(qc, kc, vv, o_b, lse, d_ob, name="attn_bwd_kv")
    invf, sgn = _rope_consts()
    (dq, dkv, dm_r), _ = _rows_vjp(_rope_fn, [q, kv, (p, EV_MISC, LANE)], [pos], [], [invf, sgn], [dqc, dkc, dvv],
                                   [BF16, BF16, F32], name="mla_rope_bwd", tm=ROW_TILE // 4)
    dcqn = _matmul(dq, wt["w_uq"], "nt", F32, name="mla_duq_x")
    g["w_uq"] = _matmul(cqn, dq, "tn", F32, name="mla_duq_w")
    dckvn = _matmul(dkv, wt["w_ukv"], "nt", F32, name="mla_dukv_x")
    g["w_ukv"] = _matmul(ckvn, dkv, "tn", F32, name="mla_dukv_w")
    (dcq, dckv), (g["gq"], g["gkv"]) = _rows_vjp(_rms2_fn, [(p, EV_CQ, 384), (p, EV_CKV, 256)], [],
                                                 [wt["gq"], wt["gkv"]], [], [dcqn, dckvn], [F32, F32], name="mla_rms_bwd")
    dq_g, dk_g, dv_g, dm_g, g["alog"], g["dtb"] = _gdn_bwd(act, p, wt["alog"], wt["dtb"], states, d_oa, name="gdn_bwd")
    dxs, dws = [], []
    for j, d in enumerate((dq_g, dk_g, dv_g)):
        dxj, dwj, _ = _conv_bwd(d, pre, 512 * j, p, EV_QKV + 512 * j, wt["conv_w"], name="ev_conv_bwd")
        dxs.append(dxj)
        dws.append(dwj)
    g["conv_w"] = jnp.concatenate(dws, axis=1)
    (dp,) = _rows(_ev_dp_fn, dxs + [dcq, dm_r] + [dm_g[i] for i in range(GDN_HEADS)] + [dz, dckv], [],
                  [(EV_W, BF16)], name="ev_dp", tm=ROW_TILE // 2)
    dh = _matmul(dp, wt["w_in"], "nt", F32, name="ev_din_x")
    g["w_in"] = _matmul(h, dp, "tn", F32, name="ev_din_w")
    dx, g["ln"] = _adaln_bwd(x, ln, dh, dxn)
    return dx, g


def _odd_fwd(x, ln, gate, wt):
    h = _adaln(x, ln)
    p = _matmul(h, wt["w_in"], "nn", F32, name="od_in")
    act, pre = _conv_fwd(p, OD_XBC, 3072, wt["conv_w"], wt["conv_b"], name="od_conv")
    ys, states = _ssd_fwd(act, p, wt["alog"], wt["dtb"], wt["dsk"], name="ssd_fwd")
    (o,) = _rows(_od_out_fn, [ys, (p, OD_Z, 2048)], [wt["norm_g"]], [(SSD_D_INNER, BF16)], name="od_out",
                 tm=ROW_TILE // 2)
    y = _matmul(o, wt["w_out"], "nn", F32, name="od_wout")
    return _resid(1.0, y, x, gate), (x, h, p, act, pre, states, ys, o, y)


def _od_dp_fn(dz, dxx, dxb, dxc, ddt):
    return (jnp.concatenate([dz, dxx, dxb, dxc, ddt, jnp.zeros_like(ddt)], axis=-1),)


def _odd_bwd(saved, dxn, ln, gate, wt):
    x, h, p, act, pre, states, ys, o, y = saved
    g = {}
    dy, g["gate"] = _resid_bwd(1.0, y, x, gate, dxn)
    do = _matmul(dy, wt["w_out"], "nt", F32, name="od_dwout_x")
    g["w_out"] = _matmul(o, dy, "tn", F32, name="od_dwout_w")
    (dys, dz), (g["norm_g"],) = _rows_vjp(_od_out_fn, [ys, (p, OD_Z, 2048)], [], [wt["norm_g"]], [], [do], [F32, F32],
                                          name="od_out_bwd", tm=ROW_TILE // 4)
    dxs, dbm, dcm, ddt, g["alog"], g["dtb"], g["dsk"] = _ssd_bwd(act, p, wt["alog"], wt["dtb"], wt["dsk"], states, dys,
                                                                 name="ssd_bwd")
    dins, dws, dbs = [], [], []
    for d, c0 in ((dxs, 0), (dbm, 2048), (dcm, 2560)):
        dxj, dwj, dbj = _conv_bwd(d, pre, c0, p, OD_XBC + c0, wt["conv_w"], name="od_conv_bwd")
        dins.append(dxj)
        dws.append(dwj)
        dbs.append(dbj)
    g["conv_w"] = jnp.concatenate(dws, axis=1)
    g["conv_b"] = jnp.concatenate(dbs, axis=1)
    (dp,) = _rows(_od_dp_fn, [dz] + dins + [ddt], [], [(OD_W, BF16)], name="od_dp", tm=ROW_TILE // 4)
    dh = _matmul(dp, wt["w_in"], "nt", F32, name="od_din_x")
    g["w_in"] = _matmul(h, dp, "tn", F32, name="od_din_w")
    dx, g["ln"] = _adaln_bwd(x, ln, dh, dxn)
    return dx, g


def _local_step(x, tgt, pos, mod, fw):
    mod = mod.reshape(DEPTH, 3, 3, 1, D_MODEL)

    def ln_of(l, i):
        return (_row(fw["norm_g"][l, i]), mod[l, i, 0], mod[l, i, 1])

    def ffn_w(l, j):
        return _ffn_w13(fw["ffn_w1"][l, j], fw["ffn_w3"][l, j]), fw["ffn_w2"][l, j]

    def mixer_w(l):
        e = l // 2
        if l % 2 == 0:
            return dict(w_in=_ev_w_in(fw["ev_w_in"][e]), conv_w=fw["gdn_conv_w"][e].astype(F32),
                        alog=_lane_vec(fw["gdn_A_log"][e]), dtb=_lane_vec(fw["gdn_dt_bias"][e]),
                        gdn_g=_row(fw["gdn_norm_g"][e]), gq=_row(fw["mla_q_norm_g"][e]), gkv=_row(fw["mla_kv_norm_g"][e]),
                        w_uq=_uq(fw["mla_w_uq"][e]), w_ukv=_ukv(fw["mla_w_ukv"][e]), w_out=fw["ev_w_out"][e])
        w_in = fw["ssd_w_in"][e]
        return dict(w_in=jnp.pad(w_in, ((0, 0), (0, OD_W - w_in.shape[1]))), conv_w=fw["ssd_conv_w"][e].astype(F32),
                    conv_b=_row(fw["ssd_conv_b"][e]), alog=_lane_vec(fw["ssd_A_log"][e]),
                    dtb=_lane_vec(fw["ssd_dt_bias"][e]), dsk=_lane_vec(fw["ssd_D"][e]),
                    norm_g=_row(fw["ssd_norm_g"][e]), w_out=fw["ssd_w_out"][e])

    saved = []
    for l in range(DEPTH):
        x, s0 = _ffn_fwd(x, ln_of(l, 0), mod[l, 0, 2], *ffn_w(l, 0))
        if l % 2 == 0:
            x, s1 = _even_fwd(x, pos, ln_of(l, 1), mod[l, 1, 2], mixer_w(l))
        else:
            x, s1 = _odd_fwd(x, ln_of(l, 1), mod[l, 1, 2], mixer_w(l))
        x, s2 = _ffn_fwd(x, ln_of(l, 2), mod[l, 2, 2], *ffn_w(l, 1))
        saved.append((s0, s1, s2))

    loss, dx, d_final_g = _loss_bwd(x, tgt, _row(fw["final_g"]), name="loss")

    gl = {k: [None] * DEPTH for k in ("norm_g", "dmod", "ffn_w1", "ffn_w3", "ffn_w2")}
    ge = {k: [None] * (DEPTH // 2) for k in ("ev_w_in", "gdn_conv_w", "gdn_A_log", "gdn_dt_bias", "gdn_norm_g",
                                              "mla_q_norm_g", "mla_w_uq", "mla_kv_norm_g", "mla_w_ukv", "ev_w_out")}
    go = {k: [None] * (DEPTH // 2) for k in ("ssd_w_in", "ssd_conv_w", "ssd_conv_b", "ssd_A_log", "ssd_dt_bias", "ssd_D",
                                              "ssd_norm_g", "ssd_w_out")}
    for l in reversed(range(DEPTH)):
        s0, s1, s2 = saved[l]
        e = l // 2
        dg, dsh, dsc, dgt = [None] * 3, [None] * 3, [None] * 3, [None] * 3
        w1s, w3s, w2s = [None] * 2, [None] * 2, [None] * 2
        dx, dw13, w2s[1], (dg[2], dsh[2], dsc[2]), dgt[2] = _ffn_bwd(s2, dx, ln_of(l, 2), mod[l, 2, 2], *ffn_w(l, 1))
        w1s[1], w3s[1] = _ffn_w13_back(dw13)
        if l % 2 == 0:
            dx, g = _even_bwd(s1, dx, pos, ln_of(l, 1), mod[l, 1, 2], mixer_w(l))
            ge["ev_w_in"][e] = _ev_w_in_back(g["w_in"])
            ge["gdn_conv_w"][e] = g["conv_w"]
            ge["gdn_A_log"][e] = g["alog"][0, :GDN_HEADS]
            ge["gdn_dt_bias"][e] = g["dtb"][0, :GDN_HEADS]
            ge["gdn_norm_g"][e] = g["gdn_g"][0]
            ge["mla_q_norm_g"][e] = g["gq"][0]
            ge["mla_w_uq"][e] = _uq_back(g["w_uq"])
            ge["mla_kv_norm_g"][e] = g["gkv"][0]
            ge["mla_w_ukv"][e] = _ukv_back(g["w_ukv"])
            ge["ev_w_out"][e] = g["w_out"]
        else:
            dx, g = _odd_bwd(s1, dx, ln_of(l, 1), mod[l, 1, 2], mixer_w(l))
            go["ssd_w_in"][e] = g["w_in"][:, :fw["ssd_w_in"].shape[2]]
            go["ssd_conv_w"][e] = g["conv_w"]
            go["ssd_conv_b"][e] = g["conv_b"][0]
            go["ssd_A_log"][e] = g["alog"][0, :SSD_HEADS]
            go["ssd_dt_bias"][e] = g["dtb"][0, :SSD_HEADS]
            go["ssd_D"][e] = g["dsk"][0, :SSD_HEADS]
            go["ssd_norm_g"][e] = g["norm_g"][0]
            go["ssd_w_out"][e] = g["w_out"]
        dg[1], dsh[1], dsc[1] = g["ln"]
        dgt[1] = g["gate"]
        dx, dw13, w2s[0], (dg[0], dsh[0], dsc[0]), dgt[0] = _ffn_bwd(s0, dx, ln_of(l, 0), mod[l, 0, 2], *ffn_w(l, 0))
        w1s[0], w3s[0] = _ffn_w13_back(dw13)
        gl["norm_g"][l] = jnp.concatenate(dg, axis=0)
        gl["dmod"][l] = jnp.concatenate([jnp.concatenate([dsh[i], dsc[i], dgt[i]], axis=1) for i in range(3)], axis=1)[0]
        gl["ffn_w1"][l], gl["ffn_w3"][l], gl["ffn_w2"][l] = jnp.stack(w1s), jnp.stack(w3s), jnp.stack(w2s)

    grads = {k: jnp.stack(v) for k, v in {**gl, **ge, **go}.items()}
    grads["final_g"] = d_final_g[0]
    dmod = grads.pop("dmod")
    return loss, dx, grads, dmod


_WEIGHTS = ("ada_w", "ada_b", "norm_g", "ffn_w1", "ffn_w3", "ffn_w2", "ev_w_in", "gdn_conv_w", "gdn_A_log", "gdn_dt_bias",
            "gdn_norm_g", "mla_q_norm_g", "mla_w_uq", "mla_kv_norm_g", "mla_w_ukv", "ev_w_out", "ssd_w_in", "ssd_conv_w",
            "ssd_conv_b", "ssd_A_log", "ssd_dt_bias", "ssd_D", "ssd_norm_g", "ssd_w_out", "final_g")
_BIG = {"ffn_w1": 3, "ffn_w3": 3, "ffn_w2": 2, "ev_w_in": 2, "mla_w_uq": 1, "mla_w_ukv": 1, "ev_w_out": 1, "ssd_w_in": 2,
        "ssd_w_out": 1}
_SMALL = {"norm_g": 2, "gdn_conv_w": 2, "ssd_conv_w": 2, "ssd_conv_b": 1, "ssd_norm_g": 1}
_REPL = ("ada_b", "gdn_A_log", "gdn_dt_bias", "gdn_norm_g", "mla_q_norm_g", "mla_kv_norm_g", "ssd_A_log", "ssd_dt_bias",
         "ssd_D", "final_g")


def _join(pieces, axis):
    moved = jnp.moveaxis(pieces, 0, axis)
    shape = moved.shape
    return moved.reshape(shape[:axis] + (shape[axis] * shape[axis + 1],) + shape[axis + 2:])


def _split(full, axis):
    shape = full.shape
    return jnp.moveaxis(full.reshape(shape[:axis] + (N_DEV, shape[axis] // N_DEV) + shape[axis + 1:]), axis, 0)


def kernel(x, c, positions, ada_w, ada_b, norm_g, ffn_w1, ffn_w3, ffn_w2, ev_w_in, gdn_conv_w, gdn_A_log, gdn_dt_bias, gdn_norm_g, mla_q_norm_g, mla_w_uq, mla_kv_norm_g, mla_w_ukv, ev_w_out, ssd_w_in, ssd_conv_w, ssd_conv_b, ssd_A_log, ssd_dt_bias, ssd_D, ssd_norm_g, ssd_w_out, final_g, loss_target, m_ada_w, m_ada_b, m_norm_g, m_ffn_w1, m_ffn_w3, m_ffn_w2, m_ev_w_in, m_gdn_conv_w, m_gdn_A_log, m_gdn_dt_bias, m_gdn_norm_g, m_mla_q_norm_g, m_mla_w_uq, m_mla_kv_norm_g, m_mla_w_ukv, m_ev_w_out, m_ssd_w_in, m_ssd_conv_w, m_ssd_conv_b, m_ssd_A_log, m_ssd_dt_bias, m_ssd_D, m_ssd_norm_g, m_ssd_w_out, m_final_g, v_ada_w, v_ada_b, v_norm_g, v_ffn_w1, v_ffn_w3, v_ffn_w2, v_ev_w_in, v_gdn_conv_w, v_gdn_A_log, v_gdn_dt_bias, v_gdn_norm_g, v_mla_q_norm_g, v_mla_w_uq, v_mla_kv_norm_g, v_mla_w_ukv, v_ev_w_out, v_ssd_w_in, v_ssd_conv_w, v_ssd_conv_b, v_ssd_A_log, v_ssd_dt_bias, v_ssd_D, v_ssd_norm_g, v_ssd_w_out, v_final_g):
    a = dict(locals())
    w = {n: a[n] for n in _WEIGHTS}
    m = {n: a["m_" + n] for n in _WEIGHTS}
    v = {n: a["v_" + n] for n in _WEIGHTS}
    mx, my, mc = _mesh_pos()
    me = 4 * mx + 2 * my + mc
    t_len = x.shape[1]

    small_names, big_names = list(_SMALL), list(_BIG)
    small = _exchange(_pack([c] + [w[n] for n in small_names], F32), False, name="gather_small")
    parts = _unpack(small, [c.shape] + [w[n].shape for n in small_names], lead=(N_DEV,))
    c_all = parts[0].reshape(N_DEV, D_MODEL)
    fw = {n: _join(p, _SMALL[n]) for n, p in zip(small_names, parts[1:])}
    big = _exchange(_pack([w[n] for n in big_names], BF16), False, name="gather_big")
    for n, p in zip(big_names, _unpack(big, [w[n].shape for n in big_names], lead=(N_DEV,))):
        fw[n] = _join(p, _BIG[n])
    for n in _REPL:
        fw[n] = w[n]

    n_col = ada_w.shape[2]
    ada_b_shard = lax.dynamic_slice(ada_b, (0, me * n_col), (DEPTH, n_col)).reshape(DEPTH, 1, n_col)
    mod_all = _exchange(_mod_shard(c_all, ada_w, ada_b_shard, name="mod"), False, name="gather_mod")
    mod_me = lax.dynamic_index_in_dim(mod_all, me, axis=2, keepdims=False)
    mod = jnp.transpose(mod_me, (1, 0, 2)).reshape(DEPTH, N_DEV * n_col)

    pos = positions.astype(F32).reshape(t_len, 1)
    loss, dx, grads, dmod = _local_step(x[0], loss_target[0], pos, mod, fw)

    repl_shapes = [w[n].shape for n in _REPL] + [(1,)]
    parts8 = _exchange(_pack([dmod] + [grads[n] for n in _REPL[1:]] + [loss[0, :1]], F32), False, name="gather_repl")
    zero = jnp.zeros((1,), F32)
    r_grad, r_delta, r_m, r_v = [
        _unpack(o, repl_shapes) for o in _adamw(_pack([w[n] for n in _REPL] + [zero], F32), parts8,
                                                _pack([m[n] for n in _REPL] + [zero], F32),
                                                _pack([v[n] for n in _REPL] + [zero], F32), name="adamw_repl")]
    out = {"grad": {}, "delta": {}, "m": {}, "v": {}}
    for i, n in enumerate(_REPL):
        out["grad"][n], out["delta"][n], out["m"][n], out["v"][n] = r_grad[i], r_delta[i], r_m[i], r_v[i]
    loss_total = r_grad[-1].reshape(())

    dmod_all = _unpack(parts8, [dmod.shape], lead=(N_DEV,))[0]
    dmod_cols = jnp.transpose(lax.dynamic_slice_in_dim(dmod_all, me * n_col, n_col, axis=2), (1, 0, 2))
    g_ada = _ada_w_grad(c_all, dmod_cols, name="ada_w_grad")
    ada_out = _adamw(ada_w.reshape(-1, PACK_W), g_ada.reshape(1, -1, PACK_W), m["ada_w"].reshape(-1, PACK_W),
                     v["ada_w"].reshape(-1, PACK_W), name="adamw_ada")
    for k, o in zip(("grad", "delta", "m", "v"), ada_out):
        out[k]["ada_w"] = o.reshape(ada_w.shape)

    sh_names = small_names + big_names
    axis = {**_SMALL, **_BIG}
    sh_shapes = [w[n].shape for n in sh_names]
    scat = _exchange(_pack([_split(grads[n], axis[n]) for n in sh_names], F32, lead=(N_DEV,)), True, name="scatter_grads")
    sh_out = _adamw(_pack([w[n] for n in sh_names], F32), scat, _pack([m[n] for n in sh_names], F32),
                    _pack([v[n] for n in sh_names], F32), name="adamw_shard")
    for k, o in zip(("grad", "delta", "m", "v"), sh_out):
        for n, val in zip(sh_names, _unpack(o, sh_shapes)):
            out[k][n] = val

    return (loss_total, dx.reshape(x.shape), *[out["grad"][n] for n in _WEIGHTS], *[out["delta"][n] for n in _WEIGHTS],
            *[out["m"][n] for n in _WEIGHTS], *[out["v"][n] for n in _WEIGHTS])
```

```python
import functools
import math

import numpy as np
import jax
import jax.numpy as jnp
from jax import lax
from jax.experimental import pallas as pl
from jax.experimental.pallas import tpu as pltpu

F32 = jnp.float32
BF16 = jnp.bfloat16
HI = lax.Precision.HIGHEST

D_MODEL = 1024
DEPTH = 4
CHUNK = 64
NORM_EPS = 1e-6
CONV_K = 4
D_FF = 2816
GDN_HEADS = 4
GDN_DK = 128
MLA_HEADS = 4
MLA_NOPE = 128
MLA_ROPE = 64
ROPE_THETA = 10000.0
SSD_HEADS = 32
SSD_HEADDIM = 64
SSD_GROUPS = 4
SSD_STATE = 128
SSD_D_INNER = 2048
N_DEV = 8

ADAM_LR = 0.001
ADAM_B1 = 0.9
ADAM_B2 = 0.999
ADAM_EPS = 1e-08
ADAM_WD = 0.01
ADAM_STEP = 10

V7X_VMEM_LIMIT = 56 * 1024 * 1024
ROW_TILE = 512
SEQ_TILE = 128
ATT_TILE = 1024
MM_RESIDENT_BYTES = 12 * 1024 * 1024
FF_HALF = D_FF // 2
LANE = 128

EV_QKV, EV_CQ, EV_MISC, EV_Z, EV_CKV, EV_W = 0, 1536, 1920, 2048, 2560, 2816
OD_Z, OD_XBC, OD_DT, OD_W = 0, 2048, 5120, 5376


def _cparams(sem=None):
    return pltpu.CompilerParams(dimension_semantics=sem, vmem_limit_bytes=V7X_VMEM_LIMIT)


def _pick(n, cands):
    for c in cands:
        if n % c == 0:
            return c
    return n


_DN = {"nn": (((1,), (0,)), ((), ())), "nt": (((1,), (1,)), ((), ())), "tn": (((0,), (0,)), ((), ()))}


def _dot(a, b, mode, hi=False):
    if hi:
        return lax.dot_general(a.astype(F32), b.astype(F32), _DN[mode], precision=HI, preferred_element_type=F32)
    return lax.dot_general(a.astype(BF16), b.astype(BF16), _DN[mode], preferred_element_type=F32)


@functools.partial(jax.custom_vjp, nondiff_argnums=(2, 3))
def _mm(a, b, mode, hi):
    return _dot(a, b, mode, hi)


def _mm_fwd(a, b, mode, hi):
    return _dot(a, b, mode, hi), (a, b)


def _mm_bwd(mode, hi, res, g):
    a, b = res
    if mode == "nn":
        return _dot(g, b, "nt", hi), _dot(a, g, "tn", hi)
    if mode == "nt":
        return _dot(g, b, "nn", hi), _dot(g, a, "tn", hi)
    return _dot(b, g, "nt", hi), _dot(a, g, "nn", hi)


_mm.defvjp(_mm_fwd, _mm_bwd)


def _iota2(shape, dim):
    return lax.broadcasted_iota(jnp.int32, shape, dim)


def _row_spec(a, tm):
    if isinstance(a, tuple):
        arr, c0, w = a
        assert c0 % w == 0
        cb = c0 // w
        return arr, pl.BlockSpec((tm, w), lambda i, cb=cb: (i, cb))
    return a, pl.BlockSpec((tm, a.shape[1]), lambda i: (i, 0))


def _full_spec(b):
    return pl.BlockSpec(b.shape, lambda i: (0,) * b.ndim)


def _rows(fn, tiled, bcast, outs, *, name, tm=ROW_TILE):
    arrs, specs = zip(*[_row_spec(a, 0) for a in tiled])
    t_len = arrs[0].shape[0]
    tm = min(tm, t_len)
    arrs, specs = zip(*[_row_spec(a, tm) for a in tiled])
    nt, nb = len(tiled), len(bcast)

    def body(*refs):
        ins = [r[...].astype(F32) for r in refs[:nt]] + [r[...] for r in refs[nt:nt + nb]]
        res = fn(*ins)
        for r, v in zip(refs[nt + nb:], res):
            r[...] = v.astype(r.dtype)

    return pl.pallas_call(
        body, grid=(t_len // tm,), name=name,
        in_specs=list(specs) + [_full_spec(b) for b in bcast],
        out_specs=[pl.BlockSpec((tm, c), lambda i: (i, 0)) for c, _ in outs],
        out_shape=[jax.ShapeDtypeStruct((t_len, c), dt) for c, dt in outs],
        compiler_params=_cparams(("parallel",)),
    )(*arrs, *bcast)


def _rows_vjp(fn, tiled, consts, bcast, bconsts, douts, grads, *, name, adds=None, tm=ROW_TILE // 2):
    adds = adds or {}
    t_arrs, t_specs = zip(*[_row_spec(a, 0) for a in tiled])
    t_len = t_arrs[0].shape[0]
    tm = min(tm, t_len)
    rows_in = list(tiled) + list(consts) + list(douts) + [adds[k] for k in sorted(adds)]
    arrs, specs = zip(*[_row_spec(a, tm) for a in rows_in])
    nt, nc, nb, nbc, nd, na = len(tiled), len(consts), len(bcast), len(bconsts), len(douts), len(adds)
    add_pos = {k: j for j, k in enumerate(sorted(adds))}
    want = [j for j, g in enumerate(grads) if g is not None]

    def body(*refs):
        p = 0
        t = [r[...].astype(F32) for r in refs[p:p + nt]]; p += nt
        c = [r[...].astype(F32) for r in refs[p:p + nc]]; p += nc
        d = [r[...].astype(F32) for r in refs[p:p + nd]]; p += nd
        a = [r[...].astype(F32) for r in refs[p:p + na]]; p += na
        b = [r[...] for r in refs[p:p + nb]]; p += nb
        bc = [r[...] for r in refs[p:p + nbc]]; p += nbc
        g_refs = refs[p:p + len(want)]; p += len(want)
        gb_refs = refs[p:p + nb]

        def f(*args):
            return fn(*args[:nt], *c, *args[nt:], *bc)

        _, vjp = jax.vjp(f, *t, *b)
        g = vjp(tuple(d))
        for r, j in zip(g_refs, want):
            val = g[j]
            if j in add_pos:
                val = val + a[add_pos[j]]
            r[...] = val.astype(r.dtype)

        @pl.when(pl.program_id(0) == 0)
        def _():
            for r in gb_refs:
                r[...] = jnp.zeros_like(r)

        for r, val in zip(gb_refs, g[nt:]):
            r[...] += val

    def width(a):
        return a[2] if isinstance(a, tuple) else a.shape[1]

    res = pl.pallas_call(
        body, grid=(t_len // tm,), name=name,
        in_specs=list(specs) + [_full_spec(b) for b in list(bcast) + list(bconsts)],
        out_specs=[pl.BlockSpec((tm, width(tiled[j])), lambda i: (i, 0)) for j in want] + [_full_spec(b) for b in bcast],
        out_shape=[jax.ShapeDtypeStruct((t_len, width(tiled[j])), grads[j]) for j in want]
        + [jax.ShapeDtypeStruct(b.shape, F32) for b in bcast],
        compiler_params=_cparams(("arbitrary",)),
    )(*arrs, *bcast, *bconsts)
    tg = [None] * nt
    for r, j in zip(res[:len(want)], want):
        tg[j] = r
    return tg, list(res[len(want):])


def _matmul(a, b, mode, out_dtype, *, name):
    if mode == "tn":
        assert out_dtype == F32
        k_len, m_len = a.shape
        n_len = b.shape[1]
        tm, tn = m_len, n_len
        while tm * tn * 4 > MM_RESIDENT_BYTES and tn % (2 * LANE) == 0:
            tn //= 2
        tk = _pick(k_len, (512, 256, 128))
    else:
        m_len, k_len = a.shape
        n_len = b.shape[1] if mode == "nn" else b.shape[0]
        tk, tn = k_len, n_len
        while tk * tn * 2 > MM_RESIDENT_BYTES and tn % (2 * LANE) == 0:
            tn //= 2
        tm = _pick(m_len, (512, 256, 128))
        while tm * max(4 * tn, 2 * tk) > MM_RESIDENT_BYTES // 2 and tm % 256 == 0:
            tm //= 2
    nk = k_len // tk
    if mode == "nn":
        a_spec = pl.BlockSpec((tm, tk), lambda j, i, k: (i, k))
        b_spec = pl.BlockSpec((tk, tn), lambda j, i, k: (k, j))
    elif mode == "nt":
        a_spec = pl.BlockSpec((tm, tk), lambda j, i, k: (i, k))
        b_spec = pl.BlockSpec((tn, tk), lambda j, i, k: (j, k))
    else:
        a_spec = pl.BlockSpec((tk, tm), lambda j, i, k: (k, i))
        b_spec = pl.BlockSpec((tk, tn), lambda j, i, k: (k, j))

    def body(a_ref, b_ref, o_ref):
        part = _dot(a_ref[...], b_ref[...], mode)
        if nk == 1:
            o_ref[...] = part.astype(o_ref.dtype)
        else:
            @pl.when(pl.program_id(2) == 0)
            def _():
                o_ref[...] = jnp.zeros_like(o_ref)

            o_ref[...] += part

    return pl.pallas_call(
        body, grid=(n_len // tn, m_len // tm, nk), name=name,
        in_specs=[a_spec, b_spec],
        out_specs=pl.BlockSpec((tm, tn), lambda j, i, k: (i, j)),
        out_shape=jax.ShapeDtypeStruct((m_len, n_len), out_dtype),
        compiler_params=_cparams(("parallel", "parallel", "arbitrary")),
    )(a, b)


def _ffn_act(h, w13, *, name):
    t_len, d = h.shape
    tm = min(ROW_TILE, t_len)

    def body(h_ref, w_ref, s_ref):
        ab = _dot(h_ref[...], w_ref[...], "nn")
        a, b = ab[:, :FF_HALF], ab[:, FF_HALF:]
        s_ref[...] = (a * jax.nn.sigmoid(a) * b).astype(s_ref.dtype)

    return pl.pallas_call(
        body, grid=(2, t_len // tm), name=name,
        in_specs=[pl.BlockSpec((tm, d), lambda f, i: (i, 0)), pl.BlockSpec((d, 2 * FF_HALF), lambda f, i: (0, f))],
        out_specs=pl.BlockSpec((tm, FF_HALF), lambda f, i: (i, f)),
        out_shape=jax.ShapeDtypeStruct((t_len, D_FF), BF16),
        compiler_params=_cparams(("parallel", "parallel")),
    )(h, w13)


def _ffn_act_bwd(h, dy, w13, w2, *, name):
    t_len, d = h.shape
    tm = min(ROW_TILE, t_len)

    def body(h_ref, dy_ref, w_ref, w2_ref, o_ref):
        ab = _dot(h_ref[...], w_ref[...], "nn")
        a, b = ab[:, :FF_HALF], ab[:, FF_HALF:]
        ds = _dot(dy_ref[...], w2_ref[...], "nt")
        sig = jax.nn.sigmoid(a)
        silu = a * sig
        da = ds * b * (sig * (1.0 + a * (1.0 - sig)))
        db = ds * silu
        o_ref[...] = jnp.concatenate([da, db], axis=-1).astype(o_ref.dtype)

    return pl.pallas_call(
        body, grid=(2, t_len // tm), name=name,
        in_specs=[pl.BlockSpec((tm, d), lambda f, i: (i, 0)), pl.BlockSpec((tm, d), lambda f, i: (i, 0)),
                  pl.BlockSpec((d, 2 * FF_HALF), lambda f, i: (0, f)), pl.BlockSpec((FF_HALF, d), lambda f, i: (f, 0))],
        out_specs=pl.BlockSpec((tm, 2 * FF_HALF), lambda f, i: (i, f)),
        out_shape=jax.ShapeDtypeStruct((t_len, 2 * D_FF), BF16),
        compiler_params=_cparams(("parallel", "parallel")),
    )(h, dy, w13, w2)


CONV_CB = 512
HALO = 8


def _conv_fwd(p, c0, n_ch, w, b, *, name):
    t_len = p.shape[0]
    tm = min(ROW_TILE, t_len)
    hb = tm // HALO
    cb0 = c0 // CONV_CB

    def body(x_ref, halo_ref, w_ref, b_ref, act_ref, pre_ref):
        first = pl.program_id(1) == 0
        halo = jnp.where(first, 0.0, halo_ref[...])
        xx = jnp.concatenate([halo, x_ref[...]], axis=0)
        wv = w_ref[...]
        acc = b_ref[...] + wv[0:1] * xx[HALO - 3:HALO - 3 + tm]
        for j in range(1, CONV_K):
            acc = acc + wv[j:j + 1] * xx[HALO - 3 + j:HALO - 3 + j + tm]
        pre_ref[...] = acc
        act_ref[...] = acc * jax.nn.sigmoid(acc)

    return pl.pallas_call(
        body, grid=(n_ch // CONV_CB, t_len // tm), name=name,
        in_specs=[pl.BlockSpec((tm, CONV_CB), lambda j, i: (i, cb0 + j)),
                  pl.BlockSpec((HALO, CONV_CB), lambda j, i: (jnp.maximum(i * hb - 1, 0), cb0 + j)),
                  pl.BlockSpec((CONV_K, CONV_CB), lambda j, i: (0, j)),
                  pl.BlockSpec((1, CONV_CB), lambda j, i: (0, j))],
        out_specs=[pl.BlockSpec((tm, CONV_CB), lambda j, i: (i, j))] * 2,
        out_shape=[jax.ShapeDtypeStruct((t_len, n_ch), F32)] * 2,
        compiler_params=_cparams(("parallel", "arbitrary")),
    )(p, p, w, b)


def _conv_bwd(dact, pre, pre_c0, p, p_c0, w, *, name):
    t_len, n_ch = dact.shape
    tm = min(ROW_TILE, t_len)
    hb = tm // HALO
    nt = t_len // tm
    last_hb = t_len // HALO - 1
    cb0 = p_c0 // CONV_CB
    cbp = pre_c0 // CONV_CB

    def dsilu(z):
        sig = jax.nn.sigmoid(z)
        return sig * (1.0 + z * (1.0 - sig))

    def body(d_ref, dn_ref, pre_ref, pren_ref, x_ref, xh_ref, w_ref, dx_ref, dw_ref, db_ref):
        i = pl.program_id(1)
        dpre = d_ref[...] * dsilu(pre_ref[...])
        dnext = jnp.where(i == nt - 1, 0.0, dn_ref[...] * dsilu(pren_ref[...]))
        ext = jnp.concatenate([dpre, dnext], axis=0)
        xx = jnp.concatenate([jnp.where(i == 0, 0.0, xh_ref[...]), x_ref[...]], axis=0)
        wv = w_ref[...]
        dx = wv[0:1] * ext[3:3 + tm]
        for j in range(1, CONV_K):
            dx = dx + wv[j:j + 1] * ext[3 - j:3 - j + tm]
        dx_ref[...] = dx
        dws = [jnp.sum(dpre * xx[HALO - 3 + j:HALO - 3 + j + tm], axis=0, keepdims=True) for j in range(CONV_K)]

        @pl.when(i == 0)
        def _():
            dw_ref[...] = jnp.zeros_like(dw_ref)
            db_ref[...] = jnp.zeros_like(db_ref)

        dw_ref[...] += jnp.concatenate(dws, axis=0)
        db_ref[...] += jnp.sum(dpre, axis=0, keepdims=True)

    tile = lambda off: pl.BlockSpec((tm, CONV_CB), lambda j, i: (i, off + j))
    nxt = lambda off: pl.BlockSpec((HALO, CONV_CB), lambda j, i: (jnp.minimum((i + 1) * hb, last_hb), off + j))
    return pl.pallas_call(
        body, grid=(n_ch // CONV_CB, nt), name=name,
        in_specs=[tile(0), nxt(0), tile(cbp), nxt(cbp), tile(cb0),
                  pl.BlockSpec((HALO, CONV_CB), lambda j, i: (jnp.maximum(i * hb - 1, 0), cb0 + j)),
                  pl.BlockSpec((CONV_K, CONV_CB), lambda j, i: (0, cbp + j))],
        out_specs=[tile(0), pl.BlockSpec((CONV_K, CONV_CB), lambda j, i: (0, j)), pl.BlockSpec((1, CONV_CB), lambda j, i: (0, j))],
        out_shape=[jax.ShapeDtypeStruct((t_len, n_ch), F32), jax.ShapeDtypeStruct((CONV_K, n_ch), F32),
                   jax.ShapeDtypeStruct((1, n_ch), F32)],
        compiler_params=_cparams(("parallel", "arbitrary")),
    )(dact, dact, pre, pre, p, p, w)


@jax.custom_vjp
def _inv_unit_lower(a_mat):
    c = a_mat.shape[0]
    a_t = a_mat.T
    rows, cols = _iota2((c, c), 0), _iota2((1, c), 1)
    x = jnp.zeros((c, c), F32)
    for i in range(c):
        r = (cols == i).astype(F32)
        if i:
            r = r - jnp.sum(a_t[:, i:i + 1] * x, axis=0, keepdims=True)
        x = jnp.where(rows == i, r, x)
    return x


def _inv_fwd(a_mat):
    x = _inv_unit_lower(a_mat)
    return x, x


def _inv_bwd(x, g):
    return (-_dot(x, _dot(g, x, "nt", True), "tn", True),)


_inv_unit_lower.defvjp(_inv_fwd, _inv_bwd)


def _l2norm(x):
    return x * lax.rsqrt(jnp.sum(x * x, axis=-1, keepdims=True) + NORM_EPS)


def _rms(x):
    return x * lax.rsqrt(jnp.mean(x * x, axis=-1, keepdims=True) + NORM_EPS)


def _tri_masks(c):
    rows, cols = _iota2((c, c), 0), _iota2((c, c), 1)
    return rows >= cols, rows > cols, (rows >= cols).astype(F32), (rows <= cols).astype(F32)


def _gdn_tile(q, k, v, misc, s0, alog, dtb, head):
    c = CHUNK
    lane = _iota2((1, LANE), 1)
    alog_s = jnp.sum(jnp.where(lane == head, alog, 0.0), axis=-1, keepdims=True)
    dtb_s = jnp.sum(jnp.where(lane == head, dtb, 0.0), axis=-1, keepdims=True)
    lane_t = _iota2(misc.shape, 1)
    bl_all = jnp.sum(jnp.where(lane_t == 64 + head, misc, 0.0), axis=-1, keepdims=True)
    a_all = jnp.sum(jnp.where(lane_t == 68 + head, misc, 0.0), axis=-1, keepdims=True)
    lower, strict, ltri, utri = _tri_masks(c)
    s = s0
    outs = []
    for ci in range(q.shape[0] // c):
        sl = slice(ci * c, (ci + 1) * c)
        qn = _l2norm(q[sl]) * (GDN_DK ** -0.5)
        kn = _l2norm(k[sl])
        beta = jax.nn.sigmoid(bl_all[sl])
        g = -jnp.exp(alog_s) * jax.nn.softplus(a_all[sl] + dtb_s)
        gb = jnp.broadcast_to(g, (c, c))
        gc_col = _mm(ltri, gb, "nn", True)
        gc_row = _mm(gb, utri, "tn", True)
        decay = jnp.where(lower, jnp.exp(jnp.where(lower, gc_col - gc_row, 0.0)), 0.0)
        gc = gc_col[:, 0:1]
        g_last = gc_col[c - 1:c, 0:1]
        kb = kn * beta
        a_mat = jnp.where(strict, _mm(kb, kn, "nt", False) * decay, 0.0)
        t_inv = _inv_unit_lower(a_mat)
        u = _mm(t_inv, v[sl] * beta, "nn", True)
        w = _mm(t_inv, kb * jnp.exp(gc), "nn", True)
        attn = _mm(qn, kn, "nt", False) * decay
        k_end = kn * jnp.exp(g_last - gc)
        q_start = qn * jnp.exp(gc)
        v_new = u - _mm(w, s, "nn", False)
        outs.append(_mm(q_start, s, "nn", False) + _mm(attn, v_new, "nn", False))
        s = s * jnp.exp(g_last) + _mm(k_end, v_new, "tn", False)
    return jnp.concatenate(outs, axis=0), s


def _gdn_specs(tt, rev_n=None):
    t = (lambda i: i) if rev_n is None else (lambda i: rev_n - 1 - i)
    col = lambda off: pl.BlockSpec((tt, LANE), lambda h, i: (t(i), off + h))
    vec = pl.BlockSpec((1, LANE), lambda h, i: (0, 0))
    misc = pl.BlockSpec((tt, LANE), lambda h, i: (t(i), EV_MISC // LANE))
    return [col(0), col(GDN_HEADS), col(2 * GDN_HEADS), misc, vec, vec], t


def _gdn_fwd(act, p, alog, dtb, *, name):
    t_len = act.shape[0]
    tt = min(SEQ_TILE, t_len)
    ntile = t_len // tt
    in_specs, _ = _gdn_specs(tt)

    def body(q_ref, k_ref, v_ref, m_ref, al_ref, dt_ref, o_ref, s_ref, state):
        @pl.when(pl.program_id(1) == 0)
        def _():
            state[...] = jnp.zeros_like(state)

        s_ref[0, 0] = state[...]
        o, s_new = _gdn_tile(q_ref[...], k_ref[...], v_ref[...], m_ref[...], state[...], al_ref[...], dt_ref[...],
                             pl.program_id(0))
        o_ref[...] = o
        state[...] = s_new

    return pl.pallas_call(
        body, grid=(GDN_HEADS, ntile), name=name, in_specs=in_specs,
        out_specs=[pl.BlockSpec((tt, LANE), lambda h, i: (i, h)),
                   pl.BlockSpec((1, 1, GDN_DK, LANE), lambda h, i: (h, i, 0, 0))],
        out_shape=[jax.ShapeDtypeStruct((t_len, GDN_HEADS * LANE), F32),
                   jax.ShapeDtypeStruct((GDN_HEADS, ntile, GDN_DK, LANE), F32)],
        scratch_shapes=[pltpu.VMEM((GDN_DK, LANE), F32)],
        compiler_params=_cparams(("arbitrary", "arbitrary")),
    )(act, act, act, p, alog, dtb)


def _gdn_bwd(act, p, alog, dtb, states, do, *, name):
    t_len = act.shape[0]
    tt = min(SEQ_TILE, t_len)
    ntile = t_len // tt
    in_specs, t = _gdn_specs(tt, ntile)

    def body(q_ref, k_ref, v_ref, m_ref, al_ref, dt_ref, s0_ref, do_ref,
             dq_ref, dk_ref, dv_ref, dm_ref, dal_ref, ddt_ref, dstate):
        head = pl.program_id(0)

        @pl.when(pl.program_id(1) == 0)
        def _():
            dstate[...] = jnp.zeros_like(dstate)

        @pl.when((pl.program_id(1) == 0) & (head == 0))
        def _():
            dal_ref[...] = jnp.zeros_like(dal_ref)
            ddt_ref[...] = jnp.zeros_like(ddt_ref)

        def f(q, k, v, m, s0, al, dt):
            return _gdn_tile(q, k, v, m, s0, al, dt, head)

        _, vjp = jax.vjp(f, q_ref[...], k_ref[...], v_ref[...], m_ref[...], s0_ref[0, 0], al_ref[...], dt_ref[...])
        dq, dk, dv, dm, ds0, dal, ddt = vjp((do_ref[...], dstate[...]))
        dq_ref[...] = dq
        dk_ref[...] = dk
        dv_ref[...] = dv
        dm_ref[0] = dm
        dstate[...] = ds0
        dal_ref[...] += dal
        ddt_ref[...] += ddt

    row = pl.BlockSpec((tt, LANE), lambda h, i: (t(i), h))
    vec = pl.BlockSpec((1, LANE), lambda h, i: (0, 0))
    return pl.pallas_call(
        body, grid=(GDN_HEADS, ntile), name=name,
        in_specs=in_specs + [pl.BlockSpec((1, 1, GDN_DK, LANE), lambda h, i: (h, t(i), 0, 0)), row],
        out_specs=[row, row, row, pl.BlockSpec((1, tt, LANE), lambda h, i: (h, t(i), 0)), vec, vec],
        out_shape=[jax.ShapeDtypeStruct((t_len, GDN_HEADS * LANE), F32)] * 3
        + [jax.ShapeDtypeStruct((GDN_HEADS, t_len, LANE), F32)] + [jax.ShapeDtypeStruct((1, LANE), F32)] * 2,
        scratch_shapes=[pltpu.VMEM((GDN_DK, LANE), F32)],
        compiler_params=_cparams(("arbitrary", "arbitrary")),
    )(act, act, act, p, alog, dtb, states, do)


def _head_expand():
    return jnp.asarray(np.repeat(np.eye(LANE, SSD_HEADS, dtype=np.float32), SSD_HEADDIM, axis=1))


def _ssd_tile(xs, bm, cm, dtr, hs0, alog, dtb, dsk, expand):
    c = CHUNK
    gw = SSD_D_INNER // SSD_GROUPS
    hpg = SSD_HEADS // SSD_GROUPS
    lower, _, ltri, utri = _tri_masks(c)
    half = _iota2((c, LANE), 1) // SSD_HEADDIM
    dt = jax.nn.softplus(dtr + dtb)
    da = dt * (-jnp.exp(alog))
    xdt = xs * _mm(dt, expand, "nn", True)
    d_x = _mm(jnp.broadcast_to(dsk, (8, LANE)), expand, "nn", True)[0:1]
    hs = [hs0[g * SSD_STATE:(g + 1) * SSD_STATE] for g in range(SSD_GROUPS)]
    ys = []
    for ci in range(xs.shape[0] // c):
        sl = slice(ci * c, (ci + 1) * c)
        acs = _mm(ltri, da[sl], "nn", True)
        acs_t = _mm(da[sl], utri, "tn", True)
        acs_last = acs[c - 1:c, :]
        e_start = _mm(jnp.exp(acs), expand, "nn", True)
        e_end = _mm(jnp.exp(acs_last - acs), expand, "nn", True)
        e_dec = _mm(jnp.broadcast_to(jnp.exp(acs_last), (8, LANE)), expand, "nn", True)[0:1]
        xdt_c = xdt[sl]
        y_tiles = [None] * (SSD_D_INNER // LANE)
        y_off = []
        for g in range(SSD_GROUPS):
            b_g = bm[sl, g * SSD_STATE:(g + 1) * SSD_STATE]
            c_g = cm[sl, g * SSD_STATE:(g + 1) * SSD_STATE]
            gs = slice(g * gw, (g + 1) * gw)
            cb = _mm(c_g, b_g, "nt", False)
            y_off.append(_mm(c_g, hs[g], "nn", False) * e_start[:, gs])
            for r in range(hpg):
                h = g * hpg + r
                j = h // 2
                lm = jnp.where(lower, jnp.exp(jnp.where(lower, acs[:, h:h + 1] - acs_t[h:h + 1, :], 0.0)), 0.0)
                xm = jnp.where(half == (h % 2), xdt_c[:, j * LANE:(j + 1) * LANE], 0.0)
                part = _mm(cb * lm, xm, "nn", False)
                y_tiles[j] = part if y_tiles[j] is None else y_tiles[j] + part
            hs[g] = hs[g] * e_dec[:, gs] + _mm(b_g, xdt_c[:, gs] * e_end[:, gs], "tn", False)
        ys.append(jnp.concatenate(y_tiles, axis=-1) + jnp.concatenate(y_off, axis=-1) + d_x * xs[sl])
    return jnp.concatenate(ys, axis=0), jnp.concatenate(hs, axis=0)


def _ssd_specs(tt, rev_n=None):
    t = (lambda i: i) if rev_n is None else (lambda i: rev_n - 1 - i)
    vec = pl.BlockSpec((1, LANE), lambda i: (0, 0))
    specs = [pl.BlockSpec((tt, SSD_D_INNER), lambda i: (t(i), 0)),
             pl.BlockSpec((tt, 512), lambda i: (t(i), SSD_D_INNER // 512)),
             pl.BlockSpec((tt, 512), lambda i: (t(i), SSD_D_INNER // 512 + 1)),
             pl.BlockSpec((tt, LANE), lambda i: (t(i), OD_DT // LANE)), vec, vec, vec,
             pl.BlockSpec((LANE, SSD_D_INNER), lambda i: (0, 0))]
    return specs, t


def _ssd_fwd(act, p, alog, dtb, dsk, *, name):
    t_len = act.shape[0]
    tt = min(SEQ_TILE, t_len)
    ntile = t_len // tt
    in_specs, _ = _ssd_specs(tt)

    def body(x_ref, b_ref, c_ref, dt_ref, al_ref, db_ref, dk_ref, e_ref, y_ref, s_ref, state):
        @pl.when(pl.program_id(0) == 0)
        def _():
            state[...] = jnp.zeros_like(state)

        s_ref[0] = state[...]
        y, hs = _ssd_tile(x_ref[...], b_ref[...], c_ref[...], dt_ref[...], state[...], al_ref[...], db_ref[...],
                          dk_ref[...], e_ref[...])
        y_ref[...] = y
        state[...] = hs

    return pl.pallas_call(
        body, grid=(ntile,), name=name, in_specs=in_specs,
        out_specs=[pl.BlockSpec((tt, SSD_D_INNER), lambda i: (i, 0)),
                   pl.BlockSpec((1, SSD_GROUPS * SSD_STATE, 512), lambda i: (i, 0, 0))],
        out_shape=[jax.ShapeDtypeStruct((t_len, SSD_D_INNER), F32),
                   jax.ShapeDtypeStruct((ntile, SSD_GROUPS * SSD_STATE, 512), F32)],
        scratch_shapes=[pltpu.VMEM((SSD_GROUPS * SSD_STATE, 512), F32)],
        compiler_params=_cparams(("arbitrary",)),
    )(act, act, act, p, alog, dtb, dsk, _head_expand())


def _ssd_bwd(act, p, alog, dtb, dsk, states, dy, *, name):
    t_len = act.shape[0]
    tt = min(SEQ_TILE, t_len)
    ntile = t_len // tt
    in_specs, t = _ssd_specs(tt, ntile)

    def body(x_ref, b_ref, c_ref, dt_ref, al_ref, db_ref, dk_ref, e_ref, s0_ref, dy_ref,
             dx_ref, dbm_ref, dcm_ref, ddt_ref, dal_ref, ddb_ref, ddk_ref, dstate):
        @pl.when(pl.program_id(0) == 0)
        def _():
            dstate[...] = jnp.zeros_like(dstate)
            dal_ref[...] = jnp.zeros_like(dal_ref)
            ddb_ref[...] = jnp.zeros_like(ddb_ref)
            ddk_ref[...] = jnp.zeros_like(ddk_ref)

        expand = e_ref[...]

        def f(xs, bm, cm, dtr, hs0, al, db, dk):
            return _ssd_tile(xs, bm, cm, dtr, hs0, al, db, dk, expand)

        _, vjp = jax.vjp(f, x_ref[...], b_ref[...], c_ref[...], dt_ref[...], s0_ref[0], al_ref[...], db_ref[...],
                         dk_ref[...])
        dx, dbm, dcm, ddt, dhs, dal, ddb, ddk = vjp((dy_ref[...], dstate[...]))
        dx_ref[...] = dx
        dbm_ref[...] = dbm
        dcm_ref[...] = dcm
        ddt_ref[...] = ddt
        dstate[...] = dhs
        dal_ref[...] += dal
        ddb_ref[...] += ddb
        ddk_ref[...] += ddk

    vec = pl.BlockSpec((1, LANE), lambda i: (0, 0))
    rows = lambda w: pl.BlockSpec((tt, w), lambda i: (t(i), 0))
    return pl.pallas_call(
        body, grid=(ntile,), name=name,
        in_specs=in_specs + [pl.BlockSpec((1, SSD_GROUPS * SSD_STATE, 512), lambda i: (t(i), 0, 0)), rows(SSD_D_INNER)],
        out_specs=[rows(SSD_D_INNER), rows(512), rows(512), rows(LANE), vec, vec, vec],
        out_shape=[jax.ShapeDtypeStruct((t_len, SSD_D_INNER), F32), jax.ShapeDtypeStruct((t_len, 512), F32),
                   jax.ShapeDtypeStruct((t_len, 512), F32), jax.ShapeDtypeStruct((t_len, LANE), F32)]
        + [jax.ShapeDtypeStruct((1, LANE), F32)] * 3,
        scratch_shapes=[pltpu.VMEM((SSD_GROUPS * SSD_STATE, 512), F32)],
        compiler_params=_cparams(("arbitrary",)),
    )(act, act, act, p, alog, dtb, dsk, _head_expand(), states, dy)


ATT_SCALE = (MLA_NOPE + MLA_ROPE) ** -0.5
QK_W = 2 * LANE


def _chunk_mask(tq):
    return (_iota2((tq, tq), 1) // CHUNK) <= (_iota2((tq, tq), 0) // CHUNK)


def _attn_fwd(qc, kc, vv, *, name):
    t_len = qc.shape[0]
    tq = min(ATT_TILE, t_len)
    nq = t_len // tq

    def body(q_ref, k_ref, v_ref, o_ref, lse_ref, m_s, l_s, acc_s):
        qi, ki = pl.program_id(1), pl.program_id(2)

        @pl.when(ki == 0)
        def _():
            m_s[...] = jnp.full_like(m_s, -jnp.inf)
            l_s[...] = jnp.zeros_like(l_s)
            acc_s[...] = jnp.zeros_like(acc_s)

        def step(masked):
            s = _dot(q_ref[...], k_ref[...], "nt") * ATT_SCALE
            if masked:
                s = jnp.where(_chunk_mask(tq), s, -jnp.inf)
            m_new = jnp.maximum(m_s[...], jnp.max(s, axis=-1, keepdims=True))
            alpha = jnp.exp(m_s[...] - m_new)
            p = jnp.exp(s - m_new)
            l_s[...] = alpha * l_s[...] + jnp.sum(p, axis=-1, keepdims=True)
            acc_s[...] = alpha * acc_s[...] + _dot(p, v_ref[...], "nn")
            m_s[...] = m_new

        @pl.when(ki < qi)
        def _():
            step(False)

        @pl.when(ki == qi)
        def _():
            step(True)
            o_ref[...] = acc_s[...] / l_s[...]
            lse_ref[...] = jnp.broadcast_to(m_s[...] + jnp.log(l_s[...]), lse_ref.shape)

    kv_idx = lambda h, i, k: (jnp.minimum(k, i), h)
    return pl.pallas_call(
        body, grid=(MLA_HEADS, nq, nq), name=name,
        in_specs=[pl.BlockSpec((tq, QK_W), lambda h, i, k: (i, h)), pl.BlockSpec((tq, QK_W), kv_idx),
                  pl.BlockSpec((tq, LANE), kv_idx)],
        out_specs=[pl.BlockSpec((tq, LANE), lambda h, i, k: (i, h))] * 2,
        out_shape=[jax.ShapeDtypeStruct((t_len, MLA_HEADS * LANE), F32)] * 2,
        scratch_shapes=[pltpu.VMEM((tq, 1), F32), pltpu.VMEM((tq, 1), F32), pltpu.VMEM((tq, LANE), F32)],
        compiler_params=_cparams(("parallel", "parallel", "arbitrary")),
    )(qc, kc, vv)


def _attn_probs(q, k, v, do, o, lse, masked, tq):
    s = _dot(q, k, "nt") * ATT_SCALE
    if masked:
        s = jnp.where(_chunk_mask(tq), s, -jnp.inf)
    p = jnp.exp(s - lse[:, 0:1])
    delta = jnp.sum(do * o, axis=-1, keepdims=True)
    ds = p * (_dot(do, v, "nt") - delta) * ATT_SCALE
    return p, ds


def _attn_bwd_q(qc, kc, vv, o, lse, do, *, name):
    t_len = qc.shape[0]
    tq = min(ATT_TILE, t_len)
    nq = t_len // tq

    def body(q_ref, k_ref, v_ref, o_ref, lse_ref, do_ref, dq_ref, acc_s):
        qi, ki = pl.program_id(1), pl.program_id(2)

        @pl.when(ki == 0)
        def _():
            acc_s[...] = jnp.zeros_like(acc_s)

        def step(masked):
            _, ds = _attn_probs(q_ref[...], k_ref[...], v_ref[...], do_ref[...], o_ref[...], lse_ref[...], masked, tq)
            acc_s[...] += _dot(ds, k_ref[...], "nn")

        @pl.when(ki < qi)
        def _():
            step(False)

        @pl.when(ki == qi)
        def _():
            step(True)
            dq_ref[...] = acc_s[...]

    kv_idx = lambda h, i, k: (jnp.minimum(k, i), h)
    q_idx = lambda h, i, k: (i, h)
    return pl.pallas_call(
        body, grid=(MLA_HEADS, nq, nq), name=name,
        in_specs=[pl.BlockSpec((tq, QK_W), q_idx), pl.BlockSpec((tq, QK_W), kv_idx), pl.BlockSpec((tq, LANE), kv_idx),
                  pl.BlockSpec((tq, LANE), q_idx), pl.BlockSpec((tq, LANE), q_idx), pl.BlockSpec((tq, LANE), q_idx)],
        out_specs=pl.BlockSpec((tq, QK_W), q_idx),
        out_shape=jax.ShapeDtypeStruct((t_len, MLA_HEADS * QK_W), F32),
        scratch_shapes=[pltpu.VMEM((tq, QK_W), F32)],
        compiler_params=_cparams(("parallel", "parallel", "arbitrary")),
    )(qc, kc, vv, o, lse, do)


def _attn_bwd_kv(qc, kc, vv, o, lse, do, *, name):
    t_len = qc.shape[0]
    tq = min(ATT_TILE, t_len)
    nq = t_len // tq

    def body(q_ref, k_ref, v_ref, o_ref, lse_ref, do_ref, dk_ref, dv_ref, dk_s, dv_s):
        ki, qi = pl.program_id(1), pl.program_id(2)

        @pl.when(qi == 0)
        def _():
            dk_s[...] = jnp.zeros_like(dk_s)
            dv_s[...] = jnp.zeros_like(dv_s)

        def step(masked):
            p, ds = _attn_probs(q_ref[...], k_ref[...], v_ref[...], do_ref[...], o_ref[...], lse_ref[...], masked, tq)
            dv_s[...] += _dot(p, do_ref[...], "tn")
            dk_s[...] += _dot(ds, q_ref[...], "tn")

        @pl.when(qi > ki)
        def _():
            step(False)

        @pl.when(qi == ki)
        def _():
            step(True)

        @pl.when(qi == nq - 1)
        def _():
            dk_ref[...] = dk_s[...]
            dv_ref[...] = dv_s[...]

    q_idx = lambda h, k, i: (jnp.maximum(i, k), h)
    k_idx = lambda h, k, i: (k, h)
    return pl.pallas_call(
        body, grid=(MLA_HEADS, nq, nq), name=name,
        in_specs=[pl.BlockSpec((tq, QK_W), q_idx), pl.BlockSpec((tq, QK_W), k_idx), pl.BlockSpec((tq, LANE), k_idx),
                  pl.BlockSpec((tq, LANE), q_idx), pl.BlockSpec((tq, LANE), q_idx), pl.BlockSpec((tq, LANE), q_idx)],
        out_specs=[pl.BlockSpec((tq, QK_W), k_idx), pl.BlockSpec((tq, LANE), k_idx)],
        out_shape=[jax.ShapeDtypeStruct((t_len, MLA_HEADS * QK_W), F32), jax.ShapeDtypeStruct((t_len, MLA_HEADS * LANE), F32)],
        scratch_shapes=[pltpu.VMEM((tq, QK_W), F32), pltpu.VMEM((tq, LANE), F32)],
        compiler_params=_cparams(("parallel", "parallel", "arbitrary")),
    )(qc, kc, vv, o, lse, do)


def _adaln_fn(x, g, shift, scale):
    return ((_rms(x) * g) * (1.0 + scale) + shift,)


def _resid_fn(coef, y, x, gate):
    return (x + coef * gate * y,)


def _rms2_fn(cq, ckv, gq, gkv):
    return _rms(cq) * gq, _rms(ckv) * gkv


@jax.custom_vjp
def _swap_halves(x):
    return jnp.concatenate([x[:, 32:64], x[:, 0:32], x[:, 64:128]], axis=-1)


_swap_halves.defvjp(lambda x: (_swap_halves(x), None), lambda _, g: (_swap_halves(g),))


def _rope_fn(q, kv, misc, pos, invf, sgn):
    ang = pos * invf
    cos, sin = jnp.cos(ang), jnp.sin(ang) * sgn

    def rope(x):
        return x * cos + _swap_halves(x) * sin

    k_pe = rope(jnp.where(_iota2(misc.shape, 1) < MLA_ROPE, misc, 0.0))
    qs, ks = [], []
    for h in range(MLA_HEADS):
        qs += [q[:, h * LANE:(h + 1) * LANE], rope(q[:, (MLA_HEADS + h) * LANE:(MLA_HEADS + h + 1) * LANE])]
        ks += [kv[:, h * LANE:(h + 1) * LANE], k_pe]
    return jnp.concatenate(qs, axis=-1), jnp.concatenate(ks, axis=-1), kv[:, MLA_HEADS * LANE:]


def _ev_out_fn(oa, z, ob, g):
    parts = []
    for h in range(GDN_HEADS):
        hs = slice(h * LANE, (h + 1) * LANE)
        zz = z[:, hs]
        parts.append(_rms(oa[:, hs]) * g * (zz * jax.nn.sigmoid(zz)))
    return (jnp.concatenate(parts + [ob], axis=-1),)


def _od_out_fn(y, z, g):
    yz = y * (z * jax.nn.sigmoid(z))
    gw = SSD_D_INNER // SSD_GROUPS
    return (jnp.concatenate([_rms(yz[:, i * gw:(i + 1) * gw]) for i in range(SSD_GROUPS)], axis=-1) * g,)


def _loss_bwd(x, tgt, g, *, name):
    t_len, d = x.shape
    tm = min(ROW_TILE // 2, t_len)

    def body(x_ref, t_ref, g_ref, loss_ref, dx_ref, dg_ref):
        tgt_v = t_ref[...]

        def f(xv, gv):
            err = _rms(xv) * gv - tgt_v
            return 0.5 * jnp.sum(jnp.mean(err * err, axis=-1, keepdims=True), axis=0, keepdims=True)

        val, vjp = jax.vjp(f, x_ref[...], g_ref[...])
        dx, dg = vjp(jnp.ones((1, 1), F32))
        dx_ref[...] = dx

        @pl.when(pl.program_id(0) == 0)
        def _():
            loss_ref[...] = jnp.zeros_like(loss_ref)
            dg_ref[...] = jnp.zeros_like(dg_ref)

        loss_ref[...] += jnp.broadcast_to(val, loss_ref.shape)
        dg_ref[...] += dg

    row = pl.BlockSpec((tm, d), lambda i: (i, 0))
    return pl.pallas_call(
        body, grid=(t_len // tm,), name=name,
        in_specs=[row, row, pl.BlockSpec((1, d), lambda i: (0, 0))],
        out_specs=[pl.BlockSpec((1, LANE), lambda i: (0, 0)), row, pl.BlockSpec((1, d), lambda i: (0, 0))],
        out_shape=[jax.ShapeDtypeStruct((1, LANE), F32), jax.ShapeDtypeStruct((t_len, d), F32),
                   jax.ShapeDtypeStruct((1, d), F32)],
        compiler_params=_cparams(("arbitrary",)),
    )(x, tgt, g)


def _mesh_pos():
    return lax.axis_index("x"), lax.axis_index("y"), lax.axis_index("c")


def _exchange(xs, scatter, *, name):
    n_arr = len(xs)

    def body(*refs):
        in_refs, out_refs = refs[:n_arr], refs[n_arr:2 * n_arr]
        send_sems, recv_sems, local_sems = refs[2 * n_arr:]
        mx, my, mc = _mesh_pos()
        me = 4 * mx + 2 * my + mc
        started = []
        for a, (in_ref, out_ref) in enumerate(zip(in_refs, out_refs)):
            def src(j, in_ref=in_ref):
                return in_ref.at[j] if scatter else in_ref

            local = pltpu.make_async_copy(src(me), out_ref.at[me], local_sems.at[a])
            local.start()
            started.append((local, None))
            for d in range(1, N_DEV):
                px = 1 - mx if d & 4 else mx
                py = 1 - my if d & 2 else my
                pc = 1 - mc if d & 1 else mc
                peer = 4 * px + 2 * py + pc
                sem = a * (N_DEV - 1) + d - 1
                send = pltpu.make_async_remote_copy(
                    src_ref=src(peer), dst_ref=out_ref.at[me], send_sem=send_sems.at[sem], recv_sem=recv_sems.at[sem],
                    device_id=(px, py, pc), device_id_type=pl.DeviceIdType.MESH)
                send.start()
                recv = pltpu.make_async_remote_copy(
                    src_ref=src(peer), dst_ref=out_ref.at[peer], send_sem=send_sems.at[sem], recv_sem=recv_sems.at[sem],
                    device_id=(px, py, pc), device_id_type=pl.DeviceIdType.MESH)
                started.append((send, recv))
        for first, recv in started:
            if recv is None:
                first.wait()
            else:
                first.wait_send()
                recv.wait_recv()

    blocks = [tuple(x.shape[1:]) if scatter else tuple(x.shape) for x in xs]
    return pl.pallas_call(
        body, name=name,
        in_specs=[pl.BlockSpec(memory_space=pl.ANY)] * n_arr,
        out_specs=[pl.BlockSpec(memory_space=pl.ANY)] * n_arr,
        out_shape=[jax.ShapeDtypeStruct((N_DEV,) + b, x.dtype) for b, x in zip(blocks, xs)],
        scratch_shapes=[pltpu.SemaphoreType.DMA((n_arr * (N_DEV - 1),)), pltpu.SemaphoreType.DMA((n_arr * (N_DEV - 1),)),
                        pltpu.SemaphoreType.DMA((n_arr,))],
        compiler_params=pltpu.CompilerParams(has_side_effects=True),
    )(*xs)


def _cols(srcs, rows, plans, out_dtype, *, name):
    n_src = len(srcs)
    rb = _pick(rows, (256, 128, 64, 32, 16, 8))

    def width(pieces):
        return sum(p[1] if p[0] == "z" else p[3] - p[2] for p in pieces)

    def body(*refs):
        ins, outs = refs[:n_src], refs[n_src:]
        loaded = {}
        for o_ref, plan in zip(outs, plans):
            for j, pieces in enumerate(plan):
                vals = []
                for pc in pieces:
                    if pc[0] == "z":
                        vals.append(jnp.zeros((rb, pc[1]), out_dtype))
                    else:
                        si, sj, c0, c1 = pc
                        if (si, sj) not in loaded:
                            loaded[(si, sj)] = ins[si][sj]
                        vals.append(loaded[(si, sj)][:, c0:c1].astype(out_dtype))
                o_ref[j] = vals[0] if len(vals) == 1 else jnp.concatenate(vals, axis=-1)

    for arr, r0 in srcs:
        assert r0 % rb == 0
    return pl.pallas_call(
        body, grid=(rows // rb,), name=name,
        in_specs=[pl.BlockSpec((arr.shape[0], rb, arr.shape[2]), lambda i, r0=r0 // rb: (0, r0 + i, 0)) for arr, r0 in srcs],
        out_specs=[pl.BlockSpec((len(p), rb, width(p[0])), lambda i: (0, i, 0)) for p in plans],
        out_shape=[jax.ShapeDtypeStruct((len(p), rows, width(p[0])), out_dtype) for p in plans],
        compiler_params=_cparams(("parallel",)),
    )(*[arr for arr, _ in srcs])


def _shard_pieces(src, a, b, shard_w):
    out = []
    while a < b:
        s = a // shard_w
        e = min(b, (s + 1) * shard_w)
        out.append((src, s, a - s * shard_w, e - s * shard_w))
        a = e
    return out


def _mapped_pieces(a, b, segs):
    out = []
    for n0, n1, k0 in sorted(segs):
        lo, hi = max(a, n0), min(b, n1)
        if lo < hi:
            out.append((0, 0, k0 + lo - n0, k0 + hi - n0))
    return out


_EV_SEGS = [(0, 1536, EV_QKV), (1536, 2048, EV_Z), (2048, 2056, EV_MISC + MLA_ROPE), (2056, 2440, EV_CQ),
            (2440, 2696, EV_CKV), (2696, 2760, EV_MISC)]
EV_NAT_W, OD_NAT_W = 2760, 5152


PACK_W = 1024


def _adamw(w, gparts, m, v, *, name):
    n_rows, n_cols = w.shape
    n_parts = gparts.shape[0]
    tm = _pick(n_rows, (512, 256, 128, 64, 32, 16, 8))
    while n_parts * tm * n_cols * 4 > 4 * 1024 * 1024 and tm % 16 == 0:
        tm //= 2

    def body(w_ref, g_ref, m_ref, v_ref, go_ref, d_ref, mo_ref, vo_ref):
        g = g_ref[0]
        for j in range(1, n_parts):
            g = g + g_ref[j]
        m_new = ADAM_B1 * m_ref[...] + (1.0 - ADAM_B1) * g
        v_new = ADAM_B2 * v_ref[...] + (1.0 - ADAM_B2) * jnp.square(g)
        m_hat = m_new / (1.0 - ADAM_B1 ** ADAM_STEP)
        v_hat = v_new / (1.0 - ADAM_B2 ** ADAM_STEP)
        go_ref[...] = g
        d_ref[...] = -ADAM_LR * (m_hat / (jnp.sqrt(v_hat) + ADAM_EPS) + ADAM_WD * w_ref[...])
        mo_ref[...] = m_new
        vo_ref[...] = v_new

    row = pl.BlockSpec((tm, n_cols), lambda i: (i, 0))
    return pl.pallas_call(
        body, grid=(n_rows // tm,), name=name,
        in_specs=[row, pl.BlockSpec((n_parts, tm, n_cols), lambda i: (0, i, 0)), row, row],
        out_specs=[row] * 4,
        out_shape=[jax.ShapeDtypeStruct((n_rows, n_cols), F32)] * 4,
        compiler_params=_cparams(("parallel",)),
    )(w, gparts, m, v)


def _adamw_nd(w, gparts, m, v, *, name):
    shape = w.shape
    two = (-1, shape[-1])
    outs = _adamw(w.reshape(two), gparts.reshape((gparts.shape[0],) + (int(np.prod(shape[:-1])), shape[-1])),
                  m.reshape(two), v.reshape(two), name=name)
    return [o.reshape(shape) for o in outs]


def _pack(parts):
    flat = [p.astype(F32).reshape(-1) for p in parts]
    n_pad = -sum(f.shape[0] for f in flat) % (8 * PACK_W)
    return jnp.concatenate(flat + [jnp.zeros((n_pad,), F32)]).reshape(-1, PACK_W)


def _unpack(packed, shapes):
    flat = packed.reshape(-1)
    out, off = [], 0
    for s in shapes:
        n = int(np.prod(s))
        out.append(flat[off:off + n].reshape(tuple(s)))
        off += n
    return out


def _mod_shard(c_all, ada_w, ada_b_shard, *, name):
    n_layer, d, n_col = ada_w.shape

    def body(c_ref, w_ref, b_ref, o_ref):
        cv = c_ref[...]
        o_ref[0] = _dot(cv * jax.nn.sigmoid(cv), w_ref[0], "nn") + b_ref[0]

    return pl.pallas_call(
        body, grid=(n_layer,), name=name,
        in_specs=[pl.BlockSpec((N_DEV, d), lambda l: (0, 0)), pl.BlockSpec((1, d, n_col), lambda l: (l, 0, 0)),
                  pl.BlockSpec((1, 1, n_col), lambda l: (l, 0, 0))],
        out_specs=pl.BlockSpec((1, N_DEV, n_col), lambda l: (l, 0, 0)),
        out_shape=jax.ShapeDtypeStruct((n_layer, N_DEV, n_col), F32),
        compiler_params=_cparams(("parallel",)),
    )(c_all, ada_w, ada_b_shard)


def _ada_w_grad(c_all, dmod_shard, *, name):
    n_layer, _, n_col = dmod_shard.shape
    d = c_all.shape[1]

    def body(c_ref, g_ref, o_ref):
        cv = c_ref[...]
        o_ref[0] = _dot(cv * jax.nn.sigmoid(cv), g_ref[0], "tn", True)

    return pl.pallas_call(
        body, grid=(n_layer,), name=name,
        in_specs=[pl.BlockSpec((N_DEV, d), lambda l: (0, 0)), pl.BlockSpec((1, N_DEV, n_col), lambda l: (l, 0, 0))],
        out_specs=pl.BlockSpec((1, d, n_col), lambda l: (l, 0, 0)),
        out_shape=jax.ShapeDtypeStruct((n_layer, d, n_col), F32),
        compiler_params=_cparams(("parallel",)),
    )(c_all, dmod_shard)


def _uq(w):
    r = w.shape[0]
    rope = jnp.pad(w[:, :, MLA_NOPE:], ((0, 0), (0, 0), (0, LANE - MLA_ROPE)))
    return jnp.concatenate([w[:, :, :MLA_NOPE].reshape(r, -1), rope.reshape(r, -1)], axis=1)


def _uq_back(d):
    r = d.shape[0]
    half = MLA_HEADS * LANE
    return jnp.concatenate([d[:, :half].reshape(r, MLA_HEADS, LANE),
                            d[:, half:].reshape(r, MLA_HEADS, LANE)[:, :, :MLA_ROPE]], axis=-1)


def _ukv(w):
    r = w.shape[0]
    return jnp.concatenate([w[:, :, :MLA_NOPE].reshape(r, -1), w[:, :, MLA_NOPE:].reshape(r, -1)], axis=1)


def _ukv_back(d):
    r = d.shape[0]
    half = MLA_HEADS * LANE
    return jnp.concatenate([d[:, :half].reshape(r, MLA_HEADS, LANE), d[:, half:].reshape(r, MLA_HEADS, LANE)], axis=-1)


def _lane_vec(v):
    return jnp.pad(v.astype(F32), (0, LANE - v.shape[0])).reshape(1, LANE)


def _row(v):
    return v.astype(F32).reshape(1, -1)


def _adaln(x, ln):
    return _rows(_adaln_fn, [x], list(ln), [(D_MODEL, BF16)], name="adaln")[0]


def _adaln_bwd(x, ln, dh, dxn):
    (dx,), dln = _rows_vjp(_adaln_fn, [x], [], list(ln), [], [dh], [F32], adds={0: dxn}, name="adaln_bwd")
    return dx, dln


def _resid(coef, y, x, gate):
    return _rows(functools.partial(_resid_fn, coef), [y, x], [gate], [(D_MODEL, F32)], name="resid")[0]


def _resid_bwd(coef, y, x, gate, dxn):
    (dy,), (dgate,) = _rows_vjp(functools.partial(_resid_fn, coef), [y], [x], [gate], [], [dxn], [BF16],
                                name="resid_bwd")
    return dy, dgate


def _ffn_fwd(x, ln, gate, w13, w2):
    h = _adaln(x, ln)
    s = _ffn_act(h, w13, name="ffn_act")
    y = _matmul(s, w2, "nn", F32, name="ffn_down")
    return _resid(0.5, y, x, gate), (x, h, s, y)


def _ffn_bwd(saved, dxn, ln, gate, w13, w2):
    x, h, s, y = saved
    dy, dgate = _resid_bwd(0.5, y, x, gate, dxn)
    dab = _ffn_act_bwd(h, dy, w13, w2, name="ffn_act_bwd")
    dh = _matmul(dab, w13, "nt", F32, name="ffn_dh")
    dw13 = _matmul(h, dab, "tn", F32, name="ffn_dw13")
    dw2 = _matmul(s, dy, "tn", F32, name="ffn_dw2")
    dx, dln = _adaln_bwd(x, ln, dh, dxn)
    return dx, dw13, dw2, dln, dgate


def _rope_consts():
    half = MLA_ROPE // 2
    inv = (ROPE_THETA ** (-jnp.arange(half, dtype=F32) / half)).astype(F32)
    zeros = jnp.zeros((LANE - MLA_ROPE,), F32)
    invf = jnp.concatenate([inv, inv, zeros]).reshape(1, LANE)
    sgn = jnp.concatenate([-jnp.ones((half,), F32), jnp.ones((half,), F32), zeros]).reshape(1, LANE)
    return invf, sgn


def _even_fwd(x, pos, ln, gate, wt):
    h = _adaln(x, ln)
    p = _matmul(h, wt["w_in"], "nn", F32, name="ev_in")
    act, pre = _conv_fwd(p, EV_QKV, 1536, wt["conv_w"], jnp.zeros((1, 1536), F32), name="ev_conv")
    o_a, states = _gdn_fwd(act, p, wt["alog"], wt["dtb"], name="gdn_fwd")
    cqn, ckvn = _rows(_rms2_fn, [(p, EV_CQ, 384), (p, EV_CKV, 256)], [wt["gq"], wt["gkv"]],
                      [(384, BF16), (256, BF16)], name="mla_rms")
    q = _matmul(cqn, wt["w_uq"], "nn", F32, name="mla_uq")
    kv = _matmul(ckvn, wt["w_ukv"], "nn", F32, name="mla_ukv")
    invf, sgn = _rope_consts()
    qc, kc, vv = _rows(_rope_fn, [q, kv, (p, EV_MISC, LANE), pos], [invf, sgn],
                       [(1024, BF16), (1024, BF16), (512, BF16)], name="mla_rope", tm=ROW_TILE // 2)
    o_b, lse = _attn_fwd(qc, kc, vv, name="attn_fwd")
    (o,) = _rows(_ev_out_fn, [o_a, (p, EV_Z, 512), o_b], [wt["gdn_g"]], [(1024, BF16)], name="ev_out")
    y = _matmul(o, wt["w_out"], "nn", F32, name="ev_wout")
    return _resid(1.0, y, x, gate), (x, h, p, act, pre, states, cqn, ckvn, q, kv, qc, kc, vv, o_a, o_b, lse, o, y)


def _cat_fn(*parts):
    return (jnp.concatenate(parts, axis=-1),)


def _ev_dp_fn(dx0, dx1, dx2, dcq, dm_r, dm0, dm1, dm2, dm3, dz, dckv):
    return (jnp.concatenate([dx0, dx1, dx2, dcq, dm_r + ((dm0 + dm1) + (dm2 + dm3)), dz, dckv], axis=-1),)


def _even_bwd(saved, dxn, pos, ln, gate, wt):
    x, h, p, act, pre, states, cqn, ckvn, q, kv, qc, kc, vv, o_a, o_b, lse, o, y = saved
    g = {}
    dy, g["gate"] = _resid_bwd(1.0, y, x, gate, dxn)
    do = _matmul(dy, wt["w_out"], "nt", F32, name="ev_dwout_x")
    g["w_out"] = _matmul(o, dy, "tn", F32, name="ev_dwout_w")
    (d_oa, dz, d_ob), (g["gdn_g"],) = _rows_vjp(_ev_out_fn, [o_a, (p, EV_Z, 512), o_b], [], [wt["gdn_g"]], [], [do],
                                                [F32, F32, F32], name="ev_out_bwd")
    dqc = _attn_bwd_q(qc, kc, vv, o_b, lse, d_ob, name="attn_bwd_q")
    dkc, dvv = _attn_bwd_kv(qc, kc, vv, o_b, lse, d_ob, name="attn_bwd_kv")
    invf, sgn = _rope_consts()
    (dq, dkv, dm_r), _ = _rows_vjp(_rope_fn, [q, kv, (p, EV_MISC, LANE)], [pos], [], [invf, sgn], [dqc, dkc, dvv],
                                   [BF16, BF16, F32], name="mla_rope_bwd", tm=ROW_TILE // 4)
    dcqn = _matmul(dq, wt["w_uq"], "nt", F32, name="mla_duq_x")
    g["w_uq"] = _matmul(cqn, dq, "tn", F32, name="mla_duq_w")
    dckvn = _matmul(dkv, wt["w_ukv"], "nt", F32, name="mla_dukv_x")
    g["w_ukv"] = _matmul(ckvn, dkv, "tn", F32, name="mla_dukv_w")
    (dcq, dckv), (g["gq"], g["gkv"]) = _rows_vjp(_rms2_fn, [(p, EV_CQ, 384), (p, EV_CKV, 256)], [],
                                                 [wt["gq"], wt["gkv"]], [], [dcqn, dckvn], [F32, F32], name="mla_rms_bwd")
    dq_g, dk_g, dv_g, dm_g, g["alog"], g["dtb"] = _gdn_bwd(act, p, wt["alog"], wt["dtb"], states, d_oa, name="gdn_bwd")
    dxs, dws = [], []
    for j, d in enumerate((dq_g, dk_g, dv_g)):
        dxj, dwj, _ = _conv_bwd(d, pre, 512 * j, p, EV_QKV + 512 * j, wt["conv_w"], name="ev_conv_bwd")
        dxs.append(dxj)
        dws.append(dwj)
    g["conv_w"] = jnp.concatenate(dws, axis=1)
    (dp,) = _rows(_ev_dp_fn, dxs + [dcq, dm_r] + [dm_g[i] for i in range(GDN_HEADS)] + [dz, dckv], [],
                  [(EV_W, BF16)], name="ev_dp", tm=ROW_TILE // 2)
    dh = _matmul(dp, wt["w_in"], "nt", F32, name="ev_din_x")
    g["w_in"] = _matmul(h, dp, "tn", F32, name="ev_din_w")
    dx, g["ln"] = _adaln_bwd(x, ln, dh, dxn)
    return dx, g


def _odd_fwd(x, ln, gate, wt):
    h = _adaln(x, ln)
    p = _matmul(h, wt["w_in"], "nn", F32, name="od_in")
    act, pre = _conv_fwd(p, OD_XBC, 3072, wt["conv_w"], wt["conv_b"], name="od_conv")
    ys, states = _ssd_fwd(act, p, wt["alog"], wt["dtb"], wt["dsk"], name="ssd_fwd")
    (o,) = _rows(_od_out_fn, [ys, (p, OD_Z, 2048)], [wt["norm_g"]], [(SSD_D_INNER, BF16)], name="od_out",
                 tm=ROW_TILE // 2)
    y = _matmul(o, wt["w_out"], "nn", F32, name="od_wout")
    return _resid(1.0, y, x, gate), (x, h, p, act, pre, states, ys, o, y)


def _od_dp_fn(dz, dxx, dxb, dxc, ddt):
    return (jnp.concatenate([dz, dxx, dxb, dxc, ddt, jnp.zeros_like(ddt)], axis=-1),)


def _odd_bwd(saved, dxn, ln, gate, wt):
    x, h, p, act, pre, states, ys, o, y = saved
    g = {}
    dy, g["gate"] = _resid_bwd(1.0, y, x, gate, dxn)
    do = _matmul(dy, wt["w_out"], "nt", F32, name="od_dwout_x")
    g["w_out"] = _matmul(o, dy, "tn", F32, name="od_dwout_w")
    (dys, dz), (g["norm_g"],) = _rows_vjp(_od_out_fn, [ys, (p, OD_Z, 2048)], [], [wt["norm_g"]], [], [do], [F32, F32],
                                          name="od_out_bwd", tm=ROW_TILE // 4)
    dxs, dbm, dcm, ddt, g["alog"], g["dtb"], g["dsk"] = _ssd_bwd(act, p, wt["alog"], wt["dtb"], wt["dsk"], states, dys,
                                                                 name="ssd_bwd")
    dins, dws, dbs = [], [], []
    for d, c0 in ((dxs, 0), (dbm, 2048), (dcm, 2560)):
        dxj, dwj, dbj = _conv_bwd(d, pre, c0, p, OD_XBC + c0, wt["conv_w"], name="od_conv_bwd")
        dins.append(dxj)
        dws.append(dwj)
        dbs.append(dbj)
    g["conv_w"] = jnp.concatenate(dws, axis=1)
    g["conv_b"] = jnp.concatenate(dbs, axis=1)
    (dp,) = _rows(_od_dp_fn, [dz] + dins + [ddt], [], [(OD_W, BF16)], name="od_dp", tm=ROW_TILE // 4)
    dh = _matmul(dp, wt["w_in"], "nt", F32, name="od_din_x")
    g["w_in"] = _matmul(h, dp, "tn", F32, name="od_din_w")
    dx, g["ln"] = _adaln_bwd(x, ln, dh, dxn)
    return dx, g


def _local_step(x, tgt, pos, mod, fw):
    mod = mod.reshape(DEPTH, 3, 3, 1, D_MODEL)

    def ln_of(l, i):
        return (_row(fw["norm_g"][l, i]), mod[l, i, 0], mod[l, i, 1])

    def ffn_w(l, j):
        return fw["w13"][l][j], fw["w2"][l][j]

    def mixer_w(l):
        e = l // 2
        if l % 2 == 0:
            return dict(w_in=fw["ev_w_in_k"][e], conv_w=fw["gdn_conv_w"][e].astype(F32),
                        alog=_lane_vec(fw["gdn_A_log"][e]), dtb=_lane_vec(fw["gdn_dt_bias"][e]),
                        gdn_g=_row(fw["gdn_norm_g"][e]), gq=_row(fw["mla_q_norm_g"][e]), gkv=_row(fw["mla_kv_norm_g"][e]),
                        w_uq=_uq(fw["mla_w_uq"][e]), w_ukv=_ukv(fw["mla_w_ukv"][e]), w_out=fw["ev_w_out"][e])
        return dict(w_in=fw["ssd_w_in_k"][e], conv_w=fw["ssd_conv_w"][e].astype(F32),
                    conv_b=_row(fw["ssd_conv_b"][e]), alog=_lane_vec(fw["ssd_A_log"][e]),
                    dtb=_lane_vec(fw["ssd_dt_bias"][e]), dsk=_lane_vec(fw["ssd_D"][e]),
                    norm_g=_row(fw["ssd_norm_g"][e]), w_out=fw["ssd_w_out"][e])

    saved = []
    for l in range(DEPTH):
        x, s0 = _ffn_fwd(x, ln_of(l, 0), mod[l, 0, 2], *ffn_w(l, 0))
        if l % 2 == 0:
            x, s1 = _even_fwd(x, pos, ln_of(l, 1), mod[l, 1, 2], mixer_w(l))
        else:
            x, s1 = _odd_fwd(x, ln_of(l, 1), mod[l, 1, 2], mixer_w(l))
        x, s2 = _ffn_fwd(x, ln_of(l, 2), mod[l, 2, 2], *ffn_w(l, 1))
        saved.append((s0, s1, s2))

    loss, dx, d_final_g = _loss_bwd(x, tgt, _row(fw["final_g"]), name="loss")

    gl = {k: [None] * DEPTH for k in ("norm_g", "dmod")}
    ge = {k: [None] * (DEPTH // 2) for k in ("gdn_conv_w", "gdn_A_log", "gdn_dt_bias", "gdn_norm_g",
                                              "mla_q_norm_g", "mla_w_uq", "mla_kv_norm_g", "mla_w_ukv", "ev_w_out")}
    go = {k: [None] * (DEPTH // 2) for k in ("ssd_conv_w", "ssd_conv_b", "ssd_A_log", "ssd_dt_bias", "ssd_D",
                                              "ssd_norm_g", "ssd_w_out")}
    gk = {"w13": [[None] * 2 for _ in range(DEPTH)], "w2": [[None] * 2 for _ in range(DEPTH)],
          "ev_w_in_k": [None] * (DEPTH // 2), "ssd_w_in_k": [None] * (DEPTH // 2)}
    for l in reversed(range(DEPTH)):
        s0, s1, s2 = saved[l]
        e = l // 2
        dg, dsh, dsc, dgt = [None] * 3, [None] * 3, [None] * 3, [None] * 3
        dx, gk["w13"][l][1], gk["w2"][l][1], (dg[2], dsh[2], dsc[2]), dgt[2] = _ffn_bwd(
            s2, dx, ln_of(l, 2), mod[l, 2, 2], *ffn_w(l, 1))
        if l % 2 == 0:
            dx, g = _even_bwd(s1, dx, pos, ln_of(l, 1), mod[l, 1, 2], mixer_w(l))
            gk["ev_w_in_k"][e] = g["w_in"]
            ge["gdn_conv_w"][e] = g["conv_w"]
            ge["gdn_A_log"][e] = g["alog"][0, :GDN_HEADS]
            ge["gdn_dt_bias"][e] = g["dtb"][0, :GDN_HEADS]
            ge["gdn_norm_g"][e] = g["gdn_g"][0]
            ge["mla_q_norm_g"][e] = g["gq"][0]
            ge["mla_w_uq"][e] = _uq_back(g["w_uq"])
            ge["mla_kv_norm_g"][e] = g["gkv"][0]
            ge["mla_w_ukv"][e] = _ukv_back(g["w_ukv"])
            ge["ev_w_out"][e] = g["w_out"]
        else:
            dx, g = _odd_bwd(s1, dx, ln_of(l, 1), mod[l, 1, 2], mixer_w(l))
            gk["ssd_w_in_k"][e] = g["w_in"]
            go["ssd_conv_w"][e] = g["conv_w"]
            go["ssd_conv_b"][e] = g["conv_b"][0]
            go["ssd_A_log"][e] = g["alog"][0, :SSD_HEADS]
            go["ssd_dt_bias"][e] = g["dtb"][0, :SSD_HEADS]
            go["ssd_D"][e] = g["dsk"][0, :SSD_HEADS]
            go["ssd_norm_g"][e] = g["norm_g"][0]
            go["ssd_w_out"][e] = g["w_out"]
        dg[1], dsh[1], dsc[1] = g["ln"]
        dgt[1] = g["gate"]
        dx, gk["w13"][l][0], gk["w2"][l][0], (dg[0], dsh[0], dsc[0]), dgt[0] = _ffn_bwd(
            s0, dx, ln_of(l, 0), mod[l, 0, 2], *ffn_w(l, 0))
        gl["norm_g"][l] = jnp.concatenate(dg, axis=0)
        gl["dmod"][l] = jnp.concatenate([jnp.concatenate([dsh[i], dsc[i], dgt[i]], axis=1) for i in range(3)], axis=1)[0]

    grads = {k: jnp.stack(v) for k, v in {**gl, **ge, **go}.items()}
    grads.update(gk)
    grads["final_g"] = d_final_g[0]
    dmod = grads.pop("dmod")
    return loss, dx, grads, dmod


_WEIGHTS = ("ada_w", "ada_b", "norm_g", "ffn_w1", "ffn_w3", "ffn_w2", "ev_w_in", "gdn_conv_w", "gdn_A_log", "gdn_dt_bias",
            "gdn_norm_g", "mla_q_norm_g", "mla_w_uq", "mla_kv_norm_g", "mla_w_ukv", "ev_w_out", "ssd_w_in", "ssd_conv_w",
            "ssd_conv_b", "ssd_A_log", "ssd_dt_bias", "ssd_D", "ssd_norm_g", "ssd_w_out", "final_g")
_BIG = {"ffn_w1": 3, "ffn_w3": 3, "ffn_w2": 2, "ev_w_in": 2, "mla_w_uq": 1, "mla_w_ukv": 1, "ev_w_out": 1, "ssd_w_in": 2,
        "ssd_w_out": 1}
_SMALL = {"norm_g": 2, "gdn_conv_w": 2, "ssd_conv_w": 2, "ssd_conv_b": 1, "ssd_norm_g": 1}
_REPL = ("ada_b", "gdn_A_log", "gdn_dt_bias", "gdn_norm_g", "mla_q_norm_g", "mla_kv_norm_g", "ssd_A_log", "ssd_dt_bias",
         "ssd_D", "final_g")


def _join(pieces, axis):
    moved = jnp.moveaxis(pieces, 0, axis)
    shape = moved.shape
    return moved.reshape(shape[:axis] + (shape[axis] * shape[axis + 1],) + shape[axis + 2:])


def _split(full, axis):
    shape = full.shape
    return jnp.moveaxis(full.reshape(shape[:axis] + (N_DEV, shape[axis] // N_DEV) + shape[axis + 1:]), axis, 0)


def kernel(x, c, positions, ada_w, ada_b, norm_g, ffn_w1, ffn_w3, ffn_w2, ev_w_in, gdn_conv_w, gdn_A_log, gdn_dt_bias, gdn_norm_g, mla_q_norm_g, mla_w_uq, mla_kv_norm_g, mla_w_ukv, ev_w_out, ssd_w_in, ssd_conv_w, ssd_conv_b, ssd_A_log, ssd_dt_bias, ssd_D, ssd_norm_g, ssd_w_out, final_g, loss_target, m_ada_w, m_ada_b, m_norm_g, m_ffn_w1, m_ffn_w3, m_ffn_w2, m_ev_w_in, m_gdn_conv_w, m_gdn_A_log, m_gdn_dt_bias, m_gdn_norm_g, m_mla_q_norm_g, m_mla_w_uq, m_mla_kv_norm_g, m_mla_w_ukv, m_ev_w_out, m_ssd_w_in, m_ssd_conv_w, m_ssd_conv_b, m_ssd_A_log, m_ssd_dt_bias, m_ssd_D, m_ssd_norm_g, m_ssd_w_out, m_final_g, v_ada_w, v_ada_b, v_norm_g, v_ffn_w1, v_ffn_w3, v_ffn_w2, v_ev_w_in, v_gdn_conv_w, v_gdn_A_log, v_gdn_dt_bias, v_gdn_norm_g, v_mla_q_norm_g, v_mla_w_uq, v_mla_kv_norm_g, v_mla_w_ukv, v_ev_w_out, v_ssd_w_in, v_ssd_conv_w, v_ssd_conv_b, v_ssd_A_log, v_ssd_dt_bias, v_ssd_D, v_ssd_norm_g, v_ssd_w_out, v_final_g):
    a = dict(locals())
    w = {n: a[n] for n in _WEIGHTS}
    m = {n: a["m_" + n] for n in _WEIGHTS}
    v = {n: a["v_" + n] for n in _WEIGHTS}
    mx, my, mc = _mesh_pos()
    me = 4 * mx + 2 * my + mc
    t_len = x.shape[1]
    shards = range(N_DEV)

    small_names, big_names = list(_SMALL), list(_BIG)
    small_g = _exchange([c] + [w[n] for n in small_names], False, name="gather_small")
    c_all = small_g[0].reshape(N_DEV, D_MODEL)
    fw = {n: _join(p, _SMALL[n]) for n, p in zip(small_names, small_g[1:])}
    big_g = dict(zip(big_names, _exchange([w[n].astype(BF16) for n in big_names], False, name="gather_big")))
    for n in ("mla_w_uq", "mla_w_ukv", "ev_w_out", "ssd_w_out"):
        fw[n] = _join(big_g[n], _BIG[n])
    for n in _REPL:
        fw[n] = w[n]

    fs, es, os_ = ffn_w1.shape[3], ev_w_in.shape[2], ssd_w_in.shape[2]
    half = range(N_DEV // 2)
    g1 = big_g["ffn_w1"].reshape(N_DEV, -1, fs)
    g3 = big_g["ffn_w3"].reshape(N_DEV, -1, fs)
    plan13 = [[[(0, s, 0, fs) for s in half] + [(1, s, 0, fs) for s in half]
               + [(0, s + 4, 0, fs) for s in half] + [(1, s + 4, 0, fs) for s in half]]]
    fw["w13"] = [[_cols([(g1, (2 * l + j) * D_MODEL), (g3, (2 * l + j) * D_MODEL)], D_MODEL, plan13, BF16,
                        name="join_w13")[0][0] for j in range(2)] for l in range(DEPTH)]
    fw["w2"] = [[big_g["ffn_w2"][:, l, j].reshape(D_FF, D_MODEL) for j in range(2)] for l in range(DEPTH)]
    plan_ev, k_at = [], 0
    for n0, n1, k0 in sorted(_EV_SEGS, key=lambda seg: seg[2]):
        if k0 > k_at:
            plan_ev.append(("z", k0 - k_at))
        plan_ev += _shard_pieces(0, n0, n1, es)
        k_at = k0 + n1 - n0
    assert k_at == EV_W and es * N_DEV == EV_NAT_W and os_ * N_DEV == OD_NAT_W
    ge_ = big_g["ev_w_in"].reshape(N_DEV, -1, es)
    fw["ev_w_in_k"] = [_cols([(ge_, e * D_MODEL)], D_MODEL, [[plan_ev]], BF16, name="join_ev_in")[0][0]
                       for e in range(DEPTH // 2)]
    go_ = big_g["ssd_w_in"].reshape(N_DEV, -1, os_)
    plan_od = [[_shard_pieces(0, 0, OD_NAT_W, os_) + [("z", OD_W - OD_NAT_W)]]]
    fw["ssd_w_in_k"] = [_cols([(go_, e * D_MODEL)], D_MODEL, plan_od, BF16, name="join_od_in")[0][0]
                        for e in range(DEPTH // 2)]

    n_col = ada_w.shape[2]
    ada_b_shard = lax.dynamic_slice(ada_b, (0, me * n_col), (DEPTH, n_col)).reshape(DEPTH, 1, n_col)
    mod_all = _exchange([_mod_shard(c_all, ada_w, ada_b_shard, name="mod")], False, name="gather_mod")[0]
    mod_me = lax.dynamic_index_in_dim(mod_all, me, axis=2, keepdims=False)
    mod = jnp.transpose(mod_me, (1, 0, 2)).reshape(DEPTH, N_DEV * n_col)

    pos = positions.astype(F32).reshape(t_len, 1)
    loss, dx, grads, dmod = _local_step(x[0], loss_target[0], pos, mod, fw)

    repl_shapes = [w[n].shape for n in _REPL] + [(1,)]
    parts8 = _exchange([_pack([dmod] + [grads[n] for n in _REPL[1:]] + [loss[0, :1]])], False, name="gather_repl")[0]
    zero = jnp.zeros((1,), F32)
    r_grad, r_delta, r_m, r_v = [
        _unpack(o, repl_shapes) for o in _adamw(_pack([w[n] for n in _REPL] + [zero]), parts8,
                                                _pack([m[n] for n in _REPL] + [zero]),
                                                _pack([v[n] for n in _REPL] + [zero]), name="adamw_repl")]
    out = {"grad": {}, "delta": {}, "m": {}, "v": {}}
    for i, n in enumerate(_REPL):
        out["grad"][n], out["delta"][n], out["m"][n], out["v"][n] = r_grad[i], r_delta[i], r_m[i], r_v[i]
    loss_total = r_grad[-1].reshape(())

    dmod_all = parts8[:, :dmod.size // PACK_W].reshape((N_DEV,) + dmod.shape)
    dmod_cols = jnp.transpose(lax.dynamic_slice_in_dim(dmod_all, me * n_col, n_col, axis=2), (1, 0, 2))
    g_ada = _ada_w_grad(c_all, dmod_cols, name="ada_w_grad")
    for k, o in zip(("grad", "delta", "m", "v"), _adamw_nd(ada_w, g_ada[None], m["ada_w"], v["ada_w"], name="adamw_ada")):
        out[k]["ada_w"] = o

    def stack_lj(f):
        return jnp.stack([jnp.stack([f(l, j) for j in range(2)], axis=1) for l in range(DEPTH)], axis=1)

    def w13_cols(s, third):
        k0 = (s % 4) * fs + (2 * FF_HALF if s >= 4 else 0) + (FF_HALF if third else 0)
        return [(0, 0, k0, k0 + fs)]

    d13 = [[_cols([(grads["w13"][l][j][None], 0)], D_MODEL, [[w13_cols(s, False) for s in shards],
                                                            [w13_cols(s, True) for s in shards]], F32, name="split_w13")
            for j in range(2)] for l in range(DEPTH)]
    split = {n: _split(grads[n], {**_SMALL, **_BIG}[n]) for n in small_names + ["mla_w_uq", "mla_w_ukv", "ev_w_out", "ssd_w_out"]}
    split["ffn_w1"] = stack_lj(lambda l, j: d13[l][j][0])
    split["ffn_w3"] = stack_lj(lambda l, j: d13[l][j][1])
    split["ffn_w2"] = stack_lj(lambda l, j: grads["w2"][l][j].reshape(N_DEV, -1, D_MODEL))
    split["ev_w_in"] = jnp.stack(
        [_cols([(grads["ev_w_in_k"][e][None], 0)], D_MODEL, [[_mapped_pieces(s * es, (s + 1) * es, _EV_SEGS) for s in shards]],
               F32, name="split_ev_in")[0] for e in range(DEPTH // 2)], axis=1)
    split["ssd_w_in"] = jnp.stack(
        [_cols([(grads["ssd_w_in_k"][e][None], 0)], D_MODEL, [[[(0, 0, s * os_, (s + 1) * os_)] for s in shards]],
               F32, name="split_od_in")[0] for e in range(DEPTH // 2)], axis=1)

    sh_names = small_names + big_names
    scat = _exchange([split[n] for n in sh_names], True, name="scatter_grads")
    for n, g8 in zip(sh_names, scat):
        for k, o in zip(("grad", "delta", "m", "v"), _adamw_nd(w[n], g8, m[n], v[n], name="adamw_" + n)):
            out[k][n] = o

    return (loss_total, dx.reshape(x.shape), *[out["grad"][n] for n in _WEIGHTS], *[out["delta"][n] for n in _WEIGHTS],
            *[out["m"][n] for n in _WEIGHTS], *[out["v"][n] for n in _WEIGHTS])
```

```python
import functools
import math

import numpy as np
import jax
import jax.numpy as jnp
from jax import lax
from jax.experimental import pallas as pl
from jax.experimental.pallas import tpu as pltpu

F32 = jnp.float32
BF16 = jnp.bfloat16
HI = lax.Precision.HIGHEST

D_MODEL = 1024
DEPTH = 4
CHUNK = 64
NORM_EPS = 1e-6
CONV_K = 4
D_FF = 2816
GDN_HEADS = 4
GDN_DK = 128
MLA_HEADS = 4
MLA_NOPE = 128
MLA_ROPE = 64
ROPE_THETA = 10000.0
SSD_HEADS = 32
SSD_HEADDIM = 64
SSD_GROUPS = 4
SSD_STATE = 128
SSD_D_INNER = 2048
N_DEV = 8

ADAM_LR = 0.001
ADAM_B1 = 0.9
ADAM_B2 = 0.999
ADAM_EPS = 1e-08
ADAM_WD = 0.01
ADAM_STEP = 10

V7X_VMEM_LIMIT = 56 * 1024 * 1024
ROW_TILE = 512
SEQ_TILE = 128
ATT_TILE = 1024
MM_RESIDENT_BYTES = 12 * 1024 * 1024
FF_HALF = D_FF // 2
LANE = 128

EV_QKV, EV_CQ, EV_MISC, EV_Z, EV_CKV, EV_W = 0, 1536, 1920, 2048, 2560, 2816
OD_Z, OD_XBC, OD_DT, OD_W = 0, 2048, 5120, 5376


def _cparams(sem=None):
    return pltpu.CompilerParams(dimension_semantics=sem, vmem_limit_bytes=V7X_VMEM_LIMIT)


def _pick(n, cands):
    for c in cands:
        if n % c == 0:
            return c
    return n


_DN = {"nn": (((1,), (0,)), ((), ())), "nt": (((1,), (1,)), ((), ())), "tn": (((0,), (0,)), ((), ()))}


def _dot(a, b, mode, hi=False):
    if hi:
        return lax.dot_general(a.astype(F32), b.astype(F32), _DN[mode], precision=HI, preferred_element_type=F32)
    return lax.dot_general(a.astype(BF16), b.astype(BF16), _DN[mode], preferred_element_type=F32)


@functools.partial(jax.custom_vjp, nondiff_argnums=(2, 3))
def _mm(a, b, mode, hi):
    return _dot(a, b, mode, hi)


def _mm_fwd(a, b, mode, hi):
    return _dot(a, b, mode, hi), (a, b)


def _mm_bwd(mode, hi, res, g):
    a, b = res
    if mode == "nn":
        return _dot(g, b, "nt", hi), _dot(a, g, "tn", hi)
    if mode == "nt":
        return _dot(g, b, "nn", hi), _dot(g, a, "tn", hi)
    return _dot(b, g, "nt", hi), _dot(a, g, "nn", hi)


_mm.defvjp(_mm_fwd, _mm_bwd)


def _iota2(shape, dim):
    return lax.broadcasted_iota(jnp.int32, shape, dim)


def _row_spec(a, tm):
    if isinstance(a, tuple):
        arr, c0, w = a
        assert c0 % w == 0
        cb = c0 // w
        return arr, pl.BlockSpec((tm, w), lambda i, cb=cb: (i, cb))
    return a, pl.BlockSpec((tm, a.shape[1]), lambda i: (i, 0))


def _full_spec(b):
    return pl.BlockSpec(b.shape, lambda i: (0,) * b.ndim)


def _rows(fn, tiled, bcast, outs, *, name, tm=ROW_TILE):
    arrs, specs = zip(*[_row_spec(a, 0) for a in tiled])
    t_len = arrs[0].shape[0]
    tm = min(tm, t_len)
    arrs, specs = zip(*[_row_spec(a, tm) for a in tiled])
    nt, nb = len(tiled), len(bcast)

    def body(*refs):
        ins = [r[...].astype(F32) for r in refs[:nt]] + [r[...] for r in refs[nt:nt + nb]]
        res = fn(*ins)
        for r, v in zip(refs[nt + nb:], res):
            r[...] = v.astype(r.dtype)

    return pl.pallas_call(
        body, grid=(t_len // tm,), name=name,
        in_specs=list(specs) + [_full_spec(b) for b in bcast],
        out_specs=[pl.BlockSpec((tm, c), lambda i: (i, 0)) for c, _ in outs],
        out_shape=[jax.ShapeDtypeStruct((t_len, c), dt) for c, dt in outs],
        compiler_params=_cparams(("parallel",)),
    )(*arrs, *bcast)


def _rows_vjp(fn, tiled, consts, bcast, bconsts, douts, grads, *, name, adds=None, tm=ROW_TILE // 2):
    adds = adds or {}
    t_arrs, t_specs = zip(*[_row_spec(a, 0) for a in tiled])
    t_len = t_arrs[0].shape[0]
    tm = min(tm, t_len)
    rows_in = list(tiled) + list(consts) + list(douts) + [adds[k] for k in sorted(adds)]
    arrs, specs = zip(*[_row_spec(a, tm) for a in rows_in])
    nt, nc, nb, nbc, nd, na = len(tiled), len(consts), len(bcast), len(bconsts), len(douts), len(adds)
    add_pos = {k: j for j, k in enumerate(sorted(adds))}
    want = [j for j, g in enumerate(grads) if g is not None]

    def body(*refs):
        p = 0
        t = [r[...].astype(F32) for r in refs[p:p + nt]]; p += nt
        c = [r[...].astype(F32) for r in refs[p:p + nc]]; p += nc
        d = [r[...].astype(F32) for r in refs[p:p + nd]]; p += nd
        a = [r[...].astype(F32) for r in refs[p:p + na]]; p += na
        b = [r[...] for r in refs[p:p + nb]]; p += nb
        bc = [r[...] for r in refs[p:p + nbc]]; p += nbc
        g_refs = refs[p:p + len(want)]; p += len(want)
        gb_refs = refs[p:p + nb]

        def f(*args):
            return fn(*args[:nt], *c, *args[nt:], *bc)

        _, vjp = jax.vjp(f, *t, *b)
        g = vjp(tuple(d))
        for r, j in zip(g_refs, want):
            val = g[j]
            if j in add_pos:
                val = val + a[add_pos[j]]
            r[...] = val.astype(r.dtype)

        @pl.when(pl.program_id(0) == 0)
        def _():
            for r in gb_refs:
                r[...] = jnp.zeros_like(r)

        for r, val in zip(gb_refs, g[nt:]):
            r[...] += val

    def width(a):
        return a[2] if isinstance(a, tuple) else a.shape[1]

    res = pl.pallas_call(
        body, grid=(t_len // tm,), name=name,
        in_specs=list(specs) + [_full_spec(b) for b in list(bcast) + list(bconsts)],
        out_specs=[pl.BlockSpec((tm, width(tiled[j])), lambda i: (i, 0)) for j in want] + [_full_spec(b) for b in bcast],
        out_shape=[jax.ShapeDtypeStruct((t_len, width(tiled[j])), grads[j]) for j in want]
        + [jax.ShapeDtypeStruct(b.shape, F32) for b in bcast],
        compiler_params=_cparams(("arbitrary",)),
    )(*arrs, *bcast, *bconsts)
    tg = [None] * nt
    for r, j in zip(res[:len(want)], want):
        tg[j] = r
    return tg, list(res[len(want):])


def _matmul(a, b, mode, out_dtype, *, name):
    if mode == "tn":
        assert out_dtype == F32
        k_len, m_len = a.shape
        n_len = b.shape[1]
        tm, tn = m_len, n_len
        while tm * tn * 4 > MM_RESIDENT_BYTES and tn % (2 * LANE) == 0:
            tn //= 2
        tk = _pick(k_len, (512, 256, 128))
    else:
        m_len, k_len = a.shape
        n_len = b.shape[1] if mode == "nn" else b.shape[0]
        tk, tn = k_len, n_len
        while tk * tn * 2 > MM_RESIDENT_BYTES and tn % (2 * LANE) == 0:
            tn //= 2
        tm = _pick(m_len, (512, 256, 128))
        while tm * max(4 * tn, 2 * tk) > MM_RESIDENT_BYTES // 2 and tm % 256 == 0:
            tm //= 2
    nk = k_len // tk
    if mode == "nn":
        a_spec = pl.BlockSpec((tm, tk), lambda j, i, k: (i, k))
        b_spec = pl.BlockSpec((tk, tn), lambda j, i, k: (k, j))
    elif mode == "nt":
        a_spec = pl.BlockSpec((tm, tk), lambda j, i, k: (i, k))
        b_spec = pl.BlockSpec((tn, tk), lambda j, i, k: (j, k))
    else:
        a_spec = pl.BlockSpec((tk, tm), lambda j, i, k: (k, i))
        b_spec = pl.BlockSpec((tk, tn), lambda j, i, k: (k, j))

    def body(a_ref, b_ref, o_ref):
        part = _dot(a_ref[...], b_ref[...], mode)
        if nk == 1:
            o_ref[...] = part.astype(o_ref.dtype)
        else:
            @pl.when(pl.program_id(2) == 0)
            def _():
                o_ref[...] = jnp.zeros_like(o_ref)

            o_ref[...] += part

    return pl.pallas_call(
        body, grid=(n_len // tn, m_len // tm, nk), name=name,
        in_specs=[a_spec, b_spec],
        out_specs=pl.BlockSpec((tm, tn), lambda j, i, k: (i, j)),
        out_shape=jax.ShapeDtypeStruct((m_len, n_len), out_dtype),
        compiler_params=_cparams(("parallel", "parallel", "arbitrary")),
    )(a, b)


def _ffn_act(h, w13, *, name):
    t_len, d = h.shape
    tm = min(ROW_TILE, t_len)

    def body(h_ref, w_ref, s_ref):
        ab = _dot(h_ref[...], w_ref[...], "nn")
        a, b = ab[:, :FF_HALF], ab[:, FF_HALF:]
        s_ref[...] = (a * jax.nn.sigmoid(a) * b).astype(s_ref.dtype)

    return pl.pallas_call(
        body, grid=(2, t_len // tm), name=name,
        in_specs=[pl.BlockSpec((tm, d), lambda f, i: (i, 0)), pl.BlockSpec((d, 2 * FF_HALF), lambda f, i: (0, f))],
        out_specs=pl.BlockSpec((tm, FF_HALF), lambda f, i: (i, f)),
        out_shape=jax.ShapeDtypeStruct((t_len, D_FF), BF16),
        compiler_params=_cparams(("parallel", "parallel")),
    )(h, w13)


def _ffn_act_bwd(h, dy, w13, w2, *, name):
    t_len, d = h.shape
    tm = min(ROW_TILE, t_len)

    def body(h_ref, dy_ref, w_ref, w2_ref, o_ref):
        ab = _dot(h_ref[...], w_ref[...], "nn")
        a, b = ab[:, :FF_HALF], ab[:, FF_HALF:]
        ds = _dot(dy_ref[...], w2_ref[...], "nt")
        sig = jax.nn.sigmoid(a)
        silu = a * sig
        da = ds * b * (sig * (1.0 + a * (1.0 - sig)))
        db = ds * silu
        o_ref[...] = jnp.concatenate([da, db], axis=-1).astype(o_ref.dtype)

    return pl.pallas_call(
        body, grid=(2, t_len // tm), name=name,
        in_specs=[pl.BlockSpec((tm, d), lambda f, i: (i, 0)), pl.BlockSpec((tm, d), lambda f, i: (i, 0)),
                  pl.BlockSpec((d, 2 * FF_HALF), lambda f, i: (0, f)), pl.BlockSpec((FF_HALF, d), lambda f, i: (f, 0))],
        out_specs=pl.BlockSpec((tm, 2 * FF_HALF), lambda f, i: (i, f)),
        out_shape=jax.ShapeDtypeStruct((t_len, 2 * D_FF), BF16),
        compiler_params=_cparams(("parallel", "parallel")),
    )(h, dy, w13, w2)


CONV_CB = 512
HALO = 8


def _conv_fwd(p, c0, n_ch, w, b, *, name):
    t_len = p.shape[0]
    tm = min(ROW_TILE, t_len)
    hb = tm // HALO
    cb0 = c0 // CONV_CB

    def body(x_ref, halo_ref, w_ref, b_ref, act_ref, pre_ref):
        first = pl.program_id(1) == 0
        halo = jnp.where(first, 0.0, halo_ref[...])
        xx = jnp.concatenate([halo, x_ref[...]], axis=0)
        wv = w_ref[...]
        acc = b_ref[...] + wv[0:1] * xx[HALO - 3:HALO - 3 + tm]
        for j in range(1, CONV_K):
            acc = acc + wv[j:j + 1] * xx[HALO - 3 + j:HALO - 3 + j + tm]
        pre_ref[...] = acc
        act_ref[...] = acc * jax.nn.sigmoid(acc)

    return pl.pallas_call(
        body, grid=(n_ch // CONV_CB, t_len // tm), name=name,
        in_specs=[pl.BlockSpec((tm, CONV_CB), lambda j, i: (i, cb0 + j)),
                  pl.BlockSpec((HALO, CONV_CB), lambda j, i: (jnp.maximum(i * hb - 1, 0), cb0 + j)),
                  pl.BlockSpec((CONV_K, CONV_CB), lambda j, i: (0, j)),
                  pl.BlockSpec((1, CONV_CB), lambda j, i: (0, j))],
        out_specs=[pl.BlockSpec((tm, CONV_CB), lambda j, i: (i, j))] * 2,
        out_shape=[jax.ShapeDtypeStruct((t_len, n_ch), F32)] * 2,
        compiler_params=_cparams(("parallel", "arbitrary")),
    )(p, p, w, b)


def _conv_bwd(dact, pre, pre_c0, p, p_c0, w, *, name):
    t_len, n_ch = dact.shape
    tm = min(ROW_TILE, t_len)
    hb = tm // HALO
    nt = t_len // tm
    last_hb = t_len // HALO - 1
    cb0 = p_c0 // CONV_CB
    cbp = pre_c0 // CONV_CB

    def dsilu(z):
        sig = jax.nn.sigmoid(z)
        return sig * (1.0 + z * (1.0 - sig))

    def body(d_ref, dn_ref, pre_ref, pren_ref, x_ref, xh_ref, w_ref, dx_ref, dw_ref, db_ref):
        i = pl.program_id(1)
        dpre = d_ref[...] * dsilu(pre_ref[...])
        dnext = jnp.where(i == nt - 1, 0.0, dn_ref[...] * dsilu(pren_ref[...]))
        ext = jnp.concatenate([dpre, dnext], axis=0)
        xx = jnp.concatenate([jnp.where(i == 0, 0.0, xh_ref[...]), x_ref[...]], axis=0)
        wv = w_ref[...]
        dx = wv[0:1] * ext[3:3 + tm]
        for j in range(1, CONV_K):
            dx = dx + wv[j:j + 1] * ext[3 - j:3 - j + tm]
        dx_ref[...] = dx
        dws = [jnp.sum(dpre * xx[HALO - 3 + j:HALO - 3 + j + tm], axis=0, keepdims=True) for j in range(CONV_K)]

        @pl.when(i == 0)
        def _():
            dw_ref[...] = jnp.zeros_like(dw_ref)
            db_ref[...] = jnp.zeros_like(db_ref)

        dw_ref[...] += jnp.concatenate(dws, axis=0)
        db_ref[...] += jnp.sum(dpre, axis=0, keepdims=True)

    tile = lambda off: pl.BlockSpec((tm, CONV_CB), lambda j, i: (i, off + j))
    nxt = lambda off: pl.BlockSpec((HALO, CONV_CB), lambda j, i: (jnp.minimum((i + 1) * hb, last_hb), off + j))
    return pl.pallas_call(
        body, grid=(n_ch // CONV_CB, nt), name=name,
        in_specs=[tile(0), nxt(0), tile(cbp), nxt(cbp), tile(cb0),
                  pl.BlockSpec((HALO, CONV_CB), lambda j, i: (jnp.maximum(i * hb - 1, 0), cb0 + j)),
                  pl.BlockSpec((CONV_K, CONV_CB), lambda j, i: (0, cbp + j))],
        out_specs=[tile(0), pl.BlockSpec((CONV_K, CONV_CB), lambda j, i: (0, j)), pl.BlockSpec((1, CONV_CB), lambda j, i: (0, j))],
        out_shape=[jax.ShapeDtypeStruct((t_len, n_ch), F32), jax.ShapeDtypeStruct((CONV_K, n_ch), F32),
                   jax.ShapeDtypeStruct((1, n_ch), F32)],
        compiler_params=_cparams(("parallel", "arbitrary")),
    )(dact, dact, pre, pre, p, p, w)


@jax.custom_vjp
def _inv_unit_lower_many(a_cat):
    c = a_cat.shape[0]
    n = a_cat.shape[1] // c
    x = (_iota2(a_cat.shape, 0) == _iota2(a_cat.shape, 1) % c).astype(F32)
    for j in range(c - 1):
        col = jnp.concatenate([jnp.broadcast_to(a_cat[:, i * c + j:i * c + j + 1], (c, c)) for i in range(n)], axis=-1)
        x = x - col * x[j:j + 1, :]
    return x


def _inv_fwd(a_cat):
    x = _inv_unit_lower_many(a_cat)
    return x, x


def _inv_bwd(x, g):
    c = x.shape[0]
    parts = [-_dot(x[:, s], _dot(g[:, s], x[:, s], "nt", True), "tn", True)
             for s in (slice(i * c, (i + 1) * c) for i in range(x.shape[1] // c))]
    return (jnp.concatenate(parts, axis=-1),)


_inv_unit_lower_many.defvjp(_inv_fwd, _inv_bwd)


def _l2norm(x):
    return x * lax.rsqrt(jnp.sum(x * x, axis=-1, keepdims=True) + NORM_EPS)


def _rms(x):
    return x * lax.rsqrt(jnp.mean(x * x, axis=-1, keepdims=True) + NORM_EPS)


def _tri_masks(c):
    rows, cols = _iota2((c, c), 0), _iota2((c, c), 1)
    return rows >= cols, rows > cols, (rows >= cols).astype(F32), (rows <= cols).astype(F32)


def _gdn_tile(q, k, v, misc, s0, alog, dtb):
    c = CHUNK
    lower, strict, ltri, utri = _tri_masks(c)
    n_chunk = q.shape[0] // c
    pre = []
    for h in range(GDN_HEADS):
        hs = slice(h * LANE, (h + 1) * LANE)
        neg_a = -jnp.exp(alog[:, h:h + 1])
        for ci in range(n_chunk):
            sl = slice(ci * c, (ci + 1) * c)
            qn = _l2norm(q[sl, hs]) * (GDN_DK ** -0.5)
            kn = _l2norm(k[sl, hs])
            beta = jax.nn.sigmoid(misc[sl, 64 + h:65 + h])
            g = neg_a * jax.nn.softplus(misc[sl, 68 + h:69 + h] + dtb[:, h:h + 1])
            gb = jnp.broadcast_to(g, (c, c))
            gc_col = _mm(ltri, gb, "nn", True)
            gc_row = _mm(gb, utri, "tn", True)
            decay = jnp.where(lower, jnp.exp(jnp.where(lower, gc_col - gc_row, 0.0)), 0.0)
            kb = kn * beta
            a_mat = jnp.where(strict, _mm(kb, kn, "nt", False) * decay, 0.0)
            pre.append((qn, kn, kb, v[sl, hs] * beta, decay, gc_col[:, 0:1], gc_col[c - 1:c, 0:1], a_mat))
    t_all = _inv_unit_lower_many(jnp.concatenate([p[7] for p in pre], axis=-1))
    o_heads, s_heads = [], []
    for h in range(GDN_HEADS):
        s = s0[h * GDN_DK:(h + 1) * GDN_DK]
        outs = []
        for ci in range(n_chunk):
            i = h * n_chunk + ci
            qn, kn, kb, vb, decay, gc, g_last, _ = pre[i]
            t_inv = t_all[:, i * c:(i + 1) * c]
            u = _mm(t_inv, vb, "nn", True)
            w = _mm(t_inv, kb * jnp.exp(gc), "nn", True)
            attn = _mm(qn, kn, "nt", False) * decay
            k_end = kn * jnp.exp(g_last - gc)
            q_start = qn * jnp.exp(gc)
            v_new = u - _mm(w, s, "nn", False)
            outs.append(_mm(q_start, s, "nn", False) + _mm(attn, v_new, "nn", False))
            s = s * jnp.exp(g_last) + _mm(k_end, v_new, "tn", False)
        o_heads.append(jnp.concatenate(outs, axis=0))
        s_heads.append(s)
    return jnp.concatenate(o_heads, axis=-1), jnp.concatenate(s_heads, axis=0)


def _gdn_specs(tt, rev_n=None):
    t = (lambda i: i) if rev_n is None else (lambda i: rev_n - 1 - i)
    col = lambda j: pl.BlockSpec((tt, GDN_HEADS * LANE), lambda i: (t(i), j))
    vec = pl.BlockSpec((1, LANE), lambda i: (0, 0))
    misc = pl.BlockSpec((tt, LANE), lambda i: (t(i), EV_MISC // LANE))
    return [col(0), col(1), col(2), misc, vec, vec], t


def _gdn_fwd(act, p, alog, dtb, *, name):
    t_len = act.shape[0]
    tt = min(SEQ_TILE, t_len)
    ntile = t_len // tt
    in_specs, _ = _gdn_specs(tt)

    def body(q_ref, k_ref, v_ref, m_ref, al_ref, dt_ref, o_ref, s_ref, state):
        @pl.when(pl.program_id(0) == 0)
        def _():
            state[...] = jnp.zeros_like(state)

        s_ref[0] = state[...]
        o, s_new = _gdn_tile(q_ref[...], k_ref[...], v_ref[...], m_ref[...], state[...], al_ref[...], dt_ref[...])
        o_ref[...] = o
        state[...] = s_new

    return pl.pallas_call(
        body, grid=(ntile,), name=name, in_specs=in_specs,
        out_specs=[pl.BlockSpec((tt, GDN_HEADS * LANE), lambda i: (i, 0)),
                   pl.BlockSpec((1, GDN_HEADS * GDN_DK, LANE), lambda i: (i, 0, 0))],
        out_shape=[jax.ShapeDtypeStruct((t_len, GDN_HEADS * LANE), F32),
                   jax.ShapeDtypeStruct((ntile, GDN_HEADS * GDN_DK, LANE), F32)],
        scratch_shapes=[pltpu.VMEM((GDN_HEADS * GDN_DK, LANE), F32)],
        compiler_params=_cparams(("arbitrary",)),
    )(act, act, act, p, alog, dtb)


def _gdn_bwd(act, p, alog, dtb, states, do, *, name):
    t_len = act.shape[0]
    tt = min(SEQ_TILE, t_len)
    ntile = t_len // tt
    in_specs, t = _gdn_specs(tt, ntile)

    def body(q_ref, k_ref, v_ref, m_ref, al_ref, dt_ref, s0_ref, do_ref,
             dq_ref, dk_ref, dv_ref, dm_ref, dal_ref, ddt_ref, dstate):
        @pl.when(pl.program_id(0) == 0)
        def _():
            dstate[...] = jnp.zeros_like(dstate)
            dal_ref[...] = jnp.zeros_like(dal_ref)
            ddt_ref[...] = jnp.zeros_like(ddt_ref)

        _, vjp = jax.vjp(_gdn_tile, q_ref[...], k_ref[...], v_ref[...], m_ref[...], s0_ref[0], al_ref[...], dt_ref[...])
        dq, dk, dv, dm, ds0, dal, ddt = vjp((do_ref[...], dstate[...]))
        dq_ref[...] = dq
        dk_ref[...] = dk
        dv_ref[...] = dv
        dm_ref[...] = dm
        dstate[...] = ds0
        dal_ref[...] += dal
        ddt_ref[...] += ddt

    row = pl.BlockSpec((tt, GDN_HEADS * LANE), lambda i: (t(i), 0))
    vec = pl.BlockSpec((1, LANE), lambda i: (0, 0))
    return pl.pallas_call(
        body, grid=(ntile,), name=name,
        in_specs=in_specs + [pl.BlockSpec((1, GDN_HEADS * GDN_DK, LANE), lambda i: (t(i), 0, 0)), row],
        out_specs=[row, row, row, pl.BlockSpec((tt, LANE), lambda i: (t(i), 0)), vec, vec],
        out_shape=[jax.ShapeDtypeStruct((t_len, GDN_HEADS * LANE), F32)] * 3
        + [jax.ShapeDtypeStruct((t_len, LANE), F32)] + [jax.ShapeDtypeStruct((1, LANE), F32)] * 2,
        scratch_shapes=[pltpu.VMEM((GDN_HEADS * GDN_DK, LANE), F32)],
        compiler_params=_cparams(("arbitrary",)),
    )(act, act, act, p, alog, dtb, states, do)


def _head_expand():
    return jnp.asarray(np.repeat(np.eye(LANE, SSD_HEADS, dtype=np.float32), SSD_HEADDIM, axis=1))


def _ssd_tile(xs, bm, cm, dtr, hs0, alog, dtb, dsk, expand):
    c = CHUNK
    gw = SSD_D_INNER // SSD_GROUPS
    hpg = SSD_HEADS // SSD_GROUPS
    lower, _, ltri, utri = _tri_masks(c)
    half = _iota2((c, LANE), 1) // SSD_HEADDIM
    dt = jax.nn.softplus(dtr + dtb)
    da = dt * (-jnp.exp(alog))
    xdt = xs * _mm(dt, expand, "nn", True)
    d_x = _mm(jnp.broadcast_to(dsk, (8, LANE)), expand, "nn", True)[0:1]
    hs = [hs0[g * SSD_STATE:(g + 1) * SSD_STATE] for g in range(SSD_GROUPS)]
    ys = []
    for ci in range(xs.shape[0] // c):
        sl = slice(ci * c, (ci + 1) * c)
        acs = _mm(ltri, da[sl], "nn", True)
        acs_t = _mm(da[sl], utri, "tn", True)
        acs_last = acs[c - 1:c, :]
        e_start = _mm(jnp.exp(acs), expand, "nn", True)
        e_end = _mm(jnp.exp(acs_last - acs), expand, "nn", True)
        e_dec = _mm(jnp.broadcast_to(jnp.exp(acs_last), (8, LANE)), expand, "nn", True)[0:1]
        xdt_c = xdt[sl]
        y_tiles = [None] * (SSD_D_INNER // LANE)
        y_off = []
        for g in range(SSD_GROUPS):
            b_g = bm[sl, g * SSD_STATE:(g + 1) * SSD_STATE]
            c_g = cm[sl, g * SSD_STATE:(g + 1) * SSD_STATE]
            gs = slice(g * gw, (g + 1) * gw)
            cb = _mm(c_g, b_g, "nt", False)
            y_off.append(_mm(c_g, hs[g], "nn", False) * e_start[:, gs])
            for r in range(hpg):
                h = g * hpg + r
                j = h // 2
                lm = jnp.where(lower, jnp.exp(jnp.where(lower, acs[:, h:h + 1] - acs_t[h:h + 1, :], 0.0)), 0.0)
                xm = jnp.where(half == (h % 2), xdt_c[:, j * LANE:(j + 1) * LANE], 0.0)
                part = _mm(cb * lm, xm, "nn", False)
                y_tiles[j] = part if y_tiles[j] is None else y_tiles[j] + part
            hs[g] = hs[g] * e_dec[:, gs] + _mm(b_g, xdt_c[:, gs] * e_end[:, gs], "tn", False)
        ys.append(jnp.concatenate(y_tiles, axis=-1) + jnp.concatenate(y_off, axis=-1) + d_x * xs[sl])
    return jnp.concatenate(ys, axis=0), jnp.concatenate(hs, axis=0)


def _ssd_specs(tt, rev_n=None):
    t = (lambda i: i) if rev_n is None else (lambda i: rev_n - 1 - i)
    vec = pl.BlockSpec((1, LANE), lambda i: (0, 0))
    specs = [pl.BlockSpec((tt, SSD_D_INNER), lambda i: (t(i), 0)),
             pl.BlockSpec((tt, 512), lambda i: (t(i), SSD_D_INNER // 512)),
             pl.BlockSpec((tt, 512), lambda i: (t(i), SSD_D_INNER // 512 + 1)),
             pl.BlockSpec((tt, LANE), lambda i: (t(i), OD_DT // LANE)), vec, vec, vec,
             pl.BlockSpec((LANE, SSD_D_INNER), lambda i: (0, 0))]
    return specs, t


def _ssd_fwd(act, p, alog, dtb, dsk, *, name):
    t_len = act.shape[0]
    tt = min(SEQ_TILE, t_len)
    ntile = t_len // tt
    in_specs, _ = _ssd_specs(tt)

    def body(x_ref, b_ref, c_ref, dt_ref, al_ref, db_ref, dk_ref, e_ref, y_ref, s_ref, state):
        @pl.when(pl.program_id(0) == 0)
        def _():
            state[...] = jnp.zeros_like(state)

        s_ref[0] = state[...]
        y, hs = _ssd_tile(x_ref[...], b_ref[...], c_ref[...], dt_ref[...], state[...], al_ref[...], db_ref[...],
                          dk_ref[...], e_ref[...])
        y_ref[...] = y
        state[...] = hs

    return pl.pallas_call(
        body, grid=(ntile,), name=name, in_specs=in_specs,
        out_specs=[pl.BlockSpec((tt, SSD_D_INNER), lambda i: (i, 0)),
                   pl.BlockSpec((1, SSD_GROUPS * SSD_STATE, 512), lambda i: (i, 0, 0))],
        out_shape=[jax.ShapeDtypeStruct((t_len, SSD_D_INNER), F32),
                   jax.ShapeDtypeStruct((ntile, SSD_GROUPS * SSD_STATE, 512), F32)],
        scratch_shapes=[pltpu.VMEM((SSD_GROUPS * SSD_STATE, 512), F32)],
        compiler_params=_cparams(("arbitrary",)),
    )(act, act, act, p, alog, dtb, dsk, _head_expand())


def _ssd_bwd(act, p, alog, dtb, dsk, states, dy, *, name):
    t_len = act.shape[0]
    tt = min(SEQ_TILE, t_len)
    ntile = t_len // tt
    in_specs, t = _ssd_specs(tt, ntile)

    def body(x_ref, b_ref, c_ref, dt_ref, al_ref, db_ref, dk_ref, e_ref, s0_ref, dy_ref,
             dx_ref, dbm_ref, dcm_ref, ddt_ref, dal_ref, ddb_ref, ddk_ref, dstate):
        @pl.when(pl.program_id(0) == 0)
        def _():
            dstate[...] = jnp.zeros_like(dstate)
            dal_ref[...] = jnp.zeros_like(dal_ref)
            ddb_ref[...] = jnp.zeros_like(ddb_ref)
            ddk_ref[...] = jnp.zeros_like(ddk_ref)

        expand = e_ref[...]

        def f(xs, bm, cm, dtr, hs0, al, db, dk):
            return _ssd_tile(xs, bm, cm, dtr, hs0, al, db, dk, expand)

        _, vjp = jax.vjp(f, x_ref[...], b_ref[...], c_ref[...], dt_ref[...], s0_ref[0], al_ref[...], db_ref[...],
                         dk_ref[...])
        dx, dbm, dcm, ddt, dhs, dal, ddb, ddk = vjp((dy_ref[...], dstate[...]))
        dx_ref[...] = dx
        dbm_ref[...] = dbm
        dcm_ref[...] = dcm
        ddt_ref[...] = ddt
        dstate[...] = dhs
        dal_ref[...] += dal
        ddb_ref[...] += ddb
        ddk_ref[...] += ddk

    vec = pl.BlockSpec((1, LANE), lambda i: (0, 0))
    rows = lambda w: pl.BlockSpec((tt, w), lambda i: (t(i), 0))
    return pl.pallas_call(
        body, grid=(ntile,), name=name,
        in_specs=in_specs + [pl.BlockSpec((1, SSD_GROUPS * SSD_STATE, 512), lambda i: (t(i), 0, 0)), rows(SSD_D_INNER)],
        out_specs=[rows(SSD_D_INNER), rows(512), rows(512), rows(LANE), vec, vec, vec],
        out_shape=[jax.ShapeDtypeStruct((t_len, SSD_D_INNER), F32), jax.ShapeDtypeStruct((t_len, 512), F32),
                   jax.ShapeDtypeStruct((t_len, 512), F32), jax.ShapeDtypeStruct((t_len, LANE), F32)]
        + [jax.ShapeDtypeStruct((1, LANE), F32)] * 3,
        scratch_shapes=[pltpu.VMEM((SSD_GROUPS * SSD_STATE, 512), F32)],
        compiler_params=_cparams(("arbitrary",)),
    )(act, act, act, p, alog, dtb, dsk, _head_expand(), states, dy)


ATT_SCALE = (MLA_NOPE + MLA_ROPE) ** -0.5
QK_W = 2 * LANE


def _chunk_mask(tq):
    return (_iota2((tq, tq), 1) // CHUNK) <= (_iota2((tq, tq), 0) // CHUNK)


def _attn_fwd(qc, kc, vv, *, name):
    t_len = qc.shape[0]
    tq = min(ATT_TILE, t_len)
    nq = t_len // tq

    def body(q_ref, k_ref, v_ref, o_ref, lse_ref, m_s, l_s, acc_s):
        qi, ki = pl.program_id(1), pl.program_id(2)

        @pl.when(ki == 0)
        def _():
            m_s[...] = jnp.full_like(m_s, -jnp.inf)
            l_s[...] = jnp.zeros_like(l_s)
            acc_s[...] = jnp.zeros_like(acc_s)

        def step(masked):
            s = _dot(q_ref[...], k_ref[...], "nt") * ATT_SCALE
            if masked:
                s = jnp.where(_chunk_mask(tq), s, -jnp.inf)
            m_new = jnp.maximum(m_s[...], jnp.max(s, axis=-1, keepdims=True))
            alpha = jnp.exp(m_s[...] - m_new)
            p = jnp.exp(s - m_new)
            l_s[...] = alpha * l_s[...] + jnp.sum(p, axis=-1, keepdims=True)
            acc_s[...] = alpha * acc_s[...] + _dot(p, v_ref[...], "nn")
            m_s[...] = m_new

        @pl.when(ki < qi)
        def _():
            step(False)

        @pl.when(ki == qi)
        def _():
            step(True)
            o_ref[...] = acc_s[...] / l_s[...]
            lse_ref[...] = jnp.broadcast_to(m_s[...] + jnp.log(l_s[...]), lse_ref.shape)

    kv_idx = lambda h, i, k: (jnp.minimum(k, i), h)
    return pl.pallas_call(
        body, grid=(MLA_HEADS, nq, nq), name=name,
        in_specs=[pl.BlockSpec((tq, QK_W), lambda h, i, k: (i, h)), pl.BlockSpec((tq, QK_W), kv_idx),
                  pl.BlockSpec((tq, LANE), kv_idx)],
        out_specs=[pl.BlockSpec((tq, LANE), lambda h, i, k: (i, h))] * 2,
        out_shape=[jax.ShapeDtypeStruct((t_len, MLA_HEADS * LANE), F32)] * 2,
        scratch_shapes=[pltpu.VMEM((tq, 1), F32), pltpu.VMEM((tq, 1), F32), pltpu.VMEM((tq, LANE), F32)],
        compiler_params=_cparams(("parallel", "parallel", "arbitrary")),
    )(qc, kc, vv)


def _attn_probs(q, k, v, do, o, lse, masked, tq):
    s = _dot(q, k, "nt") * ATT_SCALE
    if masked:
        s = jnp.where(_chunk_mask(tq), s, -jnp.inf)
    p = jnp.exp(s - lse[:, 0:1])
    delta = jnp.sum(do * o, axis=-1, keepdims=True)
    ds = p * (_dot(do, v, "nt") - delta) * ATT_SCALE
    return p, ds


def _attn_bwd_q(qc, kc, vv, o, lse, do, *, name):
    t_len = qc.shape[0]
    tq = min(ATT_TILE, t_len)
    nq = t_len // tq

    def body(q_ref, k_ref, v_ref, o_ref, lse_ref, do_ref, dq_ref, acc_s):
        qi, ki = pl.program_id(1), pl.program_id(2)

        @pl.when(ki == 0)
        def _():
            acc_s[...] = jnp.zeros_like(acc_s)

        def step(masked):
            _, ds = _attn_probs(q_ref[...], k_ref[...], v_ref[...], do_ref[...], o_ref[...], lse_ref[...], masked, tq)
            acc_s[...] += _dot(ds, k_ref[...], "nn")

        @pl.when(ki < qi)
        def _():
            step(False)

        @pl.when(ki == qi)
        def _():
            step(True)
            dq_ref[...] = acc_s[...]

    kv_idx = lambda h, i, k: (jnp.minimum(k, i), h)
    q_idx = lambda h, i, k: (i, h)
    return pl.pallas_call(
        body, grid=(MLA_HEADS, nq, nq), name=name,
        in_specs=[pl.BlockSpec((tq, QK_W), q_idx), pl.BlockSpec((tq, QK_W), kv_idx), pl.BlockSpec((tq, LANE), kv_idx),
                  pl.BlockSpec((tq, LANE), q_idx), pl.BlockSpec((tq, LANE), q_idx), pl.BlockSpec((tq, LANE), q_idx)],
        out_specs=pl.BlockSpec((tq, QK_W), q_idx),
        out_shape=jax.ShapeDtypeStruct((t_len, MLA_HEADS * QK_W), F32),
        scratch_shapes=[pltpu.VMEM((tq, QK_W), F32)],
        compiler_params=_cparams(("parallel", "parallel", "arbitrary")),
    )(qc, kc, vv, o, lse, do)


def _attn_bwd_kv(qc, kc, vv, o, lse, do, *, name):
    t_len = qc.shape[0]
    tq = min(ATT_TILE, t_len)
    nq = t_len // tq

    def body(q_ref, k_ref, v_ref, o_ref, lse_ref, do_ref, dk_ref, dv_ref, dk_s, dv_s):
        ki, qi = pl.program_id(1), pl.program_id(2)

        @pl.when(qi == 0)
        def _():
            dk_s[...] = jnp.zeros_like(dk_s)
            dv_s[...] = jnp.zeros_like(dv_s)

        def step(masked):
            p, ds = _attn_probs(q_ref[...], k_ref[...], v_ref[...], do_ref[...], o_ref[...], lse_ref[...], masked, tq)
            dv_s[...] += _dot(p, do_ref[...], "tn")
            dk_s[...] += _dot(ds, q_ref[...], "tn")

        @pl.when(qi > ki)
        def _():
            step(False)

        @pl.when(qi == ki)
        def _():
            step(True)

        @pl.when(qi == nq - 1)
        def _():
            dk_ref[...] = dk_s[...]
            dv_ref[...] = dv_s[...]

    q_idx = lambda h, k, i: (jnp.maximum(i, k), h)
    k_idx = lambda h, k, i: (k, h)
    return pl.pallas_call(
        body, grid=(MLA_HEADS, nq, nq), name=name,
        in_specs=[pl.BlockSpec((tq, QK_W), q_idx), pl.BlockSpec((tq, QK_W), k_idx), pl.BlockSpec((tq, LANE), k_idx),
                  pl.BlockSpec((tq, LANE), q_idx), pl.BlockSpec((tq, LANE), q_idx), pl.BlockSpec((tq, LANE), q_idx)],
        out_specs=[pl.BlockSpec((tq, QK_W), k_idx), pl.BlockSpec((tq, LANE), k_idx)],
        out_shape=[jax.ShapeDtypeStruct((t_len, MLA_HEADS * QK_W), F32), jax.ShapeDtypeStruct((t_len, MLA_HEADS * LANE), F32)],
        scratch_shapes=[pltpu.VMEM((tq, QK_W), F32), pltpu.VMEM((tq, LANE), F32)],
        compiler_params=_cparams(("parallel", "parallel", "arbitrary")),
    )(qc, kc, vv, o, lse, do)


def _adaln_fn(x, g, shift, scale):
    return ((_rms(x) * g) * (1.0 + scale) + shift,)


def _resid_fn(coef, y, x, gate):
    return (x + coef * gate * y,)


def _rms2_fn(cq, ckv, gq, gkv):
    return _rms(cq) * gq, _rms(ckv) * gkv


@jax.custom_vjp
def _swap_halves(x):
    return jnp.concatenate([x[:, 32:64], x[:, 0:32], x[:, 64:128]], axis=-1)


_swap_halves.defvjp(lambda x: (_swap_halves(x), None), lambda _, g: (_swap_halves(g),))


def _rope_fn(q, kv, misc, pos, invf, sgn):
    ang = pos * invf
    cos, sin = jnp.cos(ang), jnp.sin(ang) * sgn

    def rope(x):
        return x * cos + _swap_halves(x) * sin

    k_pe = rope(jnp.where(_iota2(misc.shape, 1) < MLA_ROPE, misc, 0.0))
    qs, ks = [], []
    for h in range(MLA_HEADS):
        qs += [q[:, h * LANE:(h + 1) * LANE], rope(q[:, (MLA_HEADS + h) * LANE:(MLA_HEADS + h + 1) * LANE])]
        ks += [kv[:, h * LANE:(h + 1) * LANE], k_pe]
    return jnp.concatenate(qs, axis=-1), jnp.concatenate(ks, axis=-1), kv[:, MLA_HEADS * LANE:]


def _ev_out_fn(oa, z, ob, g):
    parts = []
    for h in range(GDN_HEADS):
        hs = slice(h * LANE, (h + 1) * LANE)
        zz = z[:, hs]
        parts.append(_rms(oa[:, hs]) * g * (zz * jax.nn.sigmoid(zz)))
    return (jnp.concatenate(parts + [ob], axis=-1),)


def _od_out_fn(y, z, g):
    yz = y * (z * jax.nn.sigmoid(z))
    gw = SSD_D_INNER // SSD_GROUPS
    return (jnp.concatenate([_rms(yz[:, i * gw:(i + 1) * gw]) for i in range(SSD_GROUPS)], axis=-1) * g,)


def _loss_bwd(x, tgt, g, *, name):
    t_len, d = x.shape
    tm = min(ROW_TILE // 2, t_len)

    def body(x_ref, t_ref, g_ref, loss_ref, dx_ref, dg_ref):
        tgt_v = t_ref[...]

        def f(xv, gv):
            err = _rms(xv) * gv - tgt_v
            return 0.5 * jnp.sum(jnp.mean(err * err, axis=-1, keepdims=True), axis=0, keepdims=True)

        val, vjp = jax.vjp(f, x_ref[...], g_ref[...])
        dx, dg = vjp(jnp.ones((1, 1), F32))
        dx_ref[...] = dx

        @pl.when(pl.program_id(0) == 0)
        def _():
            loss_ref[...] = jnp.zeros_like(loss_ref)
            dg_ref[...] = jnp.zeros_like(dg_ref)

        loss_ref[...] += jnp.broadcast_to(val, loss_ref.shape)
        dg_ref[...] += dg

    row = pl.BlockSpec((tm, d), lambda i: (i, 0))
    return pl.pallas_call(
        body, grid=(t_len // tm,), name=name,
        in_specs=[row, row, pl.BlockSpec((1, d), lambda i: (0, 0))],
        out_specs=[pl.BlockSpec((1, LANE), lambda i: (0, 0)), row, pl.BlockSpec((1, d), lambda i: (0, 0))],
        out_shape=[jax.ShapeDtypeStruct((1, LANE), F32), jax.ShapeDtypeStruct((t_len, d), F32),
                   jax.ShapeDtypeStruct((1, d), F32)],
        compiler_params=_cparams(("arbitrary",)),
    )(x, tgt, g)


def _mesh_pos():
    return lax.axis_index("x"), lax.axis_index("y"), lax.axis_index("c")


def _exchange(xs, scatter, *, name):
    n_arr = len(xs)

    def body(*refs):
        in_refs, out_refs = refs[:n_arr], refs[n_arr:2 * n_arr]
        send_sems, recv_sems, local_sems = refs[2 * n_arr:]
        mx, my, mc = _mesh_pos()
        me = 4 * mx + 2 * my + mc
        started = []
        for a, (in_ref, out_ref) in enumerate(zip(in_refs, out_refs)):
            def src(j, in_ref=in_ref):
                return in_ref.at[j] if scatter else in_ref

            local = pltpu.make_async_copy(src(me), out_ref.at[me], local_sems.at[a])
            local.start()
            started.append((local, None))
            for d in range(1, N_DEV):
                px = 1 - mx if d & 4 else mx
                py = 1 - my if d & 2 else my
                pc = 1 - mc if d & 1 else mc
                peer = 4 * px + 2 * py + pc
                sem = a * (N_DEV - 1) + d - 1
                send = pltpu.make_async_remote_copy(
                    src_ref=src(peer), dst_ref=out_ref.at[me], send_sem=send_sems.at[sem], recv_sem=recv_sems.at[sem],
                    device_id=(px, py, pc), device_id_type=pl.DeviceIdType.MESH)
                send.start()
                recv = pltpu.make_async_remote_copy(
                    src_ref=src(peer), dst_ref=out_ref.at[peer], send_sem=send_sems.at[sem], recv_sem=recv_sems.at[sem],
                    device_id=(px, py, pc), device_id_type=pl.DeviceIdType.MESH)
                started.append((send, recv))
        for first, recv in started:
            if recv is None:
                first.wait()
            else:
                first.wait_send()
                recv.wait_recv()

    blocks = [tuple(x.shape[1:]) if scatter else tuple(x.shape) for x in xs]
    return pl.pallas_call(
        body, name=name,
        in_specs=[pl.BlockSpec(memory_space=pl.ANY)] * n_arr,
        out_specs=[pl.BlockSpec(memory_space=pl.ANY)] * n_arr,
        out_shape=[jax.ShapeDtypeStruct((N_DEV,) + b, x.dtype) for b, x in zip(blocks, xs)],
        scratch_shapes=[pltpu.SemaphoreType.DMA((n_arr * (N_DEV - 1),)), pltpu.SemaphoreType.DMA((n_arr * (N_DEV - 1),)),
                        pltpu.SemaphoreType.DMA((n_arr,))],
        compiler_params=pltpu.CompilerParams(has_side_effects=True),
    )(*xs)


def _peer_of(d, pos):
    mx, my, mc = pos
    px = 1 - mx if d & 4 else mx
    py = 1 - my if d & 2 else my
    pc = 1 - mc if d & 1 else mc
    return (px, py, pc), 4 * px + 2 * py + pc


_HBM = pl.BlockSpec(memory_space=pltpu.HBM)
_SEM = pl.BlockSpec(memory_space=pltpu.SEMAPHORE)


def _exchange_start(xs, scatter, *, name):
    n_arr = len(xs)
    n_sem = n_arr * (N_DEV - 1)

    def body(*refs):
        in_refs, land_refs = refs[:n_arr], refs[n_arr:2 * n_arr]
        send_sems, recv_sems, token = refs[2 * n_arr], refs[2 * n_arr + 1], refs[-1]
        pos = _mesh_pos()
        me = 4 * pos[0] + 2 * pos[1] + pos[2]
        for a in range(n_arr):
            for d in range(1, N_DEV):
                dev, peer = _peer_of(d, pos)
                sem = a * (N_DEV - 1) + d - 1
                pltpu.make_async_remote_copy(
                    src_ref=in_refs[a].at[peer] if scatter else in_refs[a], dst_ref=land_refs[a].at[me],
                    send_sem=send_sems.at[sem], recv_sem=recv_sems.at[sem], device_id=dev,
                    device_id_type=pl.DeviceIdType.MESH).start()
        token[...] = jnp.zeros_like(token)

    blocks = [tuple(x.shape[1:]) if scatter else tuple(x.shape) for x in xs]
    srcs = [pltpu.with_memory_space_constraint(x, pltpu.HBM) for x in xs]
    lands = [pltpu.with_memory_space_constraint(lax.empty((N_DEV,) + b, x.dtype), pltpu.HBM) for b, x in zip(blocks, xs)]
    res = pl.pallas_call(
        body, name=name,
        out_shape=(pltpu.SemaphoreType.DMA((n_sem,)), pltpu.SemaphoreType.DMA((n_sem,)),
                   *[pltpu.HBM(a.shape, a.dtype) for a in srcs + lands], jax.ShapeDtypeStruct((8, LANE), F32)),
        in_specs=[_HBM] * (2 * n_arr),
        out_specs=(_SEM, _SEM, *[_HBM] * (2 * n_arr), pl.BlockSpec(memory_space=pltpu.VMEM)),
        input_output_aliases={i: 2 + i for i in range(2 * n_arr)},
        compiler_params=pltpu.CompilerParams(has_side_effects=pltpu.SideEffectType.DATAFLOW_SIDE_EFFECTING),
    )(*srcs, *lands)
    handle = dict(sems=res[:2], srcs=res[2:2 + n_arr], lands=res[2 + n_arr:2 + 2 * n_arr], scatter=scatter)
    return handle, res[-1][0, 0]


def _exchange_wait(handle, after, me, *, name):
    scatter = handle["scatter"]
    n_arr = len(handle["srcs"])

    def body(*refs):
        in_refs, land_refs = refs[:n_arr], refs[n_arr:2 * n_arr]
        send_sems, recv_sems = refs[2 * n_arr], refs[2 * n_arr + 1]
        pos = _mesh_pos()
        for a in range(n_arr):
            for d in range(1, N_DEV):
                dev, peer = _peer_of(d, pos)
                sem = a * (N_DEV - 1) + d - 1
                copy = pltpu.make_async_remote_copy(
                    src_ref=in_refs[a].at[peer] if scatter else in_refs[a], dst_ref=land_refs[a].at[peer],
                    send_sem=send_sems.at[sem], recv_sem=recv_sems.at[sem], device_id=dev,
                    device_id_type=pl.DeviceIdType.MESH)
                copy.wait_send()
                copy.wait_recv()

    thru = list(handle["srcs"]) + list(handle["lands"])
    res = pl.pallas_call(
        body, name=name,
        out_shape=tuple(pltpu.HBM(a.shape, a.dtype) for a in thru),
        in_specs=[_HBM] * (2 * n_arr) + [_SEM, _SEM, pl.BlockSpec(memory_space=pl.ANY)],
        out_specs=tuple([_HBM] * (2 * n_arr)),
        input_output_aliases={i: i for i in range(2 * n_arr)},
        compiler_params=pltpu.CompilerParams(has_side_effects=pltpu.SideEffectType.DATAFLOW_SIDE_EFFECTING),
    )(*thru, *handle["sems"], after)
    out = []
    for src, land in zip(res[:n_arr], res[n_arr:]):
        own = lax.dynamic_index_in_dim(src, me, axis=0, keepdims=True) if scatter else src[None]
        out.append(lax.dynamic_update_index_in_dim(land, own, me, axis=0))
    return out


def _cols(srcs, rows, plans, out_dtype, *, name):
    n_src = len(srcs)
    rb = _pick(rows, (256, 128, 64, 32, 16, 8))

    def width(pieces):
        return sum(p[1] if p[0] == "z" else p[3] - p[2] for p in pieces)

    def body(*refs):
        ins, outs = refs[:n_src], refs[n_src:]
        loaded = {}
        for o_ref, plan in zip(outs, plans):
            for j, pieces in enumerate(plan):
                vals = []
                for pc in pieces:
                    if pc[0] == "z":
                        vals.append(jnp.zeros((rb, pc[1]), out_dtype))
                    else:
                        si, sj, c0, c1 = pc
                        if (si, sj) not in loaded:
                            loaded[(si, sj)] = ins[si][sj]
                        vals.append(loaded[(si, sj)][:, c0:c1].astype(out_dtype))
                o_ref[j] = vals[0] if len(vals) == 1 else jnp.concatenate(vals, axis=-1)

    for arr, r0 in srcs:
        assert r0 % rb == 0
    return pl.pallas_call(
        body, grid=(rows // rb,), name=name,
        in_specs=[pl.BlockSpec((arr.shape[0], rb, arr.shape[2]), lambda i, r0=r0 // rb: (0, r0 + i, 0)) for arr, r0 in srcs],
        out_specs=[pl.BlockSpec((len(p), rb, width(p[0])), lambda i: (0, i, 0)) for p in plans],
        out_shape=[jax.ShapeDtypeStruct((len(p), rows, width(p[0])), out_dtype) for p in plans],
        compiler_params=_cparams(("parallel",)),
    )(*[arr for arr, _ in srcs])


def _shard_pieces(src, a, b, shard_w):
    out = []
    while a < b:
        s = a // shard_w
        e = min(b, (s + 1) * shard_w)
        out.append((src, s, a - s * shard_w, e - s * shard_w))
        a = e
    return out


def _mapped_pieces(a, b, segs):
    out = []
    for n0, n1, k0 in sorted(segs):
        lo, hi = max(a, n0), min(b, n1)
        if lo < hi:
            out.append((0, 0, k0 + lo - n0, k0 + hi - n0))
    return out


_EV_SEGS = [(0, 1536, EV_QKV), (1536, 2048, EV_Z), (2048, 2056, EV_MISC + MLA_ROPE), (2056, 2440, EV_CQ),
            (2440, 2696, EV_CKV), (2696, 2760, EV_MISC)]
EV_NAT_W, OD_NAT_W = 2760, 5152


PACK_W = 1024


def _adamw(w, gparts, m, v, *, name):
    n_rows, n_cols = w.shape
    n_parts = gparts.shape[0]
    tm = _pick(n_rows, (512, 256, 128, 64, 32, 16, 8))
    while n_parts * tm * n_cols * 4 > 4 * 1024 * 1024 and tm % 16 == 0:
        tm //= 2

    def body(w_ref, g_ref, m_ref, v_ref, go_ref, d_ref, mo_ref, vo_ref):
        g = g_ref[0]
        for j in range(1, n_parts):
            g = g + g_ref[j]
        m_new = ADAM_B1 * m_ref[...] + (1.0 - ADAM_B1) * g
        v_new = ADAM_B2 * v_ref[...] + (1.0 - ADAM_B2) * jnp.square(g)
        m_hat = m_new / (1.0 - ADAM_B1 ** ADAM_STEP)
        v_hat = v_new / (1.0 - ADAM_B2 ** ADAM_STEP)
        go_ref[...] = g
        d_ref[...] = -ADAM_LR * (m_hat / (jnp.sqrt(v_hat) + ADAM_EPS) + ADAM_WD * w_ref[...])
        mo_ref[...] = m_new
        vo_ref[...] = v_new

    row = pl.BlockSpec((tm, n_cols), lambda i: (i, 0))
    return pl.pallas_call(
        body, grid=(n_rows // tm,), name=name,
        in_specs=[row, pl.BlockSpec((n_parts, tm, n_cols), lambda i: (0, i, 0)), row, row],
        out_specs=[row] * 4,
        out_shape=[jax.ShapeDtypeStruct((n_rows, n_cols), F32)] * 4,
        compiler_params=_cparams(("parallel",)),
    )(w, gparts, m, v)


def _adamw_nd(w, gparts, m, v, *, name):
    shape = w.shape
    two = (-1, shape[-1])
    outs = _adamw(w.reshape(two), gparts.reshape((gparts.shape[0],) + (int(np.prod(shape[:-1])), shape[-1])),
                  m.reshape(two), v.reshape(two), name=name)
    return [o.reshape(shape) for o in outs]


def _pack(parts):
    flat = [p.astype(F32).reshape(-1) for p in parts]
    n_pad = -sum(f.shape[0] for f in flat) % (8 * PACK_W)
    return jnp.concatenate(flat + [jnp.zeros((n_pad,), F32)]).reshape(-1, PACK_W)


def _unpack(packed, shapes):
    flat = packed.reshape(-1)
    out, off = [], 0
    for s in shapes:
        n = int(np.prod(s))
        out.append(flat[off:off + n].reshape(tuple(s)))
        off += n
    return out


def _mod_shard(c_all, ada_w, ada_b_shard, *, name):
    n_layer, d, n_col = ada_w.shape

    def body(c_ref, w_ref, b_ref, o_ref):
        cv = c_ref[...]
        o_ref[0] = _dot(cv * jax.nn.sigmoid(cv), w_ref[0], "nn") + b_ref[0]

    return pl.pallas_call(
        body, grid=(n_layer,), name=name,
        in_specs=[pl.BlockSpec((N_DEV, d), lambda l: (0, 0)), pl.BlockSpec((1, d, n_col), lambda l: (l, 0, 0)),
                  pl.BlockSpec((1, 1, n_col), lambda l: (l, 0, 0))],
        out_specs=pl.BlockSpec((1, N_DEV, n_col), lambda l: (l, 0, 0)),
        out_shape=jax.ShapeDtypeStruct((n_layer, N_DEV, n_col), F32),
        compiler_params=_cparams(("parallel",)),
    )(c_all, ada_w, ada_b_shard)


def _ada_w_grad(c_all, dmod_shard, *, name):
    n_layer, _, n_col = dmod_shard.shape
    d = c_all.shape[1]

    def body(c_ref, g_ref, o_ref):
        cv = c_ref[...]
        o_ref[0] = _dot(cv * jax.nn.sigmoid(cv), g_ref[0], "tn", True)

    return pl.pallas_call(
        body, grid=(n_layer,), name=name,
        in_specs=[pl.BlockSpec((N_DEV, d), lambda l: (0, 0)), pl.BlockSpec((1, N_DEV, n_col), lambda l: (l, 0, 0))],
        out_specs=pl.BlockSpec((1, d, n_col), lambda l: (l, 0, 0)),
        out_shape=jax.ShapeDtypeStruct((n_layer, d, n_col), F32),
        compiler_params=_cparams(("parallel",)),
    )(c_all, dmod_shard)


def _uq(w):
    r = w.shape[0]
    rope = jnp.pad(w[:, :, MLA_NOPE:], ((0, 0), (0, 0), (0, LANE - MLA_ROPE)))
    return jnp.concatenate([w[:, :, :MLA_NOPE].reshape(r, -1), rope.reshape(r, -1)], axis=1)


def _uq_back(d):
    r = d.shape[0]
    half = MLA_HEADS * LANE
    return jnp.concatenate([d[:, :half].reshape(r, MLA_HEADS, LANE),
                            d[:, half:].reshape(r, MLA_HEADS, LANE)[:, :, :MLA_ROPE]], axis=-1)


def _ukv(w):
    r = w.shape[0]
    return jnp.concatenate([w[:, :, :MLA_NOPE].reshape(r, -1), w[:, :, MLA_NOPE:].reshape(r, -1)], axis=1)


def _ukv_back(d):
    r = d.shape[0]
    half = MLA_HEADS * LANE
    return jnp.concatenate([d[:, :half].reshape(r, MLA_HEADS, LANE), d[:, half:].reshape(r, MLA_HEADS, LANE)], axis=-1)


def _lane_vec(v):
    return jnp.pad(v.astype(F32), (0, LANE - v.shape[0])).reshape(1, LANE)


def _row(v):
    return v.astype(F32).reshape(1, -1)


def _adaln(x, ln):
    return _rows(_adaln_fn, [x], list(ln), [(D_MODEL, BF16)], name="adaln")[0]


def _adaln_bwd(x, ln, dh, dxn):
    (dx,), dln = _rows_vjp(_adaln_fn, [x], [], list(ln), [], [dh], [F32], adds={0: dxn}, name="adaln_bwd")
    return dx, dln


def _resid(coef, y, x, gate):
    return _rows(functools.partial(_resid_fn, coef), [y, x], [gate], [(D_MODEL, F32)], name="resid")[0]


def _gated_fn(coef, y, gate):
    return (coef * gate * y,)


def _resid_bwd(coef, y, gate, dxn):
    (dy,), (dgate,) = _rows_vjp(functools.partial(_gated_fn, coef), [y], [], [gate], [], [dxn], [BF16], name="resid_bwd")
    return dy, dgate


def _ffn_fwd(x, ln, gate, w13, w2):
    h = _adaln(x, ln)
    s = _ffn_act(h, w13, name="ffn_act")
    y = _matmul(s, w2, "nn", F32, name="ffn_down")
    return _resid(0.5, y, x, gate), (x, h, s, y)


def _ffn_bwd(saved, dxn, ln, gate, w13, w2):
    x, h, s, y = saved
    dy, dgate = _resid_bwd(0.5, y, gate, dxn)
    dab = _ffn_act_bwd(h, dy, w13, w2, name="ffn_act_bwd")
    dh = _matmul(dab, w13, "nt", F32, name="ffn_dh")
    dw13 = _matmul(h, dab, "tn", F32, name="ffn_dw13")
    dw2 = _matmul(s, dy, "tn", F32, name="ffn_dw2")
    dx, dln = _adaln_bwd(x, ln, dh, dxn)
    return dx, dw13, dw2, dln, dgate


def _rope_consts():
    half = MLA_ROPE // 2
    inv = (ROPE_THETA ** (-jnp.arange(half, dtype=F32) / half)).astype(F32)
    zeros = jnp.zeros((LANE - MLA_ROPE,), F32)
    invf = jnp.concatenate([inv, inv, zeros]).reshape(1, LANE)
    sgn = jnp.concatenate([-jnp.ones((half,), F32), jnp.ones((half,), F32), zeros]).reshape(1, LANE)
    return invf, sgn


def _even_fwd(x, pos, ln, gate, wt):
    h = _adaln(x, ln)
    p = _matmul(h, wt["w_in"], "nn", F32, name="ev_in")
    act, pre = _conv_fwd(p, EV_QKV, 1536, wt["conv_w"], jnp.zeros((1, 1536), F32), name="ev_conv")
    o_a, states = _gdn_fwd(act, p, wt["alog"], wt["dtb"], name="gdn_fwd")
    cqn, ckvn = _rows(_rms2_fn, [(p, EV_CQ, 384), (p, EV_CKV, 256)], [wt["gq"], wt["gkv"]],
                      [(384, BF16), (256, BF16)], name="mla_rms")
    q = _matmul(cqn, wt["w_uq"], "nn", F32, name="mla_uq")
    kv = _matmul(ckvn, wt["w_ukv"], "nn", F32, name="mla_ukv")
    invf, sgn = _rope_consts()
    qc, kc, vv = _rows(_rope_fn, [q, kv, (p, EV_MISC, LANE), pos], [invf, sgn],
                       [(1024, BF16), (1024, BF16), (512, BF16)], name="mla_rope", tm=ROW_TILE // 2)
    o_b, lse = _attn_fwd(qc, kc, vv, name="attn_fwd")
    (o,) = _rows(_ev_out_fn, [o_a, (p, EV_Z, 512), o_b], [wt["gdn_g"]], [(1024, BF16)], name="ev_out")
    y = _matmul(o, wt["w_out"], "nn", F32, name="ev_wout")
    return _resid(1.0, y, x, gate), (x, h, p, act, pre, states, cqn, ckvn, q, kv, qc, kc, vv, o_a, o_b, lse, o, y)


def _cat_fn(*parts):
    return (jnp.concatenate(parts, axis=-1),)


def _ev_dp_fn(dx0, dx1, dx2, dcq, dm_r, dm_g, dz, dckv):
    return (jnp.concatenate([dx0, dx1, dx2, dcq, dm_r + dm_g, dz, dckv], axis=-1),)


def _even_bwd(saved, dxn, pos, ln, gate, wt):
    x, h, p, act, pre, states, cqn, ckvn, q, kv, qc, kc, vv, o_a, o_b, lse, o, y = saved
    g = {}
    dy, g["gate"] = _resid_bwd(1.0, y, gate, dxn)
    do = _matmul(dy, wt["w_out"], "nt", F32, name="ev_dwout_x")
    g["w_out"] = _matmul(o, dy, "tn", F32, name="ev_dwout_w")
    (d_oa, dz, d_ob), (g["gdn_g"],) = _rows_vjp(_ev_out_fn, [o_a, (p, EV_Z, 512), o_b], [], [wt["gdn_g"]], [], [do],
                                                [F32, F32, F32], name="ev_out_bwd")
    dqc = _attn_bwd_q(qc, kc, vv, o_b, lse, d_ob, name="attn_bwd_q")
    dkc, dvv = _attn_bwd_kv(qc, kc, vv, o_b, lse, d_ob, name="attn_bwd_kv")
    invf, sgn = _rope_consts()
    (dq, dkv, dm_r), _ = _rows_vjp(_rope_fn, [q, kv, (p, EV_MISC, LANE)], [pos], [], [invf, sgn], [dqc, dkc, dvv],
                                   [BF16, BF16, F32], name="mla_rope_bwd", tm=ROW_TILE // 4)
    dcqn = _matmul(dq, wt["w_uq"], "nt", F32, name="mla_duq_x")
    g["w_uq"] = _matmul(cqn, dq, "tn", F32, name="mla_duq_w")
    dckvn = _matmul(dkv, wt["w_ukv"], "nt", F32, name="mla_dukv_x")
    g["w_ukv"] = _matmul(ckvn, dkv, "tn", F32, name="mla_dukv_w")
    (dcq, dckv), (g["gq"], g["gkv"]) = _rows_vjp(_rms2_fn, [(p, EV_CQ, 384), (p, EV_CKV, 256)], [],
                                                 [wt["gq"], wt["gkv"]], [], [dcqn, dckvn], [F32, F32], name="mla_rms_bwd")
    dq_g, dk_g, dv_g, dm_g, g["alog"], g["dtb"] = _gdn_bwd(act, p, wt["alog"], wt["dtb"], states, d_oa, name="gdn_bwd")
    dxs, dws = [], []
    for j, d in enumerate((dq_g, dk_g, dv_g)):
        dxj, dwj, _ = _conv_bwd(d, pre, 512 * j, p, EV_QKV + 512 * j, wt["conv_w"], name="ev_conv_bwd")
        dxs.append(dxj)
        dws.append(dwj)
    g["conv_w"] = jnp.concatenate(dws, axis=1)
    (dp,) = _rows(_ev_dp_fn, dxs + [dcq, dm_r, dm_g, dz, dckv], [],
                  [(EV_W, BF16)], name="ev_dp", tm=ROW_TILE // 2)
    dh = _matmul(dp, wt["w_in"], "nt", F32, name="ev_din_x")
    g["w_in"] = _matmul(h, dp, "tn", F32, name="ev_din_w")
    dx, g["ln"] = _adaln_bwd(x, ln, dh, dxn)
    return dx, g


def _odd_fwd(x, ln, gate, wt):
    h = _adaln(x, ln)
    p = _matmul(h, wt["w_in"], "nn", F32, name="od_in")
    act, pre = _conv_fwd(p, OD_XBC, 3072, wt["conv_w"], wt["conv_b"], name="od_conv")
    ys, states = _ssd_fwd(act, p, wt["alog"], wt["dtb"], wt["dsk"], name="ssd_fwd")
    (o,) = _rows(_od_out_fn, [ys, (p, OD_Z, 2048)], [wt["norm_g"]], [(SSD_D_INNER, BF16)], name="od_out",
                 tm=ROW_TILE // 2)
    y = _matmul(o, wt["w_out"], "nn", F32, name="od_wout")
    return _resid(1.0, y, x, gate), (x, h, p, act, pre, states, ys, o, y)


def _od_dp_fn(dz, dxx, dxb, dxc, ddt):
    return (jnp.concatenate([dz, dxx, dxb, dxc, ddt, jnp.zeros_like(ddt)], axis=-1),)


def _odd_bwd(saved, dxn, ln, gate, wt):
    x, h, p, act, pre, states, ys, o, y = saved
    g = {}
    dy, g["gate"] = _resid_bwd(1.0, y, gate, dxn)
    do = _matmul(dy, wt["w_out"], "nt", F32, name="od_dwout_x")
    g["w_out"] = _matmul(o, dy, "tn", F32, name="od_dwout_w")
    (dys, dz), (g["norm_g"],) = _rows_vjp(_od_out_fn, [ys, (p, OD_Z, 2048)], [], [wt["norm_g"]], [], [do], [F32, F32],
                                          name="od_out_bwd", tm=ROW_TILE // 4)
    dxs, dbm, dcm, ddt, g["alog"], g["dtb"], g["dsk"] = _ssd_bwd(act, p, wt["alog"], wt["dtb"], wt["dsk"], states, dys,
                                                                 name="ssd_bwd")
    dins, dws, dbs = [], [], []
    for d, c0 in ((dxs, 0), (dbm, 2048), (dcm, 2560)):
        dxj, dwj, dbj = _conv_bwd(d, pre, c0, p, OD_XBC + c0, wt["conv_w"], name="od_conv_bwd")
        dins.append(dxj)
        dws.append(dwj)
        dbs.append(dbj)
    g["conv_w"] = jnp.concatenate(dws, axis=1)
    g["conv_b"] = jnp.concatenate(dbs, axis=1)
    (dp,) = _rows(_od_dp_fn, [dz] + dins + [ddt], [], [(OD_W, BF16)], name="od_dp", tm=ROW_TILE // 4)
    dh = _matmul(dp, wt["w_in"], "nt", F32, name="od_din_x")
    g["w_in"] = _matmul(h, dp, "tn", F32, name="od_din_w")
    dx, g["ln"] = _adaln_bwd(x, ln, dh, dxn)
    return dx, g


def _local_step(x, tgt, pos, mod, final_g, layer_weights, layer_done):
    mod = mod.reshape(DEPTH, 3, 3, 1, D_MODEL)
    wts = []

    def ln_of(l, i):
        return (_row(wts[l]["norm_g"][i]), mod[l, i, 0], mod[l, i, 1])

    def mixer_w(l):
        p = wts[l]
        if l % 2 == 0:
            return dict(w_in=p["w_in_k"], conv_w=p["gdn_conv_w"].astype(F32),
                        alog=_lane_vec(p["gdn_A_log"]), dtb=_lane_vec(p["gdn_dt_bias"]),
                        gdn_g=_row(p["gdn_norm_g"]), gq=_row(p["mla_q_norm_g"]), gkv=_row(p["mla_kv_norm_g"]),
                        w_uq=_uq(p["mla_w_uq"]), w_ukv=_ukv(p["mla_w_ukv"]), w_out=p["ev_w_out"])
        return dict(w_in=p["w_in_k"], conv_w=p["ssd_conv_w"].astype(F32),
                    conv_b=_row(p["ssd_conv_b"]), alog=_lane_vec(p["ssd_A_log"]),
                    dtb=_lane_vec(p["ssd_dt_bias"]), dsk=_lane_vec(p["ssd_D"]),
                    norm_g=_row(p["ssd_norm_g"]), w_out=p["ssd_w_out"])

    saved = []
    for l in range(DEPTH):
        wts.append(layer_weights(l, x))
        x, s0 = _ffn_fwd(x, ln_of(l, 0), mod[l, 0, 2], wts[l]["w13"][0], wts[l]["w2"][0])
        if l % 2 == 0:
            x, s1 = _even_fwd(x, pos, ln_of(l, 1), mod[l, 1, 2], mixer_w(l))
        else:
            x, s1 = _odd_fwd(x, ln_of(l, 1), mod[l, 1, 2], mixer_w(l))
        x, s2 = _ffn_fwd(x, ln_of(l, 2), mod[l, 2, 2], wts[l]["w13"][1], wts[l]["w2"][1])
        saved.append((s0, s1, s2))

    loss, dx, d_final_g = _loss_bwd(x, tgt, _row(final_g), name="loss")

    repl = {k: [None] * (DEPTH // 2) for k in ("gdn_A_log", "gdn_dt_bias", "gdn_norm_g", "mla_q_norm_g", "mla_kv_norm_g",
                                                "ssd_A_log", "ssd_dt_bias", "ssd_D")}
    dmod = [None] * DEPTH
    token = None
    for l in reversed(range(DEPTH)):
        s0, s1, s2 = saved[l]
        e = l // 2
        gl = {"w13": [None] * 2, "w2": [None] * 2}
        dg, dsh, dsc, dgt = [None] * 3, [None] * 3, [None] * 3, [None] * 3
        gate2 = mod[l, 2, 2] if token is None else mod[l, 2, 2] + token
        dx, gl["w13"][1], gl["w2"][1], (dg[2], dsh[2], dsc[2]), dgt[2] = _ffn_bwd(
            s2, dx, ln_of(l, 2), gate2, wts[l]["w13"][1], wts[l]["w2"][1])
        if l % 2 == 0:
            dx, g = _even_bwd(s1, dx, pos, ln_of(l, 1), mod[l, 1, 2], mixer_w(l))
            gl.update(w_in_k=g["w_in"], gdn_conv_w=g["conv_w"], mla_w_uq=_uq_back(g["w_uq"]),
                      mla_w_ukv=_ukv_back(g["w_ukv"]), ev_w_out=g["w_out"])
            repl["gdn_A_log"][e] = g["alog"][0, :GDN_HEADS]
            repl["gdn_dt_bias"][e] = g["dtb"][0, :GDN_HEADS]
            repl["gdn_norm_g"][e] = g["gdn_g"][0]
            repl["mla_q_norm_g"][e] = g["gq"][0]
            repl["mla_kv_norm_g"][e] = g["gkv"][0]
        else:
            dx, g = _odd_bwd(s1, dx, ln_of(l, 1), mod[l, 1, 2], mixer_w(l))
            gl.update(w_in_k=g["w_in"], ssd_conv_w=g["conv_w"], ssd_conv_b=g["conv_b"][0], ssd_norm_g=g["norm_g"][0],
                      ssd_w_out=g["w_out"])
            repl["ssd_A_log"][e] = g["alog"][0, :SSD_HEADS]
            repl["ssd_dt_bias"][e] = g["dtb"][0, :SSD_HEADS]
            repl["ssd_D"][e] = g["dsk"][0, :SSD_HEADS]
        dg[1], dsh[1], dsc[1] = g["ln"]
        dgt[1] = g["gate"]
        dx, gl["w13"][0], gl["w2"][0], (dg[0], dsh[0], dsc[0]), dgt[0] = _ffn_bwd(
            s0, dx, ln_of(l, 0), mod[l, 0, 2], wts[l]["w13"][0], wts[l]["w2"][0])
        gl["norm_g"] = jnp.concatenate(dg, axis=0)
        dmod[l] = jnp.concatenate([jnp.concatenate([dsh[i], dsc[i], dgt[i]], axis=1) for i in range(3)], axis=1)[0]
        token = layer_done(l, gl, dx)

    grads = {k: jnp.stack(v) for k, v in repl.items()}
    grads["final_g"] = d_final_g[0]
    return loss, dx, grads, jnp.stack(dmod)


_WEIGHTS = ("ada_w", "ada_b", "norm_g", "ffn_w1", "ffn_w3", "ffn_w2", "ev_w_in", "gdn_conv_w", "gdn_A_log", "gdn_dt_bias",
            "gdn_norm_g", "mla_q_norm_g", "mla_w_uq", "mla_kv_norm_g", "mla_w_ukv", "ev_w_out", "ssd_w_in", "ssd_conv_w",
            "ssd_conv_b", "ssd_A_log", "ssd_dt_bias", "ssd_D", "ssd_norm_g", "ssd_w_out", "final_g")
_BIG = {"ffn_w1": 3, "ffn_w3": 3, "ffn_w2": 2, "ev_w_in": 2, "mla_w_uq": 1, "mla_w_ukv": 1, "ev_w_out": 1, "ssd_w_in": 2,
        "ssd_w_out": 1}
_SMALL = {"norm_g": 2, "gdn_conv_w": 2, "ssd_conv_w": 2, "ssd_conv_b": 1, "ssd_norm_g": 1}
_REPL = ("ada_b", "gdn_A_log", "gdn_dt_bias", "gdn_norm_g", "mla_q_norm_g", "mla_kv_norm_g", "ssd_A_log", "ssd_dt_bias",
         "ssd_D", "final_g")


def _join(pieces, axis):
    moved = jnp.moveaxis(pieces, 0, axis)
    shape = moved.shape
    return moved.reshape(shape[:axis] + (shape[axis] * shape[axis + 1],) + shape[axis + 2:])


def _split(full, axis):
    shape = full.shape
    return jnp.moveaxis(full.reshape(shape[:axis] + (N_DEV, shape[axis] // N_DEV) + shape[axis + 1:]), axis, 0)


def kernel(x, c, positions, ada_w, ada_b, norm_g, ffn_w1, ffn_w3, ffn_w2, ev_w_in, gdn_conv_w, gdn_A_log, gdn_dt_bias, gdn_norm_g, mla_q_norm_g, mla_w_uq, mla_kv_norm_g, mla_w_ukv, ev_w_out, ssd_w_in, ssd_conv_w, ssd_conv_b, ssd_A_log, ssd_dt_bias, ssd_D, ssd_norm_g, ssd_w_out, final_g, loss_target, m_ada_w, m_ada_b, m_norm_g, m_ffn_w1, m_ffn_w3, m_ffn_w2, m_ev_w_in, m_gdn_conv_w, m_gdn_A_log, m_gdn_dt_bias, m_gdn_norm_g, m_mla_q_norm_g, m_mla_w_uq, m_mla_kv_norm_g, m_mla_w_ukv, m_ev_w_out, m_ssd_w_in, m_ssd_conv_w, m_ssd_conv_b, m_ssd_A_log, m_ssd_dt_bias, m_ssd_D, m_ssd_norm_g, m_ssd_w_out, m_final_g, v_ada_w, v_ada_b, v_norm_g, v_ffn_w1, v_ffn_w3, v_ffn_w2, v_ev_w_in, v_gdn_conv_w, v_gdn_A_log, v_gdn_dt_bias, v_gdn_norm_g, v_mla_q_norm_g, v_mla_w_uq, v_mla_kv_norm_g, v_mla_w_ukv, v_ev_w_out, v_ssd_w_in, v_ssd_conv_w, v_ssd_conv_b, v_ssd_A_log, v_ssd_dt_bias, v_ssd_D, v_ssd_norm_g, v_ssd_w_out, v_final_g):
    a = dict(locals())
    w = {n: a[n] for n in _WEIGHTS}
    m = {n: a["m_" + n] for n in _WEIGHTS}
    v = {n: a["v_" + n] for n in _WEIGHTS}
    mx, my, mc = _mesh_pos()
    me = 4 * mx + 2 * my + mc
    t_len = x.shape[1]
    shards = range(N_DEV)

    small_names, big_names = list(_SMALL), list(_BIG)
    axis_of = {**_SMALL, **_BIG}
    small_g = _exchange([c] + [w[n] for n in small_names], False, name="gather_small")
    c_all = small_g[0].reshape(N_DEV, D_MODEL)
    fw = {n: _join(p, _SMALL[n]) for n, p in zip(small_names, small_g[1:])}
    first_names = [n for n in big_names if not n.startswith("ssd")]
    first_g = dict(zip(first_names, _exchange([w[n][:1].astype(BF16) for n in first_names], False, name="gather_first")))
    rest_src = {n: (w[n] if n.startswith("ssd") else w[n][1:]).astype(BF16) for n in big_names}
    rest_handle, rest_token = _exchange_start([rest_src[n] for n in big_names], False, name="gather_rest_start")
    rest_g = {}

    fs, es, os_ = ffn_w1.shape[3], ev_w_in.shape[2], ssd_w_in.shape[2]
    half = range(N_DEV // 2)
    plan13 = [[[(0, s, 0, fs) for s in half] + [(1, s, 0, fs) for s in half]
               + [(0, s + 4, 0, fs) for s in half] + [(1, s + 4, 0, fs) for s in half]]]
    plan_ev, k_at = [], 0
    for n0, n1, k0 in sorted(_EV_SEGS, key=lambda seg: seg[2]):
        if k0 > k_at:
            plan_ev.append(("z", k0 - k_at))
        plan_ev += _shard_pieces(0, n0, n1, es)
        k_at = k0 + n1 - n0
    assert k_at == EV_W and es * N_DEV == EV_NAT_W and os_ * N_DEV == OD_NAT_W
    plan_od = [[_shard_pieces(0, 0, OD_NAT_W, os_) + [("z", OD_W - OD_NAT_W)]]]

    def layer_weights(l, x_in):
        if l == 1:
            rest_g.update(zip(big_names, _exchange_wait(rest_handle, x_in, me, name="gather_rest_wait")))
        e = l // 2
        src = first_g if l == 0 else rest_g
        i = 0 if l == 0 else l - 1
        ie = 0 if (l == 0 or l % 2) else e - 1
        g1 = src["ffn_w1"].reshape(N_DEV, -1, fs)
        g3 = src["ffn_w3"].reshape(N_DEV, -1, fs)
        p = {"norm_g": fw["norm_g"][l],
             "w13": [_cols([(g1, (2 * i + j) * D_MODEL), (g3, (2 * i + j) * D_MODEL)], D_MODEL, plan13, BF16,
                           name="join_w13")[0][0] for j in range(2)],
             "w2": [src["ffn_w2"][:, i, j].reshape(D_FF, D_MODEL) for j in range(2)]}
        if l % 2 == 0:
            p["w_in_k"] = _cols([(src["ev_w_in"].reshape(N_DEV, -1, es), ie * D_MODEL)], D_MODEL, [[plan_ev]], BF16,
                                name="join_ev_in")[0][0]
            for n in ("mla_w_uq", "mla_w_ukv", "ev_w_out"):
                p[n] = _join(src[n][:, ie], axis_of[n] - 1)
            p["gdn_conv_w"] = fw["gdn_conv_w"][e]
            for n in ("gdn_A_log", "gdn_dt_bias", "gdn_norm_g", "mla_q_norm_g", "mla_kv_norm_g"):
                p[n] = w[n][e]
        else:
            p["w_in_k"] = _cols([(rest_g["ssd_w_in"].reshape(N_DEV, -1, os_), e * D_MODEL)], D_MODEL, plan_od, BF16,
                                name="join_od_in")[0][0]
            p["ssd_w_out"] = _join(rest_g["ssd_w_out"][:, e], axis_of["ssd_w_out"] - 1)
            for n in ("ssd_conv_w", "ssd_conv_b", "ssd_norm_g"):
                p[n] = fw[n][e]
            for n in ("ssd_A_log", "ssd_dt_bias", "ssd_D"):
                p[n] = w[n][e]
        return p

    def w13_cols(s, third):
        k0 = (s % 4) * fs + (2 * FF_HALF if s >= 4 else 0) + (FF_HALF if third else 0)
        return [(0, 0, k0, k0 + fs)]

    sent = {}

    def layer_done(l, gl, dx_l):
        d13 = [_cols([(gl["w13"][j][None], 0)], D_MODEL, [[w13_cols(s, False) for s in shards],
                                                          [w13_cols(s, True) for s in shards]], F32, name="split_w13")
               for j in range(2)]
        pieces = {"norm_g": _split(gl["norm_g"][None], 2),
                  "ffn_w1": jnp.stack([d13[j][0] for j in range(2)], axis=1)[:, None],
                  "ffn_w3": jnp.stack([d13[j][1] for j in range(2)], axis=1)[:, None],
                  "ffn_w2": jnp.stack([gl["w2"][j].reshape(N_DEV, -1, D_MODEL) for j in range(2)], axis=1)[:, None]}
        if l % 2 == 0:
            pieces["ev_w_in"] = _cols([(gl["w_in_k"][None], 0)], D_MODEL,
                                      [[_mapped_pieces(s * es, (s + 1) * es, _EV_SEGS) for s in shards]], F32,
                                      name="split_ev_in")[0][:, None]
            for n in ("gdn_conv_w", "mla_w_uq", "mla_w_ukv", "ev_w_out"):
                pieces[n] = _split(gl[n][None], axis_of[n])
        else:
            pieces["ssd_w_in"] = _cols([(gl["w_in_k"][None], 0)], D_MODEL,
                                       [[[(0, 0, s * os_, (s + 1) * os_)] for s in shards]], F32,
                                       name="split_od_in")[0][:, None]
            for n in ("ssd_conv_w", "ssd_conv_b", "ssd_norm_g", "ssd_w_out"):
                pieces[n] = _split(gl[n][None], axis_of[n])
        names = list(pieces)
        if l == 0:
            sent[l] = dict(zip(names, _exchange([pieces[n] for n in names], True, name="scatter_layer0")))
            return None
        handle, token = _exchange_start([pieces[n] for n in names], True, name=f"scatter_start_{l}")
        sent[l] = (names, handle)
        return token

    n_col = ada_w.shape[2]
    ada_b_shard = lax.dynamic_slice(ada_b, (0, me * n_col), (DEPTH, n_col)).reshape(DEPTH, 1, n_col)
    mod_all = _exchange([_mod_shard(c_all, ada_w, ada_b_shard, name="mod")], False, name="gather_mod")[0]
    mod_me = lax.dynamic_index_in_dim(mod_all, me, axis=2, keepdims=False)
    mod = jnp.transpose(mod_me, (1, 0, 2)).reshape(DEPTH, N_DEV * n_col) + rest_token

    pos = positions.astype(F32).reshape(t_len, 1)
    loss, dx, grads, dmod = _local_step(x[0], loss_target[0], pos, mod, final_g, layer_weights, layer_done)
    for l in range(1, DEPTH):
        names, handle = sent[l]
        sent[l] = dict(zip(names, _exchange_wait(handle, dx, me, name=f"scatter_wait_{l}")))

    repl_shapes = [w[n].shape for n in _REPL] + [(1,)]
    parts8 = _exchange([_pack([dmod] + [grads[n] for n in _REPL[1:]] + [loss[0, :1]])], False, name="gather_repl")[0]
    zero = jnp.zeros((1,), F32)
    r_grad, r_delta, r_m, r_v = [
        _unpack(o, repl_shapes) for o in _adamw(_pack([w[n] for n in _REPL] + [zero]), parts8,
                                                _pack([m[n] for n in _REPL] + [zero]),
                                                _pack([v[n] for n in _REPL] + [zero]), name="adamw_repl")]
    out = {"grad": {}, "delta": {}, "m": {}, "v": {}}
    for i, n in enumerate(_REPL):
        out["grad"][n], out["delta"][n], out["m"][n], out["v"][n] = r_grad[i], r_delta[i], r_m[i], r_v[i]
    loss_total = r_grad[-1].reshape(())

    dmod_all = parts8[:, :dmod.size // PACK_W].reshape((N_DEV,) + dmod.shape)
    dmod_cols = jnp.transpose(lax.dynamic_slice_in_dim(dmod_all, me * n_col, n_col, axis=2), (1, 0, 2))
    g_ada = _ada_w_grad(c_all, dmod_cols, name="ada_w_grad")
    for k, o in zip(("grad", "delta", "m", "v"), _adamw_nd(ada_w, g_ada[None], m["ada_w"], v["ada_w"], name="adamw_ada")):
        out[k]["ada_w"] = o

    for n in small_names + big_names:
        if n in ("norm_g", "ffn_w1", "ffn_w3", "ffn_w2"):
            layers = range(DEPTH)
        else:
            layers = range(1, DEPTH, 2) if n.startswith("ssd") else range(0, DEPTH, 2)
        g8 = jnp.concatenate([sent[l][n] for l in layers], axis=1)
        for k, o in zip(("grad", "delta", "m", "v"), _adamw_nd(w[n], g8, m[n], v[n], name="adamw_" + n)):
            out[k][n] = o

    return (loss_total, dx.reshape(x.shape), *[out["grad"][n] for n in _WEIGHTS], *[out["delta"][n] for n in _WEIGHTS],
            *[out["m"][n] for n in _WEIGHTS], *[out["v"][n] for n in _WEIGHTS])
```

```python
import functools
import math

import numpy as np
import jax
import jax.numpy as jnp
from jax import lax
from jax.experimental import pallas as pl
from jax.experimental.pallas import tpu as pltpu

F32 = jnp.float32
BF16 = jnp.bfloat16
HI = lax.Precision.HIGHEST

D_MODEL = 1024
DEPTH = 4
CHUNK = 64
NORM_EPS = 1e-6
CONV_K = 4
D_FF = 2816
GDN_HEADS = 4
GDN_DK = 128
MLA_HEADS = 4
MLA_NOPE = 128
MLA_ROPE = 64
ROPE_THETA = 10000.0
SSD_HEADS = 32
SSD_HEADDIM = 64
SSD_GROUPS = 4
SSD_STATE = 128
SSD_D_INNER = 2048
N_DEV = 8

ADAM_LR = 0.001
ADAM_B1 = 0.9
ADAM_B2 = 0.999
ADAM_EPS = 1e-08
ADAM_WD = 0.01
ADAM_STEP = 10

V7X_VMEM_LIMIT = 56 * 1024 * 1024
ROW_TILE = 512
SEQ_TILE = 128
ATT_TILE = 1024
MM_RESIDENT_BYTES = 12 * 1024 * 1024
FF_HALF = D_FF // 2
LANE = 128

EV_QKV, EV_CQ, EV_MISC, EV_Z, EV_CKV, EV_W = 0, 1536, 1920, 2048, 2560, 2816
OD_Z, OD_XBC, OD_DT, OD_W = 0, 2048, 5120, 5376


def _cparams(sem=None):
    return pltpu.CompilerParams(dimension_semantics=sem, vmem_limit_bytes=V7X_VMEM_LIMIT)


def _pick(n, cands):
    for c in cands:
        if n % c == 0:
            return c
    return n


_DN = {"nn": (((1,), (0,)), ((), ())), "nt": (((1,), (1,)), ((), ())), "tn": (((0,), (0,)), ((), ()))}


def _dot(a, b, mode, hi=False):
    if hi:
        return lax.dot_general(a.astype(F32), b.astype(F32), _DN[mode], precision=HI, preferred_element_type=F32)
    return lax.dot_general(a.astype(BF16), b.astype(BF16), _DN[mode], preferred_element_type=F32)


@functools.partial(jax.custom_vjp, nondiff_argnums=(2, 3))
def _mm(a, b, mode, hi):
    return _dot(a, b, mode, hi)


def _mm_fwd(a, b, mode, hi):
    return _dot(a, b, mode, hi), (a, b)


def _mm_bwd(mode, hi, res, g):
    a, b = res
    if mode == "nn":
        return _dot(g, b, "nt", hi), _dot(a, g, "tn", hi)
    if mode == "nt":
        return _dot(g, b, "nn", hi), _dot(g, a, "tn", hi)
    return _dot(b, g, "nt", hi), _dot(a, g, "nn", hi)


_mm.defvjp(_mm_fwd, _mm_bwd)


def _iota2(shape, dim):
    return lax.broadcasted_iota(jnp.int32, shape, dim)


def _row_spec(a, tm):
    if isinstance(a, tuple):
        arr, c0, w = a
        assert c0 % w == 0
        cb = c0 // w
        return arr, pl.BlockSpec((tm, w), lambda i, cb=cb: (i, cb))
    return a, pl.BlockSpec((tm, a.shape[1]), lambda i: (i, 0))


def _full_spec(b):
    return pl.BlockSpec(b.shape, lambda i: (0,) * b.ndim)


def _rows(fn, tiled, bcast, outs, *, name, tm=ROW_TILE):
    arrs, specs = zip(*[_row_spec(a, 0) for a in tiled])
    t_len = arrs[0].shape[0]
    tm = min(tm, t_len)
    arrs, specs = zip(*[_row_spec(a, tm) for a in tiled])
    nt, nb = len(tiled), len(bcast)

    def body(*refs):
        ins = [r[...].astype(F32) for r in refs[:nt]] + [r[...] for r in refs[nt:nt + nb]]
        res = fn(*ins)
        for r, v in zip(refs[nt + nb:], res):
            r[...] = v.astype(r.dtype)

    return pl.pallas_call(
        body, grid=(t_len // tm,), name=name,
        in_specs=list(specs) + [_full_spec(b) for b in bcast],
        out_specs=[pl.BlockSpec((tm, c), lambda i: (i, 0)) for c, _ in outs],
        out_shape=[jax.ShapeDtypeStruct((t_len, c), dt) for c, dt in outs],
        compiler_params=_cparams(("parallel",)),
    )(*arrs, *bcast)


def _rows_vjp(fn, tiled, consts, bcast, bconsts, douts, grads, *, name, adds=None, tm=ROW_TILE // 2):
    adds = adds or {}
    t_arrs, t_specs = zip(*[_row_spec(a, 0) for a in tiled])
    t_len = t_arrs[0].shape[0]
    tm = min(tm, t_len)
    rows_in = list(tiled) + list(consts) + list(douts) + [adds[k] for k in sorted(adds)]
    arrs, specs = zip(*[_row_spec(a, tm) for a in rows_in])
    nt, nc, nb, nbc, nd, na = len(tiled), len(consts), len(bcast), len(bconsts), len(douts), len(adds)
    add_pos = {k: j for j, k in enumerate(sorted(adds))}
    want = [j for j, g in enumerate(grads) if g is not None]

    def body(*refs):
        p = 0
        t = [r[...].astype(F32) for r in refs[p:p + nt]]; p += nt
        c = [r[...].astype(F32) for r in refs[p:p + nc]]; p += nc
        d = [r[...].astype(F32) for r in refs[p:p + nd]]; p += nd
        a = [r[...].astype(F32) for r in refs[p:p + na]]; p += na
        b = [r[...] for r in refs[p:p + nb]]; p += nb
        bc = [r[...] for r in refs[p:p + nbc]]; p += nbc
        g_refs = refs[p:p + len(want)]; p += len(want)
        gb_refs = refs[p:p + nb]

        def f(*args):
            return fn(*args[:nt], *c, *args[nt:], *bc)

        _, vjp = jax.vjp(f, *t, *b)
        g = vjp(tuple(d))
        for r, j in zip(g_refs, want):
            val = g[j]
            if j in add_pos:
                val = val + a[add_pos[j]]
            r[...] = val.astype(r.dtype)

        @pl.when(pl.program_id(0) == 0)
        def _():
            for r in gb_refs:
                r[...] = jnp.zeros_like(r)

        for r, val in zip(gb_refs, g[nt:]):
            r[...] += val

    def width(a):
        return a[2] if isinstance(a, tuple) else a.shape[1]

    res = pl.pallas_call(
        body, grid=(t_len // tm,), name=name,
        in_specs=list(specs) + [_full_spec(b) for b in list(bcast) + list(bconsts)],
        out_specs=[pl.BlockSpec((tm, width(tiled[j])), lambda i: (i, 0)) for j in want] + [_full_spec(b) for b in bcast],
        out_shape=[jax.ShapeDtypeStruct((t_len, width(tiled[j])), grads[j]) for j in want]
        + [jax.ShapeDtypeStruct(b.shape, F32) for b in bcast],
        compiler_params=_cparams(("arbitrary",)),
    )(*arrs, *bcast, *bconsts)
    tg = [None] * nt
    for r, j in zip(res[:len(want)], want):
        tg[j] = r
    return tg, list(res[len(want):])


def _matmul(a, b, mode, out_dtype, *, name):
    if mode == "tn":
        assert out_dtype == F32
        k_len, m_len = a.shape
        n_len = b.shape[1]
        tm, tn = m_len, n_len
        while tm * tn * 4 > MM_RESIDENT_BYTES and tn % (2 * LANE) == 0:
            tn //= 2
        tk = _pick(k_len, (512, 256, 128))
    else:
        m_len, k_len = a.shape
        n_len = b.shape[1] if mode == "nn" else b.shape[0]
        tk, tn = k_len, n_len
        while tk * tn * 2 > MM_RESIDENT_BYTES and tn % (2 * LANE) == 0:
            tn //= 2
        tm = _pick(m_len, (512, 256, 128))
        while tm * max(4 * tn, 2 * tk) > MM_RESIDENT_BYTES // 2 and tm % 256 == 0:
            tm //= 2
    nk = k_len // tk
    if mode == "nn":
        a_spec = pl.BlockSpec((tm, tk), lambda j, i, k: (i, k))
        b_spec = pl.BlockSpec((tk, tn), lambda j, i, k: (k, j))
    elif mode == "nt":
        a_spec = pl.BlockSpec((tm, tk), lambda j, i, k: (i, k))
        b_spec = pl.BlockSpec((tn, tk), lambda j, i, k: (j, k))
    else:
        a_spec = pl.BlockSpec((tk, tm), lambda j, i, k: (k, i))
        b_spec = pl.BlockSpec((tk, tn), lambda j, i, k: (k, j))

    def body(a_ref, b_ref, o_ref):
        part = _dot(a_ref[...], b_ref[...], mode)
        if nk == 1:
            o_ref[...] = part.astype(o_ref.dtype)
        else:
            @pl.when(pl.program_id(2) == 0)
            def _():
                o_ref[...] = jnp.zeros_like(o_ref)

            o_ref[...] += part

    return pl.pallas_call(
        body, grid=(n_len // tn, m_len // tm, nk), name=name,
        in_specs=[a_spec, b_spec],
        out_specs=pl.BlockSpec((tm, tn), lambda j, i, k: (i, j)),
        out_shape=jax.ShapeDtypeStruct((m_len, n_len), out_dtype),
        compiler_params=_cparams(("parallel", "parallel", "arbitrary")),
    )(a, b)


def _ffn_act(h, w13, *, name):
    t_len, d = h.shape
    tm = min(ROW_TILE, t_len)

    def body(h_ref, w_ref, s_ref):
        ab = _dot(h_ref[...], w_ref[...], "nn")
        a, b = ab[:, :FF_HALF], ab[:, FF_HALF:]
        s_ref[...] = (a * jax.nn.sigmoid(a) * b).astype(s_ref.dtype)

    return pl.pallas_call(
        body, grid=(2, t_len // tm), name=name,
        in_specs=[pl.BlockSpec((tm, d), lambda f, i: (i, 0)), pl.BlockSpec((d, 2 * FF_HALF), lambda f, i: (0, f))],
        out_specs=pl.BlockSpec((tm, FF_HALF), lambda f, i: (i, f)),
        out_shape=jax.ShapeDtypeStruct((t_len, D_FF), BF16),
        compiler_params=_cparams(("parallel", "parallel")),
    )(h, w13)


def _ffn_act_bwd(h, dy, w13, w2, *, name):
    t_len, d = h.shape
    tm = min(ROW_TILE, t_len)

    def body(h_ref, dy_ref, w_ref, w2_ref, o_ref):
        ab = _dot(h_ref[...], w_ref[...], "nn")
        a, b = ab[:, :FF_HALF], ab[:, FF_HALF:]
        ds = _dot(dy_ref[...], w2_ref[...], "nt")
        sig = jax.nn.sigmoid(a)
        silu = a * sig
        da = ds * b * (sig * (1.0 + a * (1.0 - sig)))
        db = ds * silu
        o_ref[...] = jnp.concatenate([da, db], axis=-1).astype(o_ref.dtype)

    return pl.pallas_call(
        body, grid=(2, t_len // tm), name=name,
        in_specs=[pl.BlockSpec((tm, d), lambda f, i: (i, 0)), pl.BlockSpec((tm, d), lambda f, i: (i, 0)),
                  pl.BlockSpec((d, 2 * FF_HALF), lambda f, i: (0, f)), pl.BlockSpec((FF_HALF, d), lambda f, i: (f, 0))],
        out_specs=pl.BlockSpec((tm, 2 * FF_HALF), lambda f, i: (i, f)),
        out_shape=jax.ShapeDtypeStruct((t_len, 2 * D_FF), BF16),
        compiler_params=_cparams(("parallel", "parallel")),
    )(h, dy, w13, w2)


CONV_CB = 512
HALO = 8


def _conv_fwd(p, c0, n_ch, w, b, *, name):
    t_len = p.shape[0]
    tm = min(ROW_TILE, t_len)
    hb = tm // HALO
    cb0 = c0 // CONV_CB

    def body(x_ref, halo_ref, w_ref, b_ref, act_ref, pre_ref):
        first = pl.program_id(1) == 0
        halo = jnp.where(first, 0.0, halo_ref[...])
        xx = jnp.concatenate([halo, x_ref[...]], axis=0)
        wv = w_ref[...]
        acc = b_ref[...] + wv[0:1] * xx[HALO - 3:HALO - 3 + tm]
        for j in range(1, CONV_K):
            acc = acc + wv[j:j + 1] * xx[HALO - 3 + j:HALO - 3 + j + tm]
        pre_ref[...] = acc
        act_ref[...] = acc * jax.nn.sigmoid(acc)

    return pl.pallas_call(
        body, grid=(n_ch // CONV_CB, t_len // tm), name=name,
        in_specs=[pl.BlockSpec((tm, CONV_CB), lambda j, i: (i, cb0 + j)),
                  pl.BlockSpec((HALO, CONV_CB), lambda j, i: (jnp.maximum(i * hb - 1, 0), cb0 + j)),
                  pl.BlockSpec((CONV_K, CONV_CB), lambda j, i: (0, j)),
                  pl.BlockSpec((1, CONV_CB), lambda j, i: (0, j))],
        out_specs=[pl.BlockSpec((tm, CONV_CB), lambda j, i: (i, j))] * 2,
        out_shape=[jax.ShapeDtypeStruct((t_len, n_ch), F32)] * 2,
        compiler_params=_cparams(("parallel", "arbitrary")),
    )(p, p, w, b)


def _conv_bwd(dact, pre, pre_c0, p, p_c0, w, *, name):
    t_len, n_ch = dact.shape
    tm = min(ROW_TILE, t_len)
    hb = tm // HALO
    nt = t_len // tm
    last_hb = t_len // HALO - 1
    cb0 = p_c0 // CONV_CB
    cbp = pre_c0 // CONV_CB

    def dsilu(z):
        sig = jax.nn.sigmoid(z)
        return sig * (1.0 + z * (1.0 - sig))

    def body(d_ref, dn_ref, pre_ref, pren_ref, x_ref, xh_ref, w_ref, dx_ref, dw_ref, db_ref):
        i = pl.program_id(1)
        dpre = d_ref[...] * dsilu(pre_ref[...])
        dnext = jnp.where(i == nt - 1, 0.0, dn_ref[...] * dsilu(pren_ref[...]))
        ext = jnp.concatenate([dpre, dnext], axis=0)
        xx = jnp.concatenate([jnp.where(i == 0, 0.0, xh_ref[...]), x_ref[...]], axis=0)
        wv = w_ref[...]
        dx = wv[0:1] * ext[3:3 + tm]
        for j in range(1, CONV_K):
            dx = dx + wv[j:j + 1] * ext[3 - j:3 - j + tm]
        dx_ref[...] = dx
        dws = [jnp.sum(dpre * xx[HALO - 3 + j:HALO - 3 + j + tm], axis=0, keepdims=True) for j in range(CONV_K)]

        @pl.when(i == 0)
        def _():
            dw_ref[...] = jnp.zeros_like(dw_ref)
            db_ref[...] = jnp.zeros_like(db_ref)

        dw_ref[...] += jnp.concatenate(dws, axis=0)
        db_ref[...] += jnp.sum(dpre, axis=0, keepdims=True)

    tile = lambda off: pl.BlockSpec((tm, CONV_CB), lambda j, i: (i, off + j))
    nxt = lambda off: pl.BlockSpec((HALO, CONV_CB), lambda j, i: (jnp.minimum((i + 1) * hb, last_hb), off + j))
    return pl.pallas_call(
        body, grid=(n_ch // CONV_CB, nt), name=name,
        in_specs=[tile(0), nxt(0), tile(cbp), nxt(cbp), tile(cb0),
                  pl.BlockSpec((HALO, CONV_CB), lambda j, i: (jnp.maximum(i * hb - 1, 0), cb0 + j)),
                  pl.BlockSpec((CONV_K, CONV_CB), lambda j, i: (0, cbp + j))],
        out_specs=[tile(0), pl.BlockSpec((CONV_K, CONV_CB), lambda j, i: (0, j)), pl.BlockSpec((1, CONV_CB), lambda j, i: (0, j))],
        out_shape=[jax.ShapeDtypeStruct((t_len, n_ch), F32), jax.ShapeDtypeStruct((CONV_K, n_ch), F32),
                   jax.ShapeDtypeStruct((1, n_ch), F32)],
        compiler_params=_cparams(("parallel", "arbitrary")),
    )(dact, dact, pre, pre, p, p, w)


@jax.custom_vjp
def _inv_unit_lower_many(a_cat):
    c = a_cat.shape[0]
    n = a_cat.shape[1] // c
    x = (_iota2(a_cat.shape, 0) == _iota2(a_cat.shape, 1) % c).astype(F32)
    for j in range(c - 1):
        col = jnp.concatenate([jnp.broadcast_to(a_cat[:, i * c + j:i * c + j + 1], (c, c)) for i in range(n)], axis=-1)
        x = x - col * x[j:j + 1, :]
    return x


def _inv_fwd(a_cat):
    x = _inv_unit_lower_many(a_cat)
    return x, x


def _inv_bwd(x, g):
    c = x.shape[0]
    parts = [-_dot(x[:, s], _dot(g[:, s], x[:, s], "nt", True), "tn", True)
             for s in (slice(i * c, (i + 1) * c) for i in range(x.shape[1] // c))]
    return (jnp.concatenate(parts, axis=-1),)


_inv_unit_lower_many.defvjp(_inv_fwd, _inv_bwd)


def _l2norm(x):
    return x * lax.rsqrt(jnp.sum(x * x, axis=-1, keepdims=True) + NORM_EPS)


def _rms(x):
    return x * lax.rsqrt(jnp.mean(x * x, axis=-1, keepdims=True) + NORM_EPS)


def _tri_masks(c):
    rows, cols = _iota2((c, c), 0), _iota2((c, c), 1)
    return rows >= cols, rows > cols, (rows >= cols).astype(F32), (rows <= cols).astype(F32)


def _gdn_tile(q, k, v, misc, s0, alog, dtb):
    c = CHUNK
    lower, strict, ltri, utri = _tri_masks(c)
    n_chunk = q.shape[0] // c
    pre = []
    for h in range(GDN_HEADS):
        hs = slice(h * LANE, (h + 1) * LANE)
        neg_a = -jnp.exp(alog[:, h:h + 1])
        for ci in range(n_chunk):
            sl = slice(ci * c, (ci + 1) * c)
            qn = _l2norm(q[sl, hs]) * (GDN_DK ** -0.5)
            kn = _l2norm(k[sl, hs])
            beta = jax.nn.sigmoid(misc[sl, 64 + h:65 + h])
            g = neg_a * jax.nn.softplus(misc[sl, 68 + h:69 + h] + dtb[:, h:h + 1])
            gb = jnp.broadcast_to(g, (c, c))
            gc_col = _mm(ltri, gb, "nn", True)
            gc_row = _mm(gb, utri, "tn", True)
            decay = jnp.where(lower, jnp.exp(jnp.where(lower, gc_col - gc_row, 0.0)), 0.0)
            kb = kn * beta
            a_mat = jnp.where(strict, _mm(kb, kn, "nt", False) * decay, 0.0)
            pre.append((qn, kn, kb, v[sl, hs] * beta, decay, gc_col[:, 0:1], gc_col[c - 1:c, 0:1], a_mat))
    t_all = _inv_unit_lower_many(jnp.concatenate([p[7] for p in pre], axis=-1))
    o_heads, s_heads = [], []
    for h in range(GDN_HEADS):
        s = s0[h * GDN_DK:(h + 1) * GDN_DK]
        outs = []
        for ci in range(n_chunk):
            i = h * n_chunk + ci
            qn, kn, kb, vb, decay, gc, g_last, _ = pre[i]
            t_inv = t_all[:, i * c:(i + 1) * c]
            u = _mm(t_inv, vb, "nn", True)
            w = _mm(t_inv, kb * jnp.exp(gc), "nn", True)
            attn = _mm(qn, kn, "nt", False) * decay
            k_end = kn * jnp.exp(g_last - gc)
            q_start = qn * jnp.exp(gc)
            v_new = u - _mm(w, s, "nn", False)
            outs.append(_mm(q_start, s, "nn", False) + _mm(attn, v_new, "nn", False))
            s = s * jnp.exp(g_last) + _mm(k_end, v_new, "tn", False)
        o_heads.append(jnp.concatenate(outs, axis=0))
        s_heads.append(s)
    return jnp.concatenate(o_heads, axis=-1), jnp.concatenate(s_heads, axis=0)


def _gdn_specs(tt, rev_n=None):
    t = (lambda i: i) if rev_n is None else (lambda i: rev_n - 1 - i)
    col = lambda j: pl.BlockSpec((tt, GDN_HEADS * LANE), lambda i: (t(i), j))
    vec = pl.BlockSpec((1, LANE), lambda i: (0, 0))
    misc = pl.BlockSpec((tt, LANE), lambda i: (t(i), EV_MISC // LANE))
    return [col(0), col(1), col(2), misc, vec, vec], t


def _gdn_fwd(act, p, alog, dtb, *, name):
    t_len = act.shape[0]
    tt = min(SEQ_TILE, t_len)
    ntile = t_len // tt
    in_specs, _ = _gdn_specs(tt)

    def body(q_ref, k_ref, v_ref, m_ref, al_ref, dt_ref, o_ref, s_ref, state):
        @pl.when(pl.program_id(0) == 0)
        def _():
            state[...] = jnp.zeros_like(state)

        s_ref[0] = state[...]
        o, s_new = _gdn_tile(q_ref[...], k_ref[...], v_ref[...], m_ref[...], state[...], al_ref[...], dt_ref[...])
        o_ref[...] = o
        state[...] = s_new

    return pl.pallas_call(
        body, grid=(ntile,), name=name, in_specs=in_specs,
        out_specs=[pl.BlockSpec((tt, GDN_HEADS * LANE), lambda i: (i, 0)),
                   pl.BlockSpec((1, GDN_HEADS * GDN_DK, LANE), lambda i: (i, 0, 0))],
        out_shape=[jax.ShapeDtypeStruct((t_len, GDN_HEADS * LANE), F32),
                   jax.ShapeDtypeStruct((ntile, GDN_HEADS * GDN_DK, LANE), F32)],
        scratch_shapes=[pltpu.VMEM((GDN_HEADS * GDN_DK, LANE), F32)],
        compiler_params=_cparams(("arbitrary",)),
    )(act, act, act, p, alog, dtb)


def _gdn_bwd(act, p, alog, dtb, states, do, *, name):
    t_len = act.shape[0]
    tt = min(SEQ_TILE, t_len)
    ntile = t_len // tt
    in_specs, t = _gdn_specs(tt, ntile)

    def body(q_ref, k_ref, v_ref, m_ref, al_ref, dt_ref, s0_ref, do_ref,
             dq_ref, dk_ref, dv_ref, dm_ref, dal_ref, ddt_ref, dstate):
        @pl.when(pl.program_id(0) == 0)
        def _():
            dstate[...] = jnp.zeros_like(dstate)
            dal_ref[...] = jnp.zeros_like(dal_ref)
            ddt_ref[...] = jnp.zeros_like(ddt_ref)

        _, vjp = jax.vjp(_gdn_tile, q_ref[...], k_ref[...], v_ref[...], m_ref[...], s0_ref[0], al_ref[...], dt_ref[...])
        dq, dk, dv, dm, ds0, dal, ddt = vjp((do_ref[...], dstate[...]))
        dq_ref[...] = dq
        dk_ref[...] = dk
        dv_ref[...] = dv
        dm_ref[...] = dm
        dstate[...] = ds0
        dal_ref[...] += dal
        ddt_ref[...] += ddt

    row = pl.BlockSpec((tt, GDN_HEADS * LANE), lambda i: (t(i), 0))
    vec = pl.BlockSpec((1, LANE), lambda i: (0, 0))
    return pl.pallas_call(
        body, grid=(ntile,), name=name,
        in_specs=in_specs + [pl.BlockSpec((1, GDN_HEADS * GDN_DK, LANE), lambda i: (t(i), 0, 0)), row],
        out_specs=[row, row, row, pl.BlockSpec((tt, LANE), lambda i: (t(i), 0)), vec, vec],
        out_shape=[jax.ShapeDtypeStruct((t_len, GDN_HEADS * LANE), F32)] * 3
        + [jax.ShapeDtypeStruct((t_len, LANE), F32)] + [jax.ShapeDtypeStruct((1, LANE), F32)] * 2,
        scratch_shapes=[pltpu.VMEM((GDN_HEADS * GDN_DK, LANE), F32)],
        compiler_params=_cparams(("arbitrary",)),
    )(act, act, act, p, alog, dtb, states, do)


def _head_expand():
    return jnp.asarray(np.repeat(np.eye(LANE, SSD_HEADS, dtype=np.float32), SSD_HEADDIM, axis=1))


@jax.custom_vjp
def _per_head(v, expand):
    half = _iota2((v.shape[0], LANE), 1) // SSD_HEADDIM
    tiles = [jnp.where(half == 0, v[:, 2 * j:2 * j + 1], v[:, 2 * j + 1:2 * j + 2]) for j in range(SSD_D_INNER // LANE)]
    return jnp.concatenate(tiles, axis=-1)


def _per_head_fwd(v, expand):
    return _per_head(v, expand), expand


def _per_head_bwd(expand, g):
    rows = g.shape[0]
    g8 = jnp.broadcast_to(g, (8, g.shape[1])) if rows == 1 else g
    dv = lax.dot_general(g8, expand, _DN["nt"], precision=lax.Precision.HIGH, preferred_element_type=F32)
    return dv[0:1] if rows == 1 else dv, None


_per_head.defvjp(_per_head_fwd, _per_head_bwd)


def _ssd_tile(xs, bm, cm, dtr, hs0, alog, dtb, dsk, expand):
    c = CHUNK
    gw = SSD_D_INNER // SSD_GROUPS
    hpg = SSD_HEADS // SSD_GROUPS
    lower, _, ltri, utri = _tri_masks(c)
    half = _iota2((c, LANE), 1) // SSD_HEADDIM
    dt = jax.nn.softplus(dtr + dtb)
    da = dt * (-jnp.exp(alog))
    xdt = xs * _per_head(dt, expand)
    d_x = _per_head(dsk, expand)
    hs = [hs0[g * SSD_STATE:(g + 1) * SSD_STATE] for g in range(SSD_GROUPS)]
    ys = []
    for ci in range(xs.shape[0] // c):
        sl = slice(ci * c, (ci + 1) * c)
        acs = _mm(ltri, da[sl], "nn", True)
        acs_t = _mm(da[sl], utri, "tn", True)
        acs_last = acs[c - 1:c, :]
        e_start = _per_head(jnp.exp(acs), expand)
        e_end = _per_head(jnp.exp(acs_last - acs), expand)
        e_dec = _per_head(jnp.exp(acs_last), expand)
        xdt_c = xdt[sl]
        y_tiles = [None] * (SSD_D_INNER // LANE)
        y_off = []
        for g in range(SSD_GROUPS):
            b_g = bm[sl, g * SSD_STATE:(g + 1) * SSD_STATE]
            c_g = cm[sl, g * SSD_STATE:(g + 1) * SSD_STATE]
            gs = slice(g * gw, (g + 1) * gw)
            cb = _mm(c_g, b_g, "nt", False)
            y_off.append(_mm(c_g, hs[g], "nn", False) * e_start[:, gs])
            for r in range(hpg):
                h = g * hpg + r
                j = h // 2
                lm = jnp.where(lower, jnp.exp(jnp.where(lower, acs[:, h:h + 1] - acs_t[h:h + 1, :], 0.0)), 0.0)
                xm = jnp.where(half == (h % 2), xdt_c[:, j * LANE:(j + 1) * LANE], 0.0)
                part = _mm(cb * lm, xm, "nn", False)
                y_tiles[j] = part if y_tiles[j] is None else y_tiles[j] + part
            hs[g] = hs[g] * e_dec[:, gs] + _mm(b_g, xdt_c[:, gs] * e_end[:, gs], "tn", False)
        ys.append(jnp.concatenate(y_tiles, axis=-1) + jnp.concatenate(y_off, axis=-1) + d_x * xs[sl])
    return jnp.concatenate(ys, axis=0), jnp.concatenate(hs, axis=0)


def _ssd_specs(tt, rev_n=None):
    t = (lambda i: i) if rev_n is None else (lambda i: rev_n - 1 - i)
    vec = pl.BlockSpec((1, LANE), lambda i: (0, 0))
    specs = [pl.BlockSpec((tt, SSD_D_INNER), lambda i: (t(i), 0)),
             pl.BlockSpec((tt, 512), lambda i: (t(i), SSD_D_INNER // 512)),
             pl.BlockSpec((tt, 512), lambda i: (t(i), SSD_D_INNER // 512 + 1)),
             pl.BlockSpec((tt, LANE), lambda i: (t(i), OD_DT // LANE)), vec, vec, vec,
             pl.BlockSpec((LANE, SSD_D_INNER), lambda i: (0, 0))]
    return specs, t


def _ssd_fwd(act, p, alog, dtb, dsk, *, name):
    t_len = act.shape[0]
    tt = min(SEQ_TILE, t_len)
    ntile = t_len // tt
    in_specs, _ = _ssd_specs(tt)

    def body(x_ref, b_ref, c_ref, dt_ref, al_ref, db_ref, dk_ref, e_ref, y_ref, s_ref, state):
        @pl.when(pl.program_id(0) == 0)
        def _():
            state[...] = jnp.zeros_like(state)

        s_ref[0] = state[...]
        y, hs = _ssd_tile(x_ref[...], b_ref[...], c_ref[...], dt_ref[...], state[...], al_ref[...], db_ref[...],
                          dk_ref[...], e_ref[...])
        y_ref[...] = y
        state[...] = hs

    return pl.pallas_call(
        body, grid=(ntile,), name=name, in_specs=in_specs,
        out_specs=[pl.BlockSpec((tt, SSD_D_INNER), lambda i: (i, 0)),
                   pl.BlockSpec((1, SSD_GROUPS * SSD_STATE, 512), lambda i: (i, 0, 0))],
        out_shape=[jax.ShapeDtypeStruct((t_len, SSD_D_INNER), F32),
                   jax.ShapeDtypeStruct((ntile, SSD_GROUPS * SSD_STATE, 512), F32)],
        scratch_shapes=[pltpu.VMEM((SSD_GROUPS * SSD_STATE, 512), F32)],
        compiler_params=_cparams(("arbitrary",)),
    )(act, act, act, p, alog, dtb, dsk, _head_expand())


def _ssd_bwd(act, p, alog, dtb, dsk, states, dy, *, name):
    t_len = act.shape[0]
    tt = min(SEQ_TILE, t_len)
    ntile = t_len // tt
    in_specs, t = _ssd_specs(tt, ntile)

    def body(x_ref, b_ref, c_ref, dt_ref, al_ref, db_ref, dk_ref, e_ref, s0_ref, dy_ref,
             dx_ref, dbm_ref, dcm_ref, ddt_ref, dal_ref, ddb_ref, ddk_ref, dstate):
        @pl.when(pl.program_id(0) == 0)
        def _():
            dstate[...] = jnp.zeros_like(dstate)
            dal_ref[...] = jnp.zeros_like(dal_ref)
            ddb_ref[...] = jnp.zeros_like(ddb_ref)
            ddk_ref[...] = jnp.zeros_like(ddk_ref)

        expand = e_ref[...]

        def f(xs, bm, cm, dtr, hs0, al, db, dk):
            return _ssd_tile(xs, bm, cm, dtr, hs0, al, db, dk, expand)

        _, vjp = jax.vjp(f, x_ref[...], b_ref[...], c_ref[...], dt_ref[...], s0_ref[0], al_ref[...], db_ref[...],
                         dk_ref[...])
        dx, dbm, dcm, ddt, dhs, dal, ddb, ddk = vjp((dy_ref[...], dstate[...]))
        dx_ref[...] = dx
        dbm_ref[...] = dbm
        dcm_ref[...] = dcm
        ddt_ref[...] = ddt
        dstate[...] = dhs
        dal_ref[...] += dal
        ddb_ref[...] += ddb
        ddk_ref[...] += ddk

    vec = pl.BlockSpec((1, LANE), lambda i: (0, 0))
    rows = lambda w: pl.BlockSpec((tt, w), lambda i: (t(i), 0))
    return pl.pallas_call(
        body, grid=(ntile,), name=name,
        in_specs=in_specs + [pl.BlockSpec((1, SSD_GROUPS * SSD_STATE, 512), lambda i: (t(i), 0, 0)), rows(SSD_D_INNER)],
        out_specs=[rows(SSD_D_INNER), rows(512), rows(512), rows(LANE), vec, vec, vec],
        out_shape=[jax.ShapeDtypeStruct((t_len, SSD_D_INNER), F32), jax.ShapeDtypeStruct((t_len, 512), F32),
                   jax.ShapeDtypeStruct((t_len, 512), F32), jax.ShapeDtypeStruct((t_len, LANE), F32)]
        + [jax.ShapeDtypeStruct((1, LANE), F32)] * 3,
        scratch_shapes=[pltpu.VMEM((SSD_GROUPS * SSD_STATE, 512), F32)],
        compiler_params=_cparams(("arbitrary",)),
    )(act, act, act, p, alog, dtb, dsk, _head_expand(), states, dy)


ATT_SCALE = (MLA_NOPE + MLA_ROPE) ** -0.5
ATT_SCALE2 = ATT_SCALE * math.log2(math.e)
QK_W = 2 * LANE


def _chunk_mask(tq):
    return (_iota2((tq, tq), 1) // CHUNK) <= (_iota2((tq, tq), 0) // CHUNK)


def _attn_fwd(qc, kc, vv, *, name):
    t_len = qc.shape[0]
    tq = min(ATT_TILE, t_len)
    nq = t_len // tq

    def body(q_ref, k_ref, v_ref, o_ref, lse_ref, m_s, l_s, acc_s):
        qi, ki = pl.program_id(1), pl.program_id(2)

        @pl.when(ki == 0)
        def _():
            m_s[...] = jnp.full_like(m_s, -jnp.inf)
            l_s[...] = jnp.zeros_like(l_s)
            acc_s[...] = jnp.zeros_like(acc_s)

        def step(masked):
            s = _dot(q_ref[...], k_ref[...], "nt") * ATT_SCALE2
            if masked:
                s = jnp.where(_chunk_mask(tq), s, -jnp.inf)
            m_new = jnp.maximum(m_s[...], jnp.max(s, axis=-1, keepdims=True))
            alpha = jnp.exp2(m_s[...] - m_new)
            p = jnp.exp2(s - m_new)
            l_s[...] = alpha * l_s[...] + jnp.sum(p, axis=-1, keepdims=True)
            acc_s[...] = alpha * acc_s[...] + _dot(p, v_ref[...], "nn")
            m_s[...] = m_new

        @pl.when(ki < qi)
        def _():
            step(False)

        @pl.when(ki == qi)
        def _():
            step(True)
            o_ref[...] = acc_s[...] / l_s[...]
            lse_ref[...] = jnp.broadcast_to(m_s[...] + jnp.log2(l_s[...]), lse_ref.shape)

    kv_idx = lambda h, i, k: (jnp.minimum(k, i), h)
    return pl.pallas_call(
        body, grid=(MLA_HEADS, nq, nq), name=name,
        in_specs=[pl.BlockSpec((tq, QK_W), lambda h, i, k: (i, h)), pl.BlockSpec((tq, QK_W), kv_idx),
                  pl.BlockSpec((tq, LANE), kv_idx)],
        out_specs=[pl.BlockSpec((tq, LANE), lambda h, i, k: (i, h))] * 2,
        out_shape=[jax.ShapeDtypeStruct((t_len, MLA_HEADS * LANE), F32)] * 2,
        scratch_shapes=[pltpu.VMEM((tq, 1), F32), pltpu.VMEM((tq, 1), F32), pltpu.VMEM((tq, LANE), F32)],
        compiler_params=_cparams(("parallel", "parallel", "arbitrary")),
    )(qc, kc, vv)


def _attn_probs(q, k, v, do, o, lse, masked, tq):
    s = _dot(q, k, "nt") * ATT_SCALE2
    if masked:
        s = jnp.where(_chunk_mask(tq), s, -jnp.inf)
    p = jnp.exp2(s - lse[:, 0:1])
    delta = jnp.sum(do * o, axis=-1, keepdims=True)
    ds = p * (_dot(do, v, "nt") - delta)
    return p, ds


def _attn_bwd_q(qc, kc, vv, o, lse, do, *, name):
    t_len = qc.shape[0]
    tq = min(ATT_TILE, t_len)
    nq = t_len // tq

    def body(q_ref, k_ref, v_ref, o_ref, lse_ref, do_ref, dq_ref, acc_s):
        qi, ki = pl.program_id(1), pl.program_id(2)

        @pl.when(ki == 0)
        def _():
            acc_s[...] = jnp.zeros_like(acc_s)

        def step(masked):
            _, ds = _attn_probs(q_ref[...], k_ref[...], v_ref[...], do_ref[...], o_ref[...], lse_ref[...], masked, tq)
            acc_s[...] += _dot(ds, k_ref[...], "nn")

        @pl.when(ki < qi)
        def _():
            step(False)

        @pl.when(ki == qi)
        def _():
            step(True)
            dq_ref[...] = acc_s[...] * ATT_SCALE

    kv_idx = lambda h, i, k: (jnp.minimum(k, i), h)
    q_idx = lambda h, i, k: (i, h)
    return pl.pallas_call(
        body, grid=(MLA_HEADS, nq, nq), name=name,
        in_specs=[pl.BlockSpec((tq, QK_W), q_idx), pl.BlockSpec((tq, QK_W), kv_idx), pl.BlockSpec((tq, LANE), kv_idx),
                  pl.BlockSpec((tq, LANE), q_idx), pl.BlockSpec((tq, LANE), q_idx), pl.BlockSpec((tq, LANE), q_idx)],
        out_specs=pl.BlockSpec((tq, QK_W), q_idx),
        out_shape=jax.ShapeDtypeStruct((t_len, MLA_HEADS * QK_W), F32),
        scratch_shapes=[pltpu.VMEM((tq, QK_W), F32)],
        compiler_params=_cparams(("parallel", "parallel", "arbitrary")),
    )(qc, kc, vv, o, lse, do)


def _attn_bwd_kv(qc, kc, vv, o, lse, do, *, name):
    t_len = qc.shape[0]
    tq = min(ATT_TILE, t_len)
    nq = t_len // tq

    def body(q_ref, k_ref, v_ref, o_ref, lse_ref, do_ref, dk_ref, dv_ref, dk_s, dv_s):
        ki, qi = pl.program_id(1), pl.program_id(2)

        @pl.when(qi == 0)
        def _():
            dk_s[...] = jnp.zeros_like(dk_s)
            dv_s[...] = jnp.zeros_like(dv_s)

        def step(masked):
            p, ds = _attn_probs(q_ref[...], k_ref[...], v_ref[...], do_ref[...], o_ref[...], lse_ref[...], masked, tq)
            dv_s[...] += _dot(p, do_ref[...], "tn")
            dk_s[...] += _dot(ds, q_ref[...], "tn")

        @pl.when(qi > ki)
        def _():
            step(False)

        @pl.when(qi == ki)
        def _():
            step(True)

        @pl.when(qi == nq - 1)
        def _():
            dk_ref[...] = dk_s[...] * ATT_SCALE
            dv_ref[...] = dv_s[...]

    q_idx = lambda h, k, i: (jnp.maximum(i, k), h)
    k_idx = lambda h, k, i: (k, h)
    return pl.pallas_call(
        body, grid=(MLA_HEADS, nq, nq), name=name,
        in_specs=[pl.BlockSpec((tq, QK_W), q_idx), pl.BlockSpec((tq, QK_W), k_idx), pl.BlockSpec((tq, LANE), k_idx),
                  pl.BlockSpec((tq, LANE), q_idx), pl.BlockSpec((tq, LANE), q_idx), pl.BlockSpec((tq, LANE), q_idx)],
        out_specs=[pl.BlockSpec((tq, QK_W), k_idx), pl.BlockSpec((tq, LANE), k_idx)],
        out_shape=[jax.ShapeDtypeStruct((t_len, MLA_HEADS * QK_W), F32), jax.ShapeDtypeStruct((t_len, MLA_HEADS * LANE), F32)],
        scratch_shapes=[pltpu.VMEM((tq, QK_W), F32), pltpu.VMEM((tq, LANE), F32)],
        compiler_params=_cparams(("parallel", "parallel", "arbitrary")),
    )(qc, kc, vv, o, lse, do)


def _adaln_fn(x, g, shift, scale):
    return ((_rms(x) * g) * (1.0 + scale) + shift,)


def _resid_fn(coef, y, x, gate):
    return (x + coef * gate * y,)


def _rms2_fn(cq, ckv, gq, gkv):
    return _rms(cq) * gq, _rms(ckv) * gkv


@jax.custom_vjp
def _swap_halves(x):
    return jnp.concatenate([x[:, 32:64], x[:, 0:32], x[:, 64:128]], axis=-1)


_swap_halves.defvjp(lambda x: (_swap_halves(x), None), lambda _, g: (_swap_halves(g),))


def _rope_fn(q, kv, misc, pos, invf, sgn):
    ang = pos * invf
    cos, sin = jnp.cos(ang), jnp.sin(ang) * sgn

    def rope(x):
        return x * cos + _swap_halves(x) * sin

    k_pe = rope(jnp.where(_iota2(misc.shape, 1) < MLA_ROPE, misc, 0.0))
    qs, ks = [], []
    for h in range(MLA_HEADS):
        qs += [q[:, h * LANE:(h + 1) * LANE], rope(q[:, (MLA_HEADS + h) * LANE:(MLA_HEADS + h + 1) * LANE])]
        ks += [kv[:, h * LANE:(h + 1) * LANE], k_pe]
    return jnp.concatenate(qs, axis=-1), jnp.concatenate(ks, axis=-1), kv[:, MLA_HEADS * LANE:]


def _ev_out_fn(oa, z, ob, g):
    parts = []
    for h in range(GDN_HEADS):
        hs = slice(h * LANE, (h + 1) * LANE)
        zz = z[:, hs]
        parts.append(_rms(oa[:, hs]) * g * (zz * jax.nn.sigmoid(zz)))
    return (jnp.concatenate(parts + [ob], axis=-1),)


def _od_out_fn(y, z, g):
    yz = y * (z * jax.nn.sigmoid(z))
    gw = SSD_D_INNER // SSD_GROUPS
    return (jnp.concatenate([_rms(yz[:, i * gw:(i + 1) * gw]) for i in range(SSD_GROUPS)], axis=-1) * g,)


def _loss_bwd(x, tgt, g, *, name):
    t_len, d = x.shape
    tm = min(ROW_TILE // 2, t_len)

    def body(x_ref, t_ref, g_ref, loss_ref, dx_ref, dg_ref):
        tgt_v = t_ref[...]

        def f(xv, gv):
            err = _rms(xv) * gv - tgt_v
            return 0.5 * jnp.sum(jnp.mean(err * err, axis=-1, keepdims=True), axis=0, keepdims=True)

        val, vjp = jax.vjp(f, x_ref[...], g_ref[...])
        dx, dg = vjp(jnp.ones((1, 1), F32))
        dx_ref[...] = dx

        @pl.when(pl.program_id(0) == 0)
        def _():
            loss_ref[...] = jnp.zeros_like(loss_ref)
            dg_ref[...] = jnp.zeros_like(dg_ref)

        loss_ref[...] += jnp.broadcast_to(val, loss_ref.shape)
        dg_ref[...] += dg

    row = pl.BlockSpec((tm, d), lambda i: (i, 0))
    return pl.pallas_call(
        body, grid=(t_len // tm,), name=name,
        in_specs=[row, row, pl.BlockSpec((1, d), lambda i: (0, 0))],
        out_specs=[pl.BlockSpec((1, LANE), lambda i: (0, 0)), row, pl.BlockSpec((1, d), lambda i: (0, 0))],
        out_shape=[jax.ShapeDtypeStruct((1, LANE), F32), jax.ShapeDtypeStruct((t_len, d), F32),
                   jax.ShapeDtypeStruct((1, d), F32)],
        compiler_params=_cparams(("arbitrary",)),
    )(x, tgt, g)


def _mesh_pos():
    return lax.axis_index("x"), lax.axis_index("y"), lax.axis_index("c")


def _exchange(xs, scatter, *, name):
    n_arr = len(xs)

    def body(*refs):
        in_refs, out_refs = refs[:n_arr], refs[n_arr:2 * n_arr]
        send_sems, recv_sems, local_sems = refs[2 * n_arr:]
        mx, my, mc = _mesh_pos()
        me = 4 * mx + 2 * my + mc
        started = []
        for a, (in_ref, out_ref) in enumerate(zip(in_refs, out_refs)):
            def src(j, in_ref=in_ref):
                return in_ref.at[j] if scatter else in_ref

            local = pltpu.make_async_copy(src(me), out_ref.at[me], local_sems.at[a])
            local.start()
            started.append((local, None))
            for d in range(1, N_DEV):
                px = 1 - mx if d & 4 else mx
                py = 1 - my if d & 2 else my
                pc = 1 - mc if d & 1 else mc
                peer = 4 * px + 2 * py + pc
                sem = a * (N_DEV - 1) + d - 1
                send = pltpu.make_async_remote_copy(
                    src_ref=src(peer), dst_ref=out_ref.at[me], send_sem=send_sems.at[sem], recv_sem=recv_sems.at[sem],
                    device_id=(px, py, pc), device_id_type=pl.DeviceIdType.MESH)
                send.start()
                recv = pltpu.make_async_remote_copy(
                    src_ref=src(peer), dst_ref=out_ref.at[peer], send_sem=send_sems.at[sem], recv_sem=recv_sems.at[sem],
                    device_id=(px, py, pc), device_id_type=pl.DeviceIdType.MESH)
                started.append((send, recv))
        for first, recv in started:
            if recv is None:
                first.wait()
            else:
                first.wait_send()
                recv.wait_recv()

    blocks = [tuple(x.shape[1:]) if scatter else tuple(x.shape) for x in xs]
    return pl.pallas_call(
        body, name=name,
        in_specs=[pl.BlockSpec(memory_space=pl.ANY)] * n_arr,
        out_specs=[pl.BlockSpec(memory_space=pl.ANY)] * n_arr,
        out_shape=[jax.ShapeDtypeStruct((N_DEV,) + b, x.dtype) for b, x in zip(blocks, xs)],
        scratch_shapes=[pltpu.SemaphoreType.DMA((n_arr * (N_DEV - 1),)), pltpu.SemaphoreType.DMA((n_arr * (N_DEV - 1),)),
                        pltpu.SemaphoreType.DMA((n_arr,))],
        compiler_params=pltpu.CompilerParams(has_side_effects=True),
    )(*xs)


def _peer_of(d, pos):
    mx, my, mc = pos
    px = 1 - mx if d & 4 else mx
    py = 1 - my if d & 2 else my
    pc = 1 - mc if d & 1 else mc
    return (px, py, pc), 4 * px + 2 * py + pc


_HBM = pl.BlockSpec(memory_space=pltpu.HBM)
_SEM = pl.BlockSpec(memory_space=pltpu.SEMAPHORE)


def _exchange_start(xs, scatter, *, name):
    n_arr = len(xs)
    n_sem = n_arr * (N_DEV - 1)

    def body(*refs):
        in_refs, land_refs = refs[:n_arr], refs[n_arr:2 * n_arr]
        send_sems, recv_sems, token = refs[2 * n_arr], refs[2 * n_arr + 1], refs[-1]
        pos = _mesh_pos()
        me = 4 * pos[0] + 2 * pos[1] + pos[2]
        for a in range(n_arr):
            for d in range(1, N_DEV):
                dev, peer = _peer_of(d, pos)
                sem = a * (N_DEV - 1) + d - 1
                pltpu.make_async_remote_copy(
                    src_ref=in_refs[a].at[peer] if scatter else in_refs[a], dst_ref=land_refs[a].at[me],
                    send_sem=send_sems.at[sem], recv_sem=recv_sems.at[sem], device_id=dev,
                    device_id_type=pl.DeviceIdType.MESH).start()
        token[...] = jnp.zeros_like(token)

    blocks = [tuple(x.shape[1:]) if scatter else tuple(x.shape) for x in xs]
    srcs = [pltpu.with_memory_space_constraint(x, pltpu.HBM) for x in xs]
    lands = [pltpu.with_memory_space_constraint(lax.empty((N_DEV,) + b, x.dtype), pltpu.HBM) for b, x in zip(blocks, xs)]
    res = pl.pallas_call(
        body, name=name,
        out_shape=(pltpu.SemaphoreType.DMA((n_sem,)), pltpu.SemaphoreType.DMA((n_sem,)),
                   *[pltpu.HBM(a.shape, a.dtype) for a in srcs + lands], jax.ShapeDtypeStruct((8, LANE), F32)),
        in_specs=[_HBM] * (2 * n_arr),
        out_specs=(_SEM, _SEM, *[_HBM] * (2 * n_arr), pl.BlockSpec(memory_space=pltpu.VMEM)),
        input_output_aliases={i: 2 + i for i in range(2 * n_arr)},
        compiler_params=pltpu.CompilerParams(has_side_effects=pltpu.SideEffectType.DATAFLOW_SIDE_EFFECTING),
    )(*srcs, *lands)
    handle = dict(sems=res[:2], srcs=res[2:2 + n_arr], lands=res[2 + n_arr:2 + 2 * n_arr], scatter=scatter)
    return handle, res[-1][0, 0]


def _exchange_wait(handle, after, me, *, name):
    scatter = handle["scatter"]
    n_arr = len(handle["srcs"])

    def body(*refs):
        in_refs, land_refs = refs[:n_arr], refs[n_arr:2 * n_arr]
        send_sems, recv_sems = refs[2 * n_arr], refs[2 * n_arr + 1]
        pos = _mesh_pos()
        for a in range(n_arr):
            for d in range(1, N_DEV):
                dev, peer = _peer_of(d, pos)
                sem = a * (N_DEV - 1) + d - 1
                copy = pltpu.make_async_remote_copy(
                    src_ref=in_refs[a].at[peer] if scatter else in_refs[a], dst_ref=land_refs[a].at[peer],
                    send_sem=send_sems.at[sem], recv_sem=recv_sems.at[sem], device_id=dev,
                    device_id_type=pl.DeviceIdType.MESH)
                copy.wait_send()
                copy.wait_recv()

    thru = list(handle["srcs"]) + list(handle["lands"])
    res = pl.pallas_call(
        body, name=name,
        out_shape=tuple(pltpu.HBM(a.shape, a.dtype) for a in thru),
        in_specs=[_HBM] * (2 * n_arr) + [_SEM, _SEM, pl.BlockSpec(memory_space=pl.ANY)],
        out_specs=tuple([_HBM] * (2 * n_arr)),
        input_output_aliases={i: i for i in range(2 * n_arr)},
        compiler_params=pltpu.CompilerParams(has_side_effects=pltpu.SideEffectType.DATAFLOW_SIDE_EFFECTING),
    )(*thru, *handle["sems"], after)
    out = []
    for src, land in zip(res[:n_arr], res[n_arr:]):
        own = lax.dynamic_index_in_dim(src, me, axis=0, keepdims=True) if scatter else src[None]
        out.append(lax.dynamic_update_index_in_dim(land, own, me, axis=0))
    return out


def _cols(srcs, rows, plans, out_dtype, *, name):
    n_src = len(srcs)
    rb = _pick(rows, (256, 128, 64, 32, 16, 8))

    def width(pieces):
        return sum(p[1] if p[0] == "z" else p[3] - p[2] for p in pieces)

    def body(*refs):
        ins, outs = refs[:n_src], refs[n_src:]
        loaded = {}
        for o_ref, plan in zip(outs, plans):
            for j, pieces in enumerate(plan):
                vals = []
                for pc in pieces:
                    if pc[0] == "z":
                        vals.append(jnp.zeros((rb, pc[1]), out_dtype))
                    else:
                        si, sj, c0, c1 = pc
                        if (si, sj) not in loaded:
                            loaded[(si, sj)] = ins[si][sj]
                        vals.append(loaded[(si, sj)][:, c0:c1].astype(out_dtype))
                o_ref[j] = vals[0] if len(vals) == 1 else jnp.concatenate(vals, axis=-1)

    for arr, r0 in srcs:
        assert r0 % rb == 0
    return pl.pallas_call(
        body, grid=(rows // rb,), name=name,
        in_specs=[pl.BlockSpec((arr.shape[0], rb, arr.shape[2]), lambda i, r0=r0 // rb: (0, r0 + i, 0)) for arr, r0 in srcs],
        out_specs=[pl.BlockSpec((len(p), rb, width(p[0])), lambda i: (0, i, 0)) for p in plans],
        out_shape=[jax.ShapeDtypeStruct((len(p), rows, width(p[0])), out_dtype) for p in plans],
        compiler_params=_cparams(("parallel",)),
    )(*[arr for arr, _ in srcs])


def _shard_pieces(src, a, b, shard_w):
    out = []
    while a < b:
        s = a // shard_w
        e = min(b, (s + 1) * shard_w)
        out.append((src, s, a - s * shard_w, e - s * shard_w))
        a = e
    return out


def _mapped_pieces(a, b, segs):
    out = []
    for n0, n1, k0 in sorted(segs):
        lo, hi = max(a, n0), min(b, n1)
        if lo < hi:
            out.append((0, 0, k0 + lo - n0, k0 + hi - n0))
    return out


_EV_SEGS = [(0, 1536, EV_QKV), (1536, 2048, EV_Z), (2048, 2056, EV_MISC + MLA_ROPE), (2056, 2440, EV_CQ),
            (2440, 2696, EV_CKV), (2696, 2760, EV_MISC)]
EV_NAT_W, OD_NAT_W = 2760, 5152


PACK_W = 1024


def _adamw(w, gparts, m, v, *, name):
    n_rows, n_cols = w.shape
    n_parts = gparts.shape[0]
    tm = _pick(n_rows, (512, 256, 128, 64, 32, 16, 8))
    while n_parts * tm * n_cols * 4 > 4 * 1024 * 1024 and tm % 16 == 0:
        tm //= 2

    def body(w_ref, g_ref, m_ref, v_ref, go_ref, d_ref, mo_ref, vo_ref):
        g = g_ref[0]
        for j in range(1, n_parts):
            g = g + g_ref[j]
        m_new = ADAM_B1 * m_ref[...] + (1.0 - ADAM_B1) * g
        v_new = ADAM_B2 * v_ref[...] + (1.0 - ADAM_B2) * jnp.square(g)
        m_hat = m_new / (1.0 - ADAM_B1 ** ADAM_STEP)
        v_hat = v_new / (1.0 - ADAM_B2 ** ADAM_STEP)
        go_ref[...] = g
        d_ref[...] = -ADAM_LR * (m_hat / (jnp.sqrt(v_hat) + ADAM_EPS) + ADAM_WD * w_ref[...])
        mo_ref[...] = m_new
        vo_ref[...] = v_new

    row = pl.BlockSpec((tm, n_cols), lambda i: (i, 0))
    return pl.pallas_call(
        body, grid=(n_rows // tm,), name=name,
        in_specs=[row, pl.BlockSpec((n_parts, tm, n_cols), lambda i: (0, i, 0)), row, row],
        out_specs=[row] * 4,
        out_shape=[jax.ShapeDtypeStruct((n_rows, n_cols), F32)] * 4,
        compiler_params=_cparams(("parallel",)),
    )(w, gparts, m, v)


def _adamw_nd(w, gparts, m, v, *, name):
    shape = w.shape
    two = (-1, shape[-1])
    outs = _adamw(w.reshape(two), gparts.reshape((gparts.shape[0],) + (int(np.prod(shape[:-1])), shape[-1])),
                  m.reshape(two), v.reshape(two), name=name)
    return [o.reshape(shape) for o in outs]


def _pack(parts):
    flat = [p.astype(F32).reshape(-1) for p in parts]
    n_pad = -sum(f.shape[0] for f in flat) % (8 * PACK_W)
    return jnp.concatenate(flat + [jnp.zeros((n_pad,), F32)]).reshape(-1, PACK_W)


def _unpack(packed, shapes):
    flat = packed.reshape(-1)
    out, off = [], 0
    for s in shapes:
        n = int(np.prod(s))
        out.append(flat[off:off + n].reshape(tuple(s)))
        off += n
    return out


def _mod_shard(c_all, ada_w, ada_b_shard, *, name):
    n_layer, d, n_col = ada_w.shape

    def body(c_ref, w_ref, b_ref, o_ref):
        cv = c_ref[...]
        o_ref[0] = _dot(cv * jax.nn.sigmoid(cv), w_ref[0], "nn") + b_ref[0]

    return pl.pallas_call(
        body, grid=(n_layer,), name=name,
        in_specs=[pl.BlockSpec((N_DEV, d), lambda l: (0, 0)), pl.BlockSpec((1, d, n_col), lambda l: (l, 0, 0)),
                  pl.BlockSpec((1, 1, n_col), lambda l: (l, 0, 0))],
        out_specs=pl.BlockSpec((1, N_DEV, n_col), lambda l: (l, 0, 0)),
        out_shape=jax.ShapeDtypeStruct((n_layer, N_DEV, n_col), F32),
        compiler_params=_cparams(("parallel",)),
    )(c_all, ada_w, ada_b_shard)


def _ada_w_grad(c_all, dmod_shard, *, name):
    n_layer, _, n_col = dmod_shard.shape
    d = c_all.shape[1]

    def body(c_ref, g_ref, o_ref):
        cv = c_ref[...]
        o_ref[0] = _dot(cv * jax.nn.sigmoid(cv), g_ref[0], "tn", True)

    return pl.pallas_call(
        body, grid=(n_layer,), name=name,
        in_specs=[pl.BlockSpec((N_DEV, d), lambda l: (0, 0)), pl.BlockSpec((1, N_DEV, n_col), lambda l: (l, 0, 0))],
        out_specs=pl.BlockSpec((1, d, n_col), lambda l: (l, 0, 0)),
        out_shape=jax.ShapeDtypeStruct((n_layer, d, n_col), F32),
        compiler_params=_cparams(("parallel",)),
    )(c_all, dmod_shard)


def _uq(w):
    r = w.shape[0]
    rope = jnp.pad(w[:, :, MLA_NOPE:], ((0, 0), (0, 0), (0, LANE - MLA_ROPE)))
    return jnp.concatenate([w[:, :, :MLA_NOPE].reshape(r, -1), rope.reshape(r, -1)], axis=1)


def _uq_back(d):
    r = d.shape[0]
    half = MLA_HEADS * LANE
    return jnp.concatenate([d[:, :half].reshape(r, MLA_HEADS, LANE),
                            d[:, half:].reshape(r, MLA_HEADS, LANE)[:, :, :MLA_ROPE]], axis=-1)


def _ukv(w):
    r = w.shape[0]
    return jnp.concatenate([w[:, :, :MLA_NOPE].reshape(r, -1), w[:, :, MLA_NOPE:].reshape(r, -1)], axis=1)


def _ukv_back(d):
    r = d.shape[0]
    half = MLA_HEADS * LANE
    return jnp.concatenate([d[:, :half].reshape(r, MLA_HEADS, LANE), d[:, half:].reshape(r, MLA_HEADS, LANE)], axis=-1)


def _lane_vec(v):
    return jnp.pad(v.astype(F32), (0, LANE - v.shape[0])).reshape(1, LANE)


def _row(v):
    return v.astype(F32).reshape(1, -1)


def _adaln(x, ln):
    return _rows(_adaln_fn, [x], list(ln), [(D_MODEL, BF16)], name="adaln")[0]


def _adaln_bwd(x, ln, dh, dxn):
    (dx,), dln = _rows_vjp(_adaln_fn, [x], [], list(ln), [], [dh], [F32], adds={0: dxn}, name="adaln_bwd")
    return dx, dln


def _resid(coef, y, x, gate):
    return _rows(functools.partial(_resid_fn, coef), [y, x], [gate], [(D_MODEL, F32)], name="resid")[0]


def _gated_fn(coef, y, gate):
    return (coef * gate * y,)


def _resid_bwd(coef, y, gate, dxn):
    (dy,), (dgate,) = _rows_vjp(functools.partial(_gated_fn, coef), [y], [], [gate], [], [dxn], [BF16], name="resid_bwd")
    return dy, dgate


def _ffn_fwd(x, ln, gate, w13, w2):
    h = _adaln(x, ln)
    s = _ffn_act(h, w13, name="ffn_act")
    y = _matmul(s, w2, "nn", F32, name="ffn_down")
    return _resid(0.5, y, x, gate), (x, h, s, y)


def _ffn_bwd(saved, dxn, ln, gate, w13, w2):
    x, h, s, y = saved
    dy, dgate = _resid_bwd(0.5, y, gate, dxn)
    dab = _ffn_act_bwd(h, dy, w13, w2, name="ffn_act_bwd")
    dh = _matmul(dab, w13, "nt", F32, name="ffn_dh")
    dw13 = _matmul(h, dab, "tn", F32, name="ffn_dw13")
    dw2 = _matmul(s, dy, "tn", F32, name="ffn_dw2")
    dx, dln = _adaln_bwd(x, ln, dh, dxn)
    return dx, dw13, dw2, dln, dgate


def _rope_consts():
    half = MLA_ROPE // 2
    inv = (ROPE_THETA ** (-jnp.arange(half, dtype=F32) / half)).astype(F32)
    zeros = jnp.zeros((LANE - MLA_ROPE,), F32)
    invf = jnp.concatenate([inv, inv, zeros]).reshape(1, LANE)
    sgn = jnp.concatenate([-jnp.ones((half,), F32), jnp.ones((half,), F32), zeros]).reshape(1, LANE)
    return invf, sgn


def _even_fwd(x, pos, ln, gate, wt):
    h = _adaln(x, ln)
    p = _matmul(h, wt["w_in"], "nn", F32, name="ev_in")
    act, pre = _conv_fwd(p, EV_QKV, 1536, wt["conv_w"], jnp.zeros((1, 1536), F32), name="ev_conv")
    o_a, states = _gdn_fwd(act, p, wt["alog"], wt["dtb"], name="gdn_fwd")
    cqn, ckvn = _rows(_rms2_fn, [(p, EV_CQ, 384), (p, EV_CKV, 256)], [wt["gq"], wt["gkv"]],
                      [(384, BF16), (256, BF16)], name="mla_rms")
    q = _matmul(cqn, wt["w_uq"], "nn", F32, name="mla_uq")
    kv = _matmul(ckvn, wt["w_ukv"], "nn", F32, name="mla_ukv")
    invf, sgn = _rope_consts()
    qc, kc, vv = _rows(_rope_fn, [q, kv, (p, EV_MISC, LANE), pos], [invf, sgn],
                       [(1024, BF16), (1024, BF16), (512, BF16)], name="mla_rope", tm=ROW_TILE // 2)
    o_b, lse = _attn_fwd(qc, kc, vv, name="attn_fwd")
    (o,) = _rows(_ev_out_fn, [o_a, (p, EV_Z, 512), o_b], [wt["gdn_g"]], [(1024, BF16)], name="ev_out")
    y = _matmul(o, wt["w_out"], "nn", F32, name="ev_wout")
    return _resid(1.0, y, x, gate), (x, h, p, act, pre, states, cqn, ckvn, q, kv, qc, kc, vv, o_a, o_b, lse, o, y)


def _cat_fn(*parts):
    return (jnp.concatenate(parts, axis=-1),)


def _ev_dp_fn(dx0, dx1, dx2, dcq, dm_r, dm_g, dz, dckv):
    return (jnp.concatenate([dx0, dx1, dx2, dcq, dm_r + dm_g, dz, dckv], axis=-1),)


def _even_bwd(saved, dxn, pos, ln, gate, wt):
    x, h, p, act, pre, states, cqn, ckvn, q, kv, qc, kc, vv, o_a, o_b, lse, o, y = saved
    g = {}
    dy, g["gate"] = _resid_bwd(1.0, y, gate, dxn)
    do = _matmul(dy, wt["w_out"], "nt", F32, name="ev_dwout_x")
    g["w_out"] = _matmul(o, dy, "tn", F32, name="ev_dwout_w")
    (d_oa, dz, d_ob), (g["gdn_g"],) = _rows_vjp(_ev_out_fn, [o_a, (p, EV_Z, 512), o_b], [], [wt["gdn_g"]], [], [do],
                                                [F32, F32, F32], name="ev_out_bwd")
    dqc = _attn_bwd_q(qc, kc, vv, o_b, lse, d_ob, name="attn_bwd_q")
    dkc, dvv = _attn_bwd_kv(qc, kc, vv, o_b, lse, d_ob, name="attn_bwd_kv")
    invf, sgn = _rope_consts()
    (dq, dkv, dm_r), _ = _rows_vjp(_rope_fn, [q, kv, (p, EV_MISC, LANE)], [pos], [], [invf, sgn], [dqc, dkc, dvv],
                                   [BF16, BF16, F32], name="mla_rope_bwd", tm=ROW_TILE // 4)
    dcqn = _matmul(dq, wt["w_uq"], "nt", F32, name="mla_duq_x")
    g["w_uq"] = _matmul(cqn, dq, "tn", F32, name="mla_duq_w")
    dckvn = _matmul(dkv, wt["w_ukv"], "nt", F32, name="mla_dukv_x")
    g["w_ukv"] = _matmul(ckvn, dkv, "tn", F32, name="mla_dukv_w")
    (dcq, dckv), (g["gq"], g["gkv"]) = _rows_vjp(_rms2_fn, [(p, EV_CQ, 384), (p, EV_CKV, 256)], [],
                                                 [wt["gq"], wt["gkv"]], [], [dcqn, dckvn], [F32, F32], name="mla_rms_bwd")
    dq_g, dk_g, dv_g, dm_g, g["alog"], g["dtb"] = _gdn_bwd(act, p, wt["alog"], wt["dtb"], states, d_oa, name="gdn_bwd")
    dxs, dws = [], []
    for j, d in enumerate((dq_g, dk_g, dv_g)):
        dxj, dwj, _ = _conv_bwd(d, pre, 512 * j, p, EV_QKV + 512 * j, wt["conv_w"], name="ev_conv_bwd")
        dxs.append(dxj)
        dws.append(dwj)
    g["conv_w"] = jnp.concatenate(dws, axis=1)
    (dp,) = _rows(_ev_dp_fn, dxs + [dcq, dm_r, dm_g, dz, dckv], [],
                  [(EV_W, BF16)], name="ev_dp", tm=ROW_TILE // 2)
    dh = _matmul(dp, wt["w_in"], "nt", F32, name="ev_din_x")
    g["w_in"] = _matmul(h, dp, "tn", F32, name="ev_din_w")
    dx, g["ln"] = _adaln_bwd(x, ln, dh, dxn)
    return dx, g


def _odd_fwd(x, ln, gate, wt):
    h = _adaln(x, ln)
    p = _matmul(h, wt["w_in"], "nn", F32, name="od_in")
    act, pre = _conv_fwd(p, OD_XBC, 3072, wt["conv_w"], wt["conv_b"], name="od_conv")
    ys, states = _ssd_fwd(act, p, wt["alog"], wt["dtb"], wt["dsk"], name="ssd_fwd")
    (o,) = _rows(_od_out_fn, [ys, (p, OD_Z, 2048)], [wt["norm_g"]], [(SSD_D_INNER, BF16)], name="od_out",
                 tm=ROW_TILE // 2)
    y = _matmul(o, wt["w_out"], "nn", F32, name="od_wout")
    return _resid(1.0, y, x, gate), (x, h, p, act, pre, states, ys, o, y)


def _od_dp_fn(dz, dxx, dxb, dxc, ddt):
    return (jnp.concatenate([dz, dxx, dxb, dxc, ddt, jnp.zeros_like(ddt)], axis=-1),)


def _odd_bwd(saved, dxn, ln, gate, wt):
    x, h, p, act, pre, states, ys, o, y = saved
    g = {}
    dy, g["gate"] = _resid_bwd(1.0, y, gate, dxn)
    do = _matmul(dy, wt["w_out"], "nt", F32, name="od_dwout_x")
    g["w_out"] = _matmul(o, dy, "tn", F32, name="od_dwout_w")
    (dys, dz), (g["norm_g"],) = _rows_vjp(_od_out_fn, [ys, (p, OD_Z, 2048)], [], [wt["norm_g"]], [], [do], [F32, F32],
                                          name="od_out_bwd", tm=ROW_TILE // 4)
    dxs, dbm, dcm, ddt, g["alog"], g["dtb"], g["dsk"] = _ssd_bwd(act, p, wt["alog"], wt["dtb"], wt["dsk"], states, dys,
                                                                 name="ssd_bwd")
    dins, dws, dbs = [], [], []
    for d, c0 in ((dxs, 0), (dbm, 2048), (dcm, 2560)):
        dxj, dwj, dbj = _conv_bwd(d, pre, c0, p, OD_XBC + c0, wt["conv_w"], name="od_conv_bwd")
        dins.append(dxj)
        dws.append(dwj)
        dbs.append(dbj)
    g["conv_w"] = jnp.concatenate(dws, axis=1)
    g["conv_b"] = jnp.concatenate(dbs, axis=1)
    (dp,) = _rows(_od_dp_fn, [dz] + dins + [ddt], [], [(OD_W, BF16)], name="od_dp", tm=ROW_TILE // 4)
    dh = _matmul(dp, wt["w_in"], "nt", F32, name="od_din_x")
    g["w_in"] = _matmul(h, dp, "tn", F32, name="od_din_w")
    dx, g["ln"] = _adaln_bwd(x, ln, dh, dxn)
    return dx, g


def _local_step(x, tgt, pos, mod, final_g, layer_weights, layer_done):
    mod = mod.reshape(DEPTH, 3, 3, 1, D_MODEL)
    wts = []

    def ln_of(l, i):
        return (_row(wts[l]["norm_g"][i]), mod[l, i, 0], mod[l, i, 1])

    def mixer_w(l):
        p = wts[l]
        if l % 2 == 0:
            return dict(w_in=p["w_in_k"], conv_w=p["gdn_conv_w"].astype(F32),
                        alog=_lane_vec(p["gdn_A_log"]), dtb=_lane_vec(p["gdn_dt_bias"]),
                        gdn_g=_row(p["gdn_norm_g"]), gq=_row(p["mla_q_norm_g"]), gkv=_row(p["mla_kv_norm_g"]),
                        w_uq=_uq(p["mla_w_uq"]), w_ukv=_ukv(p["mla_w_ukv"]), w_out=p["ev_w_out"])
        return dict(w_in=p["w_in_k"], conv_w=p["ssd_conv_w"].astype(F32),
                    conv_b=_row(p["ssd_conv_b"]), alog=_lane_vec(p["ssd_A_log"]),
                    dtb=_lane_vec(p["ssd_dt_bias"]), dsk=_lane_vec(p["ssd_D"]),
                    norm_g=_row(p["ssd_norm_g"]), w_out=p["ssd_w_out"])

    saved = []
    for l in range(DEPTH):
        wts.append(layer_weights(l, x))
        x, s0 = _ffn_fwd(x, ln_of(l, 0), mod[l, 0, 2], wts[l]["w13"][0], wts[l]["w2"][0])
        if l % 2 == 0:
            x, s1 = _even_fwd(x, pos, ln_of(l, 1), mod[l, 1, 2], mixer_w(l))
        else:
            x, s1 = _odd_fwd(x, ln_of(l, 1), mod[l, 1, 2], mixer_w(l))
        x, s2 = _ffn_fwd(x, ln_of(l, 2), mod[l, 2, 2], wts[l]["w13"][1], wts[l]["w2"][1])
        saved.append((s0, s1, s2))

    loss, dx, d_final_g = _loss_bwd(x, tgt, _row(final_g), name="loss")

    repl = {k: [None] * (DEPTH // 2) for k in ("gdn_A_log", "gdn_dt_bias", "gdn_norm_g", "mla_q_norm_g", "mla_kv_norm_g",
                                                "ssd_A_log", "ssd_dt_bias", "ssd_D")}
    dmod = [None] * DEPTH
    token = None
    for l in reversed(range(DEPTH)):
        s0, s1, s2 = saved[l]
        e = l // 2
        gl = {"w13": [None] * 2, "w2": [None] * 2}
        dg, dsh, dsc, dgt = [None] * 3, [None] * 3, [None] * 3, [None] * 3
        gate2 = mod[l, 2, 2] if token is None else mod[l, 2, 2] + token
        dx, gl["w13"][1], gl["w2"][1], (dg[2], dsh[2], dsc[2]), dgt[2] = _ffn_bwd(
            s2, dx, ln_of(l, 2), gate2, wts[l]["w13"][1], wts[l]["w2"][1])
        if l % 2 == 0:
            dx, g = _even_bwd(s1, dx, pos, ln_of(l, 1), mod[l, 1, 2], mixer_w(l))
            gl.update(w_in_k=g["w_in"], gdn_conv_w=g["conv_w"], mla_w_uq=_uq_back(g["w_uq"]),
                      mla_w_ukv=_ukv_back(g["w_ukv"]), ev_w_out=g["w_out"])
            repl["gdn_A_log"][e] = g["alog"][0, :GDN_HEADS]
            repl["gdn_dt_bias"][e] = g["dtb"][0, :GDN_HEADS]
            repl["gdn_norm_g"][e] = g["gdn_g"][0]
            repl["mla_q_norm_g"][e] = g["gq"][0]
            repl["mla_kv_norm_g"][e] = g["gkv"][0]
        else:
            dx, g = _odd_bwd(s1, dx, ln_of(l, 1), mod[l, 1, 2], mixer_w(l))
            gl.update(w_in_k=g["w_in"], ssd_conv_w=g["conv_w"], ssd_conv_b=g["conv_b"][0], ssd_norm_g=g["norm_g"][0],
                      ssd_w_out=g["w_out"])
            repl["ssd_A_log"][e] = g["alog"][0, :SSD_HEADS]
            repl["ssd_dt_bias"][e] = g["dtb"][0, :SSD_HEADS]
            repl["ssd_D"][e] = g["dsk"][0, :SSD_HEADS]
        dg[1], dsh[1], dsc[1] = g["ln"]
        dgt[1] = g["gate"]
        dx, gl["w13"][0], gl["w2"][0], (dg[0], dsh[0], dsc[0]), dgt[0] = _ffn_bwd(
            s0, dx, ln_of(l, 0), mod[l, 0, 2], wts[l]["w13"][0], wts[l]["w2"][0])
        gl["norm_g"] = jnp.concatenate(dg, axis=0)
        dmod[l] = jnp.concatenate([jnp.concatenate([dsh[i], dsc[i], dgt[i]], axis=1) for i in range(3)], axis=1)[0]
        token = layer_done(l, gl, dx)

    grads = {k: jnp.stack(v) for k, v in repl.items()}
    grads["final_g"] = d_final_g[0]
    return loss, dx, grads, jnp.stack(dmod)


_WEIGHTS = ("ada_w", "ada_b", "norm_g", "ffn_w1", "ffn_w3", "ffn_w2", "ev_w_in", "gdn_conv_w", "gdn_A_log", "gdn_dt_bias",
            "gdn_norm_g", "mla_q_norm_g", "mla_w_uq", "mla_kv_norm_g", "mla_w_ukv", "ev_w_out", "ssd_w_in", "ssd_conv_w",
            "ssd_conv_b", "ssd_A_log", "ssd_dt_bias", "ssd_D", "ssd_norm_g", "ssd_w_out", "final_g")
_BIG = {"ffn_w1": 3, "ffn_w3": 3, "ffn_w2": 2, "ev_w_in": 2, "mla_w_uq": 1, "mla_w_ukv": 1, "ev_w_out": 1, "ssd_w_in": 2,
        "ssd_w_out": 1}
_SMALL = {"norm_g": 2, "gdn_conv_w": 2, "ssd_conv_w": 2, "ssd_conv_b": 1, "ssd_norm_g": 1}
_REPL = ("ada_b", "gdn_A_log", "gdn_dt_bias", "gdn_norm_g", "mla_q_norm_g", "mla_kv_norm_g", "ssd_A_log", "ssd_dt_bias",
         "ssd_D", "final_g")


def _join(pieces, axis):
    moved = jnp.moveaxis(pieces, 0, axis)
    shape = moved.shape
    return moved.reshape(shape[:axis] + (shape[axis] * shape[axis + 1],) + shape[axis + 2:])


def _split(full, axis):
    shape = full.shape
    return jnp.moveaxis(full.reshape(shape[:axis] + (N_DEV, shape[axis] // N_DEV) + shape[axis + 1:]), axis, 0)


def kernel(x, c, positions, ada_w, ada_b, norm_g, ffn_w1, ffn_w3, ffn_w2, ev_w_in, gdn_conv_w, gdn_A_log, gdn_dt_bias, gdn_norm_g, mla_q_norm_g, mla_w_uq, mla_kv_norm_g, mla_w_ukv, ev_w_out, ssd_w_in, ssd_conv_w, ssd_conv_b, ssd_A_log, ssd_dt_bias, ssd_D, ssd_norm_g, ssd_w_out, final_g, loss_target, m_ada_w, m_ada_b, m_norm_g, m_ffn_w1, m_ffn_w3, m_ffn_w2, m_ev_w_in, m_gdn_conv_w, m_gdn_A_log, m_gdn_dt_bias, m_gdn_norm_g, m_mla_q_norm_g, m_mla_w_uq, m_mla_kv_norm_g, m_mla_w_ukv, m_ev_w_out, m_ssd_w_in, m_ssd_conv_w, m_ssd_conv_b, m_ssd_A_log, m_ssd_dt_bias, m_ssd_D, m_ssd_norm_g, m_ssd_w_out, m_final_g, v_ada_w, v_ada_b, v_norm_g, v_ffn_w1, v_ffn_w3, v_ffn_w2, v_ev_w_in, v_gdn_conv_w, v_gdn_A_log, v_gdn_dt_bias, v_gdn_norm_g, v_mla_q_norm_g, v_mla_w_uq, v_mla_kv_norm_g, v_mla_w_ukv, v_ev_w_out, v_ssd_w_in, v_ssd_conv_w, v_ssd_conv_b, v_ssd_A_log, v_ssd_dt_bias, v_ssd_D, v_ssd_norm_g, v_ssd_w_out, v_final_g):
    a = dict(locals())
    w = {n: a[n] for n in _WEIGHTS}
    m = {n: a["m_" + n] for n in _WEIGHTS}
    v = {n: a["v_" + n] for n in _WEIGHTS}
    mx, my, mc = _mesh_pos()
    me = 4 * mx + 2 * my + mc
    t_len = x.shape[1]
    shards = range(N_DEV)

    small_names, big_names = list(_SMALL), list(_BIG)
    axis_of = {**_SMALL, **_BIG}
    small_g = _exchange([c] + [w[n] for n in small_names], False, name="gather_small")
    c_all = small_g[0].reshape(N_DEV, D_MODEL)
    fw = {n: _join(p, _SMALL[n]) for n, p in zip(small_names, small_g[1:])}
    first_names = [n for n in big_names if not n.startswith("ssd")]
    first_g = dict(zip(first_names, _exchange([w[n][:1].astype(BF16) for n in first_names], False, name="gather_first")))
    rest_src = {n: (w[n] if n.startswith("ssd") else w[n][1:]).astype(BF16) for n in big_names}
    rest_handle, rest_token = _exchange_start([rest_src[n] for n in big_names], False, name="gather_rest_start")
    rest_g = {}

    fs, es, os_ = ffn_w1.shape[3], ev_w_in.shape[2], ssd_w_in.shape[2]
    half = range(N_DEV // 2)
    plan13 = [[[(0, s, 0, fs) for s in half] + [(1, s, 0, fs) for s in half]
               + [(0, s + 4, 0, fs) for s in half] + [(1, s + 4, 0, fs) for s in half]]]
    plan_ev, k_at = [], 0
    for n0, n1, k0 in sorted(_EV_SEGS, key=lambda seg: seg[2]):
        if k0 > k_at:
            plan_ev.append(("z", k0 - k_at))
        plan_ev += _shard_pieces(0, n0, n1, es)
        k_at = k0 + n1 - n0
    assert k_at == EV_W and es * N_DEV == EV_NAT_W and os_ * N_DEV == OD_NAT_W
    plan_od = [[_shard_pieces(0, 0, OD_NAT_W, os_) + [("z", OD_W - OD_NAT_W)]]]

    def layer_weights(l, x_in):
        if l == 1:
            rest_g.update(zip(big_names, _exchange_wait(rest_handle, x_in, me, name="gather_rest_wait")))
        e = l // 2
        src = first_g if l == 0 else rest_g
        i = 0 if l == 0 else l - 1
        ie = 0 if (l == 0 or l % 2) else e - 1
        g1 = src["ffn_w1"].reshape(N_DEV, -1, fs)
        g3 = src["ffn_w3"].reshape(N_DEV, -1, fs)
        p = {"norm_g": fw["norm_g"][l],
             "w13": [_cols([(g1, (2 * i + j) * D_MODEL), (g3, (2 * i + j) * D_MODEL)], D_MODEL, plan13, BF16,
                           name="join_w13")[0][0] for j in range(2)],
             "w2": [src["ffn_w2"][:, i, j].reshape(D_FF, D_MODEL) for j in range(2)]}
        if l % 2 == 0:
            p["w_in_k"] = _cols([(src["ev_w_in"].reshape(N_DEV, -1, es), ie * D_MODEL)], D_MODEL, [[plan_ev]], BF16,
                                name="join_ev_in")[0][0]
            for n in ("mla_w_uq", "mla_w_ukv", "ev_w_out"):
                p[n] = _join(src[n][:, ie], axis_of[n] - 1)
            p["gdn_conv_w"] = fw["gdn_conv_w"][e]
            for n in ("gdn_A_log", "gdn_dt_bias", "gdn_norm_g", "mla_q_norm_g", "mla_kv_norm_g"):
                p[n] = w[n][e]
        else:
            p["w_in_k"] = _cols([(rest_g["ssd_w_in"].reshape(N_DEV, -1, os_), e * D_MODEL)], D_MODEL, plan_od, BF16,
                                name="join_od_in")[0][0]
            p["ssd_w_out"] = _join(rest_g["ssd_w_out"][:, e], axis_of["ssd_w_out"] - 1)
            for n in ("ssd_conv_w", "ssd_conv_b", "ssd_norm_g"):
                p[n] = fw[n][e]
            for n in ("ssd_A_log", "ssd_dt_bias", "ssd_D"):
                p[n] = w[n][e]
        return p

    def w13_cols(s, third):
        k0 = (s % 4) * fs + (2 * FF_HALF if s >= 4 else 0) + (FF_HALF if third else 0)
        return [(0, 0, k0, k0 + fs)]

    sent = {}

    def layer_done(l, gl, dx_l):
        d13 = [_cols([(gl["w13"][j][None], 0)], D_MODEL, [[w13_cols(s, False) for s in shards],
                                                          [w13_cols(s, True) for s in shards]], F32, name="split_w13")
               for j in range(2)]
        pieces = {"norm_g": _split(gl["norm_g"][None], 2),
                  "ffn_w1": jnp.stack([d13[j][0] for j in range(2)], axis=1)[:, None],
                  "ffn_w3": jnp.stack([d13[j][1] for j in range(2)], axis=1)[:, None],
                  "ffn_w2": jnp.stack([gl["w2"][j].reshape(N_DEV, -1, D_MODEL) for j in range(2)], axis=1)[:, None]}
        if l % 2 == 0:
            pieces["ev_w_in"] = _cols([(gl["w_in_k"][None], 0)], D_MODEL,
                                      [[_mapped_pieces(s * es, (s + 1) * es, _EV_SEGS) for s in shards]], F32,
                                      name="split_ev_in")[0][:, None]
            for n in ("gdn_conv_w", "mla_w_uq", "mla_w_ukv", "ev_w_out"):
                pieces[n] = _split(gl[n][None], axis_of[n])
        else:
            pieces["ssd_w_in"] = _cols([(gl["w_in_k"][None], 0)], D_MODEL,
                                       [[[(0, 0, s * os_, (s + 1) * os_)] for s in shards]], F32,
                                       name="split_od_in")[0][:, None]
            for n in ("ssd_conv_w", "ssd_conv_b", "ssd_norm_g", "ssd_w_out"):
                pieces[n] = _split(gl[n][None], axis_of[n])
        names = list(pieces)
        if l == 0:
            sent[l] = dict(zip(names, _exchange([pieces[n] for n in names], True, name="scatter_layer0")))
            return None
        handle, token = _exchange_start([pieces[n] for n in names], True, name=f"scatter_start_{l}")
        sent[l] = (names, handle)
        return token

    n_col = ada_w.shape[2]
    ada_b_shard = lax.dynamic_slice(ada_b, (0, me * n_col), (DEPTH, n_col)).reshape(DEPTH, 1, n_col)
    mod_all = _exchange([_mod_shard(c_all, ada_w, ada_b_shard, name="mod")], False, name="gather_mod")[0]
    mod_me = lax.dynamic_index_in_dim(mod_all, me, axis=2, keepdims=False)
    mod = jnp.transpose(mod_me, (1, 0, 2)).reshape(DEPTH, N_DEV * n_col) + rest_token

    pos = positions.astype(F32).reshape(t_len, 1)
    loss, dx, grads, dmod = _local_step(x[0], loss_target[0], pos, mod, final_g, layer_weights, layer_done)
    for l in range(1, DEPTH):
        names, handle = sent[l]
        sent[l] = dict(zip(names, _exchange_wait(handle, dx, me, name=f"scatter_wait_{l}")))

    repl_shapes = [w[n].shape for n in _REPL] + [(1,)]
    parts8 = _exchange([_pack([dmod] + [grads[n] for n in _REPL[1:]] + [loss[0, :1]])], False, name="gather_repl")[0]
    zero = jnp.zeros((1,), F32)
    r_grad, r_delta, r_m, r_v = [
        _unpack(o, repl_shapes) for o in _adamw(_pack([w[n] for n in _REPL] + [zero]), parts8,
                                                _pack([m[n] for n in _REPL] + [zero]),
                                                _pack([v[n] for n in _REPL] + [zero]), name="adamw_repl")]
    out = {"grad": {}, "delta": {}, "m": {}, "v": {}}
    for i, n in enumerate(_REPL):
        out["grad"][n], out["delta"][n], out["m"][n], out["v"][n] = r_grad[i], r_delta[i], r_m[i], r_v[i]
    loss_total = r_grad[-1].reshape(())

    dmod_all = parts8[:, :dmod.size // PACK_W].reshape((N_DEV,) + dmod.shape)
    dmod_cols = jnp.transpose(lax.dynamic_slice_in_dim(dmod_all, me * n_col, n_col, axis=2), (1, 0, 2))
    g_ada = _ada_w_grad(c_all, dmod_cols, name="ada_w_grad")
    for k, o in zip(("grad", "delta", "m", "v"), _adamw_nd(ada_w, g_ada[None], m["ada_w"], v["ada_w"], name="adamw_ada")):
        out[k]["ada_w"] = o

    for n in small_names + big_names:
        if n in ("norm_g", "ffn_w1", "ffn_w3", "ffn_w2"):
            layers = range(DEPTH)
        else:
            layers = range(1, DEPTH, 2) if n.startswith("ssd") else range(0, DEPTH, 2)
        g8 = jnp.concatenate([sent[l][n] for l in layers], axis=1)
        for k, o in zip(("grad", "delta", "m", "v"), _adamw_nd(w[n], g8, m[n], v[n], name="adamw_" + n)):
            out[k][n] = o

    return (loss_total, dx.reshape(x.shape), *[out["grad"][n] for n in _WEIGHTS], *[out["delta"][n] for n in _WEIGHTS],
            *[out["m"][n] for n in _WEIGHTS], *[out["v"][n] for n in _WEIGHTS])
```

```python
import functools
import math

import numpy as np
import jax
import jax.numpy as jnp
from jax import lax
from jax.experimental import pallas as pl
from jax.experimental.pallas import tpu as pltpu

F32 = jnp.float32
BF16 = jnp.bfloat16
HI = lax.Precision.HIGHEST

D_MODEL = 1024
DEPTH = 4
CHUNK = 64
NORM_EPS = 1e-6
CONV_K = 4
D_FF = 2816
GDN_HEADS = 4
GDN_DK = 128
MLA_HEADS = 4
MLA_NOPE = 128
MLA_ROPE = 64
ROPE_THETA = 10000.0
SSD_HEADS = 32
SSD_HEADDIM = 64
SSD_GROUPS = 4
SSD_STATE = 128
SSD_D_INNER = 2048
N_DEV = 8

ADAM_LR = 0.001
ADAM_B1 = 0.9
ADAM_B2 = 0.999
ADAM_EPS = 1e-08
ADAM_WD = 0.01
ADAM_STEP = 10

V7X_VMEM_LIMIT = 56 * 1024 * 1024
ROW_TILE = 512
SEQ_TILE = 128
ATT_TILE = 1024
MM_RESIDENT_BYTES = 12 * 1024 * 1024
FF_HALF = D_FF // 2
LANE = 128

EV_QKV, EV_CQ, EV_MISC, EV_Z, EV_CKV, EV_W = 0, 1536, 1920, 2048, 2560, 2816
OD_Z, OD_XBC, OD_DT, OD_W = 0, 2048, 5120, 5376


def _cparams(sem=None):
    return pltpu.CompilerParams(dimension_semantics=sem, vmem_limit_bytes=V7X_VMEM_LIMIT)


def _pick(n, cands):
    for c in cands:
        if n % c == 0:
            return c
    return n


_DN = {"nn": (((1,), (0,)), ((), ())), "nt": (((1,), (1,)), ((), ())), "tn": (((0,), (0,)), ((), ()))}


F32_3PASS = 2


def _dot(a, b, mode, hi=False):
    if hi:
        prec = lax.Precision.HIGH if hi == F32_3PASS else HI
        return lax.dot_general(a.astype(F32), b.astype(F32), _DN[mode], precision=prec, preferred_element_type=F32)
    return lax.dot_general(a.astype(BF16), b.astype(BF16), _DN[mode], preferred_element_type=F32)


@functools.partial(jax.custom_vjp, nondiff_argnums=(2, 3))
def _mm(a, b, mode, hi):
    return _dot(a, b, mode, hi)


def _mm_fwd(a, b, mode, hi):
    return _dot(a, b, mode, hi), (a, b)


def _mm_bwd(mode, hi, res, g):
    a, b = res
    if mode == "nn":
        return _dot(g, b, "nt", hi), _dot(a, g, "tn", hi)
    if mode == "nt":
        return _dot(g, b, "nn", hi), _dot(g, a, "tn", hi)
    return _dot(b, g, "nt", hi), _dot(a, g, "nn", hi)


_mm.defvjp(_mm_fwd, _mm_bwd)


def _iota2(shape, dim):
    return lax.broadcasted_iota(jnp.int32, shape, dim)


def _row_spec(a, tm):
    if isinstance(a, tuple):
        arr, c0, w = a
        assert c0 % w == 0
        cb = c0 // w
        return arr, pl.BlockSpec((tm, w), lambda i, cb=cb: (i, cb))
    return a, pl.BlockSpec((tm, a.shape[1]), lambda i: (i, 0))


def _full_spec(b):
    return pl.BlockSpec(b.shape, lambda i: (0,) * b.ndim)


def _rows(fn, tiled, bcast, outs, *, name, tm=ROW_TILE, also_t=()):
    arrs, specs = zip(*[_row_spec(a, 0) for a in tiled])
    t_len = arrs[0].shape[0]
    tm = min(tm, t_len)
    arrs, specs = zip(*[_row_spec(a, tm) for a in tiled])
    nt, nb, no = len(tiled), len(bcast), len(outs)

    def body(*refs):
        ins = [r[...].astype(F32) for r in refs[:nt]] + [r[...] for r in refs[nt:nt + nb]]
        res = fn(*ins)
        for r, v in zip(refs[nt + nb:nt + nb + no], res):
            r[...] = v.astype(r.dtype)
        for r, k in zip(refs[nt + nb + no:], also_t):
            r[...] = res[k].T.astype(r.dtype)

    return pl.pallas_call(
        body, grid=(t_len // tm,), name=name,
        in_specs=list(specs) + [_full_spec(b) for b in bcast],
        out_specs=[pl.BlockSpec((tm, c), lambda i: (i, 0)) for c, _ in outs]
        + [pl.BlockSpec((outs[k][0], tm), lambda i: (0, i)) for k in also_t],
        out_shape=[jax.ShapeDtypeStruct((t_len, c), dt) for c, dt in outs]
        + [jax.ShapeDtypeStruct((outs[k][0], t_len), outs[k][1]) for k in also_t],
        compiler_params=_cparams(("parallel",)),
    )(*arrs, *bcast)


def _rows_vjp(fn, tiled, consts, bcast, bconsts, douts, grads, *, name, adds=None, tm=ROW_TILE // 2):
    adds = adds or {}
    t_arrs, t_specs = zip(*[_row_spec(a, 0) for a in tiled])
    t_len = t_arrs[0].shape[0]
    tm = min(tm, t_len)
    rows_in = list(tiled) + list(consts) + list(douts) + [adds[k] for k in sorted(adds)]
    arrs, specs = zip(*[_row_spec(a, tm) for a in rows_in])
    nt, nc, nb, nbc, nd, na = len(tiled), len(consts), len(bcast), len(bconsts), len(douts), len(adds)
    add_pos = {k: j for j, k in enumerate(sorted(adds))}
    want = [j for j, g in enumerate(grads) if g is not None]

    def body(*refs):
        p = 0
        t = [r[...].astype(F32) for r in refs[p:p + nt]]; p += nt
        c = [r[...].astype(F32) for r in refs[p:p + nc]]; p += nc
        d = [r[...].astype(F32) for r in refs[p:p + nd]]; p += nd
        a = [r[...].astype(F32) for r in refs[p:p + na]]; p += na
        b = [r[...] for r in refs[p:p + nb]]; p += nb
        bc = [r[...] for r in refs[p:p + nbc]]; p += nbc
        g_refs = refs[p:p + len(want)]; p += len(want)
        gb_refs = refs[p:p + nb]

        def f(*args):
            return fn(*args[:nt], *c, *args[nt:], *bc)

        _, vjp = jax.vjp(f, *t, *b)
        g = vjp(tuple(d))
        for r, j in zip(g_refs, want):
            val = g[j]
            if j in add_pos:
                val = val + a[add_pos[j]]
            r[...] = val.astype(r.dtype)

        @pl.when(pl.program_id(0) == 0)
        def _():
            for r in gb_refs:
                r[...] = jnp.zeros_like(r)

        for r, val in zip(gb_refs, g[nt:]):
            r[...] += val

    def width(a):
        return a[2] if isinstance(a, tuple) else a.shape[1]

    res = pl.pallas_call(
        body, grid=(t_len // tm,), name=name,
        in_specs=list(specs) + [_full_spec(b) for b in list(bcast) + list(bconsts)],
        out_specs=[pl.BlockSpec((tm, width(tiled[j])), lambda i: (i, 0)) for j in want] + [_full_spec(b) for b in bcast],
        out_shape=[jax.ShapeDtypeStruct((t_len, width(tiled[j])), grads[j]) for j in want]
        + [jax.ShapeDtypeStruct(b.shape, F32) for b in bcast],
        compiler_params=_cparams(("arbitrary",)),
    )(*arrs, *bcast, *bconsts)
    tg = [None] * nt
    for r, j in zip(res[:len(want)], want):
        tg[j] = r
    return tg, list(res[len(want):])


def _matmul(a, b, mode, out_dtype, *, name):
    if mode in ("tn", "kn"):
        assert out_dtype == F32
        k_len, m_len = a.shape if mode == "tn" else a.shape[::-1]
        n_len = b.shape[1]
        tm, tn = m_len, n_len
        while tm * tn * 4 > MM_RESIDENT_BYTES and tn % (2 * LANE) == 0:
            tn //= 2
        tk = _pick(k_len, (512, 256, 128))
        if mode == "kn" and k_len % 1024 == 0 and 4 * 1024 * (tm + tn) + 8 * tm * tn <= 44 * 1024 * 1024:
            tk = 1024
    else:
        m_len, k_len = a.shape
        n_len = b.shape[1] if mode == "nn" else b.shape[0]
        tk, tn = k_len, n_len
        while tk * tn * 2 > MM_RESIDENT_BYTES and tn % (2 * LANE) == 0:
            tn //= 2
        tm = _pick(m_len, (512, 256, 128))
        while tm * max(4 * tn, 2 * tk) > MM_RESIDENT_BYTES // 2 and tm % 256 == 0:
            tm //= 2
    nk = k_len // tk
    if mode == "nn":
        a_spec = pl.BlockSpec((tm, tk), lambda j, i, k: (i, k))
        b_spec = pl.BlockSpec((tk, tn), lambda j, i, k: (k, j))
    elif mode == "nt":
        a_spec = pl.BlockSpec((tm, tk), lambda j, i, k: (i, k))
        b_spec = pl.BlockSpec((tn, tk), lambda j, i, k: (j, k))
    elif mode == "kn":
        a_spec = pl.BlockSpec((tm, tk), lambda j, i, k: (i, k))
        b_spec = pl.BlockSpec((tk, tn), lambda j, i, k: (k, j))
    else:
        a_spec = pl.BlockSpec((tk, tm), lambda j, i, k: (k, i))
        b_spec = pl.BlockSpec((tk, tn), lambda j, i, k: (k, j))

    def body(a_ref, b_ref, o_ref):
        part = _dot(a_ref[...], b_ref[...], "nn" if mode == "kn" else mode)
        if nk == 1:
            o_ref[...] = part.astype(o_ref.dtype)
        else:
            @pl.when(pl.program_id(2) == 0)
            def _():
                o_ref[...] = jnp.zeros_like(o_ref)

            o_ref[...] += part

    return pl.pallas_call(
        body, grid=(n_len // tn, m_len // tm, nk), name=name,
        in_specs=[a_spec, b_spec],
        out_specs=pl.BlockSpec((tm, tn), lambda j, i, k: (i, j)),
        out_shape=jax.ShapeDtypeStruct((m_len, n_len), out_dtype),
        compiler_params=_cparams(("parallel", "parallel", "arbitrary")),
    )(a, b)


def _ffn_act(h, w13, *, name):
    t_len, d = h.shape
    tm = min(ROW_TILE, t_len)

    def body(h_ref, w_ref, s_ref, st_ref):
        ab = _dot(h_ref[...], w_ref[...], "nn")
        a, b = ab[:, :FF_HALF], ab[:, FF_HALF:]
        s = a * jax.nn.sigmoid(a) * b
        s_ref[...] = s.astype(s_ref.dtype)
        st_ref[...] = s.T.astype(st_ref.dtype)

    return pl.pallas_call(
        body, grid=(2, t_len // tm), name=name,
        in_specs=[pl.BlockSpec((tm, d), lambda f, i: (i, 0)), pl.BlockSpec((d, 2 * FF_HALF), lambda f, i: (0, f))],
        out_specs=[pl.BlockSpec((tm, FF_HALF), lambda f, i: (i, f)), pl.BlockSpec((FF_HALF, tm), lambda f, i: (f, i))],
        out_shape=[jax.ShapeDtypeStruct((t_len, D_FF), BF16), jax.ShapeDtypeStruct((D_FF, t_len), BF16)],
        compiler_params=_cparams(("parallel", "parallel")),
    )(h, w13)


def _ffn_act_bwd(h, dy, w13, w2, *, name):
    t_len, d = h.shape
    tm = min(ROW_TILE, t_len)

    def body(h_ref, dy_ref, w_ref, w2_ref, o_ref):
        ab = _dot(h_ref[...], w_ref[...], "nn")
        a, b = ab[:, :FF_HALF], ab[:, FF_HALF:]
        ds = _dot(dy_ref[...], w2_ref[...], "nt")
        sig = jax.nn.sigmoid(a)
        silu = a * sig
        da = ds * b * (sig * (1.0 + a * (1.0 - sig)))
        db = ds * silu
        o_ref[...] = jnp.concatenate([da, db], axis=-1).astype(o_ref.dtype)

    return pl.pallas_call(
        body, grid=(2, t_len // tm), name=name,
        in_specs=[pl.BlockSpec((tm, d), lambda f, i: (i, 0)), pl.BlockSpec((tm, d), lambda f, i: (i, 0)),
                  pl.BlockSpec((d, 2 * FF_HALF), lambda f, i: (0, f)), pl.BlockSpec((FF_HALF, d), lambda f, i: (f, 0))],
        out_specs=pl.BlockSpec((tm, 2 * FF_HALF), lambda f, i: (i, f)),
        out_shape=jax.ShapeDtypeStruct((t_len, 2 * D_FF), BF16),
        compiler_params=_cparams(("parallel", "parallel")),
    )(h, dy, w13, w2)


CONV_CB = 512
HALO = 8


def _conv_fwd(p, c0, n_ch, w, b, *, name):
    t_len = p.shape[0]
    tm = min(ROW_TILE, t_len)
    hb = tm // HALO
    cb0 = c0 // CONV_CB

    def body(x_ref, halo_ref, w_ref, b_ref, act_ref, pre_ref):
        first = pl.program_id(1) == 0
        halo = jnp.where(first, 0.0, halo_ref[...])
        xx = jnp.concatenate([halo, x_ref[...]], axis=0)
        wv = w_ref[...]
        acc = b_ref[...] + wv[0:1] * xx[HALO - 3:HALO - 3 + tm]
        for j in range(1, CONV_K):
            acc = acc + wv[j:j + 1] * xx[HALO - 3 + j:HALO - 3 + j + tm]
        pre_ref[...] = acc
        act_ref[...] = acc * jax.nn.sigmoid(acc)

    return pl.pallas_call(
        body, grid=(n_ch // CONV_CB, t_len // tm), name=name,
        in_specs=[pl.BlockSpec((tm, CONV_CB), lambda j, i: (i, cb0 + j)),
                  pl.BlockSpec((HALO, CONV_CB), lambda j, i: (jnp.maximum(i * hb - 1, 0), cb0 + j)),
                  pl.BlockSpec((CONV_K, CONV_CB), lambda j, i: (0, j)),
                  pl.BlockSpec((1, CONV_CB), lambda j, i: (0, j))],
        out_specs=[pl.BlockSpec((tm, CONV_CB), lambda j, i: (i, j))] * 2,
        out_shape=[jax.ShapeDtypeStruct((t_len, n_ch), F32)] * 2,
        compiler_params=_cparams(("parallel", "arbitrary")),
    )(p, p, w, b)


def _conv_bwd(dact, pre, pre_c0, p, p_c0, w, *, name):
    t_len, n_ch = dact.shape
    tm = min(ROW_TILE, t_len)
    hb = tm // HALO
    nt = t_len // tm
    last_hb = t_len // HALO - 1
    cb0 = p_c0 // CONV_CB
    cbp = pre_c0 // CONV_CB

    def dsilu(z):
        sig = jax.nn.sigmoid(z)
        return sig * (1.0 + z * (1.0 - sig))

    def body(d_ref, dn_ref, pre_ref, pren_ref, x_ref, xh_ref, w_ref, dx_ref, dw_ref, db_ref):
        i = pl.program_id(1)
        dpre = d_ref[...] * dsilu(pre_ref[...])
        dnext = jnp.where(i == nt - 1, 0.0, dn_ref[...] * dsilu(pren_ref[...]))
        ext = jnp.concatenate([dpre, dnext], axis=0)
        xx = jnp.concatenate([jnp.where(i == 0, 0.0, xh_ref[...]), x_ref[...]], axis=0)
        wv = w_ref[...]
        dx = wv[0:1] * ext[3:3 + tm]
        for j in range(1, CONV_K):
            dx = dx + wv[j:j + 1] * ext[3 - j:3 - j + tm]
        dx_ref[...] = dx
        dws = [jnp.sum(dpre * xx[HALO - 3 + j:HALO - 3 + j + tm], axis=0, keepdims=True) for j in range(CONV_K)]

        @pl.when(i == 0)
        def _():
            dw_ref[...] = jnp.zeros_like(dw_ref)
            db_ref[...] = jnp.zeros_like(db_ref)

        dw_ref[...] += jnp.concatenate(dws, axis=0)
        db_ref[...] += jnp.sum(dpre, axis=0, keepdims=True)

    tile = lambda off: pl.BlockSpec((tm, CONV_CB), lambda j, i: (i, off + j))
    nxt = lambda off: pl.BlockSpec((HALO, CONV_CB), lambda j, i: (jnp.minimum((i + 1) * hb, last_hb), off + j))
    return pl.pallas_call(
        body, grid=(n_ch // CONV_CB, nt), name=name,
        in_specs=[tile(0), nxt(0), tile(cbp), nxt(cbp), tile(cb0),
                  pl.BlockSpec((HALO, CONV_CB), lambda j, i: (jnp.maximum(i * hb - 1, 0), cb0 + j)),
                  pl.BlockSpec((CONV_K, CONV_CB), lambda j, i: (0, cbp + j))],
        out_specs=[tile(0), pl.BlockSpec((CONV_K, CONV_CB), lambda j, i: (0, j)), pl.BlockSpec((1, CONV_CB), lambda j, i: (0, j))],
        out_shape=[jax.ShapeDtypeStruct((t_len, n_ch), F32), jax.ShapeDtypeStruct((CONV_K, n_ch), F32),
                   jax.ShapeDtypeStruct((1, n_ch), F32)],
        compiler_params=_cparams(("parallel", "arbitrary")),
    )(dact, dact, pre, pre, p, p, w)


@jax.custom_vjp
def _inv_unit_lower_many(a_cat):
    c = a_cat.shape[0]
    assert LANE % c == 0 and a_cat.shape[1] % LANE == 0
    x = (_iota2(a_cat.shape, 0) == _iota2(a_cat.shape, 1) % c).astype(F32)
    tiles = [a_cat[:, t * LANE:(t + 1) * LANE] for t in range(a_cat.shape[1] // LANE)]
    first = (_iota2((c, LANE), 1) // c) * c
    for j in range(c - 1):
        col = jnp.concatenate([jnp.take_along_axis(t, first + j, axis=1) for t in tiles], axis=-1)
        x = x - col * x[j:j + 1, :]
    return x


def _inv_fwd(a_cat):
    x = _inv_unit_lower_many(a_cat)
    return x, x


def _inv_bwd(x, g):
    c = x.shape[0]
    parts = [-_dot(x[:, s], _dot(g[:, s], x[:, s], "nt", F32_3PASS), "tn", F32_3PASS)
             for s in (slice(i * c, (i + 1) * c) for i in range(x.shape[1] // c))]
    return (jnp.concatenate(parts, axis=-1),)


_inv_unit_lower_many.defvjp(_inv_fwd, _inv_bwd)


def _l2norm(x):
    return x * lax.rsqrt(jnp.sum(x * x, axis=-1, keepdims=True) + NORM_EPS)


def _rms(x):
    return x * lax.rsqrt(jnp.mean(x * x, axis=-1, keepdims=True) + NORM_EPS)


def _tri_masks(c):
    rows, cols = _iota2((c, c), 0), _iota2((c, c), 1)
    return rows >= cols, rows > cols, (rows >= cols).astype(F32), (rows <= cols).astype(F32)


def _gdn_tile(q, k, v, misc, s0, alog, dtb):
    c = CHUNK
    lower, strict, ltri, utri = _tri_masks(c)
    n_chunk = q.shape[0] // c
    pre = []
    for h in range(GDN_HEADS):
        hs = slice(h * LANE, (h + 1) * LANE)
        neg_a = -jnp.exp(alog[:, h:h + 1])
        for ci in range(n_chunk):
            sl = slice(ci * c, (ci + 1) * c)
            qn = _l2norm(q[sl, hs]) * (GDN_DK ** -0.5)
            kn = _l2norm(k[sl, hs])
            beta = jax.nn.sigmoid(misc[sl, 64 + h:65 + h])
            g = neg_a * jax.nn.softplus(misc[sl, 68 + h:69 + h] + dtb[:, h:h + 1])
            gb = jnp.broadcast_to(g, (c, c))
            gc_col = _mm(ltri, gb, "nn", True)
            gc_row = _mm(gb, utri, "tn", True)
            decay = jnp.where(lower, jnp.exp(jnp.where(lower, gc_col - gc_row, 0.0)), 0.0)
            kb = kn * beta
            a_mat = jnp.where(strict, _mm(kb, kn, "nt", False) * decay, 0.0)
            pre.append((qn, kn, kb, v[sl, hs] * beta, decay, gc_col[:, 0:1], gc_col[c - 1:c, 0:1], a_mat))
    t_all = _inv_unit_lower_many(jnp.concatenate([p[7] for p in pre], axis=-1))
    o_heads, s_heads = [], []
    for h in range(GDN_HEADS):
        s = s0[h * GDN_DK:(h + 1) * GDN_DK]
        outs = []
        for ci in range(n_chunk):
            i = h * n_chunk + ci
            qn, kn, kb, vb, decay, gc, g_last, _ = pre[i]
            t_inv = t_all[:, i * c:(i + 1) * c]
            u = _mm(t_inv, vb, "nn", F32_3PASS)
            w = _mm(t_inv, kb * jnp.exp(gc), "nn", F32_3PASS)
            attn = _mm(qn, kn, "nt", False) * decay
            k_end = kn * jnp.exp(g_last - gc)
            q_start = qn * jnp.exp(gc)
            v_new = u - _mm(w, s, "nn", False)
            outs.append(_mm(q_start, s, "nn", False) + _mm(attn, v_new, "nn", False))
            s = s * jnp.exp(g_last) + _mm(k_end, v_new, "tn", False)
        o_heads.append(jnp.concatenate(outs, axis=0))
        s_heads.append(s)
    return jnp.concatenate(o_heads, axis=-1), jnp.concatenate(s_heads, axis=0)


def _gdn_specs(tt, rev_n=None):
    t = (lambda i: i) if rev_n is None else (lambda i: rev_n - 1 - i)
    col = lambda j: pl.BlockSpec((tt, GDN_HEADS * LANE), lambda i: (t(i), j))
    vec = pl.BlockSpec((1, LANE), lambda i: (0, 0))
    misc = pl.BlockSpec((tt, LANE), lambda i: (t(i), EV_MISC // LANE))
    return [col(0), col(1), col(2), misc, vec, vec], t


def _gdn_fwd(act, p, alog, dtb, *, name):
    t_len = act.shape[0]
    tt = min(SEQ_TILE, t_len)
    ntile = t_len // tt
    in_specs, _ = _gdn_specs(tt)

    def body(q_ref, k_ref, v_ref, m_ref, al_ref, dt_ref, o_ref, s_ref, state):
        @pl.when(pl.program_id(0) == 0)
        def _():
            state[...] = jnp.zeros_like(state)

        s_ref[0] = state[...]
        o, s_new = _gdn_tile(q_ref[...], k_ref[...], v_ref[...], m_ref[...], state[...], al_ref[...], dt_ref[...])
        o_ref[...] = o
        state[...] = s_new

    return pl.pallas_call(
        body, grid=(ntile,), name=name, in_specs=in_specs,
        out_specs=[pl.BlockSpec((tt, GDN_HEADS * LANE), lambda i: (i, 0)),
                   pl.BlockSpec((1, GDN_HEADS * GDN_DK, LANE), lambda i: (i, 0, 0))],
        out_shape=[jax.ShapeDtypeStruct((t_len, GDN_HEADS * LANE), F32),
                   jax.ShapeDtypeStruct((ntile, GDN_HEADS * GDN_DK, LANE), F32)],
        scratch_shapes=[pltpu.VMEM((GDN_HEADS * GDN_DK, LANE), F32)],
        compiler_params=_cparams(("arbitrary",)),
    )(act, act, act, p, alog, dtb)


def _gdn_bwd(act, p, alog, dtb, states, do, *, name):
    t_len = act.shape[0]
    tt = min(SEQ_TILE, t_len)
    ntile = t_len // tt
    in_specs, t = _gdn_specs(tt, ntile)

    def body(q_ref, k_ref, v_ref, m_ref, al_ref, dt_ref, s0_ref, do_ref,
             dq_ref, dk_ref, dv_ref, dm_ref, dal_ref, ddt_ref, dstate):
        @pl.when(pl.program_id(0) == 0)
        def _():
            dstate[...] = jnp.zeros_like(dstate)
            dal_ref[...] = jnp.zeros_like(dal_ref)
            ddt_ref[...] = jnp.zeros_like(ddt_ref)

        _, vjp = jax.vjp(_gdn_tile, q_ref[...], k_ref[...], v_ref[...], m_ref[...], s0_ref[0], al_ref[...], dt_ref[...])
        dq, dk, dv, dm, ds0, dal, ddt = vjp((do_ref[...], dstate[...]))
        dq_ref[...] = dq
        dk_ref[...] = dk
        dv_ref[...] = dv
        dm_ref[...] = dm
        dstate[...] = ds0
        dal_ref[...] += dal
        ddt_ref[...] += ddt

    row = pl.BlockSpec((tt, GDN_HEADS * LANE), lambda i: (t(i), 0))
    vec = pl.BlockSpec((1, LANE), lambda i: (0, 0))
    return pl.pallas_call(
        body, grid=(ntile,), name=name,
        in_specs=in_specs + [pl.BlockSpec((1, GDN_HEADS * GDN_DK, LANE), lambda i: (t(i), 0, 0)), row],
        out_specs=[row, row, row, pl.BlockSpec((tt, LANE), lambda i: (t(i), 0)), vec, vec],
        out_shape=[jax.ShapeDtypeStruct((t_len, GDN_HEADS * LANE), F32)] * 3
        + [jax.ShapeDtypeStruct((t_len, LANE), F32)] + [jax.ShapeDtypeStruct((1, LANE), F32)] * 2,
        scratch_shapes=[pltpu.VMEM((GDN_HEADS * GDN_DK, LANE), F32)],
        compiler_params=_cparams(("arbitrary",)),
    )(act, act, act, p, alog, dtb, states, do)


def _head_expand():
    return jnp.asarray(np.repeat(np.eye(LANE, SSD_HEADS, dtype=np.float32), SSD_HEADDIM, axis=1))


@jax.custom_vjp
def _per_head(v, expand):
    rows = max(v.shape[0], 8)
    v8 = jnp.broadcast_to(v, (rows, LANE))
    half = _iota2((rows, LANE), 1) // SSD_HEADDIM
    tiles = [jnp.take_along_axis(v8, half + 2 * j, axis=1) for j in range(SSD_D_INNER // LANE)]
    return jnp.concatenate(tiles, axis=-1)[:v.shape[0]]


def _per_head_fwd(v, expand):
    return _per_head(v, expand), expand


def _per_head_bwd(expand, g):
    rows = g.shape[0]
    g8 = jnp.broadcast_to(g, (8, g.shape[1])) if rows == 1 else g
    dv = lax.dot_general(g8, expand, _DN["nt"], precision=lax.Precision.HIGH, preferred_element_type=F32)
    return dv[0:1] if rows == 1 else dv, None


_per_head.defvjp(_per_head_fwd, _per_head_bwd)


def _ssd_tile(xs, bm, cm, dtr, hs0, alog, dtb, dsk, expand):
    c = CHUNK
    gw = SSD_D_INNER // SSD_GROUPS
    hpg = SSD_HEADS // SSD_GROUPS
    lower, _, ltri, utri = _tri_masks(c)
    half = _iota2((c, LANE), 1) // SSD_HEADDIM
    dt = jax.nn.softplus(dtr + dtb)
    da = dt * (-jnp.exp(alog))
    xdt = xs * _per_head(dt, expand)
    d_x = _per_head(dsk, expand)
    hs = [hs0[g * SSD_STATE:(g + 1) * SSD_STATE] for g in range(SSD_GROUPS)]
    ys = []
    for ci in range(xs.shape[0] // c):
        sl = slice(ci * c, (ci + 1) * c)
        acs = _mm(ltri, da[sl], "nn", True)
        acs_t = _mm(da[sl], utri, "tn", True)
        acs_last = acs[c - 1:c, :]
        e_start = _per_head(jnp.exp(acs), expand)
        e_end = _per_head(jnp.exp(acs_last - acs), expand)
        e_dec = _per_head(jnp.exp(acs_last), expand)
        xdt_c = xdt[sl]
        y_tiles = [None] * (SSD_D_INNER // LANE)
        y_off = []
        for g in range(SSD_GROUPS):
            b_g = bm[sl, g * SSD_STATE:(g + 1) * SSD_STATE]
            c_g = cm[sl, g * SSD_STATE:(g + 1) * SSD_STATE]
            gs = slice(g * gw, (g + 1) * gw)
            cb = _mm(c_g, b_g, "nt", False)
            y_off.append(_mm(c_g, hs[g], "nn", False) * e_start[:, gs])
            for r in range(hpg):
                h = g * hpg + r
                j = h // 2
                lm = jnp.where(lower, jnp.exp(jnp.where(lower, acs[:, h:h + 1] - acs_t[h:h + 1, :], 0.0)), 0.0)
                xm = jnp.where(half == (h % 2), xdt_c[:, j * LANE:(j + 1) * LANE], 0.0)
                part = _mm(cb * lm, xm, "nn", False)
                y_tiles[j] = part if y_tiles[j] is None else y_tiles[j] + part
            hs[g] = hs[g] * e_dec[:, gs] + _mm(b_g, xdt_c[:, gs] * e_end[:, gs], "tn", False)
        ys.append(jnp.concatenate(y_tiles, axis=-1) + jnp.concatenate(y_off, axis=-1) + d_x * xs[sl])
    return jnp.concatenate(ys, axis=0), jnp.concatenate(hs, axis=0)


def _ssd_specs(tt, rev_n=None):
    t = (lambda i: i) if rev_n is None else (lambda i: rev_n - 1 - i)
    vec = pl.BlockSpec((1, LANE), lambda i: (0, 0))
    specs = [pl.BlockSpec((tt, SSD_D_INNER), lambda i: (t(i), 0)),
             pl.BlockSpec((tt, 512), lambda i: (t(i), SSD_D_INNER // 512)),
             pl.BlockSpec((tt, 512), lambda i: (t(i), SSD_D_INNER // 512 + 1)),
             pl.BlockSpec((tt, LANE), lambda i: (t(i), OD_DT // LANE)), vec, vec, vec,
             pl.BlockSpec((LANE, SSD_D_INNER), lambda i: (0, 0))]
    return specs, t


def _ssd_fwd(act, p, alog, dtb, dsk, *, name):
    t_len = act.shape[0]
    tt = min(SEQ_TILE, t_len)
    ntile = t_len // tt
    in_specs, _ = _ssd_specs(tt)

    def body(x_ref, b_ref, c_ref, dt_ref, al_ref, db_ref, dk_ref, e_ref, y_ref, s_ref, state):
        @pl.when(pl.program_id(0) == 0)
        def _():
            state[...] = jnp.zeros_like(state)

        s_ref[0] = state[...]
        y, hs = _ssd_tile(x_ref[...], b_ref[...], c_ref[...], dt_ref[...], state[...], al_ref[...], db_ref[...],
                          dk_ref[...], e_ref[...])
        y_ref[...] = y
        state[...] = hs

    return pl.pallas_call(
        body, grid=(ntile,), name=name, in_specs=in_specs,
        out_specs=[pl.BlockSpec((tt, SSD_D_INNER), lambda i: (i, 0)),
                   pl.BlockSpec((1, SSD_GROUPS * SSD_STATE, 512), lambda i: (i, 0, 0))],
        out_shape=[jax.ShapeDtypeStruct((t_len, SSD_D_INNER), F32),
                   jax.ShapeDtypeStruct((ntile, SSD_GROUPS * SSD_STATE, 512), F32)],
        scratch_shapes=[pltpu.VMEM((SSD_GROUPS * SSD_STATE, 512), F32)],
        compiler_params=_cparams(("arbitrary",)),
    )(act, act, act, p, alog, dtb, dsk, _head_expand())


def _ssd_bwd(act, p, alog, dtb, dsk, states, dy, *, name):
    t_len = act.shape[0]
    tt = min(SEQ_TILE, t_len)
    ntile = t_len // tt
    in_specs, t = _ssd_specs(tt, ntile)

    def body(x_ref, b_ref, c_ref, dt_ref, al_ref, db_ref, dk_ref, e_ref, s0_ref, dy_ref,
             dx_ref, dbm_ref, dcm_ref, ddt_ref, dal_ref, ddb_ref, ddk_ref, dstate):
        @pl.when(pl.program_id(0) == 0)
        def _():
            dstate[...] = jnp.zeros_like(dstate)
            dal_ref[...] = jnp.zeros_like(dal_ref)
            ddb_ref[...] = jnp.zeros_like(ddb_ref)
            ddk_ref[...] = jnp.zeros_like(ddk_ref)

        expand = e_ref[...]

        def f(xs, bm, cm, dtr, hs0, al, db, dk):
            return _ssd_tile(xs, bm, cm, dtr, hs0, al, db, dk, expand)

        _, vjp = jax.vjp(f, x_ref[...], b_ref[...], c_ref[...], dt_ref[...], s0_ref[0], al_ref[...], db_ref[...],
                         dk_ref[...])
        dx, dbm, dcm, ddt, dhs, dal, ddb, ddk = vjp((dy_ref[...], dstate[...]))
        dx_ref[...] = dx
        dbm_ref[...] = dbm
        dcm_ref[...] = dcm
        ddt_ref[...] = ddt
        dstate[...] = dhs
        dal_ref[...] += dal
        ddb_ref[...] += ddb
        ddk_ref[...] += ddk

    vec = pl.BlockSpec((1, LANE), lambda i: (0, 0))
    rows = lambda w: pl.BlockSpec((tt, w), lambda i: (t(i), 0))
    return pl.pallas_call(
        body, grid=(ntile,), name=name,
        in_specs=in_specs + [pl.BlockSpec((1, SSD_GROUPS * SSD_STATE, 512), lambda i: (t(i), 0, 0)), rows(SSD_D_INNER)],
        out_specs=[rows(SSD_D_INNER), rows(512), rows(512), rows(LANE), vec, vec, vec],
        out_shape=[jax.ShapeDtypeStruct((t_len, SSD_D_INNER), F32), jax.ShapeDtypeStruct((t_len, 512), F32),
                   jax.ShapeDtypeStruct((t_len, 512), F32), jax.ShapeDtypeStruct((t_len, LANE), F32)]
        + [jax.ShapeDtypeStruct((1, LANE), F32)] * 3,
        scratch_shapes=[pltpu.VMEM((SSD_GROUPS * SSD_STATE, 512), F32)],
        compiler_params=_cparams(("arbitrary",)),
    )(act, act, act, p, alog, dtb, dsk, _head_expand(), states, dy)


ATT_SCALE = (MLA_NOPE + MLA_ROPE) ** -0.5
ATT_SCALE2 = ATT_SCALE * math.log2(math.e)
QK_W = 2 * LANE


def _chunk_mask(tq):
    return (_iota2((tq, tq), 1) // CHUNK) <= (_iota2((tq, tq), 0) // CHUNK)


def _attn_fwd(qc, kc, vv, *, name):
    t_len = qc.shape[0]
    tq = min(ATT_TILE, t_len)
    nq = t_len // tq

    def body(q_ref, k_ref, v_ref, o_ref, lse_ref, m_s, l_s, acc_s):
        qi, ki = pl.program_id(1), pl.program_id(2)

        @pl.when(ki == 0)
        def _():
            m_s[...] = jnp.full_like(m_s, -jnp.inf)
            l_s[...] = jnp.zeros_like(l_s)
            acc_s[...] = jnp.zeros_like(acc_s)

        def step(masked):
            s = _dot(q_ref[...], k_ref[...], "nt") * ATT_SCALE2
            if masked:
                s = jnp.where(_chunk_mask(tq), s, -jnp.inf)
            m_new = jnp.maximum(m_s[...], jnp.max(s, axis=-1, keepdims=True))
            alpha = jnp.exp2(m_s[...] - m_new)
            p = jnp.exp2(s - m_new)
            l_s[...] = alpha * l_s[...] + jnp.sum(p, axis=-1, keepdims=True)
            acc_s[...] = alpha * acc_s[...] + _dot(p, v_ref[...], "nn")
            m_s[...] = m_new

        @pl.when(ki < qi)
        def _():
            step(False)

        @pl.when(ki == qi)
        def _():
            step(True)
            o_ref[...] = acc_s[...] / l_s[...]
            lse_ref[...] = jnp.broadcast_to(m_s[...] + jnp.log2(l_s[...]), lse_ref.shape)

    kv_idx = lambda h, i, k: (jnp.minimum(k, i), h)
    return pl.pallas_call(
        body, grid=(MLA_HEADS, nq, nq), name=name,
        in_specs=[pl.BlockSpec((tq, QK_W), lambda h, i, k: (i, h)), pl.BlockSpec((tq, QK_W), kv_idx),
                  pl.BlockSpec((tq, LANE), kv_idx)],
        out_specs=[pl.BlockSpec((tq, LANE), lambda h, i, k: (i, h))] * 2,
        out_shape=[jax.ShapeDtypeStruct((t_len, MLA_HEADS * LANE), F32)] * 2,
        scratch_shapes=[pltpu.VMEM((tq, 1), F32), pltpu.VMEM((tq, 1), F32), pltpu.VMEM((tq, LANE), F32)],
        compiler_params=_cparams(("parallel", "parallel", "arbitrary")),
    )(qc, kc, vv)


def _attn_probs(q, k, v, do, o, lse, masked, tq):
    s = _dot(q, k, "nt") * ATT_SCALE2
    if masked:
        s = jnp.where(_chunk_mask(tq), s, -jnp.inf)
    p = jnp.exp2(s - lse[:, 0:1])
    delta = jnp.sum(do * o, axis=-1, keepdims=True)
    ds = p * (_dot(do, v, "nt") - delta)
    return p, ds


def _attn_bwd_q(qc, kc, vv, o, lse, do, *, name):
    t_len = qc.shape[0]
    tq = min(ATT_TILE, t_len)
    nq = t_len // tq

    def body(q_ref, k_ref, v_ref, o_ref, lse_ref, do_ref, dq_ref, acc_s):
        qi, ki = pl.program_id(1), pl.program_id(2)

        @pl.when(ki == 0)
        def _():
            acc_s[...] = jnp.zeros_like(acc_s)

        def step(masked):
            _, ds = _attn_probs(q_ref[...], k_ref[...], v_ref[...], do_ref[...], o_ref[...], lse_ref[...], masked, tq)
            acc_s[...] += _dot(ds, k_ref[...], "nn")

        @pl.when(ki < qi)
        def _():
            step(False)

        @pl.when(ki == qi)
        def _():
            step(True)
            dq_ref[...] = acc_s[...] * ATT_SCALE

    kv_idx = lambda h, i, k: (jnp.minimum(k, i), h)
    q_idx = lambda h, i, k: (i, h)
    return pl.pallas_call(
        body, grid=(MLA_HEADS, nq, nq), name=name,
        in_specs=[pl.BlockSpec((tq, QK_W), q_idx), pl.BlockSpec((tq, QK_W), kv_idx), pl.BlockSpec((tq, LANE), kv_idx),
                  pl.BlockSpec((tq, LANE), q_idx), pl.BlockSpec((tq, LANE), q_idx), pl.BlockSpec((tq, LANE), q_idx)],
        out_specs=pl.BlockSpec((tq, QK_W), q_idx),
        out_shape=jax.ShapeDtypeStruct((t_len, MLA_HEADS * QK_W), F32),
        scratch_shapes=[pltpu.VMEM((tq, QK_W), F32)],
        compiler_params=_cparams(("parallel", "parallel", "arbitrary")),
    )(qc, kc, vv, o, lse, do)


def _attn_bwd_kv(qc, kc, vv, o, lse, do, *, name):
    t_len = qc.shape[0]
    tq = min(ATT_TILE, t_len)
    nq = t_len // tq

    def body(q_ref, k_ref, v_ref, o_ref, lse_ref, do_ref, dk_ref, dv_ref, dk_s, dv_s):
        ki, qi = pl.program_id(1), pl.program_id(2)

        @pl.when(qi == 0)
        def _():
            dk_s[...] = jnp.zeros_like(dk_s)
            dv_s[...] = jnp.zeros_like(dv_s)

        def step(masked):
            p, ds = _attn_probs(q_ref[...], k_ref[...], v_ref[...], do_ref[...], o_ref[...], lse_ref[...], masked, tq)
            dv_s[...] += _dot(p, do_ref[...], "tn")
            dk_s[...] += _dot(ds, q_ref[...], "tn")

        @pl.when(qi > ki)
        def _():
            step(False)

        @pl.when(qi == ki)
        def _():
            step(True)

        @pl.when(qi == nq - 1)
        def _():
            dk_ref[...] = dk_s[...] * ATT_SCALE
            dv_ref[...] = dv_s[...]

    q_idx = lambda h, k, i: (jnp.maximum(i, k), h)
    k_idx = lambda h, k, i: (k, h)
    return pl.pallas_call(
        body, grid=(MLA_HEADS, nq, nq), name=name,
        in_specs=[pl.BlockSpec((tq, QK_W), q_idx), pl.BlockSpec((tq, QK_W), k_idx), pl.BlockSpec((tq, LANE), k_idx),
                  pl.BlockSpec((tq, LANE), q_idx), pl.BlockSpec((tq, LANE), q_idx), pl.BlockSpec((tq, LANE), q_idx)],
        out_specs=[pl.BlockSpec((tq, QK_W), k_idx), pl.BlockSpec((tq, LANE), k_idx)],
        out_shape=[jax.ShapeDtypeStruct((t_len, MLA_HEADS * QK_W), F32), jax.ShapeDtypeStruct((t_len, MLA_HEADS * LANE), F32)],
        scratch_shapes=[pltpu.VMEM((tq, QK_W), F32), pltpu.VMEM((tq, LANE), F32)],
        compiler_params=_cparams(("parallel", "parallel", "arbitrary")),
    )(qc, kc, vv, o, lse, do)


def _adaln_fn(x, g, shift, scale):
    return ((_rms(x) * g) * (1.0 + scale) + shift,)


def _resid_fn(coef, y, x, gate):
    return (x + coef * gate * y,)


def _rms2_fn(cq, ckv, gq, gkv):
    return _rms(cq) * gq, _rms(ckv) * gkv


@jax.custom_vjp
def _swap_halves(x):
    return jnp.concatenate([x[:, 32:64], x[:, 0:32], x[:, 64:128]], axis=-1)


_swap_halves.defvjp(lambda x: (_swap_halves(x), None), lambda _, g: (_swap_halves(g),))


def _rope_fn(q, kv, misc, pos, invf, sgn):
    ang = pos * invf
    cos, sin = jnp.cos(ang), jnp.sin(ang) * sgn

    def rope(x):
        return x * cos + _swap_halves(x) * sin

    k_pe = rope(jnp.where(_iota2(misc.shape, 1) < MLA_ROPE, misc, 0.0))
    qs, ks = [], []
    for h in range(MLA_HEADS):
        qs += [q[:, h * LANE:(h + 1) * LANE], rope(q[:, (MLA_HEADS + h) * LANE:(MLA_HEADS + h + 1) * LANE])]
        ks += [kv[:, h * LANE:(h + 1) * LANE], k_pe]
    return jnp.concatenate(qs, axis=-1), jnp.concatenate(ks, axis=-1), kv[:, MLA_HEADS * LANE:]


def _ev_out_fn(oa, z, ob, g):
    parts = []
    for h in range(GDN_HEADS):
        hs = slice(h * LANE, (h + 1) * LANE)
        zz = z[:, hs]
        parts.append(_rms(oa[:, hs]) * g * (zz * jax.nn.sigmoid(zz)))
    return (jnp.concatenate(parts + [ob], axis=-1),)


def _od_out_fn(y, z, g):
    yz = y * (z * jax.nn.sigmoid(z))
    gw = SSD_D_INNER // SSD_GROUPS
    return (jnp.concatenate([_rms(yz[:, i * gw:(i + 1) * gw]) for i in range(SSD_GROUPS)], axis=-1) * g,)


def _loss_bwd(x, tgt, g, *, name):
    t_len, d = x.shape
    tm = min(ROW_TILE // 2, t_len)

    def body(x_ref, t_ref, g_ref, loss_ref, dx_ref, dg_ref):
        tgt_v = t_ref[...]

        def f(xv, gv):
            err = _rms(xv) * gv - tgt_v
            return 0.5 * jnp.sum(jnp.mean(err * err, axis=-1, keepdims=True), axis=0, keepdims=True)

        val, vjp = jax.vjp(f, x_ref[...], g_ref[...])
        dx, dg = vjp(jnp.ones((1, 1), F32))
        dx_ref[...] = dx

        @pl.when(pl.program_id(0) == 0)
        def _():
            loss_ref[...] = jnp.zeros_like(loss_ref)
            dg_ref[...] = jnp.zeros_like(dg_ref)

        loss_ref[...] += jnp.broadcast_to(val, loss_ref.shape)
        dg_ref[...] += dg

    row = pl.BlockSpec((tm, d), lambda i: (i, 0))
    return pl.pallas_call(
        body, grid=(t_len // tm,), name=name,
        in_specs=[row, row, pl.BlockSpec((1, d), lambda i: (0, 0))],
        out_specs=[pl.BlockSpec((1, LANE), lambda i: (0, 0)), row, pl.BlockSpec((1, d), lambda i: (0, 0))],
        out_shape=[jax.ShapeDtypeStruct((1, LANE), F32), jax.ShapeDtypeStruct((t_len, d), F32),
                   jax.ShapeDtypeStruct((1, d), F32)],
        compiler_params=_cparams(("arbitrary",)),
    )(x, tgt, g)


def _mesh_pos():
    return lax.axis_index("x"), lax.axis_index("y"), lax.axis_index("c")


def _exchange(xs, scatter, *, name):
    n_arr = len(xs)

    def body(*refs):
        in_refs, out_refs = refs[:n_arr], refs[n_arr:2 * n_arr]
        send_sems, recv_sems, local_sems = refs[2 * n_arr:]
        mx, my, mc = _mesh_pos()
        me = 4 * mx + 2 * my + mc
        started = []
        for a, (in_ref, out_ref) in enumerate(zip(in_refs, out_refs)):
            def src(j, in_ref=in_ref):
                return in_ref.at[j] if scatter else in_ref

            local = pltpu.make_async_copy(src(me), out_ref.at[me], local_sems.at[a])
            local.start()
            started.append((local, None))
            for d in range(1, N_DEV):
                px = 1 - mx if d & 4 else mx
                py = 1 - my if d & 2 else my
                pc = 1 - mc if d & 1 else mc
                peer = 4 * px + 2 * py + pc
                sem = a * (N_DEV - 1) + d - 1
                send = pltpu.make_async_remote_copy(
                    src_ref=src(peer), dst_ref=out_ref.at[me], send_sem=send_sems.at[sem], recv_sem=recv_sems.at[sem],
                    device_id=(px, py, pc), device_id_type=pl.DeviceIdType.MESH)
                send.start()
                recv = pltpu.make_async_remote_copy(
                    src_ref=src(peer), dst_ref=out_ref.at[peer], send_sem=send_sems.at[sem], recv_sem=recv_sems.at[sem],
                    device_id=(px, py, pc), device_id_type=pl.DeviceIdType.MESH)
                started.append((send, recv))
        for first, recv in started:
            if recv is None:
                first.wait()
            else:
                first.wait_send()
                recv.wait_recv()

    blocks = [tuple(x.shape[1:]) if scatter else tuple(x.shape) for x in xs]
    return pl.pallas_call(
        body, name=name,
        in_specs=[pl.BlockSpec(memory_space=pl.ANY)] * n_arr,
        out_specs=[pl.BlockSpec(memory_space=pl.ANY)] * n_arr,
        out_shape=[jax.ShapeDtypeStruct((N_DEV,) + b, x.dtype) for b, x in zip(blocks, xs)],
        scratch_shapes=[pltpu.SemaphoreType.DMA((n_arr * (N_DEV - 1),)), pltpu.SemaphoreType.DMA((n_arr * (N_DEV - 1),)),
                        pltpu.SemaphoreType.DMA((n_arr,))],
        compiler_params=pltpu.CompilerParams(has_side_effects=True),
    )(*xs)


def _peer_of(d, pos):
    mx, my, mc = pos
    px = 1 - mx if d & 4 else mx
    py = 1 - my if d & 2 else my
    pc = 1 - mc if d & 1 else mc
    return (px, py, pc), 4 * px + 2 * py + pc


_HBM = pl.BlockSpec(memory_space=pltpu.HBM)
_SEM = pl.BlockSpec(memory_space=pltpu.SEMAPHORE)


def _exchange_start(xs, scatter, *, name):
    n_arr = len(xs)
    n_sem = n_arr * (N_DEV - 1)

    def body(*refs):
        in_refs, land_refs = refs[:n_arr], refs[n_arr:2 * n_arr]
        send_sems, recv_sems, token = refs[2 * n_arr], refs[2 * n_arr + 1], refs[-1]
        pos = _mesh_pos()
        me = 4 * pos[0] + 2 * pos[1] + pos[2]
        for a in range(n_arr):
            for d in range(1, N_DEV):
                dev, peer = _peer_of(d, pos)
                sem = a * (N_DEV - 1) + d - 1
                pltpu.make_async_remote_copy(
                    src_ref=in_refs[a].at[peer] if scatter else in_refs[a], dst_ref=land_refs[a].at[me],
                    send_sem=send_sems.at[sem], recv_sem=recv_sems.at[sem], device_id=dev,
                    device_id_type=pl.DeviceIdType.MESH).start()
        token[...] = jnp.zeros_like(token)

    blocks = [tuple(x.shape[1:]) if scatter else tuple(x.shape) for x in xs]
    srcs = [pltpu.with_memory_space_constraint(x, pltpu.HBM) for x in xs]
    lands = [pltpu.with_memory_space_constraint(lax.empty((N_DEV,) + b, x.dtype), pltpu.HBM) for b, x in zip(blocks, xs)]
    res = pl.pallas_call(
        body, name=name,
        out_shape=(pltpu.SemaphoreType.DMA((n_sem,)), pltpu.SemaphoreType.DMA((n_sem,)),
                   *[pltpu.HBM(a.shape, a.dtype) for a in srcs + lands], jax.ShapeDtypeStruct((8, LANE), F32)),
        in_specs=[_HBM] * (2 * n_arr),
        out_specs=(_SEM, _SEM, *[_HBM] * (2 * n_arr), pl.BlockSpec(memory_space=pltpu.VMEM)),
        input_output_aliases={i: 2 + i for i in range(2 * n_arr)},
        compiler_params=pltpu.CompilerParams(has_side_effects=pltpu.SideEffectType.DATAFLOW_SIDE_EFFECTING),
    )(*srcs, *lands)
    handle = dict(sems=res[:2], srcs=res[2:2 + n_arr], lands=res[2 + n_arr:2 + 2 * n_arr], scatter=scatter)
    return handle, res[-1][0, 0]


def _exchange_wait(handle, after, me, *, name):
    scatter = handle["scatter"]
    n_arr = len(handle["srcs"])

    def body(*refs):
        in_refs, land_refs = refs[:n_arr], refs[n_arr:2 * n_arr]
        send_sems, recv_sems = refs[2 * n_arr], refs[2 * n_arr + 1]
        pos = _mesh_pos()
        for a in range(n_arr):
            for d in range(1, N_DEV):
                dev, peer = _peer_of(d, pos)
                sem = a * (N_DEV - 1) + d - 1
                copy = pltpu.make_async_remote_copy(
                    src_ref=in_refs[a].at[peer] if scatter else in_refs[a], dst_ref=land_refs[a].at[peer],
                    send_sem=send_sems.at[sem], recv_sem=recv_sems.at[sem], device_id=dev,
                    device_id_type=pl.DeviceIdType.MESH)
                copy.wait_send()
                copy.wait_recv()

    thru = list(handle["srcs"]) + list(handle["lands"])
    res = pl.pallas_call(
        body, name=name,
        out_shape=tuple(pltpu.HBM(a.shape, a.dtype) for a in thru),
        in_specs=[_HBM] * (2 * n_arr) + [_SEM, _SEM, pl.BlockSpec(memory_space=pl.ANY)],
        out_specs=tuple([_HBM] * (2 * n_arr)),
        input_output_aliases={i: i for i in range(2 * n_arr)},
        compiler_params=pltpu.CompilerParams(has_side_effects=pltpu.SideEffectType.DATAFLOW_SIDE_EFFECTING),
    )(*thru, *handle["sems"], after)
    out = []
    for src, land in zip(res[:n_arr], res[n_arr:]):
        own = lax.dynamic_index_in_dim(src, me, axis=0, keepdims=True) if scatter else src[None]
        out.append(lax.dynamic_update_index_in_dim(land, own, me, axis=0))
    return out


def _cols(srcs, rows, plans, out_dtype, *, name):
    n_src = len(srcs)
    rb = _pick(rows, (256, 128, 64, 32, 16, 8))

    def width(pieces):
        return sum(p[1] if p[0] == "z" else p[3] - p[2] for p in pieces)

    def body(*refs):
        ins, outs = refs[:n_src], refs[n_src:]
        loaded = {}
        for o_ref, plan in zip(outs, plans):
            for j, pieces in enumerate(plan):
                vals = []
                for pc in pieces:
                    if pc[0] == "z":
                        vals.append(jnp.zeros((rb, pc[1]), out_dtype))
                    else:
                        si, sj, c0, c1 = pc
                        if (si, sj) not in loaded:
                            loaded[(si, sj)] = ins[si][sj]
                        vals.append(loaded[(si, sj)][:, c0:c1].astype(out_dtype))
                o_ref[j] = vals[0] if len(vals) == 1 else jnp.concatenate(vals, axis=-1)

    for arr, r0 in srcs:
        assert r0 % rb == 0
    return pl.pallas_call(
        body, grid=(rows // rb,), name=name,
        in_specs=[pl.BlockSpec((arr.shape[0], rb, arr.shape[2]), lambda i, r0=r0 // rb: (0, r0 + i, 0)) for arr, r0 in srcs],
        out_specs=[pl.BlockSpec((len(p), rb, width(p[0])), lambda i: (0, i, 0)) for p in plans],
        out_shape=[jax.ShapeDtypeStruct((len(p), rows, width(p[0])), out_dtype) for p in plans],
        compiler_params=_cparams(("parallel",)),
    )(*[arr for arr, _ in srcs])


def _shard_pieces(src, a, b, shard_w):
    out = []
    while a < b:
        s = a // shard_w
        e = min(b, (s + 1) * shard_w)
        out.append((src, s, a - s * shard_w, e - s * shard_w))
        a = e
    return out


def _mapped_pieces(a, b, segs):
    out = []
    for n0, n1, k0 in sorted(segs):
        lo, hi = max(a, n0), min(b, n1)
        if lo < hi:
            out.append((0, 0, k0 + lo - n0, k0 + hi - n0))
    return out


_EV_SEGS = [(0, 1536, EV_QKV), (1536, 2048, EV_Z), (2048, 2056, EV_MISC + MLA_ROPE), (2056, 2440, EV_CQ),
            (2440, 2696, EV_CKV), (2696, 2760, EV_MISC)]
EV_NAT_W, OD_NAT_W = 2760, 5152


PACK_W = 1024


def _adamw(w, gparts, m, v, *, name):
    n_rows, n_cols = w.shape
    n_parts = gparts.shape[0]
    tm = _pick(n_rows, (512, 256, 128, 64, 32, 16, 8))
    while n_parts * tm * n_cols * 4 > 4 * 1024 * 1024 and tm % 16 == 0:
        tm //= 2

    def body(w_ref, g_ref, m_ref, v_ref, go_ref, d_ref, mo_ref, vo_ref):
        g = g_ref[0]
        for j in range(1, n_parts):
            g = g + g_ref[j]
        m_new = ADAM_B1 * m_ref[...] + (1.0 - ADAM_B1) * g
        v_new = ADAM_B2 * v_ref[...] + (1.0 - ADAM_B2) * jnp.square(g)
        m_hat = m_new / (1.0 - ADAM_B1 ** ADAM_STEP)
        v_hat = v_new / (1.0 - ADAM_B2 ** ADAM_STEP)
        go_ref[...] = g
        d_ref[...] = -ADAM_LR * (m_hat / (jnp.sqrt(v_hat) + ADAM_EPS) + ADAM_WD * w_ref[...])
        mo_ref[...] = m_new
        vo_ref[...] = v_new

    row = pl.BlockSpec((tm, n_cols), lambda i: (i, 0))
    return pl.pallas_call(
        body, grid=(n_rows // tm,), name=name,
        in_specs=[row, pl.BlockSpec((n_parts, tm, n_cols), lambda i: (0, i, 0)), row, row],
        out_specs=[row] * 4,
        out_shape=[jax.ShapeDtypeStruct((n_rows, n_cols), F32)] * 4,
        compiler_params=_cparams(("parallel",)),
    )(w, gparts, m, v)


def _adamw_nd(w, gparts, m, v, *, name):
    shape = w.shape
    two = (-1, shape[-1])
    outs = _adamw(w.reshape(two), gparts.reshape((gparts.shape[0],) + (int(np.prod(shape[:-1])), shape[-1])),
                  m.reshape(two), v.reshape(two), name=name)
    return [o.reshape(shape) for o in outs]


def _pack(parts):
    flat = [p.astype(F32).reshape(-1) for p in parts]
    n_pad = -sum(f.shape[0] for f in flat) % (8 * PACK_W)
    return jnp.concatenate(flat + [jnp.zeros((n_pad,), F32)]).reshape(-1, PACK_W)


def _unpack(packed, shapes):
    flat = packed.reshape(-1)
    out, off = [], 0
    for s in shapes:
        n = int(np.prod(s))
        out.append(flat[off:off + n].reshape(tuple(s)))
        off += n
    return out


def _mod_shard(c_all, ada_w, ada_b_shard, *, name):
    n_layer, d, n_col = ada_w.shape

    def body(c_ref, w_ref, b_ref, o_ref):
        cv = c_ref[...]
        o_ref[0] = _dot(cv * jax.nn.sigmoid(cv), w_ref[0], "nn") + b_ref[0]

    return pl.pallas_call(
        body, grid=(n_layer,), name=name,
        in_specs=[pl.BlockSpec((N_DEV, d), lambda l: (0, 0)), pl.BlockSpec((1, d, n_col), lambda l: (l, 0, 0)),
                  pl.BlockSpec((1, 1, n_col), lambda l: (l, 0, 0))],
        out_specs=pl.BlockSpec((1, N_DEV, n_col), lambda l: (l, 0, 0)),
        out_shape=jax.ShapeDtypeStruct((n_layer, N_DEV, n_col), F32),
        compiler_params=_cparams(("parallel",)),
    )(c_all, ada_w, ada_b_shard)


def _ada_w_grad(c_all, dmod_shard, *, name):
    n_layer, _, n_col = dmod_shard.shape
    d = c_all.shape[1]

    def body(c_ref, g_ref, o_ref):
        cv = c_ref[...]
        o_ref[0] = _dot(cv * jax.nn.sigmoid(cv), g_ref[0], "tn", True)

    return pl.pallas_call(
        body, grid=(n_layer,), name=name,
        in_specs=[pl.BlockSpec((N_DEV, d), lambda l: (0, 0)), pl.BlockSpec((1, N_DEV, n_col), lambda l: (l, 0, 0))],
        out_specs=pl.BlockSpec((1, d, n_col), lambda l: (l, 0, 0)),
        out_shape=jax.ShapeDtypeStruct((n_layer, d, n_col), F32),
        compiler_params=_cparams(("parallel",)),
    )(c_all, dmod_shard)


def _uq(w):
    r = w.shape[0]
    rope = jnp.pad(w[:, :, MLA_NOPE:], ((0, 0), (0, 0), (0, LANE - MLA_ROPE)))
    return jnp.concatenate([w[:, :, :MLA_NOPE].reshape(r, -1), rope.reshape(r, -1)], axis=1)


def _uq_back(d):
    r = d.shape[0]
    half = MLA_HEADS * LANE
    return jnp.concatenate([d[:, :half].reshape(r, MLA_HEADS, LANE),
                            d[:, half:].reshape(r, MLA_HEADS, LANE)[:, :, :MLA_ROPE]], axis=-1)


def _ukv(w):
    r = w.shape[0]
    return jnp.concatenate([w[:, :, :MLA_NOPE].reshape(r, -1), w[:, :, MLA_NOPE:].reshape(r, -1)], axis=1)


def _ukv_back(d):
    r = d.shape[0]
    half = MLA_HEADS * LANE
    return jnp.concatenate([d[:, :half].reshape(r, MLA_HEADS, LANE), d[:, half:].reshape(r, MLA_HEADS, LANE)], axis=-1)


def _lane_vec(v):
    return jnp.pad(v.astype(F32), (0, LANE - v.shape[0])).reshape(1, LANE)


def _row(v):
    return v.astype(F32).reshape(1, -1)


def _adaln(x, ln):
    return _rows(_adaln_fn, [x], list(ln), [(D_MODEL, BF16)], name="adaln", also_t=(0,))


def _adaln_bwd(x, ln, dh, dxn):
    (dx,), dln = _rows_vjp(_adaln_fn, [x], [], list(ln), [], [dh], [F32], adds={0: dxn}, name="adaln_bwd")
    return dx, dln


def _resid(coef, y, x, gate):
    return _rows(functools.partial(_resid_fn, coef), [y, x], [gate], [(D_MODEL, F32)], name="resid")[0]


def _gated_fn(coef, y, gate):
    return (coef * gate * y,)


def _resid_bwd(coef, y, gate, dxn):
    (dy,), (dgate,) = _rows_vjp(functools.partial(_gated_fn, coef), [y], [], [gate], [], [dxn], [BF16], name="resid_bwd")
    return dy, dgate


def _ffn_fwd(x, ln, gate, w13, w2):
    h, ht = _adaln(x, ln)
    s, st = _ffn_act(h, w13, name="ffn_act")
    y = _matmul(s, w2, "nn", F32, name="ffn_down")
    return _resid(0.5, y, x, gate), (x, h, ht, st, y)


def _ffn_bwd(saved, dxn, ln, gate, w13, w2):
    x, h, ht, st, y = saved
    dy, dgate = _resid_bwd(0.5, y, gate, dxn)
    dab = _ffn_act_bwd(h, dy, w13, w2, name="ffn_act_bwd")
    dh = _matmul(dab, w13, "nt", F32, name="ffn_dh")
    dw13 = _matmul(ht, dab, "kn", F32, name="ffn_dw13")
    dw2 = _matmul(st, dy, "kn", F32, name="ffn_dw2")
    dx, dln = _adaln_bwd(x, ln, dh, dxn)
    return dx, dw13, dw2, dln, dgate


def _rope_consts():
    half = MLA_ROPE // 2
    inv = (ROPE_THETA ** (-jnp.arange(half, dtype=F32) / half)).astype(F32)
    zeros = jnp.zeros((LANE - MLA_ROPE,), F32)
    invf = jnp.concatenate([inv, inv, zeros]).reshape(1, LANE)
    sgn = jnp.concatenate([-jnp.ones((half,), F32), jnp.ones((half,), F32), zeros]).reshape(1, LANE)
    return invf, sgn


def _even_fwd(x, pos, ln, gate, wt):
    h, ht = _adaln(x, ln)
    p = _matmul(h, wt["w_in"], "nn", F32, name="ev_in")
    act, pre = _conv_fwd(p, EV_QKV, 1536, wt["conv_w"], jnp.zeros((1, 1536), F32), name="ev_conv")
    o_a, states = _gdn_fwd(act, p, wt["alog"], wt["dtb"], name="gdn_fwd")
    cqn, ckvn = _rows(_rms2_fn, [(p, EV_CQ, 384), (p, EV_CKV, 256)], [wt["gq"], wt["gkv"]],
                      [(384, BF16), (256, BF16)], name="mla_rms")
    q = _matmul(cqn, wt["w_uq"], "nn", F32, name="mla_uq")
    kv = _matmul(ckvn, wt["w_ukv"], "nn", F32, name="mla_ukv")
    invf, sgn = _rope_consts()
    qc, kc, vv = _rows(_rope_fn, [q, kv, (p, EV_MISC, LANE), pos], [invf, sgn],
                       [(1024, BF16), (1024, BF16), (512, BF16)], name="mla_rope", tm=ROW_TILE // 2)
    o_b, lse = _attn_fwd(qc, kc, vv, name="attn_fwd")
    o, ot = _rows(_ev_out_fn, [o_a, (p, EV_Z, 512), o_b], [wt["gdn_g"]], [(1024, BF16)], name="ev_out", also_t=(0,))
    y = _matmul(o, wt["w_out"], "nn", F32, name="ev_wout")
    return _resid(1.0, y, x, gate), (x, ht, p, act, pre, states, cqn, ckvn, q, kv, qc, kc, vv, o_a, o_b, lse, ot, y)


def _cat_fn(*parts):
    return (jnp.concatenate(parts, axis=-1),)


def _ev_dp_fn(dx0, dx1, dx2, dcq, dm_r, dm_g, dz, dckv):
    return (jnp.concatenate([dx0, dx1, dx2, dcq, dm_r + dm_g, dz, dckv], axis=-1),)


def _even_bwd(saved, dxn, pos, ln, gate, wt):
    x, ht, p, act, pre, states, cqn, ckvn, q, kv, qc, kc, vv, o_a, o_b, lse, ot, y = saved
    g = {}
    dy, g["gate"] = _resid_bwd(1.0, y, gate, dxn)
    do = _matmul(dy, wt["w_out"], "nt", F32, name="ev_dwout_x")
    g["w_out"] = _matmul(ot, dy, "kn", F32, name="ev_dwout_w")
    (d_oa, dz, d_ob), (g["gdn_g"],) = _rows_vjp(_ev_out_fn, [o_a, (p, EV_Z, 512), o_b], [], [wt["gdn_g"]], [], [do],
                                                [F32, F32, F32], name="ev_out_bwd")
    dqc = _attn_bwd_q(qc, kc, vv, o_b, lse, d_ob, name="attn_bwd_q")
    dkc, dvv = _attn_bwd_kv(qc, kc, vv, o_b, lse, d_ob, name="attn_bwd_kv")
    invf, sgn = _rope_consts()
    (dq, dkv, dm_r), _ = _rows_vjp(_rope_fn, [q, kv, (p, EV_MISC, LANE)], [pos], [], [invf, sgn], [dqc, dkc, dvv],
                                   [BF16, BF16, F32], name="mla_rope_bwd", tm=ROW_TILE // 4)
    dcqn = _matmul(dq, wt["w_uq"], "nt", F32, name="mla_duq_x")
    g["w_uq"] = _matmul(cqn, dq, "tn", F32, name="mla_duq_w")
    dckvn = _matmul(dkv, wt["w_ukv"], "nt", F32, name="mla_dukv_x")
    g["w_ukv"] = _matmul(ckvn, dkv, "tn", F32, name="mla_dukv_w")
    (dcq, dckv), (g["gq"], g["gkv"]) = _rows_vjp(_rms2_fn, [(p, EV_CQ, 384), (p, EV_CKV, 256)], [],
                                                 [wt["gq"], wt["gkv"]], [], [dcqn, dckvn], [F32, F32], name="mla_rms_bwd")
    dq_g, dk_g, dv_g, dm_g, g["alog"], g["dtb"] = _gdn_bwd(act, p, wt["alog"], wt["dtb"], states, d_oa, name="gdn_bwd")
    dxs, dws = [], []
    for j, d in enumerate((dq_g, dk_g, dv_g)):
        dxj, dwj, _ = _conv_bwd(d, pre, 512 * j, p, EV_QKV + 512 * j, wt["conv_w"], name="ev_conv_bwd")
        dxs.append(dxj)
        dws.append(dwj)
    g["conv_w"] = jnp.concatenate(dws, axis=1)
    (dp,) = _rows(_ev_dp_fn, dxs + [dcq, dm_r, dm_g, dz, dckv], [],
                  [(EV_W, BF16)], name="ev_dp", tm=ROW_TILE // 2)
    dh = _matmul(dp, wt["w_in"], "nt", F32, name="ev_din_x")
    g["w_in"] = _matmul(ht, dp, "kn", F32, name="ev_din_w")
    dx, g["ln"] = _adaln_bwd(x, ln, dh, dxn)
    return dx, g


def _odd_fwd(x, ln, gate, wt):
    h, ht = _adaln(x, ln)
    p = _matmul(h, wt["w_in"], "nn", F32, name="od_in")
    act, pre = _conv_fwd(p, OD_XBC, 3072, wt["conv_w"], wt["conv_b"], name="od_conv")
    ys, states = _ssd_fwd(act, p, wt["alog"], wt["dtb"], wt["dsk"], name="ssd_fwd")
    o, ot = _rows(_od_out_fn, [ys, (p, OD_Z, 2048)], [wt["norm_g"]], [(SSD_D_INNER, BF16)], name="od_out",
                  tm=ROW_TILE // 2, also_t=(0,))
    y = _matmul(o, wt["w_out"], "nn", F32, name="od_wout")
    return _resid(1.0, y, x, gate), (x, ht, p, act, pre, states, ys, ot, y)


def _od_dp_fn(dz, dxx, dxb, dxc, ddt):
    return (jnp.concatenate([dz, dxx, dxb, dxc, ddt, jnp.zeros_like(ddt)], axis=-1),)


def _odd_bwd(saved, dxn, ln, gate, wt):
    x, ht, p, act, pre, states, ys, ot, y = saved
    g = {}
    dy, g["gate"] = _resid_bwd(1.0, y, gate, dxn)
    do = _matmul(dy, wt["w_out"], "nt", F32, name="od_dwout_x")
    g["w_out"] = _matmul(ot, dy, "kn", F32, name="od_dwout_w")
    (dys, dz), (g["norm_g"],) = _rows_vjp(_od_out_fn, [ys, (p, OD_Z, 2048)], [], [wt["norm_g"]], [], [do], [F32, F32],
                                          name="od_out_bwd", tm=ROW_TILE // 4)
    dxs, dbm, dcm, ddt, g["alog"], g["dtb"], g["dsk"] = _ssd_bwd(act, p, wt["alog"], wt["dtb"], wt["dsk"], states, dys,
                                                                 name="ssd_bwd")
    dins, dws, dbs = [], [], []
    for d, c0 in ((dxs, 0), (dbm, 2048), (dcm, 2560)):
        dxj, dwj, dbj = _conv_bwd(d, pre, c0, p, OD_XBC + c0, wt["conv_w"], name="od_conv_bwd")
        dins.append(dxj)
        dws.append(dwj)
        dbs.append(dbj)
    g["conv_w"] = jnp.concatenate(dws, axis=1)
    g["conv_b"] = jnp.concatenate(dbs, axis=1)
    (dp,) = _rows(_od_dp_fn, [dz] + dins + [ddt], [], [(OD_W, BF16)], name="od_dp", tm=ROW_TILE // 4)
    dh = _matmul(dp, wt["w_in"], "nt", F32, name="od_din_x")
    g["w_in"] = _matmul(ht, dp, "kn", F32, name="od_din_w")
    dx, g["ln"] = _adaln_bwd(x, ln, dh, dxn)
    return dx, g


def _local_step(x, tgt, pos, mod, final_g, layer_weights, layer_done):
    mod = mod.reshape(DEPTH, 3, 3, 1, D_MODEL)
    wts = []

    def ln_of(l, i):
        return (_row(wts[l]["norm_g"][i]), mod[l, i, 0], mod[l, i, 1])

    def mixer_w(l):
        p = wts[l]
        if l % 2 == 0:
            return dict(w_in=p["w_in_k"], conv_w=p["gdn_conv_w"].astype(F32),
                        alog=_lane_vec(p["gdn_A_log"]), dtb=_lane_vec(p["gdn_dt_bias"]),
                        gdn_g=_row(p["gdn_norm_g"]), gq=_row(p["mla_q_norm_g"]), gkv=_row(p["mla_kv_norm_g"]),
                        w_uq=_uq(p["mla_w_uq"]), w_ukv=_ukv(p["mla_w_ukv"]), w_out=p["ev_w_out"])
        return dict(w_in=p["w_in_k"], conv_w=p["ssd_conv_w"].astype(F32),
                    conv_b=_row(p["ssd_conv_b"]), alog=_lane_vec(p["ssd_A_log"]),
                    dtb=_lane_vec(p["ssd_dt_bias"]), dsk=_lane_vec(p["ssd_D"]),
                    norm_g=_row(p["ssd_norm_g"]), w_out=p["ssd_w_out"])

    saved = []
    for l in range(DEPTH):
        wts.append(layer_weights(l, x))
        x, s0 = _ffn_fwd(x, ln_of(l, 0), mod[l, 0, 2], wts[l]["w13"][0], wts[l]["w2"][0])
        if l % 2 == 0:
            x, s1 = _even_fwd(x, pos, ln_of(l, 1), mod[l, 1, 2], mixer_w(l))
        else:
            x, s1 = _odd_fwd(x, ln_of(l, 1), mod[l, 1, 2], mixer_w(l))
        x, s2 = _ffn_fwd(x, ln_of(l, 2), mod[l, 2, 2], wts[l]["w13"][1], wts[l]["w2"][1])
        saved.append((s0, s1, s2))

    loss, dx, d_final_g = _loss_bwd(x, tgt, _row(final_g), name="loss")

    repl = {k: [None] * (DEPTH // 2) for k in ("gdn_A_log", "gdn_dt_bias", "gdn_norm_g", "mla_q_norm_g", "mla_kv_norm_g",
                                                "ssd_A_log", "ssd_dt_bias", "ssd_D")}
    dmod = [None] * DEPTH
    token = None
    for l in reversed(range(DEPTH)):
        s0, s1, s2 = saved[l]
        e = l // 2
        gl = {"w13": [None] * 2, "w2": [None] * 2}
        dg, dsh, dsc, dgt = [None] * 3, [None] * 3, [None] * 3, [None] * 3
        gate2 = mod[l, 2, 2] if token is None else mod[l, 2, 2] + token
        dx, gl["w13"][1], gl["w2"][1], (dg[2], dsh[2], dsc[2]), dgt[2] = _ffn_bwd(
            s2, dx, ln_of(l, 2), gate2, wts[l]["w13"][1], wts[l]["w2"][1])
        if l % 2 == 0:
            dx, g = _even_bwd(s1, dx, pos, ln_of(l, 1), mod[l, 1, 2], mixer_w(l))
            gl.update(w_in_k=g["w_in"], gdn_conv_w=g["conv_w"], mla_w_uq=_uq_back(g["w_uq"]),
                      mla_w_ukv=_ukv_back(g["w_ukv"]), ev_w_out=g["w_out"])
            repl["gdn_A_log"][e] = g["alog"][0, :GDN_HEADS]
            repl["gdn_dt_bias"][e] = g["dtb"][0, :GDN_HEADS]
            repl["gdn_norm_g"][e] = g["gdn_g"][0]
            repl["mla_q_norm_g"][e] = g["gq"][0]
            repl["mla_kv_norm_g"][e] = g["gkv"][0]
        else:
            dx, g = _odd_bwd(s1, dx, ln_of(l, 1), mod[l, 1, 2], mixer_w(l))
            gl.update(w_in_k=g["w_in"], ssd_conv_w=g["conv_w"], ssd_conv_b=g["conv_b"][0], ssd_norm_g=g["norm_g"][0],
                      ssd_w_out=g["w_out"])
            repl["ssd_A_log"][e] = g["alog"][0, :SSD_HEADS]
            repl["ssd_dt_bias"][e] = g["dtb"][0, :SSD_HEADS]
            repl["ssd_D"][e] = g["dsk"][0, :SSD_HEADS]
        dg[1], dsh[1], dsc[1] = g["ln"]
        dgt[1] = g["gate"]
        dx, gl["w13"][0], gl["w2"][0], (dg[0], dsh[0], dsc[0]), dgt[0] = _ffn_bwd(
            s0, dx, ln_of(l, 0), mod[l, 0, 2], wts[l]["w13"][0], wts[l]["w2"][0])
        gl["norm_g"] = jnp.concatenate(dg, axis=0)
        dmod[l] = jnp.concatenate([jnp.concatenate([dsh[i], dsc[i], dgt[i]], axis=1) for i in range(3)], axis=1)[0]
        token = layer_done(l, gl, dx)

    grads = {k: jnp.stack(v) for k, v in repl.items()}
    grads["final_g"] = d_final_g[0]
    return loss, dx, grads, jnp.stack(dmod)


_WEIGHTS = ("ada_w", "ada_b", "norm_g", "ffn_w1", "ffn_w3", "ffn_w2", "ev_w_in", "gdn_conv_w", "gdn_A_log", "gdn_dt_bias",
            "gdn_norm_g", "mla_q_norm_g", "mla_w_uq", "mla_kv_norm_g", "mla_w_ukv", "ev_w_out", "ssd_w_in", "ssd_conv_w",
            "ssd_conv_b", "ssd_A_log", "ssd_dt_bias", "ssd_D", "ssd_norm_g", "ssd_w_out", "final_g")
_BIG = {"ffn_w1": 3, "ffn_w3": 3, "ffn_w2": 2, "ev_w_in": 2, "mla_w_uq": 1, "mla_w_ukv": 1, "ev_w_out": 1, "ssd_w_in": 2,
        "ssd_w_out": 1}
_SMALL = {"norm_g": 2, "gdn_conv_w": 2, "ssd_conv_w": 2, "ssd_conv_b": 1, "ssd_norm_g": 1}
_REPL = ("ada_b", "gdn_A_log", "gdn_dt_bias", "gdn_norm_g", "mla_q_norm_g", "mla_kv_norm_g", "ssd_A_log", "ssd_dt_bias",
         "ssd_D", "final_g")


def _join(pieces, axis):
    moved = jnp.moveaxis(pieces, 0, axis)
    shape = moved.shape
    return moved.reshape(shape[:axis] + (shape[axis] * shape[axis + 1],) + shape[axis + 2:])


def _split(full, axis):
    shape = full.shape
    return jnp.moveaxis(full.reshape(shape[:axis] + (N_DEV, shape[axis] // N_DEV) + shape[axis + 1:]), axis, 0)


def kernel(x, c, positions, ada_w, ada_b, norm_g, ffn_w1, ffn_w3, ffn_w2, ev_w_in, gdn_conv_w, gdn_A_log, gdn_dt_bias, gdn_norm_g, mla_q_norm_g, mla_w_uq, mla_kv_norm_g, mla_w_ukv, ev_w_out, ssd_w_in, ssd_conv_w, ssd_conv_b, ssd_A_log, ssd_dt_bias, ssd_D, ssd_norm_g, ssd_w_out, final_g, loss_target, m_ada_w, m_ada_b, m_norm_g, m_ffn_w1, m_ffn_w3, m_ffn_w2, m_ev_w_in, m_gdn_conv_w, m_gdn_A_log, m_gdn_dt_bias, m_gdn_norm_g, m_mla_q_norm_g, m_mla_w_uq, m_mla_kv_norm_g, m_mla_w_ukv, m_ev_w_out, m_ssd_w_in, m_ssd_conv_w, m_ssd_conv_b, m_ssd_A_log, m_ssd_dt_bias, m_ssd_D, m_ssd_norm_g, m_ssd_w_out, m_final_g, v_ada_w, v_ada_b, v_norm_g, v_ffn_w1, v_ffn_w3, v_ffn_w2, v_ev_w_in, v_gdn_conv_w, v_gdn_A_log, v_gdn_dt_bias, v_gdn_norm_g, v_mla_q_norm_g, v_mla_w_uq, v_mla_kv_norm_g, v_mla_w_ukv, v_ev_w_out, v_ssd_w_in, v_ssd_conv_w, v_ssd_conv_b, v_ssd_A_log, v_ssd_dt_bias, v_ssd_D, v_ssd_norm_g, v_ssd_w_out, v_final_g):
    a = dict(locals())
    w = {n: a[n] for n in _WEIGHTS}
    m = {n: a["m_" + n] for n in _WEIGHTS}
    v = {n: a["v_" + n] for n in _WEIGHTS}
    mx, my, mc = _mesh_pos()
    me = 4 * mx + 2 * my + mc
    t_len = x.shape[1]
    shards = range(N_DEV)

    small_names, big_names = list(_SMALL), list(_BIG)
    axis_of = {**_SMALL, **_BIG}
    small_g = _exchange([c] + [w[n] for n in small_names], False, name="gather_small")
    c_all = small_g[0].reshape(N_DEV, D_MODEL)
    fw = {n: _join(p, _SMALL[n]) for n, p in zip(small_names, small_g[1:])}
    first_names = [n for n in big_names if not n.startswith("ssd")]
    first_g = dict(zip(first_names, _exchange([w[n][:1].astype(BF16) for n in first_names], False, name="gather_first")))
    rest_src = {n: (w[n] if n.startswith("ssd") else w[n][1:]).astype(BF16) for n in big_names}
    rest_handle, rest_token = _exchange_start([rest_src[n] for n in big_names], False, name="gather_rest_start")
    rest_g = {}

    fs, es, os_ = ffn_w1.shape[3], ev_w_in.shape[2], ssd_w_in.shape[2]
    half = range(N_DEV // 2)
    plan13 = [[[(0, s, 0, fs) for s in half] + [(1, s, 0, fs) for s in half]
               + [(0, s + 4, 0, fs) for s in half] + [(1, s + 4, 0, fs) for s in half]]]
    plan_ev, k_at = [], 0
    for n0, n1, k0 in sorted(_EV_SEGS, key=lambda seg: seg[2]):
        if k0 > k_at:
            plan_ev.append(("z", k0 - k_at))
        plan_ev += _shard_pieces(0, n0, n1, es)
        k_at = k0 + n1 - n0
    assert k_at == EV_W and es * N_DEV == EV_NAT_W and os_ * N_DEV == OD_NAT_W
    plan_od = [[_shard_pieces(0, 0, OD_NAT_W, os_) + [("z", OD_W - OD_NAT_W)]]]

    def layer_weights(l, x_in):
        if l == 1:
            rest_g.update(zip(big_names, _exchange_wait(rest_handle, x_in, me, name="gather_rest_wait")))
        e = l // 2
        src = first_g if l == 0 else rest_g
        i = 0 if l == 0 else l - 1
        ie = 0 if (l == 0 or l % 2) else e - 1
        g1 = src["ffn_w1"].reshape(N_DEV, -1, fs)
        g3 = src["ffn_w3"].reshape(N_DEV, -1, fs)
        p = {"norm_g": fw["norm_g"][l],
             "w13": [_cols([(g1, (2 * i + j) * D_MODEL), (g3, (2 * i + j) * D_MODEL)], D_MODEL, plan13, BF16,
                           name="join_w13")[0][0] for j in range(2)],
             "w2": [src["ffn_w2"][:, i, j].reshape(D_FF, D_MODEL) for j in range(2)]}
        if l % 2 == 0:
            p["w_in_k"] = _cols([(src["ev_w_in"].reshape(N_DEV, -1, es), ie * D_MODEL)], D_MODEL, [[plan_ev]], BF16,
                                name="join_ev_in")[0][0]
            for n in ("mla_w_uq", "mla_w_ukv", "ev_w_out"):
                p[n] = _join(src[n][:, ie], axis_of[n] - 1)
            p["gdn_conv_w"] = fw["gdn_conv_w"][e]
            for n in ("gdn_A_log", "gdn_dt_bias", "gdn_norm_g", "mla_q_norm_g", "mla_kv_norm_g"):
                p[n] = w[n][e]
        else:
            p["w_in_k"] = _cols([(rest_g["ssd_w_in"].reshape(N_DEV, -1, os_), e * D_MODEL)], D_MODEL, plan_od, BF16,
                                name="join_od_in")[0][0]
            p["ssd_w_out"] = _join(rest_g["ssd_w_out"][:, e], axis_of["ssd_w_out"] - 1)
            for n in ("ssd_conv_w", "ssd_conv_b", "ssd_norm_g"):
                p[n] = fw[n][e]
            for n in ("ssd_A_log", "ssd_dt_bias", "ssd_D"):
                p[n] = w[n][e]
        return p

    def w13_cols(s, third):
        k0 = (s % 4) * fs + (2 * FF_HALF if s >= 4 else 0) + (FF_HALF if third else 0)
        return [(0, 0, k0, k0 + fs)]

    sent = {}

    def layer_done(l, gl, dx_l):
        d13 = [_cols([(gl["w13"][j][None], 0)], D_MODEL, [[w13_cols(s, False) for s in shards],
                                                          [w13_cols(s, True) for s in shards]], F32, name="split_w13")
               for j in range(2)]
        pieces = {"norm_g": _split(gl["norm_g"][None], 2),
                  "ffn_w1": jnp.stack([d13[j][0] for j in range(2)], axis=1)[:, None],
                  "ffn_w3": jnp.stack([d13[j][1] for j in range(2)], axis=1)[:, None],
                  "ffn_w2": jnp.stack([gl["w2"][j].reshape(N_DEV, -1, D_MODEL) for j in range(2)], axis=1)[:, None]}
        if l % 2 == 0:
            pieces["ev_w_in"] = _cols([(gl["w_in_k"][None], 0)], D_MODEL,
                                      [[_mapped_pieces(s * es, (s + 1) * es, _EV_SEGS) for s in shards]], F32,
                                      name="split_ev_in")[0][:, None]
            for n in ("gdn_conv_w", "mla_w_uq", "mla_w_ukv", "ev_w_out"):
                pieces[n] = _split(gl[n][None], axis_of[n])
        else:
            pieces["ssd_w_in"] = _cols([(gl["w_in_k"][None], 0)], D_MODEL,
                                       [[[(0, 0, s * os_, (s + 1) * os_)] for s in shards]], F32,
                                       name="split_od_in")[0][:, None]
            for n in ("ssd_conv_w", "ssd_conv_b", "ssd_norm_g", "ssd_w_out"):
                pieces[n] = _split(gl[n][None], axis_of[n])
        names = list(pieces)
        if l == 0:
            sent[l] = dict(zip(names, _exchange([pieces[n] for n in names], True, name="scatter_layer0")))
            return None
        handle, token = _exchange_start([pieces[n] for n in names], True, name=f"scatter_start_{l}")
        sent[l] = (names, handle)
        return token

    n_col = ada_w.shape[2]
    ada_b_shard = lax.dynamic_slice(ada_b, (0, me * n_col), (DEPTH, n_col)).reshape(DEPTH, 1, n_col)
    mod_all = _exchange([_mod_shard(c_all, ada_w, ada_b_shard, name="mod")], False, name="gather_mod")[0]
    mod_me = lax.dynamic_index_in_dim(mod_all, me, axis=2, keepdims=False)
    mod = jnp.transpose(mod_me, (1, 0, 2)).reshape(DEPTH, N_DEV * n_col) + rest_token

    pos = positions.astype(F32).reshape(t_len, 1)
    loss, dx, grads, dmod = _local_step(x[0], loss_target[0], pos, mod, final_g, layer_weights, layer_done)
    for l in range(1, DEPTH):
        names, handle = sent[l]
        sent[l] = dict(zip(names, _exchange_wait(handle, dx, me, name=f"scatter_wait_{l}")))

    repl_shapes = [w[n].shape for n in _REPL] + [(1,)]
    parts8 = _exchange([_pack([dmod] + [grads[n] for n in _REPL[1:]] + [loss[0, :1]])], False, name="gather_repl")[0]
    zero = jnp.zeros((1,), F32)
    r_grad, r_delta, r_m, r_v = [
        _unpack(o, repl_shapes) for o in _adamw(_pack([w[n] for n in _REPL] + [zero]), parts8,
                                                _pack([m[n] for n in _REPL] + [zero]),
                                                _pack([v[n] for n in _REPL] + [zero]), name="adamw_repl")]
    out = {"grad": {}, "delta": {}, "m": {}, "v": {}}
    for i, n in enumerate(_REPL):
        out["grad"][n], out["delta"][n], out["m"][n], out["v"][n] = r_grad[i], r_delta[i], r_m[i], r_v[i]
    loss_total = r_grad[-1].reshape(())

    dmod_all = parts8[:, :dmod.size // PACK_W].reshape((N_DEV,) + dmod.shape)
    dmod_cols = jnp.transpose(lax.dynamic_slice_in_dim(dmod_all, me * n_col, n_col, axis=2), (1, 0, 2))
    g_ada = _ada_w_grad(c_all, dmod_cols, name="ada_w_grad")
    for k, o in zip(("grad", "delta", "m", "v"), _adamw_nd(ada_w, g_ada[None], m["ada_w"], v["ada_w"], name="adamw_ada")):
        out[k]["ada_w"] = o

    for n in small_names + big_names:
        if n in ("norm_g", "ffn_w1", "ffn_w3", "ffn_w2"):
            layers = range(DEPTH)
        else:
            layers = range(1, DEPTH, 2) if n.startswith("ssd") else range(0, DEPTH, 2)
        g8 = jnp.concatenate([sent[l][n] for l in layers], axis=1)
        for k, o in zip(("grad", "delta", "m", "v"), _adamw_nd(w[n], g8, m[n], v[n], name="adamw_" + n)):
            out[k][n] = o

    return (loss_total, dx.reshape(x.shape), *[out["grad"][n] for n in _WEIGHTS], *[out["delta"][n] for n in _WEIGHTS],
            *[out["m"][n] for n in _WEIGHTS], *[out["v"][n] for n in _WEIGHTS])
```

```python
import functools
import math

import numpy as np
import jax
import jax.numpy as jnp
from jax import lax
from jax.experimental import pallas as pl
from jax.experimental.pallas import tpu as pltpu

F32 = jnp.float32
BF16 = jnp.bfloat16
HI = lax.Precision.HIGHEST

D_MODEL = 1024
DEPTH = 4
CHUNK = 64
NORM_EPS = 1e-6
CONV_K = 4
D_FF = 2816
GDN_HEADS = 4
GDN_DK = 128
MLA_HEADS = 4
MLA_NOPE = 128
MLA_ROPE = 64
ROPE_THETA = 10000.0
SSD_HEADS = 32
SSD_HEADDIM = 64
SSD_GROUPS = 4
SSD_STATE = 128
SSD_D_INNER = 2048
N_DEV = 8

ADAM_LR = 0.001
ADAM_B1 = 0.9
ADAM_B2 = 0.999
ADAM_EPS = 1e-08
ADAM_WD = 0.01
ADAM_STEP = 10

V7X_VMEM_LIMIT = 56 * 1024 * 1024
ROW_TILE = 512
SEQ_TILE = 128
ATT_TILE = 1024
MM_RESIDENT_BYTES = 12 * 1024 * 1024
FF_HALF = D_FF // 2
LANE = 128

EV_QKV, EV_CQ, EV_MISC, EV_Z, EV_CKV, EV_W = 0, 1536, 1920, 2048, 2560, 2816
OD_Z, OD_XBC, OD_DT, OD_W = 0, 2048, 5120, 5376


def _cparams(sem=None):
    return pltpu.CompilerParams(dimension_semantics=sem, vmem_limit_bytes=V7X_VMEM_LIMIT)


def _pick(n, cands):
    for c in cands:
        if n % c == 0:
            return c
    return n


_DN = {"nn": (((1,), (0,)), ((), ())), "nt": (((1,), (1,)), ((), ())), "tn": (((0,), (0,)), ((), ()))}


F32_3PASS = 2


def _dot(a, b, mode, hi=False):
    if hi:
        prec = lax.Precision.HIGH if hi == F32_3PASS else HI
        return lax.dot_general(a.astype(F32), b.astype(F32), _DN[mode], precision=prec, preferred_element_type=F32)
    return lax.dot_general(a.astype(BF16), b.astype(BF16), _DN[mode], preferred_element_type=F32)


@functools.partial(jax.custom_vjp, nondiff_argnums=(2, 3))
def _mm(a, b, mode, hi):
    return _dot(a, b, mode, hi)


def _mm_fwd(a, b, mode, hi):
    return _dot(a, b, mode, hi), (a, b)


def _mm_bwd(mode, hi, res, g):
    a, b = res
    if mode == "nn":
        return _dot(g, b, "nt", hi), _dot(a, g, "tn", hi)
    if mode == "nt":
        return _dot(g, b, "nn", hi), _dot(g, a, "tn", hi)
    return _dot(b, g, "nt", hi), _dot(a, g, "nn", hi)


_mm.defvjp(_mm_fwd, _mm_bwd)


def _iota2(shape, dim):
    return lax.broadcasted_iota(jnp.int32, shape, dim)


def _row_spec(a, tm):
    if isinstance(a, tuple):
        arr, c0, w = a
        assert c0 % w == 0
        cb = c0 // w
        return arr, pl.BlockSpec((tm, w), lambda i, cb=cb: (i, cb))
    return a, pl.BlockSpec((tm, a.shape[1]), lambda i: (i, 0))


def _full_spec(b):
    return pl.BlockSpec(b.shape, lambda i: (0,) * b.ndim)


def _rows(fn, tiled, bcast, outs, *, name, tm=ROW_TILE, also_t=()):
    arrs, specs = zip(*[_row_spec(a, 0) for a in tiled])
    t_len = arrs[0].shape[0]
    tm = min(tm, t_len)
    arrs, specs = zip(*[_row_spec(a, tm) for a in tiled])
    nt, nb, no = len(tiled), len(bcast), len(outs)

    def body(*refs):
        ins = [r[...].astype(F32) for r in refs[:nt]] + [r[...] for r in refs[nt:nt + nb]]
        res = fn(*ins)
        for r, v in zip(refs[nt + nb:nt + nb + no], res):
            r[...] = v.astype(r.dtype)
        for r, k in zip(refs[nt + nb + no:], also_t):
            r[...] = res[k].T.astype(r.dtype)

    return pl.pallas_call(
        body, grid=(t_len // tm,), name=name,
        in_specs=list(specs) + [_full_spec(b) for b in bcast],
        out_specs=[pl.BlockSpec((tm, c), lambda i: (i, 0)) for c, _ in outs]
        + [pl.BlockSpec((outs[k][0], tm), lambda i: (0, i)) for k in also_t],
        out_shape=[jax.ShapeDtypeStruct((t_len, c), dt) for c, dt in outs]
        + [jax.ShapeDtypeStruct((outs[k][0], t_len), outs[k][1]) for k in also_t],
        compiler_params=_cparams(("parallel",)),
    )(*arrs, *bcast)


def _rows_vjp(fn, tiled, consts, bcast, bconsts, douts, grads, *, name, adds=None, tm=ROW_TILE // 2):
    adds = adds or {}
    t_arrs, t_specs = zip(*[_row_spec(a, 0) for a in tiled])
    t_len = t_arrs[0].shape[0]
    tm = min(tm, t_len)
    rows_in = list(tiled) + list(consts) + list(douts) + [adds[k] for k in sorted(adds)]
    arrs, specs = zip(*[_row_spec(a, tm) for a in rows_in])
    nt, nc, nb, nbc, nd, na = len(tiled), len(consts), len(bcast), len(bconsts), len(douts), len(adds)
    add_pos = {k: j for j, k in enumerate(sorted(adds))}
    want = [j for j, g in enumerate(grads) if g is not None]

    def body(*refs):
        p = 0
        t = [r[...].astype(F32) for r in refs[p:p + nt]]; p += nt
        c = [r[...].astype(F32) for r in refs[p:p + nc]]; p += nc
        d = [r[...].astype(F32) for r in refs[p:p + nd]]; p += nd
        a = [r[...].astype(F32) for r in refs[p:p + na]]; p += na
        b = [r[...] for r in refs[p:p + nb]]; p += nb
        bc = [r[...] for r in refs[p:p + nbc]]; p += nbc
        g_refs = refs[p:p + len(want)]; p += len(want)
        gb_refs = refs[p:p + nb]

        def f(*args):
            return fn(*args[:nt], *c, *args[nt:], *bc)

        _, vjp = jax.vjp(f, *t, *b)
        g = vjp(tuple(d))
        for r, j in zip(g_refs, want):
            val = g[j]
            if j in add_pos:
                val = val + a[add_pos[j]]
            r[...] = val.astype(r.dtype)

        @pl.when(pl.program_id(0) == 0)
        def _():
            for r in gb_refs:
                r[...] = jnp.zeros_like(r)

        for r, val in zip(gb_refs, g[nt:]):
            r[...] += val

    def width(a):
        return a[2] if isinstance(a, tuple) else a.shape[1]

    res = pl.pallas_call(
        body, grid=(t_len // tm,), name=name,
        in_specs=list(specs) + [_full_spec(b) for b in list(bcast) + list(bconsts)],
        out_specs=[pl.BlockSpec((tm, width(tiled[j])), lambda i: (i, 0)) for j in want] + [_full_spec(b) for b in bcast],
        out_shape=[jax.ShapeDtypeStruct((t_len, width(tiled[j])), grads[j]) for j in want]
        + [jax.ShapeDtypeStruct(b.shape, F32) for b in bcast],
        compiler_params=_cparams(("arbitrary",)),
    )(*arrs, *bcast, *bconsts)
    tg = [None] * nt
    for r, j in zip(res[:len(want)], want):
        tg[j] = r
    return tg, list(res[len(want):])


def _matmul(a, b, mode, out_dtype, *, name):
    if mode in ("tn", "kn"):
        assert out_dtype == F32
        k_len, m_len = a.shape if mode == "tn" else a.shape[::-1]
        n_len = b.shape[1]
        tm, tn = m_len, n_len
        while tm * tn * 4 > MM_RESIDENT_BYTES and tn % (2 * LANE) == 0:
            tn //= 2
        tk = _pick(k_len, (512, 256, 128))
        if mode == "kn" and k_len % 1024 == 0 and 4 * 1024 * (tm + tn) + 8 * tm * tn <= 44 * 1024 * 1024:
            tk = 1024
    else:
        m_len, k_len = a.shape
        n_len = b.shape[1] if mode == "nn" else b.shape[0]
        tk, tn = k_len, n_len
        while tk * tn * 2 > MM_RESIDENT_BYTES and tn % (2 * LANE) == 0:
            tn //= 2
        tm = _pick(m_len, (512, 256, 128))
        while tm * max(4 * tn, 2 * tk) > MM_RESIDENT_BYTES // 2 and tm % 256 == 0:
            tm //= 2
    nk = k_len // tk
    if mode == "nn":
        a_spec = pl.BlockSpec((tm, tk), lambda j, i, k: (i, k))
        b_spec = pl.BlockSpec((tk, tn), lambda j, i, k: (k, j))
    elif mode == "nt":
        a_spec = pl.BlockSpec((tm, tk), lambda j, i, k: (i, k))
        b_spec = pl.BlockSpec((tn, tk), lambda j, i, k: (j, k))
    elif mode == "kn":
        a_spec = pl.BlockSpec((tm, tk), lambda j, i, k: (i, k))
        b_spec = pl.BlockSpec((tk, tn), lambda j, i, k: (k, j))
    else:
        a_spec = pl.BlockSpec((tk, tm), lambda j, i, k: (k, i))
        b_spec = pl.BlockSpec((tk, tn), lambda j, i, k: (k, j))

    def body(a_ref, b_ref, o_ref):
        part = _dot(a_ref[...], b_ref[...], "nn" if mode == "kn" else mode)
        if nk == 1:
            o_ref[...] = part.astype(o_ref.dtype)
        else:
            @pl.when(pl.program_id(2) == 0)
            def _():
                o_ref[...] = jnp.zeros_like(o_ref)

            o_ref[...] += part

    return pl.pallas_call(
        body, grid=(n_len // tn, m_len // tm, nk), name=name,
        in_specs=[a_spec, b_spec],
        out_specs=pl.BlockSpec((tm, tn), lambda j, i, k: (i, j)),
        out_shape=jax.ShapeDtypeStruct((m_len, n_len), out_dtype),
        compiler_params=_cparams(("parallel", "parallel", "arbitrary")),
    )(a, b)


def _ffn_act(h, w13, *, name):
    t_len, d = h.shape
    tm = min(ROW_TILE, t_len)

    def body(h_ref, w_ref, s_ref, st_ref):
        ab = _dot(h_ref[...], w_ref[...], "nn")
        a, b = ab[:, :FF_HALF], ab[:, FF_HALF:]
        s = a * jax.nn.sigmoid(a) * b
        s_ref[...] = s.astype(s_ref.dtype)
        st_ref[...] = s.T.astype(st_ref.dtype)

    return pl.pallas_call(
        body, grid=(2, t_len // tm), name=name,
        in_specs=[pl.BlockSpec((tm, d), lambda f, i: (i, 0)), pl.BlockSpec((d, 2 * FF_HALF), lambda f, i: (0, f))],
        out_specs=[pl.BlockSpec((tm, FF_HALF), lambda f, i: (i, f)), pl.BlockSpec((FF_HALF, tm), lambda f, i: (f, i))],
        out_shape=[jax.ShapeDtypeStruct((t_len, D_FF), BF16), jax.ShapeDtypeStruct((D_FF, t_len), BF16)],
        compiler_params=_cparams(("parallel", "parallel")),
    )(h, w13)


def _ffn_act_bwd(h, dy, w13, w2, *, name):
    t_len, d = h.shape
    tm = min(ROW_TILE, t_len)

    def body(h_ref, dy_ref, w_ref, w2_ref, o_ref):
        ab = _dot(h_ref[...], w_ref[...], "nn")
        a, b = ab[:, :FF_HALF], ab[:, FF_HALF:]
        ds = _dot(dy_ref[...], w2_ref[...], "nt")
        sig = jax.nn.sigmoid(a)
        silu = a * sig
        da = ds * b * (sig * (1.0 + a * (1.0 - sig)))
        db = ds * silu
        o_ref[...] = jnp.concatenate([da, db], axis=-1).astype(o_ref.dtype)

    return pl.pallas_call(
        body, grid=(2, t_len // tm), name=name,
        in_specs=[pl.BlockSpec((tm, d), lambda f, i: (i, 0)), pl.BlockSpec((tm, d), lambda f, i: (i, 0)),
                  pl.BlockSpec((d, 2 * FF_HALF), lambda f, i: (0, f)), pl.BlockSpec((FF_HALF, d), lambda f, i: (f, 0))],
        out_specs=pl.BlockSpec((tm, 2 * FF_HALF), lambda f, i: (i, f)),
        out_shape=jax.ShapeDtypeStruct((t_len, 2 * D_FF), BF16),
        compiler_params=_cparams(("parallel", "parallel")),
    )(h, dy, w13, w2)


CONV_CB = 512
HALO = 8


def _conv_fwd(p, c0, n_ch, w, b, *, name):
    t_len = p.shape[0]
    tm = min(ROW_TILE, t_len)
    hb = tm // HALO
    cb0 = c0 // CONV_CB

    def body(x_ref, halo_ref, w_ref, b_ref, act_ref, pre_ref):
        first = pl.program_id(1) == 0
        halo = jnp.where(first, 0.0, halo_ref[...])
        xx = jnp.concatenate([halo, x_ref[...]], axis=0)
        wv = w_ref[...]
        acc = b_ref[...] + wv[0:1] * xx[HALO - 3:HALO - 3 + tm]
        for j in range(1, CONV_K):
            acc = acc + wv[j:j + 1] * xx[HALO - 3 + j:HALO - 3 + j + tm]
        pre_ref[...] = acc
        act_ref[...] = acc * jax.nn.sigmoid(acc)

    return pl.pallas_call(
        body, grid=(n_ch // CONV_CB, t_len // tm), name=name,
        in_specs=[pl.BlockSpec((tm, CONV_CB), lambda j, i: (i, cb0 + j)),
                  pl.BlockSpec((HALO, CONV_CB), lambda j, i: (jnp.maximum(i * hb - 1, 0), cb0 + j)),
                  pl.BlockSpec((CONV_K, CONV_CB), lambda j, i: (0, j)),
                  pl.BlockSpec((1, CONV_CB), lambda j, i: (0, j))],
        out_specs=[pl.BlockSpec((tm, CONV_CB), lambda j, i: (i, j))] * 2,
        out_shape=[jax.ShapeDtypeStruct((t_len, n_ch), F32)] * 2,
        compiler_params=_cparams(("parallel", "arbitrary")),
    )(p, p, w, b)


def _conv_bwd(dact, pre, pre_c0, p, p_c0, w, *, name):
    t_len, n_ch = dact.shape
    tm = min(ROW_TILE, t_len)
    hb = tm // HALO
    nt = t_len // tm
    last_hb = t_len // HALO - 1
    cb0 = p_c0 // CONV_CB
    cbp = pre_c0 // CONV_CB

    def dsilu(z):
        sig = jax.nn.sigmoid(z)
        return sig * (1.0 + z * (1.0 - sig))

    def body(d_ref, dn_ref, pre_ref, pren_ref, x_ref, xh_ref, w_ref, dx_ref, dw_ref, db_ref):
        i = pl.program_id(1)
        dpre = d_ref[...] * dsilu(pre_ref[...])
        dnext = jnp.where(i == nt - 1, 0.0, dn_ref[...] * dsilu(pren_ref[...]))
        ext = jnp.concatenate([dpre, dnext], axis=0)
        xx = jnp.concatenate([jnp.where(i == 0, 0.0, xh_ref[...]), x_ref[...]], axis=0)
        wv = w_ref[...]
        dx = wv[0:1] * ext[3:3 + tm]
        for j in range(1, CONV_K):
            dx = dx + wv[j:j + 1] * ext[3 - j:3 - j + tm]
        dx_ref[...] = dx
        dws = [jnp.sum(dpre * xx[HALO - 3 + j:HALO - 3 + j + tm], axis=0, keepdims=True) for j in range(CONV_K)]

        @pl.when(i == 0)
        def _():
            dw_ref[...] = jnp.zeros_like(dw_ref)
            db_ref[...] = jnp.zeros_like(db_ref)

        dw_ref[...] += jnp.concatenate(dws, axis=0)
        db_ref[...] += jnp.sum(dpre, axis=0, keepdims=True)

    tile = lambda off: pl.BlockSpec((tm, CONV_CB), lambda j, i: (i, off + j))
    nxt = lambda off: pl.BlockSpec((HALO, CONV_CB), lambda j, i: (jnp.minimum((i + 1) * hb, last_hb), off + j))
    return pl.pallas_call(
        body, grid=(n_ch // CONV_CB, nt), name=name,
        in_specs=[tile(0), nxt(0), tile(cbp), nxt(cbp), tile(cb0),
                  pl.BlockSpec((HALO, CONV_CB), lambda j, i: (jnp.maximum(i * hb - 1, 0), cb0 + j)),
                  pl.BlockSpec((CONV_K, CONV_CB), lambda j, i: (0, cbp + j))],
        out_specs=[tile(0), pl.BlockSpec((CONV_K, CONV_CB), lambda j, i: (0, j)), pl.BlockSpec((1, CONV_CB), lambda j, i: (0, j))],
        out_shape=[jax.ShapeDtypeStruct((t_len, n_ch), F32), jax.ShapeDtypeStruct((CONV_K, n_ch), F32),
                   jax.ShapeDtypeStruct((1, n_ch), F32)],
        compiler_params=_cparams(("parallel", "arbitrary")),
    )(dact, dact, pre, pre, p, p, w)


@jax.custom_vjp
def _inv_unit_lower_many(a_cat):
    c = a_cat.shape[0]
    assert LANE % c == 0 and a_cat.shape[1] % LANE == 0
    x = (_iota2(a_cat.shape, 0) == _iota2(a_cat.shape, 1) % c).astype(F32)
    tiles = [a_cat[:, t * LANE:(t + 1) * LANE] for t in range(a_cat.shape[1] // LANE)]
    first = (_iota2((c, LANE), 1) // c) * c
    for j in range(c - 1):
        col = jnp.concatenate([jnp.take_along_axis(t, first + j, axis=1) for t in tiles], axis=-1)
        x = x - col * x[j:j + 1, :]
    return x


def _inv_fwd(a_cat):
    x = _inv_unit_lower_many(a_cat)
    return x, x


def _inv_bwd(x, g):
    c = x.shape[0]
    parts = [-_dot(x[:, s], _dot(g[:, s], x[:, s], "nt", F32_3PASS), "tn", F32_3PASS)
             for s in (slice(i * c, (i + 1) * c) for i in range(x.shape[1] // c))]
    return (jnp.concatenate(parts, axis=-1),)


_inv_unit_lower_many.defvjp(_inv_fwd, _inv_bwd)


def _l2norm(x):
    return x * lax.rsqrt(jnp.sum(x * x, axis=-1, keepdims=True) + NORM_EPS)


def _rms(x):
    return x * lax.rsqrt(jnp.mean(x * x, axis=-1, keepdims=True) + NORM_EPS)


def _tri_masks(c):
    rows, cols = _iota2((c, c), 0), _iota2((c, c), 1)
    return rows >= cols, rows > cols, (rows >= cols).astype(F32), (rows <= cols).astype(F32)


def _gdn_tile(q, k, v, misc, s0, alog, dtb):
    c = CHUNK
    lower, strict, ltri, utri = _tri_masks(c)
    n_chunk = q.shape[0] // c
    pre = []
    for h in range(GDN_HEADS):
        hs = slice(h * LANE, (h + 1) * LANE)
        neg_a = -jnp.exp(alog[:, h:h + 1])
        for ci in range(n_chunk):
            sl = slice(ci * c, (ci + 1) * c)
            qn = _l2norm(q[sl, hs]) * (GDN_DK ** -0.5)
            kn = _l2norm(k[sl, hs])
            beta = jax.nn.sigmoid(misc[sl, 64 + h:65 + h])
            g = neg_a * jax.nn.softplus(misc[sl, 68 + h:69 + h] + dtb[:, h:h + 1])
            gb = jnp.broadcast_to(g, (c, c))
            gc_col = _mm(ltri, gb, "nn", True)
            gc_row = _mm(gb, utri, "tn", True)
            decay = jnp.where(lower, jnp.exp(jnp.where(lower, gc_col - gc_row, 0.0)), 0.0)
            kb = kn * beta
            a_mat = jnp.where(strict, _mm(kb, kn, "nt", False) * decay, 0.0)
            pre.append((qn, kn, kb, v[sl, hs] * beta, decay, gc_col[:, 0:1], gc_col[c - 1:c, 0:1], a_mat))
    t_all = _inv_unit_lower_many(jnp.concatenate([p[7] for p in pre], axis=-1))
    o_heads, s_heads = [], []
    for h in range(GDN_HEADS):
        s = s0[h * GDN_DK:(h + 1) * GDN_DK]
        outs = []
        for ci in range(n_chunk):
            i = h * n_chunk + ci
            qn, kn, kb, vb, decay, gc, g_last, _ = pre[i]
            t_inv = t_all[:, i * c:(i + 1) * c]
            u = _mm(t_inv, vb, "nn", F32_3PASS)
            w = _mm(t_inv, kb * jnp.exp(gc), "nn", F32_3PASS)
            attn = _mm(qn, kn, "nt", False) * decay
            k_end = kn * jnp.exp(g_last - gc)
            q_start = qn * jnp.exp(gc)
            v_new = u - _mm(w, s, "nn", False)
            outs.append(_mm(q_start, s, "nn", False) + _mm(attn, v_new, "nn", False))
            s = s * jnp.exp(g_last) + _mm(k_end, v_new, "tn", False)
        o_heads.append(jnp.concatenate(outs, axis=0))
        s_heads.append(s)
    return jnp.concatenate(o_heads, axis=-1), jnp.concatenate(s_heads, axis=0)


def _gdn_specs(tt, rev_n=None):
    t = (lambda i: i) if rev_n is None else (lambda i: rev_n - 1 - i)
    col = lambda j: pl.BlockSpec((tt, GDN_HEADS * LANE), lambda i: (t(i), j))
    vec = pl.BlockSpec((1, LANE), lambda i: (0, 0))
    misc = pl.BlockSpec((tt, LANE), lambda i: (t(i), EV_MISC // LANE))
    return [col(0), col(1), col(2), misc, vec, vec], t


def _gdn_fwd(act, p, alog, dtb, *, name):
    t_len = act.shape[0]
    tt = min(SEQ_TILE, t_len)
    ntile = t_len // tt
    in_specs, _ = _gdn_specs(tt)

    def body(q_ref, k_ref, v_ref, m_ref, al_ref, dt_ref, o_ref, s_ref, state):
        @pl.when(pl.program_id(0) == 0)
        def _():
            state[...] = jnp.zeros_like(state)

        s_ref[0] = state[...]
        o, s_new = _gdn_tile(q_ref[...], k_ref[...], v_ref[...], m_ref[...], state[...], al_ref[...], dt_ref[...])
        o_ref[...] = o
        state[...] = s_new

    return pl.pallas_call(
        body, grid=(ntile,), name=name, in_specs=in_specs,
        out_specs=[pl.BlockSpec((tt, GDN_HEADS * LANE), lambda i: (i, 0)),
                   pl.BlockSpec((1, GDN_HEADS * GDN_DK, LANE), lambda i: (i, 0, 0))],
        out_shape=[jax.ShapeDtypeStruct((t_len, GDN_HEADS * LANE), F32),
                   jax.ShapeDtypeStruct((ntile, GDN_HEADS * GDN_DK, LANE), F32)],
        scratch_shapes=[pltpu.VMEM((GDN_HEADS * GDN_DK, LANE), F32)],
        compiler_params=_cparams(("arbitrary",)),
    )(act, act, act, p, alog, dtb)


def _gdn_bwd(act, p, alog, dtb, states, do, *, name):
    t_len = act.shape[0]
    tt = min(SEQ_TILE, t_len)
    ntile = t_len // tt
    in_specs, t = _gdn_specs(tt, ntile)

    def body(q_ref, k_ref, v_ref, m_ref, al_ref, dt_ref, s0_ref, do_ref,
             dq_ref, dk_ref, dv_ref, dm_ref, dal_ref, ddt_ref, dstate):
        @pl.when(pl.program_id(0) == 0)
        def _():
            dstate[...] = jnp.zeros_like(dstate)
            dal_ref[...] = jnp.zeros_like(dal_ref)
            ddt_ref[...] = jnp.zeros_like(ddt_ref)

        _, vjp = jax.vjp(_gdn_tile, q_ref[...], k_ref[...], v_ref[...], m_ref[...], s0_ref[0], al_ref[...], dt_ref[...])
        dq, dk, dv, dm, ds0, dal, ddt = vjp((do_ref[...], dstate[...]))
        dq_ref[...] = dq
        dk_ref[...] = dk
        dv_ref[...] = dv
        dm_ref[...] = dm
        dstate[...] = ds0
        dal_ref[...] += dal
        ddt_ref[...] += ddt

    row = pl.BlockSpec((tt, GDN_HEADS * LANE), lambda i: (t(i), 0))
    vec = pl.BlockSpec((1, LANE), lambda i: (0, 0))
    return pl.pallas_call(
        body, grid=(ntile,), name=name,
        in_specs=in_specs + [pl.BlockSpec((1, GDN_HEADS * GDN_DK, LANE), lambda i: (t(i), 0, 0)), row],
        out_specs=[row, row, row, pl.BlockSpec((tt, LANE), lambda i: (t(i), 0)), vec, vec],
        out_shape=[jax.ShapeDtypeStruct((t_len, GDN_HEADS * LANE), F32)] * 3
        + [jax.ShapeDtypeStruct((t_len, LANE), F32)] + [jax.ShapeDtypeStruct((1, LANE), F32)] * 2,
        scratch_shapes=[pltpu.VMEM((GDN_HEADS * GDN_DK, LANE), F32)],
        compiler_params=_cparams(("arbitrary",)),
    )(act, act, act, p, alog, dtb, states, do)


def _head_expand():
    return jnp.asarray(np.repeat(np.eye(LANE, SSD_HEADS, dtype=np.float32), SSD_HEADDIM, axis=1))


@jax.custom_vjp
def _per_head(v, expand):
    rows = max(v.shape[0], 8)
    v8 = jnp.broadcast_to(v, (rows, LANE))
    half = _iota2((rows, LANE), 1) // SSD_HEADDIM
    tiles = [jnp.take_along_axis(v8, half + 2 * j, axis=1) for j in range(SSD_D_INNER // LANE)]
    return jnp.concatenate(tiles, axis=-1)[:v.shape[0]]


def _per_head_fwd(v, expand):
    return _per_head(v, expand), expand


def _per_head_bwd(expand, g):
    rows = g.shape[0]
    g8 = jnp.broadcast_to(g, (8, g.shape[1])) if rows == 1 else g
    dv = lax.dot_general(g8, expand, _DN["nt"], precision=lax.Precision.HIGH, preferred_element_type=F32)
    return dv[0:1] if rows == 1 else dv, None


_per_head.defvjp(_per_head_fwd, _per_head_bwd)


def _ssd_tile(xs, bm, cm, dtr, hs0, alog, dtb, dsk, expand):
    c = CHUNK
    gw = SSD_D_INNER // SSD_GROUPS
    hpg = SSD_HEADS // SSD_GROUPS
    lower, _, ltri, utri = _tri_masks(c)
    half = _iota2((c, LANE), 1) // SSD_HEADDIM
    dt = jax.nn.softplus(dtr + dtb)
    da = dt * (-jnp.exp(alog))
    xdt = xs * _per_head(dt, expand)
    d_x = _per_head(dsk, expand)
    hs = [hs0[g * SSD_STATE:(g + 1) * SSD_STATE] for g in range(SSD_GROUPS)]
    ys = []
    for ci in range(xs.shape[0] // c):
        sl = slice(ci * c, (ci + 1) * c)
        acs = _mm(ltri, da[sl], "nn", True)
        acs_t = _mm(da[sl], utri, "tn", True)
        acs_last = acs[c - 1:c, :]
        e_start = _per_head(jnp.exp(acs), expand)
        e_end = _per_head(jnp.exp(acs_last - acs), expand)
        e_dec = _per_head(jnp.exp(acs_last), expand)
        xdt_c = xdt[sl]
        y_tiles = [None] * (SSD_D_INNER // LANE)
        y_off = []
        for g in range(SSD_GROUPS):
            b_g = bm[sl, g * SSD_STATE:(g + 1) * SSD_STATE]
            c_g = cm[sl, g * SSD_STATE:(g + 1) * SSD_STATE]
            gs = slice(g * gw, (g + 1) * gw)
            cb = _mm(c_g, b_g, "nt", False)
            y_off.append(_mm(c_g, hs[g], "nn", False) * e_start[:, gs])
            for r in range(hpg):
                h = g * hpg + r
                j = h // 2
                lm = jnp.where(lower, jnp.exp(jnp.where(lower, acs[:, h:h + 1] - acs_t[h:h + 1, :], 0.0)), 0.0)
                xm = jnp.where(half == (h % 2), xdt_c[:, j * LANE:(j + 1) * LANE], 0.0)
                part = _mm(cb * lm, xm, "nn", False)
                y_tiles[j] = part if y_tiles[j] is None else y_tiles[j] + part
            hs[g] = hs[g] * e_dec[:, gs] + _mm(b_g, xdt_c[:, gs] * e_end[:, gs], "tn", False)
        ys.append(jnp.concatenate(y_tiles, axis=-1) + jnp.concatenate(y_off, axis=-1) + d_x * xs[sl])
    return jnp.concatenate(ys, axis=0), jnp.concatenate(hs, axis=0)


def _ssd_specs(tt, rev_n=None):
    t = (lambda i: i) if rev_n is None else (lambda i: rev_n - 1 - i)
    vec = pl.BlockSpec((1, LANE), lambda i: (0, 0))
    specs = [pl.BlockSpec((tt, SSD_D_INNER), lambda i: (t(i), 0)),
             pl.BlockSpec((tt, 512), lambda i: (t(i), SSD_D_INNER // 512)),
             pl.BlockSpec((tt, 512), lambda i: (t(i), SSD_D_INNER // 512 + 1)),
             pl.BlockSpec((tt, LANE), lambda i: (t(i), OD_DT // LANE)), vec, vec, vec,
             pl.BlockSpec((LANE, SSD_D_INNER), lambda i: (0, 0))]
    return specs, t


def _ssd_fwd(act, p, alog, dtb, dsk, *, name):
    t_len = act.shape[0]
    tt = min(SEQ_TILE, t_len)
    ntile = t_len // tt
    in_specs, _ = _ssd_specs(tt)

    def body(x_ref, b_ref, c_ref, dt_ref, al_ref, db_ref, dk_ref, e_ref, y_ref, s_ref, state):
        @pl.when(pl.program_id(0) == 0)
        def _():
            state[...] = jnp.zeros_like(state)

        s_ref[0] = state[...]
        y, hs = _ssd_tile(x_ref[...], b_ref[...], c_ref[...], dt_ref[...], state[...], al_ref[...], db_ref[...],
                          dk_ref[...], e_ref[...])
        y_ref[...] = y
        state[...] = hs

    return pl.pallas_call(
        body, grid=(ntile,), name=name, in_specs=in_specs,
        out_specs=[pl.BlockSpec((tt, SSD_D_INNER), lambda i: (i, 0)),
                   pl.BlockSpec((1, SSD_GROUPS * SSD_STATE, 512), lambda i: (i, 0, 0))],
        out_shape=[jax.ShapeDtypeStruct((t_len, SSD_D_INNER), F32),
                   jax.ShapeDtypeStruct((ntile, SSD_GROUPS * SSD_STATE, 512), F32)],
        scratch_shapes=[pltpu.VMEM((SSD_GROUPS * SSD_STATE, 512), F32)],
        compiler_params=_cparams(("arbitrary",)),
    )(act, act, act, p, alog, dtb, dsk, _head_expand())


def _ssd_bwd(act, p, alog, dtb, dsk, states, dy, *, name):
    t_len = act.shape[0]
    tt = min(SEQ_TILE, t_len)
    ntile = t_len // tt
    in_specs, t = _ssd_specs(tt, ntile)

    def body(x_ref, b_ref, c_ref, dt_ref, al_ref, db_ref, dk_ref, e_ref, s0_ref, dy_ref,
             dx_ref, dbm_ref, dcm_ref, ddt_ref, dal_ref, ddb_ref, ddk_ref, dstate):
        @pl.when(pl.program_id(0) == 0)
        def _():
            dstate[...] = jnp.zeros_like(dstate)
            dal_ref[...] = jnp.zeros_like(dal_ref)
            ddb_ref[...] = jnp.zeros_like(ddb_ref)
            ddk_ref[...] = jnp.zeros_like(ddk_ref)

        expand = e_ref[...]

        def f(xs, bm, cm, dtr, hs0, al, db, dk):
            return _ssd_tile(xs, bm, cm, dtr, hs0, al, db, dk, expand)

        _, vjp = jax.vjp(f, x_ref[...], b_ref[...], c_ref[...], dt_ref[...], s0_ref[0], al_ref[...], db_ref[...],
                         dk_ref[...])
        dx, dbm, dcm, ddt, dhs, dal, ddb, ddk = vjp((dy_ref[...], dstate[...]))
        dx_ref[...] = dx
        dbm_ref[...] = dbm
        dcm_ref[...] = dcm
        ddt_ref[...] = ddt
        dstate[...] = dhs
        dal_ref[...] += dal
        ddb_ref[...] += ddb
        ddk_ref[...] += ddk

    vec = pl.BlockSpec((1, LANE), lambda i: (0, 0))
    rows = lambda w: pl.BlockSpec((tt, w), lambda i: (t(i), 0))
    return pl.pallas_call(
        body, grid=(ntile,), name=name,
        in_specs=in_specs + [pl.BlockSpec((1, SSD_GROUPS * SSD_STATE, 512), lambda i: (t(i), 0, 0)), rows(SSD_D_INNER)],
        out_specs=[rows(SSD_D_INNER), rows(512), rows(512), rows(LANE), vec, vec, vec],
        out_shape=[jax.ShapeDtypeStruct((t_len, SSD_D_INNER), F32), jax.ShapeDtypeStruct((t_len, 512), F32),
                   jax.ShapeDtypeStruct((t_len, 512), F32), jax.ShapeDtypeStruct((t_len, LANE), F32)]
        + [jax.ShapeDtypeStruct((1, LANE), F32)] * 3,
        scratch_shapes=[pltpu.VMEM((SSD_GROUPS * SSD_STATE, 512), F32)],
        compiler_params=_cparams(("arbitrary",)),
    )(act, act, act, p, alog, dtb, dsk, _head_expand(), states, dy)


ATT_SCALE = (MLA_NOPE + MLA_ROPE) ** -0.5
ATT_SCALE2 = ATT_SCALE * math.log2(math.e)
QK_W = 2 * LANE


def _chunk_mask(tq):
    return (_iota2((tq, tq), 1) // CHUNK) <= (_iota2((tq, tq), 0) // CHUNK)


def _attn_fwd(qc, kc, vv, *, name):
    t_len = qc.shape[0]
    tq = min(ATT_TILE, t_len)
    nq = t_len // tq

    def body(q_ref, k_ref, v_ref, o_ref, lse_ref, m_s, l_s, acc_s):
        qi, ki = pl.program_id(1), pl.program_id(2)

        @pl.when(ki == 0)
        def _():
            m_s[...] = jnp.full_like(m_s, -jnp.inf)
            l_s[...] = jnp.zeros_like(l_s)
            acc_s[...] = jnp.zeros_like(acc_s)

        def step(masked):
            s = _dot(q_ref[...], k_ref[...], "nt") * ATT_SCALE2
            if masked:
                s = jnp.where(_chunk_mask(tq), s, -jnp.inf)
            m_new = jnp.maximum(m_s[...], jnp.max(s, axis=-1, keepdims=True))
            alpha = jnp.exp2(m_s[...] - m_new)
            p = jnp.exp2(s - m_new)
            l_s[...] = alpha * l_s[...] + jnp.sum(p, axis=-1, keepdims=True)
            acc_s[...] = alpha * acc_s[...] + _dot(p, v_ref[...], "nn")
            m_s[...] = m_new

        @pl.when(ki < qi)
        def _():
            step(False)

        @pl.when(ki == qi)
        def _():
            step(True)
            o_ref[...] = acc_s[...] / l_s[...]
            lse_ref[...] = jnp.broadcast_to(m_s[...] + jnp.log2(l_s[...]), lse_ref.shape)

    kv_idx = lambda h, i, k: (jnp.minimum(k, i), h)
    return pl.pallas_call(
        body, grid=(MLA_HEADS, nq, nq), name=name,
        in_specs=[pl.BlockSpec((tq, QK_W), lambda h, i, k: (i, h)), pl.BlockSpec((tq, QK_W), kv_idx),
                  pl.BlockSpec((tq, LANE), kv_idx)],
        out_specs=[pl.BlockSpec((tq, LANE), lambda h, i, k: (i, h))] * 2,
        out_shape=[jax.ShapeDtypeStruct((t_len, MLA_HEADS * LANE), F32)] * 2,
        scratch_shapes=[pltpu.VMEM((tq, 1), F32), pltpu.VMEM((tq, 1), F32), pltpu.VMEM((tq, LANE), F32)],
        compiler_params=_cparams(("parallel", "parallel", "arbitrary")),
    )(qc, kc, vv)


def _attn_probs(q, k, v, do, o, lse, masked, tq):
    s = _dot(q, k, "nt") * ATT_SCALE2
    if masked:
        s = jnp.where(_chunk_mask(tq), s, -jnp.inf)
    p = jnp.exp2(s - lse[:, 0:1])
    delta = jnp.sum(do * o, axis=-1, keepdims=True)
    ds = p * (_dot(do, v, "nt") - delta)
    return p, ds


def _attn_bwd(qc, kc, vv, o, lse, do, *, name):
    t_len = qc.shape[0]
    tq = min(ATT_TILE, t_len)
    nq = t_len // tq

    def body(q_ref, k_ref, v_ref, o_ref, lse_ref, do_ref, dq_hbm, dk_ref, dv_ref, dq_s, dk_s, dv_s):
        head, ki, qi = pl.program_id(0), pl.program_id(1), pl.program_id(2)
        rows = pl.ds(pl.multiple_of(qi * tq, tq), tq)

        @pl.when(qi == 0)
        def _():
            dk_s[...] = jnp.zeros_like(dk_s)
            dv_s[...] = jnp.zeros_like(dv_s)

        def step(masked):
            p, ds = _attn_probs(q_ref[...], k_ref[...], v_ref[...], do_ref[...], o_ref[...], lse_ref[...], masked, tq)
            dv_s[...] += _dot(p, do_ref[...], "tn")
            dk_s[...] += _dot(ds, q_ref[...], "tn")
            part = _dot(ds, k_ref[...], "nn")

            @pl.when(ki == 0)
            def _():
                dq_s[rows, :] = part

            @pl.when(ki > 0)
            def _():
                dq_s[rows, :] += part

        @pl.when(qi > ki)
        def _():
            step(False)

        @pl.when(qi == ki)
        def _():
            step(True)
            dq_s[rows, :] = dq_s[rows, :] * ATT_SCALE
            pltpu.sync_copy(dq_s.at[rows, :], dq_hbm.at[rows, pl.ds(pl.multiple_of(head * QK_W, QK_W), QK_W)])

        @pl.when(qi == nq - 1)
        def _():
            dk_ref[...] = dk_s[...] * ATT_SCALE
            dv_ref[...] = dv_s[...]

    q_idx = lambda h, k, i: (jnp.maximum(i, k), h)
    k_idx = lambda h, k, i: (k, h)
    return pl.pallas_call(
        body, grid=(MLA_HEADS, nq, nq), name=name,
        in_specs=[pl.BlockSpec((tq, QK_W), q_idx), pl.BlockSpec((tq, QK_W), k_idx), pl.BlockSpec((tq, LANE), k_idx),
                  pl.BlockSpec((tq, LANE), q_idx), pl.BlockSpec((tq, LANE), q_idx), pl.BlockSpec((tq, LANE), q_idx)],
        out_specs=[pl.BlockSpec(memory_space=pl.ANY), pl.BlockSpec((tq, QK_W), k_idx), pl.BlockSpec((tq, LANE), k_idx)],
        out_shape=[jax.ShapeDtypeStruct((t_len, MLA_HEADS * QK_W), F32), jax.ShapeDtypeStruct((t_len, MLA_HEADS * QK_W), F32),
                   jax.ShapeDtypeStruct((t_len, MLA_HEADS * LANE), F32)],
        scratch_shapes=[pltpu.VMEM((t_len, QK_W), F32), pltpu.VMEM((tq, QK_W), F32), pltpu.VMEM((tq, LANE), F32)],
        compiler_params=_cparams(("arbitrary", "arbitrary", "arbitrary")),
    )(qc, kc, vv, o, lse, do)


def _adaln_fn(x, g, shift, scale):
    return ((_rms(x) * g) * (1.0 + scale) + shift,)


def _resid_fn(coef, y, x, gate):
    return (x + coef * gate * y,)


def _rms2_fn(cq, ckv, gq, gkv):
    return _rms(cq) * gq, _rms(ckv) * gkv


@jax.custom_vjp
def _swap_halves(x):
    return jnp.concatenate([x[:, 32:64], x[:, 0:32], x[:, 64:128]], axis=-1)


_swap_halves.defvjp(lambda x: (_swap_halves(x), None), lambda _, g: (_swap_halves(g),))


def _rope_fn(q, kv, misc, pos, invf, sgn):
    ang = pos * invf
    cos, sin = jnp.cos(ang), jnp.sin(ang) * sgn

    def rope(x):
        return x * cos + _swap_halves(x) * sin

    k_pe = rope(jnp.where(_iota2(misc.shape, 1) < MLA_ROPE, misc, 0.0))
    qs, ks = [], []
    for h in range(MLA_HEADS):
        qs += [q[:, h * LANE:(h + 1) * LANE], rope(q[:, (MLA_HEADS + h) * LANE:(MLA_HEADS + h + 1) * LANE])]
        ks += [kv[:, h * LANE:(h + 1) * LANE], k_pe]
    return jnp.concatenate(qs, axis=-1), jnp.concatenate(ks, axis=-1), kv[:, MLA_HEADS * LANE:]


def _ev_out_fn(oa, z, ob, g):
    parts = []
    for h in range(GDN_HEADS):
        hs = slice(h * LANE, (h + 1) * LANE)
        zz = z[:, hs]
        parts.append(_rms(oa[:, hs]) * g * (zz * jax.nn.sigmoid(zz)))
    return (jnp.concatenate(parts + [ob], axis=-1),)


def _od_out_fn(y, z, g):
    yz = y * (z * jax.nn.sigmoid(z))
    gw = SSD_D_INNER // SSD_GROUPS
    return (jnp.concatenate([_rms(yz[:, i * gw:(i + 1) * gw]) for i in range(SSD_GROUPS)], axis=-1) * g,)


def _loss_bwd(x, tgt, g, *, name):
    t_len, d = x.shape
    tm = min(ROW_TILE // 2, t_len)

    def body(x_ref, t_ref, g_ref, loss_ref, dx_ref, dg_ref):
        tgt_v = t_ref[...]

        def f(xv, gv):
            err = _rms(xv) * gv - tgt_v
            return 0.5 * jnp.sum(jnp.mean(err * err, axis=-1, keepdims=True), axis=0, keepdims=True)

        val, vjp = jax.vjp(f, x_ref[...], g_ref[...])
        dx, dg = vjp(jnp.ones((1, 1), F32))
        dx_ref[...] = dx

        @pl.when(pl.program_id(0) == 0)
        def _():
            loss_ref[...] = jnp.zeros_like(loss_ref)
            dg_ref[...] = jnp.zeros_like(dg_ref)

        loss_ref[...] += jnp.broadcast_to(val, loss_ref.shape)
        dg_ref[...] += dg

    row = pl.BlockSpec((tm, d), lambda i: (i, 0))
    return pl.pallas_call(
        body, grid=(t_len // tm,), name=name,
        in_specs=[row, row, pl.BlockSpec((1, d), lambda i: (0, 0))],
        out_specs=[pl.BlockSpec((1, LANE), lambda i: (0, 0)), row, pl.BlockSpec((1, d), lambda i: (0, 0))],
        out_shape=[jax.ShapeDtypeStruct((1, LANE), F32), jax.ShapeDtypeStruct((t_len, d), F32),
                   jax.ShapeDtypeStruct((1, d), F32)],
        compiler_params=_cparams(("arbitrary",)),
    )(x, tgt, g)


def _mesh_pos():
    return lax.axis_index("x"), lax.axis_index("y"), lax.axis_index("c")


def _exchange(xs, scatter, *, name):
    n_arr = len(xs)

    def body(*refs):
        in_refs, out_refs = refs[:n_arr], refs[n_arr:2 * n_arr]
        send_sems, recv_sems, local_sems = refs[2 * n_arr:]
        mx, my, mc = _mesh_pos()
        me = 4 * mx + 2 * my + mc
        started = []
        for a, (in_ref, out_ref) in enumerate(zip(in_refs, out_refs)):
            def src(j, in_ref=in_ref):
                return in_ref.at[j] if scatter else in_ref

            local = pltpu.make_async_copy(src(me), out_ref.at[me], local_sems.at[a])
            local.start()
            started.append((local, None))
            for d in range(1, N_DEV):
                px = 1 - mx if d & 4 else mx
                py = 1 - my if d & 2 else my
                pc = 1 - mc if d & 1 else mc
                peer = 4 * px + 2 * py + pc
                sem = a * (N_DEV - 1) + d - 1
                send = pltpu.make_async_remote_copy(
                    src_ref=src(peer), dst_ref=out_ref.at[me], send_sem=send_sems.at[sem], recv_sem=recv_sems.at[sem],
                    device_id=(px, py, pc), device_id_type=pl.DeviceIdType.MESH)
                send.start()
                recv = pltpu.make_async_remote_copy(
                    src_ref=src(peer), dst_ref=out_ref.at[peer], send_sem=send_sems.at[sem], recv_sem=recv_sems.at[sem],
                    device_id=(px, py, pc), device_id_type=pl.DeviceIdType.MESH)
                started.append((send, recv))
        for first, recv in started:
            if recv is None:
                first.wait()
            else:
                first.wait_send()
                recv.wait_recv()

    blocks = [tuple(x.shape[1:]) if scatter else tuple(x.shape) for x in xs]
    return pl.pallas_call(
        body, name=name,
        in_specs=[pl.BlockSpec(memory_space=pl.ANY)] * n_arr,
        out_specs=[pl.BlockSpec(memory_space=pl.ANY)] * n_arr,
        out_shape=[jax.ShapeDtypeStruct((N_DEV,) + b, x.dtype) for b, x in zip(blocks, xs)],
        scratch_shapes=[pltpu.SemaphoreType.DMA((n_arr * (N_DEV - 1),)), pltpu.SemaphoreType.DMA((n_arr * (N_DEV - 1),)),
                        pltpu.SemaphoreType.DMA((n_arr,))],
        compiler_params=pltpu.CompilerParams(has_side_effects=True),
    )(*xs)


def _peer_of(d, pos):
    mx, my, mc = pos
    px = 1 - mx if d & 4 else mx
    py = 1 - my if d & 2 else my
    pc = 1 - mc if d & 1 else mc
    return (px, py, pc), 4 * px + 2 * py + pc


_HBM = pl.BlockSpec(memory_space=pltpu.HBM)
_SEM = pl.BlockSpec(memory_space=pltpu.SEMAPHORE)


def _exchange_start(xs, scatter, *, name):
    n_arr = len(xs)
    n_sem = n_arr * (N_DEV - 1)

    def body(*refs):
        in_refs, land_refs = refs[:n_arr], refs[n_arr:2 * n_arr]
        send_sems, recv_sems, token = refs[2 * n_arr], refs[2 * n_arr + 1], refs[-1]
        pos = _mesh_pos()
        me = 4 * pos[0] + 2 * pos[1] + pos[2]
        for a in range(n_arr):
            for d in range(1, N_DEV):
                dev, peer = _peer_of(d, pos)
                sem = a * (N_DEV - 1) + d - 1
                pltpu.make_async_remote_copy(
                    src_ref=in_refs[a].at[peer] if scatter else in_refs[a], dst_ref=land_refs[a].at[me],
                    send_sem=send_sems.at[sem], recv_sem=recv_sems.at[sem], device_id=dev,
                    device_id_type=pl.DeviceIdType.MESH).start()
        token[...] = jnp.zeros_like(token)

    blocks = [tuple(x.shape[1:]) if scatter else tuple(x.shape) for x in xs]
    srcs = [pltpu.with_memory_space_constraint(x, pltpu.HBM) for x in xs]
    lands = [pltpu.with_memory_space_constraint(lax.empty((N_DEV,) + b, x.dtype), pltpu.HBM) for b, x in zip(blocks, xs)]
    res = pl.pallas_call(
        body, name=name,
        out_shape=(pltpu.SemaphoreType.DMA((n_sem,)), pltpu.SemaphoreType.DMA((n_sem,)),
                   *[pltpu.HBM(a.shape, a.dtype) for a in srcs + lands], jax.ShapeDtypeStruct((8, LANE), F32)),
        in_specs=[_HBM] * (2 * n_arr),
        out_specs=(_SEM, _SEM, *[_HBM] * (2 * n_arr), pl.BlockSpec(memory_space=pltpu.VMEM)),
        input_output_aliases={i: 2 + i for i in range(2 * n_arr)},
        compiler_params=pltpu.CompilerParams(has_side_effects=pltpu.SideEffectType.DATAFLOW_SIDE_EFFECTING),
    )(*srcs, *lands)
    handle = dict(sems=res[:2], srcs=res[2:2 + n_arr], lands=res[2 + n_arr:2 + 2 * n_arr], scatter=scatter)
    return handle, res[-1][0, 0]


def _exchange_wait(handle, after, me, *, name):
    scatter = handle["scatter"]
    n_arr = len(handle["srcs"])

    def body(*refs):
        in_refs, land_refs = refs[:n_arr], refs[n_arr:2 * n_arr]
        send_sems, recv_sems = refs[2 * n_arr], refs[2 * n_arr + 1]
        pos = _mesh_pos()
        for a in range(n_arr):
            for d in range(1, N_DEV):
                dev, peer = _peer_of(d, pos)
                sem = a * (N_DEV - 1) + d - 1
                copy = pltpu.make_async_remote_copy(
                    src_ref=in_refs[a].at[peer] if scatter else in_refs[a], dst_ref=land_refs[a].at[peer],
                    send_sem=send_sems.at[sem], recv_sem=recv_sems.at[sem], device_id=dev,
                    device_id_type=pl.DeviceIdType.MESH)
                copy.wait_send()
                copy.wait_recv()

    thru = list(handle["srcs"]) + list(handle["lands"])
    res = pl.pallas_call(
        body, name=name,
        out_shape=tuple(pltpu.HBM(a.shape, a.dtype) for a in thru),
        in_specs=[_HBM] * (2 * n_arr) + [_SEM, _SEM, pl.BlockSpec(memory_space=pl.ANY)],
        out_specs=tuple([_HBM] * (2 * n_arr)),
        input_output_aliases={i: i for i in range(2 * n_arr)},
        compiler_params=pltpu.CompilerParams(has_side_effects=pltpu.SideEffectType.DATAFLOW_SIDE_EFFECTING),
    )(*thru, *handle["sems"], after)
    out = []
    for src, land in zip(res[:n_arr], res[n_arr:]):
        own = lax.dynamic_index_in_dim(src, me, axis=0, keepdims=True) if scatter else src[None]
        out.append(lax.dynamic_update_index_in_dim(land, own, me, axis=0))
    return out


def _cols(srcs, rows, plans, out_dtype, *, name):
    n_src = len(srcs)
    rb = _pick(rows, (256, 128, 64, 32, 16, 8))

    def width(pieces):
        return sum(p[1] if p[0] == "z" else p[3] - p[2] for p in pieces)

    def body(*refs):
        ins, outs = refs[:n_src], refs[n_src:]
        loaded = {}
        for o_ref, plan in zip(outs, plans):
            for j, pieces in enumerate(plan):
                vals = []
                for pc in pieces:
                    if pc[0] == "z":
                        vals.append(jnp.zeros((rb, pc[1]), out_dtype))
                    else:
                        si, sj, c0, c1 = pc
                        if (si, sj) not in loaded:
                            loaded[(si, sj)] = ins[si][sj]
                        vals.append(loaded[(si, sj)][:, c0:c1].astype(out_dtype))
                o_ref[j] = vals[0] if len(vals) == 1 else jnp.concatenate(vals, axis=-1)

    for arr, r0 in srcs:
        assert r0 % rb == 0
    return pl.pallas_call(
        body, grid=(rows // rb,), name=name,
        in_specs=[pl.BlockSpec((arr.shape[0], rb, arr.shape[2]), lambda i, r0=r0 // rb: (0, r0 + i, 0)) for arr, r0 in srcs],
        out_specs=[pl.BlockSpec((len(p), rb, width(p[0])), lambda i: (0, i, 0)) for p in plans],
        out_shape=[jax.ShapeDtypeStruct((len(p), rows, width(p[0])), out_dtype) for p in plans],
        compiler_params=_cparams(("parallel",)),
    )(*[arr for arr, _ in srcs])


def _shard_pieces(src, a, b, shard_w):
    out = []
    while a < b:
        s = a // shard_w
        e = min(b, (s + 1) * shard_w)
        out.append((src, s, a - s * shard_w, e - s * shard_w))
        a = e
    return out


def _mapped_pieces(a, b, segs):
    out = []
    for n0, n1, k0 in sorted(segs):
        lo, hi = max(a, n0), min(b, n1)
        if lo < hi:
            out.append((0, 0, k0 + lo - n0, k0 + hi - n0))
    return out


_EV_SEGS = [(0, 1536, EV_QKV), (1536, 2048, EV_Z), (2048, 2056, EV_MISC + MLA_ROPE), (2056, 2440, EV_CQ),
            (2440, 2696, EV_CKV), (2696, 2760, EV_MISC)]
EV_NAT_W, OD_NAT_W = 2760, 5152


PACK_W = 1024


def _adamw(w, gparts, m, v, *, name):
    n_rows, n_cols = w.shape
    n_parts = gparts.shape[0]
    tm = _pick(n_rows, (512, 256, 128, 64, 32, 16, 8))
    while n_parts * tm * n_cols * 4 > 4 * 1024 * 1024 and tm % 16 == 0:
        tm //= 2

    def body(w_ref, g_ref, m_ref, v_ref, go_ref, d_ref, mo_ref, vo_ref):
        g = g_ref[0]
        for j in range(1, n_parts):
            g = g + g_ref[j]
        m_new = ADAM_B1 * m_ref[...] + (1.0 - ADAM_B1) * g
        v_new = ADAM_B2 * v_ref[...] + (1.0 - ADAM_B2) * jnp.square(g)
        m_hat = m_new / (1.0 - ADAM_B1 ** ADAM_STEP)
        v_hat = v_new / (1.0 - ADAM_B2 ** ADAM_STEP)
        go_ref[...] = g
        d_ref[...] = -ADAM_LR * (m_hat / (jnp.sqrt(v_hat) + ADAM_EPS) + ADAM_WD * w_ref[...])
        mo_ref[...] = m_new
        vo_ref[...] = v_new

    row = pl.BlockSpec((tm, n_cols), lambda i: (i, 0))
    return pl.pallas_call(
        body, grid=(n_rows // tm,), name=name,
        in_specs=[row, pl.BlockSpec((n_parts, tm, n_cols), lambda i: (0, i, 0)), row, row],
        out_specs=[row] * 4,
        out_shape=[jax.ShapeDtypeStruct((n_rows, n_cols), F32)] * 4,
        compiler_params=_cparams(("parallel",)),
    )(w, gparts, m, v)


def _adamw_nd(w, gparts, m, v, *, name):
    shape = w.shape
    two = (-1, shape[-1])
    outs = _adamw(w.reshape(two), gparts.reshape((gparts.shape[0],) + (int(np.prod(shape[:-1])), shape[-1])),
                  m.reshape(two), v.reshape(two), name=name)
    return [o.reshape(shape) for o in outs]


def _pack(parts):
    flat = [p.astype(F32).reshape(-1) for p in parts]
    n_pad = -sum(f.shape[0] for f in flat) % (8 * PACK_W)
    return jnp.concatenate(flat + [jnp.zeros((n_pad,), F32)]).reshape(-1, PACK_W)


def _unpack(packed, shapes):
    flat = packed.reshape(-1)
    out, off = [], 0
    for s in shapes:
        n = int(np.prod(s))
        out.append(flat[off:off + n].reshape(tuple(s)))
        off += n
    return out


def _mod_shard(c_all, ada_w, ada_b_shard, *, name):
    n_layer, d, n_col = ada_w.shape

    def body(c_ref, w_ref, b_ref, o_ref):
        cv = c_ref[...]
        o_ref[0] = _dot(cv * jax.nn.sigmoid(cv), w_ref[0], "nn") + b_ref[0]

    return pl.pallas_call(
        body, grid=(n_layer,), name=name,
        in_specs=[pl.BlockSpec((N_DEV, d), lambda l: (0, 0)), pl.BlockSpec((1, d, n_col), lambda l: (l, 0, 0)),
                  pl.BlockSpec((1, 1, n_col), lambda l: (l, 0, 0))],
        out_specs=pl.BlockSpec((1, N_DEV, n_col), lambda l: (l, 0, 0)),
        out_shape=jax.ShapeDtypeStruct((n_layer, N_DEV, n_col), F32),
        compiler_params=_cparams(("parallel",)),
    )(c_all, ada_w, ada_b_shard)


def _ada_w_grad(c_all, dmod_shard, *, name):
    n_layer, _, n_col = dmod_shard.shape
    d = c_all.shape[1]

    def body(c_ref, g_ref, o_ref):
        cv = c_ref[...]
        o_ref[0] = _dot(cv * jax.nn.sigmoid(cv), g_ref[0], "tn", True)

    return pl.pallas_call(
        body, grid=(n_layer,), name=name,
        in_specs=[pl.BlockSpec((N_DEV, d), lambda l: (0, 0)), pl.BlockSpec((1, N_DEV, n_col), lambda l: (l, 0, 0))],
        out_specs=pl.BlockSpec((1, d, n_col), lambda l: (l, 0, 0)),
        out_shape=jax.ShapeDtypeStruct((n_layer, d, n_col), F32),
        compiler_params=_cparams(("parallel",)),
    )(c_all, dmod_shard)


def _uq(w):
    r = w.shape[0]
    rope = jnp.pad(w[:, :, MLA_NOPE:], ((0, 0), (0, 0), (0, LANE - MLA_ROPE)))
    return jnp.concatenate([w[:, :, :MLA_NOPE].reshape(r, -1), rope.reshape(r, -1)], axis=1)


def _uq_back(d):
    r = d.shape[0]
    half = MLA_HEADS * LANE
    return jnp.concatenate([d[:, :half].reshape(r, MLA_HEADS, LANE),
                            d[:, half:].reshape(r, MLA_HEADS, LANE)[:, :, :MLA_ROPE]], axis=-1)


def _ukv(w):
    r = w.shape[0]
    return jnp.concatenate([w[:, :, :MLA_NOPE].reshape(r, -1), w[:, :, MLA_NOPE:].reshape(r, -1)], axis=1)


def _ukv_back(d):
    r = d.shape[0]
    half = MLA_HEADS * LANE
    return jnp.concatenate([d[:, :half].reshape(r, MLA_HEADS, LANE), d[:, half:].reshape(r, MLA_HEADS, LANE)], axis=-1)


def _lane_vec(v):
    return jnp.pad(v.astype(F32), (0, LANE - v.shape[0])).reshape(1, LANE)


def _row(v):
    return v.astype(F32).reshape(1, -1)


def _adaln(x, ln):
    return _rows(_adaln_fn, [x], list(ln), [(D_MODEL, BF16)], name="adaln", also_t=(0,))


def _adaln_bwd(x, ln, dh, dxn):
    (dx,), dln = _rows_vjp(_adaln_fn, [x], [], list(ln), [], [dh], [F32], adds={0: dxn}, name="adaln_bwd")
    return dx, dln


def _resid(coef, y, x, gate):
    return _rows(functools.partial(_resid_fn, coef), [y, x], [gate], [(D_MODEL, F32)], name="resid")[0]


def _gated_fn(coef, y, gate):
    return (coef * gate * y,)


def _resid_bwd(coef, y, gate, dxn):
    (dy,), (dgate,) = _rows_vjp(functools.partial(_gated_fn, coef), [y], [], [gate], [], [dxn], [BF16], name="resid_bwd")
    return dy, dgate


def _ffn_fwd(x, ln, gate, w13, w2):
    h, ht = _adaln(x, ln)
    s, st = _ffn_act(h, w13, name="ffn_act")
    y = _matmul(s, w2, "nn", F32, name="ffn_down")
    return _resid(0.5, y, x, gate), (x, h, ht, st, y)


def _ffn_bwd(saved, dxn, ln, gate, w13, w2):
    x, h, ht, st, y = saved
    dy, dgate = _resid_bwd(0.5, y, gate, dxn)
    dab = _ffn_act_bwd(h, dy, w13, w2, name="ffn_act_bwd")
    dh = _matmul(dab, w13, "nt", F32, name="ffn_dh")
    dw13 = _matmul(ht, dab, "kn", F32, name="ffn_dw13")
    dw2 = _matmul(st, dy, "kn", F32, name="ffn_dw2")
    dx, dln = _adaln_bwd(x, ln, dh, dxn)
    return dx, dw13, dw2, dln, dgate


def _rope_consts():
    half = MLA_ROPE // 2
    inv = (ROPE_THETA ** (-jnp.arange(half, dtype=F32) / half)).astype(F32)
    zeros = jnp.zeros((LANE - MLA_ROPE,), F32)
    invf = jnp.concatenate([inv, inv, zeros]).reshape(1, LANE)
    sgn = jnp.concatenate([-jnp.ones((half,), F32), jnp.ones((half,), F32), zeros]).reshape(1, LANE)
    return invf, sgn


def _even_fwd(x, pos, ln, gate, wt):
    h, ht = _adaln(x, ln)
    p = _matmul(h, wt["w_in"], "nn", F32, name="ev_in")
    act, pre = _conv_fwd(p, EV_QKV, 1536, wt["conv_w"], jnp.zeros((1, 1536), F32), name="ev_conv")
    o_a, states = _gdn_fwd(act, p, wt["alog"], wt["dtb"], name="gdn_fwd")
    cqn, ckvn = _rows(_rms2_fn, [(p, EV_CQ, 384), (p, EV_CKV, 256)], [wt["gq"], wt["gkv"]],
                      [(384, BF16), (256, BF16)], name="mla_rms")
    q = _matmul(cqn, wt["w_uq"], "nn", F32, name="mla_uq")
    kv = _matmul(ckvn, wt["w_ukv"], "nn", F32, name="mla_ukv")
    invf, sgn = _rope_consts()
    qc, kc, vv = _rows(_rope_fn, [q, kv, (p, EV_MISC, LANE), pos], [invf, sgn],
                       [(1024, BF16), (1024, BF16), (512, BF16)], name="mla_rope", tm=ROW_TILE // 2)
    o_b, lse = _attn_fwd(qc, kc, vv, name="attn_fwd")
    o, ot = _rows(_ev_out_fn, [o_a, (p, EV_Z, 512), o_b], [wt["gdn_g"]], [(1024, BF16)], name="ev_out", also_t=(0,))
    y = _matmul(o, wt["w_out"], "nn", F32, name="ev_wout")
    return _resid(1.0, y, x, gate), (x, ht, p, act, pre, states, cqn, ckvn, q, kv, qc, kc, vv, o_a, o_b, lse, ot, y)


def _cat_fn(*parts):
    return (jnp.concatenate(parts, axis=-1),)


def _ev_dp_fn(dx0, dx1, dx2, dcq, dm_r, dm_g, dz, dckv):
    return (jnp.concatenate([dx0, dx1, dx2, dcq, dm_r + dm_g, dz, dckv], axis=-1),)


def _even_bwd(saved, dxn, pos, ln, gate, wt):
    x, ht, p, act, pre, states, cqn, ckvn, q, kv, qc, kc, vv, o_a, o_b, lse, ot, y = saved
    g = {}
    dy, g["gate"] = _resid_bwd(1.0, y, gate, dxn)
    do = _matmul(dy, wt["w_out"], "nt", F32, name="ev_dwout_x")
    g["w_out"] = _matmul(ot, dy, "kn", F32, name="ev_dwout_w")
    (d_oa, dz, d_ob), (g["gdn_g"],) = _rows_vjp(_ev_out_fn, [o_a, (p, EV_Z, 512), o_b], [], [wt["gdn_g"]], [], [do],
                                                [F32, F32, F32], name="ev_out_bwd")
    dqc, dkc, dvv = _attn_bwd(qc, kc, vv, o_b, lse, d_ob, name="attn_bwd")
    invf, sgn = _rope_consts()
    (dq, dkv, dm_r), _ = _rows_vjp(_rope_fn, [q, kv, (p, EV_MISC, LANE)], [pos], [], [invf, sgn], [dqc, dkc, dvv],
                                   [BF16, BF16, F32], name="mla_rope_bwd", tm=ROW_TILE // 4)
    dcqn = _matmul(dq, wt["w_uq"], "nt", F32, name="mla_duq_x")
    g["w_uq"] = _matmul(cqn, dq, "tn", F32, name="mla_duq_w")
    dckvn = _matmul(dkv, wt["w_ukv"], "nt", F32, name="mla_dukv_x")
    g["w_ukv"] = _matmul(ckvn, dkv, "tn", F32, name="mla_dukv_w")
    (dcq, dckv), (g["gq"], g["gkv"]) = _rows_vjp(_rms2_fn, [(p, EV_CQ, 384), (p, EV_CKV, 256)], [],
                                                 [wt["gq"], wt["gkv"]], [], [dcqn, dckvn], [F32, F32], name="mla_rms_bwd")
    dq_g, dk_g, dv_g, dm_g, g["alog"], g["dtb"] = _gdn_bwd(act, p, wt["alog"], wt["dtb"], states, d_oa, name="gdn_bwd")
    dxs, dws = [], []
    for j, d in enumerate((dq_g, dk_g, dv_g)):
        dxj, dwj, _ = _conv_bwd(d, pre, 512 * j, p, EV_QKV + 512 * j, wt["conv_w"], name="ev_conv_bwd")
        dxs.append(dxj)
        dws.append(dwj)
    g["conv_w"] = jnp.concatenate(dws, axis=1)
    (dp,) = _rows(_ev_dp_fn, dxs + [dcq, dm_r, dm_g, dz, dckv], [],
                  [(EV_W, BF16)], name="ev_dp", tm=ROW_TILE // 2)
    dh = _matmul(dp, wt["w_in"], "nt", F32, name="ev_din_x")
    g["w_in"] = _matmul(ht, dp, "kn", F32, name="ev_din_w")
    dx, g["ln"] = _adaln_bwd(x, ln, dh, dxn)
    return dx, g


def _odd_fwd(x, ln, gate, wt):
    h, ht = _adaln(x, ln)
    p = _matmul(h, wt["w_in"], "nn", F32, name="od_in")
    act, pre = _conv_fwd(p, OD_XBC, 3072, wt["conv_w"], wt["conv_b"], name="od_conv")
    ys, states = _ssd_fwd(act, p, wt["alog"], wt["dtb"], wt["dsk"], name="ssd_fwd")
    o, ot = _rows(_od_out_fn, [ys, (p, OD_Z, 2048)], [wt["norm_g"]], [(SSD_D_INNER, BF16)], name="od_out",
                  tm=ROW_TILE // 2, also_t=(0,))
    y = _matmul(o, wt["w_out"], "nn", F32, name="od_wout")
    return _resid(1.0, y, x, gate), (x, ht, p, act, pre, states, ys, ot, y)


def _od_dp_fn(dz, dxx, dxb, dxc, ddt):
    return (jnp.concatenate([dz, dxx, dxb, dxc, ddt, jnp.zeros_like(ddt)], axis=-1),)


def _odd_bwd(saved, dxn, ln, gate, wt):
    x, ht, p, act, pre, states, ys, ot, y = saved
    g = {}
    dy, g["gate"] = _resid_bwd(1.0, y, gate, dxn)
    do = _matmul(dy, wt["w_out"], "nt", F32, name="od_dwout_x")
    g["w_out"] = _matmul(ot, dy, "kn", F32, name="od_dwout_w")
    (dys, dz), (g["norm_g"],) = _rows_vjp(_od_out_fn, [ys, (p, OD_Z, 2048)], [], [wt["norm_g"]], [], [do], [F32, F32],
                                          name="od_out_bwd", tm=ROW_TILE // 4)
    dxs, dbm, dcm, ddt, g["alog"], g["dtb"], g["dsk"] = _ssd_bwd(act, p, wt["alog"], wt["dtb"], wt["dsk"], states, dys,
                                                                 name="ssd_bwd")
    dins, dws, dbs = [], [], []
    for d, c0 in ((dxs, 0), (dbm, 2048), (dcm, 2560)):
        dxj, dwj, dbj = _conv_bwd(d, pre, c0, p, OD_XBC + c0, wt["conv_w"], name="od_conv_bwd")
        dins.append(dxj)
        dws.append(dwj)
        dbs.append(dbj)
    g["conv_w"] = jnp.concatenate(dws, axis=1)
    g["conv_b"] = jnp.concatenate(dbs, axis=1)
    (dp,) = _rows(_od_dp_fn, [dz] + dins + [ddt], [], [(OD_W, BF16)], name="od_dp", tm=ROW_TILE // 4)
    dh = _matmul(dp, wt["w_in"], "nt", F32, name="od_din_x")
    g["w_in"] = _matmul(ht, dp, "kn", F32, name="od_din_w")
    dx, g["ln"] = _adaln_bwd(x, ln, dh, dxn)
    return dx, g


def _local_step(x, tgt, pos, mod, final_g, layer_weights, layer_done):
    mod = mod.reshape(DEPTH, 3, 3, 1, D_MODEL)
    wts = []

    def ln_of(l, i):
        return (_row(wts[l]["norm_g"][i]), mod[l, i, 0], mod[l, i, 1])

    def mixer_w(l):
        p = wts[l]
        if l % 2 == 0:
            return dict(w_in=p["w_in_k"], conv_w=p["gdn_conv_w"].astype(F32),
                        alog=_lane_vec(p["gdn_A_log"]), dtb=_lane_vec(p["gdn_dt_bias"]),
                        gdn_g=_row(p["gdn_norm_g"]), gq=_row(p["mla_q_norm_g"]), gkv=_row(p["mla_kv_norm_g"]),
                        w_uq=_uq(p["mla_w_uq"]), w_ukv=_ukv(p["mla_w_ukv"]), w_out=p["ev_w_out"])
        return dict(w_in=p["w_in_k"], conv_w=p["ssd_conv_w"].astype(F32),
                    conv_b=_row(p["ssd_conv_b"]), alog=_lane_vec(p["ssd_A_log"]),
                    dtb=_lane_vec(p["ssd_dt_bias"]), dsk=_lane_vec(p["ssd_D"]),
                    norm_g=_row(p["ssd_norm_g"]), w_out=p["ssd_w_out"])

    saved = []
    for l in range(DEPTH):
        wts.append(layer_weights(l, x))
        x, s0 = _ffn_fwd(x, ln_of(l, 0), mod[l, 0, 2], wts[l]["w13"][0], wts[l]["w2"][0])
        if l % 2 == 0:
            x, s1 = _even_fwd(x, pos, ln_of(l, 1), mod[l, 1, 2], mixer_w(l))
        else:
            x, s1 = _odd_fwd(x, ln_of(l, 1), mod[l, 1, 2], mixer_w(l))
        x, s2 = _ffn_fwd(x, ln_of(l, 2), mod[l, 2, 2], wts[l]["w13"][1], wts[l]["w2"][1])
        saved.append((s0, s1, s2))

    loss, dx, d_final_g = _loss_bwd(x, tgt, _row(final_g), name="loss")

    repl = {k: [None] * (DEPTH // 2) for k in ("gdn_A_log", "gdn_dt_bias", "gdn_norm_g", "mla_q_norm_g", "mla_kv_norm_g",
                                                "ssd_A_log", "ssd_dt_bias", "ssd_D")}
    dmod = [None] * DEPTH
    token = None
    for l in reversed(range(DEPTH)):
        s0, s1, s2 = saved[l]
        e = l // 2
        gl = {"w13": [None] * 2, "w2": [None] * 2}
        dg, dsh, dsc, dgt = [None] * 3, [None] * 3, [None] * 3, [None] * 3
        gate2 = mod[l, 2, 2] if token is None else mod[l, 2, 2] + token
        dx, gl["w13"][1], gl["w2"][1], (dg[2], dsh[2], dsc[2]), dgt[2] = _ffn_bwd(
            s2, dx, ln_of(l, 2), gate2, wts[l]["w13"][1], wts[l]["w2"][1])
        if l % 2 == 0:
            dx, g = _even_bwd(s1, dx, pos, ln_of(l, 1), mod[l, 1, 2], mixer_w(l))
            gl.update(w_in_k=g["w_in"], gdn_conv_w=g["conv_w"], mla_w_uq=_uq_back(g["w_uq"]),
                      mla_w_ukv=_ukv_back(g["w_ukv"]), ev_w_out=g["w_out"])
            repl["gdn_A_log"][e] = g["alog"][0, :GDN_HEADS]
            repl["gdn_dt_bias"][e] = g["dtb"][0, :GDN_HEADS]
            repl["gdn_norm_g"][e] = g["gdn_g"][0]
            repl["mla_q_norm_g"][e] = g["gq"][0]
            repl["mla_kv_norm_g"][e] = g["gkv"][0]
        else:
            dx, g = _odd_bwd(s1, dx, ln_of(l, 1), mod[l, 1, 2], mixer_w(l))
            gl.update(w_in_k=g["w_in"], ssd_conv_w=g["conv_w"], ssd_conv_b=g["conv_b"][0], ssd_norm_g=g["norm_g"][0],
                      ssd_w_out=g["w_out"])
            repl["ssd_A_log"][e] = g["alog"][0, :SSD_HEADS]
            repl["ssd_dt_bias"][e] = g["dtb"][0, :SSD_HEADS]
            repl["ssd_D"][e] = g["dsk"][0, :SSD_HEADS]
        dg[1], dsh[1], dsc[1] = g["ln"]
        dgt[1] = g["gate"]
        gl.update(w13=gl["w13"][1], w2=gl["w2"][1])
        token = layer_done(l, 0, gl, dx)
        gate0 = mod[l, 0, 2] if token is None else mod[l, 0, 2] + token
        dx, dw13, dw2, (dg[0], dsh[0], dsc[0]), dgt[0] = _ffn_bwd(s0, dx, ln_of(l, 0), gate0, wts[l]["w13"][0], wts[l]["w2"][0])
        dmod[l] = jnp.concatenate([jnp.concatenate([dsh[i], dsc[i], dgt[i]], axis=1) for i in range(3)], axis=1)[0]
        token = layer_done(l, 1, dict(w13=dw13, w2=dw2, norm_g=jnp.concatenate(dg, axis=0)), dx)

    grads = {k: jnp.stack(v) for k, v in repl.items()}
    grads["final_g"] = d_final_g[0]
    return loss, dx, grads, jnp.stack(dmod)


_WEIGHTS = ("ada_w", "ada_b", "norm_g", "ffn_w1", "ffn_w3", "ffn_w2", "ev_w_in", "gdn_conv_w", "gdn_A_log", "gdn_dt_bias",
            "gdn_norm_g", "mla_q_norm_g", "mla_w_uq", "mla_kv_norm_g", "mla_w_ukv", "ev_w_out", "ssd_w_in", "ssd_conv_w",
            "ssd_conv_b", "ssd_A_log", "ssd_dt_bias", "ssd_D", "ssd_norm_g", "ssd_w_out", "final_g")
_BIG = {"ffn_w1": 3, "ffn_w3": 3, "ffn_w2": 2, "ev_w_in": 2, "mla_w_uq": 1, "mla_w_ukv": 1, "ev_w_out": 1, "ssd_w_in": 2,
        "ssd_w_out": 1}
_SMALL = {"norm_g": 2, "gdn_conv_w": 2, "ssd_conv_w": 2, "ssd_conv_b": 1, "ssd_norm_g": 1}
_REPL = ("ada_b", "gdn_A_log", "gdn_dt_bias", "gdn_norm_g", "mla_q_norm_g", "mla_kv_norm_g", "ssd_A_log", "ssd_dt_bias",
         "ssd_D", "final_g")


def _join(pieces, axis):
    moved = jnp.moveaxis(pieces, 0, axis)
    shape = moved.shape
    return moved.reshape(shape[:axis] + (shape[axis] * shape[axis + 1],) + shape[axis + 2:])


def _split(full, axis):
    shape = full.shape
    return jnp.moveaxis(full.reshape(shape[:axis] + (N_DEV, shape[axis] // N_DEV) + shape[axis + 1:]), axis, 0)


def kernel(x, c, positions, ada_w, ada_b, norm_g, ffn_w1, ffn_w3, ffn_w2, ev_w_in, gdn_conv_w, gdn_A_log, gdn_dt_bias, gdn_norm_g, mla_q_norm_g, mla_w_uq, mla_kv_norm_g, mla_w_ukv, ev_w_out, ssd_w_in, ssd_conv_w, ssd_conv_b, ssd_A_log, ssd_dt_bias, ssd_D, ssd_norm_g, ssd_w_out, final_g, loss_target, m_ada_w, m_ada_b, m_norm_g, m_ffn_w1, m_ffn_w3, m_ffn_w2, m_ev_w_in, m_gdn_conv_w, m_gdn_A_log, m_gdn_dt_bias, m_gdn_norm_g, m_mla_q_norm_g, m_mla_w_uq, m_mla_kv_norm_g, m_mla_w_ukv, m_ev_w_out, m_ssd_w_in, m_ssd_conv_w, m_ssd_conv_b, m_ssd_A_log, m_ssd_dt_bias, m_ssd_D, m_ssd_norm_g, m_ssd_w_out, m_final_g, v_ada_w, v_ada_b, v_norm_g, v_ffn_w1, v_ffn_w3, v_ffn_w2, v_ev_w_in, v_gdn_conv_w, v_gdn_A_log, v_gdn_dt_bias, v_gdn_norm_g, v_mla_q_norm_g, v_mla_w_uq, v_mla_kv_norm_g, v_mla_w_ukv, v_ev_w_out, v_ssd_w_in, v_ssd_conv_w, v_ssd_conv_b, v_ssd_A_log, v_ssd_dt_bias, v_ssd_D, v_ssd_norm_g, v_ssd_w_out, v_final_g):
    a = dict(locals())
    w = {n: a[n] for n in _WEIGHTS}
    m = {n: a["m_" + n] for n in _WEIGHTS}
    v = {n: a["v_" + n] for n in _WEIGHTS}
    mx, my, mc = _mesh_pos()
    me = 4 * mx + 2 * my + mc
    t_len = x.shape[1]
    shards = range(N_DEV)

    small_names, big_names = list(_SMALL), list(_BIG)
    axis_of = {**_SMALL, **_BIG}
    small_g = _exchange([c] + [w[n] for n in small_names], False, name="gather_small")
    c_all = small_g[0].reshape(N_DEV, D_MODEL)
    fw = {n: _join(p, _SMALL[n]) for n, p in zip(small_names, small_g[1:])}
    first_names = [n for n in big_names if not n.startswith("ssd")]
    first_g = dict(zip(first_names, _exchange([w[n][:1].astype(BF16) for n in first_names], False, name="gather_first")))
    rest_src = {n: (w[n] if n.startswith("ssd") else w[n][1:]).astype(BF16) for n in big_names}
    rest_handle, rest_token = _exchange_start([rest_src[n] for n in big_names], False, name="gather_rest_start")
    rest_g = {}

    fs, es, os_ = ffn_w1.shape[3], ev_w_in.shape[2], ssd_w_in.shape[2]
    half = range(N_DEV // 2)
    plan13 = [[[(0, s, 0, fs) for s in half] + [(1, s, 0, fs) for s in half]
               + [(0, s + 4, 0, fs) for s in half] + [(1, s + 4, 0, fs) for s in half]]]
    plan_ev, k_at = [], 0
    for n0, n1, k0 in sorted(_EV_SEGS, key=lambda seg: seg[2]):
        if k0 > k_at:
            plan_ev.append(("z", k0 - k_at))
        plan_ev += _shard_pieces(0, n0, n1, es)
        k_at = k0 + n1 - n0
    assert k_at == EV_W and es * N_DEV == EV_NAT_W and os_ * N_DEV == OD_NAT_W
    plan_od = [[_shard_pieces(0, 0, OD_NAT_W, os_) + [("z", OD_W - OD_NAT_W)]]]

    def layer_weights(l, x_in):
        if l == 1:
            rest_g.update(zip(big_names, _exchange_wait(rest_handle, x_in, me, name="gather_rest_wait")))
        e = l // 2
        src = first_g if l == 0 else rest_g
        i = 0 if l == 0 else l - 1
        ie = 0 if (l == 0 or l % 2) else e - 1
        g1 = src["ffn_w1"].reshape(N_DEV, -1, fs)
        g3 = src["ffn_w3"].reshape(N_DEV, -1, fs)
        p = {"norm_g": fw["norm_g"][l],
             "w13": [_cols([(g1, (2 * i + j) * D_MODEL), (g3, (2 * i + j) * D_MODEL)], D_MODEL, plan13, BF16,
                           name="join_w13")[0][0] for j in range(2)],
             "w2": [src["ffn_w2"][:, i, j].reshape(D_FF, D_MODEL) for j in range(2)]}
        if l % 2 == 0:
            p["w_in_k"] = _cols([(src["ev_w_in"].reshape(N_DEV, -1, es), ie * D_MODEL)], D_MODEL, [[plan_ev]], BF16,
                                name="join_ev_in")[0][0]
            for n in ("mla_w_uq", "mla_w_ukv", "ev_w_out"):
                p[n] = _join(src[n][:, ie], axis_of[n] - 1)
            p["gdn_conv_w"] = fw["gdn_conv_w"][e]
            for n in ("gdn_A_log", "gdn_dt_bias", "gdn_norm_g", "mla_q_norm_g", "mla_kv_norm_g"):
                p[n] = w[n][e]
        else:
            p["w_in_k"] = _cols([(rest_g["ssd_w_in"].reshape(N_DEV, -1, os_), e * D_MODEL)], D_MODEL, plan_od, BF16,
                                name="join_od_in")[0][0]
            p["ssd_w_out"] = _join(rest_g["ssd_w_out"][:, e], axis_of["ssd_w_out"] - 1)
            for n in ("ssd_conv_w", "ssd_conv_b", "ssd_norm_g"):
                p[n] = fw[n][e]
            for n in ("ssd_A_log", "ssd_dt_bias", "ssd_D"):
                p[n] = w[n][e]
        return p

    def w13_cols(s, third):
        k0 = (s % 4) * fs + (2 * FF_HALF if s >= 4 else 0) + (FF_HALF if third else 0)
        return [(0, 0, k0, k0 + fs)]

    sent = {}

    def layer_done(l, stage, gl, dx_l):
        d1, d3 = _cols([(gl["w13"][None], 0)], D_MODEL, [[w13_cols(s, False) for s in shards],
                                                         [w13_cols(s, True) for s in shards]], F32, name="split_w13")
        pieces = {"ffn_w1": d1[:, None, None], "ffn_w3": d3[:, None, None],
                  "ffn_w2": gl["w2"].reshape(N_DEV, -1, D_MODEL)[:, None, None]}
        if stage == 1:
            pieces["norm_g"] = _split(gl["norm_g"][None], 2)
        elif l % 2 == 0:
            pieces["ev_w_in"] = _cols([(gl["w_in_k"][None], 0)], D_MODEL,
                                      [[_mapped_pieces(s * es, (s + 1) * es, _EV_SEGS) for s in shards]], F32,
                                      name="split_ev_in")[0][:, None]
            for n in ("gdn_conv_w", "mla_w_uq", "mla_w_ukv", "ev_w_out"):
                pieces[n] = _split(gl[n][None], axis_of[n])
        else:
            pieces["ssd_w_in"] = _cols([(gl["w_in_k"][None], 0)], D_MODEL,
                                       [[[(0, 0, s * os_, (s + 1) * os_)] for s in shards]], F32,
                                       name="split_od_in")[0][:, None]
            for n in ("ssd_conv_w", "ssd_conv_b", "ssd_norm_g", "ssd_w_out"):
                pieces[n] = _split(gl[n][None], axis_of[n])
        names = list(pieces)
        if l == 0 and stage == 1:
            sent[l, stage] = dict(zip(names, _exchange([pieces[n] for n in names], True, name="scatter_last")))
            return None
        handle, token = _exchange_start([pieces[n] for n in names], True, name=f"scatter_start_{l}_{stage}")
        sent[l, stage] = (names, handle)
        return token

    n_col = ada_w.shape[2]
    ada_b_shard = lax.dynamic_slice(ada_b, (0, me * n_col), (DEPTH, n_col)).reshape(DEPTH, 1, n_col)
    mod_all = _exchange([_mod_shard(c_all, ada_w, ada_b_shard, name="mod")], False, name="gather_mod")[0]
    mod_me = lax.dynamic_index_in_dim(mod_all, me, axis=2, keepdims=False)
    mod = jnp.transpose(mod_me, (1, 0, 2)).reshape(DEPTH, N_DEV * n_col) + rest_token

    pos = positions.astype(F32).reshape(t_len, 1)
    loss, dx, grads, dmod = _local_step(x[0], loss_target[0], pos, mod, final_g, layer_weights, layer_done)
    for (l, stage), started in list(sent.items()):
        if not isinstance(started, dict):
            names, handle = started
            sent[l, stage] = dict(zip(names, _exchange_wait(handle, dx, me, name=f"scatter_wait_{l}_{stage}")))

    repl_shapes = [w[n].shape for n in _REPL] + [(1,)]
    parts8 = _exchange([_pack([dmod] + [grads[n] for n in _REPL[1:]] + [loss[0, :1]])], False, name="gather_repl")[0]
    zero = jnp.zeros((1,), F32)
    r_grad, r_delta, r_m, r_v = [
        _unpack(o, repl_shapes) for o in _adamw(_pack([w[n] for n in _REPL] + [zero]), parts8,
                                                _pack([m[n] for n in _REPL] + [zero]),
                                                _pack([v[n] for n in _REPL] + [zero]), name="adamw_repl")]
    out = {"grad": {}, "delta": {}, "m": {}, "v": {}}
    for i, n in enumerate(_REPL):
        out["grad"][n], out["delta"][n], out["m"][n], out["v"][n] = r_grad[i], r_delta[i], r_m[i], r_v[i]
    loss_total = r_grad[-1].reshape(())

    dmod_all = parts8[:, :dmod.size // PACK_W].reshape((N_DEV,) + dmod.shape)
    dmod_cols = jnp.transpose(lax.dynamic_slice_in_dim(dmod_all, me * n_col, n_col, axis=2), (1, 0, 2))
    g_ada = _ada_w_grad(c_all, dmod_cols, name="ada_w_grad")
    for k, o in zip(("grad", "delta", "m", "v"), _adamw_nd(ada_w, g_ada[None], m["ada_w"], v["ada_w"], name="adamw_ada")):
        out[k]["ada_w"] = o

    for n in small_names + big_names:
        if n == "norm_g":
            g8 = jnp.concatenate([sent[l, 1][n] for l in range(DEPTH)], axis=1)
        elif n.startswith("ffn"):
            g8 = jnp.concatenate([jnp.concatenate([sent[l, 1][n], sent[l, 0][n]], axis=2) for l in range(DEPTH)], axis=1)
        else:
            layers = range(1, DEPTH, 2) if n.startswith("ssd") else range(0, DEPTH, 2)
            g8 = jnp.concatenate([sent[l, 0][n] for l in layers], axis=1)
        for k, o in zip(("grad", "delta", "m", "v"), _adamw_nd(w[n], g8, m[n], v[n], name="adamw_" + n)):
            out[k][n] = o

    return (loss_total, dx.reshape(x.shape), *[out["grad"][n] for n in _WEIGHTS], *[out["delta"][n] for n in _WEIGHTS],
            *[out["m"][n] for n in _WEIGHTS], *[out["v"][n] for n in _WEIGHTS])
```

```python
import functools
import math

import numpy as np
import jax
import jax.numpy as jnp
from jax import lax
from jax.experimental import pallas as pl
from jax.experimental.pallas import tpu as pltpu

F32 = jnp.float32
BF16 = jnp.bfloat16
HI = lax.Precision.HIGHEST

D_MODEL = 1024
DEPTH = 4
CHUNK = 64
NORM_EPS = 1e-6
CONV_K = 4
D_FF = 2816
GDN_HEADS = 4
GDN_DK = 128
MLA_HEADS = 4
MLA_NOPE = 128
MLA_ROPE = 64
ROPE_THETA = 10000.0
SSD_HEADS = 32
SSD_HEADDIM = 64
SSD_GROUPS = 4
SSD_STATE = 128
SSD_D_INNER = 2048
N_DEV = 8

ADAM_LR = 0.001
ADAM_B1 = 0.9
ADAM_B2 = 0.999
ADAM_EPS = 1e-08
ADAM_WD = 0.01
ADAM_STEP = 10

V7X_VMEM_LIMIT = 56 * 1024 * 1024
ROW_TILE = 512
SEQ_TILE = 256
ATT_TILE = 1024
MM_RESIDENT_BYTES = 12 * 1024 * 1024
FF_HALF = D_FF // 2
LANE = 128

EV_QKV, EV_CQ, EV_MISC, EV_Z, EV_CKV, EV_W = 0, 1536, 1920, 2048, 2560, 2816
OD_Z, OD_XBC, OD_DT, OD_W = 0, 2048, 5120, 5376


def _cparams(sem=None):
    return pltpu.CompilerParams(dimension_semantics=sem, vmem_limit_bytes=V7X_VMEM_LIMIT)


def _pick(n, cands):
    for c in cands:
        if n % c == 0:
            return c
    return n


_DN = {"nn": (((1,), (0,)), ((), ())), "nt": (((1,), (1,)), ((), ())), "tn": (((0,), (0,)), ((), ()))}


F32_3PASS = 2


def _dot(a, b, mode, hi=False):
    if hi:
        prec = lax.Precision.HIGH if hi == F32_3PASS else HI
        return lax.dot_general(a.astype(F32), b.astype(F32), _DN[mode], precision=prec, preferred_element_type=F32)
    return lax.dot_general(a.astype(BF16), b.astype(BF16), _DN[mode], preferred_element_type=F32)


@functools.partial(jax.custom_vjp, nondiff_argnums=(2, 3))
def _mm(a, b, mode, hi):
    return _dot(a, b, mode, hi)


def _mm_fwd(a, b, mode, hi):
    return _dot(a, b, mode, hi), (a, b)


def _mm_bwd(mode, hi, res, g):
    a, b = res
    if mode == "nn":
        return _dot(g, b, "nt", hi), _dot(a, g, "tn", hi)
    if mode == "nt":
        return _dot(g, b, "nn", hi), _dot(g, a, "tn", hi)
    return _dot(b, g, "nt", hi), _dot(a, g, "nn", hi)


_mm.defvjp(_mm_fwd, _mm_bwd)


def _iota2(shape, dim):
    return lax.broadcasted_iota(jnp.int32, shape, dim)


def _row_spec(a, tm):
    if isinstance(a, tuple):
        arr, c0, w = a
        assert c0 % w == 0
        cb = c0 // w
        return arr, pl.BlockSpec((tm, w), lambda i, cb=cb: (i, cb))
    return a, pl.BlockSpec((tm, a.shape[1]), lambda i: (i, 0))


def _full_spec(b):
    return pl.BlockSpec(b.shape, lambda i: (0,) * b.ndim)


def _rows(fn, tiled, bcast, outs, *, name, tm=ROW_TILE, also_t=()):
    arrs, specs = zip(*[_row_spec(a, 0) for a in tiled])
    t_len = arrs[0].shape[0]
    tm = min(tm, t_len)
    arrs, specs = zip(*[_row_spec(a, tm) for a in tiled])
    nt, nb, no = len(tiled), len(bcast), len(outs)

    def body(*refs):
        ins = [r[...].astype(F32) for r in refs[:nt]] + [r[...] for r in refs[nt:nt + nb]]
        res = fn(*ins)
        for r, v in zip(refs[nt + nb:nt + nb + no], res):
            r[...] = v.astype(r.dtype)
        for r, k in zip(refs[nt + nb + no:], also_t):
            r[...] = res[k].T.astype(r.dtype)

    return pl.pallas_call(
        body, grid=(t_len // tm,), name=name,
        in_specs=list(specs) + [_full_spec(b) for b in bcast],
        out_specs=[pl.BlockSpec((tm, c), lambda i: (i, 0)) for c, _ in outs]
        + [pl.BlockSpec((outs[k][0], tm), lambda i: (0, i)) for k in also_t],
        out_shape=[jax.ShapeDtypeStruct((t_len, c), dt) for c, dt in outs]
        + [jax.ShapeDtypeStruct((outs[k][0], t_len), outs[k][1]) for k in also_t],
        compiler_params=_cparams(("parallel",)),
    )(*arrs, *bcast)


def _rows_vjp(fn, tiled, consts, bcast, bconsts, douts, grads, *, name, adds=None, tm=ROW_TILE // 2):
    adds = adds or {}
    t_arrs, t_specs = zip(*[_row_spec(a, 0) for a in tiled])
    t_len = t_arrs[0].shape[0]
    tm = min(tm, t_len)
    rows_in = list(tiled) + list(consts) + list(douts) + [adds[k] for k in sorted(adds)]
    arrs, specs = zip(*[_row_spec(a, tm) for a in rows_in])
    nt, nc, nb, nbc, nd, na = len(tiled), len(consts), len(bcast), len(bconsts), len(douts), len(adds)
    add_pos = {k: j for j, k in enumerate(sorted(adds))}
    want = [j for j, g in enumerate(grads) if g is not None]

    def body(*refs):
        p = 0
        t = [r[...].astype(F32) for r in refs[p:p + nt]]; p += nt
        c = [r[...].astype(F32) for r in refs[p:p + nc]]; p += nc
        d = [r[...].astype(F32) for r in refs[p:p + nd]]; p += nd
        a = [r[...].astype(F32) for r in refs[p:p + na]]; p += na
        b = [r[...] for r in refs[p:p + nb]]; p += nb
        bc = [r[...] for r in refs[p:p + nbc]]; p += nbc
        g_refs = refs[p:p + len(want)]; p += len(want)
        gb_refs = refs[p:p + nb]

        def f(*args):
            return fn(*args[:nt], *c, *args[nt:], *bc)

        _, vjp = jax.vjp(f, *t, *b)
        g = vjp(tuple(d))
        for r, j in zip(g_refs, want):
            val = g[j]
            if j in add_pos:
                val = val + a[add_pos[j]]
            r[...] = val.astype(r.dtype)

        @pl.when(pl.program_id(0) == 0)
        def _():
            for r in gb_refs:
                r[...] = jnp.zeros_like(r)

        for r, val in zip(gb_refs, g[nt:]):
            r[...] += val

    def width(a):
        return a[2] if isinstance(a, tuple) else a.shape[1]

    res = pl.pallas_call(
        body, grid=(t_len // tm,), name=name,
        in_specs=list(specs) + [_full_spec(b) for b in list(bcast) + list(bconsts)],
        out_specs=[pl.BlockSpec((tm, width(tiled[j])), lambda i: (i, 0)) for j in want] + [_full_spec(b) for b in bcast],
        out_shape=[jax.ShapeDtypeStruct((t_len, width(tiled[j])), grads[j]) for j in want]
        + [jax.ShapeDtypeStruct(b.shape, F32) for b in bcast],
        compiler_params=_cparams(("arbitrary",)),
    )(*arrs, *bcast, *bconsts)
    tg = [None] * nt
    for r, j in zip(res[:len(want)], want):
        tg[j] = r
    return tg, list(res[len(want):])


def _matmul(a, b, mode, out_dtype, *, name):
    if mode in ("tn", "kn"):
        assert out_dtype == F32
        k_len, m_len = a.shape if mode == "tn" else a.shape[::-1]
        n_len = b.shape[1]
        tm, tn = m_len, n_len
        while tm * tn * 4 > MM_RESIDENT_BYTES and tn % (2 * LANE) == 0:
            tn //= 2
        tk = _pick(k_len, (512, 256, 128))
        if mode == "kn" and k_len % 1024 == 0 and 4 * 1024 * (tm + tn) + 8 * tm * tn <= 44 * 1024 * 1024:
            tk = 1024
    else:
        m_len, k_len = a.shape
        n_len = b.shape[1] if mode == "nn" else b.shape[0]
        tk, tn = k_len, n_len
        while tk * tn * 2 > MM_RESIDENT_BYTES and tn % (2 * LANE) == 0:
            tn //= 2
        tm = _pick(m_len, (512, 256, 128))
        while tm * max(4 * tn, 2 * tk) > MM_RESIDENT_BYTES // 2 and tm % 256 == 0:
            tm //= 2
    nk = k_len // tk
    if mode == "nn":
        a_spec = pl.BlockSpec((tm, tk), lambda j, i, k: (i, k))
        b_spec = pl.BlockSpec((tk, tn), lambda j, i, k: (k, j))
    elif mode == "nt":
        a_spec = pl.BlockSpec((tm, tk), lambda j, i, k: (i, k))
        b_spec = pl.BlockSpec((tn, tk), lambda j, i, k: (j, k))
    elif mode == "kn":
        a_spec = pl.BlockSpec((tm, tk), lambda j, i, k: (i, k))
        b_spec = pl.BlockSpec((tk, tn), lambda j, i, k: (k, j))
    else:
        a_spec = pl.BlockSpec((tk, tm), lambda j, i, k: (k, i))
        b_spec = pl.BlockSpec((tk, tn), lambda j, i, k: (k, j))

    def body(a_ref, b_ref, o_ref):
        part = _dot(a_ref[...], b_ref[...], "nn" if mode == "kn" else mode)
        if nk == 1:
            o_ref[...] = part.astype(o_ref.dtype)
        else:
            @pl.when(pl.program_id(2) == 0)
            def _():
                o_ref[...] = jnp.zeros_like(o_ref)

            o_ref[...] += part

    return pl.pallas_call(
        body, grid=(n_len // tn, m_len // tm, nk), name=name,
        in_specs=[a_spec, b_spec],
        out_specs=pl.BlockSpec((tm, tn), lambda j, i, k: (i, j)),
        out_shape=jax.ShapeDtypeStruct((m_len, n_len), out_dtype),
        compiler_params=_cparams(("parallel", "parallel", "arbitrary")),
    )(a, b)


def _ffn_act(h, w13, *, name):
    t_len, d = h.shape
    tm = min(ROW_TILE, t_len)

    def body(h_ref, w_ref, s_ref, st_ref):
        ab = _dot(h_ref[...], w_ref[...], "nn")
        a, b = ab[:, :FF_HALF], ab[:, FF_HALF:]
        s = a * jax.nn.sigmoid(a) * b
        s_ref[...] = s.astype(s_ref.dtype)
        st_ref[...] = s.T.astype(st_ref.dtype)

    return pl.pallas_call(
        body, grid=(2, t_len // tm), name=name,
        in_specs=[pl.BlockSpec((tm, d), lambda f, i: (i, 0)), pl.BlockSpec((d, 2 * FF_HALF), lambda f, i: (0, f))],
        out_specs=[pl.BlockSpec((tm, FF_HALF), lambda f, i: (i, f)), pl.BlockSpec((FF_HALF, tm), lambda f, i: (f, i))],
        out_shape=[jax.ShapeDtypeStruct((t_len, D_FF), BF16), jax.ShapeDtypeStruct((D_FF, t_len), BF16)],
        compiler_params=_cparams(("parallel", "parallel")),
    )(h, w13)


def _ffn_act_bwd(h, dy, w13, w2, *, name):
    t_len, d = h.shape
    tm = min(ROW_TILE, t_len)

    def body(h_ref, dy_ref, w_ref, w2_ref, o_ref):
        ab = _dot(h_ref[...], w_ref[...], "nn")
        a, b = ab[:, :FF_HALF], ab[:, FF_HALF:]
        ds = _dot(dy_ref[...], w2_ref[...], "nt")
        sig = jax.nn.sigmoid(a)
        silu = a * sig
        da = ds * b * (sig * (1.0 + a * (1.0 - sig)))
        db = ds * silu
        o_ref[...] = jnp.concatenate([da, db], axis=-1).astype(o_ref.dtype)

    return pl.pallas_call(
        body, grid=(2, t_len // tm), name=name,
        in_specs=[pl.BlockSpec((tm, d), lambda f, i: (i, 0)), pl.BlockSpec((tm, d), lambda f, i: (i, 0)),
                  pl.BlockSpec((d, 2 * FF_HALF), lambda f, i: (0, f)), pl.BlockSpec((FF_HALF, d), lambda f, i: (f, 0))],
        out_specs=pl.BlockSpec((tm, 2 * FF_HALF), lambda f, i: (i, f)),
        out_shape=jax.ShapeDtypeStruct((t_len, 2 * D_FF), BF16),
        compiler_params=_cparams(("parallel", "parallel")),
    )(h, dy, w13, w2)


CONV_CB = 512
HALO = 8


def _conv_fwd(p, c0, n_ch, w, b, *, name):
    t_len = p.shape[0]
    tm = min(ROW_TILE, t_len)
    hb = tm // HALO
    cb0 = c0 // CONV_CB

    def body(x_ref, halo_ref, w_ref, b_ref, act_ref, pre_ref):
        first = pl.program_id(1) == 0
        halo = jnp.where(first, 0.0, halo_ref[...])
        xx = jnp.concatenate([halo, x_ref[...]], axis=0)
        wv = w_ref[...]
        acc = b_ref[...] + wv[CONV_K - 1:CONV_K] * xx[HALO:]
        for j in range(CONV_K - 1):
            acc = acc + wv[j:j + 1] * pltpu.roll(xx, CONV_K - 1 - j, 0)[HALO:]
        pre_ref[...] = acc
        act_ref[...] = acc * jax.nn.sigmoid(acc)

    return pl.pallas_call(
        body, grid=(n_ch // CONV_CB, t_len // tm), name=name,
        in_specs=[pl.BlockSpec((tm, CONV_CB), lambda j, i: (i, cb0 + j)),
                  pl.BlockSpec((HALO, CONV_CB), lambda j, i: (jnp.maximum(i * hb - 1, 0), cb0 + j)),
                  pl.BlockSpec((CONV_K, CONV_CB), lambda j, i: (0, j)),
                  pl.BlockSpec((1, CONV_CB), lambda j, i: (0, j))],
        out_specs=[pl.BlockSpec((tm, CONV_CB), lambda j, i: (i, j))] * 2,
        out_shape=[jax.ShapeDtypeStruct((t_len, n_ch), F32)] * 2,
        compiler_params=_cparams(("parallel", "arbitrary")),
    )(p, p, w, b)


def _conv_bwd(dact, pre, pre_c0, p, p_c0, w, *, name):
    t_len, n_ch = dact.shape
    tm = min(ROW_TILE, t_len)
    hb = tm // HALO
    nt = t_len // tm
    last_hb = t_len // HALO - 1
    cb0 = p_c0 // CONV_CB
    cbp = pre_c0 // CONV_CB

    def dsilu(z):
        sig = jax.nn.sigmoid(z)
        return sig * (1.0 + z * (1.0 - sig))

    def body(d_ref, dn_ref, pre_ref, pren_ref, x_ref, xh_ref, w_ref, dx_ref, dw_ref, db_ref):
        i = pl.program_id(1)
        dpre = d_ref[...] * dsilu(pre_ref[...])
        dnext = jnp.where(i == nt - 1, 0.0, dn_ref[...] * dsilu(pren_ref[...]))
        ext = jnp.concatenate([dpre, dnext], axis=0)
        xx = jnp.concatenate([jnp.where(i == 0, 0.0, xh_ref[...]), x_ref[...]], axis=0)
        wv = w_ref[...]
        dx = wv[CONV_K - 1:CONV_K] * dpre
        for j in range(CONV_K - 1):
            dx = dx + wv[j:j + 1] * pltpu.roll(ext, tm + HALO - (CONV_K - 1 - j), 0)[:tm]
        dx_ref[...] = dx
        dws = [jnp.sum(dpre * pltpu.roll(xx, CONV_K - 1 - j, 0)[HALO:], axis=0, keepdims=True) for j in range(CONV_K - 1)]
        dws.append(jnp.sum(dpre * x_ref[...], axis=0, keepdims=True))

        @pl.when(i == 0)
        def _():
            dw_ref[...] = jnp.zeros_like(dw_ref)
            db_ref[...] = jnp.zeros_like(db_ref)

        dw_ref[...] += jnp.concatenate(dws, axis=0)
        db_ref[...] += jnp.sum(dpre, axis=0, keepdims=True)

    tile = lambda off: pl.BlockSpec((tm, CONV_CB), lambda j, i: (i, off + j))
    nxt = lambda off: pl.BlockSpec((HALO, CONV_CB), lambda j, i: (jnp.minimum((i + 1) * hb, last_hb), off + j))
    return pl.pallas_call(
        body, grid=(n_ch // CONV_CB, nt), name=name,
        in_specs=[tile(0), nxt(0), tile(cbp), nxt(cbp), tile(cb0),
                  pl.BlockSpec((HALO, CONV_CB), lambda j, i: (jnp.maximum(i * hb - 1, 0), cb0 + j)),
                  pl.BlockSpec((CONV_K, CONV_CB), lambda j, i: (0, cbp + j))],
        out_specs=[tile(0), pl.BlockSpec((CONV_K, CONV_CB), lambda j, i: (0, j)), pl.BlockSpec((1, CONV_CB), lambda j, i: (0, j))],
        out_shape=[jax.ShapeDtypeStruct((t_len, n_ch), F32), jax.ShapeDtypeStruct((CONV_K, n_ch), F32),
                   jax.ShapeDtypeStruct((1, n_ch), F32)],
        compiler_params=_cparams(("parallel", "arbitrary")),
    )(dact, dact, pre, pre, p, p, w)


@jax.custom_vjp
def _inv_unit_lower_many(a_cat):
    c = a_cat.shape[0]
    assert LANE % c == 0 and a_cat.shape[1] % LANE == 0
    x = (_iota2(a_cat.shape, 0) == _iota2(a_cat.shape, 1) % c).astype(F32)
    tiles = [a_cat[:, t * LANE:(t + 1) * LANE] for t in range(a_cat.shape[1] // LANE)]
    first = (_iota2((c, LANE), 1) // c) * c
    for j in range(c - 1):
        col = jnp.concatenate([jnp.take_along_axis(t, first + j, axis=1) for t in tiles], axis=-1)
        x = x - col * x[j:j + 1, :]
    return x


def _inv_fwd(a_cat):
    x = _inv_unit_lower_many(a_cat)
    return x, x


def _inv_bwd(x, g):
    c = x.shape[0]
    parts = [-_dot(x[:, s], _dot(g[:, s], x[:, s], "nt", F32_3PASS), "tn", F32_3PASS)
             for s in (slice(i * c, (i + 1) * c) for i in range(x.shape[1] // c))]
    return (jnp.concatenate(parts, axis=-1),)


_inv_unit_lower_many.defvjp(_inv_fwd, _inv_bwd)


def _l2norm(x):
    return x * lax.rsqrt(jnp.sum(x * x, axis=-1, keepdims=True) + NORM_EPS)


def _rms(x):
    return x * lax.rsqrt(jnp.mean(x * x, axis=-1, keepdims=True) + NORM_EPS)


def _tri_masks(c):
    rows, cols = _iota2((c, c), 0), _iota2((c, c), 1)
    return rows >= cols, rows > cols, (rows >= cols).astype(F32), (rows <= cols).astype(F32)


def _gdn_tile(q, k, v, misc, s0, alog, dtb):
    c = CHUNK
    lower, strict, ltri, utri = _tri_masks(c)
    n_chunk = q.shape[0] // c
    pre = []
    for h in range(GDN_HEADS):
        hs = slice(h * LANE, (h + 1) * LANE)
        neg_a = -jnp.exp(alog[:, h:h + 1])
        for ci in range(n_chunk):
            sl = slice(ci * c, (ci + 1) * c)
            qn = _l2norm(q[sl, hs]) * (GDN_DK ** -0.5)
            kn = _l2norm(k[sl, hs])
            beta = jax.nn.sigmoid(misc[sl, 64 + h:65 + h])
            g = neg_a * jax.nn.softplus(misc[sl, 68 + h:69 + h] + dtb[:, h:h + 1])
            gb = jnp.broadcast_to(g, (c, c))
            gc_col = _mm(ltri, gb, "nn", True)
            gc_row = _mm(gb, utri, "tn", True)
            decay = jnp.where(lower, jnp.exp(jnp.where(lower, gc_col - gc_row, 0.0)), 0.0)
            kb = kn * beta
            a_mat = jnp.where(strict, _mm(kb, kn, "nt", False) * decay, 0.0)
            pre.append((qn, kn, kb, v[sl, hs] * beta, decay, gc_col[:, 0:1], gc_col[c - 1:c, 0:1], a_mat))
    t_all = _inv_unit_lower_many(jnp.concatenate([p[7] for p in pre], axis=-1))
    o_heads, s_heads = [], []
    for h in range(GDN_HEADS):
        s = s0[h * GDN_DK:(h + 1) * GDN_DK]
        outs = []
        for ci in range(n_chunk):
            i = h * n_chunk + ci
            qn, kn, kb, vb, decay, gc, g_last, _ = pre[i]
            t_inv = t_all[:, i * c:(i + 1) * c]
            u = _mm(t_inv, vb, "nn", F32_3PASS)
            w = _mm(t_inv, kb * jnp.exp(gc), "nn", F32_3PASS)
            attn = _mm(qn, kn, "nt", False) * decay
            k_end = kn * jnp.exp(g_last - gc)
            q_start = qn * jnp.exp(gc)
            v_new = u - _mm(w, s, "nn", False)
            outs.append(_mm(q_start, s, "nn", False) + _mm(attn, v_new, "nn", False))
            s = s * jnp.exp(g_last) + _mm(k_end, v_new, "tn", False)
        o_heads.append(jnp.concatenate(outs, axis=0))
        s_heads.append(s)
    return jnp.concatenate(o_heads, axis=-1), jnp.concatenate(s_heads, axis=0)


def _gdn_specs(tt, rev_n=None):
    t = (lambda i: i) if rev_n is None else (lambda i: rev_n - 1 - i)
    col = lambda j: pl.BlockSpec((tt, GDN_HEADS * LANE), lambda i: (t(i), j))
    vec = pl.BlockSpec((1, LANE), lambda i: (0, 0))
    misc = pl.BlockSpec((tt, LANE), lambda i: (t(i), EV_MISC // LANE))
    return [col(0), col(1), col(2), misc, vec, vec], t


def _gdn_fwd(act, p, alog, dtb, *, name):
    t_len = act.shape[0]
    tt = min(SEQ_TILE, t_len)
    ntile = t_len // tt
    in_specs, _ = _gdn_specs(tt)

    def body(q_ref, k_ref, v_ref, m_ref, al_ref, dt_ref, o_ref, s_ref, state):
        @pl.when(pl.program_id(0) == 0)
        def _():
            state[...] = jnp.zeros_like(state)

        s_ref[0] = state[...]
        o, s_new = _gdn_tile(q_ref[...], k_ref[...], v_ref[...], m_ref[...], state[...], al_ref[...], dt_ref[...])
        o_ref[...] = o
        state[...] = s_new

    return pl.pallas_call(
        body, grid=(ntile,), name=name, in_specs=in_specs,
        out_specs=[pl.BlockSpec((tt, GDN_HEADS * LANE), lambda i: (i, 0)),
                   pl.BlockSpec((1, GDN_HEADS * GDN_DK, LANE), lambda i: (i, 0, 0))],
        out_shape=[jax.ShapeDtypeStruct((t_len, GDN_HEADS * LANE), F32),
                   jax.ShapeDtypeStruct((ntile, GDN_HEADS * GDN_DK, LANE), F32)],
        scratch_shapes=[pltpu.VMEM((GDN_HEADS * GDN_DK, LANE), F32)],
        compiler_params=_cparams(("arbitrary",)),
    )(act, act, act, p, alog, dtb)


def _gdn_bwd(act, p, alog, dtb, states, do, *, name):
    t_len = act.shape[0]
    tt = min(SEQ_TILE, t_len)
    ntile = t_len // tt
    in_specs, t = _gdn_specs(tt, ntile)

    def body(q_ref, k_ref, v_ref, m_ref, al_ref, dt_ref, s0_ref, do_ref,
             dq_ref, dk_ref, dv_ref, dm_ref, dal_ref, ddt_ref, dstate):
        @pl.when(pl.program_id(0) == 0)
        def _():
            dstate[...] = jnp.zeros_like(dstate)
            dal_ref[...] = jnp.zeros_like(dal_ref)
            ddt_ref[...] = jnp.zeros_like(ddt_ref)

        _, vjp = jax.vjp(_gdn_tile, q_ref[...], k_ref[...], v_ref[...], m_ref[...], s0_ref[0], al_ref[...], dt_ref[...])
        dq, dk, dv, dm, ds0, dal, ddt = vjp((do_ref[...], dstate[...]))
        dq_ref[...] = dq
        dk_ref[...] = dk
        dv_ref[...] = dv
        dm_ref[...] = dm
        dstate[...] = ds0
        dal_ref[...] += dal
        ddt_ref[...] += ddt

    row = pl.BlockSpec((tt, GDN_HEADS * LANE), lambda i: (t(i), 0))
    vec = pl.BlockSpec((1, LANE), lambda i: (0, 0))
    return pl.pallas_call(
        body, grid=(ntile,), name=name,
        in_specs=in_specs + [pl.BlockSpec((1, GDN_HEADS * GDN_DK, LANE), lambda i: (t(i), 0, 0)), row],
        out_specs=[row, row, row, pl.BlockSpec((tt, LANE), lambda i: (t(i), 0)), vec, vec],
        out_shape=[jax.ShapeDtypeStruct((t_len, GDN_HEADS * LANE), F32)] * 3
        + [jax.ShapeDtypeStruct((t_len, LANE), F32)] + [jax.ShapeDtypeStruct((1, LANE), F32)] * 2,
        scratch_shapes=[pltpu.VMEM((GDN_HEADS * GDN_DK, LANE), F32)],
        compiler_params=_cparams(("arbitrary",)),
    )(act, act, act, p, alog, dtb, states, do)


def _head_expand():
    return jnp.asarray(np.repeat(np.eye(LANE, SSD_HEADS, dtype=np.float32), SSD_HEADDIM, axis=1))


@jax.custom_vjp
def _per_head(v, expand):
    rows = max(v.shape[0], 8)
    v8 = jnp.broadcast_to(v, (rows, LANE))
    half = _iota2((rows, LANE), 1) // SSD_HEADDIM
    tiles = [jnp.take_along_axis(v8, half + 2 * j, axis=1) for j in range(SSD_D_INNER // LANE)]
    return jnp.concatenate(tiles, axis=-1)[:v.shape[0]]


def _per_head_fwd(v, expand):
    return _per_head(v, expand), expand


def _per_head_bwd(expand, g):
    rows = g.shape[0]
    g8 = jnp.broadcast_to(g, (8, g.shape[1])) if rows == 1 else g
    dv = lax.dot_general(g8, expand, _DN["nt"], precision=lax.Precision.HIGH, preferred_element_type=F32)
    return dv[0:1] if rows == 1 else dv, None


_per_head.defvjp(_per_head_fwd, _per_head_bwd)


def _ssd_tile(xs, bm, cm, dtr, hs0, alog, dtb, dsk, expand):
    c = CHUNK
    gw = SSD_D_INNER // SSD_GROUPS
    hpg = SSD_HEADS // SSD_GROUPS
    lower, _, ltri, utri = _tri_masks(c)
    half = _iota2((c, LANE), 1) // SSD_HEADDIM
    dt = jax.nn.softplus(dtr + dtb)
    da = dt * (-jnp.exp(alog))
    xdt = xs * _per_head(dt, expand)
    d_x = _per_head(dsk, expand)
    hs = [hs0[g * SSD_STATE:(g + 1) * SSD_STATE] for g in range(SSD_GROUPS)]
    ys = []
    for ci in range(xs.shape[0] // c):
        sl = slice(ci * c, (ci + 1) * c)
        acs = _mm(ltri, da[sl], "nn", True)
        acs_t = _mm(da[sl], utri, "tn", True)
        acs_last = acs[c - 1:c, :]
        e_start = _per_head(jnp.exp(acs), expand)
        e_end = _per_head(jnp.exp(acs_last - acs), expand)
        e_dec = _per_head(jnp.exp(acs_last), expand)
        xdt_c = xdt[sl]
        y_tiles = [None] * (SSD_D_INNER // LANE)
        y_off = []
        for g in range(SSD_GROUPS):
            b_g = bm[sl, g * SSD_STATE:(g + 1) * SSD_STATE]
            c_g = cm[sl, g * SSD_STATE:(g + 1) * SSD_STATE]
            gs = slice(g * gw, (g + 1) * gw)
            cb = _mm(c_g, b_g, "nt", False)
            y_off.append(_mm(c_g, hs[g], "nn", False) * e_start[:, gs])
            for r in range(hpg):
                h = g * hpg + r
                j = h // 2
                lm = jnp.where(lower, jnp.exp(jnp.where(lower, acs[:, h:h + 1] - acs_t[h:h + 1, :], 0.0)), 0.0)
                xm = jnp.where(half == (h % 2), xdt_c[:, j * LANE:(j + 1) * LANE], 0.0)
                part = _mm(cb * lm, xm, "nn", False)
                y_tiles[j] = part if y_tiles[j] is None else y_tiles[j] + part
            hs[g] = hs[g] * e_dec[:, gs] + _mm(b_g, xdt_c[:, gs] * e_end[:, gs], "tn", False)
        ys.append(jnp.concatenate(y_tiles, axis=-1) + jnp.concatenate(y_off, axis=-1) + d_x * xs[sl])
    return jnp.concatenate(ys, axis=0), jnp.concatenate(hs, axis=0)


def _ssd_specs(tt, rev_n=None):
    t = (lambda i: i) if rev_n is None else (lambda i: rev_n - 1 - i)
    vec = pl.BlockSpec((1, LANE), lambda i: (0, 0))
    specs = [pl.BlockSpec((tt, SSD_D_INNER), lambda i: (t(i), 0)),
             pl.BlockSpec((tt, 512), lambda i: (t(i), SSD_D_INNER // 512)),
             pl.BlockSpec((tt, 512), lambda i: (t(i), SSD_D_INNER // 512 + 1)),
             pl.BlockSpec((tt, LANE), lambda i: (t(i), OD_DT // LANE)), vec, vec, vec,
             pl.BlockSpec((LANE, SSD_D_INNER), lambda i: (0, 0))]
    return specs, t


def _ssd_fwd(act, p, alog, dtb, dsk, *, name):
    t_len = act.shape[0]
    tt = min(SEQ_TILE, t_len)
    ntile = t_len // tt
    in_specs, _ = _ssd_specs(tt)

    def body(x_ref, b_ref, c_ref, dt_ref, al_ref, db_ref, dk_ref, e_ref, y_ref, s_ref, state):
        @pl.when(pl.program_id(0) == 0)
        def _():
            state[...] = jnp.zeros_like(state)

        s_ref[0] = state[...]
        y, hs = _ssd_tile(x_ref[...], b_ref[...], c_ref[...], dt_ref[...], state[...], al_ref[...], db_ref[...],
                          dk_ref[...], e_ref[...])
        y_ref[...] = y
        state[...] = hs

    return pl.pallas_call(
        body, grid=(ntile,), name=name, in_specs=in_specs,
        out_specs=[pl.BlockSpec((tt, SSD_D_INNER), lambda i: (i, 0)),
                   pl.BlockSpec((1, SSD_GROUPS * SSD_STATE, 512), lambda i: (i, 0, 0))],
        out_shape=[jax.ShapeDtypeStruct((t_len, SSD_D_INNER), F32),
                   jax.ShapeDtypeStruct((ntile, SSD_GROUPS * SSD_STATE, 512), F32)],
        scratch_shapes=[pltpu.VMEM((SSD_GROUPS * SSD_STATE, 512), F32)],
        compiler_params=_cparams(("arbitrary",)),
    )(act, act, act, p, alog, dtb, dsk, _head_expand())


def _ssd_bwd(act, p, alog, dtb, dsk, states, dy, *, name):
    t_len = act.shape[0]
    tt = min(SEQ_TILE, t_len)
    ntile = t_len // tt
    in_specs, t = _ssd_specs(tt, ntile)

    def body(x_ref, b_ref, c_ref, dt_ref, al_ref, db_ref, dk_ref, e_ref, s0_ref, dy_ref,
             dx_ref, dbm_ref, dcm_ref, ddt_ref, dal_ref, ddb_ref, ddk_ref, dstate):
        @pl.when(pl.program_id(0) == 0)
        def _():
            dstate[...] = jnp.zeros_like(dstate)
            dal_ref[...] = jnp.zeros_like(dal_ref)
            ddb_ref[...] = jnp.zeros_like(ddb_ref)
            ddk_ref[...] = jnp.zeros_like(ddk_ref)

        expand = e_ref[...]

        def f(xs, bm, cm, dtr, hs0, al, db, dk):
            return _ssd_tile(xs, bm, cm, dtr, hs0, al, db, dk, expand)

        _, vjp = jax.vjp(f, x_ref[...], b_ref[...], c_ref[...], dt_ref[...], s0_ref[0], al_ref[...], db_ref[...],
                         dk_ref[...])
        dx, dbm, dcm, ddt, dhs, dal, ddb, ddk = vjp((dy_ref[...], dstate[...]))
        dx_ref[...] = dx
        dbm_ref[...] = dbm
        dcm_ref[...] = dcm
        ddt_ref[...] = ddt
        dstate[...] = dhs
        dal_ref[...] += dal
        ddb_ref[...] += ddb
        ddk_ref[...] += ddk

    vec = pl.BlockSpec((1, LANE), lambda i: (0, 0))
    rows = lambda w: pl.BlockSpec((tt, w), lambda i: (t(i), 0))
    return pl.pallas_call(
        body, grid=(ntile,), name=name,
        in_specs=in_specs + [pl.BlockSpec((1, SSD_GROUPS * SSD_STATE, 512), lambda i: (t(i), 0, 0)), rows(SSD_D_INNER)],
        out_specs=[rows(SSD_D_INNER), rows(512), rows(512), rows(LANE), vec, vec, vec],
        out_shape=[jax.ShapeDtypeStruct((t_len, SSD_D_INNER), F32), jax.ShapeDtypeStruct((t_len, 512), F32),
                   jax.ShapeDtypeStruct((t_len, 512), F32), jax.ShapeDtypeStruct((t_len, LANE), F32)]
        + [jax.ShapeDtypeStruct((1, LANE), F32)] * 3,
        scratch_shapes=[pltpu.VMEM((SSD_GROUPS * SSD_STATE, 512), F32)],
        compiler_params=_cparams(("arbitrary",)),
    )(act, act, act, p, alog, dtb, dsk, _head_expand(), states, dy)


ATT_SCALE = (MLA_NOPE + MLA_ROPE) ** -0.5
ATT_SCALE2 = ATT_SCALE * math.log2(math.e)
QK_W = 2 * LANE


def _chunk_mask(tq):
    return (_iota2((tq, tq), 1) // CHUNK) <= (_iota2((tq, tq), 0) // CHUNK)


def _attn_fwd(qc, kc, vv, *, name):
    t_len = qc.shape[0]
    tq = min(ATT_TILE, t_len)
    nq = t_len // tq

    def body(q_ref, k_ref, v_ref, o_ref, lse_ref, m_s, l_s, acc_s):
        qi, ki = pl.program_id(1), pl.program_id(2)

        @pl.when(ki == 0)
        def _():
            m_s[...] = jnp.full_like(m_s, -jnp.inf)
            l_s[...] = jnp.zeros_like(l_s)
            acc_s[...] = jnp.zeros_like(acc_s)

        def step(masked):
            s = _dot(q_ref[...], k_ref[...], "nt") * ATT_SCALE2
            if masked:
                s = jnp.where(_chunk_mask(tq), s, -jnp.inf)
            m_new = jnp.maximum(m_s[...], jnp.max(s, axis=-1, keepdims=True))
            alpha = jnp.exp2(m_s[...] - m_new)
            p = jnp.exp2(s - m_new)
            l_s[...] = alpha * l_s[...] + jnp.sum(p, axis=-1, keepdims=True)
            acc_s[...] = alpha * acc_s[...] + _dot(p, v_ref[...], "nn")
            m_s[...] = m_new

        @pl.when(ki < qi)
        def _():
            step(False)

        @pl.when(ki == qi)
        def _():
            step(True)
            o_ref[...] = acc_s[...] / l_s[...]
            lse_ref[...] = jnp.broadcast_to(m_s[...] + jnp.log2(l_s[...]), lse_ref.shape)

    kv_idx = lambda h, i, k: (jnp.minimum(k, i), h)
    return pl.pallas_call(
        body, grid=(MLA_HEADS, nq, nq), name=name,
        in_specs=[pl.BlockSpec((tq, QK_W), lambda h, i, k: (i, h)), pl.BlockSpec((tq, QK_W), kv_idx),
                  pl.BlockSpec((tq, LANE), kv_idx)],
        out_specs=[pl.BlockSpec((tq, LANE), lambda h, i, k: (i, h))] * 2,
        out_shape=[jax.ShapeDtypeStruct((t_len, MLA_HEADS * LANE), F32)] * 2,
        scratch_shapes=[pltpu.VMEM((tq, 1), F32), pltpu.VMEM((tq, 1), F32), pltpu.VMEM((tq, LANE), F32)],
        compiler_params=_cparams(("parallel", "parallel", "arbitrary")),
    )(qc, kc, vv)


def _attn_probs(q, k, v, do, o, lse, masked, tq):
    s = _dot(q, k, "nt") * ATT_SCALE2
    if masked:
        s = jnp.where(_chunk_mask(tq), s, -jnp.inf)
    p = jnp.exp2(s - lse[:, 0:1])
    delta = jnp.sum(do * o, axis=-1, keepdims=True)
    ds = p * (_dot(do, v, "nt") - delta)
    return p, ds


def _attn_bwd(qc, kc, vv, o, lse, do, *, name):
    t_len = qc.shape[0]
    tq = min(ATT_TILE, t_len)
    nq = t_len // tq

    def body(q_ref, k_ref, v_ref, o_ref, lse_ref, do_ref, dq_hbm, dk_ref, dv_ref, dq_s, dk_s, dv_s):
        head, ki, qi = pl.program_id(0), pl.program_id(1), pl.program_id(2)
        rows = pl.ds(pl.multiple_of(qi * tq, tq), tq)

        @pl.when(qi == 0)
        def _():
            dk_s[...] = jnp.zeros_like(dk_s)
            dv_s[...] = jnp.zeros_like(dv_s)

        def step(masked):
            p, ds = _attn_probs(q_ref[...], k_ref[...], v_ref[...], do_ref[...], o_ref[...], lse_ref[...], masked, tq)
            dv_s[...] += _dot(p, do_ref[...], "tn")
            dk_s[...] += _dot(ds, q_ref[...], "tn")
            part = _dot(ds, k_ref[...], "nn")

            @pl.when(ki == 0)
            def _():
                dq_s[rows, :] = part

            @pl.when(ki > 0)
            def _():
                dq_s[rows, :] += part

        @pl.when(qi > ki)
        def _():
            step(False)

        @pl.when(qi == ki)
        def _():
            step(True)
            dq_s[rows, :] = dq_s[rows, :] * ATT_SCALE
            pltpu.sync_copy(dq_s.at[rows, :], dq_hbm.at[rows, pl.ds(pl.multiple_of(head * QK_W, QK_W), QK_W)])

        @pl.when(qi == nq - 1)
        def _():
            dk_ref[...] = dk_s[...] * ATT_SCALE
            dv_ref[...] = dv_s[...]

    q_idx = lambda h, k, i: (jnp.maximum(i, k), h)
    k_idx = lambda h, k, i: (k, h)
    return pl.pallas_call(
        body, grid=(MLA_HEADS, nq, nq), name=name,
        in_specs=[pl.BlockSpec((tq, QK_W), q_idx), pl.BlockSpec((tq, QK_W), k_idx), pl.BlockSpec((tq, LANE), k_idx),
                  pl.BlockSpec((tq, LANE), q_idx), pl.BlockSpec((tq, LANE), q_idx), pl.BlockSpec((tq, LANE), q_idx)],
        out_specs=[pl.BlockSpec(memory_space=pl.ANY), pl.BlockSpec((tq, QK_W), k_idx), pl.BlockSpec((tq, LANE), k_idx)],
        out_shape=[jax.ShapeDtypeStruct((t_len, MLA_HEADS * QK_W), F32), jax.ShapeDtypeStruct((t_len, MLA_HEADS * QK_W), F32),
                   jax.ShapeDtypeStruct((t_len, MLA_HEADS * LANE), F32)],
        scratch_shapes=[pltpu.VMEM((t_len, QK_W), F32), pltpu.VMEM((tq, QK_W), F32), pltpu.VMEM((tq, LANE), F32)],
        compiler_params=_cparams(("arbitrary", "arbitrary", "arbitrary")),
    )(qc, kc, vv, o, lse, do)


def _adaln_fn(x, g, shift, scale):
    return ((_rms(x) * g) * (1.0 + scale) + shift,)


def _resid_fn(coef, y, x, gate):
    return (x + coef * gate * y,)


def _rms2_fn(cq, ckv, gq, gkv):
    return _rms(cq) * gq, _rms(ckv) * gkv


@jax.custom_vjp
def _swap_halves(x):
    return jnp.concatenate([x[:, 32:64], x[:, 0:32], x[:, 64:128]], axis=-1)


_swap_halves.defvjp(lambda x: (_swap_halves(x), None), lambda _, g: (_swap_halves(g),))


def _rope_fn(q, kv, misc, pos, invf, sgn):
    ang = pos * invf
    cos, sin = jnp.cos(ang), jnp.sin(ang) * sgn

    def rope(x):
        return x * cos + _swap_halves(x) * sin

    k_pe = rope(jnp.where(_iota2(misc.shape, 1) < MLA_ROPE, misc, 0.0))
    qs, ks = [], []
    for h in range(MLA_HEADS):
        qs += [q[:, h * LANE:(h + 1) * LANE], rope(q[:, (MLA_HEADS + h) * LANE:(MLA_HEADS + h + 1) * LANE])]
        ks += [kv[:, h * LANE:(h + 1) * LANE], k_pe]
    return jnp.concatenate(qs, axis=-1), jnp.concatenate(ks, axis=-1), kv[:, MLA_HEADS * LANE:]


def _ev_out_fn(oa, z, ob, g):
    parts = []
    for h in range(GDN_HEADS):
        hs = slice(h * LANE, (h + 1) * LANE)
        zz = z[:, hs]
        parts.append(_rms(oa[:, hs]) * g * (zz * jax.nn.sigmoid(zz)))
    return (jnp.concatenate(parts + [ob], axis=-1),)


def _od_out_fn(y, z, g):
    yz = y * (z * jax.nn.sigmoid(z))
    gw = SSD_D_INNER // SSD_GROUPS
    return (jnp.concatenate([_rms(yz[:, i * gw:(i + 1) * gw]) for i in range(SSD_GROUPS)], axis=-1) * g,)


def _loss_bwd(x, tgt, g, *, name):
    t_len, d = x.shape
    tm = min(ROW_TILE // 2, t_len)

    def body(x_ref, t_ref, g_ref, loss_ref, dx_ref, dg_ref):
        tgt_v = t_ref[...]

        def f(xv, gv):
            err = _rms(xv) * gv - tgt_v
            return 0.5 * jnp.sum(jnp.mean(err * err, axis=-1, keepdims=True), axis=0, keepdims=True)

        val, vjp = jax.vjp(f, x_ref[...], g_ref[...])
        dx, dg = vjp(jnp.ones((1, 1), F32))
        dx_ref[...] = dx

        @pl.when(pl.program_id(0) == 0)
        def _():
            loss_ref[...] = jnp.zeros_like(loss_ref)
            dg_ref[...] = jnp.zeros_like(dg_ref)

        loss_ref[...] += jnp.broadcast_to(val, loss_ref.shape)
        dg_ref[...] += dg

    row = pl.BlockSpec((tm, d), lambda i: (i, 0))
    return pl.pallas_call(
        body, grid=(t_len // tm,), name=name,
        in_specs=[row, row, pl.BlockSpec((1, d), lambda i: (0, 0))],
        out_specs=[pl.BlockSpec((1, LANE), lambda i: (0, 0)), row, pl.BlockSpec((1, d), lambda i: (0, 0))],
        out_shape=[jax.ShapeDtypeStruct((1, LANE), F32), jax.ShapeDtypeStruct((t_len, d), F32),
                   jax.ShapeDtypeStruct((1, d), F32)],
        compiler_params=_cparams(("arbitrary",)),
    )(x, tgt, g)


def _mesh_pos():
    return lax.axis_index("x"), lax.axis_index("y"), lax.axis_index("c")


def _exchange(xs, scatter, *, name):
    n_arr = len(xs)

    def body(*refs):
        in_refs, out_refs = refs[:n_arr], refs[n_arr:2 * n_arr]
        send_sems, recv_sems, local_sems = refs[2 * n_arr:]
        mx, my, mc = _mesh_pos()
        me = 4 * mx + 2 * my + mc
        started = []
        for a, (in_ref, out_ref) in enumerate(zip(in_refs, out_refs)):
            def src(j, in_ref=in_ref):
                return in_ref.at[j] if scatter else in_ref

            local = pltpu.make_async_copy(src(me), out_ref.at[me], local_sems.at[a])
            local.start()
            started.append((local, None))
            for d in range(1, N_DEV):
                px = 1 - mx if d & 4 else mx
                py = 1 - my if d & 2 else my
                pc = 1 - mc if d & 1 else mc
                peer = 4 * px + 2 * py + pc
                sem = a * (N_DEV - 1) + d - 1
                send = pltpu.make_async_remote_copy(
                    src_ref=src(peer), dst_ref=out_ref.at[me], send_sem=send_sems.at[sem], recv_sem=recv_sems.at[sem],
                    device_id=(px, py, pc), device_id_type=pl.DeviceIdType.MESH)
                send.start()
                recv = pltpu.make_async_remote_copy(
                    src_ref=src(peer), dst_ref=out_ref.at[peer], send_sem=send_sems.at[sem], recv_sem=recv_sems.at[sem],
                    device_id=(px, py, pc), device_id_type=pl.DeviceIdType.MESH)
                started.append((send, recv))
        for first, recv in started:
            if recv is None:
                first.wait()
            else:
                first.wait_send()
                recv.wait_recv()

    blocks = [tuple(x.shape[1:]) if scatter else tuple(x.shape) for x in xs]
    return pl.pallas_call(
        body, name=name,
        in_specs=[pl.BlockSpec(memory_space=pl.ANY)] * n_arr,
        out_specs=[pl.BlockSpec(memory_space=pl.ANY)] * n_arr,
        out_shape=[jax.ShapeDtypeStruct((N_DEV,) + b, x.dtype) for b, x in zip(blocks, xs)],
        scratch_shapes=[pltpu.SemaphoreType.DMA((n_arr * (N_DEV - 1),)), pltpu.SemaphoreType.DMA((n_arr * (N_DEV - 1),)),
                        pltpu.SemaphoreType.DMA((n_arr,))],
        compiler_params=pltpu.CompilerParams(has_side_effects=True),
    )(*xs)


def _peer_of(d, pos):
    mx, my, mc = pos
    px = 1 - mx if d & 4 else mx
    py = 1 - my if d & 2 else my
    pc = 1 - mc if d & 1 else mc
    return (px, py, pc), 4 * px + 2 * py + pc


_HBM = pl.BlockSpec(memory_space=pltpu.HBM)
_SEM = pl.BlockSpec(memory_space=pltpu.SEMAPHORE)


def _exchange_start(xs, scatter, *, name):
    n_arr = len(xs)
    n_sem = n_arr * (N_DEV - 1)

    def body(*refs):
        in_refs, land_refs = refs[:n_arr], refs[n_arr:2 * n_arr]
        send_sems, recv_sems, token = refs[2 * n_arr], refs[2 * n_arr + 1], refs[-1]
        pos = _mesh_pos()
        me = 4 * pos[0] + 2 * pos[1] + pos[2]
        for a in range(n_arr):
            for d in range(1, N_DEV):
                dev, peer = _peer_of(d, pos)
                sem = a * (N_DEV - 1) + d - 1
                pltpu.make_async_remote_copy(
                    src_ref=in_refs[a].at[peer] if scatter else in_refs[a], dst_ref=land_refs[a].at[me],
                    send_sem=send_sems.at[sem], recv_sem=recv_sems.at[sem], device_id=dev,
                    device_id_type=pl.DeviceIdType.MESH).start()
        token[...] = jnp.zeros_like(token)

    blocks = [tuple(x.shape[1:]) if scatter else tuple(x.shape) for x in xs]
    srcs = [pltpu.with_memory_space_constraint(x, pltpu.HBM) for x in xs]
    lands = [pltpu.with_memory_space_constraint(lax.empty((N_DEV,) + b, x.dtype), pltpu.HBM) for b, x in zip(blocks, xs)]
    res = pl.pallas_call(
        body, name=name,
        out_shape=(pltpu.SemaphoreType.DMA((n_sem,)), pltpu.SemaphoreType.DMA((n_sem,)),
                   *[pltpu.HBM(a.shape, a.dtype) for a in srcs + lands], jax.ShapeDtypeStruct((8, LANE), F32)),
        in_specs=[_HBM] * (2 * n_arr),
        out_specs=(_SEM, _SEM, *[_HBM] * (2 * n_arr), pl.BlockSpec(memory_space=pltpu.VMEM)),
        input_output_aliases={i: 2 + i for i in range(2 * n_arr)},
        compiler_params=pltpu.CompilerParams(has_side_effects=pltpu.SideEffectType.DATAFLOW_SIDE_EFFECTING),
    )(*srcs, *lands)
    handle = dict(sems=res[:2], srcs=res[2:2 + n_arr], lands=res[2 + n_arr:2 + 2 * n_arr], scatter=scatter)
    return handle, res[-1][0, 0]


def _exchange_wait(handle, after, me, *, name):
    scatter = handle["scatter"]
    n_arr = len(handle["srcs"])

    def body(*refs):
        in_refs, land_refs = refs[:n_arr], refs[n_arr:2 * n_arr]
        send_sems, recv_sems = refs[2 * n_arr], refs[2 * n_arr + 1]
        pos = _mesh_pos()
        for a in range(n_arr):
            for d in range(1, N_DEV):
                dev, peer = _peer_of(d, pos)
                sem = a * (N_DEV - 1) + d - 1
                copy = pltpu.make_async_remote_copy(
                    src_ref=in_refs[a].at[peer] if scatter else in_refs[a], dst_ref=land_refs[a].at[peer],
                    send_sem=send_sems.at[sem], recv_sem=recv_sems.at[sem], device_id=dev,
                    device_id_type=pl.DeviceIdType.MESH)
                copy.wait_send()
                copy.wait_recv()

    thru = list(handle["srcs"]) + list(handle["lands"])
    res = pl.pallas_call(
        body, name=name,
        out_shape=tuple(pltpu.HBM(a.shape, a.dtype) for a in thru),
        in_specs=[_HBM] * (2 * n_arr) + [_SEM, _SEM, pl.BlockSpec(memory_space=pl.ANY)],
        out_specs=tuple([_HBM] * (2 * n_arr)),
        input_output_aliases={i: i for i in range(2 * n_arr)},
        compiler_params=pltpu.CompilerParams(has_side_effects=pltpu.SideEffectType.DATAFLOW_SIDE_EFFECTING),
    )(*thru, *handle["sems"], after)
    out = []
    for src, land in zip(res[:n_arr], res[n_arr:]):
        own = lax.dynamic_index_in_dim(src, me, axis=0, keepdims=True) if scatter else src[None]
        out.append(lax.dynamic_update_index_in_dim(land, own, me, axis=0))
    return out


def _cols(srcs, rows, plans, out_dtype, *, name):
    n_src = len(srcs)
    rb = _pick(rows, (256, 128, 64, 32, 16, 8))

    def width(pieces):
        return sum(p[1] if p[0] == "z" else p[3] - p[2] for p in pieces)

    def body(*refs):
        ins, outs = refs[:n_src], refs[n_src:]
        loaded = {}
        for o_ref, plan in zip(outs, plans):
            for j, pieces in enumerate(plan):
                vals = []
                for pc in pieces:
                    if pc[0] == "z":
                        vals.append(jnp.zeros((rb, pc[1]), out_dtype))
                    else:
                        si, sj, c0, c1 = pc
                        if (si, sj) not in loaded:
                            loaded[(si, sj)] = ins[si][sj]
                        vals.append(loaded[(si, sj)][:, c0:c1].astype(out_dtype))
                o_ref[j] = vals[0] if len(vals) == 1 else jnp.concatenate(vals, axis=-1)

    for arr, r0 in srcs:
        assert r0 % rb == 0
    return pl.pallas_call(
        body, grid=(rows // rb,), name=name,
        in_specs=[pl.BlockSpec((arr.shape[0], rb, arr.shape[2]), lambda i, r0=r0 // rb: (0, r0 + i, 0)) for arr, r0 in srcs],
        out_specs=[pl.BlockSpec((len(p), rb, width(p[0])), lambda i: (0, i, 0)) for p in plans],
        out_shape=[jax.ShapeDtypeStruct((len(p), rows, width(p[0])), out_dtype) for p in plans],
        compiler_params=_cparams(("parallel",)),
    )(*[arr for arr, _ in srcs])


def _shard_pieces(src, a, b, shard_w):
    out = []
    while a < b:
        s = a // shard_w
        e = min(b, (s + 1) * shard_w)
        out.append((src, s, a - s * shard_w, e - s * shard_w))
        a = e
    return out


def _mapped_pieces(a, b, segs):
    out = []
    for n0, n1, k0 in sorted(segs):
        lo, hi = max(a, n0), min(b, n1)
        if lo < hi:
            out.append((0, 0, k0 + lo - n0, k0 + hi - n0))
    return out


_EV_SEGS = [(0, 1536, EV_QKV), (1536, 2048, EV_Z), (2048, 2056, EV_MISC + MLA_ROPE), (2056, 2440, EV_CQ),
            (2440, 2696, EV_CKV), (2696, 2760, EV_MISC)]
EV_NAT_W, OD_NAT_W = 2760, 5152


PACK_W = 1024


def _adamw(w, gparts, m, v, *, name):
    n_rows, n_cols = w.shape
    n_parts = gparts.shape[0]
    tm = _pick(n_rows, (512, 256, 128, 64, 32, 16, 8))
    while n_parts * tm * n_cols * 4 > 4 * 1024 * 1024 and tm % 16 == 0:
        tm //= 2

    def body(w_ref, g_ref, m_ref, v_ref, go_ref, d_ref, mo_ref, vo_ref):
        g = g_ref[0]
        for j in range(1, n_parts):
            g = g + g_ref[j]
        m_new = ADAM_B1 * m_ref[...] + (1.0 - ADAM_B1) * g
        v_new = ADAM_B2 * v_ref[...] + (1.0 - ADAM_B2) * jnp.square(g)
        m_hat = m_new / (1.0 - ADAM_B1 ** ADAM_STEP)
        v_hat = v_new / (1.0 - ADAM_B2 ** ADAM_STEP)
        go_ref[...] = g
        d_ref[...] = -ADAM_LR * (m_hat / (jnp.sqrt(v_hat) + ADAM_EPS) + ADAM_WD * w_ref[...])
        mo_ref[...] = m_new
        vo_ref[...] = v_new

    row = pl.BlockSpec((tm, n_cols), lambda i: (i, 0))
    return pl.pallas_call(
        body, grid=(n_rows // tm,), name=name,
        in_specs=[row, pl.BlockSpec((n_parts, tm, n_cols), lambda i: (0, i, 0)), row, row],
        out_specs=[row] * 4,
        out_shape=[jax.ShapeDtypeStruct((n_rows, n_cols), F32)] * 4,
        compiler_params=_cparams(("parallel",)),
    )(w, gparts, m, v)


def _adamw_nd(w, gparts, m, v, *, name):
    shape = w.shape
    two = (-1, shape[-1])
    outs = _adamw(w.reshape(two), gparts.reshape((gparts.shape[0],) + (int(np.prod(shape[:-1])), shape[-1])),
                  m.reshape(two), v.reshape(two), name=name)
    return [o.reshape(shape) for o in outs]


def _pack(parts):
    flat = [p.astype(F32).reshape(-1) for p in parts]
    n_pad = -sum(f.shape[0] for f in flat) % (8 * PACK_W)
    return jnp.concatenate(flat + [jnp.zeros((n_pad,), F32)]).reshape(-1, PACK_W)


def _unpack(packed, shapes):
    flat = packed.reshape(-1)
    out, off = [], 0
    for s in shapes:
        n = int(np.prod(s))
        out.append(flat[off:off + n].reshape(tuple(s)))
        off += n
    return out


def _mod_shard(c_all, ada_w, ada_b_shard, *, name):
    n_layer, d, n_col = ada_w.shape

    def body(c_ref, w_ref, b_ref, o_ref):
        cv = c_ref[...]
        o_ref[0] = _dot(cv * jax.nn.sigmoid(cv), w_ref[0], "nn") + b_ref[0]

    return pl.pallas_call(
        body, grid=(n_layer,), name=name,
        in_specs=[pl.BlockSpec((N_DEV, d), lambda l: (0, 0)), pl.BlockSpec((1, d, n_col), lambda l: (l, 0, 0)),
                  pl.BlockSpec((1, 1, n_col), lambda l: (l, 0, 0))],
        out_specs=pl.BlockSpec((1, N_DEV, n_col), lambda l: (l, 0, 0)),
        out_shape=jax.ShapeDtypeStruct((n_layer, N_DEV, n_col), F32),
        compiler_params=_cparams(("parallel",)),
    )(c_all, ada_w, ada_b_shard)


def _ada_w_grad(c_all, dmod_shard, *, name):
    n_layer, _, n_col = dmod_shard.shape
    d = c_all.shape[1]

    def body(c_ref, g_ref, o_ref):
        cv = c_ref[...]
        o_ref[0] = _dot(cv * jax.nn.sigmoid(cv), g_ref[0], "tn", True)

    return pl.pallas_call(
        body, grid=(n_layer,), name=name,
        in_specs=[pl.BlockSpec((N_DEV, d), lambda l: (0, 0)), pl.BlockSpec((1, N_DEV, n_col), lambda l: (l, 0, 0))],
        out_specs=pl.BlockSpec((1, d, n_col), lambda l: (l, 0, 0)),
        out_shape=jax.ShapeDtypeStruct((n_layer, d, n_col), F32),
        compiler_params=_cparams(("parallel",)),
    )(c_all, dmod_shard)


def _uq(w):
    r = w.shape[0]
    rope = jnp.pad(w[:, :, MLA_NOPE:], ((0, 0), (0, 0), (0, LANE - MLA_ROPE)))
    return jnp.concatenate([w[:, :, :MLA_NOPE].reshape(r, -1), rope.reshape(r, -1)], axis=1)


def _uq_back(d):
    r = d.shape[0]
    half = MLA_HEADS * LANE
    return jnp.concatenate([d[:, :half].reshape(r, MLA_HEADS, LANE),
                            d[:, half:].reshape(r, MLA_HEADS, LANE)[:, :, :MLA_ROPE]], axis=-1)


def _ukv(w):
    r = w.shape[0]
    return jnp.concatenate([w[:, :, :MLA_NOPE].reshape(r, -1), w[:, :, MLA_NOPE:].reshape(r, -1)], axis=1)


def _ukv_back(d):
    r = d.shape[0]
    half = MLA_HEADS * LANE
    return jnp.concatenate([d[:, :half].reshape(r, MLA_HEADS, LANE), d[:, half:].reshape(r, MLA_HEADS, LANE)], axis=-1)


def _lane_vec(v):
    return jnp.pad(v.astype(F32), (0, LANE - v.shape[0])).reshape(1, LANE)


def _row(v):
    return v.astype(F32).reshape(1, -1)


def _adaln(x, ln):
    return _rows(_adaln_fn, [x], list(ln), [(D_MODEL, BF16)], name="adaln", also_t=(0,))


def _adaln_bwd(x, ln, dh, dxn):
    (dx,), dln = _rows_vjp(_adaln_fn, [x], [], list(ln), [], [dh], [F32], adds={0: dxn}, name="adaln_bwd", tm=ROW_TILE)
    return dx, dln


def _resid(coef, y, x, gate):
    return _rows(functools.partial(_resid_fn, coef), [y, x], [gate], [(D_MODEL, F32)], name="resid")[0]


def _gated_fn(coef, y, gate):
    return (coef * gate * y,)


def _resid_bwd(coef, y, gate, dxn):
    (dy,), (dgate,) = _rows_vjp(functools.partial(_gated_fn, coef), [y], [], [gate], [], [dxn], [BF16], name="resid_bwd",
                                tm=ROW_TILE)
    return dy, dgate


def _ffn_fwd(x, ln, gate, w13, w2):
    h, ht = _adaln(x, ln)
    s, st = _ffn_act(h, w13, name="ffn_act")
    y = _matmul(s, w2, "nn", F32, name="ffn_down")
    return _resid(0.5, y, x, gate), (x, h, ht, st, y)


def _ffn_bwd(saved, dxn, ln, gate, w13, w2):
    x, h, ht, st, y = saved
    dy, dgate = _resid_bwd(0.5, y, gate, dxn)
    dab = _ffn_act_bwd(h, dy, w13, w2, name="ffn_act_bwd")
    dh = _matmul(dab, w13, "nt", F32, name="ffn_dh")
    dw13 = _matmul(ht, dab, "kn", F32, name="ffn_dw13")
    dw2 = _matmul(st, dy, "kn", F32, name="ffn_dw2")
    dx, dln = _adaln_bwd(x, ln, dh, dxn)
    return dx, dw13, dw2, dln, dgate


def _rope_consts():
    half = MLA_ROPE // 2
    inv = (ROPE_THETA ** (-jnp.arange(half, dtype=F32) / half)).astype(F32)
    zeros = jnp.zeros((LANE - MLA_ROPE,), F32)
    invf = jnp.concatenate([inv, inv, zeros]).reshape(1, LANE)
    sgn = jnp.concatenate([-jnp.ones((half,), F32), jnp.ones((half,), F32), zeros]).reshape(1, LANE)
    return invf, sgn


def _even_fwd(x, pos, ln, gate, wt):
    h, ht = _adaln(x, ln)
    p = _matmul(h, wt["w_in"], "nn", F32, name="ev_in")
    act, pre = _conv_fwd(p, EV_QKV, 1536, wt["conv_w"], jnp.zeros((1, 1536), F32), name="ev_conv")
    o_a, states = _gdn_fwd(act, p, wt["alog"], wt["dtb"], name="gdn_fwd")
    cqn, ckvn = _rows(_rms2_fn, [(p, EV_CQ, 384), (p, EV_CKV, 256)], [wt["gq"], wt["gkv"]],
                      [(384, BF16), (256, BF16)], name="mla_rms")
    q = _matmul(cqn, wt["w_uq"], "nn", F32, name="mla_uq")
    kv = _matmul(ckvn, wt["w_ukv"], "nn", F32, name="mla_ukv")
    invf, sgn = _rope_consts()
    qc, kc, vv = _rows(_rope_fn, [q, kv, (p, EV_MISC, LANE), pos], [invf, sgn],
                       [(1024, BF16), (1024, BF16), (512, BF16)], name="mla_rope", tm=ROW_TILE // 2)
    o_b, lse = _attn_fwd(qc, kc, vv, name="attn_fwd")
    o, ot = _rows(_ev_out_fn, [o_a, (p, EV_Z, 512), o_b], [wt["gdn_g"]], [(1024, BF16)], name="ev_out", also_t=(0,))
    y = _matmul(o, wt["w_out"], "nn", F32, name="ev_wout")
    return _resid(1.0, y, x, gate), (x, ht, p, act, pre, states, cqn, ckvn, q, kv, qc, kc, vv, o_a, o_b, lse, ot, y)


def _cat_fn(*parts):
    return (jnp.concatenate(parts, axis=-1),)


def _ev_dp_fn(dx0, dx1, dx2, dcq, dm_r, dm_g, dz, dckv):
    return (jnp.concatenate([dx0, dx1, dx2, dcq, dm_r + dm_g, dz, dckv], axis=-1),)


def _even_bwd(saved, dxn, pos, ln, gate, wt):
    x, ht, p, act, pre, states, cqn, ckvn, q, kv, qc, kc, vv, o_a, o_b, lse, ot, y = saved
    g = {}
    dy, g["gate"] = _resid_bwd(1.0, y, gate, dxn)
    do = _matmul(dy, wt["w_out"], "nt", F32, name="ev_dwout_x")
    g["w_out"] = _matmul(ot, dy, "kn", F32, name="ev_dwout_w")
    (d_oa, dz, d_ob), (g["gdn_g"],) = _rows_vjp(_ev_out_fn, [o_a, (p, EV_Z, 512), o_b], [], [wt["gdn_g"]], [], [do],
                                                [F32, F32, F32], name="ev_out_bwd")
    dqc, dkc, dvv = _attn_bwd(qc, kc, vv, o_b, lse, d_ob, name="attn_bwd")
    invf, sgn = _rope_consts()
    (dq, dkv, dm_r), _ = _rows_vjp(_rope_fn, [q, kv, (p, EV_MISC, LANE)], [pos], [], [invf, sgn], [dqc, dkc, dvv],
                                   [BF16, BF16, F32], name="mla_rope_bwd", tm=ROW_TILE // 4)
    dcqn = _matmul(dq, wt["w_uq"], "nt", F32, name="mla_duq_x")
    g["w_uq"] = _matmul(cqn, dq, "tn", F32, name="mla_duq_w")
    dckvn = _matmul(dkv, wt["w_ukv"], "nt", F32, name="mla_dukv_x")
    g["w_ukv"] = _matmul(ckvn, dkv, "tn", F32, name="mla_dukv_w")
    (dcq, dckv), (g["gq"], g["gkv"]) = _rows_vjp(_rms2_fn, [(p, EV_CQ, 384), (p, EV_CKV, 256)], [],
                                                 [wt["gq"], wt["gkv"]], [], [dcqn, dckvn], [F32, F32], name="mla_rms_bwd")
    dq_g, dk_g, dv_g, dm_g, g["alog"], g["dtb"] = _gdn_bwd(act, p, wt["alog"], wt["dtb"], states, d_oa, name="gdn_bwd")
    dxs, dws = [], []
    for j, d in enumerate((dq_g, dk_g, dv_g)):
        dxj, dwj, _ = _conv_bwd(d, pre, 512 * j, p, EV_QKV + 512 * j, wt["conv_w"], name="ev_conv_bwd")
        dxs.append(dxj)
        dws.append(dwj)
    g["conv_w"] = jnp.concatenate(dws, axis=1)
    (dp,) = _rows(_ev_dp_fn, dxs + [dcq, dm_r, dm_g, dz, dckv], [],
                  [(EV_W, BF16)], name="ev_dp", tm=ROW_TILE // 2)
    dh = _matmul(dp, wt["w_in"], "nt", F32, name="ev_din_x")
    g["w_in"] = _matmul(ht, dp, "kn", F32, name="ev_din_w")
    dx, g["ln"] = _adaln_bwd(x, ln, dh, dxn)
    return dx, g


def _odd_fwd(x, ln, gate, wt):
    h, ht = _adaln(x, ln)
    p = _matmul(h, wt["w_in"], "nn", F32, name="od_in")
    act, pre = _conv_fwd(p, OD_XBC, 3072, wt["conv_w"], wt["conv_b"], name="od_conv")
    ys, states = _ssd_fwd(act, p, wt["alog"], wt["dtb"], wt["dsk"], name="ssd_fwd")
    o, ot = _rows(_od_out_fn, [ys, (p, OD_Z, 2048)], [wt["norm_g"]], [(SSD_D_INNER, BF16)], name="od_out",
                  tm=ROW_TILE // 2, also_t=(0,))
    y = _matmul(o, wt["w_out"], "nn", F32, name="od_wout")
    return _resid(1.0, y, x, gate), (x, ht, p, act, pre, states, ys, ot, y)


def _od_dp_fn(dz, dxx, dxb, dxc, ddt):
    return (jnp.concatenate([dz, dxx, dxb, dxc, ddt, jnp.zeros_like(ddt)], axis=-1),)


def _odd_bwd(saved, dxn, ln, gate, wt):
    x, ht, p, act, pre, states, ys, ot, y = saved
    g = {}
    dy, g["gate"] = _resid_bwd(1.0, y, gate, dxn)
    do = _matmul(dy, wt["w_out"], "nt", F32, name="od_dwout_x")
    g["w_out"] = _matmul(ot, dy, "kn", F32, name="od_dwout_w")
    (dys, dz), (g["norm_g"],) = _rows_vjp(_od_out_fn, [ys, (p, OD_Z, 2048)], [], [wt["norm_g"]], [], [do], [F32, F32],
                                          name="od_out_bwd", tm=ROW_TILE // 4)
    dxs, dbm, dcm, ddt, g["alog"], g["dtb"], g["dsk"] = _ssd_bwd(act, p, wt["alog"], wt["dtb"], wt["dsk"], states, dys,
                                                                 name="ssd_bwd")
    dins, dws, dbs = [], [], []
    for d, c0 in ((dxs, 0), (dbm, 2048), (dcm, 2560)):
        dxj, dwj, dbj = _conv_bwd(d, pre, c0, p, OD_XBC + c0, wt["conv_w"], name="od_conv_bwd")
        dins.append(dxj)
        dws.append(dwj)
        dbs.append(dbj)
    g["conv_w"] = jnp.concatenate(dws, axis=1)
    g["conv_b"] = jnp.concatenate(dbs, axis=1)
    (dp,) = _rows(_od_dp_fn, [dz] + dins + [ddt], [], [(OD_W, BF16)], name="od_dp", tm=ROW_TILE // 4)
    dh = _matmul(dp, wt["w_in"], "nt", F32, name="od_din_x")
    g["w_in"] = _matmul(ht, dp, "kn", F32, name="od_din_w")
    dx, g["ln"] = _adaln_bwd(x, ln, dh, dxn)
    return dx, g


def _local_step(x, tgt, pos, mod, final_g, layer_weights, layer_done):
    mod = mod.reshape(DEPTH, 3, 3, 1, D_MODEL)
    wts = []

    def ln_of(l, i):
        return (_row(wts[l]["norm_g"][i]), mod[l, i, 0], mod[l, i, 1])

    def mixer_w(l):
        p = wts[l]
        if l % 2 == 0:
            return dict(w_in=p["w_in_k"], conv_w=p["gdn_conv_w"].astype(F32),
                        alog=_lane_vec(p["gdn_A_log"]), dtb=_lane_vec(p["gdn_dt_bias"]),
                        gdn_g=_row(p["gdn_norm_g"]), gq=_row(p["mla_q_norm_g"]), gkv=_row(p["mla_kv_norm_g"]),
                        w_uq=_uq(p["mla_w_uq"]), w_ukv=_ukv(p["mla_w_ukv"]), w_out=p["ev_w_out"])
        return dict(w_in=p["w_in_k"], conv_w=p["ssd_conv_w"].astype(F32),
                    conv_b=_row(p["ssd_conv_b"]), alog=_lane_vec(p["ssd_A_log"]),
                    dtb=_lane_vec(p["ssd_dt_bias"]), dsk=_lane_vec(p["ssd_D"]),
                    norm_g=_row(p["ssd_norm_g"]), w_out=p["ssd_w_out"])

    saved = []
    for l in range(DEPTH):
        wts.append(layer_weights(l, x))
        x, s0 = _ffn_fwd(x, ln_of(l, 0), mod[l, 0, 2], wts[l]["w13"][0], wts[l]["w2"][0])
        if l % 2 == 0:
            x, s1 = _even_fwd(x, pos, ln_of(l, 1), mod[l, 1, 2], mixer_w(l))
        else:
            x, s1 = _odd_fwd(x, ln_of(l, 1), mod[l, 1, 2], mixer_w(l))
        x, s2 = _ffn_fwd(x, ln_of(l, 2), mod[l, 2, 2], wts[l]["w13"][1], wts[l]["w2"][1])
        saved.append((s0, s1, s2))

    loss, dx, d_final_g = _loss_bwd(x, tgt, _row(final_g), name="loss")

    repl = {k: [None] * (DEPTH // 2) for k in ("gdn_A_log", "gdn_dt_bias", "gdn_norm_g", "mla_q_norm_g", "mla_kv_norm_g",
                                                "ssd_A_log", "ssd_dt_bias", "ssd_D")}
    dmod = [None] * DEPTH
    token = None
    for l in reversed(range(DEPTH)):
        s0, s1, s2 = saved[l]
        e = l // 2
        gl = {"w13": [None] * 2, "w2": [None] * 2}
        dg, dsh, dsc, dgt = [None] * 3, [None] * 3, [None] * 3, [None] * 3
        gate2 = mod[l, 2, 2] if token is None else mod[l, 2, 2] + token
        dx, gl["w13"][1], gl["w2"][1], (dg[2], dsh[2], dsc[2]), dgt[2] = _ffn_bwd(
            s2, dx, ln_of(l, 2), gate2, wts[l]["w13"][1], wts[l]["w2"][1])
        if l % 2 == 0:
            dx, g = _even_bwd(s1, dx, pos, ln_of(l, 1), mod[l, 1, 2], mixer_w(l))
            gl.update(w_in_k=g["w_in"], gdn_conv_w=g["conv_w"], mla_w_uq=_uq_back(g["w_uq"]),
                      mla_w_ukv=_ukv_back(g["w_ukv"]), ev_w_out=g["w_out"])
            repl["gdn_A_log"][e] = g["alog"][0, :GDN_HEADS]
            repl["gdn_dt_bias"][e] = g["dtb"][0, :GDN_HEADS]
            repl["gdn_norm_g"][e] = g["gdn_g"][0]
            repl["mla_q_norm_g"][e] = g["gq"][0]
            repl["mla_kv_norm_g"][e] = g["gkv"][0]
        else:
            dx, g = _odd_bwd(s1, dx, ln_of(l, 1), mod[l, 1, 2], mixer_w(l))
            gl.update(w_in_k=g["w_in"], ssd_conv_w=g["conv_w"], ssd_conv_b=g["conv_b"][0], ssd_norm_g=g["norm_g"][0],
                      ssd_w_out=g["w_out"])
            repl["ssd_A_log"][e] = g["alog"][0, :SSD_HEADS]
            repl["ssd_dt_bias"][e] = g["dtb"][0, :SSD_HEADS]
            repl["ssd_D"][e] = g["dsk"][0, :SSD_HEADS]
        dg[1], dsh[1], dsc[1] = g["ln"]
        dgt[1] = g["gate"]
        gl.update(w13=gl["w13"][1], w2=gl["w2"][1])
        token = layer_done(l, 0, gl, dx)
        gate0 = mod[l, 0, 2] if token is None else mod[l, 0, 2] + token
        dx, dw13, dw2, (dg[0], dsh[0], dsc[0]), dgt[0] = _ffn_bwd(s0, dx, ln_of(l, 0), gate0, wts[l]["w13"][0], wts[l]["w2"][0])
        dmod[l] = jnp.concatenate([jnp.concatenate([dsh[i], dsc[i], dgt[i]], axis=1) for i in range(3)], axis=1)[0]
        token = layer_done(l, 1, dict(w13=dw13, w2=dw2, norm_g=jnp.concatenate(dg, axis=0)), dx)

    grads = {k: jnp.stack(v) for k, v in repl.items()}
    grads["final_g"] = d_final_g[0]
    return loss, dx, grads, jnp.stack(dmod)


_WEIGHTS = ("ada_w", "ada_b", "norm_g", "ffn_w1", "ffn_w3", "ffn_w2", "ev_w_in", "gdn_conv_w", "gdn_A_log", "gdn_dt_bias",
            "gdn_norm_g", "mla_q_norm_g", "mla_w_uq", "mla_kv_norm_g", "mla_w_ukv", "ev_w_out", "ssd_w_in", "ssd_conv_w",
            "ssd_conv_b", "ssd_A_log", "ssd_dt_bias", "ssd_D", "ssd_norm_g", "ssd_w_out", "final_g")
_BIG = {"ffn_w1": 3, "ffn_w3": 3, "ffn_w2": 2, "ev_w_in": 2, "mla_w_uq": 1, "mla_w_ukv": 1, "ev_w_out": 1, "ssd_w_in": 2,
        "ssd_w_out": 1}
_SMALL = {"norm_g": 2, "gdn_conv_w": 2, "ssd_conv_w": 2, "ssd_conv_b": 1, "ssd_norm_g": 1}
_REPL = ("ada_b", "gdn_A_log", "gdn_dt_bias", "gdn_norm_g", "mla_q_norm_g", "mla_kv_norm_g", "ssd_A_log", "ssd_dt_bias",
         "ssd_D", "final_g")


def _join(pieces, axis):
    moved = jnp.moveaxis(pieces, 0, axis)
    shape = moved.shape
    return moved.reshape(shape[:axis] + (shape[axis] * shape[axis + 1],) + shape[axis + 2:])


def _split(full, axis):
    shape = full.shape
    return jnp.moveaxis(full.reshape(shape[:axis] + (N_DEV, shape[axis] // N_DEV) + shape[axis + 1:]), axis, 0)


def kernel(x, c, positions, ada_w, ada_b, norm_g, ffn_w1, ffn_w3, ffn_w2, ev_w_in, gdn_conv_w, gdn_A_log, gdn_dt_bias, gdn_norm_g, mla_q_norm_g, mla_w_uq, mla_kv_norm_g, mla_w_ukv, ev_w_out, ssd_w_in, ssd_conv_w, ssd_conv_b, ssd_A_log, ssd_dt_bias, ssd_D, ssd_norm_g, ssd_w_out, final_g, loss_target, m_ada_w, m_ada_b, m_norm_g, m_ffn_w1, m_ffn_w3, m_ffn_w2, m_ev_w_in, m_gdn_conv_w, m_gdn_A_log, m_gdn_dt_bias, m_gdn_norm_g, m_mla_q_norm_g, m_mla_w_uq, m_mla_kv_norm_g, m_mla_w_ukv, m_ev_w_out, m_ssd_w_in, m_ssd_conv_w, m_ssd_conv_b, m_ssd_A_log, m_ssd_dt_bias, m_ssd_D, m_ssd_norm_g, m_ssd_w_out, m_final_g, v_ada_w, v_ada_b, v_norm_g, v_ffn_w1, v_ffn_w3, v_ffn_w2, v_ev_w_in, v_gdn_conv_w, v_gdn_A_log, v_gdn_dt_bias, v_gdn_norm_g, v_mla_q_norm_g, v_mla_w_uq, v_mla_kv_norm_g, v_mla_w_ukv, v_ev_w_out, v_ssd_w_in, v_ssd_conv_w, v_ssd_conv_b, v_ssd_A_log, v_ssd_dt_bias, v_ssd_D, v_ssd_norm_g, v_ssd_w_out, v_final_g):
    a = dict(locals())
    w = {n: a[n] for n in _WEIGHTS}
    m = {n: a["m_" + n] for n in _WEIGHTS}
    v = {n: a["v_" + n] for n in _WEIGHTS}
    mx, my, mc = _mesh_pos()
    me = 4 * mx + 2 * my + mc
    t_len = x.shape[1]
    shards = range(N_DEV)

    small_names, big_names = list(_SMALL), list(_BIG)
    axis_of = {**_SMALL, **_BIG}
    small_g = _exchange([c] + [w[n] for n in small_names], False, name="gather_small")
    c_all = small_g[0].reshape(N_DEV, D_MODEL)
    fw = {n: _join(p, _SMALL[n]) for n, p in zip(small_names, small_g[1:])}
    first_names = [n for n in big_names if not n.startswith("ssd")]
    first_g = dict(zip(first_names, _exchange([w[n][:1].astype(BF16) for n in first_names], False, name="gather_first")))
    rest_src = {n: (w[n] if n.startswith("ssd") else w[n][1:]).astype(BF16) for n in big_names}
    rest_handle, rest_token = _exchange_start([rest_src[n] for n in big_names], False, name="gather_rest_start")
    rest_g = {}

    fs, es, os_ = ffn_w1.shape[3], ev_w_in.shape[2], ssd_w_in.shape[2]
    half = range(N_DEV // 2)
    plan13 = [[[(0, s, 0, fs) for s in half] + [(1, s, 0, fs) for s in half]
               + [(0, s + 4, 0, fs) for s in half] + [(1, s + 4, 0, fs) for s in half]]]
    plan_ev, k_at = [], 0
    for n0, n1, k0 in sorted(_EV_SEGS, key=lambda seg: seg[2]):
        if k0 > k_at:
            plan_ev.append(("z", k0 - k_at))
        plan_ev += _shard_pieces(0, n0, n1, es)
        k_at = k0 + n1 - n0
    assert k_at == EV_W and es * N_DEV == EV_NAT_W and os_ * N_DEV == OD_NAT_W
    plan_od = [[_shard_pieces(0, 0, OD_NAT_W, os_) + [("z", OD_W - OD_NAT_W)]]]

    def layer_weights(l, x_in):
        if l == 1:
            rest_g.update(zip(big_names, _exchange_wait(rest_handle, x_in, me, name="gather_rest_wait")))
        e = l // 2
        src = first_g if l == 0 else rest_g
        i = 0 if l == 0 else l - 1
        ie = 0 if (l == 0 or l % 2) else e - 1
        g1 = src["ffn_w1"].reshape(N_DEV, -1, fs)
        g3 = src["ffn_w3"].reshape(N_DEV, -1, fs)
        p = {"norm_g": fw["norm_g"][l],
             "w13": [_cols([(g1, (2 * i + j) * D_MODEL), (g3, (2 * i + j) * D_MODEL)], D_MODEL, plan13, BF16,
                           name="join_w13")[0][0] for j in range(2)],
             "w2": [src["ffn_w2"][:, i, j].reshape(D_FF, D_MODEL) for j in range(2)]}
        if l % 2 == 0:
            p["w_in_k"] = _cols([(src["ev_w_in"].reshape(N_DEV, -1, es), ie * D_MODEL)], D_MODEL, [[plan_ev]], BF16,
                                name="join_ev_in")[0][0]
            for n in ("mla_w_uq", "mla_w_ukv", "ev_w_out"):
                p[n] = _join(src[n][:, ie], axis_of[n] - 1)
            p["gdn_conv_w"] = fw["gdn_conv_w"][e]
            for n in ("gdn_A_log", "gdn_dt_bias", "gdn_norm_g", "mla_q_norm_g", "mla_kv_norm_g"):
                p[n] = w[n][e]
        else:
            p["w_in_k"] = _cols([(rest_g["ssd_w_in"].reshape(N_DEV, -1, os_), e * D_MODEL)], D_MODEL, plan_od, BF16,
                                name="join_od_in")[0][0]
            p["ssd_w_out"] = _join(rest_g["ssd_w_out"][:, e], axis_of["ssd_w_out"] - 1)
            for n in ("ssd_conv_w", "ssd_conv_b", "ssd_norm_g"):
                p[n] = fw[n][e]
            for n in ("ssd_A_log", "ssd_dt_bias", "ssd_D"):
                p[n] = w[n][e]
        return p

    def w13_cols(s, third):
        k0 = (s % 4) * fs + (2 * FF_HALF if s >= 4 else 0) + (FF_HALF if third else 0)
        return [(0, 0, k0, k0 + fs)]

    sent = {}

    def layer_done(l, stage, gl, dx_l):
        d1, d3 = _cols([(gl["w13"][None], 0)], D_MODEL, [[w13_cols(s, False) for s in shards],
                                                         [w13_cols(s, True) for s in shards]], F32, name="split_w13")
        pieces = {"ffn_w1": d1[:, None, None], "ffn_w3": d3[:, None, None],
                  "ffn_w2": gl["w2"].reshape(N_DEV, -1, D_MODEL)[:, None, None]}
        if stage == 1:
            pieces["norm_g"] = _split(gl["norm_g"][None], 2)
        elif l % 2 == 0:
            pieces["ev_w_in"] = _cols([(gl["w_in_k"][None], 0)], D_MODEL,
                                      [[_mapped_pieces(s * es, (s + 1) * es, _EV_SEGS) for s in shards]], F32,
                                      name="split_ev_in")[0][:, None]
            for n in ("gdn_conv_w", "mla_w_uq", "mla_w_ukv", "ev_w_out"):
                pieces[n] = _split(gl[n][None], axis_of[n])
        else:
            pieces["ssd_w_in"] = _cols([(gl["w_in_k"][None], 0)], D_MODEL,
                                       [[[(0, 0, s * os_, (s + 1) * os_)] for s in shards]], F32,
                                       name="split_od_in")[0][:, None]
            for n in ("ssd_conv_w", "ssd_conv_b", "ssd_norm_g", "ssd_w_out"):
                pieces[n] = _split(gl[n][None], axis_of[n])
        names = list(pieces)
        if l == 0 and stage == 1:
            sent[l, stage] = dict(zip(names, _exchange([pieces[n] for n in names], True, name="scatter_last")))
            return None
        handle, token = _exchange_start([pieces[n] for n in names], True, name=f"scatter_start_{l}_{stage}")
        sent[l, stage] = (names, handle)
        return token

    n_col = ada_w.shape[2]
    ada_b_shard = lax.dynamic_slice(ada_b, (0, me * n_col), (DEPTH, n_col)).reshape(DEPTH, 1, n_col)
    mod_all = _exchange([_mod_shard(c_all, ada_w, ada_b_shard, name="mod")], False, name="gather_mod")[0]
    mod_me = lax.dynamic_index_in_dim(mod_all, me, axis=2, keepdims=False)
    mod = jnp.transpose(mod_me, (1, 0, 2)).reshape(DEPTH, N_DEV * n_col) + rest_token

    pos = positions.astype(F32).reshape(t_len, 1)
    loss, dx, grads, dmod = _local_step(x[0], loss_target[0], pos, mod, final_g, layer_weights, layer_done)
    for (l, stage), started in list(sent.items()):
        if not isinstance(started, dict):
            names, handle = started
            sent[l, stage] = dict(zip(names, _exchange_wait(handle, dx, me, name=f"scatter_wait_{l}_{stage}")))

    repl_shapes = [w[n].shape for n in _REPL] + [(1,)]
    parts8 = _exchange([_pack([dmod] + [grads[n] for n in _REPL[1:]] + [loss[0, :1]])], False, name="gather_repl")[0]
    zero = jnp.zeros((1,), F32)
    r_grad, r_delta, r_m, r_v = [
        _unpack(o, repl_shapes) for o in _adamw(_pack([w[n] for n in _REPL] + [zero]), parts8,
                                                _pack([m[n] for n in _REPL] + [zero]),
                                                _pack([v[n] for n in _REPL] + [zero]), name="adamw_repl")]
    out = {"grad": {}, "delta": {}, "m": {}, "v": {}}
    for i, n in enumerate(_REPL):
        out["grad"][n], out["delta"][n], out["m"][n], out["v"][n] = r_grad[i], r_delta[i], r_m[i], r_v[i]
    loss_total = r_grad[-1].reshape(())

    dmod_all = parts8[:, :dmod.size // PACK_W].reshape((N_DEV,) + dmod.shape)
    dmod_cols = jnp.transpose(lax.dynamic_slice_in_dim(dmod_all, me * n_col, n_col, axis=2), (1, 0, 2))
    g_ada = _ada_w_grad(c_all, dmod_cols, name="ada_w_grad")
    for k, o in zip(("grad", "delta", "m", "v"), _adamw_nd(ada_w, g_ada[None], m["ada_w"], v["ada_w"], name="adamw_ada")):
        out[k]["ada_w"] = o

    for n in small_names + big_names:
        if n == "norm_g":
            g8 = jnp.concatenate([sent[l, 1][n] for l in range(DEPTH)], axis=1)
        elif n.startswith("ffn"):
            g8 = jnp.concatenate([jnp.concatenate([sent[l, 1][n], sent[l, 0][n]], axis=2) for l in range(DEPTH)], axis=1)
        else:
            layers = range(1, DEPTH, 2) if n.startswith("ssd") else range(0, DEPTH, 2)
            g8 = jnp.concatenate([sent[l, 0][n] for l in layers], axis=1)
        for k, o in zip(("grad", "delta", "m", "v"), _adamw_nd(w[n], g8, m[n], v[n], name="adamw_" + n)):
            out[k][n] = o

    return (loss_total, dx.reshape(x.shape), *[out["grad"][n] for n in _WEIGHTS], *[out["delta"][n] for n in _WEIGHTS],
            *[out["m"][n] for n in _WEIGHTS], *[out["v"][n] for n in _WEIGHTS])
```

```python
import functools
import math

import numpy as np
import jax
import jax.numpy as jnp
from jax import lax
from jax.experimental import pallas as pl
from jax.experimental.pallas import tpu as pltpu

F32 = jnp.float32
BF16 = jnp.bfloat16
HI = lax.Precision.HIGHEST

D_MODEL = 1024
DEPTH = 4
CHUNK = 64
NORM_EPS = 1e-6
CONV_K = 4
D_FF = 2816
GDN_HEADS = 4
GDN_DK = 128
MLA_HEADS = 4
MLA_NOPE = 128
MLA_ROPE = 64
ROPE_THETA = 10000.0
SSD_HEADS = 32
SSD_HEADDIM = 64
SSD_GROUPS = 4
SSD_STATE = 128
SSD_D_INNER = 2048
N_DEV = 8

ADAM_LR = 0.001
ADAM_B1 = 0.9
ADAM_B2 = 0.999
ADAM_EPS = 1e-08
ADAM_WD = 0.01
ADAM_STEP = 10

V7X_VMEM_LIMIT = 56 * 1024 * 1024
ROW_TILE = 512
SEQ_TILE = 256
ATT_TILE = 1024
MM_RESIDENT_BYTES = 12 * 1024 * 1024
FF_HALF = D_FF // 2
LANE = 128

EV_QKV, EV_CQ, EV_MISC, EV_Z, EV_CKV, EV_W = 0, 1536, 1920, 2048, 2560, 2816
OD_Z, OD_XBC, OD_DT, OD_W = 0, 2048, 5120, 5376


def _cparams(sem=None):
    return pltpu.CompilerParams(dimension_semantics=sem, vmem_limit_bytes=V7X_VMEM_LIMIT)


def _pick(n, cands):
    for c in cands:
        if n % c == 0:
            return c
    return n


_DN = {"nn": (((1,), (0,)), ((), ())), "nt": (((1,), (1,)), ((), ())), "tn": (((0,), (0,)), ((), ()))}


F32_3PASS = 2


def _dot(a, b, mode, hi=False):
    if hi:
        prec = lax.Precision.HIGH if hi == F32_3PASS else HI
        return lax.dot_general(a.astype(F32), b.astype(F32), _DN[mode], precision=prec, preferred_element_type=F32)
    return lax.dot_general(a.astype(BF16), b.astype(BF16), _DN[mode], preferred_element_type=F32)


@functools.partial(jax.custom_vjp, nondiff_argnums=(2, 3))
def _mm(a, b, mode, hi):
    return _dot(a, b, mode, hi)


def _mm_fwd(a, b, mode, hi):
    return _dot(a, b, mode, hi), (a, b)


def _mm_bwd(mode, hi, res, g):
    a, b = res
    if mode == "nn":
        return _dot(g, b, "nt", hi), _dot(a, g, "tn", hi)
    if mode == "nt":
        return _dot(g, b, "nn", hi), _dot(g, a, "tn", hi)
    return _dot(b, g, "nt", hi), _dot(a, g, "nn", hi)


_mm.defvjp(_mm_fwd, _mm_bwd)


def _iota2(shape, dim):
    return lax.broadcasted_iota(jnp.int32, shape, dim)


def _row_spec(a, tm):
    if isinstance(a, tuple):
        arr, c0, w = a
        assert c0 % w == 0
        cb = c0 // w
        return arr, pl.BlockSpec((tm, w), lambda i, cb=cb: (i, cb))
    return a, pl.BlockSpec((tm, a.shape[1]), lambda i: (i, 0))


def _full_spec(b):
    return pl.BlockSpec(b.shape, lambda i: (0,) * b.ndim)


def _rows(fn, tiled, bcast, outs, *, name, tm=ROW_TILE, also_t=()):
    arrs, specs = zip(*[_row_spec(a, 0) for a in tiled])
    t_len = arrs[0].shape[0]
    tm = min(tm, t_len)
    arrs, specs = zip(*[_row_spec(a, tm) for a in tiled])
    nt, nb, no = len(tiled), len(bcast), len(outs)

    def body(*refs):
        ins = [r[...].astype(F32) for r in refs[:nt]] + [r[...] for r in refs[nt:nt + nb]]
        res = fn(*ins)
        for r, v in zip(refs[nt + nb:nt + nb + no], res):
            r[...] = v.astype(r.dtype)
        for r, k in zip(refs[nt + nb + no:], also_t):
            r[...] = res[k].T.astype(r.dtype)

    return pl.pallas_call(
        body, grid=(t_len // tm,), name=name,
        in_specs=list(specs) + [_full_spec(b) for b in bcast],
        out_specs=[pl.BlockSpec((tm, c), lambda i: (i, 0)) for c, _ in outs]
        + [pl.BlockSpec((outs[k][0], tm), lambda i: (0, i)) for k in also_t],
        out_shape=[jax.ShapeDtypeStruct((t_len, c), dt) for c, dt in outs]
        + [jax.ShapeDtypeStruct((outs[k][0], t_len), outs[k][1]) for k in also_t],
        compiler_params=_cparams(("parallel",)),
    )(*arrs, *bcast)


def _rows_vjp(fn, tiled, consts, bcast, bconsts, douts, grads, *, name, adds=None, tm=ROW_TILE // 2):
    adds = adds or {}
    t_arrs, t_specs = zip(*[_row_spec(a, 0) for a in tiled])
    t_len = t_arrs[0].shape[0]
    tm = min(tm, t_len)
    rows_in = list(tiled) + list(consts) + list(douts) + [adds[k] for k in sorted(adds)]
    arrs, specs = zip(*[_row_spec(a, tm) for a in rows_in])
    nt, nc, nb, nbc, nd, na = len(tiled), len(consts), len(bcast), len(bconsts), len(douts), len(adds)
    add_pos = {k: j for j, k in enumerate(sorted(adds))}
    want = [j for j, g in enumerate(grads) if g is not None]

    def body(*refs):
        p = 0
        t = [r[...].astype(F32) for r in refs[p:p + nt]]; p += nt
        c = [r[...].astype(F32) for r in refs[p:p + nc]]; p += nc
        d = [r[...].astype(F32) for r in refs[p:p + nd]]; p += nd
        a = [r[...].astype(F32) for r in refs[p:p + na]]; p += na
        b = [r[...] for r in refs[p:p + nb]]; p += nb
        bc = [r[...] for r in refs[p:p + nbc]]; p += nbc
        g_refs = refs[p:p + len(want)]; p += len(want)
        gb_refs = refs[p:p + nb]

        def f(*args):
            return fn(*args[:nt], *c, *args[nt:], *bc)

        _, vjp = jax.vjp(f, *t, *b)
        g = vjp(tuple(d))
        for r, j in zip(g_refs, want):
            val = g[j]
            if j in add_pos:
                val = val + a[add_pos[j]]
            r[...] = val.astype(r.dtype)

        @pl.when(pl.program_id(0) == 0)
        def _():
            for r in gb_refs:
                r[...] = jnp.zeros_like(r)

        for r, val in zip(gb_refs, g[nt:]):
            r[...] += val

    def width(a):
        return a[2] if isinstance(a, tuple) else a.shape[1]

    res = pl.pallas_call(
        body, grid=(t_len // tm,), name=name,
        in_specs=list(specs) + [_full_spec(b) for b in list(bcast) + list(bconsts)],
        out_specs=[pl.BlockSpec((tm, width(tiled[j])), lambda i: (i, 0)) for j in want] + [_full_spec(b) for b in bcast],
        out_shape=[jax.ShapeDtypeStruct((t_len, width(tiled[j])), grads[j]) for j in want]
        + [jax.ShapeDtypeStruct(b.shape, F32) for b in bcast],
        compiler_params=_cparams(("arbitrary",)),
    )(*arrs, *bcast, *bconsts)
    tg = [None] * nt
    for r, j in zip(res[:len(want)], want):
        tg[j] = r
    return tg, list(res[len(want):])


def _matmul(a, b, mode, out_dtype, *, name):
    if mode in ("tn", "kn"):
        assert out_dtype == F32
        k_len, m_len = a.shape if mode == "tn" else a.shape[::-1]
        n_len = b.shape[1]
        tm, tn = m_len, n_len
        while tm * tn * 4 > MM_RESIDENT_BYTES and tn % (2 * LANE) == 0:
            tn //= 2
        tk = _pick(k_len, (512, 256, 128))
        if mode == "kn" and k_len % 1024 == 0 and 4 * 1024 * (tm + tn) + 8 * tm * tn <= 44 * 1024 * 1024:
            tk = 1024
    else:
        m_len, k_len = a.shape
        n_len = b.shape[1] if mode == "nn" else b.shape[0]
        tk, tn = k_len, n_len
        while tk * tn * 2 > MM_RESIDENT_BYTES and tn % (2 * LANE) == 0:
            tn //= 2
        tm = _pick(m_len, (512, 256, 128))
        while tm * max(4 * tn, 2 * tk) > MM_RESIDENT_BYTES // 2 and tm % 256 == 0:
            tm //= 2
    nk = k_len // tk
    if mode == "nn":
        a_spec = pl.BlockSpec((tm, tk), lambda j, i, k: (i, k))
        b_spec = pl.BlockSpec((tk, tn), lambda j, i, k: (k, j))
    elif mode == "nt":
        a_spec = pl.BlockSpec((tm, tk), lambda j, i, k: (i, k))
        b_spec = pl.BlockSpec((tn, tk), lambda j, i, k: (j, k))
    elif mode == "kn":
        a_spec = pl.BlockSpec((tm, tk), lambda j, i, k: (i, k))
        b_spec = pl.BlockSpec((tk, tn), lambda j, i, k: (k, j))
    else:
        a_spec = pl.BlockSpec((tk, tm), lambda j, i, k: (k, i))
        b_spec = pl.BlockSpec((tk, tn), lambda j, i, k: (k, j))

    def body(a_ref, b_ref, o_ref):
        part = _dot(a_ref[...], b_ref[...], "nn" if mode == "kn" else mode)
        if nk == 1:
            o_ref[...] = part.astype(o_ref.dtype)
        else:
            @pl.when(pl.program_id(2) == 0)
            def _():
                o_ref[...] = jnp.zeros_like(o_ref)

            o_ref[...] += part

    return pl.pallas_call(
        body, grid=(n_len // tn, m_len // tm, nk), name=name,
        in_specs=[a_spec, b_spec],
        out_specs=pl.BlockSpec((tm, tn), lambda j, i, k: (i, j)),
        out_shape=jax.ShapeDtypeStruct((m_len, n_len), out_dtype),
        compiler_params=_cparams(("parallel", "parallel", "arbitrary")),
    )(a, b)


def _ffn_act(h, w13, *, name):
    t_len, d = h.shape
    tm = min(ROW_TILE, t_len)

    def body(h_ref, w_ref, s_ref, st_ref, ab_ref):
        ab = _dot(h_ref[...], w_ref[...], "nn")
        a, b = ab[:, :FF_HALF], ab[:, FF_HALF:]
        s = a * jax.nn.sigmoid(a) * b
        s_ref[...] = s.astype(s_ref.dtype)
        st_ref[...] = s.T.astype(st_ref.dtype)
        ab_ref[...] = ab.astype(ab_ref.dtype)

    return pl.pallas_call(
        body, grid=(2, t_len // tm), name=name,
        in_specs=[pl.BlockSpec((tm, d), lambda f, i: (i, 0)), pl.BlockSpec((d, 2 * FF_HALF), lambda f, i: (0, f))],
        out_specs=[pl.BlockSpec((tm, FF_HALF), lambda f, i: (i, f)), pl.BlockSpec((FF_HALF, tm), lambda f, i: (f, i)),
                   pl.BlockSpec((tm, 2 * FF_HALF), lambda f, i: (i, f))],
        out_shape=[jax.ShapeDtypeStruct((t_len, D_FF), BF16), jax.ShapeDtypeStruct((D_FF, t_len), BF16),
                   jax.ShapeDtypeStruct((t_len, 2 * D_FF), BF16)],
        compiler_params=_cparams(("parallel", "parallel")),
    )(h, w13)


def _ffn_act_bwd(ab, dy, w2, *, name):
    t_len = ab.shape[0]
    d = dy.shape[1]
    tm = min(ROW_TILE, t_len)

    def body(ab_ref, dy_ref, w2_ref, o_ref):
        abv = ab_ref[...].astype(F32)
        a, b = abv[:, :FF_HALF], abv[:, FF_HALF:]
        ds = _dot(dy_ref[...], w2_ref[...], "nt")
        sig = jax.nn.sigmoid(a)
        silu = a * sig
        da = ds * b * (sig * (1.0 + a * (1.0 - sig)))
        db = ds * silu
        o_ref[...] = jnp.concatenate([da, db], axis=-1).astype(o_ref.dtype)

    return pl.pallas_call(
        body, grid=(2, t_len // tm), name=name,
        in_specs=[pl.BlockSpec((tm, 2 * FF_HALF), lambda f, i: (i, f)), pl.BlockSpec((tm, d), lambda f, i: (i, 0)),
                  pl.BlockSpec((FF_HALF, d), lambda f, i: (f, 0))],
        out_specs=pl.BlockSpec((tm, 2 * FF_HALF), lambda f, i: (i, f)),
        out_shape=jax.ShapeDtypeStruct((t_len, 2 * D_FF), BF16),
        compiler_params=_cparams(("parallel", "parallel")),
    )(ab, dy, w2)


CONV_CB = 512
HALO = 8


def _conv_fwd(p, c0, n_ch, w, b, *, name):
    t_len = p.shape[0]
    tm = min(ROW_TILE, t_len)
    hb = tm // HALO
    cb0 = c0 // CONV_CB

    def body(x_ref, halo_ref, w_ref, b_ref, act_ref, pre_ref):
        first = pl.program_id(1) == 0
        halo = jnp.where(first, 0.0, halo_ref[...])
        xx = jnp.concatenate([halo, x_ref[...]], axis=0)
        wv = w_ref[...]
        acc = b_ref[...] + wv[CONV_K - 1:CONV_K] * xx[HALO:]
        for j in range(CONV_K - 1):
            acc = acc + wv[j:j + 1] * pltpu.roll(xx, CONV_K - 1 - j, 0)[HALO:]
        pre_ref[...] = acc
        act_ref[...] = acc * jax.nn.sigmoid(acc)

    return pl.pallas_call(
        body, grid=(n_ch // CONV_CB, t_len // tm), name=name,
        in_specs=[pl.BlockSpec((tm, CONV_CB), lambda j, i: (i, cb0 + j)),
                  pl.BlockSpec((HALO, CONV_CB), lambda j, i: (jnp.maximum(i * hb - 1, 0), cb0 + j)),
                  pl.BlockSpec((CONV_K, CONV_CB), lambda j, i: (0, j)),
                  pl.BlockSpec((1, CONV_CB), lambda j, i: (0, j))],
        out_specs=[pl.BlockSpec((tm, CONV_CB), lambda j, i: (i, j))] * 2,
        out_shape=[jax.ShapeDtypeStruct((t_len, n_ch), F32)] * 2,
        compiler_params=_cparams(("parallel", "arbitrary")),
    )(p, p, w, b)


def _conv_bwd(dact, pre, pre_c0, p, p_c0, w, *, name):
    t_len, n_ch = dact.shape
    tm = min(ROW_TILE, t_len)
    hb = tm // HALO
    nt = t_len // tm
    last_hb = t_len // HALO - 1
    cb0 = p_c0 // CONV_CB
    cbp = pre_c0 // CONV_CB

    def dsilu(z):
        sig = jax.nn.sigmoid(z)
        return sig * (1.0 + z * (1.0 - sig))

    def body(d_ref, dn_ref, pre_ref, pren_ref, x_ref, xh_ref, w_ref, dx_ref, dw_ref, db_ref):
        i = pl.program_id(1)
        dpre = d_ref[...] * dsilu(pre_ref[...])
        dnext = jnp.where(i == nt - 1, 0.0, dn_ref[...] * dsilu(pren_ref[...]))
        ext = jnp.concatenate([dpre, dnext], axis=0)
        xx = jnp.concatenate([jnp.where(i == 0, 0.0, xh_ref[...]), x_ref[...]], axis=0)
        wv = w_ref[...]
        dx = wv[CONV_K - 1:CONV_K] * dpre
        for j in range(CONV_K - 1):
            dx = dx + wv[j:j + 1] * pltpu.roll(ext, tm + HALO - (CONV_K - 1 - j), 0)[:tm]
        dx_ref[...] = dx
        dws = [jnp.sum(dpre * pltpu.roll(xx, CONV_K - 1 - j, 0)[HALO:], axis=0, keepdims=True) for j in range(CONV_K - 1)]
        dws.append(jnp.sum(dpre * x_ref[...], axis=0, keepdims=True))

        @pl.when(i == 0)
        def _():
            dw_ref[...] = jnp.zeros_like(dw_ref)
            db_ref[...] = jnp.zeros_like(db_ref)

        dw_ref[...] += jnp.concatenate(dws, axis=0)
        db_ref[...] += jnp.sum(dpre, axis=0, keepdims=True)

    tile = lambda off: pl.BlockSpec((tm, CONV_CB), lambda j, i: (i, off + j))
    nxt = lambda off: pl.BlockSpec((HALO, CONV_CB), lambda j, i: (jnp.minimum((i + 1) * hb, last_hb), off + j))
    return pl.pallas_call(
        body, grid=(n_ch // CONV_CB, nt), name=name,
        in_specs=[tile(0), nxt(0), tile(cbp), nxt(cbp), tile(cb0),
                  pl.BlockSpec((HALO, CONV_CB), lambda j, i: (jnp.maximum(i * hb - 1, 0), cb0 + j)),
                  pl.BlockSpec((CONV_K, CONV_CB), lambda j, i: (0, cbp + j))],
        out_specs=[tile(0), pl.BlockSpec((CONV_K, CONV_CB), lambda j, i: (0, j)), pl.BlockSpec((1, CONV_CB), lambda j, i: (0, j))],
        out_shape=[jax.ShapeDtypeStruct((t_len, n_ch), F32), jax.ShapeDtypeStruct((CONV_K, n_ch), F32),
                   jax.ShapeDtypeStruct((1, n_ch), F32)],
        compiler_params=_cparams(("parallel", "arbitrary")),
    )(dact, dact, pre, pre, p, p, w)


@jax.custom_vjp
def _inv_unit_lower_many(a_cat):
    c = a_cat.shape[0]
    assert LANE % c == 0 and a_cat.shape[1] % LANE == 0
    x = (_iota2(a_cat.shape, 0) == _iota2(a_cat.shape, 1) % c).astype(F32)
    tiles = [a_cat[:, t * LANE:(t + 1) * LANE] for t in range(a_cat.shape[1] // LANE)]
    first = (_iota2((c, LANE), 1) // c) * c
    for j in range(c - 1):
        col = jnp.concatenate([jnp.take_along_axis(t, first + j, axis=1) for t in tiles], axis=-1)
        x = x - col * x[j:j + 1, :]
    return x


def _inv_fwd(a_cat):
    x = _inv_unit_lower_many(a_cat)
    return x, x


def _inv_bwd(x, g):
    c = x.shape[0]
    parts = [-_dot(x[:, s], _dot(g[:, s], x[:, s], "nt", F32_3PASS), "tn", F32_3PASS)
             for s in (slice(i * c, (i + 1) * c) for i in range(x.shape[1] // c))]
    return (jnp.concatenate(parts, axis=-1),)


_inv_unit_lower_many.defvjp(_inv_fwd, _inv_bwd)


def _l2norm(x):
    return x * lax.rsqrt(jnp.sum(x * x, axis=-1, keepdims=True) + NORM_EPS)


def _rms(x):
    return x * lax.rsqrt(jnp.mean(x * x, axis=-1, keepdims=True) + NORM_EPS)


def _tri_masks(c):
    rows, cols = _iota2((c, c), 0), _iota2((c, c), 1)
    return rows >= cols, rows > cols, (rows >= cols).astype(F32), (rows <= cols).astype(F32)


def _gdn_tile(q, k, v, misc, s0, alog, dtb):
    c = CHUNK
    lower, strict, ltri, utri = _tri_masks(c)
    n_chunk = q.shape[0] // c
    pre = []
    for h in range(GDN_HEADS):
        hs = slice(h * LANE, (h + 1) * LANE)
        neg_a = -jnp.exp(alog[:, h:h + 1])
        for ci in range(n_chunk):
            sl = slice(ci * c, (ci + 1) * c)
            qn = _l2norm(q[sl, hs]) * (GDN_DK ** -0.5)
            kn = _l2norm(k[sl, hs])
            beta = jax.nn.sigmoid(misc[sl, 64 + h:65 + h])
            g = neg_a * jax.nn.softplus(misc[sl, 68 + h:69 + h] + dtb[:, h:h + 1])
            gb = jnp.broadcast_to(g, (c, c))
            gc_col = _mm(ltri, gb, "nn", True)
            gc_row = _mm(gb, utri, "tn", True)
            decay = jnp.where(lower, jnp.exp(jnp.where(lower, gc_col - gc_row, 0.0)), 0.0)
            kb = kn * beta
            a_mat = jnp.where(strict, _mm(kb, kn, "nt", False) * decay, 0.0)
            pre.append((qn, kn, kb, v[sl, hs] * beta, decay, gc_col[:, 0:1], gc_col[c - 1:c, 0:1], a_mat))
    t_all = _inv_unit_lower_many(jnp.concatenate([p[7] for p in pre], axis=-1))
    o_heads, s_heads = [], []
    for h in range(GDN_HEADS):
        s = s0[h * GDN_DK:(h + 1) * GDN_DK]
        outs = []
        for ci in range(n_chunk):
            i = h * n_chunk + ci
            qn, kn, kb, vb, decay, gc, g_last, _ = pre[i]
            t_inv = t_all[:, i * c:(i + 1) * c]
            u = _mm(t_inv, vb, "nn", F32_3PASS)
            w = _mm(t_inv, kb * jnp.exp(gc), "nn", F32_3PASS)
            attn = _mm(qn, kn, "nt", False) * decay
            k_end = kn * jnp.exp(g_last - gc)
            q_start = qn * jnp.exp(gc)
            v_new = u - _mm(w, s, "nn", False)
            outs.append(_mm(q_start, s, "nn", False) + _mm(attn, v_new, "nn", False))
            s = s * jnp.exp(g_last) + _mm(k_end, v_new, "tn", False)
        o_heads.append(jnp.concatenate(outs, axis=0))
        s_heads.append(s)
    return jnp.concatenate(o_heads, axis=-1), jnp.concatenate(s_heads, axis=0)


def _gdn_specs(tt, rev_n=None):
    t = (lambda i: i) if rev_n is None else (lambda i: rev_n - 1 - i)
    col = lambda j: pl.BlockSpec((tt, GDN_HEADS * LANE), lambda i: (t(i), j))
    vec = pl.BlockSpec((1, LANE), lambda i: (0, 0))
    misc = pl.BlockSpec((tt, LANE), lambda i: (t(i), EV_MISC // LANE))
    return [col(0), col(1), col(2), misc, vec, vec], t


def _gdn_fwd(act, p, alog, dtb, *, name):
    t_len = act.shape[0]
    tt = min(SEQ_TILE, t_len)
    ntile = t_len // tt
    in_specs, _ = _gdn_specs(tt)

    def body(q_ref, k_ref, v_ref, m_ref, al_ref, dt_ref, o_ref, s_ref, state):
        @pl.when(pl.program_id(0) == 0)
        def _():
            state[...] = jnp.zeros_like(state)

        s_ref[0] = state[...]
        o, s_new = _gdn_tile(q_ref[...], k_ref[...], v_ref[...], m_ref[...], state[...], al_ref[...], dt_ref[...])
        o_ref[...] = o
        state[...] = s_new

    return pl.pallas_call(
        body, grid=(ntile,), name=name, in_specs=in_specs,
        out_specs=[pl.BlockSpec((tt, GDN_HEADS * LANE), lambda i: (i, 0)),
                   pl.BlockSpec((1, GDN_HEADS * GDN_DK, LANE), lambda i: (i, 0, 0))],
        out_shape=[jax.ShapeDtypeStruct((t_len, GDN_HEADS * LANE), F32),
                   jax.ShapeDtypeStruct((ntile, GDN_HEADS * GDN_DK, LANE), F32)],
        scratch_shapes=[pltpu.VMEM((GDN_HEADS * GDN_DK, LANE), F32)],
        compiler_params=_cparams(("arbitrary",)),
    )(act, act, act, p, alog, dtb)


def _gdn_bwd(act, p, alog, dtb, states, do, *, name):
    t_len = act.shape[0]
    tt = min(SEQ_TILE, t_len)
    ntile = t_len // tt
    in_specs, t = _gdn_specs(tt, ntile)

    def body(q_ref, k_ref, v_ref, m_ref, al_ref, dt_ref, s0_ref, do_ref,
             dq_ref, dk_ref, dv_ref, dm_ref, dal_ref, ddt_ref, dstate):
        @pl.when(pl.program_id(0) == 0)
        def _():
            dstate[...] = jnp.zeros_like(dstate)
            dal_ref[...] = jnp.zeros_like(dal_ref)
            ddt_ref[...] = jnp.zeros_like(ddt_ref)

        _, vjp = jax.vjp(_gdn_tile, q_ref[...], k_ref[...], v_ref[...], m_ref[...], s0_ref[0], al_ref[...], dt_ref[...])
        dq, dk, dv, dm, ds0, dal, ddt = vjp((do_ref[...], dstate[...]))
        dq_ref[...] = dq
        dk_ref[...] = dk
        dv_ref[...] = dv
        dm_ref[...] = dm
        dstate[...] = ds0
        dal_ref[...] += dal
        ddt_ref[...] += ddt

    row = pl.BlockSpec((tt, GDN_HEADS * LANE), lambda i: (t(i), 0))
    vec = pl.BlockSpec((1, LANE), lambda i: (0, 0))
    return pl.pallas_call(
        body, grid=(ntile,), name=name,
        in_specs=in_specs + [pl.BlockSpec((1, GDN_HEADS * GDN_DK, LANE), lambda i: (t(i), 0, 0)), row],
        out_specs=[row, row, row, pl.BlockSpec((tt, LANE), lambda i: (t(i), 0)), vec, vec],
        out_shape=[jax.ShapeDtypeStruct((t_len, GDN_HEADS * LANE), F32)] * 3
        + [jax.ShapeDtypeStruct((t_len, LANE), F32)] + [jax.ShapeDtypeStruct((1, LANE), F32)] * 2,
        scratch_shapes=[pltpu.VMEM((GDN_HEADS * GDN_DK, LANE), F32)],
        compiler_params=_cparams(("arbitrary",)),
    )(act, act, act, p, alog, dtb, states, do)


def _head_expand():
    return jnp.asarray(np.repeat(np.eye(LANE, SSD_HEADS, dtype=np.float32), SSD_HEADDIM, axis=1))


@jax.custom_vjp
def _per_head(v, expand):
    rows = max(v.shape[0], 8)
    v8 = jnp.broadcast_to(v, (rows, LANE))
    half = _iota2((rows, LANE), 1) // SSD_HEADDIM
    tiles = [jnp.take_along_axis(v8, half + 2 * j, axis=1) for j in range(SSD_D_INNER // LANE)]
    return jnp.concatenate(tiles, axis=-1)[:v.shape[0]]


def _per_head_fwd(v, expand):
    return _per_head(v, expand), expand


def _per_head_bwd(expand, g):
    rows = g.shape[0]
    g8 = jnp.broadcast_to(g, (8, g.shape[1])) if rows == 1 else g
    dv = lax.dot_general(g8, expand, _DN["nt"], precision=lax.Precision.HIGH, preferred_element_type=F32)
    return dv[0:1] if rows == 1 else dv, None


_per_head.defvjp(_per_head_fwd, _per_head_bwd)


def _ssd_tile(xs, bm, cm, dtr, hs0, alog, dtb, dsk, expand):
    c = CHUNK
    gw = SSD_D_INNER // SSD_GROUPS
    hpg = SSD_HEADS // SSD_GROUPS
    lower, _, ltri, utri = _tri_masks(c)
    half = _iota2((c, LANE), 1) // SSD_HEADDIM
    dt = jax.nn.softplus(dtr + dtb)
    da = dt * (-jnp.exp(alog))
    xdt = xs * _per_head(dt, expand)
    d_x = _per_head(dsk, expand)
    hs = [hs0[g * SSD_STATE:(g + 1) * SSD_STATE] for g in range(SSD_GROUPS)]
    ys = []
    for ci in range(xs.shape[0] // c):
        sl = slice(ci * c, (ci + 1) * c)
        acs = _mm(ltri, da[sl], "nn", True)
        acs_t = _mm(da[sl], utri, "tn", True)
        acs_last = acs[c - 1:c, :]
        e_start = _per_head(jnp.exp(acs), expand)
        e_end = _per_head(jnp.exp(acs_last - acs), expand)
        e_dec = _per_head(jnp.exp(acs_last), expand)
        xdt_c = xdt[sl]
        y_tiles = [None] * (SSD_D_INNER // LANE)
        y_off = []
        for g in range(SSD_GROUPS):
            b_g = bm[sl, g * SSD_STATE:(g + 1) * SSD_STATE]
            c_g = cm[sl, g * SSD_STATE:(g + 1) * SSD_STATE]
            gs = slice(g * gw, (g + 1) * gw)
            cb = _mm(c_g, b_g, "nt", False)
            y_off.append(_mm(c_g, hs[g], "nn", False) * e_start[:, gs])
            for r in range(hpg):
                h = g * hpg + r
                j = h // 2
                lm = jnp.where(lower, jnp.exp(jnp.where(lower, acs[:, h:h + 1] - acs_t[h:h + 1, :], 0.0)), 0.0)
                xm = jnp.where(half == (h % 2), xdt_c[:, j * LANE:(j + 1) * LANE], 0.0)
                part = _mm(cb * lm, xm, "nn", False)
                y_tiles[j] = part if y_tiles[j] is None else y_tiles[j] + part
            hs[g] = hs[g] * e_dec[:, gs] + _mm(b_g, xdt_c[:, gs] * e_end[:, gs], "tn", False)
        ys.append(jnp.concatenate(y_tiles, axis=-1) + jnp.concatenate(y_off, axis=-1) + d_x * xs[sl])
    return jnp.concatenate(ys, axis=0), jnp.concatenate(hs, axis=0)


def _ssd_specs(tt, rev_n=None):
    t = (lambda i: i) if rev_n is None else (lambda i: rev_n - 1 - i)
    vec = pl.BlockSpec((1, LANE), lambda i: (0, 0))
    specs = [pl.BlockSpec((tt, SSD_D_INNER), lambda i: (t(i), 0)),
             pl.BlockSpec((tt, 512), lambda i: (t(i), SSD_D_INNER // 512)),
             pl.BlockSpec((tt, 512), lambda i: (t(i), SSD_D_INNER // 512 + 1)),
             pl.BlockSpec((tt, LANE), lambda i: (t(i), OD_DT // LANE)), vec, vec, vec,
             pl.BlockSpec((LANE, SSD_D_INNER), lambda i: (0, 0))]
    return specs, t


def _ssd_fwd(act, p, alog, dtb, dsk, *, name):
    t_len = act.shape[0]
    tt = min(SEQ_TILE, t_len)
    ntile = t_len // tt
    in_specs, _ = _ssd_specs(tt)

    def body(x_ref, b_ref, c_ref, dt_ref, al_ref, db_ref, dk_ref, e_ref, y_ref, s_ref, state):
        @pl.when(pl.program_id(0) == 0)
        def _():
            state[...] = jnp.zeros_like(state)

        s_ref[0] = state[...]
        y, hs = _ssd_tile(x_ref[...], b_ref[...], c_ref[...], dt_ref[...], state[...], al_ref[...], db_ref[...],
                          dk_ref[...], e_ref[...])
        y_ref[...] = y
        state[...] = hs

    return pl.pallas_call(
        body, grid=(ntile,), name=name, in_specs=in_specs,
        out_specs=[pl.BlockSpec((tt, SSD_D_INNER), lambda i: (i, 0)),
                   pl.BlockSpec((1, SSD_GROUPS * SSD_STATE, 512), lambda i: (i, 0, 0))],
        out_shape=[jax.ShapeDtypeStruct((t_len, SSD_D_INNER), F32),
                   jax.ShapeDtypeStruct((ntile, SSD_GROUPS * SSD_STATE, 512), F32)],
        scratch_shapes=[pltpu.VMEM((SSD_GROUPS * SSD_STATE, 512), F32)],
        compiler_params=_cparams(("arbitrary",)),
    )(act, act, act, p, alog, dtb, dsk, _head_expand())


def _ssd_bwd(act, p, alog, dtb, dsk, states, dy, *, name):
    t_len = act.shape[0]
    tt = min(SEQ_TILE, t_len)
    ntile = t_len // tt
    in_specs, t = _ssd_specs(tt, ntile)

    def body(x_ref, b_ref, c_ref, dt_ref, al_ref, db_ref, dk_ref, e_ref, s0_ref, dy_ref,
             dx_ref, dbm_ref, dcm_ref, ddt_ref, dal_ref, ddb_ref, ddk_ref, dstate):
        @pl.when(pl.program_id(0) == 0)
        def _():
            dstate[...] = jnp.zeros_like(dstate)
            dal_ref[...] = jnp.zeros_like(dal_ref)
            ddb_ref[...] = jnp.zeros_like(ddb_ref)
            ddk_ref[...] = jnp.zeros_like(ddk_ref)

        expand = e_ref[...]

        def f(xs, bm, cm, dtr, hs0, al, db, dk):
            return _ssd_tile(xs, bm, cm, dtr, hs0, al, db, dk, expand)

        _, vjp = jax.vjp(f, x_ref[...], b_ref[...], c_ref[...], dt_ref[...], s0_ref[0], al_ref[...], db_ref[...],
                         dk_ref[...])
        dx, dbm, dcm, ddt, dhs, dal, ddb, ddk = vjp((dy_ref[...], dstate[...]))
        dx_ref[...] = dx
        dbm_ref[...] = dbm
        dcm_ref[...] = dcm
        ddt_ref[...] = ddt
        dstate[...] = dhs
        dal_ref[...] += dal
        ddb_ref[...] += ddb
        ddk_ref[...] += ddk

    vec = pl.BlockSpec((1, LANE), lambda i: (0, 0))
    rows = lambda w: pl.BlockSpec((tt, w), lambda i: (t(i), 0))
    return pl.pallas_call(
        body, grid=(ntile,), name=name,
        in_specs=in_specs + [pl.BlockSpec((1, SSD_GROUPS * SSD_STATE, 512), lambda i: (t(i), 0, 0)), rows(SSD_D_INNER)],
        out_specs=[rows(SSD_D_INNER), rows(512), rows(512), rows(LANE), vec, vec, vec],
        out_shape=[jax.ShapeDtypeStruct((t_len, SSD_D_INNER), F32), jax.ShapeDtypeStruct((t_len, 512), F32),
                   jax.ShapeDtypeStruct((t_len, 512), F32), jax.ShapeDtypeStruct((t_len, LANE), F32)]
        + [jax.ShapeDtypeStruct((1, LANE), F32)] * 3,
        scratch_shapes=[pltpu.VMEM((SSD_GROUPS * SSD_STATE, 512), F32)],
        compiler_params=_cparams(("arbitrary",)),
    )(act, act, act, p, alog, dtb, dsk, _head_expand(), states, dy)


ATT_SCALE = (MLA_NOPE + MLA_ROPE) ** -0.5
ATT_SCALE2 = ATT_SCALE * math.log2(math.e)
QK_W = 2 * LANE


def _chunk_mask(tq):
    return (_iota2((tq, tq), 1) // CHUNK) <= (_iota2((tq, tq), 0) // CHUNK)


def _attn_fwd(qc, kc, vv, *, name):
    t_len = qc.shape[0]
    tq = min(ATT_TILE, t_len)
    nq = t_len // tq

    def body(q_ref, k_ref, v_ref, o_ref, lse_ref, m_s, l_s, acc_s):
        qi, ki = pl.program_id(1), pl.program_id(2)

        @pl.when(ki == 0)
        def _():
            m_s[...] = jnp.full_like(m_s, -jnp.inf)
            l_s[...] = jnp.zeros_like(l_s)
            acc_s[...] = jnp.zeros_like(acc_s)

        def step(masked):
            s = _dot(q_ref[...], k_ref[...], "nt") * ATT_SCALE2
            if masked:
                s = jnp.where(_chunk_mask(tq), s, -jnp.inf)
            m_new = jnp.maximum(m_s[...], jnp.max(s, axis=-1, keepdims=True))
            alpha = jnp.exp2(m_s[...] - m_new)
            p = jnp.exp2(s - m_new)
            l_s[...] = alpha * l_s[...] + jnp.sum(p, axis=-1, keepdims=True)
            acc_s[...] = alpha * acc_s[...] + _dot(p, v_ref[...], "nn")
            m_s[...] = m_new

        @pl.when(ki < qi)
        def _():
            step(False)

        @pl.when(ki == qi)
        def _():
            step(True)
            o_ref[...] = acc_s[...] / l_s[...]
            lse_ref[...] = jnp.broadcast_to(m_s[...] + jnp.log2(l_s[...]), lse_ref.shape)

    kv_idx = lambda h, i, k: (jnp.minimum(k, i), h)
    return pl.pallas_call(
        body, grid=(MLA_HEADS, nq, nq), name=name,
        in_specs=[pl.BlockSpec((tq, QK_W), lambda h, i, k: (i, h)), pl.BlockSpec((tq, QK_W), kv_idx),
                  pl.BlockSpec((tq, LANE), kv_idx)],
        out_specs=[pl.BlockSpec((tq, LANE), lambda h, i, k: (i, h))] * 2,
        out_shape=[jax.ShapeDtypeStruct((t_len, MLA_HEADS * LANE), F32)] * 2,
        scratch_shapes=[pltpu.VMEM((tq, 1), F32), pltpu.VMEM((tq, 1), F32), pltpu.VMEM((tq, LANE), F32)],
        compiler_params=_cparams(("parallel", "parallel", "arbitrary")),
    )(qc, kc, vv)


def _attn_probs(q, k, v, do, o, lse, masked, tq):
    s = _dot(q, k, "nt") * ATT_SCALE2
    if masked:
        s = jnp.where(_chunk_mask(tq), s, -jnp.inf)
    p = jnp.exp2(s - lse[:, 0:1])
    delta = jnp.sum(do * o, axis=-1, keepdims=True)
    ds = p * (_dot(do, v, "nt") - delta)
    return p, ds


def _attn_bwd(qc, kc, vv, o, lse, do, *, name):
    t_len = qc.shape[0]
    tq = min(ATT_TILE, t_len)
    nq = t_len // tq

    def body(q_ref, k_ref, v_ref, o_ref, lse_ref, do_ref, dq_hbm, dk_ref, dv_ref, dq_s, dk_s, dv_s):
        head, ki, qi = pl.program_id(0), pl.program_id(1), pl.program_id(2)
        rows = pl.ds(pl.multiple_of(qi * tq, tq), tq)

        @pl.when(qi == 0)
        def _():
            dk_s[...] = jnp.zeros_like(dk_s)
            dv_s[...] = jnp.zeros_like(dv_s)

        def step(masked):
            p, ds = _attn_probs(q_ref[...], k_ref[...], v_ref[...], do_ref[...], o_ref[...], lse_ref[...], masked, tq)
            dv_s[...] += _dot(p, do_ref[...], "tn")
            dk_s[...] += _dot(ds, q_ref[...], "tn")
            part = _dot(ds, k_ref[...], "nn")

            @pl.when(ki == 0)
            def _():
                dq_s[rows, :] = part

            @pl.when(ki > 0)
            def _():
                dq_s[rows, :] += part

        @pl.when(qi > ki)
        def _():
            step(False)

        @pl.when(qi == ki)
        def _():
            step(True)
            dq_s[rows, :] = dq_s[rows, :] * ATT_SCALE
            pltpu.sync_copy(dq_s.at[rows, :], dq_hbm.at[rows, pl.ds(pl.multiple_of(head * QK_W, QK_W), QK_W)])

        @pl.when(qi == nq - 1)
        def _():
            dk_ref[...] = dk_s[...] * ATT_SCALE
            dv_ref[...] = dv_s[...]

    q_idx = lambda h, k, i: (jnp.maximum(i, k), h)
    k_idx = lambda h, k, i: (k, h)
    return pl.pallas_call(
        body, grid=(MLA_HEADS, nq, nq), name=name,
        in_specs=[pl.BlockSpec((tq, QK_W), q_idx), pl.BlockSpec((tq, QK_W), k_idx), pl.BlockSpec((tq, LANE), k_idx),
                  pl.BlockSpec((tq, LANE), q_idx), pl.BlockSpec((tq, LANE), q_idx), pl.BlockSpec((tq, LANE), q_idx)],
        out_specs=[pl.BlockSpec(memory_space=pl.ANY), pl.BlockSpec((tq, QK_W), k_idx), pl.BlockSpec((tq, LANE), k_idx)],
        out_shape=[jax.ShapeDtypeStruct((t_len, MLA_HEADS * QK_W), F32), jax.ShapeDtypeStruct((t_len, MLA_HEADS * QK_W), F32),
                   jax.ShapeDtypeStruct((t_len, MLA_HEADS * LANE), F32)],
        scratch_shapes=[pltpu.VMEM((t_len, QK_W), F32), pltpu.VMEM((tq, QK_W), F32), pltpu.VMEM((tq, LANE), F32)],
        compiler_params=_cparams(("arbitrary", "arbitrary", "arbitrary")),
    )(qc, kc, vv, o, lse, do)


def _adaln_fn(x, g, shift, scale):
    return ((_rms(x) * g) * (1.0 + scale) + shift,)


def _resid_fn(coef, y, x, gate):
    return (x + coef * gate * y,)


def _rms2_fn(cq, ckv, gq, gkv):
    return _rms(cq) * gq, _rms(ckv) * gkv


@jax.custom_vjp
def _swap_halves(x):
    return jnp.concatenate([x[:, 32:64], x[:, 0:32], x[:, 64:128]], axis=-1)


_swap_halves.defvjp(lambda x: (_swap_halves(x), None), lambda _, g: (_swap_halves(g),))


def _rope_fn(q, kv, misc, pos, invf, sgn):
    ang = pos * invf
    cos, sin = jnp.cos(ang), jnp.sin(ang) * sgn

    def rope(x):
        return x * cos + _swap_halves(x) * sin

    k_pe = rope(jnp.where(_iota2(misc.shape, 1) < MLA_ROPE, misc, 0.0))
    qs, ks = [], []
    for h in range(MLA_HEADS):
        qs += [q[:, h * LANE:(h + 1) * LANE], rope(q[:, (MLA_HEADS + h) * LANE:(MLA_HEADS + h + 1) * LANE])]
        ks += [kv[:, h * LANE:(h + 1) * LANE], k_pe]
    return jnp.concatenate(qs, axis=-1), jnp.concatenate(ks, axis=-1), kv[:, MLA_HEADS * LANE:]


def _ev_out_fn(oa, z, ob, g):
    parts = []
    for h in range(GDN_HEADS):
        hs = slice(h * LANE, (h + 1) * LANE)
        zz = z[:, hs]
        parts.append(_rms(oa[:, hs]) * g * (zz * jax.nn.sigmoid(zz)))
    return (jnp.concatenate(parts + [ob], axis=-1),)


def _od_out_fn(y, z, g):
    yz = y * (z * jax.nn.sigmoid(z))
    gw = SSD_D_INNER // SSD_GROUPS
    return (jnp.concatenate([_rms(yz[:, i * gw:(i + 1) * gw]) for i in range(SSD_GROUPS)], axis=-1) * g,)


def _loss_bwd(x, tgt, g, *, name):
    t_len, d = x.shape
    tm = min(ROW_TILE // 2, t_len)

    def body(x_ref, t_ref, g_ref, loss_ref, dx_ref, dg_ref):
        tgt_v = t_ref[...]

        def f(xv, gv):
            err = _rms(xv) * gv - tgt_v
            return 0.5 * jnp.sum(jnp.mean(err * err, axis=-1, keepdims=True), axis=0, keepdims=True)

        val, vjp = jax.vjp(f, x_ref[...], g_ref[...])
        dx, dg = vjp(jnp.ones((1, 1), F32))
        dx_ref[...] = dx

        @pl.when(pl.program_id(0) == 0)
        def _():
            loss_ref[...] = jnp.zeros_like(loss_ref)
            dg_ref[...] = jnp.zeros_like(dg_ref)

        loss_ref[...] += jnp.broadcast_to(val, loss_ref.shape)
        dg_ref[...] += dg

    row = pl.BlockSpec((tm, d), lambda i: (i, 0))
    return pl.pallas_call(
        body, grid=(t_len // tm,), name=name,
        in_specs=[row, row, pl.BlockSpec((1, d), lambda i: (0, 0))],
        out_specs=[pl.BlockSpec((1, LANE), lambda i: (0, 0)), row, pl.BlockSpec((1, d), lambda i: (0, 0))],
        out_shape=[jax.ShapeDtypeStruct((1, LANE), F32), jax.ShapeDtypeStruct((t_len, d), F32),
                   jax.ShapeDtypeStruct((1, d), F32)],
        compiler_params=_cparams(("arbitrary",)),
    )(x, tgt, g)


def _mesh_pos():
    return lax.axis_index("x"), lax.axis_index("y"), lax.axis_index("c")


def _exchange(xs, scatter, *, name):
    n_arr = len(xs)

    def body(*refs):
        in_refs, out_refs = refs[:n_arr], refs[n_arr:2 * n_arr]
        send_sems, recv_sems, local_sems = refs[2 * n_arr:]
        mx, my, mc = _mesh_pos()
        me = 4 * mx + 2 * my + mc
        started = []
        for a, (in_ref, out_ref) in enumerate(zip(in_refs, out_refs)):
            def src(j, in_ref=in_ref):
                return in_ref.at[j] if scatter else in_ref

            local = pltpu.make_async_copy(src(me), out_ref.at[me], local_sems.at[a])
            local.start()
            started.append((local, None))
            for d in range(1, N_DEV):
                px = 1 - mx if d & 4 else mx
                py = 1 - my if d & 2 else my
                pc = 1 - mc if d & 1 else mc
                peer = 4 * px + 2 * py + pc
                sem = a * (N_DEV - 1) + d - 1
                send = pltpu.make_async_remote_copy(
                    src_ref=src(peer), dst_ref=out_ref.at[me], send_sem=send_sems.at[sem], recv_sem=recv_sems.at[sem],
                    device_id=(px, py, pc), device_id_type=pl.DeviceIdType.MESH)
                send.start()
                recv = pltpu.make_async_remote_copy(
                    src_ref=src(peer), dst_ref=out_ref.at[peer], send_sem=send_sems.at[sem], recv_sem=recv_sems.at[sem],
                    device_id=(px, py, pc), device_id_type=pl.DeviceIdType.MESH)
                started.append((send, recv))
        for first, recv in started:
            if recv is None:
                first.wait()
            else:
                first.wait_send()
                recv.wait_recv()

    blocks = [tuple(x.shape[1:]) if scatter else tuple(x.shape) for x in xs]
    return pl.pallas_call(
        body, name=name,
        in_specs=[pl.BlockSpec(memory_space=pl.ANY)] * n_arr,
        out_specs=[pl.BlockSpec(memory_space=pl.ANY)] * n_arr,
        out_shape=[jax.ShapeDtypeStruct((N_DEV,) + b, x.dtype) for b, x in zip(blocks, xs)],
        scratch_shapes=[pltpu.SemaphoreType.DMA((n_arr * (N_DEV - 1),)), pltpu.SemaphoreType.DMA((n_arr * (N_DEV - 1),)),
                        pltpu.SemaphoreType.DMA((n_arr,))],
        compiler_params=pltpu.CompilerParams(has_side_effects=True),
    )(*xs)


def _peer_of(d, pos):
    mx, my, mc = pos
    px = 1 - mx if d & 4 else mx
    py = 1 - my if d & 2 else my
    pc = 1 - mc if d & 1 else mc
    return (px, py, pc), 4 * px + 2 * py + pc


_HBM = pl.BlockSpec(memory_space=pltpu.HBM)
_SEM = pl.BlockSpec(memory_space=pltpu.SEMAPHORE)


def _exchange_start(xs, scatter, *, name):
    n_arr = len(xs)
    n_sem = n_arr * (N_DEV - 1)

    def body(*refs):
        in_refs, land_refs = refs[:n_arr], refs[n_arr:2 * n_arr]
        send_sems, recv_sems, token = refs[2 * n_arr], refs[2 * n_arr + 1], refs[-1]
        pos = _mesh_pos()
        me = 4 * pos[0] + 2 * pos[1] + pos[2]
        for a in range(n_arr):
            for d in range(1, N_DEV):
                dev, peer = _peer_of(d, pos)
                sem = a * (N_DEV - 1) + d - 1
                pltpu.make_async_remote_copy(
                    src_ref=in_refs[a].at[peer] if scatter else in_refs[a], dst_ref=land_refs[a].at[me],
                    send_sem=send_sems.at[sem], recv_sem=recv_sems.at[sem], device_id=dev,
                    device_id_type=pl.DeviceIdType.MESH).start()
        token[...] = jnp.zeros_like(token)

    blocks = [tuple(x.shape[1:]) if scatter else tuple(x.shape) for x in xs]
    srcs = [pltpu.with_memory_space_constraint(x, pltpu.HBM) for x in xs]
    lands = [pltpu.with_memory_space_constraint(lax.empty((N_DEV,) + b, x.dtype), pltpu.HBM) for b, x in zip(blocks, xs)]
    res = pl.pallas_call(
        body, name=name,
        out_shape=(pltpu.SemaphoreType.DMA((n_sem,)), pltpu.SemaphoreType.DMA((n_sem,)),
                   *[pltpu.HBM(a.shape, a.dtype) for a in srcs + lands], jax.ShapeDtypeStruct((8, LANE), F32)),
        in_specs=[_HBM] * (2 * n_arr),
        out_specs=(_SEM, _SEM, *[_HBM] * (2 * n_arr), pl.BlockSpec(memory_space=pltpu.VMEM)),
        input_output_aliases={i: 2 + i for i in range(2 * n_arr)},
        compiler_params=pltpu.CompilerParams(has_side_effects=pltpu.SideEffectType.DATAFLOW_SIDE_EFFECTING),
    )(*srcs, *lands)
    handle = dict(sems=res[:2], srcs=res[2:2 + n_arr], lands=res[2 + n_arr:2 + 2 * n_arr], scatter=scatter)
    return handle, res[-1][0, 0]


def _exchange_wait(handle, after, me, *, name):
    scatter = handle["scatter"]
    n_arr = len(handle["srcs"])

    def body(*refs):
        in_refs, land_refs = refs[:n_arr], refs[n_arr:2 * n_arr]
        send_sems, recv_sems = refs[2 * n_arr], refs[2 * n_arr + 1]
        pos = _mesh_pos()
        for a in range(n_arr):
            for d in range(1, N_DEV):
                dev, peer = _peer_of(d, pos)
                sem = a * (N_DEV - 1) + d - 1
                copy = pltpu.make_async_remote_copy(
                    src_ref=in_refs[a].at[peer] if scatter else in_refs[a], dst_ref=land_refs[a].at[peer],
                    send_sem=send_sems.at[sem], recv_sem=recv_sems.at[sem], device_id=dev,
                    device_id_type=pl.DeviceIdType.MESH)
                copy.wait_send()
                copy.wait_recv()

    thru = list(handle["srcs"]) + list(handle["lands"])
    res = pl.pallas_call(
        body, name=name,
        out_shape=tuple(pltpu.HBM(a.shape, a.dtype) for a in thru),
        in_specs=[_HBM] * (2 * n_arr) + [_SEM, _SEM, pl.BlockSpec(memory_space=pl.ANY)],
        out_specs=tuple([_HBM] * (2 * n_arr)),
        input_output_aliases={i: i for i in range(2 * n_arr)},
        compiler_params=pltpu.CompilerParams(has_side_effects=pltpu.SideEffectType.DATAFLOW_SIDE_EFFECTING),
    )(*thru, *handle["sems"], after)
    out = []
    for src, land in zip(res[:n_arr], res[n_arr:]):
        own = lax.dynamic_index_in_dim(src, me, axis=0, keepdims=True) if scatter else src[None]
        out.append(lax.dynamic_update_index_in_dim(land, own, me, axis=0))
    return out


def _cols(srcs, rows, plans, out_dtype, *, name):
    n_src = len(srcs)
    rb = _pick(rows, (256, 128, 64, 32, 16, 8))

    def width(pieces):
        return sum(p[1] if p[0] == "z" else p[3] - p[2] for p in pieces)

    def body(*refs):
        ins, outs = refs[:n_src], refs[n_src:]
        loaded = {}
        for o_ref, plan in zip(outs, plans):
            for j, pieces in enumerate(plan):
                vals = []
                for pc in pieces:
                    if pc[0] == "z":
                        vals.append(jnp.zeros((rb, pc[1]), out_dtype))
                    else:
                        si, sj, c0, c1 = pc
                        if (si, sj) not in loaded:
                            loaded[(si, sj)] = ins[si][sj]
                        vals.append(loaded[(si, sj)][:, c0:c1].astype(out_dtype))
                o_ref[j] = vals[0] if len(vals) == 1 else jnp.concatenate(vals, axis=-1)

    for arr, r0 in srcs:
        assert r0 % rb == 0
    return pl.pallas_call(
        body, grid=(rows // rb,), name=name,
        in_specs=[pl.BlockSpec((arr.shape[0], rb, arr.shape[2]), lambda i, r0=r0 // rb: (0, r0 + i, 0)) for arr, r0 in srcs],
        out_specs=[pl.BlockSpec((len(p), rb, width(p[0])), lambda i: (0, i, 0)) for p in plans],
        out_shape=[jax.ShapeDtypeStruct((len(p), rows, width(p[0])), out_dtype) for p in plans],
        compiler_params=_cparams(("parallel",)),
    )(*[arr for arr, _ in srcs])


def _shard_pieces(src, a, b, shard_w):
    out = []
    while a < b:
        s = a // shard_w
        e = min(b, (s + 1) * shard_w)
        out.append((src, s, a - s * shard_w, e - s * shard_w))
        a = e
    return out


def _mapped_pieces(a, b, segs):
    out = []
    for n0, n1, k0 in sorted(segs):
        lo, hi = max(a, n0), min(b, n1)
        if lo < hi:
            out.append((0, 0, k0 + lo - n0, k0 + hi - n0))
    return out


_EV_SEGS = [(0, 1536, EV_QKV), (1536, 2048, EV_Z), (2048, 2056, EV_MISC + MLA_ROPE), (2056, 2440, EV_CQ),
            (2440, 2696, EV_CKV), (2696, 2760, EV_MISC)]
EV_NAT_W, OD_NAT_W = 2760, 5152


PACK_W = 1024


def _adamw(w, gparts, m, v, *, name):
    n_rows, n_cols = w.shape
    n_parts = gparts.shape[0]
    tm = _pick(n_rows, (512, 256, 128, 64, 32, 16, 8))
    while n_parts * tm * n_cols * 4 > 4 * 1024 * 1024 and tm % 16 == 0:
        tm //= 2

    def body(w_ref, g_ref, m_ref, v_ref, go_ref, d_ref, mo_ref, vo_ref):
        g = g_ref[0]
        for j in range(1, n_parts):
            g = g + g_ref[j]
        m_new = ADAM_B1 * m_ref[...] + (1.0 - ADAM_B1) * g
        v_new = ADAM_B2 * v_ref[...] + (1.0 - ADAM_B2) * jnp.square(g)
        m_hat = m_new / (1.0 - ADAM_B1 ** ADAM_STEP)
        v_hat = v_new / (1.0 - ADAM_B2 ** ADAM_STEP)
        go_ref[...] = g
        d_ref[...] = -ADAM_LR * (m_hat / (jnp.sqrt(v_hat) + ADAM_EPS) + ADAM_WD * w_ref[...])
        mo_ref[...] = m_new
        vo_ref[...] = v_new

    row = pl.BlockSpec((tm, n_cols), lambda i: (i, 0))
    return pl.pallas_call(
        body, grid=(n_rows // tm,), name=name,
        in_specs=[row, pl.BlockSpec((n_parts, tm, n_cols), lambda i: (0, i, 0)), row, row],
        out_specs=[row] * 4,
        out_shape=[jax.ShapeDtypeStruct((n_rows, n_cols), F32)] * 4,
        compiler_params=_cparams(("parallel",)),
    )(w, gparts, m, v)


def _adamw_nd(w, gparts, m, v, *, name):
    shape = w.shape
    two = (-1, shape[-1])
    outs = _adamw(w.reshape(two), gparts.reshape((gparts.shape[0],) + (int(np.prod(shape[:-1])), shape[-1])),
                  m.reshape(two), v.reshape(two), name=name)
    return [o.reshape(shape) for o in outs]


def _pack(parts):
    flat = [p.astype(F32).reshape(-1) for p in parts]
    n_pad = -sum(f.shape[0] for f in flat) % (8 * PACK_W)
    return jnp.concatenate(flat + [jnp.zeros((n_pad,), F32)]).reshape(-1, PACK_W)


def _unpack(packed, shapes):
    flat = packed.reshape(-1)
    out, off = [], 0
    for s in shapes:
        n = int(np.prod(s))
        out.append(flat[off:off + n].reshape(tuple(s)))
        off += n
    return out


def _mod_shard(c_all, ada_w, ada_b_shard, *, name):
    n_layer, d, n_col = ada_w.shape

    def body(c_ref, w_ref, b_ref, o_ref):
        cv = c_ref[...]
        o_ref[0] = _dot(cv * jax.nn.sigmoid(cv), w_ref[0], "nn") + b_ref[0]

    return pl.pallas_call(
        body, grid=(n_layer,), name=name,
        in_specs=[pl.BlockSpec((N_DEV, d), lambda l: (0, 0)), pl.BlockSpec((1, d, n_col), lambda l: (l, 0, 0)),
                  pl.BlockSpec((1, 1, n_col), lambda l: (l, 0, 0))],
        out_specs=pl.BlockSpec((1, N_DEV, n_col), lambda l: (l, 0, 0)),
        out_shape=jax.ShapeDtypeStruct((n_layer, N_DEV, n_col), F32),
        compiler_params=_cparams(("parallel",)),
    )(c_all, ada_w, ada_b_shard)


def _ada_w_grad(c_all, dmod_shard, *, name):
    n_layer, _, n_col = dmod_shard.shape
    d = c_all.shape[1]

    def body(c_ref, g_ref, o_ref):
        cv = c_ref[...]
        o_ref[0] = _dot(cv * jax.nn.sigmoid(cv), g_ref[0], "tn", True)

    return pl.pallas_call(
        body, grid=(n_layer,), name=name,
        in_specs=[pl.BlockSpec((N_DEV, d), lambda l: (0, 0)), pl.BlockSpec((1, N_DEV, n_col), lambda l: (l, 0, 0))],
        out_specs=pl.BlockSpec((1, d, n_col), lambda l: (l, 0, 0)),
        out_shape=jax.ShapeDtypeStruct((n_layer, d, n_col), F32),
        compiler_params=_cparams(("parallel",)),
    )(c_all, dmod_shard)


def _uq(w):
    r = w.shape[0]
    rope = jnp.pad(w[:, :, MLA_NOPE:], ((0, 0), (0, 0), (0, LANE - MLA_ROPE)))
    return jnp.concatenate([w[:, :, :MLA_NOPE].reshape(r, -1), rope.reshape(r, -1)], axis=1)


def _uq_back(d):
    r = d.shape[0]
    half = MLA_HEADS * LANE
    return jnp.concatenate([d[:, :half].reshape(r, MLA_HEADS, LANE),
                            d[:, half:].reshape(r, MLA_HEADS, LANE)[:, :, :MLA_ROPE]], axis=-1)


def _ukv(w):
    r = w.shape[0]
    return jnp.concatenate([w[:, :, :MLA_NOPE].reshape(r, -1), w[:, :, MLA_NOPE:].reshape(r, -1)], axis=1)


def _ukv_back(d):
    r = d.shape[0]
    half = MLA_HEADS * LANE
    return jnp.concatenate([d[:, :half].reshape(r, MLA_HEADS, LANE), d[:, half:].reshape(r, MLA_HEADS, LANE)], axis=-1)


def _lane_vec(v):
    return jnp.pad(v.astype(F32), (0, LANE - v.shape[0])).reshape(1, LANE)


def _row(v):
    return v.astype(F32).reshape(1, -1)


def _adaln(x, ln):
    return _rows(_adaln_fn, [x], list(ln), [(D_MODEL, BF16)], name="adaln", also_t=(0,))


def _adaln_bwd(x, ln, dh, dxn):
    (dx,), dln = _rows_vjp(_adaln_fn, [x], [], list(ln), [], [dh], [F32], adds={0: dxn}, name="adaln_bwd", tm=ROW_TILE)
    return dx, dln


def _resid(coef, y, x, gate):
    return _rows(functools.partial(_resid_fn, coef), [y, x], [gate], [(D_MODEL, F32)], name="resid")[0]


def _resid_adaln_fn(coef, y, x, gate, g, shift, scale):
    xn = x + coef * gate * y
    return (xn,) + _adaln_fn(xn, g, shift, scale)


def _norm_in(x, pending, ln):
    if pending is None:
        return (x,) + tuple(_adaln(x, ln))
    coef, y, gate = pending
    return _rows(functools.partial(_resid_adaln_fn, coef), [y, x], [gate] + list(ln), [(D_MODEL, F32), (D_MODEL, BF16)],
                 name="resid_adaln", also_t=(1,))


def _gated_fn(coef, y, gate):
    return (coef * gate * y,)


def _resid_bwd(coef, y, gate, dxn):
    (dy,), (dgate,) = _rows_vjp(functools.partial(_gated_fn, coef), [y], [], [gate], [], [dxn], [BF16], name="resid_bwd",
                                tm=ROW_TILE)
    return dy, dgate


def _ffn_fwd(x, h, ht, w13, w2):
    s, st, ab = _ffn_act(h, w13, name="ffn_act")
    y = _matmul(s, w2, "nn", F32, name="ffn_down")
    return y, (x, ab, ht, st, y)


def _ffn_bwd(saved, dxn, ln, gate, w13, w2):
    x, ab, ht, st, y = saved
    dy, dgate = _resid_bwd(0.5, y, gate, dxn)
    dab = _ffn_act_bwd(ab, dy, w2, name="ffn_act_bwd")
    dh = _matmul(dab, w13, "nt", F32, name="ffn_dh")
    dw13 = _matmul(ht, dab, "kn", F32, name="ffn_dw13")
    dw2 = _matmul(st, dy, "kn", F32, name="ffn_dw2")
    dx, dln = _adaln_bwd(x, ln, dh, dxn)
    return dx, dw13, dw2, dln, dgate


def _rope_consts():
    half = MLA_ROPE // 2
    inv = (ROPE_THETA ** (-jnp.arange(half, dtype=F32) / half)).astype(F32)
    zeros = jnp.zeros((LANE - MLA_ROPE,), F32)
    invf = jnp.concatenate([inv, inv, zeros]).reshape(1, LANE)
    sgn = jnp.concatenate([-jnp.ones((half,), F32), jnp.ones((half,), F32), zeros]).reshape(1, LANE)
    return invf, sgn


def _even_fwd(x, h, ht, pos, wt):
    p = _matmul(h, wt["w_in"], "nn", F32, name="ev_in")
    act, pre = _conv_fwd(p, EV_QKV, 1536, wt["conv_w"], jnp.zeros((1, 1536), F32), name="ev_conv")
    o_a, states = _gdn_fwd(act, p, wt["alog"], wt["dtb"], name="gdn_fwd")
    cqn, ckvn = _rows(_rms2_fn, [(p, EV_CQ, 384), (p, EV_CKV, 256)], [wt["gq"], wt["gkv"]],
                      [(384, BF16), (256, BF16)], name="mla_rms")
    q = _matmul(cqn, wt["w_uq"], "nn", F32, name="mla_uq")
    kv = _matmul(ckvn, wt["w_ukv"], "nn", F32, name="mla_ukv")
    invf, sgn = _rope_consts()
    qc, kc, vv = _rows(_rope_fn, [q, kv, (p, EV_MISC, LANE), pos], [invf, sgn],
                       [(1024, BF16), (1024, BF16), (512, BF16)], name="mla_rope", tm=ROW_TILE // 2)
    o_b, lse = _attn_fwd(qc, kc, vv, name="attn_fwd")
    o, ot = _rows(_ev_out_fn, [o_a, (p, EV_Z, 512), o_b], [wt["gdn_g"]], [(1024, BF16)], name="ev_out", also_t=(0,))
    y = _matmul(o, wt["w_out"], "nn", F32, name="ev_wout")
    return y, (x, ht, p, act, pre, states, cqn, ckvn, q, kv, qc, kc, vv, o_a, o_b, lse, ot, y)


def _cat_fn(*parts):
    return (jnp.concatenate(parts, axis=-1),)


def _ev_dp_fn(dx0, dx1, dx2, dcq, dm_r, dm_g, dz, dckv):
    return (jnp.concatenate([dx0, dx1, dx2, dcq, dm_r + dm_g, dz, dckv], axis=-1),)


def _even_bwd(saved, dxn, pos, ln, gate, wt):
    x, ht, p, act, pre, states, cqn, ckvn, q, kv, qc, kc, vv, o_a, o_b, lse, ot, y = saved
    g = {}
    dy, g["gate"] = _resid_bwd(1.0, y, gate, dxn)
    do = _matmul(dy, wt["w_out"], "nt", F32, name="ev_dwout_x")
    g["w_out"] = _matmul(ot, dy, "kn", F32, name="ev_dwout_w")
    (d_oa, dz, d_ob), (g["gdn_g"],) = _rows_vjp(_ev_out_fn, [o_a, (p, EV_Z, 512), o_b], [], [wt["gdn_g"]], [], [do],
                                                [F32, F32, F32], name="ev_out_bwd")
    dqc, dkc, dvv = _attn_bwd(qc, kc, vv, o_b, lse, d_ob, name="attn_bwd")
    invf, sgn = _rope_consts()
    (dq, dkv, dm_r), _ = _rows_vjp(_rope_fn, [q, kv, (p, EV_MISC, LANE)], [pos], [], [invf, sgn], [dqc, dkc, dvv],
                                   [BF16, BF16, F32], name="mla_rope_bwd", tm=ROW_TILE // 2)
    dcqn = _matmul(dq, wt["w_uq"], "nt", F32, name="mla_duq_x")
    g["w_uq"] = _matmul(cqn, dq, "tn", F32, name="mla_duq_w")
    dckvn = _matmul(dkv, wt["w_ukv"], "nt", F32, name="mla_dukv_x")
    g["w_ukv"] = _matmul(ckvn, dkv, "tn", F32, name="mla_dukv_w")
    (dcq, dckv), (g["gq"], g["gkv"]) = _rows_vjp(_rms2_fn, [(p, EV_CQ, 384), (p, EV_CKV, 256)], [],
                                                 [wt["gq"], wt["gkv"]], [], [dcqn, dckvn], [F32, F32], name="mla_rms_bwd")
    dq_g, dk_g, dv_g, dm_g, g["alog"], g["dtb"] = _gdn_bwd(act, p, wt["alog"], wt["dtb"], states, d_oa, name="gdn_bwd")
    dxs, dws = [], []
    for j, d in enumerate((dq_g, dk_g, dv_g)):
        dxj, dwj, _ = _conv_bwd(d, pre, 512 * j, p, EV_QKV + 512 * j, wt["conv_w"], name="ev_conv_bwd")
        dxs.append(dxj)
        dws.append(dwj)
    g["conv_w"] = jnp.concatenate(dws, axis=1)
    (dp,) = _rows(_ev_dp_fn, dxs + [dcq, dm_r, dm_g, dz, dckv], [],
                  [(EV_W, BF16)], name="ev_dp", tm=ROW_TILE // 2)
    dh = _matmul(dp, wt["w_in"], "nt", F32, name="ev_din_x")
    g["w_in"] = _matmul(ht, dp, "kn", F32, name="ev_din_w")
    dx, g["ln"] = _adaln_bwd(x, ln, dh, dxn)
    return dx, g


def _odd_fwd(x, h, ht, wt):
    p = _matmul(h, wt["w_in"], "nn", F32, name="od_in")
    act, pre = _conv_fwd(p, OD_XBC, 3072, wt["conv_w"], wt["conv_b"], name="od_conv")
    ys, states = _ssd_fwd(act, p, wt["alog"], wt["dtb"], wt["dsk"], name="ssd_fwd")
    o, ot = _rows(_od_out_fn, [ys, (p, OD_Z, 2048)], [wt["norm_g"]], [(SSD_D_INNER, BF16)], name="od_out",
                  tm=ROW_TILE // 2, also_t=(0,))
    y = _matmul(o, wt["w_out"], "nn", F32, name="od_wout")
    return y, (x, ht, p, act, pre, states, ys, ot, y)


def _od_dp_fn(dz, dxx, dxb, dxc, ddt):
    return (jnp.concatenate([dz, dxx, dxb, dxc, ddt, jnp.zeros_like(ddt)], axis=-1),)


def _odd_bwd(saved, dxn, ln, gate, wt):
    x, ht, p, act, pre, states, ys, ot, y = saved
    g = {}
    dy, g["gate"] = _resid_bwd(1.0, y, gate, dxn)
    do = _matmul(dy, wt["w_out"], "nt", F32, name="od_dwout_x")
    g["w_out"] = _matmul(ot, dy, "kn", F32, name="od_dwout_w")
    (dys, dz), (g["norm_g"],) = _rows_vjp(_od_out_fn, [ys, (p, OD_Z, 2048)], [], [wt["norm_g"]], [], [do], [F32, F32],
                                          name="od_out_bwd", tm=ROW_TILE // 2)
    dxs, dbm, dcm, ddt, g["alog"], g["dtb"], g["dsk"] = _ssd_bwd(act, p, wt["alog"], wt["dtb"], wt["dsk"], states, dys,
                                                                 name="ssd_bwd")
    dins, dws, dbs = [], [], []
    for d, c0 in ((dxs, 0), (dbm, 2048), (dcm, 2560)):
        dxj, dwj, dbj = _conv_bwd(d, pre, c0, p, OD_XBC + c0, wt["conv_w"], name="od_conv_bwd")
        dins.append(dxj)
        dws.append(dwj)
        dbs.append(dbj)
    g["conv_w"] = jnp.concatenate(dws, axis=1)
    g["conv_b"] = jnp.concatenate(dbs, axis=1)
    (dp,) = _rows(_od_dp_fn, [dz] + dins + [ddt], [], [(OD_W, BF16)], name="od_dp", tm=ROW_TILE // 2)
    dh = _matmul(dp, wt["w_in"], "nt", F32, name="od_din_x")
    g["w_in"] = _matmul(ht, dp, "kn", F32, name="od_din_w")
    dx, g["ln"] = _adaln_bwd(x, ln, dh, dxn)
    return dx, g


def _local_step(x, tgt, pos, mod, final_g, layer_weights, layer_done):
    mod = mod.reshape(DEPTH, 3, 3, 1, D_MODEL)
    wts = []

    def ln_of(l, i):
        return (_row(wts[l]["norm_g"][i]), mod[l, i, 0], mod[l, i, 1])

    def mixer_w(l):
        p = wts[l]
        if l % 2 == 0:
            return dict(w_in=p["w_in_k"], conv_w=p["gdn_conv_w"].astype(F32),
                        alog=_lane_vec(p["gdn_A_log"]), dtb=_lane_vec(p["gdn_dt_bias"]),
                        gdn_g=_row(p["gdn_norm_g"]), gq=_row(p["mla_q_norm_g"]), gkv=_row(p["mla_kv_norm_g"]),
                        w_uq=_uq(p["mla_w_uq"]), w_ukv=_ukv(p["mla_w_ukv"]), w_out=p["ev_w_out"])
        return dict(w_in=p["w_in_k"], conv_w=p["ssd_conv_w"].astype(F32),
                    conv_b=_row(p["ssd_conv_b"]), alog=_lane_vec(p["ssd_A_log"]),
                    dtb=_lane_vec(p["ssd_dt_bias"]), dsk=_lane_vec(p["ssd_D"]),
                    norm_g=_row(p["ssd_norm_g"]), w_out=p["ssd_w_out"])

    saved = []
    pending = None
    for l in range(DEPTH):
        wts.append(layer_weights(l, x if pending is None else pending[1]))
        x, h, ht = _norm_in(x, pending, ln_of(l, 0))
        y, s0 = _ffn_fwd(x, h, ht, wts[l]["w13"][0], wts[l]["w2"][0])
        x, h, ht = _norm_in(x, (0.5, y, mod[l, 0, 2]), ln_of(l, 1))
        if l % 2 == 0:
            y, s1 = _even_fwd(x, h, ht, pos, mixer_w(l))
        else:
            y, s1 = _odd_fwd(x, h, ht, mixer_w(l))
        x, h, ht = _norm_in(x, (1.0, y, mod[l, 1, 2]), ln_of(l, 2))
        y, s2 = _ffn_fwd(x, h, ht, wts[l]["w13"][1], wts[l]["w2"][1])
        pending = (0.5, y, mod[l, 2, 2])
        saved.append((s0, s1, s2))
    x = _resid(*pending[:2], x, pending[2])

    loss, dx, d_final_g = _loss_bwd(x, tgt, _row(final_g), name="loss")

    repl = {k: [None] * (DEPTH // 2) for k in ("gdn_A_log", "gdn_dt_bias", "gdn_norm_g", "mla_q_norm_g", "mla_kv_norm_g",
                                                "ssd_A_log", "ssd_dt_bias", "ssd_D")}
    dmod = [None] * DEPTH
    token = None
    for l in reversed(range(DEPTH)):
        s0, s1, s2 = saved[l]
        e = l // 2
        gl = {"w13": [None] * 2, "w2": [None] * 2}
        dg, dsh, dsc, dgt = [None] * 3, [None] * 3, [None] * 3, [None] * 3
        gate2 = mod[l, 2, 2] if token is None else mod[l, 2, 2] + token
        dx, gl["w13"][1], gl["w2"][1], (dg[2], dsh[2], dsc[2]), dgt[2] = _ffn_bwd(
            s2, dx, ln_of(l, 2), gate2, wts[l]["w13"][1], wts[l]["w2"][1])
        if l % 2 == 0:
            dx, g = _even_bwd(s1, dx, pos, ln_of(l, 1), mod[l, 1, 2], mixer_w(l))
            gl.update(w_in_k=g["w_in"], gdn_conv_w=g["conv_w"], mla_w_uq=_uq_back(g["w_uq"]),
                      mla_w_ukv=_ukv_back(g["w_ukv"]), ev_w_out=g["w_out"])
            repl["gdn_A_log"][e] = g["alog"][0, :GDN_HEADS]
            repl["gdn_dt_bias"][e] = g["dtb"][0, :GDN_HEADS]
            repl["gdn_norm_g"][e] = g["gdn_g"][0]
            repl["mla_q_norm_g"][e] = g["gq"][0]
            repl["mla_kv_norm_g"][e] = g["gkv"][0]
        else:
            dx, g = _odd_bwd(s1, dx, ln_of(l, 1), mod[l, 1, 2], mixer_w(l))
            gl.update(w_in_k=g["w_in"], ssd_conv_w=g["conv_w"], ssd_conv_b=g["conv_b"][0], ssd_norm_g=g["norm_g"][0],
                      ssd_w_out=g["w_out"])
            repl["ssd_A_log"][e] = g["alog"][0, :SSD_HEADS]
            repl["ssd_dt_bias"][e] = g["dtb"][0, :SSD_HEADS]
            repl["ssd_D"][e] = g["dsk"][0, :SSD_HEADS]
        dg[1], dsh[1], dsc[1] = g["ln"]
        dgt[1] = g["gate"]
        gl.update(w13=gl["w13"][1], w2=gl["w2"][1])
        token = layer_done(l, 0, gl, dx)
        gate0 = mod[l, 0, 2] if token is None else mod[l, 0, 2] + token
        dx, dw13, dw2, (dg[0], dsh[0], dsc[0]), dgt[0] = _ffn_bwd(s0, dx, ln_of(l, 0), gate0, wts[l]["w13"][0], wts[l]["w2"][0])
        dmod[l] = jnp.concatenate([jnp.concatenate([dsh[i], dsc[i], dgt[i]], axis=1) for i in range(3)], axis=1)[0]
        token = layer_done(l, 1, dict(w13=dw13, w2=dw2, norm_g=jnp.concatenate(dg, axis=0)), dx)

    grads = {k: jnp.stack(v) for k, v in repl.items()}
    grads["final_g"] = d_final_g[0]
    return loss, dx, grads, jnp.stack(dmod)


_WEIGHTS = ("ada_w", "ada_b", "norm_g", "ffn_w1", "ffn_w3", "ffn_w2", "ev_w_in", "gdn_conv_w", "gdn_A_log", "gdn_dt_bias",
            "gdn_norm_g", "mla_q_norm_g", "mla_w_uq", "mla_kv_norm_g", "mla_w_ukv", "ev_w_out", "ssd_w_in", "ssd_conv_w",
            "ssd_conv_b", "ssd_A_log", "ssd_dt_bias", "ssd_D", "ssd_norm_g", "ssd_w_out", "final_g")
_BIG = {"ffn_w1": 3, "ffn_w3": 3, "ffn_w2": 2, "ev_w_in": 2, "mla_w_uq": 1, "mla_w_ukv": 1, "ev_w_out": 1, "ssd_w_in": 2,
        "ssd_w_out": 1}
_SMALL = {"norm_g": 2, "gdn_conv_w": 2, "ssd_conv_w": 2, "ssd_conv_b": 1, "ssd_norm_g": 1}
_REPL = ("ada_b", "gdn_A_log", "gdn_dt_bias", "gdn_norm_g", "mla_q_norm_g", "mla_kv_norm_g", "ssd_A_log", "ssd_dt_bias",
         "ssd_D", "final_g")


def _join(pieces, axis):
    moved = jnp.moveaxis(pieces, 0, axis)
    shape = moved.shape
    return moved.reshape(shape[:axis] + (shape[axis] * shape[axis + 1],) + shape[axis + 2:])


def _split(full, axis):
    shape = full.shape
    return jnp.moveaxis(full.reshape(shape[:axis] + (N_DEV, shape[axis] // N_DEV) + shape[axis + 1:]), axis, 0)


def kernel(x, c, positions, ada_w, ada_b, norm_g, ffn_w1, ffn_w3, ffn_w2, ev_w_in, gdn_conv_w, gdn_A_log, gdn_dt_bias, gdn_norm_g, mla_q_norm_g, mla_w_uq, mla_kv_norm_g, mla_w_ukv, ev_w_out, ssd_w_in, ssd_conv_w, ssd_conv_b, ssd_A_log, ssd_dt_bias, ssd_D, ssd_norm_g, ssd_w_out, final_g, loss_target, m_ada_w, m_ada_b, m_norm_g, m_ffn_w1, m_ffn_w3, m_ffn_w2, m_ev_w_in, m_gdn_conv_w, m_gdn_A_log, m_gdn_dt_bias, m_gdn_norm_g, m_mla_q_norm_g, m_mla_w_uq, m_mla_kv_norm_g, m_mla_w_ukv, m_ev_w_out, m_ssd_w_in, m_ssd_conv_w, m_ssd_conv_b, m_ssd_A_log, m_ssd_dt_bias, m_ssd_D, m_ssd_norm_g, m_ssd_w_out, m_final_g, v_ada_w, v_ada_b, v_norm_g, v_ffn_w1, v_ffn_w3, v_ffn_w2, v_ev_w_in, v_gdn_conv_w, v_gdn_A_log, v_gdn_dt_bias, v_gdn_norm_g, v_mla_q_norm_g, v_mla_w_uq, v_mla_kv_norm_g, v_mla_w_ukv, v_ev_w_out, v_ssd_w_in, v_ssd_conv_w, v_ssd_conv_b, v_ssd_A_log, v_ssd_dt_bias, v_ssd_D, v_ssd_norm_g, v_ssd_w_out, v_final_g):
    a = dict(locals())
    w = {n: a[n] for n in _WEIGHTS}
    m = {n: a["m_" + n] for n in _WEIGHTS}
    v = {n: a["v_" + n] for n in _WEIGHTS}
    mx, my, mc = _mesh_pos()
    me = 4 * mx + 2 * my + mc
    t_len = x.shape[1]
    shards = range(N_DEV)

    small_names, big_names = list(_SMALL), list(_BIG)
    axis_of = {**_SMALL, **_BIG}
    small_g = _exchange([c] + [w[n] for n in small_names], False, name="gather_small")
    c_all = small_g[0].reshape(N_DEV, D_MODEL)
    fw = {n: _join(p, _SMALL[n]) for n, p in zip(small_names, small_g[1:])}
    first_names = [n for n in big_names if not n.startswith("ssd")]
    first_g = dict(zip(first_names, _exchange([w[n][:1].astype(BF16) for n in first_names], False, name="gather_first")))
    rest_src = {n: (w[n] if n.startswith("ssd") else w[n][1:]).astype(BF16) for n in big_names}
    rest_handle, rest_token = _exchange_start([rest_src[n] for n in big_names], False, name="gather_rest_start")
    rest_g = {}

    fs, es, os_ = ffn_w1.shape[3], ev_w_in.shape[2], ssd_w_in.shape[2]
    half = range(N_DEV // 2)
    plan13 = [[[(0, s, 0, fs) for s in half] + [(1, s, 0, fs) for s in half]
               + [(0, s + 4, 0, fs) for s in half] + [(1, s + 4, 0, fs) for s in half]]]
    plan_ev, k_at = [], 0
    for n0, n1, k0 in sorted(_EV_SEGS, key=lambda seg: seg[2]):
        if k0 > k_at:
            plan_ev.append(("z", k0 - k_at))
        plan_ev += _shard_pieces(0, n0, n1, es)
        k_at = k0 + n1 - n0
    assert k_at == EV_W and es * N_DEV == EV_NAT_W and os_ * N_DEV == OD_NAT_W
    plan_od = [[_shard_pieces(0, 0, OD_NAT_W, os_) + [("z", OD_W - OD_NAT_W)]]]

    def layer_weights(l, x_in):
        if l == 1:
            rest_g.update(zip(big_names, _exchange_wait(rest_handle, x_in, me, name="gather_rest_wait")))
        e = l // 2
        src = first_g if l == 0 else rest_g
        i = 0 if l == 0 else l - 1
        ie = 0 if (l == 0 or l % 2) else e - 1
        g1 = src["ffn_w1"].reshape(N_DEV, -1, fs)
        g3 = src["ffn_w3"].reshape(N_DEV, -1, fs)
        p = {"norm_g": fw["norm_g"][l],
             "w13": [_cols([(g1, (2 * i + j) * D_MODEL), (g3, (2 * i + j) * D_MODEL)], D_MODEL, plan13, BF16,
                           name="join_w13")[0][0] for j in range(2)],
             "w2": [src["ffn_w2"][:, i, j].reshape(D_FF, D_MODEL) for j in range(2)]}
        if l % 2 == 0:
            p["w_in_k"] = _cols([(src["ev_w_in"].reshape(N_DEV, -1, es), ie * D_MODEL)], D_MODEL, [[plan_ev]], BF16,
                                name="join_ev_in")[0][0]
            for n in ("mla_w_uq", "mla_w_ukv", "ev_w_out"):
                p[n] = _join(src[n][:, ie], axis_of[n] - 1)
            p["gdn_conv_w"] = fw["gdn_conv_w"][e]
            for n in ("gdn_A_log", "gdn_dt_bias", "gdn_norm_g", "mla_q_norm_g", "mla_kv_norm_g"):
                p[n] = w[n][e]
        else:
            p["w_in_k"] = _cols([(rest_g["ssd_w_in"].reshape(N_DEV, -1, os_), e * D_MODEL)], D_MODEL, plan_od, BF16,
                                name="join_od_in")[0][0]
            p["ssd_w_out"] = _join(rest_g["ssd_w_out"][:, e], axis_of["ssd_w_out"] - 1)
            for n in ("ssd_conv_w", "ssd_conv_b", "ssd_norm_g"):
                p[n] = fw[n][e]
            for n in ("ssd_A_log", "ssd_dt_bias", "ssd_D"):
                p[n] = w[n][e]
        return p

    def w13_cols(s, third):
        k0 = (s % 4) * fs + (2 * FF_HALF if s >= 4 else 0) + (FF_HALF if third else 0)
        return [(0, 0, k0, k0 + fs)]

    sent = {}

    def layer_done(l, stage, gl, dx_l):
        d1, d3 = _cols([(gl["w13"][None], 0)], D_MODEL, [[w13_cols(s, False) for s in shards],
                                                         [w13_cols(s, True) for s in shards]], F32, name="split_w13")
        pieces = {"ffn_w1": d1[:, None, None], "ffn_w3": d3[:, None, None],
                  "ffn_w2": gl["w2"].reshape(N_DEV, -1, D_MODEL)[:, None, None]}
        if stage == 1:
            pieces["norm_g"] = _split(gl["norm_g"][None], 2)
        elif l % 2 == 0:
            pieces["ev_w_in"] = _cols([(gl["w_in_k"][None], 0)], D_MODEL,
                                      [[_mapped_pieces(s * es, (s + 1) * es, _EV_SEGS) for s in shards]], F32,
                                      name="split_ev_in")[0][:, None]
            for n in ("gdn_conv_w", "mla_w_uq", "mla_w_ukv", "ev_w_out"):
                pieces[n] = _split(gl[n][None], axis_of[n])
        else:
            pieces["ssd_w_in"] = _cols([(gl["w_in_k"][None], 0)], D_MODEL,
                                       [[[(0, 0, s * os_, (s + 1) * os_)] for s in shards]], F32,
                                       name="split_od_in")[0][:, None]
            for n in ("ssd_conv_w", "ssd_conv_b", "ssd_norm_g", "ssd_w_out"):
                pieces[n] = _split(gl[n][None], axis_of[n])
        names = list(pieces)
        if l == 0 and stage == 1:
            sent[l, stage] = dict(zip(names, _exchange([pieces[n] for n in names], True, name="scatter_last")))
            return None
        handle, token = _exchange_start([pieces[n] for n in names], True, name=f"scatter_start_{l}_{stage}")
        sent[l, stage] = (names, handle)
        return token

    n_col = ada_w.shape[2]
    ada_b_shard = lax.dynamic_slice(ada_b, (0, me * n_col), (DEPTH, n_col)).reshape(DEPTH, 1, n_col)
    mod_all = _exchange([_mod_shard(c_all, ada_w, ada_b_shard, name="mod")], False, name="gather_mod")[0]
    mod_me = lax.dynamic_index_in_dim(mod_all, me, axis=2, keepdims=False)
    mod = jnp.transpose(mod_me, (1, 0, 2)).reshape(DEPTH, N_DEV * n_col) + rest_token

    pos = positions.astype(F32).reshape(t_len, 1)
    loss, dx, grads, dmod = _local_step(x[0], loss_target[0], pos, mod, final_g, layer_weights, layer_done)
    for (l, stage), started in list(sent.items()):
        if not isinstance(started, dict):
            names, handle = started
            sent[l, stage] = dict(zip(names, _exchange_wait(handle, dx, me, name=f"scatter_wait_{l}_{stage}")))

    repl_shapes = [w[n].shape for n in _REPL] + [(1,)]
    parts8 = _exchange([_pack([dmod] + [grads[n] for n in _REPL[1:]] + [loss[0, :1]])], False, name="gather_repl")[0]
    zero = jnp.zeros((1,), F32)
    r_grad, r_delta, r_m, r_v = [
        _unpack(o, repl_shapes) for o in _adamw(_pack([w[n] for n in _REPL] + [zero]), parts8,
                                                _pack([m[n] for n in _REPL] + [zero]),
                                                _pack([v[n] for n in _REPL] + [zero]), name="adamw_repl")]
    out = {"grad": {}, "delta": {}, "m": {}, "v": {}}
    for i, n in enumerate(_REPL):
        out["grad"][n], out["delta"][n], out["m"][n], out["v"][n] = r_grad[i], r_delta[i], r_m[i], r_v[i]
    loss_total = r_grad[-1].reshape(())

    dmod_all = parts8[:, :dmod.size // PACK_W].reshape((N_DEV,) + dmod.shape)
    dmod_cols = jnp.transpose(lax.dynamic_slice_in_dim(dmod_all, me * n_col, n_col, axis=2), (1, 0, 2))
    g_ada = _ada_w_grad(c_all, dmod_cols, name="ada_w_grad")
    for k, o in zip(("grad", "delta", "m", "v"), _adamw_nd(ada_w, g_ada[None], m["ada_w"], v["ada_w"], name="adamw_ada")):
        out[k]["ada_w"] = o

    for n in small_names + big_names:
        if n == "norm_g":
            g8 = jnp.concatenate([sent[l, 1][n] for l in range(DEPTH)], axis=1)
        elif n.startswith("ffn"):
            g8 = jnp.concatenate([jnp.concatenate([sent[l, 1][n], sent[l, 0][n]], axis=2) for l in range(DEPTH)], axis=1)
        else:
            layers = range(1, DEPTH, 2) if n.startswith("ssd") else range(0, DEPTH, 2)
            g8 = jnp.concatenate([sent[l, 0][n] for l in layers], axis=1)
        for k, o in zip(("grad", "delta", "m", "v"), _adamw_nd(w[n], g8, m[n], v[n], name="adamw_" + n)):
            out[k][n] = o

    return (loss_total, dx.reshape(x.shape), *[out["grad"][n] for n in _WEIGHTS], *[out["delta"][n] for n in _WEIGHTS],
            *[out["m"][n] for n in _WEIGHTS], *[out["v"][n] for n in _WEIGHTS])
```

```python
import functools
import math

import numpy as np
import jax
import jax.numpy as jnp
from jax import lax
from jax.experimental import pallas as pl
from jax.experimental.pallas import tpu as pltpu

F32 = jnp.float32
BF16 = jnp.bfloat16
HI = lax.Precision.HIGHEST

D_MODEL = 1024
DEPTH = 4
CHUNK = 64
NORM_EPS = 1e-6
CONV_K = 4
D_FF = 2816
GDN_HEADS = 4
GDN_DK = 128
MLA_HEADS = 4
MLA_NOPE = 128
MLA_ROPE = 64
ROPE_THETA = 10000.0
SSD_HEADS = 32
SSD_HEADDIM = 64
SSD_GROUPS = 4
SSD_STATE = 128
SSD_D_INNER = 2048
N_DEV = 8

ADAM_LR = 0.001
ADAM_B1 = 0.9
ADAM_B2 = 0.999
ADAM_EPS = 1e-08
ADAM_WD = 0.01
ADAM_STEP = 10

V7X_VMEM_LIMIT = 56 * 1024 * 1024
ROW_TILE = 512
SEQ_TILE = 256
ATT_TILE = 1024
MM_RESIDENT_BYTES = 12 * 1024 * 1024
FF_HALF = D_FF // 2
LANE = 128

EV_QKV, EV_CQ, EV_MISC, EV_Z, EV_CKV, EV_W = 0, 1536, 1920, 2048, 2560, 2816
OD_Z, OD_XBC, OD_DT, OD_W = 0, 2048, 5120, 5376


def _cparams(sem=None):
    return pltpu.CompilerParams(dimension_semantics=sem, vmem_limit_bytes=V7X_VMEM_LIMIT)


def _pick(n, cands):
    for c in cands:
        if n % c == 0:
            return c
    return n


_DN = {"nn": (((1,), (0,)), ((), ())), "nt": (((1,), (1,)), ((), ())), "tn": (((0,), (0,)), ((), ()))}


F32_3PASS = 2


def _dot(a, b, mode, hi=False):
    if hi:
        prec = lax.Precision.HIGH if hi == F32_3PASS else HI
        return lax.dot_general(a.astype(F32), b.astype(F32), _DN[mode], precision=prec, preferred_element_type=F32)
    return lax.dot_general(a.astype(BF16), b.astype(BF16), _DN[mode], preferred_element_type=F32)


@functools.partial(jax.custom_vjp, nondiff_argnums=(2, 3))
def _mm(a, b, mode, hi):
    return _dot(a, b, mode, hi)


def _mm_fwd(a, b, mode, hi):
    return _dot(a, b, mode, hi), (a, b)


def _mm_bwd(mode, hi, res, g):
    a, b = res
    if mode == "nn":
        return _dot(g, b, "nt", hi), _dot(a, g, "tn", hi)
    if mode == "nt":
        return _dot(g, b, "nn", hi), _dot(g, a, "tn", hi)
    return _dot(b, g, "nt", hi), _dot(a, g, "nn", hi)


_mm.defvjp(_mm_fwd, _mm_bwd)


def _iota2(shape, dim):
    return lax.broadcasted_iota(jnp.int32, shape, dim)


def _row_spec(a, tm):
    if isinstance(a, tuple):
        arr, c0, w = a
        assert c0 % w == 0
        cb = c0 // w
        return arr, pl.BlockSpec((tm, w), lambda i, cb=cb: (i, cb))
    return a, pl.BlockSpec((tm, a.shape[1]), lambda i: (i, 0))


def _full_spec(b):
    return pl.BlockSpec(b.shape, lambda i: (0,) * b.ndim)


def _rows(fn, tiled, bcast, outs, *, name, tm=ROW_TILE, also_t=()):
    arrs, specs = zip(*[_row_spec(a, 0) for a in tiled])
    t_len = arrs[0].shape[0]
    tm = min(tm, t_len)
    arrs, specs = zip(*[_row_spec(a, tm) for a in tiled])
    nt, nb, no = len(tiled), len(bcast), len(outs)

    def body(*refs):
        ins = [r[...].astype(F32) for r in refs[:nt]] + [r[...] for r in refs[nt:nt + nb]]
        res = fn(*ins)
        for r, v in zip(refs[nt + nb:nt + nb + no], res):
            r[...] = v.astype(r.dtype)
        for r, k in zip(refs[nt + nb + no:], also_t):
            r[...] = res[k].T.astype(r.dtype)

    return pl.pallas_call(
        body, grid=(t_len // tm,), name=name,
        in_specs=list(specs) + [_full_spec(b) for b in bcast],
        out_specs=[pl.BlockSpec((tm, c), lambda i: (i, 0)) for c, _ in outs]
        + [pl.BlockSpec((outs[k][0], tm), lambda i: (0, i)) for k in also_t],
        out_shape=[jax.ShapeDtypeStruct((t_len, c), dt) for c, dt in outs]
        + [jax.ShapeDtypeStruct((outs[k][0], t_len), outs[k][1]) for k in also_t],
        compiler_params=_cparams(("parallel",)),
    )(*arrs, *bcast)


def _rows_vjp(fn, tiled, consts, bcast, bconsts, douts, grads, *, name, adds=None, tm=ROW_TILE // 2):
    adds = adds or {}
    t_arrs, t_specs = zip(*[_row_spec(a, 0) for a in tiled])
    t_len = t_arrs[0].shape[0]
    tm = min(tm, t_len)
    rows_in = list(tiled) + list(consts) + list(douts) + [adds[k] for k in sorted(adds)]
    arrs, specs = zip(*[_row_spec(a, tm) for a in rows_in])
    nt, nc, nb, nbc, nd, na = len(tiled), len(consts), len(bcast), len(bconsts), len(douts), len(adds)
    add_pos = {k: j for j, k in enumerate(sorted(adds))}
    want = [j for j, g in enumerate(grads) if g is not None]

    def body(*refs):
        p = 0
        t = [r[...].astype(F32) for r in refs[p:p + nt]]; p += nt
        c = [r[...].astype(F32) for r in refs[p:p + nc]]; p += nc
        d = [r[...].astype(F32) for r in refs[p:p + nd]]; p += nd
        a = [r[...].astype(F32) for r in refs[p:p + na]]; p += na
        b = [r[...] for r in refs[p:p + nb]]; p += nb
        bc = [r[...] for r in refs[p:p + nbc]]; p += nbc
        g_refs = refs[p:p + len(want)]; p += len(want)
        gb_refs = refs[p:p + nb]

        def f(*args):
            return fn(*args[:nt], *c, *args[nt:], *bc)

        _, vjp = jax.vjp(f, *t, *b)
        g = vjp(tuple(d))
        for r, j in zip(g_refs, want):
            val = g[j]
            if j in add_pos:
                val = val + a[add_pos[j]]
            r[...] = val.astype(r.dtype)

        @pl.when(pl.program_id(0) == 0)
        def _():
            for r in gb_refs:
                r[...] = jnp.zeros_like(r)

        for r, val in zip(gb_refs, g[nt:]):
            r[...] += val

    def width(a):
        return a[2] if isinstance(a, tuple) else a.shape[1]

    res = pl.pallas_call(
        body, grid=(t_len // tm,), name=name,
        in_specs=list(specs) + [_full_spec(b) for b in list(bcast) + list(bconsts)],
        out_specs=[pl.BlockSpec((tm, width(tiled[j])), lambda i: (i, 0)) for j in want] + [_full_spec(b) for b in bcast],
        out_shape=[jax.ShapeDtypeStruct((t_len, width(tiled[j])), grads[j]) for j in want]
        + [jax.ShapeDtypeStruct(b.shape, F32) for b in bcast],
        compiler_params=_cparams(("arbitrary",)),
    )(*arrs, *bcast, *bconsts)
    tg = [None] * nt
    for r, j in zip(res[:len(want)], want):
        tg[j] = r
    return tg, list(res[len(want):])


def _matmul(a, b, mode, out_dtype, *, name):
    if mode in ("tn", "kn"):
        assert out_dtype == F32
        k_len, m_len = a.shape if mode == "tn" else a.shape[::-1]
        n_len = b.shape[1]
        tm, tn = m_len, n_len
        while tm * tn * 4 > MM_RESIDENT_BYTES and tn % (2 * LANE) == 0:
            tn //= 2
        tk = _pick(k_len, (512, 256, 128))
        if mode == "kn" and k_len % 1024 == 0 and 4 * 1024 * (tm + tn) + 8 * tm * tn <= 44 * 1024 * 1024:
            tk = 1024
    else:
        m_len, k_len = a.shape
        n_len = b.shape[1] if mode == "nn" else b.shape[0]
        tk, tn = k_len, n_len
        while tk * tn * 2 > MM_RESIDENT_BYTES and tn % (2 * LANE) == 0:
            tn //= 2
        tm = _pick(m_len, (512, 256, 128))
        while tm * max(4 * tn, 2 * tk) > MM_RESIDENT_BYTES // 2 and tm % 256 == 0:
            tm //= 2
    nk = k_len // tk
    if mode == "nn":
        a_spec = pl.BlockSpec((tm, tk), lambda j, i, k: (i, k))
        b_spec = pl.BlockSpec((tk, tn), lambda j, i, k: (k, j))
    elif mode == "nt":
        a_spec = pl.BlockSpec((tm, tk), lambda j, i, k: (i, k))
        b_spec = pl.BlockSpec((tn, tk), lambda j, i, k: (j, k))
    elif mode == "kn":
        a_spec = pl.BlockSpec((tm, tk), lambda j, i, k: (i, k))
        b_spec = pl.BlockSpec((tk, tn), lambda j, i, k: (k, j))
    else:
        a_spec = pl.BlockSpec((tk, tm), lambda j, i, k: (k, i))
        b_spec = pl.BlockSpec((tk, tn), lambda j, i, k: (k, j))

    def body(a_ref, b_ref, o_ref):
        part = _dot(a_ref[...], b_ref[...], "nn" if mode == "kn" else mode)
        if nk == 1:
            o_ref[...] = part.astype(o_ref.dtype)
        else:
            @pl.when(pl.program_id(2) == 0)
            def _():
                o_ref[...] = jnp.zeros_like(o_ref)

            o_ref[...] += part

    return pl.pallas_call(
        body, grid=(n_len // tn, m_len // tm, nk), name=name,
        in_specs=[a_spec, b_spec],
        out_specs=pl.BlockSpec((tm, tn), lambda j, i, k: (i, j)),
        out_shape=jax.ShapeDtypeStruct((m_len, n_len), out_dtype),
        compiler_params=_cparams(("parallel", "parallel", "arbitrary")),
    )(a, b)


def _ffn_act(h, w13, *, name):
    t_len, d = h.shape
    tm = min(ROW_TILE, t_len)

    def body(h_ref, w_ref, s_ref, st_ref, ab_ref):
        ab = _dot(h_ref[...], w_ref[...], "nn")
        a, b = ab[:, :FF_HALF], ab[:, FF_HALF:]
        s = a * jax.nn.sigmoid(a) * b
        s_ref[...] = s.astype(s_ref.dtype)
        st_ref[...] = s.T.astype(st_ref.dtype)
        ab_ref[...] = ab.astype(ab_ref.dtype)

    return pl.pallas_call(
        body, grid=(2, t_len // tm), name=name,
        in_specs=[pl.BlockSpec((tm, d), lambda f, i: (i, 0)), pl.BlockSpec((d, 2 * FF_HALF), lambda f, i: (0, f))],
        out_specs=[pl.BlockSpec((tm, FF_HALF), lambda f, i: (i, f)), pl.BlockSpec((FF_HALF, tm), lambda f, i: (f, i)),
                   pl.BlockSpec((tm, 2 * FF_HALF), lambda f, i: (i, f))],
        out_shape=[jax.ShapeDtypeStruct((t_len, D_FF), BF16), jax.ShapeDtypeStruct((D_FF, t_len), BF16),
                   jax.ShapeDtypeStruct((t_len, 2 * D_FF), BF16)],
        compiler_params=_cparams(("parallel", "parallel")),
    )(h, w13)


def _ffn_act_bwd(ab, dy, w2, *, name):
    t_len = ab.shape[0]
    d = dy.shape[1]
    tm = min(ROW_TILE, t_len)

    def body(ab_ref, dy_ref, w2_ref, o_ref):
        abv = ab_ref[...].astype(F32)
        a, b = abv[:, :FF_HALF], abv[:, FF_HALF:]
        ds = _dot(dy_ref[...], w2_ref[...], "nt")
        sig = jax.nn.sigmoid(a)
        silu = a * sig
        da = ds * b * (sig * (1.0 + a * (1.0 - sig)))
        db = ds * silu
        o_ref[...] = jnp.concatenate([da, db], axis=-1).astype(o_ref.dtype)

    return pl.pallas_call(
        body, grid=(2, t_len // tm), name=name,
        in_specs=[pl.BlockSpec((tm, 2 * FF_HALF), lambda f, i: (i, f)), pl.BlockSpec((tm, d), lambda f, i: (i, 0)),
                  pl.BlockSpec((FF_HALF, d), lambda f, i: (f, 0))],
        out_specs=pl.BlockSpec((tm, 2 * FF_HALF), lambda f, i: (i, f)),
        out_shape=jax.ShapeDtypeStruct((t_len, 2 * D_FF), BF16),
        compiler_params=_cparams(("parallel", "parallel")),
    )(ab, dy, w2)


CONV_CB = 512
HALO = 8


def _conv_fwd(p, c0, n_ch, w, b, *, name):
    t_len = p.shape[0]
    tm = min(ROW_TILE, t_len)
    hb = tm // HALO
    cb0 = c0 // CONV_CB

    def body(x_ref, halo_ref, w_ref, b_ref, act_ref, pre_ref):
        first = pl.program_id(1) == 0
        halo = jnp.where(first, 0.0, halo_ref[...])
        xx = jnp.concatenate([halo, x_ref[...]], axis=0)
        wv = w_ref[...]
        acc = b_ref[...] + wv[CONV_K - 1:CONV_K] * xx[HALO:]
        for j in range(CONV_K - 1):
            acc = acc + wv[j:j + 1] * pltpu.roll(xx, CONV_K - 1 - j, 0)[HALO:]
        pre_ref[...] = acc
        act_ref[...] = acc * jax.nn.sigmoid(acc)

    return pl.pallas_call(
        body, grid=(n_ch // CONV_CB, t_len // tm), name=name,
        in_specs=[pl.BlockSpec((tm, CONV_CB), lambda j, i: (i, cb0 + j)),
                  pl.BlockSpec((HALO, CONV_CB), lambda j, i: (jnp.maximum(i * hb - 1, 0), cb0 + j)),
                  pl.BlockSpec((CONV_K, CONV_CB), lambda j, i: (0, j)),
                  pl.BlockSpec((1, CONV_CB), lambda j, i: (0, j))],
        out_specs=[pl.BlockSpec((tm, CONV_CB), lambda j, i: (i, j))] * 2,
        out_shape=[jax.ShapeDtypeStruct((t_len, n_ch), F32)] * 2,
        compiler_params=_cparams(("parallel", "arbitrary")),
    )(p, p, w, b)


def _conv_bwd(dact, pre, pre_c0, p, p_c0, w, *, name):
    t_len, n_ch = dact.shape
    tm = min(ROW_TILE, t_len)
    hb = tm // HALO
    nt = t_len // tm
    last_hb = t_len // HALO - 1
    cb0 = p_c0 // CONV_CB
    cbp = pre_c0 // CONV_CB

    def dsilu(z):
        sig = jax.nn.sigmoid(z)
        return sig * (1.0 + z * (1.0 - sig))

    def body(d_ref, dn_ref, pre_ref, pren_ref, x_ref, xh_ref, w_ref, dx_ref, dw_ref, db_ref):
        i = pl.program_id(1)
        dpre = d_ref[...] * dsilu(pre_ref[...])
        dnext = jnp.where(i == nt - 1, 0.0, dn_ref[...] * dsilu(pren_ref[...]))
        ext = jnp.concatenate([dpre, dnext], axis=0)
        xx = jnp.concatenate([jnp.where(i == 0, 0.0, xh_ref[...]), x_ref[...]], axis=0)
        wv = w_ref[...]
        dx = wv[CONV_K - 1:CONV_K] * dpre
        for j in range(CONV_K - 1):
            dx = dx + wv[j:j + 1] * pltpu.roll(ext, tm + HALO - (CONV_K - 1 - j), 0)[:tm]
        dx_ref[...] = dx
        dws = [jnp.sum(dpre * pltpu.roll(xx, CONV_K - 1 - j, 0)[HALO:], axis=0, keepdims=True) for j in range(CONV_K - 1)]
        dws.append(jnp.sum(dpre * x_ref[...], axis=0, keepdims=True))

        @pl.when(i == 0)
        def _():
            dw_ref[...] = jnp.zeros_like(dw_ref)
            db_ref[...] = jnp.zeros_like(db_ref)

        dw_ref[...] += jnp.concatenate(dws, axis=0)
        db_ref[...] += jnp.sum(dpre, axis=0, keepdims=True)

    tile = lambda off: pl.BlockSpec((tm, CONV_CB), lambda j, i: (i, off + j))
    nxt = lambda off: pl.BlockSpec((HALO, CONV_CB), lambda j, i: (jnp.minimum((i + 1) * hb, last_hb), off + j))
    return pl.pallas_call(
        body, grid=(n_ch // CONV_CB, nt), name=name,
        in_specs=[tile(0), nxt(0), tile(cbp), nxt(cbp), tile(cb0),
                  pl.BlockSpec((HALO, CONV_CB), lambda j, i: (jnp.maximum(i * hb - 1, 0), cb0 + j)),
                  pl.BlockSpec((CONV_K, CONV_CB), lambda j, i: (0, cbp + j))],
        out_specs=[tile(0), pl.BlockSpec((CONV_K, CONV_CB), lambda j, i: (0, j)), pl.BlockSpec((1, CONV_CB), lambda j, i: (0, j))],
        out_shape=[jax.ShapeDtypeStruct((t_len, n_ch), F32), jax.ShapeDtypeStruct((CONV_K, n_ch), F32),
                   jax.ShapeDtypeStruct((1, n_ch), F32)],
        compiler_params=_cparams(("parallel", "arbitrary")),
    )(dact, dact, pre, pre, p, p, w)


@jax.custom_vjp
def _inv_unit_lower_many(a_cat):
    c = a_cat.shape[0]
    assert LANE % c == 0 and a_cat.shape[1] % LANE == 0
    x = (_iota2(a_cat.shape, 0) == _iota2(a_cat.shape, 1) % c).astype(F32)
    tiles = [a_cat[:, t * LANE:(t + 1) * LANE] for t in range(a_cat.shape[1] // LANE)]
    first = (_iota2((c, LANE), 1) // c) * c
    for j in range(c - 1):
        col = jnp.concatenate([jnp.take_along_axis(t, first + j, axis=1) for t in tiles], axis=-1)
        x = x - col * x[j:j + 1, :]
    return x


def _inv_fwd(a_cat):
    x = _inv_unit_lower_many(a_cat)
    return x, x


def _inv_bwd(x, g):
    c = x.shape[0]
    parts = [-_dot(x[:, s], _dot(g[:, s], x[:, s], "nt", F32_3PASS), "tn", F32_3PASS)
             for s in (slice(i * c, (i + 1) * c) for i in range(x.shape[1] // c))]
    return (jnp.concatenate(parts, axis=-1),)


_inv_unit_lower_many.defvjp(_inv_fwd, _inv_bwd)


def _l2norm(x):
    return x * lax.rsqrt(jnp.sum(x * x, axis=-1, keepdims=True) + NORM_EPS)


def _rms(x):
    return x * lax.rsqrt(jnp.mean(x * x, axis=-1, keepdims=True) + NORM_EPS)


def _tri_masks(c):
    rows, cols = _iota2((c, c), 0), _iota2((c, c), 1)
    return rows >= cols, rows > cols, (rows >= cols).astype(F32), (rows <= cols).astype(F32)


def _gdn_tile(q, k, v, misc, s0, alog, dtb):
    c = CHUNK
    lower, strict, ltri, utri = _tri_masks(c)
    n_chunk = q.shape[0] // c
    pre = []
    for h in range(GDN_HEADS):
        hs = slice(h * LANE, (h + 1) * LANE)
        neg_a = -jnp.exp(alog[:, h:h + 1])
        for ci in range(n_chunk):
            sl = slice(ci * c, (ci + 1) * c)
            qn = _l2norm(q[sl, hs]) * (GDN_DK ** -0.5)
            kn = _l2norm(k[sl, hs])
            beta = jax.nn.sigmoid(misc[sl, 64 + h:65 + h])
            g = neg_a * jax.nn.softplus(misc[sl, 68 + h:69 + h] + dtb[:, h:h + 1])
            gb = jnp.broadcast_to(g, (c, c))
            gc_col = _mm(ltri, gb, "nn", True)
            gc_row = _mm(gb, utri, "tn", True)
            decay = jnp.where(lower, jnp.exp(jnp.where(lower, gc_col - gc_row, 0.0)), 0.0)
            kb = kn * beta
            a_mat = jnp.where(strict, _mm(kb, kn, "nt", False) * decay, 0.0)
            pre.append((qn, kn, kb, v[sl, hs] * beta, decay, gc_col[:, 0:1], gc_col[c - 1:c, 0:1], a_mat))
    t_all = _inv_unit_lower_many(jnp.concatenate([p[7] for p in pre], axis=-1))
    o_heads, s_heads = [], []
    for h in range(GDN_HEADS):
        s = s0[h * GDN_DK:(h + 1) * GDN_DK]
        outs = []
        for ci in range(n_chunk):
            i = h * n_chunk + ci
            qn, kn, kb, vb, decay, gc, g_last, _ = pre[i]
            t_inv = t_all[:, i * c:(i + 1) * c]
            u = _mm(t_inv, vb, "nn", F32_3PASS)
            w = _mm(t_inv, kb * jnp.exp(gc), "nn", F32_3PASS)
            attn = _mm(qn, kn, "nt", False) * decay
            k_end = kn * jnp.exp(g_last - gc)
            q_start = qn * jnp.exp(gc)
            v_new = u - _mm(w, s, "nn", False)
            outs.append(_mm(q_start, s, "nn", False) + _mm(attn, v_new, "nn", False))
            s = s * jnp.exp(g_last) + _mm(k_end, v_new, "tn", False)
        o_heads.append(jnp.concatenate(outs, axis=0))
        s_heads.append(s)
    return jnp.concatenate(o_heads, axis=-1), jnp.concatenate(s_heads, axis=0)


def _gdn_specs(tt, rev_n=None):
    t = (lambda i: i) if rev_n is None else (lambda i: rev_n - 1 - i)
    col = lambda j: pl.BlockSpec((tt, GDN_HEADS * LANE), lambda i: (t(i), j))
    vec = pl.BlockSpec((1, LANE), lambda i: (0, 0))
    misc = pl.BlockSpec((tt, LANE), lambda i: (t(i), EV_MISC // LANE))
    return [col(0), col(1), col(2), misc, vec, vec], t


def _gdn_fwd(act, p, alog, dtb, *, name):
    t_len = act.shape[0]
    tt = min(SEQ_TILE, t_len)
    ntile = t_len // tt
    in_specs, _ = _gdn_specs(tt)

    def body(q_ref, k_ref, v_ref, m_ref, al_ref, dt_ref, o_ref, s_ref, state):
        @pl.when(pl.program_id(0) == 0)
        def _():
            state[...] = jnp.zeros_like(state)

        s_ref[0] = state[...]
        o, s_new = _gdn_tile(q_ref[...], k_ref[...], v_ref[...], m_ref[...], state[...], al_ref[...], dt_ref[...])
        o_ref[...] = o
        state[...] = s_new

    return pl.pallas_call(
        body, grid=(ntile,), name=name, in_specs=in_specs,
        out_specs=[pl.BlockSpec((tt, GDN_HEADS * LANE), lambda i: (i, 0)),
                   pl.BlockSpec((1, GDN_HEADS * GDN_DK, LANE), lambda i: (i, 0, 0))],
        out_shape=[jax.ShapeDtypeStruct((t_len, GDN_HEADS * LANE), F32),
                   jax.ShapeDtypeStruct((ntile, GDN_HEADS * GDN_DK, LANE), F32)],
        scratch_shapes=[pltpu.VMEM((GDN_HEADS * GDN_DK, LANE), F32)],
        compiler_params=_cparams(("arbitrary",)),
    )(act, act, act, p, alog, dtb)


def _gdn_bwd(act, p, alog, dtb, states, do, *, name):
    t_len = act.shape[0]
    tt = min(SEQ_TILE, t_len)
    ntile = t_len // tt
    in_specs, t = _gdn_specs(tt, ntile)

    def body(q_ref, k_ref, v_ref, m_ref, al_ref, dt_ref, s0_ref, do_ref,
             dq_ref, dk_ref, dv_ref, dm_ref, dal_ref, ddt_ref, dstate):
        @pl.when(pl.program_id(0) == 0)
        def _():
            dstate[...] = jnp.zeros_like(dstate)
            dal_ref[...] = jnp.zeros_like(dal_ref)
            ddt_ref[...] = jnp.zeros_like(ddt_ref)

        _, vjp = jax.vjp(_gdn_tile, q_ref[...], k_ref[...], v_ref[...], m_ref[...], s0_ref[0], al_ref[...], dt_ref[...])
        dq, dk, dv, dm, ds0, dal, ddt = vjp((do_ref[...], dstate[...]))
        dq_ref[...] = dq
        dk_ref[...] = dk
        dv_ref[...] = dv
        dm_ref[...] = dm
        dstate[...] = ds0
        dal_ref[...] += dal
        ddt_ref[...] += ddt

    row = pl.BlockSpec((tt, GDN_HEADS * LANE), lambda i: (t(i), 0))
    vec = pl.BlockSpec((1, LANE), lambda i: (0, 0))
    return pl.pallas_call(
        body, grid=(ntile,), name=name,
        in_specs=in_specs + [pl.BlockSpec((1, GDN_HEADS * GDN_DK, LANE), lambda i: (t(i), 0, 0)), row],
        out_specs=[row, row, row, pl.BlockSpec((tt, LANE), lambda i: (t(i), 0)), vec, vec],
        out_shape=[jax.ShapeDtypeStruct((t_len, GDN_HEADS * LANE), F32)] * 3
        + [jax.ShapeDtypeStruct((t_len, LANE), F32)] + [jax.ShapeDtypeStruct((1, LANE), F32)] * 2,
        scratch_shapes=[pltpu.VMEM((GDN_HEADS * GDN_DK, LANE), F32)],
        compiler_params=_cparams(("arbitrary",)),
    )(act, act, act, p, alog, dtb, states, do)


def _head_expand():
    return jnp.asarray(np.repeat(np.eye(LANE, SSD_HEADS, dtype=np.float32), SSD_HEADDIM, axis=1))


@jax.custom_vjp
def _per_head(v, expand):
    rows = max(v.shape[0], 8)
    v8 = jnp.broadcast_to(v, (rows, LANE))
    half = _iota2((rows, LANE), 1) // SSD_HEADDIM
    tiles = [jnp.take_along_axis(v8, half + 2 * j, axis=1) for j in range(SSD_D_INNER // LANE)]
    return jnp.concatenate(tiles, axis=-1)[:v.shape[0]]


def _per_head_fwd(v, expand):
    return _per_head(v, expand), expand


def _per_head_bwd(expand, g):
    rows = g.shape[0]
    g8 = jnp.broadcast_to(g, (8, g.shape[1])) if rows == 1 else g
    dv = lax.dot_general(g8, expand, _DN["nt"], precision=lax.Precision.HIGH, preferred_element_type=F32)
    return dv[0:1] if rows == 1 else dv, None


_per_head.defvjp(_per_head_fwd, _per_head_bwd)


def _ssd_tile(xs, bm, cm, dtr, hs0, alog, dtb, dsk, expand):
    c = CHUNK
    gw = SSD_D_INNER // SSD_GROUPS
    hpg = SSD_HEADS // SSD_GROUPS
    lower, _, ltri, utri = _tri_masks(c)
    half = _iota2((c, LANE), 1) // SSD_HEADDIM
    dt = jax.nn.softplus(dtr + dtb)
    da = dt * (-jnp.exp(alog))
    xdt = xs * _per_head(dt, expand)
    d_x = _per_head(dsk, expand)
    hs = [hs0[g * SSD_STATE:(g + 1) * SSD_STATE] for g in range(SSD_GROUPS)]
    ys = []
    for ci in range(xs.shape[0] // c):
        sl = slice(ci * c, (ci + 1) * c)
        acs = _mm(ltri, da[sl], "nn", True)
        acs_t = _mm(da[sl], utri, "tn", True)
        acs_last = acs[c - 1:c, :]
        e_start = _per_head(jnp.exp(acs), expand)
        e_end = _per_head(jnp.exp(acs_last - acs), expand)
        e_dec = _per_head(jnp.exp(acs_last), expand)
        xdt_c = xdt[sl]
        y_tiles = [None] * (SSD_D_INNER // LANE)
        y_off = []
        for g in range(SSD_GROUPS):
            b_g = bm[sl, g * SSD_STATE:(g + 1) * SSD_STATE]
            c_g = cm[sl, g * SSD_STATE:(g + 1) * SSD_STATE]
            gs = slice(g * gw, (g + 1) * gw)
            cb = _mm(c_g, b_g, "nt", False)
            y_off.append(_mm(c_g, hs[g], "nn", False) * e_start[:, gs])
            for r in range(hpg):
                h = g * hpg + r
                j = h // 2
                lm = jnp.where(lower, jnp.exp(jnp.where(lower, acs[:, h:h + 1] - acs_t[h:h + 1, :], 0.0)), 0.0)
                xm = jnp.where(half == (h % 2), xdt_c[:, j * LANE:(j + 1) * LANE], 0.0)
                part = _mm(cb * lm, xm, "nn", False)
                y_tiles[j] = part if y_tiles[j] is None else y_tiles[j] + part
            hs[g] = hs[g] * e_dec[:, gs] + _mm(b_g, xdt_c[:, gs] * e_end[:, gs], "tn", False)
        ys.append(jnp.concatenate(y_tiles, axis=-1) + jnp.concatenate(y_off, axis=-1) + d_x * xs[sl])
    return jnp.concatenate(ys, axis=0), jnp.concatenate(hs, axis=0)


def _ssd_specs(tt, rev_n=None):
    t = (lambda i: i) if rev_n is None else (lambda i: rev_n - 1 - i)
    vec = pl.BlockSpec((1, LANE), lambda i: (0, 0))
    specs = [pl.BlockSpec((tt, SSD_D_INNER), lambda i: (t(i), 0)),
             pl.BlockSpec((tt, 512), lambda i: (t(i), SSD_D_INNER // 512)),
             pl.BlockSpec((tt, 512), lambda i: (t(i), SSD_D_INNER // 512 + 1)),
             pl.BlockSpec((tt, LANE), lambda i: (t(i), OD_DT // LANE)), vec, vec, vec,
             pl.BlockSpec((LANE, SSD_D_INNER), lambda i: (0, 0))]
    return specs, t


def _ssd_fwd(act, p, alog, dtb, dsk, *, name):
    t_len = act.shape[0]
    tt = min(SEQ_TILE, t_len)
    ntile = t_len // tt
    in_specs, _ = _ssd_specs(tt)

    def body(x_ref, b_ref, c_ref, dt_ref, al_ref, db_ref, dk_ref, e_ref, y_ref, s_ref, state):
        @pl.when(pl.program_id(0) == 0)
        def _():
            state[...] = jnp.zeros_like(state)

        s_ref[0] = state[...]
        y, hs = _ssd_tile(x_ref[...], b_ref[...], c_ref[...], dt_ref[...], state[...], al_ref[...], db_ref[...],
                          dk_ref[...], e_ref[...])
        y_ref[...] = y
        state[...] = hs

    return pl.pallas_call(
        body, grid=(ntile,), name=name, in_specs=in_specs,
        out_specs=[pl.BlockSpec((tt, SSD_D_INNER), lambda i: (i, 0)),
                   pl.BlockSpec((1, SSD_GROUPS * SSD_STATE, 512), lambda i: (i, 0, 0))],
        out_shape=[jax.ShapeDtypeStruct((t_len, SSD_D_INNER), F32),
                   jax.ShapeDtypeStruct((ntile, SSD_GROUPS * SSD_STATE, 512), F32)],
        scratch_shapes=[pltpu.VMEM((SSD_GROUPS * SSD_STATE, 512), F32)],
        compiler_params=_cparams(("arbitrary",)),
    )(act, act, act, p, alog, dtb, dsk, _head_expand())


def _ssd_bwd(act, p, alog, dtb, dsk, states, dy, *, name):
    t_len = act.shape[0]
    tt = min(SEQ_TILE, t_len)
    ntile = t_len // tt
    in_specs, t = _ssd_specs(tt, ntile)

    def body(x_ref, b_ref, c_ref, dt_ref, al_ref, db_ref, dk_ref, e_ref, s0_ref, dy_ref,
             dx_ref, dbm_ref, dcm_ref, ddt_ref, dal_ref, ddb_ref, ddk_ref, dstate):
        @pl.when(pl.program_id(0) == 0)
        def _():
            dstate[...] = jnp.zeros_like(dstate)
            dal_ref[...] = jnp.zeros_like(dal_ref)
            ddb_ref[...] = jnp.zeros_like(ddb_ref)
            ddk_ref[...] = jnp.zeros_like(ddk_ref)

        expand = e_ref[...]

        def f(xs, bm, cm, dtr, hs0, al, db, dk):
            return _ssd_tile(xs, bm, cm, dtr, hs0, al, db, dk, expand)

        _, vjp = jax.vjp(f, x_ref[...], b_ref[...], c_ref[...], dt_ref[...], s0_ref[0], al_ref[...], db_ref[...],
                         dk_ref[...])
        dx, dbm, dcm, ddt, dhs, dal, ddb, ddk = vjp((dy_ref[...], dstate[...]))
        dx_ref[...] = dx
        dbm_ref[...] = dbm
        dcm_ref[...] = dcm
        ddt_ref[...] = ddt
        dstate[...] = dhs
        dal_ref[...] += dal
        ddb_ref[...] += ddb
        ddk_ref[...] += ddk

    vec = pl.BlockSpec((1, LANE), lambda i: (0, 0))
    rows = lambda w: pl.BlockSpec((tt, w), lambda i: (t(i), 0))
    return pl.pallas_call(
        body, grid=(ntile,), name=name,
        in_specs=in_specs + [pl.BlockSpec((1, SSD_GROUPS * SSD_STATE, 512), lambda i: (t(i), 0, 0)), rows(SSD_D_INNER)],
        out_specs=[rows(SSD_D_INNER), rows(512), rows(512), rows(LANE), vec, vec, vec],
        out_shape=[jax.ShapeDtypeStruct((t_len, SSD_D_INNER), F32), jax.ShapeDtypeStruct((t_len, 512), F32),
                   jax.ShapeDtypeStruct((t_len, 512), F32), jax.ShapeDtypeStruct((t_len, LANE), F32)]
        + [jax.ShapeDtypeStruct((1, LANE), F32)] * 3,
        scratch_shapes=[pltpu.VMEM((SSD_GROUPS * SSD_STATE, 512), F32)],
        compiler_params=_cparams(("arbitrary",)),
    )(act, act, act, p, alog, dtb, dsk, _head_expand(), states, dy)


ATT_SCALE = (MLA_NOPE + MLA_ROPE) ** -0.5
ATT_SCALE2 = ATT_SCALE * math.log2(math.e)
QK_W = 2 * LANE


def _chunk_mask(tq):
    return (_iota2((tq, tq), 1) // CHUNK) <= (_iota2((tq, tq), 0) // CHUNK)


def _attn_fwd(qc, kc, vv, *, name):
    t_len = qc.shape[0]
    tq = min(ATT_TILE, t_len)
    nq = t_len // tq

    pairs = [(qi, ki) for qi in range(nq) for ki in range(qi + 1)]
    q_tab = jnp.asarray([p[0] for p in pairs], jnp.int32)
    k_tab = jnp.asarray([p[1] for p in pairs], jnp.int32)

    def body(qt_ref, kt_ref, q_ref, k_ref, v_ref, o_ref, lse_ref, m_s, l_s, acc_s):
        qi, ki = qt_ref[pl.program_id(1)], kt_ref[pl.program_id(1)]

        @pl.when(ki == 0)
        def _():
            m_s[...] = jnp.full_like(m_s, -jnp.inf)
            l_s[...] = jnp.zeros_like(l_s)
            acc_s[...] = jnp.zeros_like(acc_s)

        def step(masked):
            s = _dot(q_ref[...], k_ref[...], "nt") * ATT_SCALE2
            if masked:
                s = jnp.where(_chunk_mask(tq), s, -jnp.inf)
            m_new = jnp.maximum(m_s[...], jnp.max(s, axis=-1, keepdims=True))
            alpha = jnp.exp2(m_s[...] - m_new)
            p = jnp.exp2(s - m_new)
            l_s[...] = alpha * l_s[...] + jnp.sum(p, axis=-1, keepdims=True)
            acc_s[...] = alpha * acc_s[...] + _dot(p, v_ref[...], "nn")
            m_s[...] = m_new

        @pl.when(ki < qi)
        def _():
            step(False)

        @pl.when(ki == qi)
        def _():
            step(True)
            o_ref[...] = acc_s[...] / l_s[...]
            lse_ref[...] = jnp.broadcast_to(m_s[...] + jnp.log2(l_s[...]), lse_ref.shape)

    q_idx = lambda h, s, qt, kt: (qt[s], h)
    k_idx = lambda h, s, qt, kt: (kt[s], h)
    return pl.pallas_call(
        body, name=name,
        grid_spec=pltpu.PrefetchScalarGridSpec(
            num_scalar_prefetch=2, grid=(MLA_HEADS, len(pairs)),
            in_specs=[pl.BlockSpec((tq, QK_W), q_idx), pl.BlockSpec((tq, QK_W), k_idx), pl.BlockSpec((tq, LANE), k_idx)],
            out_specs=[pl.BlockSpec((tq, LANE), q_idx)] * 2,
            scratch_shapes=[pltpu.VMEM((tq, 1), F32), pltpu.VMEM((tq, 1), F32), pltpu.VMEM((tq, LANE), F32)]),
        out_shape=[jax.ShapeDtypeStruct((t_len, MLA_HEADS * LANE), F32)] * 2,
        compiler_params=_cparams(("parallel", "arbitrary")),
    )(q_tab, k_tab, qc, kc, vv)


def _attn_probs(q, k, v, do, o, lse, masked, tq):
    s = _dot(q, k, "nt") * ATT_SCALE2
    if masked:
        s = jnp.where(_chunk_mask(tq), s, -jnp.inf)
    p = jnp.exp2(s - lse[:, 0:1])
    delta = jnp.sum(do * o, axis=-1, keepdims=True)
    ds = p * (_dot(do, v, "nt") - delta)
    return p, ds


def _attn_bwd(qc, kc, vv, o, lse, do, *, name):
    t_len = qc.shape[0]
    tq = min(ATT_TILE, t_len)
    nq = t_len // tq

    pairs = [(qi, ki) for ki in range(nq) for qi in range(ki, nq)]
    q_tab = jnp.asarray([p[0] for p in pairs], jnp.int32)
    k_tab = jnp.asarray([p[1] for p in pairs], jnp.int32)

    def body(qt_ref, kt_ref, q_ref, k_ref, v_ref, o_ref, lse_ref, do_ref, dq_hbm, dk_ref, dv_ref, dq_s, dk_s, dv_s):
        head = pl.program_id(0)
        qi, ki = qt_ref[pl.program_id(1)], kt_ref[pl.program_id(1)]
        rows = pl.ds(pl.multiple_of(qi * tq, tq), tq)

        @pl.when(qi == ki)
        def _():
            dk_s[...] = jnp.zeros_like(dk_s)
            dv_s[...] = jnp.zeros_like(dv_s)

        def step(masked):
            p, ds = _attn_probs(q_ref[...], k_ref[...], v_ref[...], do_ref[...], o_ref[...], lse_ref[...], masked, tq)
            dv_s[...] += _dot(p, do_ref[...], "tn")
            dk_s[...] += _dot(ds, q_ref[...], "tn")
            part = _dot(ds, k_ref[...], "nn")

            @pl.when(ki == 0)
            def _():
                dq_s[rows, :] = part

            @pl.when(ki > 0)
            def _():
                dq_s[rows, :] += part

        @pl.when(qi > ki)
        def _():
            step(False)

        @pl.when(qi == ki)
        def _():
            step(True)
            dq_s[rows, :] = dq_s[rows, :] * ATT_SCALE
            pltpu.sync_copy(dq_s.at[rows, :], dq_hbm.at[rows, pl.ds(pl.multiple_of(head * QK_W, QK_W), QK_W)])

        @pl.when(qi == nq - 1)
        def _():
            dk_ref[...] = dk_s[...] * ATT_SCALE
            dv_ref[...] = dv_s[...]

    q_idx = lambda h, s, qt, kt: (qt[s], h)
    k_idx = lambda h, s, qt, kt: (kt[s], h)
    return pl.pallas_call(
        body, name=name,
        grid_spec=pltpu.PrefetchScalarGridSpec(
            num_scalar_prefetch=2, grid=(MLA_HEADS, len(pairs)),
            in_specs=[pl.BlockSpec((tq, QK_W), q_idx), pl.BlockSpec((tq, QK_W), k_idx), pl.BlockSpec((tq, LANE), k_idx),
                      pl.BlockSpec((tq, LANE), q_idx), pl.BlockSpec((tq, LANE), q_idx), pl.BlockSpec((tq, LANE), q_idx)],
            out_specs=[pl.BlockSpec(memory_space=pl.ANY), pl.BlockSpec((tq, QK_W), k_idx), pl.BlockSpec((tq, LANE), k_idx)],
            scratch_shapes=[pltpu.VMEM((t_len, QK_W), F32), pltpu.VMEM((tq, QK_W), F32), pltpu.VMEM((tq, LANE), F32)]),
        out_shape=[jax.ShapeDtypeStruct((t_len, MLA_HEADS * QK_W), F32), jax.ShapeDtypeStruct((t_len, MLA_HEADS * QK_W), F32),
                   jax.ShapeDtypeStruct((t_len, MLA_HEADS * LANE), F32)],
        compiler_params=_cparams(("arbitrary", "arbitrary")),
    )(q_tab, k_tab, qc, kc, vv, o, lse, do)


def _adaln_fn(x, g, shift, scale):
    return ((_rms(x) * g) * (1.0 + scale) + shift,)


def _resid_fn(coef, y, x, gate):
    return (x + coef * gate * y,)


def _rms2_fn(cq, ckv, gq, gkv):
    return _rms(cq) * gq, _rms(ckv) * gkv


@jax.custom_vjp
def _swap_halves(x):
    return jnp.concatenate([x[:, 32:64], x[:, 0:32], x[:, 64:128]], axis=-1)


_swap_halves.defvjp(lambda x: (_swap_halves(x), None), lambda _, g: (_swap_halves(g),))


def _rope_fn(q, kv, misc, pos, invf, sgn):
    ang = pos * invf
    cos, sin = jnp.cos(ang), jnp.sin(ang) * sgn

    def rope(x):
        return x * cos + _swap_halves(x) * sin

    k_pe = rope(jnp.where(_iota2(misc.shape, 1) < MLA_ROPE, misc, 0.0))
    qs, ks = [], []
    for h in range(MLA_HEADS):
        qs += [q[:, h * LANE:(h + 1) * LANE], rope(q[:, (MLA_HEADS + h) * LANE:(MLA_HEADS + h + 1) * LANE])]
        ks += [kv[:, h * LANE:(h + 1) * LANE], k_pe]
    return jnp.concatenate(qs, axis=-1), jnp.concatenate(ks, axis=-1), kv[:, MLA_HEADS * LANE:]


def _ev_out_fn(oa, z, ob, g):
    parts = []
    for h in range(GDN_HEADS):
        hs = slice(h * LANE, (h + 1) * LANE)
        zz = z[:, hs]
        parts.append(_rms(oa[:, hs]) * g * (zz * jax.nn.sigmoid(zz)))
    return (jnp.concatenate(parts + [ob], axis=-1),)


def _od_out_fn(y, z, g):
    yz = y * (z * jax.nn.sigmoid(z))
    gw = SSD_D_INNER // SSD_GROUPS
    return (jnp.concatenate([_rms(yz[:, i * gw:(i + 1) * gw]) for i in range(SSD_GROUPS)], axis=-1) * g,)


def _loss_bwd(x, tgt, g, *, name):
    t_len, d = x.shape
    tm = min(ROW_TILE // 2, t_len)

    def body(x_ref, t_ref, g_ref, loss_ref, dx_ref, dg_ref):
        tgt_v = t_ref[...]

        def f(xv, gv):
            err = _rms(xv) * gv - tgt_v
            return 0.5 * jnp.sum(jnp.mean(err * err, axis=-1, keepdims=True), axis=0, keepdims=True)

        val, vjp = jax.vjp(f, x_ref[...], g_ref[...])
        dx, dg = vjp(jnp.ones((1, 1), F32))
        dx_ref[...] = dx

        @pl.when(pl.program_id(0) == 0)
        def _():
            loss_ref[...] = jnp.zeros_like(loss_ref)
            dg_ref[...] = jnp.zeros_like(dg_ref)

        loss_ref[...] += jnp.broadcast_to(val, loss_ref.shape)
        dg_ref[...] += dg

    row = pl.BlockSpec((tm, d), lambda i: (i, 0))
    return pl.pallas_call(
        body, grid=(t_len // tm,), name=name,
        in_specs=[row, row, pl.BlockSpec((1, d), lambda i: (0, 0))],
        out_specs=[pl.BlockSpec((1, LANE), lambda i: (0, 0)), row, pl.BlockSpec((1, d), lambda i: (0, 0))],
        out_shape=[jax.ShapeDtypeStruct((1, LANE), F32), jax.ShapeDtypeStruct((t_len, d), F32),
                   jax.ShapeDtypeStruct((1, d), F32)],
        compiler_params=_cparams(("arbitrary",)),
    )(x, tgt, g)


def _mesh_pos():
    return lax.axis_index("x"), lax.axis_index("y"), lax.axis_index("c")


def _exchange(xs, scatter, *, name):
    n_arr = len(xs)

    def body(*refs):
        in_refs, out_refs = refs[:n_arr], refs[n_arr:2 * n_arr]
        send_sems, recv_sems, local_sems = refs[2 * n_arr:]
        mx, my, mc = _mesh_pos()
        me = 4 * mx + 2 * my + mc
        started = []
        for a, (in_ref, out_ref) in enumerate(zip(in_refs, out_refs)):
            def src(j, in_ref=in_ref):
                return in_ref.at[j] if scatter else in_ref

            local = pltpu.make_async_copy(src(me), out_ref.at[me], local_sems.at[a])
            local.start()
            started.append((local, None))
            for d in range(1, N_DEV):
                px = 1 - mx if d & 4 else mx
                py = 1 - my if d & 2 else my
                pc = 1 - mc if d & 1 else mc
                peer = 4 * px + 2 * py + pc
                sem = a * (N_DEV - 1) + d - 1
                send = pltpu.make_async_remote_copy(
                    src_ref=src(peer), dst_ref=out_ref.at[me], send_sem=send_sems.at[sem], recv_sem=recv_sems.at[sem],
                    device_id=(px, py, pc), device_id_type=pl.DeviceIdType.MESH)
                send.start()
                recv = pltpu.make_async_remote_copy(
                    src_ref=src(peer), dst_ref=out_ref.at[peer], send_sem=send_sems.at[sem], recv_sem=recv_sems.at[sem],
                    device_id=(px, py, pc), device_id_type=pl.DeviceIdType.MESH)
                started.append((send, recv))
        for first, recv in started:
            if recv is None:
                first.wait()
            else:
                first.wait_send()
                recv.wait_recv()

    blocks = [tuple(x.shape[1:]) if scatter else tuple(x.shape) for x in xs]
    return pl.pallas_call(
        body, name=name,
        in_specs=[pl.BlockSpec(memory_space=pl.ANY)] * n_arr,
        out_specs=[pl.BlockSpec(memory_space=pl.ANY)] * n_arr,
        out_shape=[jax.ShapeDtypeStruct((N_DEV,) + b, x.dtype) for b, x in zip(blocks, xs)],
        scratch_shapes=[pltpu.SemaphoreType.DMA((n_arr * (N_DEV - 1),)), pltpu.SemaphoreType.DMA((n_arr * (N_DEV - 1),)),
                        pltpu.SemaphoreType.DMA((n_arr,))],
        compiler_params=pltpu.CompilerParams(has_side_effects=True),
    )(*xs)


def _peer_of(d, pos):
    mx, my, mc = pos
    px = 1 - mx if d & 4 else mx
    py = 1 - my if d & 2 else my
    pc = 1 - mc if d & 1 else mc
    return (px, py, pc), 4 * px + 2 * py + pc


_HBM = pl.BlockSpec(memory_space=pltpu.HBM)
_SEM = pl.BlockSpec(memory_space=pltpu.SEMAPHORE)


def _exchange_start(xs, scatter, *, name):
    n_arr = len(xs)
    n_sem = n_arr * (N_DEV - 1)

    def body(*refs):
        in_refs, land_refs = refs[:n_arr], refs[n_arr:2 * n_arr]
        send_sems, recv_sems, token = refs[2 * n_arr], refs[2 * n_arr + 1], refs[-1]
        pos = _mesh_pos()
        me = 4 * pos[0] + 2 * pos[1] + pos[2]
        for a in range(n_arr):
            for d in range(1, N_DEV):
                dev, peer = _peer_of(d, pos)
                sem = a * (N_DEV - 1) + d - 1
                pltpu.make_async_remote_copy(
                    src_ref=in_refs[a].at[peer] if scatter else in_refs[a], dst_ref=land_refs[a].at[me],
                    send_sem=send_sems.at[sem], recv_sem=recv_sems.at[sem], device_id=dev,
                    device_id_type=pl.DeviceIdType.MESH).start()
        token[...] = jnp.zeros_like(token)

    blocks = [tuple(x.shape[1:]) if scatter else tuple(x.shape) for x in xs]
    srcs = [pltpu.with_memory_space_constraint(x, pltpu.HBM) for x in xs]
    lands = [pltpu.with_memory_space_constraint(lax.empty((N_DEV,) + b, x.dtype), pltpu.HBM) for b, x in zip(blocks, xs)]
    res = pl.pallas_call(
        body, name=name,
        out_shape=(pltpu.SemaphoreType.DMA((n_sem,)), pltpu.SemaphoreType.DMA((n_sem,)),
                   *[pltpu.HBM(a.shape, a.dtype) for a in srcs + lands], jax.ShapeDtypeStruct((8, LANE), F32)),
        in_specs=[_HBM] * (2 * n_arr),
        out_specs=(_SEM, _SEM, *[_HBM] * (2 * n_arr), pl.BlockSpec(memory_space=pltpu.VMEM)),
        input_output_aliases={i: 2 + i for i in range(2 * n_arr)},
        compiler_params=pltpu.CompilerParams(has_side_effects=pltpu.SideEffectType.DATAFLOW_SIDE_EFFECTING),
    )(*srcs, *lands)
    handle = dict(sems=res[:2], srcs=res[2:2 + n_arr], lands=res[2 + n_arr:2 + 2 * n_arr], scatter=scatter)
    return handle, res[-1][0, 0]


def _exchange_wait(handle, after, me, *, name):
    scatter = handle["scatter"]
    n_arr = len(handle["srcs"])

    def body(*refs):
        in_refs, land_refs = refs[:n_arr], refs[n_arr:2 * n_arr]
        send_sems, recv_sems = refs[2 * n_arr], refs[2 * n_arr + 1]
        pos = _mesh_pos()
        for a in range(n_arr):
            for d in range(1, N_DEV):
                dev, peer = _peer_of(d, pos)
                sem = a * (N_DEV - 1) + d - 1
                copy = pltpu.make_async_remote_copy(
                    src_ref=in_refs[a].at[peer] if scatter else in_refs[a], dst_ref=land_refs[a].at[peer],
                    send_sem=send_sems.at[sem], recv_sem=recv_sems.at[sem], device_id=dev,
                    device_id_type=pl.DeviceIdType.MESH)
                copy.wait_send()
                copy.wait_recv()

    thru = list(handle["srcs"]) + list(handle["lands"])
    res = pl.pallas_call(
        body, name=name,
        out_shape=tuple(pltpu.HBM(a.shape, a.dtype) for a in thru),
        in_specs=[_HBM] * (2 * n_arr) + [_SEM, _SEM, pl.BlockSpec(memory_space=pl.ANY)],
        out_specs=tuple([_HBM] * (2 * n_arr)),
        input_output_aliases={i: i for i in range(2 * n_arr)},
        compiler_params=pltpu.CompilerParams(has_side_effects=pltpu.SideEffectType.DATAFLOW_SIDE_EFFECTING),
    )(*thru, *handle["sems"], after)
    out = []
    for src, land in zip(res[:n_arr], res[n_arr:]):
        own = lax.dynamic_index_in_dim(src, me, axis=0, keepdims=True) if scatter else src[None]
        out.append(lax.dynamic_update_index_in_dim(land, own, me, axis=0))
    return out


def _cols(srcs, rows, plans, out_dtype, *, name):
    n_src = len(srcs)
    rb = _pick(rows, (256, 128, 64, 32, 16, 8))

    def width(pieces):
        return sum(p[1] if p[0] == "z" else p[3] - p[2] for p in pieces)

    def body(*refs):
        ins, outs = refs[:n_src], refs[n_src:]
        loaded = {}
        for o_ref, plan in zip(outs, plans):
            for j, pieces in enumerate(plan):
                vals = []
                for pc in pieces:
                    if pc[0] == "z":
                        vals.append(jnp.zeros((rb, pc[1]), out_dtype))
                    else:
                        si, sj, c0, c1 = pc
                        if (si, sj) not in loaded:
                            loaded[(si, sj)] = ins[si][sj]
                        vals.append(loaded[(si, sj)][:, c0:c1].astype(out_dtype))
                o_ref[j] = vals[0] if len(vals) == 1 else jnp.concatenate(vals, axis=-1)

    for arr, r0 in srcs:
        assert r0 % rb == 0
    return pl.pallas_call(
        body, grid=(rows // rb,), name=name,
        in_specs=[pl.BlockSpec((arr.shape[0], rb, arr.shape[2]), lambda i, r0=r0 // rb: (0, r0 + i, 0)) for arr, r0 in srcs],
        out_specs=[pl.BlockSpec((len(p), rb, width(p[0])), lambda i: (0, i, 0)) for p in plans],
        out_shape=[jax.ShapeDtypeStruct((len(p), rows, width(p[0])), out_dtype) for p in plans],
        compiler_params=_cparams(("parallel",)),
    )(*[arr for arr, _ in srcs])


def _shard_pieces(src, a, b, shard_w):
    out = []
    while a < b:
        s = a // shard_w
        e = min(b, (s + 1) * shard_w)
        out.append((src, s, a - s * shard_w, e - s * shard_w))
        a = e
    return out


def _mapped_pieces(a, b, segs):
    out = []
    for n0, n1, k0 in sorted(segs):
        lo, hi = max(a, n0), min(b, n1)
        if lo < hi:
            out.append((0, 0, k0 + lo - n0, k0 + hi - n0))
    return out


_EV_SEGS = [(0, 1536, EV_QKV), (1536, 2048, EV_Z), (2048, 2056, EV_MISC + MLA_ROPE), (2056, 2440, EV_CQ),
            (2440, 2696, EV_CKV), (2696, 2760, EV_MISC)]
EV_NAT_W, OD_NAT_W = 2760, 5152


PACK_W = 1024


def _adamw(w, gparts, m, v, *, name):
    n_rows, n_cols = w.shape
    n_parts = gparts.shape[0]
    tm = _pick(n_rows, (512, 256, 128, 64, 32, 16, 8))
    while n_parts * tm * n_cols * 4 > 4 * 1024 * 1024 and tm % 16 == 0:
        tm //= 2

    def body(w_ref, g_ref, m_ref, v_ref, go_ref, d_ref, mo_ref, vo_ref):
        g = g_ref[0]
        for j in range(1, n_parts):
            g = g + g_ref[j]
        m_new = ADAM_B1 * m_ref[...] + (1.0 - ADAM_B1) * g
        v_new = ADAM_B2 * v_ref[...] + (1.0 - ADAM_B2) * jnp.square(g)
        m_hat = m_new / (1.0 - ADAM_B1 ** ADAM_STEP)
        v_hat = v_new / (1.0 - ADAM_B2 ** ADAM_STEP)
        go_ref[...] = g
        d_ref[...] = -ADAM_LR * (m_hat / (jnp.sqrt(v_hat) + ADAM_EPS) + ADAM_WD * w_ref[...])
        mo_ref[...] = m_new
        vo_ref[...] = v_new

    row = pl.BlockSpec((tm, n_cols), lambda i: (i, 0))
    return pl.pallas_call(
        body, grid=(n_rows // tm,), name=name,
        in_specs=[row, pl.BlockSpec((n_parts, tm, n_cols), lambda i: (0, i, 0)), row, row],
        out_specs=[row] * 4,
        out_shape=[jax.ShapeDtypeStruct((n_rows, n_cols), F32)] * 4,
        compiler_params=_cparams(("parallel",)),
    )(w, gparts, m, v)


def _adamw_nd(w, gparts, m, v, *, name):
    shape = w.shape
    two = (-1, shape[-1])
    outs = _adamw(w.reshape(two), gparts.reshape((gparts.shape[0],) + (int(np.prod(shape[:-1])), shape[-1])),
                  m.reshape(two), v.reshape(two), name=name)
    return [o.reshape(shape) for o in outs]


def _pack(parts):
    flat = [p.astype(F32).reshape(-1) for p in parts]
    n_pad = -sum(f.shape[0] for f in flat) % (8 * PACK_W)
    return jnp.concatenate(flat + [jnp.zeros((n_pad,), F32)]).reshape(-1, PACK_W)


def _unpack(packed, shapes):
    flat = packed.reshape(-1)
    out, off = [], 0
    for s in shapes:
        n = int(np.prod(s))
        out.append(flat[off:off + n].reshape(tuple(s)))
        off += n
    return out


def _mod_shard(c_all, ada_w, ada_b_shard, *, name):
    n_layer, d, n_col = ada_w.shape

    def body(c_ref, w_ref, b_ref, o_ref):
        cv = c_ref[...]
        o_ref[0] = _dot(cv * jax.nn.sigmoid(cv), w_ref[0], "nn") + b_ref[0]

    return pl.pallas_call(
        body, grid=(n_layer,), name=name,
        in_specs=[pl.BlockSpec((N_DEV, d), lambda l: (0, 0)), pl.BlockSpec((1, d, n_col), lambda l: (l, 0, 0)),
                  pl.BlockSpec((1, 1, n_col), lambda l: (l, 0, 0))],
        out_specs=pl.BlockSpec((1, N_DEV, n_col), lambda l: (l, 0, 0)),
        out_shape=jax.ShapeDtypeStruct((n_layer, N_DEV, n_col), F32),
        compiler_params=_cparams(("parallel",)),
    )(c_all, ada_w, ada_b_shard)


def _ada_w_grad(c_all, dmod_shard, *, name):
    n_layer, _, n_col = dmod_shard.shape
    d = c_all.shape[1]

    def body(c_ref, g_ref, o_ref):
        cv = c_ref[...]
        o_ref[0] = _dot(cv * jax.nn.sigmoid(cv), g_ref[0], "tn", True)

    return pl.pallas_call(
        body, grid=(n_layer,), name=name,
        in_specs=[pl.BlockSpec((N_DEV, d), lambda l: (0, 0)), pl.BlockSpec((1, N_DEV, n_col), lambda l: (l, 0, 0))],
        out_specs=pl.BlockSpec((1, d, n_col), lambda l: (l, 0, 0)),
        out_shape=jax.ShapeDtypeStruct((n_layer, d, n_col), F32),
        compiler_params=_cparams(("parallel",)),
    )(c_all, dmod_shard)


def _uq(w):
    r = w.shape[0]
    rope = jnp.pad(w[:, :, MLA_NOPE:], ((0, 0), (0, 0), (0, LANE - MLA_ROPE)))
    return jnp.concatenate([w[:, :, :MLA_NOPE].reshape(r, -1), rope.reshape(r, -1)], axis=1)


def _uq_back(d):
    r = d.shape[0]
    half = MLA_HEADS * LANE
    return jnp.concatenate([d[:, :half].reshape(r, MLA_HEADS, LANE),
                            d[:, half:].reshape(r, MLA_HEADS, LANE)[:, :, :MLA_ROPE]], axis=-1)


def _ukv(w):
    r = w.shape[0]
    return jnp.concatenate([w[:, :, :MLA_NOPE].reshape(r, -1), w[:, :, MLA_NOPE:].reshape(r, -1)], axis=1)


def _ukv_back(d):
    r = d.shape[0]
    half = MLA_HEADS * LANE
    return jnp.concatenate([d[:, :half].reshape(r, MLA_HEADS, LANE), d[:, half:].reshape(r, MLA_HEADS, LANE)], axis=-1)


def _lane_vec(v):
    return jnp.pad(v.astype(F32), (0, LANE - v.shape[0])).reshape(1, LANE)


def _row(v):
    return v.astype(F32).reshape(1, -1)


def _adaln(x, ln):
    return _rows(_adaln_fn, [x], list(ln), [(D_MODEL, BF16)], name="adaln", also_t=(0,))


def _adaln_bwd(x, ln, dh, dxn):
    (dx,), dln = _rows_vjp(_adaln_fn, [x], [], list(ln), [], [dh], [F32], adds={0: dxn}, name="adaln_bwd", tm=ROW_TILE)
    return dx, dln


def _resid(coef, y, x, gate):
    return _rows(functools.partial(_resid_fn, coef), [y, x], [gate], [(D_MODEL, F32)], name="resid")[0]


def _resid_adaln_fn(coef, y, x, gate, g, shift, scale):
    xn = x + coef * gate * y
    return (xn,) + _adaln_fn(xn, g, shift, scale)


def _norm_in(x, pending, ln):
    if pending is None:
        return (x,) + tuple(_adaln(x, ln))
    coef, y, gate = pending
    return _rows(functools.partial(_resid_adaln_fn, coef), [y, x], [gate] + list(ln), [(D_MODEL, F32), (D_MODEL, BF16)],
                 name="resid_adaln", also_t=(1,))


def _gated_fn(coef, y, gate):
    return (coef * gate * y,)


def _resid_bwd(coef, y, gate, dxn):
    (dy,), (dgate,) = _rows_vjp(functools.partial(_gated_fn, coef), [y], [], [gate], [], [dxn], [BF16], name="resid_bwd",
                                tm=ROW_TILE)
    return dy, dgate


def _ffn_fwd(x, h, ht, w13, w2):
    s, st, ab = _ffn_act(h, w13, name="ffn_act")
    y = _matmul(s, w2, "nn", F32, name="ffn_down")
    return y, (x, ab, ht, st, y)


def _ffn_bwd(saved, dxn, ln, gate, w13, w2):
    x, ab, ht, st, y = saved
    dy, dgate = _resid_bwd(0.5, y, gate, dxn)
    dab = _ffn_act_bwd(ab, dy, w2, name="ffn_act_bwd")
    dh = _matmul(dab, w13, "nt", F32, name="ffn_dh")
    dw13 = _matmul(ht, dab, "kn", F32, name="ffn_dw13")
    dw2 = _matmul(st, dy, "kn", F32, name="ffn_dw2")
    dx, dln = _adaln_bwd(x, ln, dh, dxn)
    return dx, dw13, dw2, dln, dgate


def _rope_consts():
    half = MLA_ROPE // 2
    inv = (ROPE_THETA ** (-jnp.arange(half, dtype=F32) / half)).astype(F32)
    zeros = jnp.zeros((LANE - MLA_ROPE,), F32)
    invf = jnp.concatenate([inv, inv, zeros]).reshape(1, LANE)
    sgn = jnp.concatenate([-jnp.ones((half,), F32), jnp.ones((half,), F32), zeros]).reshape(1, LANE)
    return invf, sgn


def _even_fwd(x, h, ht, pos, wt):
    p = _matmul(h, wt["w_in"], "nn", F32, name="ev_in")
    act, pre = _conv_fwd(p, EV_QKV, 1536, wt["conv_w"], jnp.zeros((1, 1536), F32), name="ev_conv")
    o_a, states = _gdn_fwd(act, p, wt["alog"], wt["dtb"], name="gdn_fwd")
    cqn, ckvn = _rows(_rms2_fn, [(p, EV_CQ, 384), (p, EV_CKV, 256)], [wt["gq"], wt["gkv"]],
                      [(384, BF16), (256, BF16)], name="mla_rms")
    q = _matmul(cqn, wt["w_uq"], "nn", F32, name="mla_uq")
    kv = _matmul(ckvn, wt["w_ukv"], "nn", F32, name="mla_ukv")
    invf, sgn = _rope_consts()
    qc, kc, vv = _rows(_rope_fn, [q, kv, (p, EV_MISC, LANE), pos], [invf, sgn],
                       [(1024, BF16), (1024, BF16), (512, BF16)], name="mla_rope", tm=ROW_TILE // 2)
    o_b, lse = _attn_fwd(qc, kc, vv, name="attn_fwd")
    o, ot = _rows(_ev_out_fn, [o_a, (p, EV_Z, 512), o_b], [wt["gdn_g"]], [(1024, BF16)], name="ev_out", also_t=(0,))
    y = _matmul(o, wt["w_out"], "nn", F32, name="ev_wout")
    return y, (x, ht, p, act, pre, states, cqn, ckvn, q, kv, qc, kc, vv, o_a, o_b, lse, ot, y)


def _ev_dp_fn(dx0, dx1, dx2, dcq, dm_r, dm_g, dz, dckv):
    return (jnp.concatenate([dx0, dx1, dx2, dcq, dm_r + dm_g, dz, dckv], axis=-1),)


def _even_bwd(saved, dxn, pos, ln, gate, wt):
    x, ht, p, act, pre, states, cqn, ckvn, q, kv, qc, kc, vv, o_a, o_b, lse, ot, y = saved
    g = {}
    dy, g["gate"] = _resid_bwd(1.0, y, gate, dxn)
    do = _matmul(dy, wt["w_out"], "nt", F32, name="ev_dwout_x")
    g["w_out"] = _matmul(ot, dy, "kn", F32, name="ev_dwout_w")
    (d_oa, dz, d_ob), (g["gdn_g"],) = _rows_vjp(_ev_out_fn, [o_a, (p, EV_Z, 512), o_b], [], [wt["gdn_g"]], [], [do],
                                                [F32, F32, F32], name="ev_out_bwd")
    dqc, dkc, dvv = _attn_bwd(qc, kc, vv, o_b, lse, d_ob, name="attn_bwd")
    invf, sgn = _rope_consts()
    (dq, dkv, dm_r), _ = _rows_vjp(_rope_fn, [q, kv, (p, EV_MISC, LANE)], [pos], [], [invf, sgn], [dqc, dkc, dvv],
                                   [BF16, BF16, F32], name="mla_rope_bwd", tm=ROW_TILE // 2)
    dcqn = _matmul(dq, wt["w_uq"], "nt", F32, name="mla_duq_x")
    g["w_uq"] = _matmul(cqn, dq, "tn", F32, name="mla_duq_w")
    dckvn = _matmul(dkv, wt["w_ukv"], "nt", F32, name="mla_dukv_x")
    g["w_ukv"] = _matmul(ckvn, dkv, "tn", F32, name="mla_dukv_w")
    (dcq, dckv), (g["gq"], g["gkv"]) = _rows_vjp(_rms2_fn, [(p, EV_CQ, 384), (p, EV_CKV, 256)], [],
                                                 [wt["gq"], wt["gkv"]], [], [dcqn, dckvn], [F32, F32], name="mla_rms_bwd")
    dq_g, dk_g, dv_g, dm_g, g["alog"], g["dtb"] = _gdn_bwd(act, p, wt["alog"], wt["dtb"], states, d_oa, name="gdn_bwd")
    dxs, dws = [], []
    for j, d in enumerate((dq_g, dk_g, dv_g)):
        dxj, dwj, _ = _conv_bwd(d, pre, 512 * j, p, EV_QKV + 512 * j, wt["conv_w"], name="ev_conv_bwd")
        dxs.append(dxj)
        dws.append(dwj)
    g["conv_w"] = jnp.concatenate(dws, axis=1)
    (dp,) = _rows(_ev_dp_fn, dxs + [dcq, dm_r, dm_g, dz, dckv], [],
                  [(EV_W, BF16)], name="ev_dp", tm=ROW_TILE // 2)
    dh = _matmul(dp, wt["w_in"], "nt", F32, name="ev_din_x")
    g["w_in"] = _matmul(ht, dp, "kn", F32, name="ev_din_w")
    dx, g["ln"] = _adaln_bwd(x, ln, dh, dxn)
    return dx, g


def _odd_fwd(x, h, ht, wt):
    p = _matmul(h, wt["w_in"], "nn", F32, name="od_in")
    act, pre = _conv_fwd(p, OD_XBC, 3072, wt["conv_w"], wt["conv_b"], name="od_conv")
    ys, states = _ssd_fwd(act, p, wt["alog"], wt["dtb"], wt["dsk"], name="ssd_fwd")
    o, ot = _rows(_od_out_fn, [ys, (p, OD_Z, 2048)], [wt["norm_g"]], [(SSD_D_INNER, BF16)], name="od_out",
                  tm=ROW_TILE // 2, also_t=(0,))
    y = _matmul(o, wt["w_out"], "nn", F32, name="od_wout")
    return y, (x, ht, p, act, pre, states, ys, ot, y)


def _od_dp_fn(dz, dxx, dxb, dxc, ddt):
    return (jnp.concatenate([dz, dxx, dxb, dxc, ddt, jnp.zeros_like(ddt)], axis=-1),)


def _odd_bwd(saved, dxn, ln, gate, wt):
    x, ht, p, act, pre, states, ys, ot, y = saved
    g = {}
    dy, g["gate"] = _resid_bwd(1.0, y, gate, dxn)
    do = _matmul(dy, wt["w_out"], "nt", F32, name="od_dwout_x")
    g["w_out"] = _matmul(ot, dy, "kn", F32, name="od_dwout_w")
    (dys, dz), (g["norm_g"],) = _rows_vjp(_od_out_fn, [ys, (p, OD_Z, 2048)], [], [wt["norm_g"]], [], [do], [F32, F32],
                                          name="od_out_bwd", tm=ROW_TILE // 2)
    dxs, dbm, dcm, ddt, g["alog"], g["dtb"], g["dsk"] = _ssd_bwd(act, p, wt["alog"], wt["dtb"], wt["dsk"], states, dys,
                                                                 name="ssd_bwd")
    dins, dws, dbs = [], [], []
    for d, c0 in ((dxs, 0), (dbm, 2048), (dcm, 2560)):
        dxj, dwj, dbj = _conv_bwd(d, pre, c0, p, OD_XBC + c0, wt["conv_w"], name="od_conv_bwd")
        dins.append(dxj)
        dws.append(dwj)
        dbs.append(dbj)
    g["conv_w"] = jnp.concatenate(dws, axis=1)
    g["conv_b"] = jnp.concatenate(dbs, axis=1)
    (dp,) = _rows(_od_dp_fn, [dz] + dins + [ddt], [], [(OD_W, BF16)], name="od_dp", tm=ROW_TILE // 2)
    dh = _matmul(dp, wt["w_in"], "nt", F32, name="od_din_x")
    g["w_in"] = _matmul(ht, dp, "kn", F32, name="od_din_w")
    dx, g["ln"] = _adaln_bwd(x, ln, dh, dxn)
    return dx, g


def _local_step(x, tgt, pos, mod, final_g, layer_weights, layer_done):
    mod = mod.reshape(DEPTH, 3, 3, 1, D_MODEL)
    wts = []

    def ln_of(l, i):
        return (_row(wts[l]["norm_g"][i]), mod[l, i, 0], mod[l, i, 1])

    def mixer_w(l):
        p = wts[l]
        if l % 2 == 0:
            return dict(w_in=p["w_in_k"], conv_w=p["gdn_conv_w"].astype(F32),
                        alog=_lane_vec(p["gdn_A_log"]), dtb=_lane_vec(p["gdn_dt_bias"]),
                        gdn_g=_row(p["gdn_norm_g"]), gq=_row(p["mla_q_norm_g"]), gkv=_row(p["mla_kv_norm_g"]),
                        w_uq=_uq(p["mla_w_uq"]), w_ukv=_ukv(p["mla_w_ukv"]), w_out=p["ev_w_out"])
        return dict(w_in=p["w_in_k"], conv_w=p["ssd_conv_w"].astype(F32),
                    conv_b=_row(p["ssd_conv_b"]), alog=_lane_vec(p["ssd_A_log"]),
                    dtb=_lane_vec(p["ssd_dt_bias"]), dsk=_lane_vec(p["ssd_D"]),
                    norm_g=_row(p["ssd_norm_g"]), w_out=p["ssd_w_out"])

    saved = []
    pending = None
    for l in range(DEPTH):
        wts.append(layer_weights(l, x if pending is None else pending[1]))
        x, h, ht = _norm_in(x, pending, ln_of(l, 0))
        y, s0 = _ffn_fwd(x, h, ht, wts[l]["w13"][0], wts[l]["w2"][0])
        x, h, ht = _norm_in(x, (0.5, y, mod[l, 0, 2]), ln_of(l, 1))
        if l % 2 == 0:
            y, s1 = _even_fwd(x, h, ht, pos, mixer_w(l))
        else:
            y, s1 = _odd_fwd(x, h, ht, mixer_w(l))
        x, h, ht = _norm_in(x, (1.0, y, mod[l, 1, 2]), ln_of(l, 2))
        y, s2 = _ffn_fwd(x, h, ht, wts[l]["w13"][1], wts[l]["w2"][1])
        pending = (0.5, y, mod[l, 2, 2])
        saved.append((s0, s1, s2))
    x = _resid(*pending[:2], x, pending[2])

    loss, dx, d_final_g = _loss_bwd(x, tgt, _row(final_g), name="loss")

    repl = {k: [None] * (DEPTH // 2) for k in ("gdn_A_log", "gdn_dt_bias", "gdn_norm_g", "mla_q_norm_g", "mla_kv_norm_g",
                                                "ssd_A_log", "ssd_dt_bias", "ssd_D")}
    dmod = [None] * DEPTH
    token = None
    for l in reversed(range(DEPTH)):
        s0, s1, s2 = saved[l]
        e = l // 2
        gl = {"w13": [None] * 2, "w2": [None] * 2}
        dg, dsh, dsc, dgt = [None] * 3, [None] * 3, [None] * 3, [None] * 3
        gate2 = mod[l, 2, 2] if token is None else mod[l, 2, 2] + token
        dx, gl["w13"][1], gl["w2"][1], (dg[2], dsh[2], dsc[2]), dgt[2] = _ffn_bwd(
            s2, dx, ln_of(l, 2), gate2, wts[l]["w13"][1], wts[l]["w2"][1])
        if l % 2 == 0:
            dx, g = _even_bwd(s1, dx, pos, ln_of(l, 1), mod[l, 1, 2], mixer_w(l))
            gl.update(w_in_k=g["w_in"], gdn_conv_w=g["conv_w"], mla_w_uq=_uq_back(g["w_uq"]),
                      mla_w_ukv=_ukv_back(g["w_ukv"]), ev_w_out=g["w_out"])
            repl["gdn_A_log"][e] = g["alog"][0, :GDN_HEADS]
            repl["gdn_dt_bias"][e] = g["dtb"][0, :GDN_HEADS]
            repl["gdn_norm_g"][e] = g["gdn_g"][0]
            repl["mla_q_norm_g"][e] = g["gq"][0]
            repl["mla_kv_norm_g"][e] = g["gkv"][0]
        else:
            dx, g = _odd_bwd(s1, dx, ln_of(l, 1), mod[l, 1, 2], mixer_w(l))
            gl.update(w_in_k=g["w_in"], ssd_conv_w=g["conv_w"], ssd_conv_b=g["conv_b"][0], ssd_norm_g=g["norm_g"][0],
                      ssd_w_out=g["w_out"])
            repl["ssd_A_log"][e] = g["alog"][0, :SSD_HEADS]
            repl["ssd_dt_bias"][e] = g["dtb"][0, :SSD_HEADS]
            repl["ssd_D"][e] = g["dsk"][0, :SSD_HEADS]
        dg[1], dsh[1], dsc[1] = g["ln"]
        dgt[1] = g["gate"]
        gl.update(w13=gl["w13"][1], w2=gl["w2"][1])
        token = layer_done(l, 0, gl, dx)
        gate0 = mod[l, 0, 2] if token is None else mod[l, 0, 2] + token
        dx, dw13, dw2, (dg[0], dsh[0], dsc[0]), dgt[0] = _ffn_bwd(s0, dx, ln_of(l, 0), gate0, wts[l]["w13"][0], wts[l]["w2"][0])
        dmod[l] = jnp.concatenate([jnp.concatenate([dsh[i], dsc[i], dgt[i]], axis=1) for i in range(3)], axis=1)[0]
        token = layer_done(l, 1, dict(w13=dw13, w2=dw2, norm_g=jnp.concatenate(dg, axis=0)), dx)

    grads = {k: jnp.stack(v) for k, v in repl.items()}
    grads["final_g"] = d_final_g[0]
    return loss, dx, grads, jnp.stack(dmod)


_WEIGHTS = ("ada_w", "ada_b", "norm_g", "ffn_w1", "ffn_w3", "ffn_w2", "ev_w_in", "gdn_conv_w", "gdn_A_log", "gdn_dt_bias",
            "gdn_norm_g", "mla_q_norm_g", "mla_w_uq", "mla_kv_norm_g", "mla_w_ukv", "ev_w_out", "ssd_w_in", "ssd_conv_w",
            "ssd_conv_b", "ssd_A_log", "ssd_dt_bias", "ssd_D", "ssd_norm_g", "ssd_w_out", "final_g")
_BIG = {"ffn_w1": 3, "ffn_w3": 3, "ffn_w2": 2, "ev_w_in": 2, "mla_w_uq": 1, "mla_w_ukv": 1, "ev_w_out": 1, "ssd_w_in": 2,
        "ssd_w_out": 1}
_SMALL = {"norm_g": 2, "gdn_conv_w": 2, "ssd_conv_w": 2, "ssd_conv_b": 1, "ssd_norm_g": 1}
_REPL = ("ada_b", "gdn_A_log", "gdn_dt_bias", "gdn_norm_g", "mla_q_norm_g", "mla_kv_norm_g", "ssd_A_log", "ssd_dt_bias",
         "ssd_D", "final_g")


def _join(pieces, axis):
    moved = jnp.moveaxis(pieces, 0, axis)
    shape = moved.shape
    return moved.reshape(shape[:axis] + (shape[axis] * shape[axis + 1],) + shape[axis + 2:])


def _split(full, axis):
    shape = full.shape
    return jnp.moveaxis(full.reshape(shape[:axis] + (N_DEV, shape[axis] // N_DEV) + shape[axis + 1:]), axis, 0)


def kernel(x, c, positions, ada_w, ada_b, norm_g, ffn_w1, ffn_w3, ffn_w2, ev_w_in, gdn_conv_w, gdn_A_log, gdn_dt_bias, gdn_norm_g, mla_q_norm_g, mla_w_uq, mla_kv_norm_g, mla_w_ukv, ev_w_out, ssd_w_in, ssd_conv_w, ssd_conv_b, ssd_A_log, ssd_dt_bias, ssd_D, ssd_norm_g, ssd_w_out, final_g, loss_target, m_ada_w, m_ada_b, m_norm_g, m_ffn_w1, m_ffn_w3, m_ffn_w2, m_ev_w_in, m_gdn_conv_w, m_gdn_A_log, m_gdn_dt_bias, m_gdn_norm_g, m_mla_q_norm_g, m_mla_w_uq, m_mla_kv_norm_g, m_mla_w_ukv, m_ev_w_out, m_ssd_w_in, m_ssd_conv_w, m_ssd_conv_b, m_ssd_A_log, m_ssd_dt_bias, m_ssd_D, m_ssd_norm_g, m_ssd_w_out, m_final_g, v_ada_w, v_ada_b, v_norm_g, v_ffn_w1, v_ffn_w3, v_ffn_w2, v_ev_w_in, v_gdn_conv_w, v_gdn_A_log, v_gdn_dt_bias, v_gdn_norm_g, v_mla_q_norm_g, v_mla_w_uq, v_mla_kv_norm_g, v_mla_w_ukv, v_ev_w_out, v_ssd_w_in, v_ssd_conv_w, v_ssd_conv_b, v_ssd_A_log, v_ssd_dt_bias, v_ssd_D, v_ssd_norm_g, v_ssd_w_out, v_final_g):
    a = dict(locals())
    w = {n: a[n] for n in _WEIGHTS}
    m = {n: a["m_" + n] for n in _WEIGHTS}
    v = {n: a["v_" + n] for n in _WEIGHTS}
    mx, my, mc = _mesh_pos()
    me = 4 * mx + 2 * my + mc
    t_len = x.shape[1]
    shards = range(N_DEV)

    small_names, big_names = list(_SMALL), list(_BIG)
    axis_of = {**_SMALL, **_BIG}
    small_g = _exchange([c] + [w[n] for n in small_names], False, name="gather_small")
    c_all = small_g[0].reshape(N_DEV, D_MODEL)
    fw = {n: _join(p, _SMALL[n]) for n, p in zip(small_names, small_g[1:])}
    first_names = [n for n in big_names if not n.startswith("ssd")]
    first_g = dict(zip(first_names, _exchange([w[n][:1].astype(BF16) for n in first_names], False, name="gather_first")))
    rest_src = {n: (w[n] if n.startswith("ssd") else w[n][1:]).astype(BF16) for n in big_names}
    rest_handle, rest_token = _exchange_start([rest_src[n] for n in big_names], False, name="gather_rest_start")
    rest_g = {}

    fs, es, os_ = ffn_w1.shape[3], ev_w_in.shape[2], ssd_w_in.shape[2]
    half = range(N_DEV // 2)
    plan13 = [[[(0, s, 0, fs) for s in half] + [(1, s, 0, fs) for s in half]
               + [(0, s + 4, 0, fs) for s in half] + [(1, s + 4, 0, fs) for s in half]]]
    plan_ev, k_at = [], 0
    for n0, n1, k0 in sorted(_EV_SEGS, key=lambda seg: seg[2]):
        if k0 > k_at:
            plan_ev.append(("z", k0 - k_at))
        plan_ev += _shard_pieces(0, n0, n1, es)
        k_at = k0 + n1 - n0
    assert k_at == EV_W and es * N_DEV == EV_NAT_W and os_ * N_DEV == OD_NAT_W
    plan_od = [[_shard_pieces(0, 0, OD_NAT_W, os_) + [("z", OD_W - OD_NAT_W)]]]

    def layer_weights(l, x_in):
        if l == 1:
            rest_g.update(zip(big_names, _exchange_wait(rest_handle, x_in, me, name="gather_rest_wait")))
        e = l // 2
        src = first_g if l == 0 else rest_g
        i = 0 if l == 0 else l - 1
        ie = 0 if (l == 0 or l % 2) else e - 1
        g1 = src["ffn_w1"].reshape(N_DEV, -1, fs)
        g3 = src["ffn_w3"].reshape(N_DEV, -1, fs)
        p = {"norm_g": fw["norm_g"][l],
             "w13": [_cols([(g1, (2 * i + j) * D_MODEL), (g3, (2 * i + j) * D_MODEL)], D_MODEL, plan13, BF16,
                           name="join_w13")[0][0] for j in range(2)],
             "w2": [src["ffn_w2"][:, i, j].reshape(D_FF, D_MODEL) for j in range(2)]}
        if l % 2 == 0:
            p["w_in_k"] = _cols([(src["ev_w_in"].reshape(N_DEV, -1, es), ie * D_MODEL)], D_MODEL, [[plan_ev]], BF16,
                                name="join_ev_in")[0][0]
            for n in ("mla_w_uq", "mla_w_ukv", "ev_w_out"):
                p[n] = _join(src[n][:, ie], axis_of[n] - 1)
            p["gdn_conv_w"] = fw["gdn_conv_w"][e]
            for n in ("gdn_A_log", "gdn_dt_bias", "gdn_norm_g", "mla_q_norm_g", "mla_kv_norm_g"):
                p[n] = w[n][e]
        else:
            p["w_in_k"] = _cols([(rest_g["ssd_w_in"].reshape(N_DEV, -1, os_), e * D_MODEL)], D_MODEL, plan_od, BF16,
                                name="join_od_in")[0][0]
            p["ssd_w_out"] = _join(rest_g["ssd_w_out"][:, e], axis_of["ssd_w_out"] - 1)
            for n in ("ssd_conv_w", "ssd_conv_b", "ssd_norm_g"):
                p[n] = fw[n][e]
            for n in ("ssd_A_log", "ssd_dt_bias", "ssd_D"):
                p[n] = w[n][e]
        return p

    def w13_cols(s, third):
        k0 = (s % 4) * fs + (2 * FF_HALF if s >= 4 else 0) + (FF_HALF if third else 0)
        return [(0, 0, k0, k0 + fs)]

    sent = {}

    def layer_done(l, stage, gl, dx_l):
        d1, d3 = _cols([(gl["w13"][None], 0)], D_MODEL, [[w13_cols(s, False) for s in shards],
                                                         [w13_cols(s, True) for s in shards]], F32, name="split_w13")
        pieces = {"ffn_w1": d1[:, None, None], "ffn_w3": d3[:, None, None],
                  "ffn_w2": gl["w2"].reshape(N_DEV, -1, D_MODEL)[:, None, None]}
        if stage == 1:
            pieces["norm_g"] = _split(gl["norm_g"][None], 2)
        elif l % 2 == 0:
            pieces["ev_w_in"] = _cols([(gl["w_in_k"][None], 0)], D_MODEL,
                                      [[_mapped_pieces(s * es, (s + 1) * es, _EV_SEGS) for s in shards]], F32,
                                      name="split_ev_in")[0][:, None]
            for n in ("gdn_conv_w", "mla_w_uq", "mla_w_ukv", "ev_w_out"):
                pieces[n] = _split(gl[n][None], axis_of[n])
        else:
            pieces["ssd_w_in"] = _cols([(gl["w_in_k"][None], 0)], D_MODEL,
                                       [[[(0, 0, s * os_, (s + 1) * os_)] for s in shards]], F32,
                                       name="split_od_in")[0][:, None]
            for n in ("ssd_conv_w", "ssd_conv_b", "ssd_norm_g", "ssd_w_out"):
                pieces[n] = _split(gl[n][None], axis_of[n])
        names = list(pieces)
        if l == 0 and stage == 1:
            sent[l, stage] = dict(zip(names, _exchange([pieces[n] for n in names], True, name="scatter_last")))
            return None
        handle, token = _exchange_start([pieces[n] for n in names], True, name=f"scatter_start_{l}_{stage}")
        sent[l, stage] = (names, handle)
        return token

    n_col = ada_w.shape[2]
    ada_b_shard = lax.dynamic_slice(ada_b, (0, me * n_col), (DEPTH, n_col)).reshape(DEPTH, 1, n_col)
    mod_all = _exchange([_mod_shard(c_all, ada_w, ada_b_shard, name="mod")], False, name="gather_mod")[0]
    mod_me = lax.dynamic_index_in_dim(mod_all, me, axis=2, keepdims=False)
    mod = jnp.transpose(mod_me, (1, 0, 2)).reshape(DEPTH, N_DEV * n_col) + rest_token

    pos = positions.astype(F32).reshape(t_len, 1)
    loss, dx, grads, dmod = _local_step(x[0], loss_target[0], pos, mod, final_g, layer_weights, layer_done)
    for (l, stage), started in list(sent.items()):
        if not isinstance(started, dict):
            names, handle = started
            sent[l, stage] = dict(zip(names, _exchange_wait(handle, dx, me, name=f"scatter_wait_{l}_{stage}")))

    repl_shapes = [w[n].shape for n in _REPL] + [(1,)]
    parts8 = _exchange([_pack([dmod] + [grads[n] for n in _REPL[1:]] + [loss[0, :1]])], False, name="gather_repl")[0]
    zero = jnp.zeros((1,), F32)
    r_grad, r_delta, r_m, r_v = [
        _unpack(o, repl_shapes) for o in _adamw(_pack([w[n] for n in _REPL] + [zero]), parts8,
                                                _pack([m[n] for n in _REPL] + [zero]),
                                                _pack([v[n] for n in _REPL] + [zero]), name="adamw_repl")]
    out = {"grad": {}, "delta": {}, "m": {}, "v": {}}
    for i, n in enumerate(_REPL):
        out["grad"][n], out["delta"][n], out["m"][n], out["v"][n] = r_grad[i], r_delta[i], r_m[i], r_v[i]
    loss_total = r_grad[-1].reshape(())

    dmod_all = parts8[:, :dmod.size // PACK_W].reshape((N_DEV,) + dmod.shape)
    dmod_cols = jnp.transpose(lax.dynamic_slice_in_dim(dmod_all, me * n_col, n_col, axis=2), (1, 0, 2))
    g_ada = _ada_w_grad(c_all, dmod_cols, name="ada_w_grad")
    for k, o in zip(("grad", "delta", "m", "v"), _adamw_nd(ada_w, g_ada[None], m["ada_w"], v["ada_w"], name="adamw_ada")):
        out[k]["ada_w"] = o

    for n in small_names + big_names:
        if n == "norm_g":
            g8 = jnp.concatenate([sent[l, 1][n] for l in range(DEPTH)], axis=1)
        elif n.startswith("ffn"):
            g8 = jnp.concatenate([jnp.concatenate([sent[l, 1][n], sent[l, 0][n]], axis=2) for l in range(DEPTH)], axis=1)
        else:
            layers = range(1, DEPTH, 2) if n.startswith("ssd") else range(0, DEPTH, 2)
            g8 = jnp.concatenate([sent[l, 0][n] for l in layers], axis=1)
        for k, o in zip(("grad", "delta", "m", "v"), _adamw_nd(w[n], g8, m[n], v[n], name="adamw_" + n)):
            out[k][n] = o

    return (loss_total, dx.reshape(x.shape), *[out["grad"][n] for n in _WEIGHTS], *[out["delta"][n] for n in _WEIGHTS],
            *[out["m"][n] for n in _WEIGHTS], *[out["v"][n] for n in _WEIGHTS])
```

```python
import functools
import math

import numpy as np
import jax
import jax.numpy as jnp
from jax import lax
from jax.experimental import pallas as pl
from jax.experimental.pallas import tpu as pltpu

F32 = jnp.float32
BF16 = jnp.bfloat16
HI = lax.Precision.HIGHEST

D_MODEL = 1024
DEPTH = 4
CHUNK = 64
NORM_EPS = 1e-6
CONV_K = 4
D_FF = 2816
GDN_HEADS = 4
GDN_DK = 128
MLA_HEADS = 4
MLA_NOPE = 128
MLA_ROPE = 64
ROPE_THETA = 10000.0
SSD_HEADS = 32
SSD_HEADDIM = 64
SSD_GROUPS = 4
SSD_STATE = 128
SSD_D_INNER = 2048
N_DEV = 8

ADAM_LR = 0.001
ADAM_B1 = 0.9
ADAM_B2 = 0.999
ADAM_EPS = 1e-08
ADAM_WD = 0.01
ADAM_STEP = 10

V7X_VMEM_LIMIT = 56 * 1024 * 1024
ROW_TILE = 512
SEQ_TILE = 256
ATT_TILE = 1024
MM_RESIDENT_BYTES = 12 * 1024 * 1024
FF_HALF = D_FF // 2
LANE = 128

EV_QKV, EV_CQ, EV_MISC, EV_Z, EV_CKV, EV_W = 0, 1536, 1920, 2048, 2560, 2816
OD_Z, OD_XBC, OD_DT, OD_W = 0, 2048, 5120, 5376


def _cparams(sem=None):
    return pltpu.CompilerParams(dimension_semantics=sem, vmem_limit_bytes=V7X_VMEM_LIMIT)


def _pick(n, cands):
    for c in cands:
        if n % c == 0:
            return c
    return n


_DN = {"nn": (((1,), (0,)), ((), ())), "nt": (((1,), (1,)), ((), ())), "tn": (((0,), (0,)), ((), ()))}


F32_3PASS = 2


def _dot(a, b, mode, hi=False):
    if hi:
        prec = lax.Precision.HIGH if hi == F32_3PASS else HI
        return lax.dot_general(a.astype(F32), b.astype(F32), _DN[mode], precision=prec, preferred_element_type=F32)
    return lax.dot_general(a.astype(BF16), b.astype(BF16), _DN[mode], preferred_element_type=F32)


@functools.partial(jax.custom_vjp, nondiff_argnums=(2, 3))
def _mm(a, b, mode, hi):
    return _dot(a, b, mode, hi)


def _mm_fwd(a, b, mode, hi):
    return _dot(a, b, mode, hi), (a, b)


def _mm_bwd(mode, hi, res, g):
    a, b = res
    if mode == "nn":
        return _dot(g, b, "nt", hi), _dot(a, g, "tn", hi)
    if mode == "nt":
        return _dot(g, b, "nn", hi), _dot(g, a, "tn", hi)
    return _dot(b, g, "nt", hi), _dot(a, g, "nn", hi)


_mm.defvjp(_mm_fwd, _mm_bwd)


def _iota2(shape, dim):
    return lax.broadcasted_iota(jnp.int32, shape, dim)


def _row_spec(a, tm):
    if isinstance(a, tuple):
        arr, c0, w = a
        assert c0 % w == 0
        cb = c0 // w
        return arr, pl.BlockSpec((tm, w), lambda i, cb=cb: (i, cb))
    return a, pl.BlockSpec((tm, a.shape[1]), lambda i: (i, 0))


def _full_spec(b):
    return pl.BlockSpec(b.shape, lambda i: (0,) * b.ndim)


def _rows(fn, tiled, bcast, outs, *, name, tm=ROW_TILE, also_t=()):
    arrs, specs = zip(*[_row_spec(a, 0) for a in tiled])
    t_len = arrs[0].shape[0]
    tm = min(tm, t_len)
    arrs, specs = zip(*[_row_spec(a, tm) for a in tiled])
    nt, nb, no = len(tiled), len(bcast), len(outs)

    def body(*refs):
        ins = [r[...].astype(F32) for r in refs[:nt]] + [r[...] for r in refs[nt:nt + nb]]
        res = fn(*ins)
        for r, v in zip(refs[nt + nb:nt + nb + no], res):
            r[...] = v.astype(r.dtype)
        for r, k in zip(refs[nt + nb + no:], also_t):
            r[...] = res[k].T.astype(r.dtype)

    return pl.pallas_call(
        body, grid=(t_len // tm,), name=name,
        in_specs=list(specs) + [_full_spec(b) for b in bcast],
        out_specs=[pl.BlockSpec((tm, c), lambda i: (i, 0)) for c, _ in outs]
        + [pl.BlockSpec((outs[k][0], tm), lambda i: (0, i)) for k in also_t],
        out_shape=[jax.ShapeDtypeStruct((t_len, c), dt) for c, dt in outs]
        + [jax.ShapeDtypeStruct((outs[k][0], t_len), outs[k][1]) for k in also_t],
        compiler_params=_cparams(("parallel",)),
    )(*arrs, *bcast)


def _rows_vjp(fn, tiled, consts, bcast, bconsts, douts, grads, *, name, adds=None, tm=ROW_TILE // 2):
    adds = adds or {}
    t_arrs, t_specs = zip(*[_row_spec(a, 0) for a in tiled])
    t_len = t_arrs[0].shape[0]
    tm = min(tm, t_len)
    rows_in = list(tiled) + list(consts) + list(douts) + [adds[k] for k in sorted(adds)]
    arrs, specs = zip(*[_row_spec(a, tm) for a in rows_in])
    nt, nc, nb, nbc, nd, na = len(tiled), len(consts), len(bcast), len(bconsts), len(douts), len(adds)
    add_pos = {k: j for j, k in enumerate(sorted(adds))}
    want = [j for j, g in enumerate(grads) if g is not None]

    def body(*refs):
        p = 0
        t = [r[...].astype(F32) for r in refs[p:p + nt]]; p += nt
        c = [r[...].astype(F32) for r in refs[p:p + nc]]; p += nc
        d = [r[...].astype(F32) for r in refs[p:p + nd]]; p += nd
        a = [r[...].astype(F32) for r in refs[p:p + na]]; p += na
        b = [r[...] for r in refs[p:p + nb]]; p += nb
        bc = [r[...] for r in refs[p:p + nbc]]; p += nbc
        g_refs = refs[p:p + len(want)]; p += len(want)
        gb_refs = refs[p:p + nb]

        def f(*args):
            return fn(*args[:nt], *c, *args[nt:], *bc)

        _, vjp = jax.vjp(f, *t, *b)
        g = vjp(tuple(d))
        for r, j in zip(g_refs, want):
            val = g[j]
            if j in add_pos:
                val = val + a[add_pos[j]]
            r[...] = val.astype(r.dtype)

        @pl.when(pl.program_id(0) == 0)
        def _():
            for r in gb_refs:
                r[...] = jnp.zeros_like(r)

        for r, val in zip(gb_refs, g[nt:]):
            r[...] += val

    def width(a):
        return a[2] if isinstance(a, tuple) else a.shape[1]

    res = pl.pallas_call(
        body, grid=(t_len // tm,), name=name,
        in_specs=list(specs) + [_full_spec(b) for b in list(bcast) + list(bconsts)],
        out_specs=[pl.BlockSpec((tm, width(tiled[j])), lambda i: (i, 0)) for j in want] + [_full_spec(b) for b in bcast],
        out_shape=[jax.ShapeDtypeStruct((t_len, width(tiled[j])), grads[j]) for j in want]
        + [jax.ShapeDtypeStruct(b.shape, F32) for b in bcast],
        compiler_params=_cparams(("arbitrary",)),
    )(*arrs, *bcast, *bconsts)
    tg = [None] * nt
    for r, j in zip(res[:len(want)], want):
        tg[j] = r
    return tg, list(res[len(want):])


def _matmul(a, b, mode, out_dtype, *, name):
    if mode in ("tn", "kn"):
        assert out_dtype == F32
        k_len, m_len = a.shape if mode == "tn" else a.shape[::-1]
        n_len = b.shape[1]
        tm, tn = m_len, n_len
        while tm * tn * 4 > MM_RESIDENT_BYTES and tn % (2 * LANE) == 0:
            tn //= 2
        tk = _pick(k_len, (512, 256, 128))
        if mode == "kn" and k_len % 1024 == 0 and 4 * 1024 * (tm + tn) + 8 * tm * tn <= 44 * 1024 * 1024:
            tk = 1024
    else:
        m_len, k_len = a.shape
        n_len = b.shape[1] if mode == "nn" else b.shape[0]
        tk, tn = k_len, n_len
        while tk * tn * 2 > MM_RESIDENT_BYTES and tn % (2 * LANE) == 0:
            tn //= 2
        tm = _pick(m_len, (512, 256, 128))
        while tm * max(4 * tn, 2 * tk) > MM_RESIDENT_BYTES // 2 and tm % 256 == 0:
            tm //= 2
    nk = k_len // tk
    if mode == "nn":
        a_spec = pl.BlockSpec((tm, tk), lambda j, i, k: (i, k))
        b_spec = pl.BlockSpec((tk, tn), lambda j, i, k: (k, j))
    elif mode == "nt":
        a_spec = pl.BlockSpec((tm, tk), lambda j, i, k: (i, k))
        b_spec = pl.BlockSpec((tn, tk), lambda j, i, k: (j, k))
    elif mode == "kn":
        a_spec = pl.BlockSpec((tm, tk), lambda j, i, k: (i, k))
        b_spec = pl.BlockSpec((tk, tn), lambda j, i, k: (k, j))
    else:
        a_spec = pl.BlockSpec((tk, tm), lambda j, i, k: (k, i))
        b_spec = pl.BlockSpec((tk, tn), lambda j, i, k: (k, j))

    def body(a_ref, b_ref, o_ref):
        part = _dot(a_ref[...], b_ref[...], "nn" if mode == "kn" else mode)
        if nk == 1:
            o_ref[...] = part.astype(o_ref.dtype)
        else:
            @pl.when(pl.program_id(2) == 0)
            def _():
                o_ref[...] = jnp.zeros_like(o_ref)

            o_ref[...] += part

    return pl.pallas_call(
        body, grid=(n_len // tn, m_len // tm, nk), name=name,
        in_specs=[a_spec, b_spec],
        out_specs=pl.BlockSpec((tm, tn), lambda j, i, k: (i, j)),
        out_shape=jax.ShapeDtypeStruct((m_len, n_len), out_dtype),
        compiler_params=_cparams(("parallel", "parallel", "arbitrary")),
    )(a, b)


def _ffn_act(h, w13, *, name):
    t_len, d = h.shape
    tm = min(ROW_TILE, t_len)

    def body(h_ref, w_ref, s_ref, st_ref, ab_ref):
        ab = _dot(h_ref[...], w_ref[...], "nn")
        a, b = ab[:, :FF_HALF], ab[:, FF_HALF:]
        s = a * jax.nn.sigmoid(a) * b
        s_ref[...] = s.astype(s_ref.dtype)
        st_ref[...] = s.T.astype(st_ref.dtype)
        ab_ref[...] = ab.astype(ab_ref.dtype)

    return pl.pallas_call(
        body, grid=(2, t_len // tm), name=name,
        in_specs=[pl.BlockSpec((tm, d), lambda f, i: (i, 0)), pl.BlockSpec((d, 2 * FF_HALF), lambda f, i: (0, f))],
        out_specs=[pl.BlockSpec((tm, FF_HALF), lambda f, i: (i, f)), pl.BlockSpec((FF_HALF, tm), lambda f, i: (f, i)),
                   pl.BlockSpec((tm, 2 * FF_HALF), lambda f, i: (i, f))],
        out_shape=[jax.ShapeDtypeStruct((t_len, D_FF), BF16), jax.ShapeDtypeStruct((D_FF, t_len), BF16),
                   jax.ShapeDtypeStruct((t_len, 2 * D_FF), BF16)],
        compiler_params=_cparams(("parallel", "parallel")),
    )(h, w13)


def _ffn_act_dh_bwd(ab, dy, w13, w2, *, name):
    t_len = ab.shape[0]
    d = dy.shape[1]
    tm = min(ROW_TILE, t_len)

    def body(ab_ref, dy_ref, w_ref, w2_ref, o_ref, dh_ref):
        abv = ab_ref[...].astype(F32)
        a, b = abv[:, :FF_HALF], abv[:, FF_HALF:]
        ds = _dot(dy_ref[...], w2_ref[...], "nt")
        sig = jax.nn.sigmoid(a)
        dab = jnp.concatenate([ds * b * (sig * (1.0 + a * (1.0 - sig))), ds * (a * sig)], axis=-1).astype(o_ref.dtype)
        o_ref[...] = dab
        part = _dot(dab, w_ref[...], "nt")

        @pl.when(pl.program_id(1) == 0)
        def _():
            dh_ref[...] = part

        @pl.when(pl.program_id(1) == 1)
        def _():
            dh_ref[...] += part

    return pl.pallas_call(
        body, grid=(t_len // tm, 2), name=name,
        in_specs=[pl.BlockSpec((tm, 2 * FF_HALF), lambda i, f: (i, f)), pl.BlockSpec((tm, d), lambda i, f: (i, 0)),
                  pl.BlockSpec((d, 2 * FF_HALF), lambda i, f: (0, f)), pl.BlockSpec((FF_HALF, d), lambda i, f: (f, 0))],
        out_specs=[pl.BlockSpec((tm, 2 * FF_HALF), lambda i, f: (i, f)), pl.BlockSpec((tm, d), lambda i, f: (i, 0))],
        out_shape=[jax.ShapeDtypeStruct((t_len, 2 * D_FF), BF16), jax.ShapeDtypeStruct((t_len, d), F32)],
        compiler_params=_cparams(("parallel", "arbitrary")),
    )(ab, dy, w13, w2)


CONV_CB = 512
HALO = 8


def _conv_fwd(p, c0, n_ch, w, b, *, name):
    t_len = p.shape[0]
    tm = min(ROW_TILE, t_len)
    hb = tm // HALO
    cb0 = c0 // CONV_CB

    def body(x_ref, halo_ref, w_ref, b_ref, act_ref, pre_ref):
        first = pl.program_id(1) == 0
        halo = jnp.where(first, 0.0, halo_ref[...])
        xx = jnp.concatenate([halo, x_ref[...]], axis=0)
        wv = w_ref[...]
        acc = b_ref[...] + wv[CONV_K - 1:CONV_K] * xx[HALO:]
        for j in range(CONV_K - 1):
            acc = acc + wv[j:j + 1] * pltpu.roll(xx, CONV_K - 1 - j, 0)[HALO:]
        pre_ref[...] = acc
        act_ref[...] = acc * jax.nn.sigmoid(acc)

    return pl.pallas_call(
        body, grid=(n_ch // CONV_CB, t_len // tm), name=name,
        in_specs=[pl.BlockSpec((tm, CONV_CB), lambda j, i: (i, cb0 + j)),
                  pl.BlockSpec((HALO, CONV_CB), lambda j, i: (jnp.maximum(i * hb - 1, 0), cb0 + j)),
                  pl.BlockSpec((CONV_K, CONV_CB), lambda j, i: (0, j)),
                  pl.BlockSpec((1, CONV_CB), lambda j, i: (0, j))],
        out_specs=[pl.BlockSpec((tm, CONV_CB), lambda j, i: (i, j))] * 2,
        out_shape=[jax.ShapeDtypeStruct((t_len, n_ch), F32)] * 2,
        compiler_params=_cparams(("parallel", "arbitrary")),
    )(p, p, w, b)


def _conv_bwd(dact, pre, pre_c0, p, p_c0, w, *, name):
    t_len, n_ch = dact.shape
    tm = min(ROW_TILE, t_len)
    hb = tm // HALO
    nt = t_len // tm
    last_hb = t_len // HALO - 1
    cb0 = p_c0 // CONV_CB
    cbp = pre_c0 // CONV_CB

    def dsilu(z):
        sig = jax.nn.sigmoid(z)
        return sig * (1.0 + z * (1.0 - sig))

    def body(d_ref, dn_ref, pre_ref, pren_ref, x_ref, xh_ref, w_ref, dx_ref, dw_ref, db_ref):
        i = pl.program_id(1)
        dpre = d_ref[...] * dsilu(pre_ref[...])
        dnext = jnp.where(i == nt - 1, 0.0, dn_ref[...] * dsilu(pren_ref[...]))
        ext = jnp.concatenate([dpre, dnext], axis=0)
        xx = jnp.concatenate([jnp.where(i == 0, 0.0, xh_ref[...]), x_ref[...]], axis=0)
        wv = w_ref[...]
        dx = wv[CONV_K - 1:CONV_K] * dpre
        for j in range(CONV_K - 1):
            dx = dx + wv[j:j + 1] * pltpu.roll(ext, tm + HALO - (CONV_K - 1 - j), 0)[:tm]
        dx_ref[...] = dx
        dws = [jnp.sum(dpre * pltpu.roll(xx, CONV_K - 1 - j, 0)[HALO:], axis=0, keepdims=True) for j in range(CONV_K - 1)]
        dws.append(jnp.sum(dpre * x_ref[...], axis=0, keepdims=True))

        @pl.when(i == 0)
        def _():
            dw_ref[...] = jnp.zeros_like(dw_ref)
            db_ref[...] = jnp.zeros_like(db_ref)

        dw_ref[...] += jnp.concatenate(dws, axis=0)
        db_ref[...] += jnp.sum(dpre, axis=0, keepdims=True)

    tile = lambda off: pl.BlockSpec((tm, CONV_CB), lambda j, i: (i, off + j))
    nxt = lambda off: pl.BlockSpec((HALO, CONV_CB), lambda j, i: (jnp.minimum((i + 1) * hb, last_hb), off + j))
    return pl.pallas_call(
        body, grid=(n_ch // CONV_CB, nt), name=name,
        in_specs=[tile(0), nxt(0), tile(cbp), nxt(cbp), tile(cb0),
                  pl.BlockSpec((HALO, CONV_CB), lambda j, i: (jnp.maximum(i * hb - 1, 0), cb0 + j)),
                  pl.BlockSpec((CONV_K, CONV_CB), lambda j, i: (0, cbp + j))],
        out_specs=[tile(0), pl.BlockSpec((CONV_K, CONV_CB), lambda j, i: (0, j)), pl.BlockSpec((1, CONV_CB), lambda j, i: (0, j))],
        out_shape=[jax.ShapeDtypeStruct((t_len, n_ch), F32), jax.ShapeDtypeStruct((CONV_K, n_ch), F32),
                   jax.ShapeDtypeStruct((1, n_ch), F32)],
        compiler_params=_cparams(("parallel", "arbitrary")),
    )(dact, dact, pre, pre, p, p, w)


@jax.custom_vjp
def _inv_unit_lower_many(a_cat):
    c = a_cat.shape[0]
    assert LANE % c == 0 and a_cat.shape[1] % LANE == 0
    x = (_iota2(a_cat.shape, 0) == _iota2(a_cat.shape, 1) % c).astype(F32)
    tiles = [a_cat[:, t * LANE:(t + 1) * LANE] for t in range(a_cat.shape[1] // LANE)]
    first = (_iota2((c, LANE), 1) // c) * c
    for j in range(c - 1):
        col = jnp.concatenate([jnp.take_along_axis(t, first + j, axis=1) for t in tiles], axis=-1)
        x = x - col * x[j:j + 1, :]
    return x


def _inv_fwd(a_cat):
    x = _inv_unit_lower_many(a_cat)
    return x, x


def _inv_bwd(x, g):
    c = x.shape[0]
    parts = [-_dot(x[:, s], _dot(g[:, s], x[:, s], "nt", F32_3PASS), "tn", F32_3PASS)
             for s in (slice(i * c, (i + 1) * c) for i in range(x.shape[1] // c))]
    return (jnp.concatenate(parts, axis=-1),)


_inv_unit_lower_many.defvjp(_inv_fwd, _inv_bwd)


def _l2norm(x):
    return x * lax.rsqrt(jnp.sum(x * x, axis=-1, keepdims=True) + NORM_EPS)


def _rms(x):
    return x * lax.rsqrt(jnp.mean(x * x, axis=-1, keepdims=True) + NORM_EPS)


def _tri_masks(c):
    rows, cols = _iota2((c, c), 0), _iota2((c, c), 1)
    return rows >= cols, rows > cols, (rows >= cols).astype(F32), (rows <= cols).astype(F32)


def _gdn_tile(q, k, v, misc, s0, alog, dtb):
    c = CHUNK
    lower, strict, ltri, utri = _tri_masks(c)
    n_chunk = q.shape[0] // c
    pre = []
    for h in range(GDN_HEADS):
        hs = slice(h * LANE, (h + 1) * LANE)
        neg_a = -jnp.exp(alog[:, h:h + 1])
        for ci in range(n_chunk):
            sl = slice(ci * c, (ci + 1) * c)
            qn = _l2norm(q[sl, hs]) * (GDN_DK ** -0.5)
            kn = _l2norm(k[sl, hs])
            beta = jax.nn.sigmoid(misc[sl, 64 + h:65 + h])
            g = neg_a * jax.nn.softplus(misc[sl, 68 + h:69 + h] + dtb[:, h:h + 1])
            gb = jnp.broadcast_to(g, (c, c))
            gc_col = _mm(ltri, gb, "nn", True)
            gc_row = _mm(gb, utri, "tn", True)
            decay = jnp.where(lower, jnp.exp(jnp.where(lower, gc_col - gc_row, 0.0)), 0.0)
            kb = kn * beta
            a_mat = jnp.where(strict, _mm(kb, kn, "nt", False) * decay, 0.0)
            pre.append((qn, kn, kb, v[sl, hs] * beta, decay, gc_col[:, 0:1], gc_col[c - 1:c, 0:1], a_mat))
    t_all = _inv_unit_lower_many(jnp.concatenate([p[7] for p in pre], axis=-1))
    o_heads, s_heads = [], []
    for h in range(GDN_HEADS):
        s = s0[h * GDN_DK:(h + 1) * GDN_DK]
        outs = []
        for ci in range(n_chunk):
            i = h * n_chunk + ci
            qn, kn, kb, vb, decay, gc, g_last, _ = pre[i]
            t_inv = t_all[:, i * c:(i + 1) * c]
            u = _mm(t_inv, vb, "nn", F32_3PASS)
            w = _mm(t_inv, kb * jnp.exp(gc), "nn", F32_3PASS)
            attn = _mm(qn, kn, "nt", False) * decay
            k_end = kn * jnp.exp(g_last - gc)
            q_start = qn * jnp.exp(gc)
            v_new = u - _mm(w, s, "nn", False)
            outs.append(_mm(q_start, s, "nn", False) + _mm(attn, v_new, "nn", False))
            s = s * jnp.exp(g_last) + _mm(k_end, v_new, "tn", False)
        o_heads.append(jnp.concatenate(outs, axis=0))
        s_heads.append(s)
    return jnp.concatenate(o_heads, axis=-1), jnp.concatenate(s_heads, axis=0)


def _gdn_specs(tt, rev_n=None):
    t = (lambda i: i) if rev_n is None else (lambda i: rev_n - 1 - i)
    col = lambda j: pl.BlockSpec((tt, GDN_HEADS * LANE), lambda i: (t(i), j))
    vec = pl.BlockSpec((1, LANE), lambda i: (0, 0))
    misc = pl.BlockSpec((tt, LANE), lambda i: (t(i), EV_MISC // LANE))
    return [col(0), col(1), col(2), misc, vec, vec], t


def _gdn_fwd(act, p, alog, dtb, *, name):
    t_len = act.shape[0]
    tt = min(SEQ_TILE, t_len)
    ntile = t_len // tt
    in_specs, _ = _gdn_specs(tt)

    def body(q_ref, k_ref, v_ref, m_ref, al_ref, dt_ref, o_ref, s_ref, state):
        @pl.when(pl.program_id(0) == 0)
        def _():
            state[...] = jnp.zeros_like(state)

        s_ref[0] = state[...]
        o, s_new = _gdn_tile(q_ref[...], k_ref[...], v_ref[...], m_ref[...], state[...], al_ref[...], dt_ref[...])
        o_ref[...] = o
        state[...] = s_new

    return pl.pallas_call(
        body, grid=(ntile,), name=name, in_specs=in_specs,
        out_specs=[pl.BlockSpec((tt, GDN_HEADS * LANE), lambda i: (i, 0)),
                   pl.BlockSpec((1, GDN_HEADS * GDN_DK, LANE), lambda i: (i, 0, 0))],
        out_shape=[jax.ShapeDtypeStruct((t_len, GDN_HEADS * LANE), F32),
                   jax.ShapeDtypeStruct((ntile, GDN_HEADS * GDN_DK, LANE), F32)],
        scratch_shapes=[pltpu.VMEM((GDN_HEADS * GDN_DK, LANE), F32)],
        compiler_params=_cparams(("arbitrary",)),
    )(act, act, act, p, alog, dtb)


def _gdn_bwd(act, p, alog, dtb, states, do, *, name):
    t_len = act.shape[0]
    tt = min(SEQ_TILE, t_len)
    ntile = t_len // tt
    in_specs, t = _gdn_specs(tt, ntile)

    def body(q_ref, k_ref, v_ref, m_ref, al_ref, dt_ref, s0_ref, do_ref,
             dq_ref, dk_ref, dv_ref, dm_ref, dal_ref, ddt_ref, dstate):
        @pl.when(pl.program_id(0) == 0)
        def _():
            dstate[...] = jnp.zeros_like(dstate)
            dal_ref[...] = jnp.zeros_like(dal_ref)
            ddt_ref[...] = jnp.zeros_like(ddt_ref)

        _, vjp = jax.vjp(_gdn_tile, q_ref[...], k_ref[...], v_ref[...], m_ref[...], s0_ref[0], al_ref[...], dt_ref[...])
        dq, dk, dv, dm, ds0, dal, ddt = vjp((do_ref[...], dstate[...]))
        dq_ref[...] = dq
        dk_ref[...] = dk
        dv_ref[...] = dv
        dm_ref[...] = dm
        dstate[...] = ds0
        dal_ref[...] += dal
        ddt_ref[...] += ddt

    row = pl.BlockSpec((tt, GDN_HEADS * LANE), lambda i: (t(i), 0))
    vec = pl.BlockSpec((1, LANE), lambda i: (0, 0))
    return pl.pallas_call(
        body, grid=(ntile,), name=name,
        in_specs=in_specs + [pl.BlockSpec((1, GDN_HEADS * GDN_DK, LANE), lambda i: (t(i), 0, 0)), row],
        out_specs=[row, row, row, pl.BlockSpec((tt, LANE), lambda i: (t(i), 0)), vec, vec],
        out_shape=[jax.ShapeDtypeStruct((t_len, GDN_HEADS * LANE), F32)] * 3
        + [jax.ShapeDtypeStruct((t_len, LANE), F32)] + [jax.ShapeDtypeStruct((1, LANE), F32)] * 2,
        scratch_shapes=[pltpu.VMEM((GDN_HEADS * GDN_DK, LANE), F32)],
        compiler_params=_cparams(("arbitrary",)),
    )(act, act, act, p, alog, dtb, states, do)


def _head_expand():
    return jnp.asarray(np.repeat(np.eye(LANE, SSD_HEADS, dtype=np.float32), SSD_HEADDIM, axis=1))


@jax.custom_vjp
def _per_head(v, expand):
    rows = max(v.shape[0], 8)
    v8 = jnp.broadcast_to(v, (rows, LANE))
    half = _iota2((rows, LANE), 1) // SSD_HEADDIM
    tiles = [jnp.take_along_axis(v8, half + 2 * j, axis=1) for j in range(SSD_D_INNER // LANE)]
    return jnp.concatenate(tiles, axis=-1)[:v.shape[0]]


def _per_head_fwd(v, expand):
    return _per_head(v, expand), expand


def _per_head_bwd(expand, g):
    rows = g.shape[0]
    g8 = jnp.broadcast_to(g, (8, g.shape[1])) if rows == 1 else g
    dv = lax.dot_general(g8, expand, _DN["nt"], precision=lax.Precision.HIGH, preferred_element_type=F32)
    return dv[0:1] if rows == 1 else dv, None


_per_head.defvjp(_per_head_fwd, _per_head_bwd)


def _ssd_tile(xs, bm, cm, dtr, hs0, alog, dtb, dsk, expand):
    c = CHUNK
    gw = SSD_D_INNER // SSD_GROUPS
    hpg = SSD_HEADS // SSD_GROUPS
    lower, _, ltri, utri = _tri_masks(c)
    half = _iota2((c, LANE), 1) // SSD_HEADDIM
    dt = jax.nn.softplus(dtr + dtb)
    da = dt * (-jnp.exp(alog))
    xdt = xs * _per_head(dt, expand)
    d_x = _per_head(dsk, expand)
    hs = [hs0[g * SSD_STATE:(g + 1) * SSD_STATE] for g in range(SSD_GROUPS)]
    ys = []
    for ci in range(xs.shape[0] // c):
        sl = slice(ci * c, (ci + 1) * c)
        acs = _mm(ltri, da[sl], "nn", True)
        acs_t = _mm(da[sl], utri, "tn", True)
        acs_last = acs[c - 1:c, :]
        e_start = _per_head(jnp.exp(acs), expand)
        e_end = _per_head(jnp.exp(acs_last - acs), expand)
        e_dec = _per_head(jnp.exp(acs_last), expand)
        xdt_c = xdt[sl]
        y_tiles = [None] * (SSD_D_INNER // LANE)
        y_off = []
        for g in range(SSD_GROUPS):
            b_g = bm[sl, g * SSD_STATE:(g + 1) * SSD_STATE]
            c_g = cm[sl, g * SSD_STATE:(g + 1) * SSD_STATE]
            gs = slice(g * gw, (g + 1) * gw)
            cb = _mm(c_g, b_g, "nt", False)
            y_off.append(_mm(c_g, hs[g], "nn", False) * e_start[:, gs])
            for r in range(hpg):
                h = g * hpg + r
                j = h // 2
                lm = jnp.where(lower, jnp.exp(jnp.where(lower, acs[:, h:h + 1] - acs_t[h:h + 1, :], 0.0)), 0.0)
                xm = jnp.where(half == (h % 2), xdt_c[:, j * LANE:(j + 1) * LANE], 0.0)
                part = _mm(cb * lm, xm, "nn", False)
                y_tiles[j] = part if y_tiles[j] is None else y_tiles[j] + part
            hs[g] = hs[g] * e_dec[:, gs] + _mm(b_g, xdt_c[:, gs] * e_end[:, gs], "tn", False)
        ys.append(jnp.concatenate(y_tiles, axis=-1) + jnp.concatenate(y_off, axis=-1) + d_x * xs[sl])
    return jnp.concatenate(ys, axis=0), jnp.concatenate(hs, axis=0)


def _ssd_specs(tt, rev_n=None):
    t = (lambda i: i) if rev_n is None else (lambda i: rev_n - 1 - i)
    vec = pl.BlockSpec((1, LANE), lambda i: (0, 0))
    specs = [pl.BlockSpec((tt, SSD_D_INNER), lambda i: (t(i), 0)),
             pl.BlockSpec((tt, 512), lambda i: (t(i), SSD_D_INNER // 512)),
             pl.BlockSpec((tt, 512), lambda i: (t(i), SSD_D_INNER // 512 + 1)),
             pl.BlockSpec((tt, LANE), lambda i: (t(i), OD_DT // LANE)), vec, vec, vec,
             pl.BlockSpec((LANE, SSD_D_INNER), lambda i: (0, 0))]
    return specs, t


def _ssd_fwd(act, p, alog, dtb, dsk, *, name):
    t_len = act.shape[0]
    tt = min(SEQ_TILE, t_len)
    ntile = t_len // tt
    in_specs, _ = _ssd_specs(tt)

    def body(x_ref, b_ref, c_ref, dt_ref, al_ref, db_ref, dk_ref, e_ref, y_ref, s_ref, state):
        @pl.when(pl.program_id(0) == 0)
        def _():
            state[...] = jnp.zeros_like(state)

        s_ref[0] = state[...]
        y, hs = _ssd_tile(x_ref[...], b_ref[...], c_ref[...], dt_ref[...], state[...], al_ref[...], db_ref[...],
                          dk_ref[...], e_ref[...])
        y_ref[...] = y
        state[...] = hs

    return pl.pallas_call(
        body, grid=(ntile,), name=name, in_specs=in_specs,
        out_specs=[pl.BlockSpec((tt, SSD_D_INNER), lambda i: (i, 0)),
                   pl.BlockSpec((1, SSD_GROUPS * SSD_STATE, 512), lambda i: (i, 0, 0))],
        out_shape=[jax.ShapeDtypeStruct((t_len, SSD_D_INNER), F32),
                   jax.ShapeDtypeStruct((ntile, SSD_GROUPS * SSD_STATE, 512), F32)],
        scratch_shapes=[pltpu.VMEM((SSD_GROUPS * SSD_STATE, 512), F32)],
        compiler_params=_cparams(("arbitrary",)),
    )(act, act, act, p, alog, dtb, dsk, _head_expand())


def _ssd_bwd(act, p, alog, dtb, dsk, states, dy, *, name):
    t_len = act.shape[0]
    tt = min(SEQ_TILE, t_len)
    ntile = t_len // tt
    in_specs, t = _ssd_specs(tt, ntile)

    def body(x_ref, b_ref, c_ref, dt_ref, al_ref, db_ref, dk_ref, e_ref, s0_ref, dy_ref,
             dx_ref, dbm_ref, dcm_ref, ddt_ref, dal_ref, ddb_ref, ddk_ref, dstate):
        @pl.when(pl.program_id(0) == 0)
        def _():
            dstate[...] = jnp.zeros_like(dstate)
            dal_ref[...] = jnp.zeros_like(dal_ref)
            ddb_ref[...] = jnp.zeros_like(ddb_ref)
            ddk_ref[...] = jnp.zeros_like(ddk_ref)

        expand = e_ref[...]

        def f(xs, bm, cm, dtr, hs0, al, db, dk):
            return _ssd_tile(xs, bm, cm, dtr, hs0, al, db, dk, expand)

        _, vjp = jax.vjp(f, x_ref[...], b_ref[...], c_ref[...], dt_ref[...], s0_ref[0], al_ref[...], db_ref[...],
                         dk_ref[...])
        dx, dbm, dcm, ddt, dhs, dal, ddb, ddk = vjp((dy_ref[...], dstate[...]))
        dx_ref[...] = dx
        dbm_ref[...] = dbm
        dcm_ref[...] = dcm
        ddt_ref[...] = ddt
        dstate[...] = dhs
        dal_ref[...] += dal
        ddb_ref[...] += ddb
        ddk_ref[...] += ddk

    vec = pl.BlockSpec((1, LANE), lambda i: (0, 0))
    rows = lambda w: pl.BlockSpec((tt, w), lambda i: (t(i), 0))
    return pl.pallas_call(
        body, grid=(ntile,), name=name,
        in_specs=in_specs + [pl.BlockSpec((1, SSD_GROUPS * SSD_STATE, 512), lambda i: (t(i), 0, 0)), rows(SSD_D_INNER)],
        out_specs=[rows(SSD_D_INNER), rows(512), rows(512), rows(LANE), vec, vec, vec],
        out_shape=[jax.ShapeDtypeStruct((t_len, SSD_D_INNER), F32), jax.ShapeDtypeStruct((t_len, 512), F32),
                   jax.ShapeDtypeStruct((t_len, 512), F32), jax.ShapeDtypeStruct((t_len, LANE), F32)]
        + [jax.ShapeDtypeStruct((1, LANE), F32)] * 3,
        scratch_shapes=[pltpu.VMEM((SSD_GROUPS * SSD_STATE, 512), F32)],
        compiler_params=_cparams(("arbitrary",)),
    )(act, act, act, p, alog, dtb, dsk, _head_expand(), states, dy)


ATT_SCALE = (MLA_NOPE + MLA_ROPE) ** -0.5
ATT_SCALE2 = ATT_SCALE * math.log2(math.e)
QK_W = 2 * LANE


def _chunk_mask(tq):
    return (_iota2((tq, tq), 1) // CHUNK) <= (_iota2((tq, tq), 0) // CHUNK)


def _attn_fwd(qc, kc, vv, *, name):
    t_len = qc.shape[0]
    tq = min(ATT_TILE, t_len)
    nq = t_len // tq

    pairs = [(qi, ki) for qi in range(nq) for ki in range(qi + 1)]
    q_tab = jnp.asarray([p[0] for p in pairs], jnp.int32)
    k_tab = jnp.asarray([p[1] for p in pairs], jnp.int32)

    def body(qt_ref, kt_ref, q_ref, k_ref, v_ref, o_ref, lse_ref, m_s, l_s, acc_s):
        qi, ki = qt_ref[pl.program_id(1)], kt_ref[pl.program_id(1)]

        @pl.when(ki == 0)
        def _():
            m_s[...] = jnp.full_like(m_s, -jnp.inf)
            l_s[...] = jnp.zeros_like(l_s)
            acc_s[...] = jnp.zeros_like(acc_s)

        def step(masked):
            s = _dot(q_ref[...], k_ref[...], "nt") * ATT_SCALE2
            if masked:
                s = jnp.where(_chunk_mask(tq), s, -jnp.inf)
            m_new = jnp.maximum(m_s[...], jnp.max(s, axis=-1, keepdims=True))
            alpha = jnp.exp2(m_s[...] - m_new)
            p = jnp.exp2(s - m_new)
            l_s[...] = alpha * l_s[...] + jnp.sum(p, axis=-1, keepdims=True)
            acc_s[...] = alpha * acc_s[...] + _dot(p, v_ref[...], "nn")
            m_s[...] = m_new

        @pl.when(ki < qi)
        def _():
            step(False)

        @pl.when(ki == qi)
        def _():
            step(True)
            o_ref[...] = acc_s[...] / l_s[...]
            lse_ref[...] = jnp.broadcast_to(m_s[...] + jnp.log2(l_s[...]), lse_ref.shape)

    q_idx = lambda h, s, qt, kt: (qt[s], h)
    k_idx = lambda h, s, qt, kt: (kt[s], h)
    return pl.pallas_call(
        body, name=name,
        grid_spec=pltpu.PrefetchScalarGridSpec(
            num_scalar_prefetch=2, grid=(MLA_HEADS, len(pairs)),
            in_specs=[pl.BlockSpec((tq, QK_W), q_idx), pl.BlockSpec((tq, QK_W), k_idx), pl.BlockSpec((tq, LANE), k_idx)],
            out_specs=[pl.BlockSpec((tq, LANE), q_idx)] * 2,
            scratch_shapes=[pltpu.VMEM((tq, 1), F32), pltpu.VMEM((tq, 1), F32), pltpu.VMEM((tq, LANE), F32)]),
        out_shape=[jax.ShapeDtypeStruct((t_len, MLA_HEADS * LANE), F32)] * 2,
        compiler_params=_cparams(("parallel", "arbitrary")),
    )(q_tab, k_tab, qc, kc, vv)


def _attn_probs(q, k, v, do, o, lse, masked, tq):
    s = _dot(q, k, "nt") * ATT_SCALE2
    if masked:
        s = jnp.where(_chunk_mask(tq), s, -jnp.inf)
    p = jnp.exp2(s - lse[:, 0:1])
    delta = jnp.sum(do * o, axis=-1, keepdims=True)
    ds = p * (_dot(do, v, "nt") - delta)
    return p, ds


def _attn_bwd(qc, kc, vv, o, lse, do, *, name):
    t_len = qc.shape[0]
    tq = min(ATT_TILE, t_len)
    nq = t_len // tq

    pairs = [(qi, ki) for ki in range(nq) for qi in range(ki, nq)]
    q_tab = jnp.asarray([p[0] for p in pairs], jnp.int32)
    k_tab = jnp.asarray([p[1] for p in pairs], jnp.int32)

    def body(qt_ref, kt_ref, q_ref, k_ref, v_ref, o_ref, lse_ref, do_ref, dq_hbm, dk_ref, dv_ref, dq_s, dk_s, dv_s):
        head = pl.program_id(0)
        qi, ki = qt_ref[pl.program_id(1)], kt_ref[pl.program_id(1)]
        rows = pl.ds(pl.multiple_of(qi * tq, tq), tq)

        @pl.when(qi == ki)
        def _():
            dk_s[...] = jnp.zeros_like(dk_s)
            dv_s[...] = jnp.zeros_like(dv_s)

        def step(masked):
            p, ds = _attn_probs(q_ref[...], k_ref[...], v_ref[...], do_ref[...], o_ref[...], lse_ref[...], masked, tq)
            dv_s[...] += _dot(p, do_ref[...], "tn")
            dk_s[...] += _dot(ds, q_ref[...], "tn")
            part = _dot(ds, k_ref[...], "nn")

            @pl.when(ki == 0)
            def _():
                dq_s[rows, :] = part

            @pl.when(ki > 0)
            def _():
                dq_s[rows, :] += part

        @pl.when(qi > ki)
        def _():
            step(False)

        @pl.when(qi == ki)
        def _():
            step(True)
            dq_s[rows, :] = dq_s[rows, :] * ATT_SCALE
            pltpu.sync_copy(dq_s.at[rows, :], dq_hbm.at[rows, pl.ds(pl.multiple_of(head * QK_W, QK_W), QK_W)])

        @pl.when(qi == nq - 1)
        def _():
            dk_ref[...] = dk_s[...] * ATT_SCALE
            dv_ref[...] = dv_s[...]

    q_idx = lambda h, s, qt, kt: (qt[s], h)
    k_idx = lambda h, s, qt, kt: (kt[s], h)
    return pl.pallas_call(
        body, name=name,
        grid_spec=pltpu.PrefetchScalarGridSpec(
            num_scalar_prefetch=2, grid=(MLA_HEADS, len(pairs)),
            in_specs=[pl.BlockSpec((tq, QK_W), q_idx), pl.BlockSpec((tq, QK_W), k_idx), pl.BlockSpec((tq, LANE), k_idx),
                      pl.BlockSpec((tq, LANE), q_idx), pl.BlockSpec((tq, LANE), q_idx), pl.BlockSpec((tq, LANE), q_idx)],
            out_specs=[pl.BlockSpec(memory_space=pl.ANY), pl.BlockSpec((tq, QK_W), k_idx), pl.BlockSpec((tq, LANE), k_idx)],
            scratch_shapes=[pltpu.VMEM((t_len, QK_W), F32), pltpu.VMEM((tq, QK_W), F32), pltpu.VMEM((tq, LANE), F32)]),
        out_shape=[jax.ShapeDtypeStruct((t_len, MLA_HEADS * QK_W), F32), jax.ShapeDtypeStruct((t_len, MLA_HEADS * QK_W), F32),
                   jax.ShapeDtypeStruct((t_len, MLA_HEADS * LANE), F32)],
        compiler_params=_cparams(("arbitrary", "arbitrary")),
    )(q_tab, k_tab, qc, kc, vv, o, lse, do)


def _adaln_fn(x, g, shift, scale):
    return ((_rms(x) * g) * (1.0 + scale) + shift,)


def _resid_fn(coef, y, x, gate):
    return (x + coef * gate * y,)


def _rms2_fn(cq, ckv, gq, gkv):
    return _rms(cq) * gq, _rms(ckv) * gkv


@jax.custom_vjp
def _swap_halves(x):
    return jnp.concatenate([x[:, 32:64], x[:, 0:32], x[:, 64:128]], axis=-1)


_swap_halves.defvjp(lambda x: (_swap_halves(x), None), lambda _, g: (_swap_halves(g),))


def _rope_fn(q, kv, misc, pos, invf, sgn):
    ang = pos * invf
    cos, sin = jnp.cos(ang), jnp.sin(ang) * sgn

    def rope(x):
        return x * cos + _swap_halves(x) * sin

    k_pe = rope(jnp.where(_iota2(misc.shape, 1) < MLA_ROPE, misc, 0.0))
    qs, ks = [], []
    for h in range(MLA_HEADS):
        qs += [q[:, h * LANE:(h + 1) * LANE], rope(q[:, (MLA_HEADS + h) * LANE:(MLA_HEADS + h + 1) * LANE])]
        ks += [kv[:, h * LANE:(h + 1) * LANE], k_pe]
    return jnp.concatenate(qs, axis=-1), jnp.concatenate(ks, axis=-1), kv[:, MLA_HEADS * LANE:]


def _ev_out_fn(oa, z, ob, g):
    parts = []
    for h in range(GDN_HEADS):
        hs = slice(h * LANE, (h + 1) * LANE)
        zz = z[:, hs]
        parts.append(_rms(oa[:, hs]) * g * (zz * jax.nn.sigmoid(zz)))
    return (jnp.concatenate(parts + [ob], axis=-1),)


def _od_out_fn(y, z, g):
    yz = y * (z * jax.nn.sigmoid(z))
    gw = SSD_D_INNER // SSD_GROUPS
    return (jnp.concatenate([_rms(yz[:, i * gw:(i + 1) * gw]) for i in range(SSD_GROUPS)], axis=-1) * g,)


def _loss_bwd(x, tgt, g, *, name):
    t_len, d = x.shape
    tm = min(ROW_TILE // 2, t_len)

    def body(x_ref, t_ref, g_ref, loss_ref, dx_ref, dg_ref):
        tgt_v = t_ref[...]

        def f(xv, gv):
            err = _rms(xv) * gv - tgt_v
            return 0.5 * jnp.sum(jnp.mean(err * err, axis=-1, keepdims=True), axis=0, keepdims=True)

        val, vjp = jax.vjp(f, x_ref[...], g_ref[...])
        dx, dg = vjp(jnp.ones((1, 1), F32))
        dx_ref[...] = dx

        @pl.when(pl.program_id(0) == 0)
        def _():
            loss_ref[...] = jnp.zeros_like(loss_ref)
            dg_ref[...] = jnp.zeros_like(dg_ref)

        loss_ref[...] += jnp.broadcast_to(val, loss_ref.shape)
        dg_ref[...] += dg

    row = pl.BlockSpec((tm, d), lambda i: (i, 0))
    return pl.pallas_call(
        body, grid=(t_len // tm,), name=name,
        in_specs=[row, row, pl.BlockSpec((1, d), lambda i: (0, 0))],
        out_specs=[pl.BlockSpec((1, LANE), lambda i: (0, 0)), row, pl.BlockSpec((1, d), lambda i: (0, 0))],
        out_shape=[jax.ShapeDtypeStruct((1, LANE), F32), jax.ShapeDtypeStruct((t_len, d), F32),
                   jax.ShapeDtypeStruct((1, d), F32)],
        compiler_params=_cparams(("arbitrary",)),
    )(x, tgt, g)


def _mesh_pos():
    return lax.axis_index("x"), lax.axis_index("y"), lax.axis_index("c")


def _exchange(xs, scatter, *, name):
    n_arr = len(xs)

    def body(*refs):
        in_refs, out_refs = refs[:n_arr], refs[n_arr:2 * n_arr]
        send_sems, recv_sems, local_sems = refs[2 * n_arr:]
        mx, my, mc = _mesh_pos()
        me = 4 * mx + 2 * my + mc
        started = []
        for a, (in_ref, out_ref) in enumerate(zip(in_refs, out_refs)):
            def src(j, in_ref=in_ref):
                return in_ref.at[j] if scatter else in_ref

            local = pltpu.make_async_copy(src(me), out_ref.at[me], local_sems.at[a])
            local.start()
            started.append((local, None))
            for d in range(1, N_DEV):
                px = 1 - mx if d & 4 else mx
                py = 1 - my if d & 2 else my
                pc = 1 - mc if d & 1 else mc
                peer = 4 * px + 2 * py + pc
                sem = a * (N_DEV - 1) + d - 1
                send = pltpu.make_async_remote_copy(
                    src_ref=src(peer), dst_ref=out_ref.at[me], send_sem=send_sems.at[sem], recv_sem=recv_sems.at[sem],
                    device_id=(px, py, pc), device_id_type=pl.DeviceIdType.MESH)
                send.start()
                recv = pltpu.make_async_remote_copy(
                    src_ref=src(peer), dst_ref=out_ref.at[peer], send_sem=send_sems.at[sem], recv_sem=recv_sems.at[sem],
                    device_id=(px, py, pc), device_id_type=pl.DeviceIdType.MESH)
                started.append((send, recv))
        for first, recv in started:
            if recv is None:
                first.wait()
            else:
                first.wait_send()
                recv.wait_recv()

    blocks = [tuple(x.shape[1:]) if scatter else tuple(x.shape) for x in xs]
    return pl.pallas_call(
        body, name=name,
        in_specs=[pl.BlockSpec(memory_space=pl.ANY)] * n_arr,
        out_specs=[pl.BlockSpec(memory_space=pl.ANY)] * n_arr,
        out_shape=[jax.ShapeDtypeStruct((N_DEV,) + b, x.dtype) for b, x in zip(blocks, xs)],
        scratch_shapes=[pltpu.SemaphoreType.DMA((n_arr * (N_DEV - 1),)), pltpu.SemaphoreType.DMA((n_arr * (N_DEV - 1),)),
                        pltpu.SemaphoreType.DMA((n_arr,))],
        compiler_params=pltpu.CompilerParams(has_side_effects=True),
    )(*xs)


def _peer_of(d, pos):
    mx, my, mc = pos
    px = 1 - mx if d & 4 else mx
    py = 1 - my if d & 2 else my
    pc = 1 - mc if d & 1 else mc
    return (px, py, pc), 4 * px + 2 * py + pc


_HBM = pl.BlockSpec(memory_space=pltpu.HBM)
_SEM = pl.BlockSpec(memory_space=pltpu.SEMAPHORE)


def _exchange_start(xs, scatter, *, name):
    n_arr = len(xs)
    n_sem = n_arr * (N_DEV - 1)

    def body(*refs):
        in_refs, land_refs = refs[:n_arr], refs[n_arr:2 * n_arr]
        send_sems, recv_sems, token = refs[2 * n_arr], refs[2 * n_arr + 1], refs[-1]
        pos = _mesh_pos()
        me = 4 * pos[0] + 2 * pos[1] + pos[2]
        for a in range(n_arr):
            for d in range(1, N_DEV):
                dev, peer = _peer_of(d, pos)
                sem = a * (N_DEV - 1) + d - 1
                pltpu.make_async_remote_copy(
                    src_ref=in_refs[a].at[peer] if scatter else in_refs[a], dst_ref=land_refs[a].at[me],
                    send_sem=send_sems.at[sem], recv_sem=recv_sems.at[sem], device_id=dev,
                    device_id_type=pl.DeviceIdType.MESH).start()
        token[...] = jnp.zeros_like(token)

    blocks = [tuple(x.shape[1:]) if scatter else tuple(x.shape) for x in xs]
    srcs = [pltpu.with_memory_space_constraint(x, pltpu.HBM) for x in xs]
    lands = [pltpu.with_memory_space_constraint(lax.empty((N_DEV,) + b, x.dtype), pltpu.HBM) for b, x in zip(blocks, xs)]
    res = pl.pallas_call(
        body, name=name,
        out_shape=(pltpu.SemaphoreType.DMA((n_sem,)), pltpu.SemaphoreType.DMA((n_sem,)),
                   *[pltpu.HBM(a.shape, a.dtype) for a in srcs + lands], jax.ShapeDtypeStruct((8, LANE), F32)),
        in_specs=[_HBM] * (2 * n_arr),
        out_specs=(_SEM, _SEM, *[_HBM] * (2 * n_arr), pl.BlockSpec(memory_space=pltpu.VMEM)),
        input_output_aliases={i: 2 + i for i in range(2 * n_arr)},
        compiler_params=pltpu.CompilerParams(has_side_effects=pltpu.SideEffectType.DATAFLOW_SIDE_EFFECTING),
    )(*srcs, *lands)
    handle = dict(sems=res[:2], srcs=res[2:2 + n_arr], lands=res[2 + n_arr:2 + 2 * n_arr], scatter=scatter)
    return handle, res[-1][0, 0]


def _exchange_wait(handle, after, me, *, name):
    scatter = handle["scatter"]
    n_arr = len(handle["srcs"])

    def body(*refs):
        in_refs, land_refs = refs[:n_arr], refs[n_arr:2 * n_arr]
        send_sems, recv_sems = refs[2 * n_arr], refs[2 * n_arr + 1]
        pos = _mesh_pos()
        for a in range(n_arr):
            for d in range(1, N_DEV):
                dev, peer = _peer_of(d, pos)
                sem = a * (N_DEV - 1) + d - 1
                copy = pltpu.make_async_remote_copy(
                    src_ref=in_refs[a].at[peer] if scatter else in_refs[a], dst_ref=land_refs[a].at[peer],
                    send_sem=send_sems.at[sem], recv_sem=recv_sems.at[sem], device_id=dev,
                    device_id_type=pl.DeviceIdType.MESH)
                copy.wait_send()
                copy.wait_recv()

    thru = list(handle["srcs"]) + list(handle["lands"])
    res = pl.pallas_call(
        body, name=name,
        out_shape=tuple(pltpu.HBM(a.shape, a.dtype) for a in thru),
        in_specs=[_HBM] * (2 * n_arr) + [_SEM, _SEM, pl.BlockSpec(memory_space=pl.ANY)],
        out_specs=tuple([_HBM] * (2 * n_arr)),
        input_output_aliases={i: i for i in range(2 * n_arr)},
        compiler_params=pltpu.CompilerParams(has_side_effects=pltpu.SideEffectType.DATAFLOW_SIDE_EFFECTING),
    )(*thru, *handle["sems"], after)
    out = []
    for src, land in zip(res[:n_arr], res[n_arr:]):
        own = lax.dynamic_index_in_dim(src, me, axis=0, keepdims=True) if scatter else src[None]
        out.append(lax.dynamic_update_index_in_dim(land, own, me, axis=0))
    return out


def _cols(srcs, rows, plans, out_dtype, *, name):
    n_src = len(srcs)
    rb = _pick(rows, (256, 128, 64, 32, 16, 8))

    def width(pieces):
        return sum(p[1] if p[0] == "z" else p[3] - p[2] for p in pieces)

    def body(*refs):
        ins, outs = refs[:n_src], refs[n_src:]
        loaded = {}
        for o_ref, plan in zip(outs, plans):
            for j, pieces in enumerate(plan):
                vals = []
                for pc in pieces:
                    if pc[0] == "z":
                        vals.append(jnp.zeros((rb, pc[1]), out_dtype))
                    else:
                        si, sj, c0, c1 = pc
                        if (si, sj) not in loaded:
                            loaded[(si, sj)] = ins[si][sj]
                        vals.append(loaded[(si, sj)][:, c0:c1].astype(out_dtype))
                o_ref[j] = vals[0] if len(vals) == 1 else jnp.concatenate(vals, axis=-1)

    for arr, r0 in srcs:
        assert r0 % rb == 0
    return pl.pallas_call(
        body, grid=(rows // rb,), name=name,
        in_specs=[pl.BlockSpec((arr.shape[0], rb, arr.shape[2]), lambda i, r0=r0 // rb: (0, r0 + i, 0)) for arr, r0 in srcs],
        out_specs=[pl.BlockSpec((len(p), rb, width(p[0])), lambda i: (0, i, 0)) for p in plans],
        out_shape=[jax.ShapeDtypeStruct((len(p), rows, width(p[0])), out_dtype) for p in plans],
        compiler_params=_cparams(("parallel",)),
    )(*[arr for arr, _ in srcs])


def _shard_pieces(src, a, b, shard_w):
    out = []
    while a < b:
        s = a // shard_w
        e = min(b, (s + 1) * shard_w)
        out.append((src, s, a - s * shard_w, e - s * shard_w))
        a = e
    return out


def _mapped_pieces(a, b, segs):
    out = []
    for n0, n1, k0 in sorted(segs):
        lo, hi = max(a, n0), min(b, n1)
        if lo < hi:
            out.append((0, 0, k0 + lo - n0, k0 + hi - n0))
    return out


_EV_SEGS = [(0, 1536, EV_QKV), (1536, 2048, EV_Z), (2048, 2056, EV_MISC + MLA_ROPE), (2056, 2440, EV_CQ),
            (2440, 2696, EV_CKV), (2696, 2760, EV_MISC)]
EV_NAT_W, OD_NAT_W = 2760, 5152


PACK_W = 1024


def _adamw(w, gparts, m, v, *, name):
    n_rows, n_cols = w.shape
    n_parts = gparts.shape[0]
    tm = _pick(n_rows, (512, 256, 128, 64, 32, 16, 8))
    while n_parts * tm * n_cols * 4 > 4 * 1024 * 1024 and tm % 16 == 0:
        tm //= 2

    def body(w_ref, g_ref, m_ref, v_ref, go_ref, d_ref, mo_ref, vo_ref):
        g = g_ref[0]
        for j in range(1, n_parts):
            g = g + g_ref[j]
        m_new = ADAM_B1 * m_ref[...] + (1.0 - ADAM_B1) * g
        v_new = ADAM_B2 * v_ref[...] + (1.0 - ADAM_B2) * jnp.square(g)
        m_hat = m_new / (1.0 - ADAM_B1 ** ADAM_STEP)
        v_hat = v_new / (1.0 - ADAM_B2 ** ADAM_STEP)
        go_ref[...] = g
        d_ref[...] = -ADAM_LR * (m_hat / (jnp.sqrt(v_hat) + ADAM_EPS) + ADAM_WD * w_ref[...])
        mo_ref[...] = m_new
        vo_ref[...] = v_new

    row = pl.BlockSpec((tm, n_cols), lambda i: (i, 0))
    return pl.pallas_call(
        body, grid=(n_rows // tm,), name=name,
        in_specs=[row, pl.BlockSpec((n_parts, tm, n_cols), lambda i: (0, i, 0)), row, row],
        out_specs=[row] * 4,
        out_shape=[jax.ShapeDtypeStruct((n_rows, n_cols), F32)] * 4,
        compiler_params=_cparams(("parallel",)),
    )(w, gparts, m, v)


def _adamw_nd(w, gparts, m, v, *, name):
    shape = w.shape
    two = (-1, shape[-1])
    outs = _adamw(w.reshape(two), gparts.reshape((gparts.shape[0],) + (int(np.prod(shape[:-1])), shape[-1])),
                  m.reshape(two), v.reshape(two), name=name)
    return [o.reshape(shape) for o in outs]


def _pack(parts):
    flat = [p.astype(F32).reshape(-1) for p in parts]
    n_pad = -sum(f.shape[0] for f in flat) % (8 * PACK_W)
    return jnp.concatenate(flat + [jnp.zeros((n_pad,), F32)]).reshape(-1, PACK_W)


def _unpack(packed, shapes):
    flat = packed.reshape(-1)
    out, off = [], 0
    for s in shapes:
        n = int(np.prod(s))
        out.append(flat[off:off + n].reshape(tuple(s)))
        off += n
    return out


def _mod_shard(c_all, ada_w, ada_b_shard, *, name):
    n_layer, d, n_col = ada_w.shape

    def body(c_ref, w_ref, b_ref, o_ref):
        cv = c_ref[...]
        o_ref[0] = _dot(cv * jax.nn.sigmoid(cv), w_ref[0], "nn") + b_ref[0]

    return pl.pallas_call(
        body, grid=(n_layer,), name=name,
        in_specs=[pl.BlockSpec((N_DEV, d), lambda l: (0, 0)), pl.BlockSpec((1, d, n_col), lambda l: (l, 0, 0)),
                  pl.BlockSpec((1, 1, n_col), lambda l: (l, 0, 0))],
        out_specs=pl.BlockSpec((1, N_DEV, n_col), lambda l: (l, 0, 0)),
        out_shape=jax.ShapeDtypeStruct((n_layer, N_DEV, n_col), F32),
        compiler_params=_cparams(("parallel",)),
    )(c_all, ada_w, ada_b_shard)


def _ada_w_grad(c_all, dmod_shard, *, name):
    n_layer, _, n_col = dmod_shard.shape
    d = c_all.shape[1]

    def body(c_ref, g_ref, o_ref):
        cv = c_ref[...]
        o_ref[0] = _dot(cv * jax.nn.sigmoid(cv), g_ref[0], "tn", True)

    return pl.pallas_call(
        body, grid=(n_layer,), name=name,
        in_specs=[pl.BlockSpec((N_DEV, d), lambda l: (0, 0)), pl.BlockSpec((1, N_DEV, n_col), lambda l: (l, 0, 0))],
        out_specs=pl.BlockSpec((1, d, n_col), lambda l: (l, 0, 0)),
        out_shape=jax.ShapeDtypeStruct((n_layer, d, n_col), F32),
        compiler_params=_cparams(("parallel",)),
    )(c_all, dmod_shard)


def _uq(w):
    r = w.shape[0]
    rope = jnp.pad(w[:, :, MLA_NOPE:], ((0, 0), (0, 0), (0, LANE - MLA_ROPE)))
    return jnp.concatenate([w[:, :, :MLA_NOPE].reshape(r, -1), rope.reshape(r, -1)], axis=1)


def _uq_back(d):
    r = d.shape[0]
    half = MLA_HEADS * LANE
    return jnp.concatenate([d[:, :half].reshape(r, MLA_HEADS, LANE),
                            d[:, half:].reshape(r, MLA_HEADS, LANE)[:, :, :MLA_ROPE]], axis=-1)


def _ukv(w):
    r = w.shape[0]
    return jnp.concatenate([w[:, :, :MLA_NOPE].reshape(r, -1), w[:, :, MLA_NOPE:].reshape(r, -1)], axis=1)


def _ukv_back(d):
    r = d.shape[0]
    half = MLA_HEADS * LANE
    return jnp.concatenate([d[:, :half].reshape(r, MLA_HEADS, LANE), d[:, half:].reshape(r, MLA_HEADS, LANE)], axis=-1)


def _lane_vec(v):
    return jnp.pad(v.astype(F32), (0, LANE - v.shape[0])).reshape(1, LANE)


def _row(v):
    return v.astype(F32).reshape(1, -1)


def _adaln(x, ln):
    return _rows(_adaln_fn, [x], list(ln), [(D_MODEL, BF16)], name="adaln", also_t=(0,))


def _adaln_bwd(x, ln, dh, dxn):
    (dx,), dln = _rows_vjp(_adaln_fn, [x], [], list(ln), [], [dh], [F32], adds={0: dxn}, name="adaln_bwd", tm=ROW_TILE)
    return dx, dln


def _resid(coef, y, x, gate):
    return _rows(functools.partial(_resid_fn, coef), [y, x], [gate], [(D_MODEL, F32)], name="resid")[0]


def _resid_adaln_fn(coef, y, x, gate, g, shift, scale):
    xn = x + coef * gate * y
    return (xn,) + _adaln_fn(xn, g, shift, scale)


def _norm_in(x, pending, ln):
    if pending is None:
        return (x,) + tuple(_adaln(x, ln))
    coef, y, gate = pending
    return _rows(functools.partial(_resid_adaln_fn, coef), [y, x], [gate] + list(ln), [(D_MODEL, F32), (D_MODEL, BF16)],
                 name="resid_adaln", also_t=(1,))


def _gated_fn(coef, y, gate):
    return (coef * gate * y,)


def _resid_bwd(coef, y, gate, dxn):
    (dy,), (dgate,) = _rows_vjp(functools.partial(_gated_fn, coef), [y], [], [gate], [], [dxn], [BF16], name="resid_bwd",
                                tm=ROW_TILE)
    return dy, dgate


def _ffn_fwd(x, h, ht, w13, w2):
    s, st, ab = _ffn_act(h, w13, name="ffn_act")
    y = _matmul(s, w2, "nn", F32, name="ffn_down")
    return y, (x, ab, ht, st, y)


def _ffn_bwd(saved, dxn, ln, gate, w13, w2):
    x, ab, ht, st, y = saved
    dy, dgate = _resid_bwd(0.5, y, gate, dxn)
    dab, dh = _ffn_act_dh_bwd(ab, dy, w13, w2, name="ffn_act_dh_bwd")
    dw13 = _matmul(ht, dab, "kn", F32, name="ffn_dw13")
    dw2 = _matmul(st, dy, "kn", F32, name="ffn_dw2")
    dx, dln = _adaln_bwd(x, ln, dh, dxn)
    return dx, dw13, dw2, dln, dgate


def _rope_consts():
    half = MLA_ROPE // 2
    inv = (ROPE_THETA ** (-jnp.arange(half, dtype=F32) / half)).astype(F32)
    zeros = jnp.zeros((LANE - MLA_ROPE,), F32)
    invf = jnp.concatenate([inv, inv, zeros]).reshape(1, LANE)
    sgn = jnp.concatenate([-jnp.ones((half,), F32), jnp.ones((half,), F32), zeros]).reshape(1, LANE)
    return invf, sgn


def _even_fwd(x, h, ht, pos, wt):
    p = _matmul(h, wt["w_in"], "nn", F32, name="ev_in")
    act, pre = _conv_fwd(p, EV_QKV, 1536, wt["conv_w"], jnp.zeros((1, 1536), F32), name="ev_conv")
    o_a, states = _gdn_fwd(act, p, wt["alog"], wt["dtb"], name="gdn_fwd")
    cqn, ckvn = _rows(_rms2_fn, [(p, EV_CQ, 384), (p, EV_CKV, 256)], [wt["gq"], wt["gkv"]],
                      [(384, BF16), (256, BF16)], name="mla_rms")
    q = _matmul(cqn, wt["w_uq"], "nn", F32, name="mla_uq")
    kv = _matmul(ckvn, wt["w_ukv"], "nn", F32, name="mla_ukv")
    invf, sgn = _rope_consts()
    qc, kc, vv = _rows(_rope_fn, [q, kv, (p, EV_MISC, LANE), pos], [invf, sgn],
                       [(1024, BF16), (1024, BF16), (512, BF16)], name="mla_rope", tm=ROW_TILE // 2)
    o_b, lse = _attn_fwd(qc, kc, vv, name="attn_fwd")
    o, ot = _rows(_ev_out_fn, [o_a, (p, EV_Z, 512), o_b], [wt["gdn_g"]], [(1024, BF16)], name="ev_out", also_t=(0,))
    y = _matmul(o, wt["w_out"], "nn", F32, name="ev_wout")
    return y, (x, ht, p, act, pre, states, cqn, ckvn, q, kv, qc, kc, vv, o_a, o_b, lse, ot, y)


def _ev_dp_fn(dx0, dx1, dx2, dcq, dm_r, dm_g, dz, dckv):
    return (jnp.concatenate([dx0, dx1, dx2, dcq, dm_r + dm_g, dz, dckv], axis=-1),)


def _even_bwd(saved, dxn, pos, ln, gate, wt):
    x, ht, p, act, pre, states, cqn, ckvn, q, kv, qc, kc, vv, o_a, o_b, lse, ot, y = saved
    g = {}
    dy, g["gate"] = _resid_bwd(1.0, y, gate, dxn)
    do = _matmul(dy, wt["w_out"], "nt", F32, name="ev_dwout_x")
    g["w_out"] = _matmul(ot, dy, "kn", F32, name="ev_dwout_w")
    (d_oa, dz, d_ob), (g["gdn_g"],) = _rows_vjp(_ev_out_fn, [o_a, (p, EV_Z, 512), o_b], [], [wt["gdn_g"]], [], [do],
                                                [F32, F32, F32], name="ev_out_bwd")
    dqc, dkc, dvv = _attn_bwd(qc, kc, vv, o_b, lse, d_ob, name="attn_bwd")
    invf, sgn = _rope_consts()
    (dq, dkv, dm_r), _ = _rows_vjp(_rope_fn, [q, kv, (p, EV_MISC, LANE)], [pos], [], [invf, sgn], [dqc, dkc, dvv],
                                   [BF16, BF16, F32], name="mla_rope_bwd", tm=ROW_TILE // 2)
    dcqn = _matmul(dq, wt["w_uq"], "nt", F32, name="mla_duq_x")
    g["w_uq"] = _matmul(cqn, dq, "tn", F32, name="mla_duq_w")
    dckvn = _matmul(dkv, wt["w_ukv"], "nt", F32, name="mla_dukv_x")
    g["w_ukv"] = _matmul(ckvn, dkv, "tn", F32, name="mla_dukv_w")
    (dcq, dckv), (g["gq"], g["gkv"]) = _rows_vjp(_rms2_fn, [(p, EV_CQ, 384), (p, EV_CKV, 256)], [],
                                                 [wt["gq"], wt["gkv"]], [], [dcqn, dckvn], [F32, F32], name="mla_rms_bwd")
    dq_g, dk_g, dv_g, dm_g, g["alog"], g["dtb"] = _gdn_bwd(act, p, wt["alog"], wt["dtb"], states, d_oa, name="gdn_bwd")
    dxs, dws = [], []
    for j, d in enumerate((dq_g, dk_g, dv_g)):
        dxj, dwj, _ = _conv_bwd(d, pre, 512 * j, p, EV_QKV + 512 * j, wt["conv_w"], name="ev_conv_bwd")
        dxs.append(dxj)
        dws.append(dwj)
    g["conv_w"] = jnp.concatenate(dws, axis=1)
    (dp,) = _rows(_ev_dp_fn, dxs + [dcq, dm_r, dm_g, dz, dckv], [],
                  [(EV_W, BF16)], name="ev_dp", tm=ROW_TILE // 2)
    dh = _matmul(dp, wt["w_in"], "nt", F32, name="ev_din_x")
    g["w_in"] = _matmul(ht, dp, "kn", F32, name="ev_din_w")
    dx, g["ln"] = _adaln_bwd(x, ln, dh, dxn)
    return dx, g


def _odd_fwd(x, h, ht, wt):
    p = _matmul(h, wt["w_in"], "nn", F32, name="od_in")
    act, pre = _conv_fwd(p, OD_XBC, 3072, wt["conv_w"], wt["conv_b"], name="od_conv")
    ys, states = _ssd_fwd(act, p, wt["alog"], wt["dtb"], wt["dsk"], name="ssd_fwd")
    o, ot = _rows(_od_out_fn, [ys, (p, OD_Z, 2048)], [wt["norm_g"]], [(SSD_D_INNER, BF16)], name="od_out",
                  tm=ROW_TILE // 2, also_t=(0,))
    y = _matmul(o, wt["w_out"], "nn", F32, name="od_wout")
    return y, (x, ht, p, act, pre, states, ys, ot, y)


def _od_dp_fn(dz, dxx, dxb, dxc, ddt):
    return (jnp.concatenate([dz, dxx, dxb, dxc, ddt, jnp.zeros_like(ddt)], axis=-1),)


def _odd_bwd(saved, dxn, ln, gate, wt):
    x, ht, p, act, pre, states, ys, ot, y = saved
    g = {}
    dy, g["gate"] = _resid_bwd(1.0, y, gate, dxn)
    do = _matmul(dy, wt["w_out"], "nt", F32, name="od_dwout_x")
    g["w_out"] = _matmul(ot, dy, "kn", F32, name="od_dwout_w")
    (dys, dz), (g["norm_g"],) = _rows_vjp(_od_out_fn, [ys, (p, OD_Z, 2048)], [], [wt["norm_g"]], [], [do], [F32, F32],
                                          name="od_out_bwd", tm=ROW_TILE // 2)
    dxs, dbm, dcm, ddt, g["alog"], g["dtb"], g["dsk"] = _ssd_bwd(act, p, wt["alog"], wt["dtb"], wt["dsk"], states, dys,
                                                                 name="ssd_bwd")
    dins, dws, dbs = [], [], []
    for d, c0 in ((dxs, 0), (dbm, 2048), (dcm, 2560)):
        dxj, dwj, dbj = _conv_bwd(d, pre, c0, p, OD_XBC + c0, wt["conv_w"], name="od_conv_bwd")
        dins.append(dxj)
        dws.append(dwj)
        dbs.append(dbj)
    g["conv_w"] = jnp.concatenate(dws, axis=1)
    g["conv_b"] = jnp.concatenate(dbs, axis=1)
    (dp,) = _rows(_od_dp_fn, [dz] + dins + [ddt], [], [(OD_W, BF16)], name="od_dp", tm=ROW_TILE // 2)
    dh = _matmul(dp, wt["w_in"], "nt", F32, name="od_din_x")
    g["w_in"] = _matmul(ht, dp, "kn", F32, name="od_din_w")
    dx, g["ln"] = _adaln_bwd(x, ln, dh, dxn)
    return dx, g


def _local_step(x, tgt, pos, mod, final_g, layer_weights, layer_done):
    mod = mod.reshape(DEPTH, 3, 3, 1, D_MODEL)
    wts = []

    def ln_of(l, i):
        return (_row(wts[l]["norm_g"][i]), mod[l, i, 0], mod[l, i, 1])

    def mixer_w(l):
        p = wts[l]
        if l % 2 == 0:
            return dict(w_in=p["w_in_k"], conv_w=p["gdn_conv_w"].astype(F32),
                        alog=_lane_vec(p["gdn_A_log"]), dtb=_lane_vec(p["gdn_dt_bias"]),
                        gdn_g=_row(p["gdn_norm_g"]), gq=_row(p["mla_q_norm_g"]), gkv=_row(p["mla_kv_norm_g"]),
                        w_uq=_uq(p["mla_w_uq"]), w_ukv=_ukv(p["mla_w_ukv"]), w_out=p["ev_w_out"])
        return dict(w_in=p["w_in_k"], conv_w=p["ssd_conv_w"].astype(F32),
                    conv_b=_row(p["ssd_conv_b"]), alog=_lane_vec(p["ssd_A_log"]),
                    dtb=_lane_vec(p["ssd_dt_bias"]), dsk=_lane_vec(p["ssd_D"]),
                    norm_g=_row(p["ssd_norm_g"]), w_out=p["ssd_w_out"])

    saved = []
    pending = None
    for l in range(DEPTH):
        wts.append(layer_weights(l, x if pending is None else pending[1]))
        x, h, ht = _norm_in(x, pending, ln_of(l, 0))
        y, s0 = _ffn_fwd(x, h, ht, wts[l]["w13"][0], wts[l]["w2"][0])
        x, h, ht = _norm_in(x, (0.5, y, mod[l, 0, 2]), ln_of(l, 1))
        if l % 2 == 0:
            y, s1 = _even_fwd(x, h, ht, pos, mixer_w(l))
        else:
            y, s1 = _odd_fwd(x, h, ht, mixer_w(l))
        x, h, ht = _norm_in(x, (1.0, y, mod[l, 1, 2]), ln_of(l, 2))
        y, s2 = _ffn_fwd(x, h, ht, wts[l]["w13"][1], wts[l]["w2"][1])
        pending = (0.5, y, mod[l, 2, 2])
        saved.append((s0, s1, s2))
    x = _resid(*pending[:2], x, pending[2])

    loss, dx, d_final_g = _loss_bwd(x, tgt, _row(final_g), name="loss")

    repl = {k: [None] * (DEPTH // 2) for k in ("gdn_A_log", "gdn_dt_bias", "gdn_norm_g", "mla_q_norm_g", "mla_kv_norm_g",
                                                "ssd_A_log", "ssd_dt_bias", "ssd_D")}
    dmod = [None] * DEPTH
    token = None
    for l in reversed(range(DEPTH)):
        s0, s1, s2 = saved[l]
        e = l // 2
        gl = {"w13": [None] * 2, "w2": [None] * 2}
        dg, dsh, dsc, dgt = [None] * 3, [None] * 3, [None] * 3, [None] * 3
        gate2 = mod[l, 2, 2] if token is None else mod[l, 2, 2] + token
        dx, gl["w13"][1], gl["w2"][1], (dg[2], dsh[2], dsc[2]), dgt[2] = _ffn_bwd(
            s2, dx, ln_of(l, 2), gate2, wts[l]["w13"][1], wts[l]["w2"][1])
        if l % 2 == 0:
            dx, g = _even_bwd(s1, dx, pos, ln_of(l, 1), mod[l, 1, 2], mixer_w(l))
            gl.update(w_in_k=g["w_in"], gdn_conv_w=g["conv_w"], mla_w_uq=_uq_back(g["w_uq"]),
                      mla_w_ukv=_ukv_back(g["w_ukv"]), ev_w_out=g["w_out"])
            repl["gdn_A_log"][e] = g["alog"][0, :GDN_HEADS]
            repl["gdn_dt_bias"][e] = g["dtb"][0, :GDN_HEADS]
            repl["gdn_norm_g"][e] = g["gdn_g"][0]
            repl["mla_q_norm_g"][e] = g["gq"][0]
            repl["mla_kv_norm_g"][e] = g["gkv"][0]
        else:
            dx, g = _odd_bwd(s1, dx, ln_of(l, 1), mod[l, 1, 2], mixer_w(l))
            gl.update(w_in_k=g["w_in"], ssd_conv_w=g["conv_w"], ssd_conv_b=g["conv_b"][0], ssd_norm_g=g["norm_g"][0],
                      ssd_w_out=g["w_out"])
            repl["ssd_A_log"][e] = g["alog"][0, :SSD_HEADS]
            repl["ssd_dt_bias"][e] = g["dtb"][0, :SSD_HEADS]
            repl["ssd_D"][e] = g["dsk"][0, :SSD_HEADS]
        dg[1], dsh[1], dsc[1] = g["ln"]
        dgt[1] = g["gate"]
        gl.update(w13=gl["w13"][1], w2=gl["w2"][1])
        token = layer_done(l, 0, gl, dx)
        gate0 = mod[l, 0, 2] if token is None else mod[l, 0, 2] + token
        dx, dw13, dw2, (dg[0], dsh[0], dsc[0]), dgt[0] = _ffn_bwd(s0, dx, ln_of(l, 0), gate0, wts[l]["w13"][0], wts[l]["w2"][0])
        dmod[l] = jnp.concatenate([jnp.concatenate([dsh[i], dsc[i], dgt[i]], axis=1) for i in range(3)], axis=1)[0]
        token = layer_done(l, 1, dict(w13=dw13, w2=dw2, norm_g=jnp.concatenate(dg, axis=0)), dx)

    grads = {k: jnp.stack(v) for k, v in repl.items()}
    grads["final_g"] = d_final_g[0]
    return loss, dx, grads, jnp.stack(dmod)


_WEIGHTS = ("ada_w", "ada_b", "norm_g", "ffn_w1", "ffn_w3", "ffn_w2", "ev_w_in", "gdn_conv_w", "gdn_A_log", "gdn_dt_bias",
            "gdn_norm_g", "mla_q_norm_g", "mla_w_uq", "mla_kv_norm_g", "mla_w_ukv", "ev_w_out", "ssd_w_in", "ssd_conv_w",
            "ssd_conv_b", "ssd_A_log", "ssd_dt_bias", "ssd_D", "ssd_norm_g", "ssd_w_out", "final_g")
_BIG = {"ffn_w1": 3, "ffn_w3": 3, "ffn_w2": 2, "ev_w_in": 2, "mla_w_uq": 1, "mla_w_ukv": 1, "ev_w_out": 1, "ssd_w_in": 2,
        "ssd_w_out": 1}
_SMALL = {"norm_g": 2, "gdn_conv_w": 2, "ssd_conv_w": 2, "ssd_conv_b": 1, "ssd_norm_g": 1}
_REPL = ("ada_b", "gdn_A_log", "gdn_dt_bias", "gdn_norm_g", "mla_q_norm_g", "mla_kv_norm_g", "ssd_A_log", "ssd_dt_bias",
         "ssd_D", "final_g")


def _join(pieces, axis):
    moved = jnp.moveaxis(pieces, 0, axis)
    shape = moved.shape
    return moved.reshape(shape[:axis] + (shape[axis] * shape[axis + 1],) + shape[axis + 2:])


def _split(full, axis):
    shape = full.shape
    return jnp.moveaxis(full.reshape(shape[:axis] + (N_DEV, shape[axis] // N_DEV) + shape[axis + 1:]), axis, 0)


def kernel(x, c, positions, ada_w, ada_b, norm_g, ffn_w1, ffn_w3, ffn_w2, ev_w_in, gdn_conv_w, gdn_A_log, gdn_dt_bias, gdn_norm_g, mla_q_norm_g, mla_w_uq, mla_kv_norm_g, mla_w_ukv, ev_w_out, ssd_w_in, ssd_conv_w, ssd_conv_b, ssd_A_log, ssd_dt_bias, ssd_D, ssd_norm_g, ssd_w_out, final_g, loss_target, m_ada_w, m_ada_b, m_norm_g, m_ffn_w1, m_ffn_w3, m_ffn_w2, m_ev_w_in, m_gdn_conv_w, m_gdn_A_log, m_gdn_dt_bias, m_gdn_norm_g, m_mla_q_norm_g, m_mla_w_uq, m_mla_kv_norm_g, m_mla_w_ukv, m_ev_w_out, m_ssd_w_in, m_ssd_conv_w, m_ssd_conv_b, m_ssd_A_log, m_ssd_dt_bias, m_ssd_D, m_ssd_norm_g, m_ssd_w_out, m_final_g, v_ada_w, v_ada_b, v_norm_g, v_ffn_w1, v_ffn_w3, v_ffn_w2, v_ev_w_in, v_gdn_conv_w, v_gdn_A_log, v_gdn_dt_bias, v_gdn_norm_g, v_mla_q_norm_g, v_mla_w_uq, v_mla_kv_norm_g, v_mla_w_ukv, v_ev_w_out, v_ssd_w_in, v_ssd_conv_w, v_ssd_conv_b, v_ssd_A_log, v_ssd_dt_bias, v_ssd_D, v_ssd_norm_g, v_ssd_w_out, v_final_g):
    a = dict(locals())
    w = {n: a[n] for n in _WEIGHTS}
    m = {n: a["m_" + n] for n in _WEIGHTS}
    v = {n: a["v_" + n] for n in _WEIGHTS}
    mx, my, mc = _mesh_pos()
    me = 4 * mx + 2 * my + mc
    t_len = x.shape[1]
    shards = range(N_DEV)

    small_names, big_names = list(_SMALL), list(_BIG)
    axis_of = {**_SMALL, **_BIG}
    small_g = _exchange([c] + [w[n] for n in small_names], False, name="gather_small")
    c_all = small_g[0].reshape(N_DEV, D_MODEL)
    fw = {n: _join(p, _SMALL[n]) for n, p in zip(small_names, small_g[1:])}
    first_names = [n for n in big_names if not n.startswith("ssd")]
    first_g = dict(zip(first_names, _exchange([w[n][:1].astype(BF16) for n in first_names], False, name="gather_first")))
    rest_src = {n: (w[n] if n.startswith("ssd") else w[n][1:]).astype(BF16) for n in big_names}
    rest_handle, rest_token = _exchange_start([rest_src[n] for n in big_names], False, name="gather_rest_start")
    rest_g = {}

    fs, es, os_ = ffn_w1.shape[3], ev_w_in.shape[2], ssd_w_in.shape[2]
    half = range(N_DEV // 2)
    plan13 = [[[(0, s, 0, fs) for s in half] + [(1, s, 0, fs) for s in half]
               + [(0, s + 4, 0, fs) for s in half] + [(1, s + 4, 0, fs) for s in half]]]
    plan_ev, k_at = [], 0
    for n0, n1, k0 in sorted(_EV_SEGS, key=lambda seg: seg[2]):
        if k0 > k_at:
            plan_ev.append(("z", k0 - k_at))
        plan_ev += _shard_pieces(0, n0, n1, es)
        k_at = k0 + n1 - n0
    assert k_at == EV_W and es * N_DEV == EV_NAT_W and os_ * N_DEV == OD_NAT_W
    plan_od = [[_shard_pieces(0, 0, OD_NAT_W, os_) + [("z", OD_W - OD_NAT_W)]]]

    def layer_weights(l, x_in):
        if l == 1:
            rest_g.update(zip(big_names, _exchange_wait(rest_handle, x_in, me, name="gather_rest_wait")))
        e = l // 2
        src = first_g if l == 0 else rest_g
        i = 0 if l == 0 else l - 1
        ie = 0 if (l == 0 or l % 2) else e - 1
        g1 = src["ffn_w1"].reshape(N_DEV, -1, fs)
        g3 = src["ffn_w3"].reshape(N_DEV, -1, fs)
        p = {"norm_g": fw["norm_g"][l],
             "w13": [_cols([(g1, (2 * i + j) * D_MODEL), (g3, (2 * i + j) * D_MODEL)], D_MODEL, plan13, BF16,
                           name="join_w13")[0][0] for j in range(2)],
             "w2": [src["ffn_w2"][:, i, j].reshape(D_FF, D_MODEL) for j in range(2)]}
        if l % 2 == 0:
            p["w_in_k"] = _cols([(src["ev_w_in"].reshape(N_DEV, -1, es), ie * D_MODEL)], D_MODEL, [[plan_ev]], BF16,
                                name="join_ev_in")[0][0]
            for n in ("mla_w_uq", "mla_w_ukv", "ev_w_out"):
                p[n] = _join(src[n][:, ie], axis_of[n] - 1)
            p["gdn_conv_w"] = fw["gdn_conv_w"][e]
            for n in ("gdn_A_log", "gdn_dt_bias", "gdn_norm_g", "mla_q_norm_g", "mla_kv_norm_g"):
                p[n] = w[n][e]
        else:
            p["w_in_k"] = _cols([(rest_g["ssd_w_in"].reshape(N_DEV, -1, os_), e * D_MODEL)], D_MODEL, plan_od, BF16,
                                name="join_od_in")[0][0]
            p["ssd_w_out"] = _join(rest_g["ssd_w_out"][:, e], axis_of["ssd_w_out"] - 1)
            for n in ("ssd_conv_w", "ssd_conv_b", "ssd_norm_g"):
                p[n] = fw[n][e]
            for n in ("ssd_A_log", "ssd_dt_bias", "ssd_D"):
                p[n] = w[n][e]
        return p

    def w13_cols(s, third):
        k0 = (s % 4) * fs + (2 * FF_HALF if s >= 4 else 0) + (FF_HALF if third else 0)
        return [(0, 0, k0, k0 + fs)]

    sent = {}

    def layer_done(l, stage, gl, dx_l):
        d1, d3 = _cols([(gl["w13"][None], 0)], D_MODEL, [[w13_cols(s, False) for s in shards],
                                                         [w13_cols(s, True) for s in shards]], F32, name="split_w13")
        pieces = {"ffn_w1": d1[:, None, None], "ffn_w3": d3[:, None, None],
                  "ffn_w2": gl["w2"].reshape(N_DEV, -1, D_MODEL)[:, None, None]}
        if stage == 1:
            pieces["norm_g"] = _split(gl["norm_g"][None], 2)
        elif l % 2 == 0:
            pieces["ev_w_in"] = _cols([(gl["w_in_k"][None], 0)], D_MODEL,
                                      [[_mapped_pieces(s * es, (s + 1) * es, _EV_SEGS) for s in shards]], F32,
                                      name="split_ev_in")[0][:, None]
            for n in ("gdn_conv_w", "mla_w_uq", "mla_w_ukv", "ev_w_out"):
                pieces[n] = _split(gl[n][None], axis_of[n])
        else:
            pieces["ssd_w_in"] = _cols([(gl["w_in_k"][None], 0)], D_MODEL,
                                       [[[(0, 0, s * os_, (s + 1) * os_)] for s in shards]], F32,
                                       name="split_od_in")[0][:, None]
            for n in ("ssd_conv_w", "ssd_conv_b", "ssd_norm_g", "ssd_w_out"):
                pieces[n] = _split(gl[n][None], axis_of[n])
        names = list(pieces)
        if l == 0 and stage == 1:
            sent[l, stage] = dict(zip(names, _exchange([pieces[n] for n in names], True, name="scatter_last")))
            return None
        handle, token = _exchange_start([pieces[n] for n in names], True, name=f"scatter_start_{l}_{stage}")
        sent[l, stage] = (names, handle)
        return token

    n_col = ada_w.shape[2]
    ada_b_shard = lax.dynamic_slice(ada_b, (0, me * n_col), (DEPTH, n_col)).reshape(DEPTH, 1, n_col)
    mod_all = _exchange([_mod_shard(c_all, ada_w, ada_b_shard, name="mod")], False, name="gather_mod")[0]
    mod_me = lax.dynamic_index_in_dim(mod_all, me, axis=2, keepdims=False)
    mod = jnp.transpose(mod_me, (1, 0, 2)).reshape(DEPTH, N_DEV * n_col) + rest_token

    pos = positions.astype(F32).reshape(t_len, 1)
    loss, dx, grads, dmod = _local_step(x[0], loss_target[0], pos, mod, final_g, layer_weights, layer_done)
    for (l, stage), started in list(sent.items()):
        if not isinstance(started, dict):
            names, handle = started
            sent[l, stage] = dict(zip(names, _exchange_wait(handle, dx, me, name=f"scatter_wait_{l}_{stage}")))

    repl_shapes = [w[n].shape for n in _REPL] + [(1,)]
    parts8 = _exchange([_pack([dmod] + [grads[n] for n in _REPL[1:]] + [loss[0, :1]])], False, name="gather_repl")[0]
    zero = jnp.zeros((1,), F32)
    r_grad, r_delta, r_m, r_v = [
        _unpack(o, repl_shapes) for o in _adamw(_pack([w[n] for n in _REPL] + [zero]), parts8,
                                                _pack([m[n] for n in _REPL] + [zero]),
                                                _pack([v[n] for n in _REPL] + [zero]), name="adamw_repl")]
    out = {"grad": {}, "delta": {}, "m": {}, "v": {}}
    for i, n in enumerate(_REPL):
        out["grad"][n], out["delta"][n], out["m"][n], out["v"][n] = r_grad[i], r_delta[i], r_m[i], r_v[i]
    loss_total = r_grad[-1].reshape(())

    dmod_all = parts8[:, :dmod.size // PACK_W].reshape((N_DEV,) + dmod.shape)
    dmod_cols = jnp.transpose(lax.dynamic_slice_in_dim(dmod_all, me * n_col, n_col, axis=2), (1, 0, 2))
    g_ada = _ada_w_grad(c_all, dmod_cols, name="ada_w_grad")
    for k, o in zip(("grad", "delta", "m", "v"), _adamw_nd(ada_w, g_ada[None], m["ada_w"], v["ada_w"], name="adamw_ada")):
        out[k]["ada_w"] = o

    for n in small_names + big_names:
        if n == "norm_g":
            g8 = jnp.concatenate([sent[l, 1][n] for l in range(DEPTH)], axis=1)
        elif n.startswith("ffn"):
            g8 = jnp.concatenate([jnp.concatenate([sent[l, 1][n], sent[l, 0][n]], axis=2) for l in range(DEPTH)], axis=1)
        else:
            layers = range(1, DEPTH, 2) if n.startswith("ssd") else range(0, DEPTH, 2)
            g8 = jnp.concatenate([sent[l, 0][n] for l in layers], axis=1)
        for k, o in zip(("grad", "delta", "m", "v"), _adamw_nd(w[n], g8, m[n], v[n], name="adamw_" + n)):
            out[k][n] = o

    return (loss_total, dx.reshape(x.shape), *[out["grad"][n] for n in _WEIGHTS], *[out["delta"][n] for n in _WEIGHTS],
            *[out["m"][n] for n in _WEIGHTS], *[out["v"][n] for n in _WEIGHTS])
```
